```python
import math
import jax, jax.numpy as jnp
from jax import lax
import numpy as np

D_MODEL = 1024
BATCH = 8
SEQ = 8192
DEPTH = 1

PLE_DIM = 256
MIX_WIDTH = D_MODEL
ATTN_WIDTH = MIX_WIDTH // 2
SGU_WIDTH = MIX_WIDTH - ATTN_WIDTH
HEAD_DIM = 64
N_ATTN_HEADS = ATTN_WIDTH // HEAD_DIM
N_SGU_GROUPS = 4
SGU_GROUP_DIM = SGU_WIDTH // N_SGU_GROUPS
SGU_CHUNK = 128
DILATION_PAIRS = ((128, 1), (512, 4), (2048, 16))
QBLK = 128
D_FF = ((8 * D_MODEL + 3 * 256 - 1) // (3 * 256)) * 256
PROJ_COLS = 3 * ATTN_WIDTH + 2 * SGU_WIDTH
EPS = 1e-6
NEG = -1e30

kernel_name = "hybrid_dilated_attn_gmlp_block"


def rmsnorm(x, g):
    xf = x.astype(jnp.float32)
    y = xf * lax.rsqrt(jnp.mean(xf * xf, axis=-1, keepdims=True) + EPS)
    return (y * g.astype(jnp.float32)).astype(x.dtype)


def layernorm(x, g, b):
    xf = x.astype(jnp.float32)
    mu = jnp.mean(xf, axis=-1, keepdims=True)
    xc = xf - mu
    y = xc * lax.rsqrt(jnp.mean(xc * xc, axis=-1, keepdims=True) + EPS)
    return (y * g.astype(jnp.float32) + b.astype(jnp.float32)).astype(x.dtype)


def dilated_branch(q, k, v, slopes, window, dilation):
    B, H, S, hd = q.shape
    span = dilation * QBLK
    s_pad = -(-S // span) * span
    M = s_pad // dilation
    nb = M // QBLK
    n_steps = window // dilation
    pad = ((0, 0), (0, 0), (0, s_pad - S), (0, 0))

    def to_blocks(t):
        t = jnp.pad(t, pad).reshape(B, H, M, dilation, hd).transpose(0, 1, 3, 2, 4)
        return t.reshape(B, H, dilation, nb, QBLK, hd)

    def with_prev(t):
        prev = jnp.pad(t, ((0, 0), (0, 0), (0, 0), (1, 0), (0, 0), (0, 0)))[:, :, :, :-1]
        return jnp.concatenate([prev, t], axis=-2)

    qb = to_blocks(q)
    kc = with_prev(to_blocks(k))
    vc = with_prev(to_blocks(v))
    s = jnp.einsum('bhrnqc,bhrnkc->bhrnqk', qb, kc)

    qi = jnp.arange(QBLK)[:, None]
    ki = jnp.arange(2 * QBLK)[None, :]
    steps = QBLK + qi - ki
    blk = jnp.arange(nb)[:, None, None]
    valid = (steps >= 0) & (steps <= n_steps) & (blk * QBLK - QBLK + ki >= 0)
    dist = (jnp.clip(steps, 0, None) * dilation).astype(jnp.float32)
    bias = -slopes[:, None, None] * dist[None]
    s = s + bias[None, :, None, None]
    s = jnp.where(valid[None, None, None], s, NEG)
    mx = jnp.max(s, axis=-1, keepdims=True)
    e = jnp.exp(s - mx)
    den = jnp.sum(e, axis=-1)
    o = jnp.einsum('bhrnqk,bhrnkc->bhrnqc', e, vc) / den[..., None]
    lse = mx[..., 0] + jnp.log(den)
    o = o.reshape(B, H, dilation, M, hd).transpose(0, 1, 3, 2, 4).reshape(B, H, s_pad, hd)[:, :, :S]
    lse = lse.reshape(B, H, dilation, M).transpose(0, 1, 3, 2).reshape(B, H, s_pad)[:, :, :S]
    return o, lse


def dilated_attention(q, k, v):
    B, S, _ = q.shape
    dtype = q.dtype

    def heads(t):
        return t.reshape(B, S, N_ATTN_HEADS, HEAD_DIM).transpose(0, 2, 1, 3).astype(jnp.float32)

    qh = heads(q) * (HEAD_DIM ** -0.5)
    kh, vh = heads(k), heads(v)
    slopes = 2.0 ** (-8.0 * (jnp.arange(N_ATTN_HEADS, dtype=jnp.float32) + 1.0) / N_ATTN_HEADS)
    outs, lses = [], []
    for window, dilation in DILATION_PAIRS:
        o, l = dilated_branch(qh, kh, vh, slopes, window, dilation)
        outs.append(o)
        lses.append(l)
    w = jax.nn.softmax(jnp.stack(lses, axis=0), axis=0)
    out = jnp.sum(w[..., None] * jnp.stack(outs, axis=0), axis=0)
    return out.transpose(0, 2, 1, 3).reshape(B, S, ATTN_WIDTH).astype(dtype)


def spatial_gating(u, z, ln_g, ln_b, w_s, b_s):
    B, S, _ = u.shape
    nc = S // SGU_CHUNK
    u = jax.nn.gelu(u).reshape(B, S, N_SGU_GROUPS, SGU_GROUP_DIM)
    z = jax.nn.gelu(z).reshape(B, S, N_SGU_GROUPS, SGU_GROUP_DIM)
    z = layernorm(z, ln_g, ln_b)
    zc = z.reshape(B, nc, SGU_CHUNK, N_SGU_GROUPS, SGU_GROUP_DIM)
    causal = jnp.tril(jnp.ones((SGU_CHUNK, SGU_CHUNK), dtype=w_s.dtype))
    wm = w_s * causal[None]
    mixed = jnp.einsum('gij,bnjgc->bnigc', wm, zc) + b_s.T[None, None, :, :, None]
    out = u * mixed.reshape(B, S, N_SGU_GROUPS, SGU_GROUP_DIM)
    return out.reshape(B, S, SGU_WIDTH)


def _fwd_setup_inputs(seed: int = 0) -> dict:
    key = jax.random.key(seed)
    ks = jax.random.split(key, 20)
    f32 = jnp.float32

    def nrm(k, shape, scale):
        return jax.random.normal(k, shape, f32) * scale

    def gain(k, shape):
        return 1.0 + 0.05 * jax.random.normal(k, shape, f32)

    L = DEPTH
    return {
        "x": jax.random.normal(ks[0], (BATCH, SEQ, D_MODEL), f32),
        "p": jax.random.normal(ks[1], (DEPTH, BATCH, SEQ, PLE_DIM), f32),
        "ln_pre_mix": gain(ks[2], (L, D_MODEL)),
        "w_in": nrm(ks[3], (L, D_MODEL, PROJ_COLS), D_MODEL ** -0.5),
        "sgu_ln_g": gain(ks[4], (L, SGU_GROUP_DIM)),
        "sgu_ln_b": nrm(ks[5], (L, SGU_GROUP_DIM), 0.02),
        "w_spatial": nrm(ks[6], (L, N_SGU_GROUPS, SGU_CHUNK, SGU_CHUNK), SGU_CHUNK ** -0.5),
        "b_spatial": gain(ks[7], (L, N_SGU_GROUPS, SGU_CHUNK)),
        "attn_out_norm": gain(ks[8], (L, ATTN_WIDTH)),
        "sgu_out_norm": gain(ks[9], (L, SGU_WIDTH)),
        "w_out": nrm(ks[10], (L, MIX_WIDTH, D_MODEL), MIX_WIDTH ** -0.5),
        "ln_post_mix": gain(ks[11], (L, D_MODEL)),
        "ln_pre_ffn": gain(ks[12], (L, D_MODEL)),
        "w_gate_up": nrm(ks[13], (L, D_MODEL, 2 * D_FF), D_MODEL ** -0.5),
        "w_down": nrm(ks[14], (L, D_FF, D_MODEL), D_FF ** -0.5),
        "ln_post_ffn": gain(ks[15], (L, D_MODEL)),
        "w_pe_gate": nrm(ks[16], (L, D_MODEL, D_MODEL), D_MODEL ** -0.5),
        "b_pe_gate": nrm(ks[17], (L, D_MODEL), 0.02),
        "w_pe_proj": nrm(ks[18], (L, PLE_DIM, D_MODEL), PLE_DIM ** -0.5),
    }


def _fwd_reference(x, p, ln_pre_mix, w_in, sgu_ln_g, sgu_ln_b, w_spatial, b_spatial,
              attn_out_norm, sgu_out_norm, w_out, ln_post_mix, ln_pre_ffn, w_gate_up,
              w_down, ln_post_ffn, w_pe_gate, b_pe_gate, w_pe_proj):
    h = x
    splits = [ATTN_WIDTH, 2 * ATTN_WIDTH, 3 * ATTN_WIDTH, 3 * ATTN_WIDTH + SGU_WIDTH]
    for i in range(DEPTH):
        a = rmsnorm(h, ln_pre_mix[i])
        proj = a @ w_in[i]
        q, k, v, u, z = jnp.split(proj, splits, axis=-1)
        attn = dilated_attention(q, k, v)
        sgu = spatial_gating(u, z, sgu_ln_g[i], sgu_ln_b[i], w_spatial[i], b_spatial[i])
        groups = jnp.concatenate([rmsnorm(attn, attn_out_norm[i]),
                                  rmsnorm(sgu, sgu_out_norm[i])], axis=-1)
        mixed = groups @ w_out[i]
        h = h + rmsnorm(mixed, ln_post_mix[i])
        f = rmsnorm(h, ln_pre_ffn[i])
        g, up = jnp.split(f @ w_gate_up[i], 2, axis=-1)
        y = (jax.nn.silu(g) * up) @ w_down[i]
        h = h + rmsnorm(y, ln_post_ffn[i])
        gate = jax.nn.sigmoid(h @ w_pe_gate[i] + b_pe_gate[i])
        h = h + gate * (p[i] @ w_pe_proj[i])
    return h


import jax as _jax
import jax.numpy as _jnp

TWIN_FORMAT = 'train_step'
FWD_PARAMS = ['x', 'p', 'ln_pre_mix', 'w_in', 'sgu_ln_g', 'sgu_ln_b', 'w_spatial', 'b_spatial', 'attn_out_norm', 'sgu_out_norm', 'w_out', 'ln_post_mix', 'ln_pre_ffn', 'w_gate_up', 'w_down', 'ln_post_ffn', 'w_pe_gate', 'b_pe_gate', 'w_pe_proj']
TWIN_WEIGHTS = ['ln_pre_mix', 'w_in', 'sgu_ln_g', 'sgu_ln_b', 'w_spatial', 'b_spatial', 'attn_out_norm', 'sgu_out_norm', 'w_out', 'ln_post_mix', 'ln_pre_ffn', 'w_gate_up', 'w_down', 'ln_post_ffn', 'w_pe_gate', 'b_pe_gate', 'w_pe_proj']
TWIN_DIFF_INPUT = 'x'
TWIN_INPUTS = ['x', 'p', 'ln_pre_mix', 'w_in', 'sgu_ln_g', 'sgu_ln_b', 'w_spatial', 'b_spatial', 'attn_out_norm', 'sgu_out_norm', 'w_out', 'ln_post_mix', 'ln_pre_ffn', 'w_gate_up', 'w_down', 'ln_post_ffn', 'w_pe_gate', 'b_pe_gate', 'w_pe_proj', 'loss_target', 'm_ln_pre_mix', 'm_w_in', 'm_sgu_ln_g', 'm_sgu_ln_b', 'm_w_spatial', 'm_b_spatial', 'm_attn_out_norm', 'm_sgu_out_norm', 'm_w_out', 'm_ln_post_mix', 'm_ln_pre_ffn', 'm_w_gate_up', 'm_w_down', 'm_ln_post_ffn', 'm_w_pe_gate', 'm_b_pe_gate', 'm_w_pe_proj', 'v_ln_pre_mix', 'v_w_in', 'v_sgu_ln_g', 'v_sgu_ln_b', 'v_w_spatial', 'v_b_spatial', 'v_attn_out_norm', 'v_sgu_out_norm', 'v_w_out', 'v_ln_post_mix', 'v_ln_pre_ffn', 'v_w_gate_up', 'v_w_down', 'v_ln_post_ffn', 'v_w_pe_gate', 'v_b_pe_gate', 'v_w_pe_proj']
TWIN_OUTPUTS = ['loss', 'grad_x', 'grad_ln_pre_mix', 'grad_w_in', 'grad_sgu_ln_g', 'grad_sgu_ln_b', 'grad_w_spatial', 'grad_b_spatial', 'grad_attn_out_norm', 'grad_sgu_out_norm', 'grad_w_out', 'grad_ln_post_mix', 'grad_ln_pre_ffn', 'grad_w_gate_up', 'grad_w_down', 'grad_ln_post_ffn', 'grad_w_pe_gate', 'grad_b_pe_gate', 'grad_w_pe_proj', 'delta_ln_pre_mix', 'delta_w_in', 'delta_sgu_ln_g', 'delta_sgu_ln_b', 'delta_w_spatial', 'delta_b_spatial', 'delta_attn_out_norm', 'delta_sgu_out_norm', 'delta_w_out', 'delta_ln_post_mix', 'delta_ln_pre_ffn', 'delta_w_gate_up', 'delta_w_down', 'delta_ln_post_ffn', 'delta_w_pe_gate', 'delta_b_pe_gate', 'delta_w_pe_proj', 'new_m_ln_pre_mix', 'new_m_w_in', 'new_m_sgu_ln_g', 'new_m_sgu_ln_b', 'new_m_w_spatial', 'new_m_b_spatial', 'new_m_attn_out_norm', 'new_m_sgu_out_norm', 'new_m_w_out', 'new_m_ln_post_mix', 'new_m_ln_pre_ffn', 'new_m_w_gate_up', 'new_m_w_down', 'new_m_ln_post_ffn', 'new_m_w_pe_gate', 'new_m_b_pe_gate', 'new_m_w_pe_proj', 'new_v_ln_pre_mix', 'new_v_w_in', 'new_v_sgu_ln_g', 'new_v_sgu_ln_b', 'new_v_w_spatial', 'new_v_b_spatial', 'new_v_attn_out_norm', 'new_v_sgu_out_norm', 'new_v_w_out', 'new_v_ln_post_mix', 'new_v_ln_pre_ffn', 'new_v_w_gate_up', 'new_v_w_down', 'new_v_ln_post_ffn', 'new_v_w_pe_gate', 'new_v_b_pe_gate', 'new_v_w_pe_proj']
TWIN_LEAF_KINDS = {'loss': 'loss', 'grad_x': 'grad_x', 'grad_ln_pre_mix': 'grad_w', 'grad_w_in': 'grad_w', 'grad_sgu_ln_g': 'grad_w', 'grad_sgu_ln_b': 'grad_w', 'grad_w_spatial': 'grad_w', 'grad_b_spatial': 'grad_w', 'grad_attn_out_norm': 'grad_w', 'grad_sgu_out_norm': 'grad_w', 'grad_w_out': 'grad_w', 'grad_ln_post_mix': 'grad_w', 'grad_ln_pre_ffn': 'grad_w', 'grad_w_gate_up': 'grad_w', 'grad_w_down': 'grad_w', 'grad_ln_post_ffn': 'grad_w', 'grad_w_pe_gate': 'grad_w', 'grad_b_pe_gate': 'grad_w', 'grad_w_pe_proj': 'grad_w', 'delta_ln_pre_mix': 'delta_w', 'delta_w_in': 'delta_w', 'delta_sgu_ln_g': 'delta_w', 'delta_sgu_ln_b': 'delta_w', 'delta_w_spatial': 'delta_w', 'delta_b_spatial': 'delta_w', 'delta_attn_out_norm': 'delta_w', 'delta_sgu_out_norm': 'delta_w', 'delta_w_out': 'delta_w', 'delta_ln_post_mix': 'delta_w', 'delta_ln_pre_ffn': 'delta_w', 'delta_w_gate_up': 'delta_w', 'delta_w_down': 'delta_w', 'delta_ln_post_ffn': 'delta_w', 'delta_w_pe_gate': 'delta_w', 'delta_b_pe_gate': 'delta_w', 'delta_w_pe_proj': 'delta_w', 'new_m_ln_pre_mix': 'new_m', 'new_m_w_in': 'new_m', 'new_m_sgu_ln_g': 'new_m', 'new_m_sgu_ln_b': 'new_m', 'new_m_w_spatial': 'new_m', 'new_m_b_spatial': 'new_m', 'new_m_attn_out_norm': 'new_m', 'new_m_sgu_out_norm': 'new_m', 'new_m_w_out': 'new_m', 'new_m_ln_post_mix': 'new_m', 'new_m_ln_pre_ffn': 'new_m', 'new_m_w_gate_up': 'new_m', 'new_m_w_down': 'new_m', 'new_m_ln_post_ffn': 'new_m', 'new_m_w_pe_gate': 'new_m', 'new_m_b_pe_gate': 'new_m', 'new_m_w_pe_proj': 'new_m', 'new_v_ln_pre_mix': 'new_v', 'new_v_w_in': 'new_v', 'new_v_sgu_ln_g': 'new_v', 'new_v_sgu_ln_b': 'new_v', 'new_v_w_spatial': 'new_v', 'new_v_b_spatial': 'new_v', 'new_v_attn_out_norm': 'new_v', 'new_v_sgu_out_norm': 'new_v', 'new_v_w_out': 'new_v', 'new_v_ln_post_mix': 'new_v', 'new_v_ln_pre_ffn': 'new_v', 'new_v_w_gate_up': 'new_v', 'new_v_w_down': 'new_v', 'new_v_ln_post_ffn': 'new_v', 'new_v_w_pe_gate': 'new_v', 'new_v_b_pe_gate': 'new_v', 'new_v_w_pe_proj': 'new_v'}


def _forward(args):
    return _fwd_reference(*[args[k] for k in FWD_PARAMS])


def _output_shape():
    def fwd():
        inp = _fwd_setup_inputs(0)
        return _fwd_reference(*[inp[k] for k in FWD_PARAMS])
    out = _jax.eval_shape(fwd)
    return out.shape, out.dtype

N_MICROBATCH = 1
ADAM_LR = 0.001
ADAM_B1 = 0.9
ADAM_B2 = 0.999
ADAM_EPS = 1e-08
ADAM_WD = 0.01
ADAM_STEP = 10
PER_EXAMPLE_BATCH_AXIS = {'x': 0, 'p': 1, 'loss_target': 0}
SHARED_INPUTS = []
_WEIGHT_DTYPES = {'ln_pre_mix': _jnp.float32, 'w_in': _jnp.float32, 'sgu_ln_g': _jnp.float32, 'sgu_ln_b': _jnp.float32, 'w_spatial': _jnp.float32, 'b_spatial': _jnp.float32, 'attn_out_norm': _jnp.float32, 'sgu_out_norm': _jnp.float32, 'w_out': _jnp.float32, 'ln_post_mix': _jnp.float32, 'ln_pre_ffn': _jnp.float32, 'w_gate_up': _jnp.float32, 'w_down': _jnp.float32, 'ln_post_ffn': _jnp.float32, 'w_pe_gate': _jnp.float32, 'b_pe_gate': _jnp.float32, 'w_pe_proj': _jnp.float32}
MOMENT_SCALE = {'ln_pre_mix': 1.245999e+00, 'w_in': 7.858585e-01, 'sgu_ln_g': 8.086595e-01, 'sgu_ln_b': 1.088814e+00, 'w_spatial': 3.783232e-01, 'b_spatial': 5.389993e-01, 'attn_out_norm': 2.082970e+00, 'sgu_out_norm': 2.879874e+00, 'w_out': 2.264020e+00, 'ln_post_mix': 6.524930e+01, 'ln_pre_ffn': 1.761120e+00, 'w_gate_up': 6.685112e-01, 'w_down': 1.393642e+00, 'ln_post_ffn': 6.501164e+01, 'w_pe_gate': 1.151299e+00, 'b_pe_gate': 5.291501e+00, 'w_pe_proj': 1.002159e+00}


def _to_microbatches(a, axis):
    t = _jnp.moveaxis(a, axis, 0)
    t = t.reshape((N_MICROBATCH, t.shape[0] // N_MICROBATCH) + t.shape[1:])
    return _jnp.moveaxis(t, 1, axis + 1)


def setup_inputs(seed: int = 0) -> dict:
    inp = _fwd_setup_inputs(seed)
    key = _jax.random.fold_in(_jax.random.key(seed), 7919)
    shape, _ = _output_shape()
    out = dict(inp)
    out["loss_target"] = _jax.random.normal(_jax.random.fold_in(key, 0), shape, _jnp.float32)
    for i, name in enumerate(TWIN_WEIGHTS):
        w = inp[name].astype(_jnp.float32)
        if MOMENT_SCALE is None:
            s = _jnp.sqrt(_jnp.mean(_jnp.square(w)) + 1e-30)
        else:
            s = MOMENT_SCALE[name]
        km, kv = _jax.random.split(_jax.random.fold_in(key, i + 1))
        out[name] = w
        out["m_" + name] = s * _jax.random.normal(km, w.shape, _jnp.float32)
        out["v_" + name] = (s * s) * _jax.random.uniform(kv, w.shape, _jnp.float32, 0.5, 1.5)
    if N_MICROBATCH > 1:
        for name, axis in PER_EXAMPLE_BATCH_AXIS.items():
            out[name] = _to_microbatches(out[name], axis)
    return {'x': out['x'], 'p': out['p'], 'ln_pre_mix': out['ln_pre_mix'], 'w_in': out['w_in'], 'sgu_ln_g': out['sgu_ln_g'], 'sgu_ln_b': out['sgu_ln_b'], 'w_spatial': out['w_spatial'], 'b_spatial': out['b_spatial'], 'attn_out_norm': out['attn_out_norm'], 'sgu_out_norm': out['sgu_out_norm'], 'w_out': out['w_out'], 'ln_post_mix': out['ln_post_mix'], 'ln_pre_ffn': out['ln_pre_ffn'], 'w_gate_up': out['w_gate_up'], 'w_down': out['w_down'], 'ln_post_ffn': out['ln_post_ffn'], 'w_pe_gate': out['w_pe_gate'], 'b_pe_gate': out['b_pe_gate'], 'w_pe_proj': out['w_pe_proj'], 'loss_target': out['loss_target'], 'm_ln_pre_mix': out['m_ln_pre_mix'], 'm_w_in': out['m_w_in'], 'm_sgu_ln_g': out['m_sgu_ln_g'], 'm_sgu_ln_b': out['m_sgu_ln_b'], 'm_w_spatial': out['m_w_spatial'], 'm_b_spatial': out['m_b_spatial'], 'm_attn_out_norm': out['m_attn_out_norm'], 'm_sgu_out_norm': out['m_sgu_out_norm'], 'm_w_out': out['m_w_out'], 'm_ln_post_mix': out['m_ln_post_mix'], 'm_ln_pre_ffn': out['m_ln_pre_ffn'], 'm_w_gate_up': out['m_w_gate_up'], 'm_w_down': out['m_w_down'], 'm_ln_post_ffn': out['m_ln_post_ffn'], 'm_w_pe_gate': out['m_w_pe_gate'], 'm_b_pe_gate': out['m_b_pe_gate'], 'm_w_pe_proj': out['m_w_pe_proj'], 'v_ln_pre_mix': out['v_ln_pre_mix'], 'v_w_in': out['v_w_in'], 'v_sgu_ln_g': out['v_sgu_ln_g'], 'v_sgu_ln_b': out['v_sgu_ln_b'], 'v_w_spatial': out['v_w_spatial'], 'v_b_spatial': out['v_b_spatial'], 'v_attn_out_norm': out['v_attn_out_norm'], 'v_sgu_out_norm': out['v_sgu_out_norm'], 'v_w_out': out['v_w_out'], 'v_ln_post_mix': out['v_ln_post_mix'], 'v_ln_pre_ffn': out['v_ln_pre_ffn'], 'v_w_gate_up': out['v_w_gate_up'], 'v_w_down': out['v_w_down'], 'v_ln_post_ffn': out['v_ln_post_ffn'], 'v_w_pe_gate': out['v_w_pe_gate'], 'v_b_pe_gate': out['v_b_pe_gate'], 'v_w_pe_proj': out['v_w_pe_proj']}


def _loss(weights, diff, rest, loss_target):
    with _jax.named_scope("forward"):
        args = {**rest, TWIN_DIFF_INPUT: diff, **{k: w.astype(_WEIGHT_DTYPES[k]) for k, w in weights.items()}}
        y = _forward(args)
    with _jax.named_scope("loss_head"):
        err = _jnp.square(y.astype(_jnp.float32) - loss_target)
        return 0.5 * _jnp.sum(_jnp.mean(err, axis=-1)) if err.ndim else 0.5 * err


def _adamw(w, g, m, v):
    m = ADAM_B1 * m + (1.0 - ADAM_B1) * g
    v = ADAM_B2 * v + (1.0 - ADAM_B2) * _jnp.square(g)
    m_hat = m / (1.0 - ADAM_B1 ** ADAM_STEP)
    v_hat = v / (1.0 - ADAM_B2 ** ADAM_STEP)
    delta = -ADAM_LR * (m_hat / (_jnp.sqrt(v_hat) + ADAM_EPS) + ADAM_WD * w)
    return delta, m, v


def reference(x, p, ln_pre_mix, w_in, sgu_ln_g, sgu_ln_b, w_spatial, b_spatial, attn_out_norm, sgu_out_norm, w_out, ln_post_mix, ln_pre_ffn, w_gate_up, w_down, ln_post_ffn, w_pe_gate, b_pe_gate, w_pe_proj, loss_target, m_ln_pre_mix, m_w_in, m_sgu_ln_g, m_sgu_ln_b, m_w_spatial, m_b_spatial, m_attn_out_norm, m_sgu_out_norm, m_w_out, m_ln_post_mix, m_ln_pre_ffn, m_w_gate_up, m_w_down, m_ln_post_ffn, m_w_pe_gate, m_b_pe_gate, m_w_pe_proj, v_ln_pre_mix, v_w_in, v_sgu_ln_g, v_sgu_ln_b, v_w_spatial, v_b_spatial, v_attn_out_norm, v_sgu_out_norm, v_w_out, v_ln_post_mix, v_ln_pre_ffn, v_w_gate_up, v_w_down, v_ln_post_ffn, v_w_pe_gate, v_b_pe_gate, v_w_pe_proj):
    given = dict(x=x, p=p, ln_pre_mix=ln_pre_mix, w_in=w_in, sgu_ln_g=sgu_ln_g, sgu_ln_b=sgu_ln_b, w_spatial=w_spatial, b_spatial=b_spatial, attn_out_norm=attn_out_norm, sgu_out_norm=sgu_out_norm, w_out=w_out, ln_post_mix=ln_post_mix, ln_pre_ffn=ln_pre_ffn, w_gate_up=w_gate_up, w_down=w_down, ln_post_ffn=ln_post_ffn, w_pe_gate=w_pe_gate, b_pe_gate=b_pe_gate, w_pe_proj=w_pe_proj, loss_target=loss_target, m_ln_pre_mix=m_ln_pre_mix, m_w_in=m_w_in, m_sgu_ln_g=m_sgu_ln_g, m_sgu_ln_b=m_sgu_ln_b, m_w_spatial=m_w_spatial, m_b_spatial=m_b_spatial, m_attn_out_norm=m_attn_out_norm, m_sgu_out_norm=m_sgu_out_norm, m_w_out=m_w_out, m_ln_post_mix=m_ln_post_mix, m_ln_pre_ffn=m_ln_pre_ffn, m_w_gate_up=m_w_gate_up, m_w_down=m_w_down, m_ln_post_ffn=m_ln_post_ffn, m_w_pe_gate=m_w_pe_gate, m_b_pe_gate=m_b_pe_gate, m_w_pe_proj=m_w_pe_proj, v_ln_pre_mix=v_ln_pre_mix, v_w_in=v_w_in, v_sgu_ln_g=v_sgu_ln_g, v_sgu_ln_b=v_sgu_ln_b, v_w_spatial=v_w_spatial, v_b_spatial=v_b_spatial, v_attn_out_norm=v_attn_out_norm, v_sgu_out_norm=v_sgu_out_norm, v_w_out=v_w_out, v_ln_post_mix=v_ln_post_mix, v_ln_pre_ffn=v_ln_pre_ffn, v_w_gate_up=v_w_gate_up, v_w_down=v_w_down, v_ln_post_ffn=v_ln_post_ffn, v_w_pe_gate=v_w_pe_gate, v_b_pe_gate=v_b_pe_gate, v_w_pe_proj=v_w_pe_proj)
    weights = {n: given[n] for n in TWIN_WEIGHTS}
    shared = {n: given[n] for n in SHARED_INPUTS}
    per_example = {n: given[n] for n in ['x', 'p']}
    grad_fn = _jax.value_and_grad(_loss, argnums=(0, 1))

    def one_microbatch(ex, loss_target):
        ex = dict(ex)
        diff = ex.pop(TWIN_DIFF_INPUT)
        return grad_fn(weights, diff, {**shared, **ex}, loss_target)

    if N_MICROBATCH == 1:
        loss, (grad_w, grad_x) = one_microbatch(per_example, given["loss_target"])
    else:
        def body(carry, xs):
            loss_sum, grad_sum = carry
            l_k, (gw_k, gx_k) = one_microbatch(xs[0], xs[1])
            with _jax.named_scope("update"):
                return (loss_sum + l_k, _jax.tree.map(_jnp.add, grad_sum, gw_k)), gx_k

        init = (_jnp.zeros((), _jnp.float32), _jax.tree.map(_jnp.zeros_like, weights))
        (loss, grad_w), grad_x = _jax.lax.scan(body, init, (per_example, given["loss_target"]))
    with _jax.named_scope("update"):
        delta_w, new_m, new_v = {}, {}, {}
        for n in TWIN_WEIGHTS:
            delta_w[n], new_m[n], new_v[n] = _adamw(weights[n], grad_w[n], given["m_" + n], given["v_" + n])
    return (loss, grad_x, *[grad_w[n] for n in TWIN_WEIGHTS], *[delta_w[n] for n in TWIN_WEIGHTS],
            *[new_m[n] for n in TWIN_WEIGHTS], *[new_v[n] for n in TWIN_WEIGHTS])
```

```python
import math

import jax
import jax.numpy as jnp
from jax import lax
from jax.experimental import pallas as pl
from jax.experimental.pallas import tpu as pltpu

F32 = jnp.float32
BF = jnp.bfloat16
SDS = jax.ShapeDtypeStruct

D_MODEL = 1024
ATTN_W = 512
SGU_W = 512
HEAD_DIM = 64
N_GROUPS = 4
GROUP_DIM = 128
CHUNK = 128
D_FF = 2816
PLE_DIM = 256
PROJ = 3 * ATTN_W + 2 * SGU_W
DILATIONS = (1, 4, 16)
QBLK = 128
EPS = 1e-6
NEG = -1e30
N_DEV = 8
LANES = 128

ADAM_LR = 0.001
ADAM_B1 = 0.9
ADAM_B2 = 0.999
ADAM_EPS = 1e-08
ADAM_WD = 0.01
ADAM_STEP = 10

BIG = ("w_in", "w_out", "w_gate_up", "w_down", "w_pe_gate", "w_pe_proj")
BIG_ROWS = (320, 128, 704, 352, 128, 32)
BIG_TOTAL = sum(BIG_ROWS)
SMALL_ROWS = 80

MIB = 2 ** 20


def _params(sem, vmem_mib):
    return pltpu.CompilerParams(dimension_semantics=sem, vmem_limit_bytes=vmem_mib * MIB)


def _dot(a, b):
    return jnp.dot(a, b, preferred_element_type=F32)


def _dot_nt(a, b):
    return lax.dot_general(a, b, (((1,), (1,)), ((), ())), preferred_element_type=F32)


def _dot_tn(a, b):
    return lax.dot_general(a, b, (((0,), (0,)), ((), ())), preferred_element_type=F32)


def _rstd(x):
    return lax.rsqrt(jnp.mean(x * x, axis=-1, keepdims=True) + EPS)


def _rms_bwd(dy, x, g):
    r = _rstd(x)
    n = x * r
    dn = dy * g
    dx = r * (dn - n * jnp.mean(dn * n, axis=-1, keepdims=True))
    return dx, dy * n


def _colsum(v):
    return jnp.sum(v, axis=0, keepdims=True)


_G0 = math.sqrt(2.0 / math.pi)
_G1 = 0.044715


def _gelu(x):
    t = jnp.tanh(_G0 * (x + _G1 * x * x * x))
    return 0.5 * x * (1.0 + t), t


def _gelu_grad(x, t):
    return 0.5 * (1.0 + t) + 0.5 * x * (1.0 - t * t) * (_G0 * (1.0 + 3.0 * _G1 * x * x))


def _sigmoid(x):
    return 1.0 / (1.0 + jnp.exp(-x))


def _row(i):
    return (i, 0)


def _fixed(i):
    return (0, 0)


def _acc_init(step, *refs):
    @pl.when(step == 0)
    def _():
        for r in refs:
            r[...] = jnp.zeros_like(r)


def _in_proj(x, g1, w_in, tm=256):
    T = x.shape[0]

    def body(x_ref, g_ref, w_ref, proj_ref, a_ref):
        xv = x_ref[...]
        a = (xv * _rstd(xv) * g_ref[...]).astype(BF)
        a_ref[...] = a
        proj_ref[...] = _dot(a, w_ref[...])

    return pl.pallas_call(
        body, grid=(T // tm,),
        in_specs=[pl.BlockSpec((tm, D_MODEL), _row), pl.BlockSpec((1, D_MODEL), _fixed),
                  pl.BlockSpec((D_MODEL, PROJ), _fixed)],
        out_specs=[pl.BlockSpec((tm, PROJ), _row), pl.BlockSpec((tm, D_MODEL), _row)],
        out_shape=[SDS((T, PROJ), F32), SDS((T, D_MODEL), BF)],
        name="in_proj", compiler_params=_params(("arbitrary",), 40))(x, g1, w_in)


def _attn_bias(sl_ref, bias_ref):
    qi = lax.broadcasted_iota(jnp.int32, (QBLK, QBLK), 0)
    kj = lax.broadcasted_iota(jnp.int32, (QBLK, QBLK), 1)
    step = qi - kj
    for di, d in enumerate(DILATIONS):
        for j in range(2):
            sl = sl_ref[0, j:j + 1, :]
            cur = jnp.where(step >= 0, -sl * (step * d).astype(F32), NEG)
            prev = jnp.where(step <= 0, -sl * ((step + QBLK) * d).astype(F32), NEG)
            bias_ref[(di * 2 + j) * 2] = cur
            bias_ref[(di * 2 + j) * 2 + 1] = prev


def _attn_rows(start, d):
    if d == 1:
        return pl.ds(pl.multiple_of(start, QBLK), QBLK)
    return pl.ds(start, QBLK, stride=d)


def _attn_block_index(i, nblk, d):
    nb = nblk // d
    r = i // nb
    n = i % nb
    start = r + (d * QBLK) * n
    pstart = jnp.maximum(start - d * QBLK, r)
    return start, pstart, n == 0


def _attn_fwd(proj, slopes):
    T = proj.shape[0]
    nblk = T // QBLK
    big = dict(pipeline_mode=pl.Buffered(1))

    def body(q_ref, k_ref, v_ref, sl_ref, o_ref, m_ref, l_ref, bias_ref):
        _attn_bias(sl_ref, bias_ref)
        lo = lax.broadcasted_iota(jnp.int32, (QBLK, LANES), 1) < HEAD_DIM

        for di, d in enumerate(DILATIONS):
            def block(i, carry, d=d, di=di):
                start, pstart, first = _attn_block_index(i, nblk, d)
                rows, prows = _attn_rows(start, d), _attn_rows(pstart, d)
                q = q_ref[rows, :] * (HEAD_DIM ** -0.5)
                kc, kp = k_ref[rows, :].astype(BF), k_ref[prows, :].astype(BF)
                vc, vp = v_ref[rows, :].astype(BF), v_ref[prows, :].astype(BF)
                parts = []
                for j in range(2):
                    qm = jnp.where(lo if j == 0 else ~lo, q, 0.0).astype(BF)
                    sc = _dot_nt(qm, kc) + bias_ref[(di * 2 + j) * 2]
                    sp = jnp.where(first, NEG, _dot_nt(qm, kp) + bias_ref[(di * 2 + j) * 2 + 1])
                    m = jnp.maximum(jnp.max(sc, axis=-1, keepdims=True), jnp.max(sp, axis=-1, keepdims=True))
                    pc, pp = jnp.exp(sc - m), jnp.exp(sp - m)
                    l = jnp.sum(pc, axis=-1, keepdims=True) + jnp.sum(pp, axis=-1, keepdims=True)
                    pv = _dot(pc.astype(BF), vc) + _dot(pp.astype(BF), vp)
                    parts.append((m, l, pv))
                m_b = jnp.where(lo, parts[0][0], parts[1][0])
                l_b = jnp.where(lo, parts[0][1], parts[1][1])
                o_b = jnp.where(lo, parts[0][2], parts[1][2])
                if di == 0:
                    m_ref[rows, :] = m_b
                    l_ref[rows, :] = l_b
                    o_ref[rows, :] = o_b
                else:
                    m_o = m_ref[rows, :]
                    m_n = jnp.maximum(m_o, m_b)
                    wa, wb = jnp.exp(m_o - m_n), jnp.exp(m_b - m_n)
                    m_ref[rows, :] = m_n
                    l_ref[rows, :] = wa * l_ref[rows, :] + wb * l_b
                    o_ref[rows, :] = wa * o_ref[rows, :] + wb * o_b
                return carry

            lax.fori_loop(0, nblk, block, 0)

        def finish(i, carry):
            rows = pl.ds(pl.multiple_of(i * QBLK, QBLK), QBLK)
            l = l_ref[rows, :]
            o_ref[rows, :] = o_ref[rows, :] / l
            m_ref[rows, :] = m_ref[rows, :] + jnp.log(l)
            return carry

        lax.fori_loop(0, nblk, finish, 0)

    col = lambda base: pl.BlockSpec((T, LANES), lambda h: (0, base + h), **big)
    return pl.pallas_call(
        body, grid=(4,),
        in_specs=[col(0), col(4), col(8), pl.BlockSpec((1, 8, LANES), lambda h: (h, 0, 0))],
        out_specs=[pl.BlockSpec((T, LANES), lambda h: (0, h)), pl.BlockSpec((T, LANES), lambda h: (0, h))],
        out_shape=[SDS((T, ATTN_W), F32), SDS((T, ATTN_W), F32)],
        scratch_shapes=[pltpu.VMEM((T, LANES), F32), pltpu.VMEM((12, QBLK, QBLK), F32)],
        name="attn_fwd", compiler_params=_params(("arbitrary",), 48))(proj, proj, proj, slopes)


def _sgu_norm(zv, ln_g, ln_b):
    gz, tz = _gelu(zv)
    mu = jnp.mean(gz, axis=-1, keepdims=True)
    xc = gz - mu
    rs = lax.rsqrt(jnp.mean(xc * xc, axis=-1, keepdims=True) + EPS)
    xhat = xc * rs
    return xhat * ln_g + ln_b, xhat, rs, tz


def _causal(w):
    i = lax.broadcasted_iota(jnp.int32, (CHUNK, CHUNK), 0)
    j = lax.broadcasted_iota(jnp.int32, (CHUNK, CHUNK), 1)
    return jnp.where(i >= j, w, 0.0)


def _sgu_fwd(proj, ln_g, ln_b, w_s, b_st, tm=512):
    T = proj.shape[0]

    def body(u_ref, z_ref, g_ref, b_ref, w_ref, bs_ref, out_ref):
        for g in range(N_GROUPS):
            wm = _causal(w_ref[g]).astype(BF)
            cols = slice(g * GROUP_DIM, (g + 1) * GROUP_DIM)
            for c in range(tm // CHUNK):
                rows = slice(c * CHUNK, (c + 1) * CHUNK)
                zn, _, _, _ = _sgu_norm(z_ref[rows, cols], g_ref[...], b_ref[...])
                mixed = _dot(wm, zn.astype(BF)) + bs_ref[:, g:g + 1]
                gu, _ = _gelu(u_ref[rows, cols])
                out_ref[rows, cols] = gu * mixed

    return pl.pallas_call(
        body, grid=(T // tm,),
        in_specs=[pl.BlockSpec((tm, SGU_W), lambda i: (i, 3)), pl.BlockSpec((tm, SGU_W), lambda i: (i, 4)),
                  pl.BlockSpec((1, GROUP_DIM), _fixed), pl.BlockSpec((1, GROUP_DIM), _fixed),
                  pl.BlockSpec((N_GROUPS, CHUNK, CHUNK), lambda i: (0, 0, 0)), pl.BlockSpec((CHUNK, LANES), _fixed)],
        out_specs=pl.BlockSpec((tm, SGU_W), _row),
        out_shape=SDS((T, SGU_W), F32),
        name="sgu_fwd", compiler_params=_params(("arbitrary",), 32))(proj, proj, ln_g, ln_b, w_s, b_st)


def _out_proj(attn, sgu, x, g_a, g_s, w_out, g_pm, g_pf, tm=256):
    T = x.shape[0]

    def body(a_ref, s_ref, x_ref, ga_ref, gs_ref, w_ref, gpm_ref, gpf_ref, grp_ref, mixed_ref, h1_ref, f_ref):
        av, sv = a_ref[...], s_ref[...]
        an = (av * _rstd(av) * ga_ref[...]).astype(BF)
        sn = (sv * _rstd(sv) * gs_ref[...]).astype(BF)
        grp_ref[:, :ATTN_W] = an
        grp_ref[:, ATTN_W:] = sn
        mixed = _dot(an, w_ref[:ATTN_W, :]) + _dot(sn, w_ref[ATTN_W:, :])
        mixed_ref[...] = mixed
        h1 = x_ref[...] + mixed * _rstd(mixed) * gpm_ref[...]
        h1_ref[...] = h1
        f_ref[...] = (h1 * _rstd(h1) * gpf_ref[...]).astype(BF)

    tok = lambda w: pl.BlockSpec((tm, w), _row)
    vec = lambda w: pl.BlockSpec((1, w), _fixed)
    return pl.pallas_call(
        body, grid=(T // tm,),
        in_specs=[tok(ATTN_W), tok(SGU_W), tok(D_MODEL), vec(ATTN_W), vec(SGU_W),
                  pl.BlockSpec((D_MODEL, D_MODEL), _fixed), vec(D_MODEL), vec(D_MODEL)],
        out_specs=[tok(D_MODEL)] * 4,
        out_shape=[SDS((T, D_MODEL), BF), SDS((T, D_MODEL), F32), SDS((T, D_MODEL), F32), SDS((T, D_MODEL), BF)],
        name="out_proj", compiler_params=_params(("arbitrary",), 32))(attn, sgu, x, g_a, g_s, w_out, g_pm, g_pf)


FF_TILE = 1408


def _gate_up(f, w_g, w_u, tm=512):
    T = f.shape[0]
    tn = FF_TILE

    def body(f_ref, wg_ref, wu_ref, g_ref, u_ref, act_ref):
        fv = f_ref[...]
        g = _dot(fv, wg_ref[...])
        u = _dot(fv, wu_ref[...])
        g_ref[...] = g.astype(BF)
        u_ref[...] = u.astype(BF)
        act_ref[...] = (g * _sigmoid(g) * u).astype(BF)

    wspec = pl.BlockSpec((D_MODEL, tn), lambda j, i: (0, j))
    ospec = pl.BlockSpec((tm, tn), lambda j, i: (i, j))
    return pl.pallas_call(
        body, grid=(D_FF // tn, T // tm),
        in_specs=[pl.BlockSpec((tm, D_MODEL), lambda j, i: (i, 0)), wspec, wspec],
        out_specs=[ospec] * 3, out_shape=[SDS((T, D_FF), BF)] * 3,
        name="gate_up", compiler_params=_params(("arbitrary", "arbitrary"), 40))(f, w_g, w_u)


def _down_proj(act, w_down, h1, g_pff, tm=256):
    T = act.shape[0]

    def body(a_ref, w_ref, h1_ref, g_ref, y_ref, h2_ref):
        y = _dot(a_ref[...], w_ref[...])
        y_ref[...] = y
        h2_ref[...] = h1_ref[...] + y * _rstd(y) * g_ref[...]

    return pl.pallas_call(
        body, grid=(T // tm,),
        in_specs=[pl.BlockSpec((tm, D_FF), _row), pl.BlockSpec((D_FF, D_MODEL), _fixed),
                  pl.BlockSpec((tm, D_MODEL), _row), pl.BlockSpec((1, D_MODEL), _fixed)],
        out_specs=[pl.BlockSpec((tm, D_MODEL), _row)] * 2,
        out_shape=[SDS((T, D_MODEL), F32)] * 2,
        name="down_proj", compiler_params=_params(("arbitrary",), 40))(act, w_down, h1, g_pff)


def _pe_and_loss(h2, p, target, w_peg, b_peg, w_pep, tm=256):
    T = h2.shape[0]

    def body(h2_ref, p_ref, t_ref, wg_ref, b_ref, wp_ref,
             dh3_ref, dpp_ref, dpre_ref, h2b_ref, pb_ref, loss_ref, db_ref):
        _acc_init(pl.program_id(0), loss_ref, db_ref)
        h2v = h2_ref[...]
        h2b = h2v.astype(BF)
        pb = p_ref[...].astype(BF)
        h2b_ref[...] = h2b
        pb_ref[...] = pb
        gate = _sigmoid(_dot(h2b, wg_ref[...]) + b_ref[...])
        pp = _dot(pb, wp_ref[...])
        diff = h2v + gate * pp - t_ref[...]
        loss_ref[...] += _colsum(diff * diff)
        dh3 = diff * (1.0 / D_MODEL)
        dh3_ref[...] = dh3
        dpp_ref[...] = (dh3 * gate).astype(BF)
        dpre = dh3 * pp * (gate * (1.0 - gate))
        dpre_ref[...] = dpre.astype(BF)
        db_ref[...] += _colsum(dpre)

    tok = lambda w: pl.BlockSpec((tm, w), _row)
    vec = pl.BlockSpec((1, D_MODEL), _fixed)
    return pl.pallas_call(
        body, grid=(T // tm,),
        in_specs=[tok(D_MODEL), tok(PLE_DIM), tok(D_MODEL), pl.BlockSpec((D_MODEL, D_MODEL), _fixed), vec,
                  pl.BlockSpec((PLE_DIM, D_MODEL), _fixed)],
        out_specs=[tok(D_MODEL), tok(D_MODEL), tok(D_MODEL), tok(D_MODEL), tok(PLE_DIM), vec, vec],
        out_shape=[SDS((T, D_MODEL), F32), SDS((T, D_MODEL), BF), SDS((T, D_MODEL), BF), SDS((T, D_MODEL), BF),
                   SDS((T, PLE_DIM), BF), SDS((1, D_MODEL), F32), SDS((1, D_MODEL), F32)],
        name="pe_and_loss", compiler_params=_params(("arbitrary",), 32))(h2, p, target, w_peg, b_peg, w_pep)


def _weight_grad(a, dy, name, tk=512):
    T, ka = a.shape
    n = dy.shape[1]
    tka = 1408 if ka == D_FF else min(ka, 1024)
    tn = FF_TILE if n == D_FF else min(n, 1024)

    def body(a_ref, dy_ref, out_ref):
        _acc_init(pl.program_id(2), out_ref)
        out_ref[...] += _dot_tn(a_ref[...].astype(BF), dy_ref[...].astype(BF))

    return pl.pallas_call(
        body, grid=(ka // tka, n // tn, T // tk),
        in_specs=[pl.BlockSpec((tk, tka), lambda i, j, k: (k, i)), pl.BlockSpec((tk, tn), lambda i, j, k: (k, j))],
        out_specs=pl.BlockSpec((tka, tn), lambda i, j, k: (i, j)),
        out_shape=SDS((ka, n), F32),
        name="grad_" + name, compiler_params=_params(("arbitrary",) * 3, 40))(a, dy)


def _pe_bwd(dpre, w_peg_t, dh3, y, g_pff, tm=256):
    T = y.shape[0]

    def body(dp_ref, w_ref, dh3_ref, y_ref, g_ref, dh2_ref, dy_ref, dg_ref):
        _acc_init(pl.program_id(0), dg_ref)
        dh2 = dh3_ref[...] + _dot(dp_ref[...], w_ref[...])
        dh2_ref[...] = dh2
        dy, dg = _rms_bwd(dh2, y_ref[...], g_ref[...])
        dy_ref[...] = dy.astype(BF)
        dg_ref[...] += _colsum(dg)

    tok = pl.BlockSpec((tm, D_MODEL), _row)
    vec = pl.BlockSpec((1, D_MODEL), _fixed)
    return pl.pallas_call(
        body, grid=(T // tm,),
        in_specs=[tok, pl.BlockSpec((D_MODEL, D_MODEL), _fixed), tok, tok, vec],
        out_specs=[tok, tok, vec],
        out_shape=[SDS((T, D_MODEL), F32), SDS((T, D_MODEL), BF), SDS((1, D_MODEL), F32)],
        name="pe_bwd", compiler_params=_params(("arbitrary",), 32))(dpre, w_peg_t, dh3, y, g_pff)


def _down_bwd(dy, w_down_t, g, u, tm=512):
    T = dy.shape[0]
    tn = FF_TILE

    def body(dy_ref, w_ref, g_ref, u_ref, dg_ref, du_ref):
        dact = _dot(dy_ref[...], w_ref[...])
        gv = g_ref[...].astype(F32)
        uv = u_ref[...].astype(F32)
        s = _sigmoid(gv)
        dg_ref[...] = (dact * uv * (s * (1.0 + gv * (1.0 - s)))).astype(BF)
        du_ref[...] = (dact * (gv * s)).astype(BF)

    tile = pl.BlockSpec((tm, tn), lambda j, i: (i, j))
    return pl.pallas_call(
        body, grid=(D_FF // tn, T // tm),
        in_specs=[pl.BlockSpec((tm, D_MODEL), lambda j, i: (i, 0)), pl.BlockSpec((D_MODEL, tn), lambda j, i: (0, j)),
                  tile, tile],
        out_specs=[tile, tile], out_shape=[SDS((T, D_FF), BF)] * 2,
        name="down_bwd", compiler_params=_params(("arbitrary", "arbitrary"), 40))(dy, w_down_t, g, u)


def _ffn_in_bwd(dg, du, w_g_t, w_u_t, h1, dh2, mixed, g_pf, g_pm, tm=256):
    T = h1.shape[0]

    def body(dg_ref, du_ref, wg_ref, wu_ref, h1_ref, dh2_ref, mx_ref, gpf_ref, gpm_ref,
             dh1_ref, dmx_ref, dgpf_ref, dgpm_ref):
        _acc_init(pl.program_id(0), dgpf_ref, dgpm_ref)
        df = _dot(dg_ref[...], wg_ref[...]) + _dot(du_ref[...], wu_ref[...])
        dx, dgf = _rms_bwd(df, h1_ref[...], gpf_ref[...])
        dh1 = dh2_ref[...] + dx
        dh1_ref[...] = dh1
        dmx, dgm = _rms_bwd(dh1, mx_ref[...], gpm_ref[...])
        dmx_ref[...] = dmx.astype(BF)
        dgpf_ref[...] += _colsum(dgf)
        dgpm_ref[...] += _colsum(dgm)

    tok = lambda w: pl.BlockSpec((tm, w), _row)
    vec = pl.BlockSpec((1, D_MODEL), _fixed)
    wsp = pl.BlockSpec((D_FF, D_MODEL), _fixed)
    return pl.pallas_call(
        body, grid=(T // tm,),
        in_specs=[tok(D_FF), tok(D_FF), wsp, wsp, tok(D_MODEL), tok(D_MODEL), tok(D_MODEL), vec, vec],
        out_specs=[tok(D_MODEL), tok(D_MODEL), vec, vec],
        out_shape=[SDS((T, D_MODEL), F32), SDS((T, D_MODEL), BF), SDS((1, D_MODEL), F32), SDS((1, D_MODEL), F32)],
        name="ffn_in_bwd", compiler_params=_params(("arbitrary",), 48))(dg, du, w_g_t, w_u_t, h1, dh2, mixed, g_pf, g_pm)


def _out_bwd(dmx, w_out_t, attn, sgu, g_a, g_s, tm=256):
    T = attn.shape[0]

    def body(dm_ref, w_ref, a_ref, s_ref, ga_ref, gs_ref, da_ref, ds_ref, dga_ref, dgs_ref):
        _acc_init(pl.program_id(0), dga_ref, dgs_ref)
        dgr = _dot(dm_ref[...], w_ref[...])
        da, dga = _rms_bwd(dgr[:, :ATTN_W], a_ref[...], ga_ref[...])
        ds, dgs = _rms_bwd(dgr[:, ATTN_W:], s_ref[...], gs_ref[...])
        da_ref[...] = da
        ds_ref[...] = ds
        dga_ref[...] += _colsum(dga)
        dgs_ref[...] += _colsum(dgs)

    tok = lambda w: pl.BlockSpec((tm, w), _row)
    vec = lambda w: pl.BlockSpec((1, w), _fixed)
    return pl.pallas_call(
        body, grid=(T // tm,),
        in_specs=[tok(D_MODEL), pl.BlockSpec((D_MODEL, D_MODEL), _fixed), tok(ATTN_W), tok(SGU_W), vec(ATTN_W), vec(SGU_W)],
        out_specs=[tok(ATTN_W), tok(SGU_W), vec(ATTN_W), vec(SGU_W)],
        out_shape=[SDS((T, ATTN_W), F32), SDS((T, SGU_W), F32), SDS((1, ATTN_W), F32), SDS((1, SGU_W), F32)],
        name="out_bwd", compiler_params=_params(("arbitrary",), 32))(dmx, w_out_t, attn, sgu, g_a, g_s)


def _sgu_bwd(proj, dsgu, ln_g, ln_b, w_s, b_st, tm=512):
    T = proj.shape[0]

    def body(u_ref, z_ref, ds_ref, g_ref, b_ref, w_ref, bs_ref,
             du_ref, dz_ref, dw_ref, dbs_ref, dlg_ref, dlb_ref, dbacc_ref):
        step = pl.program_id(0)
        _acc_init(step, dw_ref, dbs_ref, dlg_ref, dlb_ref, dbacc_ref)
        lng, lnb = g_ref[...], b_ref[...]
        for g in range(N_GROUPS):
            wm = _causal(w_ref[g]).astype(BF)
            cols = slice(g * GROUP_DIM, (g + 1) * GROUP_DIM)
            for c in range(tm // CHUNK):
                rows = slice(c * CHUNK, (c + 1) * CHUNK)
                zv, uv, dout = z_ref[rows, cols], u_ref[rows, cols], ds_ref[rows, cols]
                zn, xhat, rs, tz = _sgu_norm(zv, lng, lnb)
                znb = zn.astype(BF)
                mixed = _dot(wm, znb) + bs_ref[:, g:g + 1]
                gu, tu = _gelu(uv)
                du_ref[rows, cols] = (dout * mixed * _gelu_grad(uv, tu)).astype(BF)
                dmix = dout * gu
                dmb = dmix.astype(BF)
                dw_ref[g] += _causal(_dot_nt(dmb, znb))
                dbacc_ref[g] += dmix
                dzn = _dot_tn(wm, dmb)
                dlg_ref[...] += _colsum(dzn * xhat)
                dlb_ref[...] += _colsum(dzn)
                dxh = dzn * lng
                dgz = rs * (dxh - jnp.mean(dxh, axis=-1, keepdims=True)
                            - xhat * jnp.mean(dxh * xhat, axis=-1, keepdims=True))
                dz_ref[rows, cols] = (dgz * _gelu_grad(zv, tz)).astype(BF)

        @pl.when(step == pl.num_programs(0) - 1)
        def _():
            lane = lax.broadcasted_iota(jnp.int32, (CHUNK, LANES), 1)
            acc = jnp.zeros((CHUNK, LANES), F32)
            for g in range(N_GROUPS):
                acc = jnp.where(lane == g, jnp.sum(dbacc_ref[g], axis=-1, keepdims=True), acc)
            dbs_ref[...] = acc

    tok = pl.BlockSpec((tm, SGU_W), _row)
    vec = pl.BlockSpec((1, GROUP_DIM), _fixed)
    wsp = pl.BlockSpec((N_GROUPS, CHUNK, CHUNK), lambda i: (0, 0, 0))
    sq = pl.BlockSpec((CHUNK, LANES), _fixed)
    return pl.pallas_call(
        body, grid=(T // tm,),
        in_specs=[pl.BlockSpec((tm, SGU_W), lambda i: (i, 3)), pl.BlockSpec((tm, SGU_W), lambda i: (i, 4)), tok,
                  vec, vec, wsp, sq],
        out_specs=[tok, tok, wsp, sq, vec, vec],
        out_shape=[SDS((T, SGU_W), BF), SDS((T, SGU_W), BF), SDS((N_GROUPS, CHUNK, CHUNK), F32),
                   SDS((CHUNK, LANES), F32), SDS((1, GROUP_DIM), F32), SDS((1, GROUP_DIM), F32)],
        scratch_shapes=[pltpu.VMEM((N_GROUPS, CHUNK, LANES), F32)],
        name="sgu_bwd", compiler_params=_params(("arbitrary",), 32))(proj, proj, dsgu, ln_g, ln_b, w_s, b_st)


def _attn_bwd(proj, do, o, lse, slopes):
    T = proj.shape[0]
    nblk = T // QBLK
    big = dict(pipeline_mode=pl.Buffered(1))

    def body(q_ref, k_ref, v_ref, do_ref, o_ref, l_ref, sl_ref, dq_ref, dk_ref, dv_ref, bias_ref):
        _attn_bias(sl_ref, bias_ref)
        lo = lax.broadcasted_iota(jnp.int32, (QBLK, LANES), 1) < HEAD_DIM
        scale = HEAD_DIM ** -0.5
        dq_ref[...] = jnp.zeros_like(dq_ref)
        dk_ref[...] = jnp.zeros_like(dk_ref)
        dv_ref[...] = jnp.zeros_like(dv_ref)

        for di, d in enumerate(DILATIONS):
            def block(i, carry, d=d, di=di):
                start, pstart, first = _attn_block_index(i, nblk, d)
                rows, prows = _attn_rows(start, d), _attn_rows(pstart, d)
                q = q_ref[rows, :] * scale
                kc, kp = k_ref[rows, :].astype(BF), k_ref[prows, :].astype(BF)
                vc, vp = v_ref[rows, :].astype(BF), v_ref[prows, :].astype(BF)
                dov = do_ref[rows, :]
                prod = dov * o_ref[rows, :]
                lse_b = l_ref[rows, :]
                dq_parts = []
                dkc = dkp = dvc = dvp = None
                for j in range(2):
                    msk = lo if j == 0 else ~lo
                    qm = jnp.where(msk, q, 0.0).astype(BF)
                    dom = jnp.where(msk, dov, 0.0).astype(BF)
                    delta = jnp.sum(jnp.where(msk, prod, 0.0), axis=-1, keepdims=True)
                    lj = lse_b[:, j * HEAD_DIM:j * HEAD_DIM + 1]
                    sc = _dot_nt(qm, kc) + bias_ref[(di * 2 + j) * 2]
                    sp = jnp.where(first, NEG, _dot_nt(qm, kp) + bias_ref[(di * 2 + j) * 2 + 1])
                    pc, pp = jnp.exp(sc - lj), jnp.exp(sp - lj)
                    dsc = (pc * (_dot_nt(dom, vc) - delta)).astype(BF)
                    dsp = (pp * (_dot_nt(dom, vp) - delta)).astype(BF)
                    pcb, ppb = pc.astype(BF), pp.astype(BF)
                    dq_parts.append(_dot(dsc, kc) + _dot(dsp, kp))
                    t_kc, t_kp = _dot_tn(dsc, qm), _dot_tn(dsp, qm)
                    t_vc, t_vp = _dot_tn(pcb, dom), _dot_tn(ppb, dom)
                    dkc = t_kc if j == 0 else dkc + t_kc
                    dkp = t_kp if j == 0 else dkp + t_kp
                    dvc = t_vc if j == 0 else dvc + t_vc
                    dvp = t_vp if j == 0 else dvp + t_vp
                dq_ref[rows, :] += jnp.where(lo, dq_parts[0], dq_parts[1]) * scale
                dk_ref[rows, :] += dkc
                dv_ref[rows, :] += dvc
                dk_ref[prows, :] += dkp
                dv_ref[prows, :] += dvp
                return carry

            lax.fori_loop(0, nblk, block, 0)

    col = lambda base: pl.BlockSpec((T, LANES), lambda h: (0, base + h), **big)
    out = pl.BlockSpec((T, LANES), lambda h: (0, h))
    return pl.pallas_call(
        body, grid=(4,),
        in_specs=[col(0), col(4), col(8), col(0), col(0), col(0), pl.BlockSpec((1, 8, LANES), lambda h: (h, 0, 0))],
        out_specs=[out, out, out],
        out_shape=[SDS((T, ATTN_W), F32)] * 3,
        scratch_shapes=[pltpu.VMEM((12, QBLK, QBLK), F32)],
        name="attn_bwd", compiler_params=_params(("arbitrary",), 60))(proj, proj, proj, do, o, lse, slopes)


def _in_bwd(dparts, w_in_t, x, dh1, g1, tm=256):
    T = x.shape[0]
    n = len(dparts)
    w = ATTN_W

    def body(*refs):
        d_refs, (w_ref, x_ref, dh1_ref, g_ref, dx_ref, dg_ref) = refs[:n], refs[n:]
        _acc_init(pl.program_id(0), dg_ref)
        da = None
        for i, r in enumerate(d_refs):
            t = _dot(r[...].astype(BF), w_ref[i * w:(i + 1) * w, :])
            da = t if da is None else da + t
        dx, dg = _rms_bwd(da, x_ref[...], g_ref[...])
        dx_ref[...] = dh1_ref[...] + dx
        dg_ref[...] += _colsum(dg)

    tok = lambda c: pl.BlockSpec((tm, c), _row)
    vec = pl.BlockSpec((1, D_MODEL), _fixed)
    return pl.pallas_call(
        body, grid=(T // tm,),
        in_specs=[tok(w)] * n + [pl.BlockSpec((PROJ, D_MODEL), _fixed), tok(D_MODEL), tok(D_MODEL), vec],
        out_specs=[tok(D_MODEL), vec],
        out_shape=[SDS((T, D_MODEL), F32), SDS((1, D_MODEL), F32)],
        name="in_bwd", compiler_params=_params(("arbitrary",), 40))(*dparts, w_in_t, x, dh1, g1)


def _local_step(x, p, target, big, small):
    slopes = jnp.broadcast_to((2.0 ** -(jnp.arange(8, dtype=F32) + 1.0)).reshape(4, 2, 1), (4, 2, LANES))
    slopes = jnp.concatenate([slopes, jnp.zeros((4, 6, LANES), F32)], axis=1)
    b_st = jnp.pad(small["b_spatial"].T, ((0, 0), (0, LANES - N_GROUPS)))

    proj, a = _in_proj(x, small["ln_pre_mix"], big["w_in"])
    attn, lse = _attn_fwd(proj, slopes)
    sgu = _sgu_fwd(proj, small["sgu_ln_g"], small["sgu_ln_b"], small["w_spatial"], b_st)
    groups, mixed, h1, f = _out_proj(attn, sgu, x, small["attn_out_norm"], small["sgu_out_norm"], big["w_out"],
                                     small["ln_post_mix"], small["ln_pre_ffn"])
    g, u, act = _gate_up(f, big["w_g"], big["w_u"])
    y, h2 = _down_proj(act, big["w_down"], h1, small["ln_post_ffn"])
    dh3, dpp, dpre, h2b, pb, loss_cols, db_peg = _pe_and_loss(h2, p, target, big["w_pe_gate"], small["b_pe_gate"],
                                                              big["w_pe_proj"])
    loss = 0.5 * jnp.sum(loss_cols) * (1.0 / D_MODEL)

    gw = {}
    gw["w_pe_proj"] = _weight_grad(pb, dpp, "w_pe_proj")
    gw["w_pe_gate"] = _weight_grad(h2b, dpre, "w_pe_gate")
    dh2, dy, d_pff = _pe_bwd(dpre, big["w_pe_gate_t"], dh3, y, small["ln_post_ffn"])
    gw["w_down"] = _weight_grad(act, dy, "w_down")
    dg, du = _down_bwd(dy, big["w_down_t"], g, u)
    gw["w_gate_up"] = jnp.concatenate([_weight_grad(f, dg, "w_gate"), _weight_grad(f, du, "w_up")], axis=1)
    dh1, dmx, d_pf, d_pm = _ffn_in_bwd(dg, du, big["w_g_t"], big["w_u_t"], h1, dh2, mixed,
                                       small["ln_pre_ffn"], small["ln_post_mix"])
    gw["w_out"] = _weight_grad(groups, dmx, "w_out")
    dattn, dsgu, d_ga, d_gs = _out_bwd(dmx, big["w_out_t"], attn, sgu, small["attn_out_norm"], small["sgu_out_norm"])
    dus, dzs, d_ws, d_bst, d_lg, d_lb = _sgu_bwd(proj, dsgu, small["sgu_ln_g"], small["sgu_ln_b"],
                                                 small["w_spatial"], b_st)
    dq, dk, dv = _attn_bwd(proj, dattn, attn, lse, slopes)
    dparts = [dq, dk, dv, dus, dzs]
    gw["w_in"] = jnp.concatenate([_weight_grad(a, t, "w_in_%d" % i) for i, t in enumerate(dparts)], axis=1)
    grad_x, d_g1 = _in_bwd(dparts, big["w_in_t"], x, dh1, small["ln_pre_mix"])

    gs = dict(ln_pre_mix=d_g1, sgu_ln_g=d_lg, sgu_ln_b=d_lb, w_spatial=d_ws, b_spatial=d_bst[:, :N_GROUPS].T,
              attn_out_norm=d_ga, sgu_out_norm=d_gs, ln_post_mix=d_pm, ln_pre_ffn=d_pf, ln_post_ffn=d_pff,
              b_pe_gate=db_peg)
    return loss, grad_x, gw, gs


MESH_ID = pl.DeviceIdType.MESH


def _all_gather(shard):
    rows, width = shard.shape

    def body(x_ref, out_ref, send_sems, recv_sems, local_sem):
        x, y, c = lax.axis_index("x"), lax.axis_index("y"), lax.axis_index("c")
        me, sibling = (x, y, c), (x, y, 1 - c)
        chips = [(1 - x, y), (x, 1 - y), (1 - x, 1 - y)]

        def slot(px, py, pc):
            return out_ref.at[4 * px + 2 * py + pc]

        def copy(k, block, to, src=None):
            return pltpu.make_async_remote_copy(
                src_ref=slot(*block) if src is None else src, dst_ref=slot(*block),
                send_sem=send_sems.at[k], recv_sem=recv_sems.at[k], device_id=to, device_id_type=MESH_ID)

        mine = pltpu.make_async_copy(x_ref, slot(*me), local_sem)
        mine.start()
        first = [copy(0, me, sibling, src=x_ref)]
        first += [copy(1 + j, me, (*chip, c), src=x_ref) for j, chip in enumerate(chips)]
        for cp in first:
            cp.start()
        passed = [copy(4 + j, (*chip, c), sibling) for j, chip in enumerate(chips)]
        for j, chip in enumerate(chips):
            copy(1 + j, (*chip, c), me).wait_recv()
            passed[j].start()
        copy(0, sibling, me).wait_recv()
        for j, chip in enumerate(chips):
            copy(4 + j, (*chip, 1 - c), me).wait_recv()
        for cp in first + passed:
            cp.wait_send()
        mine.wait()

    return pl.pallas_call(
        body, out_shape=SDS((N_DEV, rows, width), shard.dtype),
        in_specs=[pl.BlockSpec(memory_space=pl.ANY)], out_specs=pl.BlockSpec(memory_space=pl.ANY),
        scratch_shapes=[pltpu.SemaphoreType.DMA((7,)), pltpu.SemaphoreType.DMA((7,)), pltpu.SemaphoreType.DMA],
        name="weight_all_gather")(shard)


def _grad_exchange(g_big, g_small):
    _, rows, width = g_big.shape
    srows = g_small.shape[0]
    flips = [(dx, dy, dc) for dx in (0, 1) for dy in (0, 1) for dc in (0, 1)][1:]

    def body(gb_ref, gs_ref, rb_ref, rs_ref, send_sems, recv_sems, local_sems):
        x, y, c = lax.axis_index("x"), lax.axis_index("y"), lax.axis_index("c")
        me = 4 * x + 2 * y + c
        own_b = pltpu.make_async_copy(gb_ref.at[me], rb_ref.at[me], local_sems.at[0])
        own_s = pltpu.make_async_copy(gs_ref, rs_ref.at[me], local_sems.at[1])
        own_b.start()
        own_s.start()
        copies = []
        for k, (dx, dy, dc) in enumerate(flips):
            px, py, pc = (1 - x if dx else x), (1 - y if dy else y), (1 - c if dc else c)
            peer = 4 * px + 2 * py + pc
            copies.append((
                pltpu.make_async_remote_copy(src_ref=gb_ref.at[peer], dst_ref=rb_ref.at[me], send_sem=send_sems.at[k],
                                             recv_sem=recv_sems.at[k], device_id=(px, py, pc), device_id_type=MESH_ID),
                pltpu.make_async_remote_copy(src_ref=gb_ref.at[peer], dst_ref=rb_ref.at[peer], send_sem=send_sems.at[k],
                                             recv_sem=recv_sems.at[k], device_id=(px, py, pc), device_id_type=MESH_ID)))
            copies.append((
                pltpu.make_async_remote_copy(src_ref=gs_ref, dst_ref=rs_ref.at[me], send_sem=send_sems.at[7 + k],
                                             recv_sem=recv_sems.at[7 + k], device_id=(px, py, pc), device_id_type=MESH_ID),
                pltpu.make_async_remote_copy(src_ref=gs_ref, dst_ref=rs_ref.at[peer], send_sem=send_sems.at[7 + k],
                                             recv_sem=recv_sems.at[7 + k], device_id=(px, py, pc), device_id_type=MESH_ID)))
        for send, _ in copies:
            send.start()
        for _, arrival in copies:
            arrival.wait_recv()
        for send, _ in copies:
            send.wait_send()
        own_b.wait()
        own_s.wait()

    hbm = pl.BlockSpec(memory_space=pl.ANY)
    return pl.pallas_call(
        body, out_shape=[SDS((N_DEV, rows, width), g_big.dtype), SDS((N_DEV, srows, width), g_small.dtype)],
        in_specs=[hbm, hbm], out_specs=[hbm, hbm],
        scratch_shapes=[pltpu.SemaphoreType.DMA((14,)), pltpu.SemaphoreType.DMA((14,)), pltpu.SemaphoreType.DMA((2,))],
        name="grad_exchange")(g_big, g_small)


def _adamw(parts, w, m, v, tr):
    rows, width = w.shape
    c1 = 1.0 - ADAM_B1 ** ADAM_STEP
    c2 = 1.0 - ADAM_B2 ** ADAM_STEP

    def body(p_ref, w_ref, m_ref, v_ref, g_ref, d_ref, nm_ref, nv_ref):
        g = p_ref[0]
        for s in range(1, N_DEV):
            g = g + p_ref[s]
        nm = ADAM_B1 * m_ref[...] + (1.0 - ADAM_B1) * g
        nv = ADAM_B2 * v_ref[...] + (1.0 - ADAM_B2) * (g * g)
        g_ref[...] = g
        nm_ref[...] = nm
        nv_ref[...] = nv
        d_ref[...] = -ADAM_LR * ((nm / c1) / (jnp.sqrt(nv / c2) + ADAM_EPS) + ADAM_WD * w_ref[...])

    blk = pl.BlockSpec((tr, width), _row)
    return pl.pallas_call(
        body, grid=(rows // tr,),
        in_specs=[pl.BlockSpec((N_DEV, tr, width), lambda i: (0, i, 0)), blk, blk, blk],
        out_specs=[blk] * 4, out_shape=[SDS((rows, width), F32)] * 4,
        name="adamw", compiler_params=_params(("arbitrary",), 32))(parts, w, m, v)


SMALL = ("w_spatial", "ln_pre_mix", "ln_post_mix", "ln_pre_ffn", "ln_post_ffn", "b_pe_gate",
         "attn_out_norm", "sgu_out_norm", "b_spatial", "sgu_ln_g", "sgu_ln_b")


def _pack_small(t):
    rows = [t["w_spatial"].reshape(-1, D_MODEL)]
    for name in SMALL[1:]:
        flat = t[name].reshape(1, -1)
        rows.append(jnp.pad(flat, ((0, 0), (0, D_MODEL - flat.shape[1]))))
    used = sum(r.shape[0] for r in rows)
    rows.append(jnp.zeros((SMALL_ROWS - used, D_MODEL), F32))
    return jnp.concatenate(rows, axis=0)


def _unpack_small(packed, shapes):
    out = {"w_spatial": packed[:64].reshape(shapes["w_spatial"])}
    for i, name in enumerate(SMALL[1:]):
        size = math.prod(shapes[name])
        out[name] = packed[64 + i, :size].reshape(shapes[name])
    return out


def _pack_big_shards(t):
    return jnp.concatenate([t[name].reshape(-1, D_MODEL) for name in BIG], axis=0)


def _unpack_big_shards(packed, shapes):
    out, off = {}, 0
    for name, r in zip(BIG, BIG_ROWS):
        out[name] = packed[off:off + r].reshape(shapes[name])
        off += r
    return out


def _split_cols(full, n):
    k = full.shape[0]
    return full.reshape(k, N_DEV, n).transpose(1, 0, 2).reshape(N_DEV, -1, D_MODEL)


def _join_cols(slabs, k):
    n = slabs.shape[1] * D_MODEL // k
    return slabs.reshape(N_DEV, k, n).transpose(1, 0, 2).reshape(k, N_DEV * n)


def _full_weights(gathered):
    parts, off = {}, 0
    for name, r in zip(BIG, BIG_ROWS):
        parts[name] = gathered[:, off:off + r]
        off += r
    w_in = _join_cols(parts["w_in"], D_MODEL)
    w_gu = _join_cols(parts["w_gate_up"], D_MODEL)
    big = dict(
        w_in=w_in, w_out=parts["w_out"].reshape(D_MODEL, D_MODEL), w_g=w_gu[:, :D_FF], w_u=w_gu[:, D_FF:],
        w_down=parts["w_down"].reshape(D_FF, D_MODEL), w_pe_gate=parts["w_pe_gate"].reshape(D_MODEL, D_MODEL),
        w_pe_proj=_join_cols(parts["w_pe_proj"], PLE_DIM))
    for name in ("w_in", "w_out", "w_g", "w_u", "w_down", "w_pe_gate"):
        big[name + "_t"] = big[name].T
    return big


def _grad_slabs(gw):
    return jnp.concatenate([
        _split_cols(gw["w_in"], PROJ // N_DEV), gw["w_out"].reshape(N_DEV, -1, D_MODEL),
        _split_cols(gw["w_gate_up"], 2 * D_FF // N_DEV), gw["w_down"].reshape(N_DEV, -1, D_MODEL),
        gw["w_pe_gate"].reshape(N_DEV, -1, D_MODEL), _split_cols(gw["w_pe_proj"], D_MODEL // N_DEV)], axis=1)


WEIGHTS = ("ln_pre_mix", "w_in", "sgu_ln_g", "sgu_ln_b", "w_spatial", "b_spatial", "attn_out_norm", "sgu_out_norm",
           "w_out", "ln_post_mix", "ln_pre_ffn", "w_gate_up", "w_down", "ln_post_ffn", "w_pe_gate", "b_pe_gate",
           "w_pe_proj")


def kernel(x, p, ln_pre_mix, w_in, sgu_ln_g, sgu_ln_b, w_spatial, b_spatial, attn_out_norm, sgu_out_norm, w_out, ln_post_mix, ln_pre_ffn, w_gate_up, w_down, ln_post_ffn, w_pe_gate, b_pe_gate, w_pe_proj, loss_target, m_ln_pre_mix, m_w_in, m_sgu_ln_g, m_sgu_ln_b, m_w_spatial, m_b_spatial, m_attn_out_norm, m_sgu_out_norm, m_w_out, m_ln_post_mix, m_ln_pre_ffn, m_w_gate_up, m_w_down, m_ln_post_ffn, m_w_pe_gate, m_b_pe_gate, m_w_pe_proj, v_ln_pre_mix, v_w_in, v_sgu_ln_g, v_sgu_ln_b, v_w_spatial, v_b_spatial, v_attn_out_norm, v_sgu_out_norm, v_w_out, v_ln_post_mix, v_ln_pre_ffn, v_w_gate_up, v_w_down, v_ln_post_ffn, v_w_pe_gate, v_b_pe_gate, v_w_pe_proj):
    given = dict(locals())
    w = {n: given[n] for n in WEIGHTS}
    m = {n: given["m_" + n] for n in WEIGHTS}
    v = {n: given["v_" + n] for n in WEIGHTS}
    shapes = {n: w[n].shape for n in WEIGHTS}

    gathered = _all_gather(_pack_big_shards(w).astype(BF))
    big = _full_weights(gathered)
    small = {n: w[n][0] for n in SMALL}
    small = {n: (a.reshape(1, -1) if a.ndim == 1 else a) for n, a in small.items()}

    loss, grad_x, gw, gs = _local_step(x[0], p[0, 0], loss_target[0], big, small)
    loss = lax.psum(loss, ("x", "y", "c"))

    arrived_big, arrived_small = _grad_exchange(_grad_slabs(gw), _pack_small(gs))
    out_big = _adamw(arrived_big, _pack_big_shards(w), _pack_big_shards(m), _pack_big_shards(v), tr=128)
    out_small = _adamw(arrived_small, _pack_small(w), _pack_small(m), _pack_small(v), tr=SMALL_ROWS)

    results = []
    for packed_b, packed_s in zip(out_big, out_small):
        merged = {**_unpack_big_shards(packed_b, shapes), **_unpack_small(packed_s, shapes)}
        results.append([merged[n] for n in WEIGHTS])
    grads, deltas, new_m, new_v = results
    return (loss, grad_x[None], *grads, *deltas, *new_m, *new_v)
```

```python
import math

import jax
import jax.numpy as jnp
from jax import lax
from jax.experimental import pallas as pl
from jax.experimental.pallas import tpu as pltpu

F32 = jnp.float32
BF = jnp.bfloat16
SDS = jax.ShapeDtypeStruct

D_MODEL = 1024
ATTN_W = 512
SGU_W = 512
HEAD_DIM = 64
N_GROUPS = 4
GROUP_DIM = 128
CHUNK = 128
D_FF = 2816
PLE_DIM = 256
PROJ = 3 * ATTN_W + 2 * SGU_W
DILATIONS = (1, 4, 16)
QBLK = 128
EPS = 1e-6
NEG = -1e30
N_DEV = 8
LANES = 128

ADAM_LR = 0.001
ADAM_B1 = 0.9
ADAM_B2 = 0.999
ADAM_EPS = 1e-08
ADAM_WD = 0.01
ADAM_STEP = 10

BIG = ("w_in", "w_out", "w_gate_up", "w_down", "w_pe_gate", "w_pe_proj")
BIG_ROWS = (320, 128, 704, 352, 128, 32)
BIG_TOTAL = sum(BIG_ROWS)
SMALL_ROWS = 80

MIB = 2 ** 20


def _params(sem, vmem_mib):
    return pltpu.CompilerParams(dimension_semantics=sem, vmem_limit_bytes=vmem_mib * MIB)


def _dot(a, b):
    return jnp.dot(a, b, preferred_element_type=F32)


def _dot_nt(a, b):
    return lax.dot_general(a, b, (((1,), (1,)), ((), ())), preferred_element_type=F32)


def _dot_tn(a, b):
    return lax.dot_general(a, b, (((0,), (0,)), ((), ())), preferred_element_type=F32)


def _rstd(x):
    return lax.rsqrt(jnp.mean(x * x, axis=-1, keepdims=True) + EPS)


def _rms_bwd(dy, x, g):
    r = _rstd(x)
    n = x * r
    dn = dy * g
    dx = r * (dn - n * jnp.mean(dn * n, axis=-1, keepdims=True))
    return dx, dy * n


def _colsum(v):
    return jnp.sum(v, axis=0, keepdims=True)


_G0 = math.sqrt(2.0 / math.pi)
_G1 = 0.044715


def _gelu(x):
    t = jnp.tanh(_G0 * (x + _G1 * x * x * x))
    return 0.5 * x * (1.0 + t), t


def _gelu_grad(x, t):
    return 0.5 * (1.0 + t) + 0.5 * x * (1.0 - t * t) * (_G0 * (1.0 + 3.0 * _G1 * x * x))


def _sigmoid(x):
    return 1.0 / (1.0 + jnp.exp(-x))


def _row(i):
    return (i, 0)


def _fixed(i):
    return (0, 0)


def _acc_init(step, *refs):
    @pl.when(step == 0)
    def _():
        for r in refs:
            r[...] = jnp.zeros_like(r)


def _in_proj(x, g1, w_in, tm=256):
    T = x.shape[0]

    def body(x_ref, g_ref, w_ref, proj_ref, a_ref):
        xv = x_ref[...]
        a = (xv * _rstd(xv) * g_ref[...]).astype(BF)
        a_ref[...] = a
        proj_ref[...] = _dot(a, w_ref[...])

    return pl.pallas_call(
        body, grid=(T // tm,),
        in_specs=[pl.BlockSpec((tm, D_MODEL), _row), pl.BlockSpec((1, D_MODEL), _fixed),
                  pl.BlockSpec((D_MODEL, PROJ), _fixed)],
        out_specs=[pl.BlockSpec((tm, PROJ), _row), pl.BlockSpec((tm, D_MODEL), _row)],
        out_shape=[SDS((T, PROJ), F32), SDS((T, D_MODEL), BF)],
        name="in_proj", compiler_params=_params(("arbitrary",), 40))(x, g1, w_in)


ATTN_GROUP = 4


def _attn_bias(sl_ref, bias_ref):
    qi = lax.broadcasted_iota(jnp.int32, (QBLK, QBLK), 0)
    kj = lax.broadcasted_iota(jnp.int32, (QBLK, QBLK), 1)
    step = qi - kj
    for di, d in enumerate(DILATIONS):
        for j in range(2):
            sl = sl_ref[0, j:j + 1, :]
            cur = jnp.where(step >= 0, -sl * (step * d).astype(F32), NEG)
            prev = jnp.where(step <= 0, -sl * ((step + QBLK) * d).astype(F32), NEG)
            base = (di * 2 + j) * 2
            bias_ref[base, :, :QBLK] = prev
            bias_ref[base, :, QBLK:] = cur
            bias_ref[base + 1, :, :QBLK] = jnp.full((QBLK, QBLK), NEG, F32)
            bias_ref[base + 1, :, QBLK:] = cur


def _attn_rows(start, d, blocks=1):
    if d == 1:
        return pl.ds(pl.multiple_of(start, QBLK), blocks * QBLK)
    return pl.ds(start, blocks * QBLK, stride=d)


def _attn_group_index(i, nblk, d, group):
    per = nblk // d // group
    r = i // per
    n0 = (i % per) * group
    start = r + (d * QBLK) * n0
    pstart = jnp.maximum(start - d * QBLK, r)
    return start, pstart, n0 == 0


def _attn_group(nblk, d):
    return min(ATTN_GROUP, nblk // d)


def _attn_fwd(proj, slopes):
    T = proj.shape[0]
    nblk = T // QBLK
    big = dict(pipeline_mode=pl.Buffered(1))

    def body(q_ref, k_ref, v_ref, sl_ref, o_ref, m_ref, l_ref, bias_ref):
        _attn_bias(sl_ref, bias_ref)
        lo = lax.broadcasted_iota(jnp.int32, (1, LANES), 1) < HEAD_DIM

        for di, d in enumerate(DILATIONS):
            group = _attn_group(nblk, d)

            def step(i, carry, d=d, di=di, group=group):
                start, pstart, first = _attn_group_index(i, nblk, d, group)
                rows, prows = _attn_rows(start, d, group), _attn_rows(pstart, d)
                q = q_ref[rows, :] * (HEAD_DIM ** -0.5)
                k = jnp.concatenate([k_ref[prows, :], k_ref[rows, :]], axis=0).astype(BF)
                v = jnp.concatenate([v_ref[prows, :], v_ref[rows, :]], axis=0).astype(BF)
                qm = [jnp.where(lo if j == 0 else ~lo, q, 0.0).astype(BF) for j in range(2)]
                for b in range(group):
                    qb = slice(b * QBLK, (b + 1) * QBLK)
                    kb = slice(b * QBLK, (b + 2) * QBLK)
                    parts = []
                    for j in range(2):
                        base = (di * 2 + j) * 2
                        bias = bias_ref[base + first.astype(jnp.int32)] if b == 0 else bias_ref[base]
                        s = _dot_nt(qm[j][qb], k[kb]) + bias
                        m = jnp.max(s, axis=-1, keepdims=True)
                        pr = jnp.exp(s - m)
                        parts.append((m, jnp.sum(pr, axis=-1, keepdims=True), _dot(pr.astype(BF), v[kb])))
                    m_b = jnp.where(lo, parts[0][0], parts[1][0])
                    l_b = jnp.where(lo, parts[0][1], parts[1][1])
                    o_b = jnp.where(lo, parts[0][2], parts[1][2])
                    out = _attn_rows(start + b * (d * QBLK), d)
                    if di == 0:
                        m_ref[out, :] = m_b
                        l_ref[out, :] = l_b
                        o_ref[out, :] = o_b
                    else:
                        m_o = m_ref[out, :]
                        m_n = jnp.maximum(m_o, m_b)
                        wa, wb = jnp.exp(m_o - m_n), jnp.exp(m_b - m_n)
                        m_ref[out, :] = m_n
                        l_ref[out, :] = wa * l_ref[out, :] + wb * l_b
                        o_ref[out, :] = wa * o_ref[out, :] + wb * o_b
                return carry

            lax.fori_loop(0, nblk // group, step, 0)

        def finish(i, carry):
            rows = pl.ds(pl.multiple_of(i * QBLK, QBLK), QBLK)
            l = l_ref[rows, :]
            o_ref[rows, :] = o_ref[rows, :] / l
            m_ref[rows, :] = m_ref[rows, :] + jnp.log(l)
            return carry

        lax.fori_loop(0, nblk, finish, 0)

    col = lambda base: pl.BlockSpec((T, LANES), lambda h: (0, base + h), **big)
    return pl.pallas_call(
        body, grid=(4,),
        in_specs=[col(0), col(4), col(8), pl.BlockSpec((1, 8, LANES), lambda h: (h, 0, 0))],
        out_specs=[pl.BlockSpec((T, LANES), lambda h: (0, h)), pl.BlockSpec((T, LANES), lambda h: (0, h))],
        out_shape=[SDS((T, ATTN_W), F32), SDS((T, ATTN_W), F32)],
        scratch_shapes=[pltpu.VMEM((T, LANES), F32), pltpu.VMEM((12, QBLK, 2 * QBLK), F32)],
        name="attn_fwd", compiler_params=_params(("arbitrary",), 48))(proj, proj, proj, slopes)


def _sgu_norm(zv, ln_g, ln_b):
    gz, tz = _gelu(zv)
    mu = jnp.mean(gz, axis=-1, keepdims=True)
    xc = gz - mu
    rs = lax.rsqrt(jnp.mean(xc * xc, axis=-1, keepdims=True) + EPS)
    xhat = xc * rs
    return xhat * ln_g + ln_b, xhat, rs, tz


def _causal(w):
    i = lax.broadcasted_iota(jnp.int32, (CHUNK, CHUNK), 0)
    j = lax.broadcasted_iota(jnp.int32, (CHUNK, CHUNK), 1)
    return jnp.where(i >= j, w, 0.0)


def _sgu_fwd(proj, ln_g, ln_b, w_s, b_st, tm=512):
    T = proj.shape[0]

    def body(u_ref, z_ref, g_ref, b_ref, w_ref, bs_ref, out_ref):
        for g in range(N_GROUPS):
            wm = _causal(w_ref[g]).astype(BF)
            cols = slice(g * GROUP_DIM, (g + 1) * GROUP_DIM)
            for c in range(tm // CHUNK):
                rows = slice(c * CHUNK, (c + 1) * CHUNK)
                zn, _, _, _ = _sgu_norm(z_ref[rows, cols], g_ref[...], b_ref[...])
                mixed = _dot(wm, zn.astype(BF)) + bs_ref[:, g:g + 1]
                gu, _ = _gelu(u_ref[rows, cols])
                out_ref[rows, cols] = gu * mixed

    return pl.pallas_call(
        body, grid=(T // tm,),
        in_specs=[pl.BlockSpec((tm, SGU_W), lambda i: (i, 3)), pl.BlockSpec((tm, SGU_W), lambda i: (i, 4)),
                  pl.BlockSpec((1, GROUP_DIM), _fixed), pl.BlockSpec((1, GROUP_DIM), _fixed),
                  pl.BlockSpec((N_GROUPS, CHUNK, CHUNK), lambda i: (0, 0, 0)), pl.BlockSpec((CHUNK, LANES), _fixed)],
        out_specs=pl.BlockSpec((tm, SGU_W), _row),
        out_shape=SDS((T, SGU_W), F32),
        name="sgu_fwd", compiler_params=_params(("arbitrary",), 32))(proj, proj, ln_g, ln_b, w_s, b_st)


def _out_proj(attn, sgu, x, g_a, g_s, w_out, g_pm, g_pf, tm=256):
    T = x.shape[0]

    def body(a_ref, s_ref, x_ref, ga_ref, gs_ref, w_ref, gpm_ref, gpf_ref, grp_ref, mixed_ref, h1_ref, f_ref):
        av, sv = a_ref[...], s_ref[...]
        an = (av * _rstd(av) * ga_ref[...]).astype(BF)
        sn = (sv * _rstd(sv) * gs_ref[...]).astype(BF)
        grp_ref[:, :ATTN_W] = an
        grp_ref[:, ATTN_W:] = sn
        mixed = _dot(an, w_ref[:ATTN_W, :]) + _dot(sn, w_ref[ATTN_W:, :])
        mixed_ref[...] = mixed
        h1 = x_ref[...] + mixed * _rstd(mixed) * gpm_ref[...]
        h1_ref[...] = h1
        f_ref[...] = (h1 * _rstd(h1) * gpf_ref[...]).astype(BF)

    tok = lambda w: pl.BlockSpec((tm, w), _row)
    vec = lambda w: pl.BlockSpec((1, w), _fixed)
    return pl.pallas_call(
        body, grid=(T // tm,),
        in_specs=[tok(ATTN_W), tok(SGU_W), tok(D_MODEL), vec(ATTN_W), vec(SGU_W),
                  pl.BlockSpec((D_MODEL, D_MODEL), _fixed), vec(D_MODEL), vec(D_MODEL)],
        out_specs=[tok(D_MODEL)] * 4,
        out_shape=[SDS((T, D_MODEL), BF), SDS((T, D_MODEL), F32), SDS((T, D_MODEL), F32), SDS((T, D_MODEL), BF)],
        name="out_proj", compiler_params=_params(("arbitrary",), 32))(attn, sgu, x, g_a, g_s, w_out, g_pm, g_pf)


FF_TILE = 1408


def _gate_up(f, w_g, w_u, tm=512):
    T = f.shape[0]
    tn = FF_TILE

    def body(f_ref, wg_ref, wu_ref, g_ref, u_ref, act_ref):
        fv = f_ref[...]
        g = _dot(fv, wg_ref[...])
        u = _dot(fv, wu_ref[...])
        g_ref[...] = g.astype(BF)
        u_ref[...] = u.astype(BF)
        act_ref[...] = (g * _sigmoid(g) * u).astype(BF)

    wspec = pl.BlockSpec((D_MODEL, tn), lambda j, i: (0, j))
    ospec = pl.BlockSpec((tm, tn), lambda j, i: (i, j))
    return pl.pallas_call(
        body, grid=(D_FF // tn, T // tm),
        in_specs=[pl.BlockSpec((tm, D_MODEL), lambda j, i: (i, 0)), wspec, wspec],
        out_specs=[ospec] * 3, out_shape=[SDS((T, D_FF), BF)] * 3,
        name="gate_up", compiler_params=_params(("arbitrary", "arbitrary"), 40))(f, w_g, w_u)


def _down_proj(act, w_down, h1, g_pff, tm=256):
    T = act.shape[0]

    def body(a_ref, w_ref, h1_ref, g_ref, y_ref, h2_ref):
        y = _dot(a_ref[...], w_ref[...])
        y_ref[...] = y
        h2_ref[...] = h1_ref[...] + y * _rstd(y) * g_ref[...]

    return pl.pallas_call(
        body, grid=(T // tm,),
        in_specs=[pl.BlockSpec((tm, D_FF), _row), pl.BlockSpec((D_FF, D_MODEL), _fixed),
                  pl.BlockSpec((tm, D_MODEL), _row), pl.BlockSpec((1, D_MODEL), _fixed)],
        out_specs=[pl.BlockSpec((tm, D_MODEL), _row)] * 2,
        out_shape=[SDS((T, D_MODEL), F32)] * 2,
        name="down_proj", compiler_params=_params(("arbitrary",), 40))(act, w_down, h1, g_pff)


def _pe_and_loss(h2, p, target, w_peg, b_peg, w_pep, tm=256):
    T = h2.shape[0]

    def body(h2_ref, p_ref, t_ref, wg_ref, b_ref, wp_ref,
             dh3_ref, dpp_ref, dpre_ref, h2b_ref, pb_ref, loss_ref, db_ref):
        _acc_init(pl.program_id(0), loss_ref, db_ref)
        h2v = h2_ref[...]
        h2b = h2v.astype(BF)
        pb = p_ref[...].astype(BF)
        h2b_ref[...] = h2b
        pb_ref[...] = pb
        gate = _sigmoid(_dot(h2b, wg_ref[...]) + b_ref[...])
        pp = _dot(pb, wp_ref[...])
        diff = h2v + gate * pp - t_ref[...]
        loss_ref[...] += _colsum(diff * diff)
        dh3 = diff * (1.0 / D_MODEL)
        dh3_ref[...] = dh3
        dpp_ref[...] = (dh3 * gate).astype(BF)
        dpre = dh3 * pp * (gate * (1.0 - gate))
        dpre_ref[...] = dpre.astype(BF)
        db_ref[...] += _colsum(dpre)

    tok = lambda w: pl.BlockSpec((tm, w), _row)
    vec = pl.BlockSpec((1, D_MODEL), _fixed)
    return pl.pallas_call(
        body, grid=(T // tm,),
        in_specs=[tok(D_MODEL), tok(PLE_DIM), tok(D_MODEL), pl.BlockSpec((D_MODEL, D_MODEL), _fixed), vec,
                  pl.BlockSpec((PLE_DIM, D_MODEL), _fixed)],
        out_specs=[tok(D_MODEL), tok(D_MODEL), tok(D_MODEL), tok(D_MODEL), tok(PLE_DIM), vec, vec],
        out_shape=[SDS((T, D_MODEL), F32), SDS((T, D_MODEL), BF), SDS((T, D_MODEL), BF), SDS((T, D_MODEL), BF),
                   SDS((T, PLE_DIM), BF), SDS((1, D_MODEL), F32), SDS((1, D_MODEL), F32)],
        name="pe_and_loss", compiler_params=_params(("arbitrary",), 32))(h2, p, target, w_peg, b_peg, w_pep)


def _weight_grad(a, dy, name, tk=512):
    T, ka = a.shape
    n = dy.shape[1]
    tka = 1408 if ka == D_FF else min(ka, 1024)
    tn = FF_TILE if n == D_FF else min(n, 1024)

    def body(a_ref, dy_ref, out_ref):
        _acc_init(pl.program_id(2), out_ref)
        out_ref[...] += _dot_tn(a_ref[...].astype(BF), dy_ref[...].astype(BF))

    return pl.pallas_call(
        body, grid=(ka // tka, n // tn, T // tk),
        in_specs=[pl.BlockSpec((tk, tka), lambda i, j, k: (k, i)), pl.BlockSpec((tk, tn), lambda i, j, k: (k, j))],
        out_specs=pl.BlockSpec((tka, tn), lambda i, j, k: (i, j)),
        out_shape=SDS((ka, n), F32),
        name="grad_" + name, compiler_params=_params(("arbitrary",) * 3, 40))(a, dy)


def _pe_bwd(dpre, w_peg_t, dh3, y, g_pff, tm=256):
    T = y.shape[0]

    def body(dp_ref, w_ref, dh3_ref, y_ref, g_ref, dh2_ref, dy_ref, dg_ref):
        _acc_init(pl.program_id(0), dg_ref)
        dh2 = dh3_ref[...] + _dot(dp_ref[...], w_ref[...])
        dh2_ref[...] = dh2
        dy, dg = _rms_bwd(dh2, y_ref[...], g_ref[...])
        dy_ref[...] = dy.astype(BF)
        dg_ref[...] += _colsum(dg)

    tok = pl.BlockSpec((tm, D_MODEL), _row)
    vec = pl.BlockSpec((1, D_MODEL), _fixed)
    return pl.pallas_call(
        body, grid=(T // tm,),
        in_specs=[tok, pl.BlockSpec((D_MODEL, D_MODEL), _fixed), tok, tok, vec],
        out_specs=[tok, tok, vec],
        out_shape=[SDS((T, D_MODEL), F32), SDS((T, D_MODEL), BF), SDS((1, D_MODEL), F32)],
        name="pe_bwd", compiler_params=_params(("arbitrary",), 32))(dpre, w_peg_t, dh3, y, g_pff)


def _down_bwd(dy, w_down_t, g, u, tm=512):
    T = dy.shape[0]
    tn = FF_TILE

    def body(dy_ref, w_ref, g_ref, u_ref, dg_ref, du_ref):
        dact = _dot(dy_ref[...], w_ref[...])
        gv = g_ref[...].astype(F32)
        uv = u_ref[...].astype(F32)
        s = _sigmoid(gv)
        dg_ref[...] = (dact * uv * (s * (1.0 + gv * (1.0 - s)))).astype(BF)
        du_ref[...] = (dact * (gv * s)).astype(BF)

    tile = pl.BlockSpec((tm, tn), lambda j, i: (i, j))
    return pl.pallas_call(
        body, grid=(D_FF // tn, T // tm),
        in_specs=[pl.BlockSpec((tm, D_MODEL), lambda j, i: (i, 0)), pl.BlockSpec((D_MODEL, tn), lambda j, i: (0, j)),
                  tile, tile],
        out_specs=[tile, tile], out_shape=[SDS((T, D_FF), BF)] * 2,
        name="down_bwd", compiler_params=_params(("arbitrary", "arbitrary"), 40))(dy, w_down_t, g, u)


def _ffn_in_bwd(dg, du, w_g_t, w_u_t, h1, dh2, mixed, g_pf, g_pm, tm=256):
    T = h1.shape[0]

    def body(dg_ref, du_ref, wg_ref, wu_ref, h1_ref, dh2_ref, mx_ref, gpf_ref, gpm_ref,
             dh1_ref, dmx_ref, dgpf_ref, dgpm_ref):
        _acc_init(pl.program_id(0), dgpf_ref, dgpm_ref)
        df = _dot(dg_ref[...], wg_ref[...]) + _dot(du_ref[...], wu_ref[...])
        dx, dgf = _rms_bwd(df, h1_ref[...], gpf_ref[...])
        dh1 = dh2_ref[...] + dx
        dh1_ref[...] = dh1
        dmx, dgm = _rms_bwd(dh1, mx_ref[...], gpm_ref[...])
        dmx_ref[...] = dmx.astype(BF)
        dgpf_ref[...] += _colsum(dgf)
        dgpm_ref[...] += _colsum(dgm)

    tok = lambda w: pl.BlockSpec((tm, w), _row)
    vec = pl.BlockSpec((1, D_MODEL), _fixed)
    wsp = pl.BlockSpec((D_FF, D_MODEL), _fixed)
    return pl.pallas_call(
        body, grid=(T // tm,),
        in_specs=[tok(D_FF), tok(D_FF), wsp, wsp, tok(D_MODEL), tok(D_MODEL), tok(D_MODEL), vec, vec],
        out_specs=[tok(D_MODEL), tok(D_MODEL), vec, vec],
        out_shape=[SDS((T, D_MODEL), F32), SDS((T, D_MODEL), BF), SDS((1, D_MODEL), F32), SDS((1, D_MODEL), F32)],
        name="ffn_in_bwd", compiler_params=_params(("arbitrary",), 48))(dg, du, w_g_t, w_u_t, h1, dh2, mixed, g_pf, g_pm)


def _out_bwd(dmx, w_out_t, attn, sgu, g_a, g_s, tm=256):
    T = attn.shape[0]

    def body(dm_ref, w_ref, a_ref, s_ref, ga_ref, gs_ref, da_ref, ds_ref, dga_ref, dgs_ref):
        _acc_init(pl.program_id(0), dga_ref, dgs_ref)
        dgr = _dot(dm_ref[...], w_ref[...])
        da, dga = _rms_bwd(dgr[:, :ATTN_W], a_ref[...], ga_ref[...])
        ds, dgs = _rms_bwd(dgr[:, ATTN_W:], s_ref[...], gs_ref[...])
        da_ref[...] = da
        ds_ref[...] = ds
        dga_ref[...] += _colsum(dga)
        dgs_ref[...] += _colsum(dgs)

    tok = lambda w: pl.BlockSpec((tm, w), _row)
    vec = lambda w: pl.BlockSpec((1, w), _fixed)
    return pl.pallas_call(
        body, grid=(T // tm,),
        in_specs=[tok(D_MODEL), pl.BlockSpec((D_MODEL, D_MODEL), _fixed), tok(ATTN_W), tok(SGU_W), vec(ATTN_W), vec(SGU_W)],
        out_specs=[tok(ATTN_W), tok(SGU_W), vec(ATTN_W), vec(SGU_W)],
        out_shape=[SDS((T, ATTN_W), F32), SDS((T, SGU_W), F32), SDS((1, ATTN_W), F32), SDS((1, SGU_W), F32)],
        name="out_bwd", compiler_params=_params(("arbitrary",), 32))(dmx, w_out_t, attn, sgu, g_a, g_s)


def _sgu_bwd(proj, dsgu, ln_g, ln_b, w_s, b_st, tm=512):
    T = proj.shape[0]

    def body(u_ref, z_ref, ds_ref, g_ref, b_ref, w_ref, bs_ref,
             du_ref, dz_ref, dw_ref, dbs_ref, dlg_ref, dlb_ref, dbacc_ref):
        step = pl.program_id(0)
        _acc_init(step, dw_ref, dbs_ref, dlg_ref, dlb_ref, dbacc_ref)
        lng, lnb = g_ref[...], b_ref[...]
        for g in range(N_GROUPS):
            wm = _causal(w_ref[g]).astype(BF)
            cols = slice(g * GROUP_DIM, (g + 1) * GROUP_DIM)
            for c in range(tm // CHUNK):
                rows = slice(c * CHUNK, (c + 1) * CHUNK)
                zv, uv, dout = z_ref[rows, cols], u_ref[rows, cols], ds_ref[rows, cols]
                zn, xhat, rs, tz = _sgu_norm(zv, lng, lnb)
                znb = zn.astype(BF)
                mixed = _dot(wm, znb) + bs_ref[:, g:g + 1]
                gu, tu = _gelu(uv)
                du_ref[rows, cols] = (dout * mixed * _gelu_grad(uv, tu)).astype(BF)
                dmix = dout * gu
                dmb = dmix.astype(BF)
                dw_ref[g] += _causal(_dot_nt(dmb, znb))
                dbacc_ref[g] += dmix
                dzn = _dot_tn(wm, dmb)
                dlg_ref[...] += _colsum(dzn * xhat)
                dlb_ref[...] += _colsum(dzn)
                dxh = dzn * lng
                dgz = rs * (dxh - jnp.mean(dxh, axis=-1, keepdims=True)
                            - xhat * jnp.mean(dxh * xhat, axis=-1, keepdims=True))
                dz_ref[rows, cols] = (dgz * _gelu_grad(zv, tz)).astype(BF)

        @pl.when(step == pl.num_programs(0) - 1)
        def _():
            lane = lax.broadcasted_iota(jnp.int32, (CHUNK, LANES), 1)
            acc = jnp.zeros((CHUNK, LANES), F32)
            for g in range(N_GROUPS):
                acc = jnp.where(lane == g, jnp.sum(dbacc_ref[g], axis=-1, keepdims=True), acc)
            dbs_ref[...] = acc

    tok = pl.BlockSpec((tm, SGU_W), _row)
    vec = pl.BlockSpec((1, GROUP_DIM), _fixed)
    wsp = pl.BlockSpec((N_GROUPS, CHUNK, CHUNK), lambda i: (0, 0, 0))
    sq = pl.BlockSpec((CHUNK, LANES), _fixed)
    return pl.pallas_call(
        body, grid=(T // tm,),
        in_specs=[pl.BlockSpec((tm, SGU_W), lambda i: (i, 3)), pl.BlockSpec((tm, SGU_W), lambda i: (i, 4)), tok,
                  vec, vec, wsp, sq],
        out_specs=[tok, tok, wsp, sq, vec, vec],
        out_shape=[SDS((T, SGU_W), BF), SDS((T, SGU_W), BF), SDS((N_GROUPS, CHUNK, CHUNK), F32),
                   SDS((CHUNK, LANES), F32), SDS((1, GROUP_DIM), F32), SDS((1, GROUP_DIM), F32)],
        scratch_shapes=[pltpu.VMEM((N_GROUPS, CHUNK, LANES), F32)],
        name="sgu_bwd", compiler_params=_params(("arbitrary",), 32))(proj, proj, dsgu, ln_g, ln_b, w_s, b_st)


def _attn_bwd(proj, do, o, lse, slopes):
    T = proj.shape[0]
    nblk = T // QBLK
    big = dict(pipeline_mode=pl.Buffered(1))

    def body(q_ref, k_ref, v_ref, do_ref, o_ref, l_ref, sl_ref, dq_ref, dk_ref, dv_ref, bias_ref):
        _attn_bias(sl_ref, bias_ref)
        lo = lax.broadcasted_iota(jnp.int32, (1, LANES), 1) < HEAD_DIM
        scale = HEAD_DIM ** -0.5
        dq_ref[...] = jnp.zeros_like(dq_ref)
        dk_ref[...] = jnp.zeros_like(dk_ref)
        dv_ref[...] = jnp.zeros_like(dv_ref)

        for di, d in enumerate(DILATIONS):
            group = _attn_group(nblk, d)

            def step(i, carry, d=d, di=di, group=group):
                start, pstart, first = _attn_group_index(i, nblk, d, group)
                rows, prows = _attn_rows(start, d, group), _attn_rows(pstart, d)
                q = q_ref[rows, :] * scale
                k = jnp.concatenate([k_ref[prows, :], k_ref[rows, :]], axis=0).astype(BF)
                v = jnp.concatenate([v_ref[prows, :], v_ref[rows, :]], axis=0).astype(BF)
                dov = do_ref[rows, :]
                prod = dov * o_ref[rows, :]
                lse_g = l_ref[rows, :]
                masks = [lo if j == 0 else ~lo for j in range(2)]
                qm = [jnp.where(masks[j], q, 0.0).astype(BF) for j in range(2)]
                dom = [jnp.where(masks[j], dov, 0.0).astype(BF) for j in range(2)]
                for b in range(group):
                    qb = slice(b * QBLK, (b + 1) * QBLK)
                    kb = slice(b * QBLK, (b + 2) * QBLK)
                    dq_parts = []
                    dk_b = dv_b = None
                    for j in range(2):
                        base = (di * 2 + j) * 2
                        bias = bias_ref[base + first.astype(jnp.int32)] if b == 0 else bias_ref[base]
                        delta = jnp.sum(jnp.where(masks[j], prod[qb], 0.0), axis=-1, keepdims=True)
                        lj = lse_g[qb, j * HEAD_DIM:j * HEAD_DIM + 1]
                        pr = jnp.exp(_dot_nt(qm[j][qb], k[kb]) + bias - lj)
                        ds = (pr * (_dot_nt(dom[j][qb], v[kb]) - delta)).astype(BF)
                        dq_parts.append(_dot(ds, k[kb]))
                        t_k, t_v = _dot_tn(ds, qm[j][qb]), _dot_tn(pr.astype(BF), dom[j][qb])
                        dk_b = t_k if j == 0 else dk_b + t_k
                        dv_b = t_v if j == 0 else dv_b + t_v
                    own = _attn_rows(start + b * (d * QBLK), d)
                    dq_ref[own, :] += jnp.where(lo, dq_parts[0], dq_parts[1]) * scale
                    if b == 0:
                        dk_ref[prows, :] += dk_b[:QBLK]
                        dv_ref[prows, :] += dv_b[:QBLK]
                        dk_ref[own, :] += dk_b[QBLK:]
                        dv_ref[own, :] += dv_b[QBLK:]
                    else:
                        two = _attn_rows(start + (b - 1) * (d * QBLK), d, 2)
                        dk_ref[two, :] += dk_b
                        dv_ref[two, :] += dv_b
                return carry

            lax.fori_loop(0, nblk // group, step, 0)

    col = lambda base: pl.BlockSpec((T, LANES), lambda h: (0, base + h), **big)
    out = pl.BlockSpec((T, LANES), lambda h: (0, h))
    return pl.pallas_call(
        body, grid=(4,),
        in_specs=[col(0), col(4), col(8), col(0), col(0), col(0), pl.BlockSpec((1, 8, LANES), lambda h: (h, 0, 0))],
        out_specs=[out, out, out],
        out_shape=[SDS((T, ATTN_W), F32)] * 3,
        scratch_shapes=[pltpu.VMEM((12, QBLK, 2 * QBLK), F32)],
        name="attn_bwd", compiler_params=_params(("arbitrary",), 60))(proj, proj, proj, do, o, lse, slopes)


def _in_bwd(dparts, w_in_t, x, dh1, g1, tm=256):
    T = x.shape[0]
    n = len(dparts)
    w = ATTN_W

    def body(*refs):
        d_refs, (w_ref, x_ref, dh1_ref, g_ref, dx_ref, dg_ref) = refs[:n], refs[n:]
        _acc_init(pl.program_id(0), dg_ref)
        da = None
        for i, r in enumerate(d_refs):
            t = _dot(r[...].astype(BF), w_ref[i * w:(i + 1) * w, :])
            da = t if da is None else da + t
        dx, dg = _rms_bwd(da, x_ref[...], g_ref[...])
        dx_ref[...] = dh1_ref[...] + dx
        dg_ref[...] += _colsum(dg)

    tok = lambda c: pl.BlockSpec((tm, c), _row)
    vec = pl.BlockSpec((1, D_MODEL), _fixed)
    return pl.pallas_call(
        body, grid=(T // tm,),
        in_specs=[tok(w)] * n + [pl.BlockSpec((PROJ, D_MODEL), _fixed), tok(D_MODEL), tok(D_MODEL), vec],
        out_specs=[tok(D_MODEL), vec],
        out_shape=[SDS((T, D_MODEL), F32), SDS((1, D_MODEL), F32)],
        name="in_bwd", compiler_params=_params(("arbitrary",), 40))(*dparts, w_in_t, x, dh1, g1)


def _local_step(x, p, target, big, small):
    slopes = jnp.broadcast_to((2.0 ** -(jnp.arange(8, dtype=F32) + 1.0)).reshape(4, 2, 1), (4, 2, LANES))
    slopes = jnp.concatenate([slopes, jnp.zeros((4, 6, LANES), F32)], axis=1)
    b_st = jnp.pad(small["b_spatial"].T, ((0, 0), (0, LANES - N_GROUPS)))

    proj, a = _in_proj(x, small["ln_pre_mix"], big["w_in"])
    attn, lse = _attn_fwd(proj, slopes)
    sgu = _sgu_fwd(proj, small["sgu_ln_g"], small["sgu_ln_b"], small["w_spatial"], b_st)
    groups, mixed, h1, f = _out_proj(attn, sgu, x, small["attn_out_norm"], small["sgu_out_norm"], big["w_out"],
                                     small["ln_post_mix"], small["ln_pre_ffn"])
    g, u, act = _gate_up(f, big["w_g"], big["w_u"])
    y, h2 = _down_proj(act, big["w_down"], h1, small["ln_post_ffn"])
    dh3, dpp, dpre, h2b, pb, loss_cols, db_peg = _pe_and_loss(h2, p, target, big["w_pe_gate"], small["b_pe_gate"],
                                                              big["w_pe_proj"])
    loss = 0.5 * jnp.sum(loss_cols) * (1.0 / D_MODEL)

    gw = {}
    gw["w_pe_proj"] = _weight_grad(pb, dpp, "w_pe_proj")
    gw["w_pe_gate"] = _weight_grad(h2b, dpre, "w_pe_gate")
    dh2, dy, d_pff = _pe_bwd(dpre, big["w_pe_gate_t"], dh3, y, small["ln_post_ffn"])
    gw["w_down"] = _weight_grad(act, dy, "w_down")
    dg, du = _down_bwd(dy, big["w_down_t"], g, u)
    gw["w_gate_up"] = jnp.concatenate([_weight_grad(f, dg, "w_gate"), _weight_grad(f, du, "w_up")], axis=1)
    dh1, dmx, d_pf, d_pm = _ffn_in_bwd(dg, du, big["w_g_t"], big["w_u_t"], h1, dh2, mixed,
                                       small["ln_pre_ffn"], small["ln_post_mix"])
    gw["w_out"] = _weight_grad(groups, dmx, "w_out")
    dattn, dsgu, d_ga, d_gs = _out_bwd(dmx, big["w_out_t"], attn, sgu, small["attn_out_norm"], small["sgu_out_norm"])
    dus, dzs, d_ws, d_bst, d_lg, d_lb = _sgu_bwd(proj, dsgu, small["sgu_ln_g"], small["sgu_ln_b"],
                                                 small["w_spatial"], b_st)
    dq, dk, dv = _attn_bwd(proj, dattn, attn, lse, slopes)
    dparts = [dq, dk, dv, dus, dzs]
    gw["w_in"] = jnp.concatenate([_weight_grad(a, t, "w_in_%d" % i) for i, t in enumerate(dparts)], axis=1)
    grad_x, d_g1 = _in_bwd(dparts, big["w_in_t"], x, dh1, small["ln_pre_mix"])

    gs = dict(ln_pre_mix=d_g1, sgu_ln_g=d_lg, sgu_ln_b=d_lb, w_spatial=d_ws, b_spatial=d_bst[:, :N_GROUPS].T,
              attn_out_norm=d_ga, sgu_out_norm=d_gs, ln_post_mix=d_pm, ln_pre_ffn=d_pf, ln_post_ffn=d_pff,
              b_pe_gate=db_peg)
    return loss, grad_x, gw, gs


MESH_ID = pl.DeviceIdType.MESH


def _all_gather(shard):
    rows, width = shard.shape

    def body(x_ref, out_ref, send_sems, recv_sems, local_sem):
        x, y, c = lax.axis_index("x"), lax.axis_index("y"), lax.axis_index("c")
        me, sibling = (x, y, c), (x, y, 1 - c)
        chips = [(1 - x, y), (x, 1 - y), (1 - x, 1 - y)]

        def slot(px, py, pc):
            return out_ref.at[4 * px + 2 * py + pc]

        def copy(k, block, to, src=None):
            return pltpu.make_async_remote_copy(
                src_ref=slot(*block) if src is None else src, dst_ref=slot(*block),
                send_sem=send_sems.at[k], recv_sem=recv_sems.at[k], device_id=to, device_id_type=MESH_ID)

        mine = pltpu.make_async_copy(x_ref, slot(*me), local_sem)
        mine.start()
        first = [copy(0, me, sibling, src=x_ref)]
        first += [copy(1 + j, me, (*chip, c), src=x_ref) for j, chip in enumerate(chips)]
        for cp in first:
            cp.start()
        passed = [copy(4 + j, (*chip, c), sibling) for j, chip in enumerate(chips)]
        for j, chip in enumerate(chips):
            copy(1 + j, (*chip, c), me).wait_recv()
            passed[j].start()
        copy(0, sibling, me).wait_recv()
        for j, chip in enumerate(chips):
            copy(4 + j, (*chip, 1 - c), me).wait_recv()
        for cp in first + passed:
            cp.wait_send()
        mine.wait()

    return pl.pallas_call(
        body, out_shape=SDS((N_DEV, rows, width), shard.dtype),
        in_specs=[pl.BlockSpec(memory_space=pl.ANY)], out_specs=pl.BlockSpec(memory_space=pl.ANY),
        scratch_shapes=[pltpu.SemaphoreType.DMA((7,)), pltpu.SemaphoreType.DMA((7,)), pltpu.SemaphoreType.DMA],
        name="weight_all_gather")(shard)


def _grad_exchange(g_big, g_small):
    _, rows, width = g_big.shape
    srows = g_small.shape[0]
    flips = [(dx, dy, dc) for dx in (0, 1) for dy in (0, 1) for dc in (0, 1)][1:]

    def body(gb_ref, gs_ref, rb_ref, rs_ref, send_sems, recv_sems, local_sems):
        x, y, c = lax.axis_index("x"), lax.axis_index("y"), lax.axis_index("c")
        me = 4 * x + 2 * y + c
        own_b = pltpu.make_async_copy(gb_ref.at[me], rb_ref.at[me], local_sems.at[0])
        own_s = pltpu.make_async_copy(gs_ref, rs_ref.at[me], local_sems.at[1])
        own_b.start()
        own_s.start()
        copies = []
        for k, (dx, dy, dc) in enumerate(flips):
            px, py, pc = (1 - x if dx else x), (1 - y if dy else y), (1 - c if dc else c)
            peer = 4 * px + 2 * py + pc
            copies.append((
                pltpu.make_async_remote_copy(src_ref=gb_ref.at[peer], dst_ref=rb_ref.at[me], send_sem=send_sems.at[k],
                                             recv_sem=recv_sems.at[k], device_id=(px, py, pc), device_id_type=MESH_ID),
                pltpu.make_async_remote_copy(src_ref=gb_ref.at[peer], dst_ref=rb_ref.at[peer], send_sem=send_sems.at[k],
                                             recv_sem=recv_sems.at[k], device_id=(px, py, pc), device_id_type=MESH_ID)))
            copies.append((
                pltpu.make_async_remote_copy(src_ref=gs_ref, dst_ref=rs_ref.at[me], send_sem=send_sems.at[7 + k],
                                             recv_sem=recv_sems.at[7 + k], device_id=(px, py, pc), device_id_type=MESH_ID),
                pltpu.make_async_remote_copy(src_ref=gs_ref, dst_ref=rs_ref.at[peer], send_sem=send_sems.at[7 + k],
                                             recv_sem=recv_sems.at[7 + k], device_id=(px, py, pc), device_id_type=MESH_ID)))
        for send, _ in copies:
            send.start()
        for _, arrival in copies:
            arrival.wait_recv()
        for send, _ in copies:
            send.wait_send()
        own_b.wait()
        own_s.wait()

    hbm = pl.BlockSpec(memory_space=pl.ANY)
    return pl.pallas_call(
        body, out_shape=[SDS((N_DEV, rows, width), g_big.dtype), SDS((N_DEV, srows, width), g_small.dtype)],
        in_specs=[hbm, hbm], out_specs=[hbm, hbm],
        scratch_shapes=[pltpu.SemaphoreType.DMA((14,)), pltpu.SemaphoreType.DMA((14,)), pltpu.SemaphoreType.DMA((2,))],
        name="grad_exchange")(g_big, g_small)


def _adamw(parts, w, m, v, tr):
    rows, width = w.shape
    c1 = 1.0 - ADAM_B1 ** ADAM_STEP
    c2 = 1.0 - ADAM_B2 ** ADAM_STEP

    def body(p_ref, w_ref, m_ref, v_ref, g_ref, d_ref, nm_ref, nv_ref):
        g = p_ref[0]
        for s in range(1, N_DEV):
            g = g + p_ref[s]
        nm = ADAM_B1 * m_ref[...] + (1.0 - ADAM_B1) * g
        nv = ADAM_B2 * v_ref[...] + (1.0 - ADAM_B2) * (g * g)
        g_ref[...] = g
        nm_ref[...] = nm
        nv_ref[...] = nv
        d_ref[...] = -ADAM_LR * ((nm / c1) / (jnp.sqrt(nv / c2) + ADAM_EPS) + ADAM_WD * w_ref[...])

    blk = pl.BlockSpec((tr, width), _row)
    return pl.pallas_call(
        body, grid=(rows // tr,),
        in_specs=[pl.BlockSpec((N_DEV, tr, width), lambda i: (0, i, 0)), blk, blk, blk],
        out_specs=[blk] * 4, out_shape=[SDS((rows, width), F32)] * 4,
        name="adamw", compiler_params=_params(("arbitrary",), 32))(parts, w, m, v)


SMALL = ("w_spatial", "ln_pre_mix", "ln_post_mix", "ln_pre_ffn", "ln_post_ffn", "b_pe_gate",
         "attn_out_norm", "sgu_out_norm", "b_spatial", "sgu_ln_g", "sgu_ln_b")


def _pack_small(t):
    rows = [t["w_spatial"].reshape(-1, D_MODEL)]
    for name in SMALL[1:]:
        flat = t[name].reshape(1, -1)
        rows.append(jnp.pad(flat, ((0, 0), (0, D_MODEL - flat.shape[1]))))
    used = sum(r.shape[0] for r in rows)
    rows.append(jnp.zeros((SMALL_ROWS - used, D_MODEL), F32))
    return jnp.concatenate(rows, axis=0)


def _unpack_small(packed, shapes):
    out = {"w_spatial": packed[:64].reshape(shapes["w_spatial"])}
    for i, name in enumerate(SMALL[1:]):
        size = math.prod(shapes[name])
        out[name] = packed[64 + i, :size].reshape(shapes[name])
    return out


def _pack_big_shards(t):
    return jnp.concatenate([t[name].reshape(-1, D_MODEL) for name in BIG], axis=0)


def _unpack_big_shards(packed, shapes):
    out, off = {}, 0
    for name, r in zip(BIG, BIG_ROWS):
        out[name] = packed[off:off + r].reshape(shapes[name])
        off += r
    return out


def _split_cols(full, n):
    k = full.shape[0]
    return full.reshape(k, N_DEV, n).transpose(1, 0, 2).reshape(N_DEV, -1, D_MODEL)


def _join_cols(slabs, k):
    n = slabs.shape[1] * D_MODEL // k
    return slabs.reshape(N_DEV, k, n).transpose(1, 0, 2).reshape(k, N_DEV * n)


def _full_weights(gathered):
    parts, off = {}, 0
    for name, r in zip(BIG, BIG_ROWS):
        parts[name] = gathered[:, off:off + r]
        off += r
    w_in = _join_cols(parts["w_in"], D_MODEL)
    w_gu = _join_cols(parts["w_gate_up"], D_MODEL)
    big = dict(
        w_in=w_in, w_out=parts["w_out"].reshape(D_MODEL, D_MODEL), w_g=w_gu[:, :D_FF], w_u=w_gu[:, D_FF:],
        w_down=parts["w_down"].reshape(D_FF, D_MODEL), w_pe_gate=parts["w_pe_gate"].reshape(D_MODEL, D_MODEL),
        w_pe_proj=_join_cols(parts["w_pe_proj"], PLE_DIM))
    for name in ("w_in", "w_out", "w_g", "w_u", "w_down", "w_pe_gate"):
        big[name + "_t"] = big[name].T
    return big


def _grad_slabs(gw):
    return jnp.concatenate([
        _split_cols(gw["w_in"], PROJ // N_DEV), gw["w_out"].reshape(N_DEV, -1, D_MODEL),
        _split_cols(gw["w_gate_up"], 2 * D_FF // N_DEV), gw["w_down"].reshape(N_DEV, -1, D_MODEL),
        gw["w_pe_gate"].reshape(N_DEV, -1, D_MODEL), _split_cols(gw["w_pe_proj"], D_MODEL // N_DEV)], axis=1)


WEIGHTS = ("ln_pre_mix", "w_in", "sgu_ln_g", "sgu_ln_b", "w_spatial", "b_spatial", "attn_out_norm", "sgu_out_norm",
           "w_out", "ln_post_mix", "ln_pre_ffn", "w_gate_up", "w_down", "ln_post_ffn", "w_pe_gate", "b_pe_gate",
           "w_pe_proj")


def kernel(x, p, ln_pre_mix, w_in, sgu_ln_g, sgu_ln_b, w_spatial, b_spatial, attn_out_norm, sgu_out_norm, w_out, ln_post_mix, ln_pre_ffn, w_gate_up, w_down, ln_post_ffn, w_pe_gate, b_pe_gate, w_pe_proj, loss_target, m_ln_pre_mix, m_w_in, m_sgu_ln_g, m_sgu_ln_b, m_w_spatial, m_b_spatial, m_attn_out_norm, m_sgu_out_norm, m_w_out, m_ln_post_mix, m_ln_pre_ffn, m_w_gate_up, m_w_down, m_ln_post_ffn, m_w_pe_gate, m_b_pe_gate, m_w_pe_proj, v_ln_pre_mix, v_w_in, v_sgu_ln_g, v_sgu_ln_b, v_w_spatial, v_b_spatial, v_attn_out_norm, v_sgu_out_norm, v_w_out, v_ln_post_mix, v_ln_pre_ffn, v_w_gate_up, v_w_down, v_ln_post_ffn, v_w_pe_gate, v_b_pe_gate, v_w_pe_proj):
    given = dict(locals())
    w = {n: given[n] for n in WEIGHTS}
    m = {n: given["m_" + n] for n in WEIGHTS}
    v = {n: given["v_" + n] for n in WEIGHTS}
    shapes = {n: w[n].shape for n in WEIGHTS}

    gathered = _all_gather(_pack_big_shards(w).astype(BF))
    big = _full_weights(gathered)
    small = {n: w[n][0] for n in SMALL}
    small = {n: (a.reshape(1, -1) if a.ndim == 1 else a) for n, a in small.items()}

    loss, grad_x, gw, gs = _local_step(x[0], p[0, 0], loss_target[0], big, small)
    loss = lax.psum(loss, ("x", "y", "c"))

    arrived_big, arrived_small = _grad_exchange(_grad_slabs(gw), _pack_small(gs))
    out_big = _adamw(arrived_big, _pack_big_shards(w), _pack_big_shards(m), _pack_big_shards(v), tr=128)
    out_small = _adamw(arrived_small, _pack_small(w), _pack_small(m), _pack_small(v), tr=SMALL_ROWS)

    results = []
    for packed_b, packed_s in zip(out_big, out_small):
        merged = {**_unpack_big_shards(packed_b, shapes), **_unpack_small(packed_s, shapes)}
        results.append([merged[n] for n in WEIGHTS])
    grads, deltas, new_m, new_v = results
    return (loss, grad_x[None], *grads, *deltas, *new_m, *new_v)
```

```python
import math

import jax
import jax.numpy as jnp
from jax import lax
from jax.experimental import pallas as pl
from jax.experimental.pallas import tpu as pltpu

F32 = jnp.float32
BF = jnp.bfloat16
SDS = jax.ShapeDtypeStruct

D_MODEL = 1024
ATTN_W = 512
SGU_W = 512
HEAD_DIM = 64
N_GROUPS = 4
GROUP_DIM = 128
CHUNK = 128
D_FF = 2816
PLE_DIM = 256
PROJ = 3 * ATTN_W + 2 * SGU_W
DILATIONS = (1, 4, 16)
QBLK = 128
EPS = 1e-6
NEG = -1e30
N_DEV = 8
LANES = 128

ADAM_LR = 0.001
ADAM_B1 = 0.9
ADAM_B2 = 0.999
ADAM_EPS = 1e-08
ADAM_WD = 0.01
ADAM_STEP = 10

SMALL_ROWS = 80
MIB = 2 ** 20
MESH_ID = pl.DeviceIdType.MESH
HBM = pl.BlockSpec(memory_space=pl.ANY)


def _params(sem, vmem_mib):
    return pltpu.CompilerParams(dimension_semantics=sem, vmem_limit_bytes=vmem_mib * MIB)


def _dot(a, b):
    return jnp.dot(a, b, preferred_element_type=F32)


def _dot_nt(a, b):
    return lax.dot_general(a, b, (((1,), (1,)), ((), ())), preferred_element_type=F32)


def _dot_tn(a, b):
    return lax.dot_general(a, b, (((0,), (0,)), ((), ())), preferred_element_type=F32)


def _rstd(x):
    return lax.rsqrt(jnp.mean(x * x, axis=-1, keepdims=True) + EPS)


def _rms_bwd(dy, x, g):
    r = _rstd(x)
    n = x * r
    dn = dy * g
    dx = r * (dn - n * jnp.mean(dn * n, axis=-1, keepdims=True))
    return dx, dy * n


def _colsum(v):
    return jnp.sum(v, axis=0, keepdims=True)


_G0 = math.sqrt(2.0 / math.pi)
_G1 = 0.044715


def _gelu(x):
    t = jnp.tanh(_G0 * (x + _G1 * x * x * x))
    return 0.5 * x * (1.0 + t), t


def _gelu_grad(x, t):
    return 0.5 * (1.0 + t) + 0.5 * x * (1.0 - t * t) * (_G0 * (1.0 + 3.0 * _G1 * x * x))


def _sigmoid(x):
    return 1.0 / (1.0 + jnp.exp(-x))


def _row(i):
    return (i, 0)


def _fixed(i):
    return (0, 0)


def _acc_init(step, *refs):
    @pl.when(step == 0)
    def _():
        for r in refs:
            r[...] = jnp.zeros_like(r)


FLIPS = [(dx, dy, dc) for dx in (0, 1) for dy in (0, 1) for dc in (0, 1)][1:]
DMA_SEMS = pltpu.SemaphoreType.DMA


def _mesh_pos():
    return lax.axis_index("x"), lax.axis_index("y"), lax.axis_index("c")


def _remote(src, dst, sems, n, to):
    return pltpu.make_async_remote_copy(src_ref=src, dst_ref=dst, send_sem=sems[0].at[n], recv_sem=sems[1].at[n],
                                        device_id=to, device_id_type=MESH_ID)


class _Scatter:
    def __init__(self, items, sems):
        x, y, c = _mesh_pos()
        me = 4 * x + 2 * y + c
        self.local, self.sends, self.arrivals = [], [], []
        for i, (src, dst, slabbed) in enumerate(items):
            self.local.append(pltpu.make_async_copy(src.at[me] if slabbed else src, dst.at[me], sems[2].at[i]))
            for k, (dx, dy, dc) in enumerate(FLIPS):
                to = (1 - x if dx else x, 1 - y if dy else y, 1 - c if dc else c)
                peer = 4 * to[0] + 2 * to[1] + to[2]
                out = src.at[peer] if slabbed else src
                self.sends.append(_remote(out, dst.at[me], sems, 7 * i + k, to))
                self.arrivals.append(_remote(out, dst.at[peer], sems, 7 * i + k, to))

    def start(self):
        for cp in self.local + self.sends:
            cp.start()

    def wait(self):
        for cp in self.arrivals:
            cp.wait_recv()
        for cp in self.sends:
            cp.wait_send()
        for cp in self.local:
            cp.wait()


def _scatter_sems(n):
    return [DMA_SEMS((7 * n,)), DMA_SEMS((7 * n,)), DMA_SEMS((n,))]


class _Gather:
    def __init__(self, items, sems):
        x, y, c = _mesh_pos()
        me, sibling = (x, y, c), (x, y, 1 - c)
        chips = [(1 - x, y), (x, 1 - y), (1 - x, 1 - y)]
        self.first, self.passed, self.from_chips, self.rest, self.local = [], [], [], [], []
        for i, (src, dst) in enumerate(items):
            def slot(p, dst=dst):
                return dst.at[4 * p[0] + 2 * p[1] + p[2]]

            def copy(k, block, to, own=False, i=i, src=src, slot=slot):
                return _remote(src if own else slot(block), slot(block), sems, 7 * i + k, to)

            self.local.append(pltpu.make_async_copy(src, slot(me), sems[2].at[i]))
            self.first.append(copy(0, me, sibling, own=True))
            self.first += [copy(1 + j, me, (*chip, c), own=True) for j, chip in enumerate(chips)]
            self.passed += [copy(4 + j, (*chip, c), sibling) for j, chip in enumerate(chips)]
            self.from_chips += [copy(1 + j, (*chip, c), me) for j, chip in enumerate(chips)]
            self.rest.append(copy(0, sibling, me))
            self.rest += [copy(4 + j, (*chip, 1 - c), me) for j, chip in enumerate(chips)]

    def start(self):
        for cp in self.local + self.first:
            cp.start()

    def forward(self):
        for arrived, onward in zip(self.from_chips, self.passed):
            arrived.wait_recv()
            onward.start()

    def finish(self):
        for cp in self.rest:
            cp.wait_recv()
        for cp in self.first + self.passed:
            cp.wait_send()
        for cp in self.local:
            cp.wait()


def _all_gather(shard, name):
    def body(x_ref, out_ref, *sems):
        g = _Gather([(x_ref, out_ref)], sems)
        g.start()
        g.forward()
        g.finish()

    return pl.pallas_call(
        body, out_shape=SDS((N_DEV,) + shard.shape, shard.dtype), in_specs=[HBM], out_specs=HBM,
        scratch_shapes=_scatter_sems(1), name=name)(shard)


def _scatter_call(srcs, slabbed, name):
    n = len(srcs)

    def body(*refs):
        sc = _Scatter(list(zip(refs[:n], refs[n:2 * n], slabbed)), refs[2 * n:])
        sc.start()
        sc.wait()

    shapes = [SDS(s.shape if sl else (N_DEV,) + s.shape, s.dtype) for s, sl in zip(srcs, slabbed)]
    return pl.pallas_call(body, out_shape=shapes, in_specs=[HBM] * n, out_specs=[HBM] * n,
                          scratch_shapes=_scatter_sems(n), name=name)(*srcs)


def _in_proj(x, g1, w_in_t, tm=256):
    T = x.shape[0]

    def body(x_ref, g_ref, w_ref, proj_ref, a_ref):
        xv = x_ref[...]
        a = (xv * _rstd(xv) * g_ref[...]).astype(BF)
        a_ref[...] = a
        proj_ref[...] = _dot_nt(a, w_ref[...])

    return pl.pallas_call(
        body, grid=(T // tm,),
        in_specs=[pl.BlockSpec((tm, D_MODEL), _row), pl.BlockSpec((1, D_MODEL), _fixed),
                  pl.BlockSpec((PROJ, D_MODEL), _fixed)],
        out_specs=[pl.BlockSpec((tm, PROJ), _row), pl.BlockSpec((tm, D_MODEL), _row)],
        out_shape=[SDS((T, PROJ), F32), SDS((T, D_MODEL), BF)],
        name="in_proj", compiler_params=_params(("arbitrary",), 40))(x, g1, w_in_t)


ATTN_GROUP = 4


def _attn_bias(sl_ref, bias_ref):
    qi = lax.broadcasted_iota(jnp.int32, (QBLK, QBLK), 0)
    kj = lax.broadcasted_iota(jnp.int32, (QBLK, QBLK), 1)
    step = qi - kj
    for di, d in enumerate(DILATIONS):
        for j in range(2):
            sl = sl_ref[0, j:j + 1, :]
            cur = jnp.where(step >= 0, -sl * (step * d).astype(F32), NEG)
            prev = jnp.where(step <= 0, -sl * ((step + QBLK) * d).astype(F32), NEG)
            base = (di * 2 + j) * 2
            bias_ref[base, :, :QBLK] = prev
            bias_ref[base, :, QBLK:] = cur
            bias_ref[base + 1, :, :QBLK] = jnp.full((QBLK, QBLK), NEG, F32)
            bias_ref[base + 1, :, QBLK:] = cur


def _attn_rows(start, d, blocks=1):
    if d == 1:
        return pl.ds(pl.multiple_of(start, QBLK), blocks * QBLK)
    return pl.ds(start, blocks * QBLK, stride=d)


def _attn_group_index(i, nblk, d, group):
    per = nblk // d // group
    r = i // per
    n0 = (i % per) * group
    start = r + (d * QBLK) * n0
    pstart = jnp.maximum(start - d * QBLK, r)
    return start, pstart, n0 == 0


def _attn_group(nblk, d):
    return min(ATTN_GROUP, nblk // d)


def _attn_fwd(proj, slopes, to_gather):
    T = proj.shape[0]
    nblk = T // QBLK
    big = dict(pipeline_mode=pl.Buffered(1))
    n_g = len(to_gather)

    def body(q_ref, k_ref, v_ref, sl_ref, *rest):
        srcs, (o_ref, m_ref), dsts = rest[:n_g], rest[n_g:n_g + 2], rest[n_g + 2:2 * n_g + 2]
        sems, (l_ref, bias_ref) = rest[2 * n_g + 2:2 * n_g + 5], rest[2 * n_g + 5:]
        h = pl.program_id(0)

        @pl.when(h == 0)
        def _():
            _Gather(list(zip(srcs, dsts)), sems).start()

        @pl.when(h == 2)
        def _():
            _Gather(list(zip(srcs, dsts)), sems).forward()

        _attn_bias(sl_ref, bias_ref)
        lo = lax.broadcasted_iota(jnp.int32, (1, LANES), 1) < HEAD_DIM

        for di, d in enumerate(DILATIONS):
            group = _attn_group(nblk, d)

            def step(i, carry, d=d, di=di, group=group):
                start, pstart, first = _attn_group_index(i, nblk, d, group)
                rows, prows = _attn_rows(start, d, group), _attn_rows(pstart, d)
                q = q_ref[rows, :] * (HEAD_DIM ** -0.5)
                k = jnp.concatenate([k_ref[prows, :], k_ref[rows, :]], axis=0).astype(BF)
                v = jnp.concatenate([v_ref[prows, :], v_ref[rows, :]], axis=0).astype(BF)
                qm = [jnp.where(lo if j == 0 else ~lo, q, 0.0).astype(BF) for j in range(2)]
                for b in range(group):
                    qb = slice(b * QBLK, (b + 1) * QBLK)
                    kb = slice(b * QBLK, (b + 2) * QBLK)
                    parts = []
                    for j in range(2):
                        base = (di * 2 + j) * 2
                        bias = bias_ref[base + first.astype(jnp.int32)] if b == 0 else bias_ref[base]
                        s = _dot_nt(qm[j][qb], k[kb]) + bias
                        m = jnp.max(s, axis=-1, keepdims=True)
                        pr = jnp.exp(s - m)
                        parts.append((m, jnp.sum(pr, axis=-1, keepdims=True), _dot(pr.astype(BF), v[kb])))
                    m_b = jnp.where(lo, parts[0][0], parts[1][0])
                    l_b = jnp.where(lo, parts[0][1], parts[1][1])
                    o_b = jnp.where(lo, parts[0][2], parts[1][2])
                    out = _attn_rows(start + b * (d * QBLK), d)
                    if di == 0:
                        m_ref[out, :] = m_b
                        l_ref[out, :] = l_b
                        o_ref[out, :] = o_b
                    else:
                        m_o = m_ref[out, :]
                        m_n = jnp.maximum(m_o, m_b)
                        wa, wb = jnp.exp(m_o - m_n), jnp.exp(m_b - m_n)
                        m_ref[out, :] = m_n
                        l_ref[out, :] = wa * l_ref[out, :] + wb * l_b
                        o_ref[out, :] = wa * o_ref[out, :] + wb * o_b
                return carry

            lax.fori_loop(0, nblk // group, step, 0)

        def finish(i, carry):
            rows = pl.ds(pl.multiple_of(i * QBLK, QBLK), QBLK)
            l = l_ref[rows, :]
            o_ref[rows, :] = o_ref[rows, :] / l
            m_ref[rows, :] = m_ref[rows, :] + jnp.log(l)
            return carry

        lax.fori_loop(0, nblk, finish, 0)

        @pl.when(h == pl.num_programs(0) - 1)
        def _():
            _Gather(list(zip(srcs, dsts)), sems).finish()

    col = lambda base: pl.BlockSpec((T, LANES), lambda h: (0, base + h), **big)
    tok = pl.BlockSpec((T, LANES), lambda h: (0, h))
    outs = pl.pallas_call(
        body, grid=(4,),
        in_specs=[col(0), col(4), col(8), pl.BlockSpec((1, 8, LANES), lambda h: (h, 0, 0))] + [HBM] * n_g,
        out_specs=[tok, tok] + [HBM] * n_g,
        out_shape=[SDS((T, ATTN_W), F32), SDS((T, ATTN_W), F32)]
        + [SDS((N_DEV,) + g.shape, g.dtype) for g in to_gather],
        scratch_shapes=_scatter_sems(n_g) + [pltpu.VMEM((T, LANES), F32), pltpu.VMEM((12, QBLK, 2 * QBLK), F32)],
        name="attn_fwd", compiler_params=_params(("arbitrary",), 48))(proj, proj, proj, slopes, *to_gather)
    return outs[0], outs[1], outs[2:]


def _sgu_norm(zv, ln_g, ln_b):
    gz, tz = _gelu(zv)
    mu = jnp.mean(gz, axis=-1, keepdims=True)
    xc = gz - mu
    rs = lax.rsqrt(jnp.mean(xc * xc, axis=-1, keepdims=True) + EPS)
    xhat = xc * rs
    return xhat * ln_g + ln_b, xhat, rs, tz


def _causal(w):
    i = lax.broadcasted_iota(jnp.int32, (CHUNK, CHUNK), 0)
    j = lax.broadcasted_iota(jnp.int32, (CHUNK, CHUNK), 1)
    return jnp.where(i >= j, w, 0.0)


def _sgu_fwd(proj, ln_g, ln_b, w_s, b_st, tm=512):
    T = proj.shape[0]

    def body(u_ref, z_ref, g_ref, b_ref, w_ref, bs_ref, out_ref):
        for g in range(N_GROUPS):
            wm = _causal(w_ref[g]).astype(BF)
            cols = slice(g * GROUP_DIM, (g + 1) * GROUP_DIM)
            for c in range(tm // CHUNK):
                rows = slice(c * CHUNK, (c + 1) * CHUNK)
                zn, _, _, _ = _sgu_norm(z_ref[rows, cols], g_ref[...], b_ref[...])
                mixed = _dot(wm, zn.astype(BF)) + bs_ref[:, g:g + 1]
                gu, _ = _gelu(u_ref[rows, cols])
                out_ref[rows, cols] = gu * mixed

    return pl.pallas_call(
        body, grid=(T // tm,),
        in_specs=[pl.BlockSpec((tm, SGU_W), lambda i: (i, 3)), pl.BlockSpec((tm, SGU_W), lambda i: (i, 4)),
                  pl.BlockSpec((1, GROUP_DIM), _fixed), pl.BlockSpec((1, GROUP_DIM), _fixed),
                  pl.BlockSpec((N_GROUPS, CHUNK, CHUNK), lambda i: (0, 0, 0)), pl.BlockSpec((CHUNK, LANES), _fixed)],
        out_specs=pl.BlockSpec((tm, SGU_W), _row),
        out_shape=SDS((T, SGU_W), F32),
        name="sgu_fwd", compiler_params=_params(("arbitrary",), 32))(proj, proj, ln_g, ln_b, w_s, b_st)


def _out_proj(attn, sgu, x, g_a, g_s, w_out, g_pm, g_pf, tm=256):
    T = x.shape[0]

    def body(a_ref, s_ref, x_ref, ga_ref, gs_ref, w_ref, gpm_ref, gpf_ref, grp_ref, mixed_ref, h1_ref, f_ref):
        av, sv = a_ref[...], s_ref[...]
        an = (av * _rstd(av) * ga_ref[...]).astype(BF)
        sn = (sv * _rstd(sv) * gs_ref[...]).astype(BF)
        grp_ref[:, :ATTN_W] = an
        grp_ref[:, ATTN_W:] = sn
        mixed = _dot(an, w_ref[:ATTN_W, :]) + _dot(sn, w_ref[ATTN_W:, :])
        mixed_ref[...] = mixed
        h1 = x_ref[...] + mixed * _rstd(mixed) * gpm_ref[...]
        h1_ref[...] = h1
        f_ref[...] = (h1 * _rstd(h1) * gpf_ref[...]).astype(BF)

    tok = lambda w: pl.BlockSpec((tm, w), _row)
    vec = lambda w: pl.BlockSpec((1, w), _fixed)
    return pl.pallas_call(
        body, grid=(T // tm,),
        in_specs=[tok(ATTN_W), tok(SGU_W), tok(D_MODEL), vec(ATTN_W), vec(SGU_W),
                  pl.BlockSpec((D_MODEL, D_MODEL), _fixed), vec(D_MODEL), vec(D_MODEL)],
        out_specs=[tok(D_MODEL)] * 4,
        out_shape=[SDS((T, D_MODEL), BF), SDS((T, D_MODEL), F32), SDS((T, D_MODEL), F32), SDS((T, D_MODEL), BF)],
        name="out_proj", compiler_params=_params(("arbitrary",), 32))(attn, sgu, x, g_a, g_s, w_out, g_pm, g_pf)


FF_TILE = 1408
FF_TILES = D_FF // FF_TILE


def _gate_up(f, w_gu_t, tm=512):
    T = f.shape[0]
    tn = FF_TILE

    def body(f_ref, wg_ref, wu_ref, g_ref, u_ref, act_ref):
        fv = f_ref[...]
        g = _dot_nt(fv, wg_ref[...])
        u = _dot_nt(fv, wu_ref[...])
        g_ref[...] = g.astype(BF)
        u_ref[...] = u.astype(BF)
        act_ref[...] = (g * _sigmoid(g) * u).astype(BF)

    ospec = pl.BlockSpec((tm, tn), lambda j, i: (i, j))
    return pl.pallas_call(
        body, grid=(FF_TILES, T // tm),
        in_specs=[pl.BlockSpec((tm, D_MODEL), lambda j, i: (i, 0)), pl.BlockSpec((tn, D_MODEL), lambda j, i: (j, 0)),
                  pl.BlockSpec((tn, D_MODEL), lambda j, i: (j + FF_TILES, 0))],
        out_specs=[ospec] * 3, out_shape=[SDS((T, D_FF), BF)] * 3,
        name="gate_up", compiler_params=_params(("arbitrary", "arbitrary"), 40))(f, w_gu_t, w_gu_t)


def _down_proj(act, w_down, h1, g_pff, tm=256):
    T = act.shape[0]

    def body(a_ref, w_ref, h1_ref, g_ref, y_ref, h2_ref):
        y = _dot(a_ref[...], w_ref[...])
        y_ref[...] = y
        h2_ref[...] = h1_ref[...] + y * _rstd(y) * g_ref[...]

    return pl.pallas_call(
        body, grid=(T // tm,),
        in_specs=[pl.BlockSpec((tm, D_FF), _row), pl.BlockSpec((D_FF, D_MODEL), _fixed),
                  pl.BlockSpec((tm, D_MODEL), _row), pl.BlockSpec((1, D_MODEL), _fixed)],
        out_specs=[pl.BlockSpec((tm, D_MODEL), _row)] * 2,
        out_shape=[SDS((T, D_MODEL), F32)] * 2,
        name="down_proj", compiler_params=_params(("arbitrary",), 40))(act, w_down, h1, g_pff)


def _pe_and_loss(h2, p, target, w_peg, b_peg, w_pep_t, tm=256):
    T = h2.shape[0]

    def body(h2_ref, p_ref, t_ref, wg_ref, b_ref, wp_ref,
             dh3_ref, dpp_ref, dpre_ref, h2b_ref, pb_ref, loss_ref, db_ref):
        _acc_init(pl.program_id(0), loss_ref, db_ref)
        h2v = h2_ref[...]
        h2b = h2v.astype(BF)
        pb = p_ref[...].astype(BF)
        h2b_ref[...] = h2b
        pb_ref[...] = pb
        gate = _sigmoid(_dot(h2b, wg_ref[...]) + b_ref[...])
        pp = _dot_nt(pb, wp_ref[...])
        diff = h2v + gate * pp - t_ref[...]
        loss_ref[...] += _colsum(diff * diff)
        dh3 = diff * (1.0 / D_MODEL)
        dh3_ref[...] = dh3
        dpp_ref[...] = (dh3 * gate).astype(BF)
        dpre = dh3 * pp * (gate * (1.0 - gate))
        dpre_ref[...] = dpre.astype(BF)
        db_ref[...] += _colsum(dpre)

    tok = lambda w: pl.BlockSpec((tm, w), _row)
    vec = pl.BlockSpec((1, D_MODEL), _fixed)
    return pl.pallas_call(
        body, grid=(T // tm,),
        in_specs=[tok(D_MODEL), tok(PLE_DIM), tok(D_MODEL), pl.BlockSpec((D_MODEL, D_MODEL), _fixed), vec,
                  pl.BlockSpec((D_MODEL, PLE_DIM), _fixed)],
        out_specs=[tok(D_MODEL), tok(D_MODEL), tok(D_MODEL), tok(D_MODEL), tok(PLE_DIM), vec, vec],
        out_shape=[SDS((T, D_MODEL), F32), SDS((T, D_MODEL), BF), SDS((T, D_MODEL), BF), SDS((T, D_MODEL), BF),
                   SDS((T, PLE_DIM), BF), SDS((1, D_MODEL), F32), SDS((1, D_MODEL), F32)],
        name="pe_and_loss", compiler_params=_params(("arbitrary",), 32))(h2, p, target, w_peg, b_peg, w_pep_t)


def _weight_grad(a, dy, name, into=None, row_tile=0, rows=None, tk=512):
    T, ka = a.shape
    n = dy.shape[1]
    tka = FF_TILE if ka == D_FF else min(ka, 1024)
    tn = min(n, 1024)
    rows = ka if rows is None else rows

    def body(a_ref, dy_ref, *rest):
        out_ref = rest[-1]
        _acc_init(pl.program_id(2), out_ref)
        out_ref[...] += _dot_tn(a_ref[...].astype(BF), dy_ref[...].astype(BF))

    carried = [] if into is None else [into]
    return pl.pallas_call(
        body, grid=(ka // tka, n // tn, T // tk),
        in_specs=[pl.BlockSpec((tk, tka), lambda i, j, k: (k, i)), pl.BlockSpec((tk, tn), lambda i, j, k: (k, j))]
        + [HBM] * len(carried),
        out_specs=pl.BlockSpec((tka, tn), lambda i, j, k: (i + row_tile, j)),
        out_shape=SDS((rows, n), F32), input_output_aliases={2: 0} if carried else {},
        name="grad_" + name, compiler_params=_params(("arbitrary",) * 3, 40))(a, dy, *carried)


def _pe_bwd(dpre, w_peg, dh3, y, g_pff, tm=256):
    T = y.shape[0]

    def body(dp_ref, w_ref, dh3_ref, y_ref, g_ref, dh2_ref, dy_ref, dg_ref):
        _acc_init(pl.program_id(0), dg_ref)
        dh2 = dh3_ref[...] + _dot_nt(dp_ref[...], w_ref[...])
        dh2_ref[...] = dh2
        dy, dg = _rms_bwd(dh2, y_ref[...], g_ref[...])
        dy_ref[...] = dy.astype(BF)
        dg_ref[...] += _colsum(dg)

    tok = pl.BlockSpec((tm, D_MODEL), _row)
    vec = pl.BlockSpec((1, D_MODEL), _fixed)
    return pl.pallas_call(
        body, grid=(T // tm,),
        in_specs=[tok, pl.BlockSpec((D_MODEL, D_MODEL), _fixed), tok, tok, vec],
        out_specs=[tok, tok, vec],
        out_shape=[SDS((T, D_MODEL), F32), SDS((T, D_MODEL), BF), SDS((1, D_MODEL), F32)],
        name="pe_bwd", compiler_params=_params(("arbitrary",), 32))(dpre, w_peg, dh3, y, g_pff)


def _down_bwd(dy, w_down, g, u, to_send, tm=512):
    T = dy.shape[0]
    tn = FF_TILE
    n_s = len(to_send)

    def body(dy_ref, w_ref, g_ref, u_ref, *rest):
        srcs, (dg_ref, du_ref), dsts, sems = rest[:n_s], rest[n_s:n_s + 2], rest[n_s + 2:2 * n_s + 2], rest[2 * n_s + 2:]
        j, i = pl.program_id(0), pl.program_id(1)
        items = list(zip(srcs, dsts, [True] * n_s))

        @pl.when((j == 0) & (i == 0))
        def _():
            _Scatter(items, sems).start()

        dact = _dot_nt(dy_ref[...], w_ref[...])
        gv = g_ref[...].astype(F32)
        uv = u_ref[...].astype(F32)
        s = _sigmoid(gv)
        dg_ref[...] = (dact * uv * (s * (1.0 + gv * (1.0 - s)))).astype(BF)
        du_ref[...] = (dact * (gv * s)).astype(BF)

        @pl.when((j == pl.num_programs(0) - 1) & (i == pl.num_programs(1) - 1))
        def _():
            _Scatter(items, sems).wait()

    tile = pl.BlockSpec((tm, tn), lambda j, i: (i, j))
    outs = pl.pallas_call(
        body, grid=(FF_TILES, T // tm),
        in_specs=[pl.BlockSpec((tm, D_MODEL), lambda j, i: (i, 0)), pl.BlockSpec((tn, D_MODEL), lambda j, i: (j, 0)),
                  tile, tile] + [HBM] * n_s,
        out_specs=[tile, tile] + [HBM] * n_s,
        out_shape=[SDS((T, D_FF), BF)] * 2 + [SDS(s.shape, s.dtype) for s in to_send],
        scratch_shapes=_scatter_sems(n_s),
        name="down_bwd", compiler_params=_params(("arbitrary", "arbitrary"), 40))(dy, w_down, g, u, *to_send)
    return outs[0], outs[1], outs[2:]


def _ffn_in_bwd(dg, du, w_gu_t, h1, dh2, mixed, g_pf, g_pm, to_send, tm=256):
    T = h1.shape[0]
    n_s = len(to_send)

    def body(dg_ref, du_ref, wg_ref, wu_ref, h1_ref, dh2_ref, mx_ref, gpf_ref, gpm_ref, *rest):
        srcs, outs, dsts, sems = rest[:n_s], rest[n_s:n_s + 4], rest[n_s + 4:2 * n_s + 4], rest[2 * n_s + 4:]
        dh1_ref, dmx_ref, dgpf_ref, dgpm_ref = outs
        i = pl.program_id(0)
        items = list(zip(srcs, dsts, [True] * n_s))
        _acc_init(i, dgpf_ref, dgpm_ref)

        @pl.when(i == 0)
        def _():
            _Scatter(items, sems).start()

        df = _dot(dg_ref[...], wg_ref[...]) + _dot(du_ref[...], wu_ref[...])
        dx, dgf = _rms_bwd(df, h1_ref[...], gpf_ref[...])
        dh1 = dh2_ref[...] + dx
        dh1_ref[...] = dh1
        dmx, dgm = _rms_bwd(dh1, mx_ref[...], gpm_ref[...])
        dmx_ref[...] = dmx.astype(BF)
        dgpf_ref[...] += _colsum(dgf)
        dgpm_ref[...] += _colsum(dgm)

        @pl.when(i == pl.num_programs(0) - 1)
        def _():
            _Scatter(items, sems).wait()

    tok = lambda w: pl.BlockSpec((tm, w), _row)
    vec = pl.BlockSpec((1, D_MODEL), _fixed)
    outs = pl.pallas_call(
        body, grid=(T // tm,),
        in_specs=[tok(D_FF), tok(D_FF), pl.BlockSpec((D_FF, D_MODEL), lambda i: (0, 0)),
                  pl.BlockSpec((D_FF, D_MODEL), lambda i: (1, 0)), tok(D_MODEL), tok(D_MODEL), tok(D_MODEL), vec, vec]
        + [HBM] * n_s,
        out_specs=[tok(D_MODEL), tok(D_MODEL), vec, vec] + [HBM] * n_s,
        out_shape=[SDS((T, D_MODEL), F32), SDS((T, D_MODEL), BF), SDS((1, D_MODEL), F32), SDS((1, D_MODEL), F32)]
        + [SDS(s.shape, s.dtype) for s in to_send],
        scratch_shapes=_scatter_sems(n_s),
        name="ffn_in_bwd", compiler_params=_params(("arbitrary",), 48))(
            dg, du, w_gu_t, w_gu_t, h1, dh2, mixed, g_pf, g_pm, *to_send)
    return outs[0], outs[1], outs[2], outs[3], outs[4:]


def _out_bwd(dmx, w_out, attn, sgu, g_a, g_s, tm=256):
    T = attn.shape[0]

    def body(dm_ref, w_ref, a_ref, s_ref, ga_ref, gs_ref, da_ref, ds_ref, dga_ref, dgs_ref):
        _acc_init(pl.program_id(0), dga_ref, dgs_ref)
        dgr = _dot_nt(dm_ref[...], w_ref[...])
        da, dga = _rms_bwd(dgr[:, :ATTN_W], a_ref[...], ga_ref[...])
        ds, dgs = _rms_bwd(dgr[:, ATTN_W:], s_ref[...], gs_ref[...])
        da_ref[...] = da
        ds_ref[...] = ds
        dga_ref[...] += _colsum(dga)
        dgs_ref[...] += _colsum(dgs)

    tok = lambda w: pl.BlockSpec((tm, w), _row)
    vec = lambda w: pl.BlockSpec((1, w), _fixed)
    return pl.pallas_call(
        body, grid=(T // tm,),
        in_specs=[tok(D_MODEL), pl.BlockSpec((D_MODEL, D_MODEL), _fixed), tok(ATTN_W), tok(SGU_W), vec(ATTN_W), vec(SGU_W)],
        out_specs=[tok(ATTN_W), tok(SGU_W), vec(ATTN_W), vec(SGU_W)],
        out_shape=[SDS((T, ATTN_W), F32), SDS((T, SGU_W), F32), SDS((1, ATTN_W), F32), SDS((1, SGU_W), F32)],
        name="out_bwd", compiler_params=_params(("arbitrary",), 32))(dmx, w_out, attn, sgu, g_a, g_s)


def _sgu_bwd(proj, dsgu, ln_g, ln_b, w_s, b_st, tm=512):
    T = proj.shape[0]

    def body(u_ref, z_ref, ds_ref, g_ref, b_ref, w_ref, bs_ref,
             du_ref, dz_ref, dw_ref, dbs_ref, dlg_ref, dlb_ref, dbacc_ref):
        step = pl.program_id(0)
        _acc_init(step, dw_ref, dbs_ref, dlg_ref, dlb_ref, dbacc_ref)
        lng, lnb = g_ref[...], b_ref[...]
        for g in range(N_GROUPS):
            wm = _causal(w_ref[g]).astype(BF)
            cols = slice(g * GROUP_DIM, (g + 1) * GROUP_DIM)
            for c in range(tm // CHUNK):
                rows = slice(c * CHUNK, (c + 1) * CHUNK)
                zv, uv, dout = z_ref[rows, cols], u_ref[rows, cols], ds_ref[rows, cols]
                zn, xhat, rs, tz = _sgu_norm(zv, lng, lnb)
                znb = zn.astype(BF)
                mixed = _dot(wm, znb) + bs_ref[:, g:g + 1]
                gu, tu = _gelu(uv)
                du_ref[rows, cols] = (dout * mixed * _gelu_grad(uv, tu)).astype(BF)
                dmix = dout * gu
                dmb = dmix.astype(BF)
                dw_ref[g] += _causal(_dot_nt(dmb, znb))
                dbacc_ref[g] += dmix
                dzn = _dot_tn(wm, dmb)
                dlg_ref[...] += _colsum(dzn * xhat)
                dlb_ref[...] += _colsum(dzn)
                dxh = dzn * lng
                dgz = rs * (dxh - jnp.mean(dxh, axis=-1, keepdims=True)
                            - xhat * jnp.mean(dxh * xhat, axis=-1, keepdims=True))
                dz_ref[rows, cols] = (dgz * _gelu_grad(zv, tz)).astype(BF)

        @pl.when(step == pl.num_programs(0) - 1)
        def _():
            lane = lax.broadcasted_iota(jnp.int32, (CHUNK, LANES), 1)
            acc = jnp.zeros((CHUNK, LANES), F32)
            for g in range(N_GROUPS):
                acc = jnp.where(lane == g, jnp.sum(dbacc_ref[g], axis=-1, keepdims=True), acc)
            dbs_ref[...] = acc

    tok = pl.BlockSpec((tm, SGU_W), _row)
    vec = pl.BlockSpec((1, GROUP_DIM), _fixed)
    wsp = pl.BlockSpec((N_GROUPS, CHUNK, CHUNK), lambda i: (0, 0, 0))
    sq = pl.BlockSpec((CHUNK, LANES), _fixed)
    return pl.pallas_call(
        body, grid=(T // tm,),
        in_specs=[pl.BlockSpec((tm, SGU_W), lambda i: (i, 3)), pl.BlockSpec((tm, SGU_W), lambda i: (i, 4)), tok,
                  vec, vec, wsp, sq],
        out_specs=[tok, tok, wsp, sq, vec, vec],
        out_shape=[SDS((T, SGU_W), BF), SDS((T, SGU_W), BF), SDS((N_GROUPS, CHUNK, CHUNK), F32),
                   SDS((CHUNK, LANES), F32), SDS((1, GROUP_DIM), F32), SDS((1, GROUP_DIM), F32)],
        scratch_shapes=[pltpu.VMEM((N_GROUPS, CHUNK, LANES), F32)],
        name="sgu_bwd", compiler_params=_params(("arbitrary",), 32))(proj, proj, dsgu, ln_g, ln_b, w_s, b_st)


def _attn_bwd(proj, do, o, lse, slopes, to_send):
    T = proj.shape[0]
    nblk = T // QBLK
    big = dict(pipeline_mode=pl.Buffered(1))
    n_s = len(to_send)

    def body(q_ref, k_ref, v_ref, do_ref, o_ref, l_ref, sl_ref, *rest):
        srcs, (dq_ref, dk_ref, dv_ref), dsts = rest[:n_s], rest[n_s:n_s + 3], rest[n_s + 3:2 * n_s + 3]
        sems, bias_ref = rest[2 * n_s + 3:2 * n_s + 6], rest[2 * n_s + 6]
        h = pl.program_id(0)
        items = list(zip(srcs, dsts, [True] * n_s))

        @pl.when(h == 0)
        def _():
            _Scatter(items, sems).start()

        _attn_bias(sl_ref, bias_ref)
        lo = lax.broadcasted_iota(jnp.int32, (1, LANES), 1) < HEAD_DIM
        scale = HEAD_DIM ** -0.5
        dq_ref[...] = jnp.zeros_like(dq_ref)
        dk_ref[...] = jnp.zeros_like(dk_ref)
        dv_ref[...] = jnp.zeros_like(dv_ref)

        for di, d in enumerate(DILATIONS):
            group = _attn_group(nblk, d)

            def step(i, carry, d=d, di=di, group=group):
                start, pstart, first = _attn_group_index(i, nblk, d, group)
                rows, prows = _attn_rows(start, d, group), _attn_rows(pstart, d)
                q = q_ref[rows, :] * scale
                k = jnp.concatenate([k_ref[prows, :], k_ref[rows, :]], axis=0).astype(BF)
                v = jnp.concatenate([v_ref[prows, :], v_ref[rows, :]], axis=0).astype(BF)
                dov = do_ref[rows, :]
                prod = dov * o_ref[rows, :]
                lse_g = l_ref[rows, :]
                masks = [lo if j == 0 else ~lo for j in range(2)]
                qm = [jnp.where(masks[j], q, 0.0).astype(BF) for j in range(2)]
                dom = [jnp.where(masks[j], dov, 0.0).astype(BF) for j in range(2)]
                for b in range(group):
                    qb = slice(b * QBLK, (b + 1) * QBLK)
                    kb = slice(b * QBLK, (b + 2) * QBLK)
                    dq_parts = []
                    dk_b = dv_b = None
                    for j in range(2):
                        base = (di * 2 + j) * 2
                        bias = bias_ref[base + first.astype(jnp.int32)] if b == 0 else bias_ref[base]
                        delta = jnp.sum(jnp.where(masks[j], prod[qb], 0.0), axis=-1, keepdims=True)
                        lj = lse_g[qb, j * HEAD_DIM:j * HEAD_DIM + 1]
                        pr = jnp.exp(_dot_nt(qm[j][qb], k[kb]) + bias - lj)
                        ds = (pr * (_dot_nt(dom[j][qb], v[kb]) - delta)).astype(BF)
                        dq_parts.append(_dot(ds, k[kb]))
                        t_k, t_v = _dot_tn(ds, qm[j][qb]), _dot_tn(pr.astype(BF), dom[j][qb])
                        dk_b = t_k if j == 0 else dk_b + t_k
                        dv_b = t_v if j == 0 else dv_b + t_v
                    own = _attn_rows(start + b * (d * QBLK), d)
                    dq_ref[own, :] += jnp.where(lo, dq_parts[0], dq_parts[1]) * scale
                    if b == 0:
                        dk_ref[prows, :] += dk_b[:QBLK]
                        dv_ref[prows, :] += dv_b[:QBLK]
                        dk_ref[own, :] += dk_b[QBLK:]
                        dv_ref[own, :] += dv_b[QBLK:]
                    else:
                        two = _attn_rows(start + (b - 1) * (d * QBLK), d, 2)
                        dk_ref[two, :] += dk_b
                        dv_ref[two, :] += dv_b
                return carry

            lax.fori_loop(0, nblk // group, step, 0)

        @pl.when(h == pl.num_programs(0) - 1)
        def _():
            _Scatter(items, sems).wait()

    col = lambda base: pl.BlockSpec((T, LANES), lambda h: (0, base + h), **big)
    out = pl.BlockSpec((T, LANES), lambda h: (0, h))
    outs = pl.pallas_call(
        body, grid=(4,),
        in_specs=[col(0), col(4), col(8), col(0), col(0), col(0), pl.BlockSpec((1, 8, LANES), lambda h: (h, 0, 0))]
        + [HBM] * n_s,
        out_specs=[out, out, out] + [HBM] * n_s,
        out_shape=[SDS((T, ATTN_W), F32)] * 3 + [SDS(s.shape, s.dtype) for s in to_send],
        scratch_shapes=_scatter_sems(n_s) + [pltpu.VMEM((12, QBLK, 2 * QBLK), F32)],
        name="attn_bwd", compiler_params=_params(("arbitrary",), 60))(proj, proj, proj, do, o, lse, slopes, *to_send)
    return outs[0], outs[1], outs[2], outs[3:]


def _in_bwd(dparts, w_in_t, x, dh1, g1, tm=256):
    T = x.shape[0]
    n = len(dparts)
    w = ATTN_W

    def body(*refs):
        d_refs, (w_ref, x_ref, dh1_ref, g_ref, dx_ref, dg_ref) = refs[:n], refs[n:]
        _acc_init(pl.program_id(0), dg_ref)
        da = None
        for i, r in enumerate(d_refs):
            t = _dot(r[...].astype(BF), w_ref[i * w:(i + 1) * w, :])
            da = t if da is None else da + t
        dx, dg = _rms_bwd(da, x_ref[...], g_ref[...])
        dx_ref[...] = dh1_ref[...] + dx
        dg_ref[...] += _colsum(dg)

    tok = lambda c: pl.BlockSpec((tm, c), _row)
    vec = pl.BlockSpec((1, D_MODEL), _fixed)
    return pl.pallas_call(
        body, grid=(T // tm,),
        in_specs=[tok(w)] * n + [pl.BlockSpec((PROJ, D_MODEL), _fixed), tok(D_MODEL), tok(D_MODEL), vec],
        out_specs=[tok(D_MODEL), vec],
        out_shape=[SDS((T, D_MODEL), F32), SDS((1, D_MODEL), F32)],
        name="in_bwd", compiler_params=_params(("arbitrary",), 40))(*dparts, w_in_t, x, dh1, g1)


def _sum_parts(p_ref):
    g = p_ref[0]
    for s in range(1, N_DEV):
        g = g + p_ref[s]
    return g


def _adamw_math(g, w, m, v):
    nm = ADAM_B1 * m + (1.0 - ADAM_B1) * g
    nv = ADAM_B2 * v + (1.0 - ADAM_B2) * (g * g)
    m_hat = nm / (1.0 - ADAM_B1 ** ADAM_STEP)
    v_hat = nv / (1.0 - ADAM_B2 ** ADAM_STEP)
    return -ADAM_LR * (m_hat / (jnp.sqrt(v_hat) + ADAM_EPS) + ADAM_WD * w), nm, nv


def _row_tile(rows):
    for t in (256, 176, 128, 80):
        if rows % t == 0:
            return t
    raise ValueError(rows)


def _reduce_adamw(parts, w, m, v, name):
    rows, width = w.shape
    tr = _row_tile(rows)

    def body(p_ref, w_ref, m_ref, v_ref, g_ref, d_ref, nm_ref, nv_ref):
        g = _sum_parts(p_ref)
        g_ref[...] = g
        d_ref[...], nm_ref[...], nv_ref[...] = _adamw_math(g, w_ref[...], m_ref[...], v_ref[...])

    blk = pl.BlockSpec((tr, width), _row)
    return pl.pallas_call(
        body, grid=(rows // tr,),
        in_specs=[pl.BlockSpec((N_DEV, tr, width), lambda i: (0, i, 0)), blk, blk, blk],
        out_specs=[blk] * 4, out_shape=[SDS((rows, width), F32)] * 4,
        name="adamw_" + name, compiler_params=_params(("arbitrary",), 32))(parts, w, m, v)


def _reduce(parts, name):
    _, rows, width = parts.shape
    tr = _row_tile(rows)

    def body(p_ref, g_ref):
        g_ref[...] = _sum_parts(p_ref)

    return pl.pallas_call(
        body, grid=(rows // tr,),
        in_specs=[pl.BlockSpec((N_DEV, tr, width), lambda i: (0, i, 0))],
        out_specs=pl.BlockSpec((tr, width), _row), out_shape=SDS((rows, width), F32),
        name="sum_" + name, compiler_params=_params(("arbitrary",), 32))(parts)


def _adamw(g, w, m, v, name):
    rows, width = w.shape
    tr = _row_tile(rows)

    def body(g_ref, w_ref, m_ref, v_ref, d_ref, nm_ref, nv_ref):
        d_ref[...], nm_ref[...], nv_ref[...] = _adamw_math(g_ref[...], w_ref[...], m_ref[...], v_ref[...])

    blk = pl.BlockSpec((tr, width), _row)
    return pl.pallas_call(
        body, grid=(rows // tr,), in_specs=[blk] * 4, out_specs=[blk] * 3, out_shape=[SDS((rows, width), F32)] * 3,
        name="adamw_" + name, compiler_params=_params(("arbitrary",), 32))(g, w, m, v)


SMALL = ("w_spatial", "ln_pre_mix", "ln_post_mix", "ln_pre_ffn", "ln_post_ffn", "b_pe_gate",
         "attn_out_norm", "sgu_out_norm", "b_spatial", "sgu_ln_g", "sgu_ln_b")
ROW_SHARDED = ("w_out", "w_down", "w_pe_gate")
COL_SHARDED = ("w_in", "w_gate_up", "w_pe_proj")
WEIGHTS = ("ln_pre_mix", "w_in", "sgu_ln_g", "sgu_ln_b", "w_spatial", "b_spatial", "attn_out_norm", "sgu_out_norm",
           "w_out", "ln_post_mix", "ln_pre_ffn", "w_gate_up", "w_down", "ln_post_ffn", "w_pe_gate", "b_pe_gate",
           "w_pe_proj")


def _pack_small(t):
    rows = [t["w_spatial"].reshape(-1, D_MODEL)]
    for name in SMALL[1:]:
        flat = t[name].reshape(1, -1)
        rows.append(jnp.pad(flat, ((0, 0), (0, D_MODEL - flat.shape[1]))))
    used = sum(r.shape[0] for r in rows)
    rows.append(jnp.zeros((SMALL_ROWS - used, D_MODEL), F32))
    return jnp.concatenate(rows, axis=0)


def _unpack_small(packed, shapes):
    out = {"w_spatial": packed[:64].reshape(shapes["w_spatial"])}
    for i, name in enumerate(SMALL[1:]):
        size = math.prod(shapes[name])
        out[name] = packed[64 + i, :size].reshape(shapes[name])
    return out


def _slabs(full):
    return full.reshape(N_DEV, full.shape[0] // N_DEV, full.shape[1])


def kernel(x, p, ln_pre_mix, w_in, sgu_ln_g, sgu_ln_b, w_spatial, b_spatial, attn_out_norm, sgu_out_norm, w_out, ln_post_mix, ln_pre_ffn, w_gate_up, w_down, ln_post_ffn, w_pe_gate, b_pe_gate, w_pe_proj, loss_target, m_ln_pre_mix, m_w_in, m_sgu_ln_g, m_sgu_ln_b, m_w_spatial, m_b_spatial, m_attn_out_norm, m_sgu_out_norm, m_w_out, m_ln_post_mix, m_ln_pre_ffn, m_w_gate_up, m_w_down, m_ln_post_ffn, m_w_pe_gate, m_b_pe_gate, m_w_pe_proj, v_ln_pre_mix, v_w_in, v_sgu_ln_g, v_sgu_ln_b, v_w_spatial, v_b_spatial, v_attn_out_norm, v_sgu_out_norm, v_w_out, v_ln_post_mix, v_ln_pre_ffn, v_w_gate_up, v_w_down, v_ln_post_ffn, v_w_pe_gate, v_b_pe_gate, v_w_pe_proj):
    given = dict(locals())
    w = {n: given[n] for n in WEIGHTS}
    m = {n: given["m_" + n] for n in WEIGHTS}
    v = {n: given["v_" + n] for n in WEIGHTS}
    shapes = {n: w[n].shape for n in WEIGHTS}
    xs, ps, target = x[0], p[0, 0], loss_target[0]

    shard = {n: w[n][0].astype(BF) for n in ROW_SHARDED}
    shard.update({n: w[n][0].T.astype(BF) for n in COL_SHARDED})
    sm = {n: w[n][0] for n in SMALL}
    sm = {n: (a.reshape(1, -1) if a.ndim == 1 else a) for n, a in sm.items()}
    slopes = jnp.broadcast_to((2.0 ** -(jnp.arange(8, dtype=F32) + 1.0)).reshape(4, 2, 1), (4, 2, LANES))
    slopes = jnp.concatenate([slopes, jnp.zeros((4, 6, LANES), F32)], axis=1)
    b_st = jnp.pad(sm["b_spatial"].T, ((0, 0), (0, LANES - N_GROUPS)))

    def full(gathered):
        return gathered.reshape(-1, gathered.shape[-1])

    w_in_t = full(_all_gather(shard["w_in"], "gather_w_in"))
    proj, a = _in_proj(xs, sm["ln_pre_mix"], w_in_t)
    later = ("w_out", "w_gate_up", "w_down", "w_pe_gate", "w_pe_proj")
    attn, lse, gathered = _attn_fwd(proj, slopes, [shard[n] for n in later])
    w_out_f, w_gu_t, w_down_f, w_peg_f, w_pep_t = [full(g) for g in gathered]
    sgu = _sgu_fwd(proj, sm["sgu_ln_g"], sm["sgu_ln_b"], sm["w_spatial"], b_st)
    groups, mixed, h1, f = _out_proj(attn, sgu, xs, sm["attn_out_norm"], sm["sgu_out_norm"], w_out_f,
                                     sm["ln_post_mix"], sm["ln_pre_ffn"])
    g, u, act = _gate_up(f, w_gu_t)
    y, h2 = _down_proj(act, w_down_f, h1, sm["ln_post_ffn"])
    dh3, dpp, dpre, h2b, pb, loss_cols, db_peg = _pe_and_loss(h2, ps, target, w_peg_f, sm["b_pe_gate"], w_pep_t)
    loss = lax.psum(0.5 * jnp.sum(loss_cols) * (1.0 / D_MODEL), ("x", "y", "c"))

    arrived = {}
    g_pep_t = _weight_grad(dpp, pb, "w_pe_proj")
    g_peg = _weight_grad(h2b, dpre, "w_pe_gate")
    dh2, dy, d_pff = _pe_bwd(dpre, w_peg_f, dh3, y, sm["ln_post_ffn"])
    g_down = _weight_grad(act, dy, "w_down")
    dg, du, (arrived["w_pe_proj"], arrived["w_pe_gate"]) = _down_bwd(dy, w_down_f, g, u, [_slabs(g_pep_t), _slabs(g_peg)])
    g_gu_t = _weight_grad(dg, f, "w_gate", rows=2 * D_FF)
    g_gu_t = _weight_grad(du, f, "w_up", into=g_gu_t, row_tile=FF_TILES, rows=2 * D_FF)
    dh1, dmx, d_pf, d_pm, (arrived["w_down"],) = _ffn_in_bwd(dg, du, w_gu_t, h1, dh2, mixed, sm["ln_pre_ffn"],
                                                            sm["ln_post_mix"], [_slabs(g_down)])
    g_out = _weight_grad(groups, dmx, "w_out")
    dattn, dsgu, d_ga, d_gs = _out_bwd(dmx, w_out_f, attn, sgu, sm["attn_out_norm"], sm["sgu_out_norm"])
    dus, dzs, d_ws, d_bst, d_lg, d_lb = _sgu_bwd(proj, dsgu, sm["sgu_ln_g"], sm["sgu_ln_b"], sm["w_spatial"], b_st)
    dq, dk, dv, (arrived["w_gate_up"], arrived["w_out"]) = _attn_bwd(proj, dattn, attn, lse, slopes,
                                                                     [_slabs(g_gu_t), _slabs(g_out)])
    dparts = [dq, dk, dv, dus, dzs]
    g_in_t = None
    for i, part in enumerate(dparts):
        g_in_t = _weight_grad(part, a, "w_in_%d" % i, into=g_in_t, row_tile=i, rows=PROJ)
    grad_x, d_g1 = _in_bwd(dparts, w_in_t, xs, dh1, sm["ln_pre_mix"])
    gs = dict(ln_pre_mix=d_g1, sgu_ln_g=d_lg, sgu_ln_b=d_lb, w_spatial=d_ws, b_spatial=d_bst[:, :N_GROUPS].T,
              attn_out_norm=d_ga, sgu_out_norm=d_gs, ln_post_mix=d_pm, ln_pre_ffn=d_pf, ln_post_ffn=d_pff,
              b_pe_gate=db_peg)
    arrived["w_in"], arrived_small = _scatter_call([_slabs(g_in_t), _pack_small(gs)], [True, False], "grad_exchange")

    res = {}
    for n in ROW_SHARDED:
        res[n] = _reduce_adamw(arrived[n], w[n][0], m[n][0], v[n][0], n)
    for n in COL_SHARDED:
        grad = _reduce(arrived[n], n).T
        res[n] = (grad, *_adamw(grad, w[n][0], m[n][0], v[n][0], n))
    small = _reduce_adamw(arrived_small, _pack_small(w), _pack_small(m), _pack_small(v), "small")
    small = [_unpack_small(t, shapes) for t in small]

    out = []
    for k in range(4):
        out += [res[n][k][None] if n in res else small[k][n] for n in WEIGHTS]
    return (loss, grad_x[None], *out)
```

```python
import math

import jax
import jax.numpy as jnp
from jax import lax
from jax.experimental import pallas as pl
from jax.experimental.pallas import tpu as pltpu

F32 = jnp.float32
BF = jnp.bfloat16
SDS = jax.ShapeDtypeStruct

D_MODEL = 1024
ATTN_W = 512
SGU_W = 512
HEAD_DIM = 64
N_GROUPS = 4
GROUP_DIM = 128
CHUNK = 128
D_FF = 2816
PLE_DIM = 256
PROJ = 3 * ATTN_W + 2 * SGU_W
DILATIONS = (1, 4, 16)
QBLK = 128
EPS = 1e-6
NEG = -1e30
N_DEV = 8
LANES = 128

ADAM_LR = 0.001
ADAM_B1 = 0.9
ADAM_B2 = 0.999
ADAM_EPS = 1e-08
ADAM_WD = 0.01
ADAM_STEP = 10

SMALL_ROWS = 80
MIB = 2 ** 20
MESH_ID = pl.DeviceIdType.MESH
HBM = pl.BlockSpec(memory_space=pl.ANY)


def _params(sem, vmem_mib):
    return pltpu.CompilerParams(dimension_semantics=sem, vmem_limit_bytes=vmem_mib * MIB)


def _dot(a, b):
    return jnp.dot(a, b, preferred_element_type=F32)


def _dot_nt(a, b):
    return lax.dot_general(a, b, (((1,), (1,)), ((), ())), preferred_element_type=F32)


def _dot_tn(a, b):
    return lax.dot_general(a, b, (((0,), (0,)), ((), ())), preferred_element_type=F32)


def _rstd(x):
    return lax.rsqrt(jnp.mean(x * x, axis=-1, keepdims=True) + EPS)


def _rms_bwd(dy, x, g):
    r = _rstd(x)
    n = x * r
    dn = dy * g
    dx = r * (dn - n * jnp.mean(dn * n, axis=-1, keepdims=True))
    return dx, dy * n


def _colsum(v):
    return jnp.sum(v, axis=0, keepdims=True)


_G0 = math.sqrt(2.0 / math.pi)
_G1 = 0.044715


def _gelu(x):
    t = jnp.tanh(_G0 * (x + _G1 * x * x * x))
    return 0.5 * x * (1.0 + t), t


def _gelu_grad(x, t):
    return 0.5 * (1.0 + t) + 0.5 * x * (1.0 - t * t) * (_G0 * (1.0 + 3.0 * _G1 * x * x))


def _sigmoid(x):
    return 1.0 / (1.0 + jnp.exp(-x))


def _row(i):
    return (i, 0)


def _fixed(i):
    return (0, 0)


def _acc_init(step, *refs):
    @pl.when(step == 0)
    def _():
        for r in refs:
            r[...] = jnp.zeros_like(r)


FLIPS = [(dx, dy, dc) for dx in (0, 1) for dy in (0, 1) for dc in (0, 1)][1:]
DMA_SEMS = pltpu.SemaphoreType.DMA


def _mesh_pos():
    return lax.axis_index("x"), lax.axis_index("y"), lax.axis_index("c")


def _remote(src, dst, sems, n, to):
    return pltpu.make_async_remote_copy(src_ref=src, dst_ref=dst, send_sem=sems[0].at[n], recv_sem=sems[1].at[n],
                                        device_id=to, device_id_type=MESH_ID)


class _Scatter:
    def __init__(self, items, sems):
        x, y, c = _mesh_pos()
        me = 4 * x + 2 * y + c
        self.local, self.sends, self.arrivals = [], [], []
        for i, (src, dst, slabbed) in enumerate(items):
            self.local.append(pltpu.make_async_copy(src.at[me] if slabbed else src, dst.at[me], sems[2].at[i]))
            for k, (dx, dy, dc) in enumerate(FLIPS):
                to = (1 - x if dx else x, 1 - y if dy else y, 1 - c if dc else c)
                peer = 4 * to[0] + 2 * to[1] + to[2]
                out = src.at[peer] if slabbed else src
                self.sends.append(_remote(out, dst.at[me], sems, 7 * i + k, to))
                self.arrivals.append(_remote(out, dst.at[peer], sems, 7 * i + k, to))

    def start(self):
        for cp in self.local + self.sends:
            cp.start()

    def wait(self):
        for cp in self.arrivals:
            cp.wait_recv()
        for cp in self.sends:
            cp.wait_send()
        for cp in self.local:
            cp.wait()


def _scatter_sems(n):
    return [DMA_SEMS((7 * n,)), DMA_SEMS((7 * n,)), DMA_SEMS((n,))]


class _Gather:
    def __init__(self, items, sems):
        x, y, c = _mesh_pos()
        me, sibling = (x, y, c), (x, y, 1 - c)
        chips = [(1 - x, y), (x, 1 - y), (1 - x, 1 - y)]
        self.first, self.passed, self.from_chips, self.rest, self.local = [], [], [], [], []
        for i, (src, dst) in enumerate(items):
            def slot(p, dst=dst):
                return dst.at[4 * p[0] + 2 * p[1] + p[2]]

            def copy(k, block, to, own=False, i=i, src=src, slot=slot):
                return _remote(src if own else slot(block), slot(block), sems, 7 * i + k, to)

            self.local.append(pltpu.make_async_copy(src, slot(me), sems[2].at[i]))
            self.first.append(copy(0, me, sibling, own=True))
            self.first += [copy(1 + j, me, (*chip, c), own=True) for j, chip in enumerate(chips)]
            self.passed += [copy(4 + j, (*chip, c), sibling) for j, chip in enumerate(chips)]
            self.from_chips += [copy(1 + j, (*chip, c), me) for j, chip in enumerate(chips)]
            self.rest.append(copy(0, sibling, me))
            self.rest += [copy(4 + j, (*chip, 1 - c), me) for j, chip in enumerate(chips)]

    def start(self):
        for cp in self.local + self.first:
            cp.start()

    def forward(self):
        for arrived, onward in zip(self.from_chips, self.passed):
            arrived.wait_recv()
            onward.start()

    def finish(self):
        for cp in self.rest:
            cp.wait_recv()
        for cp in self.first + self.passed:
            cp.wait_send()
        for cp in self.local:
            cp.wait()


def _all_gather(shard, name):
    def body(x_ref, out_ref, *sems):
        g = _Gather([(x_ref, out_ref)], sems)
        g.start()
        g.forward()
        g.finish()

    return pl.pallas_call(
        body, out_shape=SDS((N_DEV,) + shard.shape, shard.dtype), in_specs=[HBM], out_specs=HBM,
        scratch_shapes=_scatter_sems(1), name=name)(shard)


def _scatter_call(srcs, slabbed, name):
    n = len(srcs)

    def body(*refs):
        sc = _Scatter(list(zip(refs[:n], refs[n:2 * n], slabbed)), refs[2 * n:])
        sc.start()
        sc.wait()

    shapes = [SDS(s.shape if sl else (N_DEV,) + s.shape, s.dtype) for s, sl in zip(srcs, slabbed)]
    return pl.pallas_call(body, out_shape=shapes, in_specs=[HBM] * n, out_specs=[HBM] * n,
                          scratch_shapes=_scatter_sems(n), name=name)(*srcs)


def _in_proj(x, g1, w_in_t, tm=256):
    T = x.shape[0]

    def body(x_ref, g_ref, w_ref, proj_ref, a_ref):
        xv = x_ref[...]
        a = (xv * _rstd(xv) * g_ref[...]).astype(BF)
        a_ref[...] = a
        proj_ref[...] = _dot_nt(a, w_ref[...])

    return pl.pallas_call(
        body, grid=(T // tm,),
        in_specs=[pl.BlockSpec((tm, D_MODEL), _row), pl.BlockSpec((1, D_MODEL), _fixed),
                  pl.BlockSpec((PROJ, D_MODEL), _fixed)],
        out_specs=[pl.BlockSpec((tm, PROJ), _row), pl.BlockSpec((tm, D_MODEL), _row)],
        out_shape=[SDS((T, PROJ), F32), SDS((T, D_MODEL), BF)],
        name="in_proj", compiler_params=_params(("arbitrary",), 40))(x, g1, w_in_t)


ATTN_GROUP = 16


def _attn_bias(sl_ref, bias_ref):
    qi = lax.broadcasted_iota(jnp.int32, (QBLK, QBLK), 0)
    kj = lax.broadcasted_iota(jnp.int32, (QBLK, QBLK), 1)
    step = qi - kj
    for di, d in enumerate(DILATIONS):
        for j in range(2):
            sl = sl_ref[0, j:j + 1, :]
            cur = jnp.where(step >= 0, -sl * (step * d).astype(F32), NEG)
            prev = jnp.where(step <= 0, -sl * ((step + QBLK) * d).astype(F32), NEG)
            rows = slice(j * QBLK, (j + 1) * QBLK)
            bias_ref[di * 2, rows, :QBLK] = prev
            bias_ref[di * 2, rows, QBLK:] = cur
            bias_ref[di * 2 + 1, rows, :QBLK] = jnp.full((QBLK, QBLK), NEG, F32)
            bias_ref[di * 2 + 1, rows, QBLK:] = cur


def _stack_heads(x, lo):
    return jnp.concatenate([jnp.where(lo, x, 0.0), jnp.where(lo, 0.0, x)], axis=0).astype(BF)


def _unstack_heads(x, lo):
    return jnp.where(lo, x[:QBLK], x[QBLK:])


def _attn_rows(start, d, blocks=1):
    if d == 1:
        return pl.ds(pl.multiple_of(start, QBLK), blocks * QBLK)
    return pl.ds(start, blocks * QBLK, stride=d)


def _attn_group_index(i, nblk, d, group):
    per = nblk // d // group
    r = i // per
    n0 = (i % per) * group
    start = r + (d * QBLK) * n0
    pstart = jnp.maximum(start - d * QBLK, r)
    return start, pstart, n0 == 0


def _attn_plan(nblk, d):
    group = min(ATTN_GROUP, nblk // d)
    return group, max(1, min(ATTN_GROUP // group, d))


def _attn_fwd(proj, slopes, to_gather):
    T = proj.shape[0]
    nblk = T // QBLK
    big = dict(pipeline_mode=pl.Buffered(1))
    n_g = len(to_gather)

    def body(q_ref, k_ref, v_ref, sl_ref, *rest):
        srcs, (o_ref, m_ref), dsts = rest[:n_g], rest[n_g:n_g + 2], rest[n_g + 2:2 * n_g + 2]
        sems, (l_ref, bias_ref) = rest[2 * n_g + 2:2 * n_g + 5], rest[2 * n_g + 5:]
        h = pl.program_id(0)

        @pl.when(h == 0)
        def _():
            _Gather(list(zip(srcs, dsts)), sems).start()

        @pl.when(h == 2)
        def _():
            _Gather(list(zip(srcs, dsts)), sems).forward()

        _attn_bias(sl_ref, bias_ref)
        lo = lax.broadcasted_iota(jnp.int32, (1, LANES), 1) < HEAD_DIM

        for di, d in enumerate(DILATIONS):
            group, segs = _attn_plan(nblk, d)

            def step(i, carry, segs=segs, **kw):
                for s in range(segs):
                    segment(i * segs + s, **kw)
                return carry

            def segment(i, d=d, di=di, group=group):
                start, pstart, first = _attn_group_index(i, nblk, d, group)
                rows, prows = _attn_rows(start, d, group), _attn_rows(pstart, d)
                q = q_ref[rows, :] * (HEAD_DIM ** -0.5)
                k = jnp.concatenate([k_ref[prows, :], k_ref[rows, :]], axis=0).astype(BF)
                v = jnp.concatenate([v_ref[prows, :], v_ref[rows, :]], axis=0).astype(BF)
                for b in range(group):
                    qb = slice(b * QBLK, (b + 1) * QBLK)
                    kb = slice(b * QBLK, (b + 2) * QBLK)
                    bias = bias_ref[di * 2 + first.astype(jnp.int32)] if b == 0 else bias_ref[di * 2]
                    s = _dot_nt(_stack_heads(q[qb], lo), k[kb]) + bias
                    m = jnp.max(s, axis=-1, keepdims=True)
                    pr = jnp.exp(s - m)
                    m_b = _unstack_heads(m, lo)
                    l_b = _unstack_heads(jnp.sum(pr, axis=-1, keepdims=True), lo)
                    o_b = _unstack_heads(_dot(pr.astype(BF), v[kb]), lo)
                    out = _attn_rows(start + b * (d * QBLK), d)
                    if di == 0:
                        m_ref[out, :] = m_b
                        l_ref[out, :] = l_b
                        o_ref[out, :] = o_b
                    else:
                        m_o = m_ref[out, :]
                        m_n = jnp.maximum(m_o, m_b)
                        wa, wb = jnp.exp(m_o - m_n), jnp.exp(m_b - m_n)
                        m_ref[out, :] = m_n
                        l_ref[out, :] = wa * l_ref[out, :] + wb * l_b
                        o_ref[out, :] = wa * o_ref[out, :] + wb * o_b

            lax.fori_loop(0, nblk // (group * segs), step, 0)

        def finish(i, carry):
            rows = pl.ds(pl.multiple_of(i * QBLK, QBLK), QBLK)
            l = l_ref[rows, :]
            o_ref[rows, :] = o_ref[rows, :] / l
            m_ref[rows, :] = m_ref[rows, :] + jnp.log(l)
            return carry

        lax.fori_loop(0, nblk, finish, 0)

        @pl.when(h == pl.num_programs(0) - 1)
        def _():
            _Gather(list(zip(srcs, dsts)), sems).finish()

    col = lambda base: pl.BlockSpec((T, LANES), lambda h: (0, base + h), **big)
    tok = pl.BlockSpec((T, LANES), lambda h: (0, h))
    outs = pl.pallas_call(
        body, grid=(4,),
        in_specs=[col(0), col(4), col(8), pl.BlockSpec((1, 8, LANES), lambda h: (h, 0, 0))] + [HBM] * n_g,
        out_specs=[tok, tok] + [HBM] * n_g,
        out_shape=[SDS((T, ATTN_W), F32), SDS((T, ATTN_W), F32)]
        + [SDS((N_DEV,) + g.shape, g.dtype) for g in to_gather],
        scratch_shapes=_scatter_sems(n_g) + [pltpu.VMEM((T, LANES), F32), pltpu.VMEM((6, 2 * QBLK, 2 * QBLK), F32)],
        name="attn_fwd", compiler_params=_params(("arbitrary",), 48))(proj, proj, proj, slopes, *to_gather)
    return outs[0], outs[1], outs[2:]


def _sgu_norm(zv, ln_g, ln_b):
    gz, tz = _gelu(zv)
    mu = jnp.mean(gz, axis=-1, keepdims=True)
    xc = gz - mu
    rs = lax.rsqrt(jnp.mean(xc * xc, axis=-1, keepdims=True) + EPS)
    xhat = xc * rs
    return xhat * ln_g + ln_b, xhat, rs, tz


def _causal(w):
    i = lax.broadcasted_iota(jnp.int32, (CHUNK, CHUNK), 0)
    j = lax.broadcasted_iota(jnp.int32, (CHUNK, CHUNK), 1)
    return jnp.where(i >= j, w, 0.0)


def _sgu_fwd(proj, ln_g, ln_b, w_s, b_st, tm=512):
    T = proj.shape[0]

    def body(u_ref, z_ref, g_ref, b_ref, w_ref, bs_ref, out_ref):
        for g in range(N_GROUPS):
            wm = _causal(w_ref[g]).astype(BF)
            cols = slice(g * GROUP_DIM, (g + 1) * GROUP_DIM)
            for c in range(tm // CHUNK):
                rows = slice(c * CHUNK, (c + 1) * CHUNK)
                zn, _, _, _ = _sgu_norm(z_ref[rows, cols], g_ref[...], b_ref[...])
                mixed = _dot(wm, zn.astype(BF)) + bs_ref[:, g:g + 1]
                gu, _ = _gelu(u_ref[rows, cols])
                out_ref[rows, cols] = gu * mixed

    return pl.pallas_call(
        body, grid=(T // tm,),
        in_specs=[pl.BlockSpec((tm, SGU_W), lambda i: (i, 3)), pl.BlockSpec((tm, SGU_W), lambda i: (i, 4)),
                  pl.BlockSpec((1, GROUP_DIM), _fixed), pl.BlockSpec((1, GROUP_DIM), _fixed),
                  pl.BlockSpec((N_GROUPS, CHUNK, CHUNK), lambda i: (0, 0, 0)), pl.BlockSpec((CHUNK, LANES), _fixed)],
        out_specs=pl.BlockSpec((tm, SGU_W), _row),
        out_shape=SDS((T, SGU_W), F32),
        name="sgu_fwd", compiler_params=_params(("arbitrary",), 32))(proj, proj, ln_g, ln_b, w_s, b_st)


def _out_proj(attn, sgu, x, g_a, g_s, w_out, g_pm, g_pf, tm=256):
    T = x.shape[0]

    def body(a_ref, s_ref, x_ref, ga_ref, gs_ref, w_ref, gpm_ref, gpf_ref, grp_ref, mixed_ref, h1_ref, f_ref):
        av, sv = a_ref[...], s_ref[...]
        an = (av * _rstd(av) * ga_ref[...]).astype(BF)
        sn = (sv * _rstd(sv) * gs_ref[...]).astype(BF)
        grp_ref[:, :ATTN_W] = an
        grp_ref[:, ATTN_W:] = sn
        mixed = _dot(an, w_ref[:ATTN_W, :]) + _dot(sn, w_ref[ATTN_W:, :])
        mixed_ref[...] = mixed
        h1 = x_ref[...] + mixed * _rstd(mixed) * gpm_ref[...]
        h1_ref[...] = h1
        f_ref[...] = (h1 * _rstd(h1) * gpf_ref[...]).astype(BF)

    tok = lambda w: pl.BlockSpec((tm, w), _row)
    vec = lambda w: pl.BlockSpec((1, w), _fixed)
    return pl.pallas_call(
        body, grid=(T // tm,),
        in_specs=[tok(ATTN_W), tok(SGU_W), tok(D_MODEL), vec(ATTN_W), vec(SGU_W),
                  pl.BlockSpec((D_MODEL, D_MODEL), _fixed), vec(D_MODEL), vec(D_MODEL)],
        out_specs=[tok(D_MODEL)] * 4,
        out_shape=[SDS((T, D_MODEL), BF), SDS((T, D_MODEL), F32), SDS((T, D_MODEL), F32), SDS((T, D_MODEL), BF)],
        name="out_proj", compiler_params=_params(("arbitrary",), 32))(attn, sgu, x, g_a, g_s, w_out, g_pm, g_pf)


FF_TILE = 1408
FF_TILES = D_FF // FF_TILE


def _gate_up(f, w_gu_t, tm=512):
    T = f.shape[0]
    tn = FF_TILE

    def body(f_ref, wg_ref, wu_ref, g_ref, u_ref, act_ref):
        fv = f_ref[...]
        g = _dot_nt(fv, wg_ref[...])
        u = _dot_nt(fv, wu_ref[...])
        g_ref[...] = g.astype(BF)
        u_ref[...] = u.astype(BF)
        act_ref[...] = (g * _sigmoid(g) * u).astype(BF)

    ospec = pl.BlockSpec((tm, tn), lambda j, i: (i, j))
    return pl.pallas_call(
        body, grid=(FF_TILES, T // tm),
        in_specs=[pl.BlockSpec((tm, D_MODEL), lambda j, i: (i, 0)), pl.BlockSpec((tn, D_MODEL), lambda j, i: (j, 0)),
                  pl.BlockSpec((tn, D_MODEL), lambda j, i: (j + FF_TILES, 0))],
        out_specs=[ospec] * 3, out_shape=[SDS((T, D_FF), BF)] * 3,
        name="gate_up", compiler_params=_params(("arbitrary", "arbitrary"), 40))(f, w_gu_t, w_gu_t)


def _down_proj(act, w_down, h1, g_pff, tm=256):
    T = act.shape[0]

    def body(a_ref, w_ref, h1_ref, g_ref, y_ref, h2_ref):
        y = _dot(a_ref[...], w_ref[...])
        y_ref[...] = y
        h2_ref[...] = h1_ref[...] + y * _rstd(y) * g_ref[...]

    return pl.pallas_call(
        body, grid=(T // tm,),
        in_specs=[pl.BlockSpec((tm, D_FF), _row), pl.BlockSpec((D_FF, D_MODEL), _fixed),
                  pl.BlockSpec((tm, D_MODEL), _row), pl.BlockSpec((1, D_MODEL), _fixed)],
        out_specs=[pl.BlockSpec((tm, D_MODEL), _row)] * 2,
        out_shape=[SDS((T, D_MODEL), F32)] * 2,
        name="down_proj", compiler_params=_params(("arbitrary",), 40))(act, w_down, h1, g_pff)


def _pe_and_loss(h2, p, target, w_peg, b_peg, w_pep_t, tm=256):
    T = h2.shape[0]

    def body(h2_ref, p_ref, t_ref, wg_ref, b_ref, wp_ref,
             dh3_ref, dpp_ref, dpre_ref, h2b_ref, pb_ref, loss_ref, db_ref):
        _acc_init(pl.program_id(0), loss_ref, db_ref)
        h2v = h2_ref[...]
        h2b = h2v.astype(BF)
        pb = p_ref[...].astype(BF)
        h2b_ref[...] = h2b
        pb_ref[...] = pb
        gate = _sigmoid(_dot(h2b, wg_ref[...]) + b_ref[...])
        pp = _dot_nt(pb, wp_ref[...])
        diff = h2v + gate * pp - t_ref[...]
        loss_ref[...] += _colsum(diff * diff)
        dh3 = diff * (1.0 / D_MODEL)
        dh3_ref[...] = dh3
        dpp_ref[...] = (dh3 * gate).astype(BF)
        dpre = dh3 * pp * (gate * (1.0 - gate))
        dpre_ref[...] = dpre.astype(BF)
        db_ref[...] += _colsum(dpre)

    tok = lambda w: pl.BlockSpec((tm, w), _row)
    vec = pl.BlockSpec((1, D_MODEL), _fixed)
    return pl.pallas_call(
        body, grid=(T // tm,),
        in_specs=[tok(D_MODEL), tok(PLE_DIM), tok(D_MODEL), pl.BlockSpec((D_MODEL, D_MODEL), _fixed), vec,
                  pl.BlockSpec((D_MODEL, PLE_DIM), _fixed)],
        out_specs=[tok(D_MODEL), tok(D_MODEL), tok(D_MODEL), tok(D_MODEL), tok(PLE_DIM), vec, vec],
        out_shape=[SDS((T, D_MODEL), F32), SDS((T, D_MODEL), BF), SDS((T, D_MODEL), BF), SDS((T, D_MODEL), BF),
                   SDS((T, PLE_DIM), BF), SDS((1, D_MODEL), F32), SDS((1, D_MODEL), F32)],
        name="pe_and_loss", compiler_params=_params(("arbitrary",), 32))(h2, p, target, w_peg, b_peg, w_pep_t)


def _weight_grad(a, dy, name, into=None, row_tile=0, rows=None, out_dtype=F32, tk=512):
    T, ka = a.shape
    n = dy.shape[1]
    tka = FF_TILE if ka == D_FF else min(ka, 1024)
    tn = min(n, 1024)
    rows = ka if rows is None else rows
    direct = out_dtype == F32

    def body(a_ref, dy_ref, *rest):
        out_ref, acc_ref = (rest[-1], rest[-1]) if direct else rest[-2:]
        _acc_init(pl.program_id(2), acc_ref)
        acc_ref[...] += _dot_tn(a_ref[...].astype(BF), dy_ref[...].astype(BF))
        if not direct:
            @pl.when(pl.program_id(2) == pl.num_programs(2) - 1)
            def _():
                out_ref[...] = acc_ref[...].astype(out_dtype)

    carried = [] if into is None else [into]
    return pl.pallas_call(
        body, grid=(ka // tka, n // tn, T // tk),
        in_specs=[pl.BlockSpec((tk, tka), lambda i, j, k: (k, i)), pl.BlockSpec((tk, tn), lambda i, j, k: (k, j))]
        + [HBM] * len(carried),
        out_specs=pl.BlockSpec((tka, tn), lambda i, j, k: (i + row_tile, j)),
        out_shape=SDS((rows, n), out_dtype), input_output_aliases={2: 0} if carried else {},
        scratch_shapes=[] if direct else [pltpu.VMEM((tka, tn), F32)],
        name="grad_" + name, compiler_params=_params(("arbitrary",) * 3, 40))(a, dy, *carried)


def _pe_bwd(dpre, w_peg, dh3, y, g_pff, tm=256):
    T = y.shape[0]

    def body(dp_ref, w_ref, dh3_ref, y_ref, g_ref, dh2_ref, dy_ref, dg_ref):
        _acc_init(pl.program_id(0), dg_ref)
        dh2 = dh3_ref[...] + _dot_nt(dp_ref[...], w_ref[...])
        dh2_ref[...] = dh2
        dy, dg = _rms_bwd(dh2, y_ref[...], g_ref[...])
        dy_ref[...] = dy.astype(BF)
        dg_ref[...] += _colsum(dg)

    tok = pl.BlockSpec((tm, D_MODEL), _row)
    vec = pl.BlockSpec((1, D_MODEL), _fixed)
    return pl.pallas_call(
        body, grid=(T // tm,),
        in_specs=[tok, pl.BlockSpec((D_MODEL, D_MODEL), _fixed), tok, tok, vec],
        out_specs=[tok, tok, vec],
        out_shape=[SDS((T, D_MODEL), F32), SDS((T, D_MODEL), BF), SDS((1, D_MODEL), F32)],
        name="pe_bwd", compiler_params=_params(("arbitrary",), 32))(dpre, w_peg, dh3, y, g_pff)


def _down_bwd(dy, w_down, g, u, to_send, tm=512):
    T = dy.shape[0]
    tn = FF_TILE
    n_s = len(to_send)

    def body(dy_ref, w_ref, g_ref, u_ref, *rest):
        srcs, (dg_ref, du_ref), dsts, sems = rest[:n_s], rest[n_s:n_s + 2], rest[n_s + 2:2 * n_s + 2], rest[2 * n_s + 2:]
        j, i = pl.program_id(0), pl.program_id(1)
        items = list(zip(srcs, dsts, [True] * n_s))

        @pl.when((j == 0) & (i == 0))
        def _():
            _Scatter(items, sems).start()

        dact = _dot_nt(dy_ref[...], w_ref[...])
        gv = g_ref[...].astype(F32)
        uv = u_ref[...].astype(F32)
        s = _sigmoid(gv)
        dg_ref[...] = (dact * uv * (s * (1.0 + gv * (1.0 - s)))).astype(BF)
        du_ref[...] = (dact * (gv * s)).astype(BF)

        @pl.when((j == pl.num_programs(0) - 1) & (i == pl.num_programs(1) - 1))
        def _():
            _Scatter(items, sems).wait()

    tile = pl.BlockSpec((tm, tn), lambda j, i: (i, j))
    outs = pl.pallas_call(
        body, grid=(FF_TILES, T // tm),
        in_specs=[pl.BlockSpec((tm, D_MODEL), lambda j, i: (i, 0)), pl.BlockSpec((tn, D_MODEL), lambda j, i: (j, 0)),
                  tile, tile] + [HBM] * n_s,
        out_specs=[tile, tile] + [HBM] * n_s,
        out_shape=[SDS((T, D_FF), BF)] * 2 + [SDS(s.shape, s.dtype) for s in to_send],
        scratch_shapes=_scatter_sems(n_s),
        name="down_bwd", compiler_params=_params(("arbitrary", "arbitrary"), 40))(dy, w_down, g, u, *to_send)
    return outs[0], outs[1], outs[2:]


def _ffn_in_bwd(dg, du, w_gu_t, h1, dh2, mixed, g_pf, g_pm, to_send, tm=256):
    T = h1.shape[0]
    n_s = len(to_send)

    def body(dg_ref, du_ref, wg_ref, wu_ref, h1_ref, dh2_ref, mx_ref, gpf_ref, gpm_ref, *rest):
        srcs, outs, dsts, sems = rest[:n_s], rest[n_s:n_s + 4], rest[n_s + 4:2 * n_s + 4], rest[2 * n_s + 4:]
        dh1_ref, dmx_ref, dgpf_ref, dgpm_ref = outs
        i = pl.program_id(0)
        items = list(zip(srcs, dsts, [True] * n_s))
        _acc_init(i, dgpf_ref, dgpm_ref)

        @pl.when(i == 0)
        def _():
            _Scatter(items, sems).start()

        df = _dot(dg_ref[...], wg_ref[...]) + _dot(du_ref[...], wu_ref[...])
        dx, dgf = _rms_bwd(df, h1_ref[...], gpf_ref[...])
        dh1 = dh2_ref[...] + dx
        dh1_ref[...] = dh1
        dmx, dgm = _rms_bwd(dh1, mx_ref[...], gpm_ref[...])
        dmx_ref[...] = dmx.astype(BF)
        dgpf_ref[...] += _colsum(dgf)
        dgpm_ref[...] += _colsum(dgm)

        @pl.when(i == pl.num_programs(0) - 1)
        def _():
            _Scatter(items, sems).wait()

    tok = lambda w: pl.BlockSpec((tm, w), _row)
    vec = pl.BlockSpec((1, D_MODEL), _fixed)
    outs = pl.pallas_call(
        body, grid=(T // tm,),
        in_specs=[tok(D_FF), tok(D_FF), pl.BlockSpec((D_FF, D_MODEL), lambda i: (0, 0)),
                  pl.BlockSpec((D_FF, D_MODEL), lambda i: (1, 0)), tok(D_MODEL), tok(D_MODEL), tok(D_MODEL), vec, vec]
        + [HBM] * n_s,
        out_specs=[tok(D_MODEL), tok(D_MODEL), vec, vec] + [HBM] * n_s,
        out_shape=[SDS((T, D_MODEL), F32), SDS((T, D_MODEL), BF), SDS((1, D_MODEL), F32), SDS((1, D_MODEL), F32)]
        + [SDS(s.shape, s.dtype) for s in to_send],
        scratch_shapes=_scatter_sems(n_s),
        name="ffn_in_bwd", compiler_params=_params(("arbitrary",), 48))(
            dg, du, w_gu_t, w_gu_t, h1, dh2, mixed, g_pf, g_pm, *to_send)
    return outs[0], outs[1], outs[2], outs[3], outs[4:]


def _out_bwd(dmx, w_out, attn, sgu, g_a, g_s, tm=256):
    T = attn.shape[0]

    def body(dm_ref, w_ref, a_ref, s_ref, ga_ref, gs_ref, da_ref, ds_ref, dga_ref, dgs_ref):
        _acc_init(pl.program_id(0), dga_ref, dgs_ref)
        dgr = _dot_nt(dm_ref[...], w_ref[...])
        da, dga = _rms_bwd(dgr[:, :ATTN_W], a_ref[...], ga_ref[...])
        ds, dgs = _rms_bwd(dgr[:, ATTN_W:], s_ref[...], gs_ref[...])
        da_ref[...] = da
        ds_ref[...] = ds
        dga_ref[...] += _colsum(dga)
        dgs_ref[...] += _colsum(dgs)

    tok = lambda w: pl.BlockSpec((tm, w), _row)
    vec = lambda w: pl.BlockSpec((1, w), _fixed)
    return pl.pallas_call(
        body, grid=(T // tm,),
        in_specs=[tok(D_MODEL), pl.BlockSpec((D_MODEL, D_MODEL), _fixed), tok(ATTN_W), tok(SGU_W), vec(ATTN_W), vec(SGU_W)],
        out_specs=[tok(ATTN_W), tok(SGU_W), vec(ATTN_W), vec(SGU_W)],
        out_shape=[SDS((T, ATTN_W), F32), SDS((T, SGU_W), F32), SDS((1, ATTN_W), F32), SDS((1, SGU_W), F32)],
        name="out_bwd", compiler_params=_params(("arbitrary",), 32))(dmx, w_out, attn, sgu, g_a, g_s)


def _sgu_bwd(proj, dsgu, ln_g, ln_b, w_s, b_st, tm=512):
    T = proj.shape[0]

    def body(u_ref, z_ref, ds_ref, g_ref, b_ref, w_ref, bs_ref,
             du_ref, dz_ref, dw_ref, dbs_ref, dlg_ref, dlb_ref, dbacc_ref):
        step = pl.program_id(0)
        _acc_init(step, dw_ref, dbs_ref, dlg_ref, dlb_ref, dbacc_ref)
        lng, lnb = g_ref[...], b_ref[...]
        for g in range(N_GROUPS):
            wm = _causal(w_ref[g]).astype(BF)
            cols = slice(g * GROUP_DIM, (g + 1) * GROUP_DIM)
            for c in range(tm // CHUNK):
                rows = slice(c * CHUNK, (c + 1) * CHUNK)
                zv, uv, dout = z_ref[rows, cols], u_ref[rows, cols], ds_ref[rows, cols]
                zn, xhat, rs, tz = _sgu_norm(zv, lng, lnb)
                znb = zn.astype(BF)
                mixed = _dot(wm, znb) + bs_ref[:, g:g + 1]
                gu, tu = _gelu(uv)
                du_ref[rows, cols] = (dout * mixed * _gelu_grad(uv, tu)).astype(BF)
                dmix = dout * gu
                dmb = dmix.astype(BF)
                dw_ref[g] += _causal(_dot_nt(dmb, znb))
                dbacc_ref[g] += dmix
                dzn = _dot_tn(wm, dmb)
                dlg_ref[...] += _colsum(dzn * xhat)
                dlb_ref[...] += _colsum(dzn)
                dxh = dzn * lng
                dgz = rs * (dxh - jnp.mean(dxh, axis=-1, keepdims=True)
                            - xhat * jnp.mean(dxh * xhat, axis=-1, keepdims=True))
                dz_ref[rows, cols] = (dgz * _gelu_grad(zv, tz)).astype(BF)

        @pl.when(step == pl.num_programs(0) - 1)
        def _():
            lane = lax.broadcasted_iota(jnp.int32, (CHUNK, LANES), 1)
            acc = jnp.zeros((CHUNK, LANES), F32)
            for g in range(N_GROUPS):
                acc = jnp.where(lane == g, jnp.sum(dbacc_ref[g], axis=-1, keepdims=True), acc)
            dbs_ref[...] = acc

    tok = pl.BlockSpec((tm, SGU_W), _row)
    vec = pl.BlockSpec((1, GROUP_DIM), _fixed)
    wsp = pl.BlockSpec((N_GROUPS, CHUNK, CHUNK), lambda i: (0, 0, 0))
    sq = pl.BlockSpec((CHUNK, LANES), _fixed)
    return pl.pallas_call(
        body, grid=(T // tm,),
        in_specs=[pl.BlockSpec((tm, SGU_W), lambda i: (i, 3)), pl.BlockSpec((tm, SGU_W), lambda i: (i, 4)), tok,
                  vec, vec, wsp, sq],
        out_specs=[tok, tok, wsp, sq, vec, vec],
        out_shape=[SDS((T, SGU_W), BF), SDS((T, SGU_W), BF), SDS((N_GROUPS, CHUNK, CHUNK), F32),
                   SDS((CHUNK, LANES), F32), SDS((1, GROUP_DIM), F32), SDS((1, GROUP_DIM), F32)],
        scratch_shapes=[pltpu.VMEM((N_GROUPS, CHUNK, LANES), F32)],
        name="sgu_bwd", compiler_params=_params(("arbitrary",), 32))(proj, proj, dsgu, ln_g, ln_b, w_s, b_st)


def _attn_bwd(proj, do, o, lse, slopes, to_send):
    T = proj.shape[0]
    nblk = T // QBLK
    big = dict(pipeline_mode=pl.Buffered(1))
    n_s = len(to_send)

    def body(q_ref, k_ref, v_ref, do_ref, o_ref, l_ref, sl_ref, *rest):
        srcs, (dq_ref, dk_ref, dv_ref), dsts = rest[:n_s], rest[n_s:n_s + 3], rest[n_s + 3:2 * n_s + 3]
        sems, bias_ref = rest[2 * n_s + 3:2 * n_s + 6], rest[2 * n_s + 6]
        h = pl.program_id(0)
        items = list(zip(srcs, dsts, [True] * n_s))

        @pl.when(h == 0)
        def _():
            _Scatter(items, sems).start()

        _attn_bias(sl_ref, bias_ref)
        lo = lax.broadcasted_iota(jnp.int32, (1, LANES), 1) < HEAD_DIM
        scale = HEAD_DIM ** -0.5
        dq_ref[...] = jnp.zeros_like(dq_ref)
        dk_ref[...] = jnp.zeros_like(dk_ref)
        dv_ref[...] = jnp.zeros_like(dv_ref)

        for di, d in enumerate(DILATIONS):
            group, segs = _attn_plan(nblk, d)

            def step(i, carry, segs=segs, **kw):
                for s in range(segs):
                    segment(i * segs + s, **kw)
                return carry

            def segment(i, d=d, di=di, group=group):
                start, pstart, first = _attn_group_index(i, nblk, d, group)
                rows, prows = _attn_rows(start, d, group), _attn_rows(pstart, d)
                q = q_ref[rows, :] * scale
                k = jnp.concatenate([k_ref[prows, :], k_ref[rows, :]], axis=0).astype(BF)
                v = jnp.concatenate([v_ref[prows, :], v_ref[rows, :]], axis=0).astype(BF)
                dov = do_ref[rows, :]
                prod = dov * o_ref[rows, :]
                lse_g = l_ref[rows, :]
                masks = [lo, ~lo]
                qm = [jnp.where(masks[j], q, 0.0).astype(BF) for j in range(2)]
                dom = [jnp.where(masks[j], dov, 0.0).astype(BF) for j in range(2)]
                for b in range(group):
                    qb = slice(b * QBLK, (b + 1) * QBLK)
                    kb = slice(b * QBLK, (b + 2) * QBLK)
                    which = di * 2 + first.astype(jnp.int32) if b == 0 else di * 2
                    dq_parts = []
                    dk_b = dv_b = None
                    for j in range(2):
                        bias = bias_ref[which, j * QBLK:(j + 1) * QBLK, :]
                        delta = jnp.sum(jnp.where(masks[j], prod[qb], 0.0), axis=-1, keepdims=True)
                        lj = lse_g[qb, j * HEAD_DIM:j * HEAD_DIM + 1]
                        pr = jnp.exp(_dot_nt(qm[j][qb], k[kb]) + bias - lj)
                        ds = (pr * (_dot_nt(dom[j][qb], v[kb]) - delta)).astype(BF)
                        dq_parts.append(_dot(ds, k[kb]))
                        t_k, t_v = _dot_tn(ds, qm[j][qb]), _dot_tn(pr.astype(BF), dom[j][qb])
                        dk_b = t_k if j == 0 else dk_b + t_k
                        dv_b = t_v if j == 0 else dv_b + t_v
                    own = _attn_rows(start + b * (d * QBLK), d)
                    dq_ref[own, :] += jnp.where(lo, dq_parts[0], dq_parts[1]) * scale
                    if b == 0:
                        dk_ref[prows, :] += dk_b[:QBLK]
                        dv_ref[prows, :] += dv_b[:QBLK]
                        dk_ref[own, :] += dk_b[QBLK:]
                        dv_ref[own, :] += dv_b[QBLK:]
                    else:
                        two = _attn_rows(start + (b - 1) * (d * QBLK), d, 2)
                        dk_ref[two, :] += dk_b
                        dv_ref[two, :] += dv_b

            lax.fori_loop(0, nblk // (group * segs), step, 0)

        @pl.when(h == pl.num_programs(0) - 1)
        def _():
            _Scatter(items, sems).wait()

    col = lambda base: pl.BlockSpec((T, LANES), lambda h: (0, base + h), **big)
    out = pl.BlockSpec((T, LANES), lambda h: (0, h))
    outs = pl.pallas_call(
        body, grid=(4,),
        in_specs=[col(0), col(4), col(8), col(0), col(0), col(0), pl.BlockSpec((1, 8, LANES), lambda h: (h, 0, 0))]
        + [HBM] * n_s,
        out_specs=[out, out, out] + [HBM] * n_s,
        out_shape=[SDS((T, ATTN_W), F32)] * 3 + [SDS(s.shape, s.dtype) for s in to_send],
        scratch_shapes=_scatter_sems(n_s) + [pltpu.VMEM((6, 2 * QBLK, 2 * QBLK), F32)],
        name="attn_bwd", compiler_params=_params(("arbitrary",), 60))(proj, proj, proj, do, o, lse, slopes, *to_send)
    return outs[0], outs[1], outs[2], outs[3:]


def _in_bwd(dparts, w_in_t, x, dh1, g1, to_send, tm=256):
    T = x.shape[0]
    n = len(dparts)
    n_s = len(to_send)
    w = ATTN_W

    def body(*refs):
        d_refs, (w_ref, x_ref, dh1_ref, g_ref), rest = refs[:n], refs[n:n + 4], refs[n + 4:]
        srcs, (dx_ref, dg_ref), dsts, sems = rest[:n_s], rest[n_s:n_s + 2], rest[n_s + 2:2 * n_s + 2], rest[2 * n_s + 2:]
        step = pl.program_id(0)
        items = list(zip(srcs, dsts, [True] * n_s))
        _acc_init(step, dg_ref)

        @pl.when(step == 0)
        def _():
            _Scatter(items, sems).start()

        da = None
        for i, r in enumerate(d_refs):
            t = _dot(r[...].astype(BF), w_ref[i * w:(i + 1) * w, :])
            da = t if da is None else da + t
        dx, dg = _rms_bwd(da, x_ref[...], g_ref[...])
        dx_ref[...] = dh1_ref[...] + dx
        dg_ref[...] += _colsum(dg)

        @pl.when(step == pl.num_programs(0) - 1)
        def _():
            _Scatter(items, sems).wait()

    tok = lambda c: pl.BlockSpec((tm, c), _row)
    vec = pl.BlockSpec((1, D_MODEL), _fixed)
    outs = pl.pallas_call(
        body, grid=(T // tm,),
        in_specs=[tok(w)] * n + [pl.BlockSpec((PROJ, D_MODEL), _fixed), tok(D_MODEL), tok(D_MODEL), vec] + [HBM] * n_s,
        out_specs=[tok(D_MODEL), vec] + [HBM] * n_s,
        out_shape=[SDS((T, D_MODEL), F32), SDS((1, D_MODEL), F32)] + [SDS(s.shape, s.dtype) for s in to_send],
        scratch_shapes=_scatter_sems(n_s),
        name="in_bwd", compiler_params=_params(("arbitrary",), 40))(*dparts, w_in_t, x, dh1, g1, *to_send)
    return outs[0], outs[1], outs[2:]


def _sum_parts(p_ref):
    g = p_ref[0].astype(F32)
    for s in range(1, N_DEV):
        g = g + p_ref[s].astype(F32)
    return g


def _adamw_math(g, w, m, v):
    nm = ADAM_B1 * m + (1.0 - ADAM_B1) * g
    nv = ADAM_B2 * v + (1.0 - ADAM_B2) * (g * g)
    m_hat = nm / (1.0 - ADAM_B1 ** ADAM_STEP)
    v_hat = nv / (1.0 - ADAM_B2 ** ADAM_STEP)
    return -ADAM_LR * (m_hat / (jnp.sqrt(v_hat) + ADAM_EPS) + ADAM_WD * w), nm, nv


def _row_tile(rows):
    for t in (256, 176, 128, 80):
        if rows % t == 0:
            return t
    raise ValueError(rows)


def _reduce_adamw(parts, w, m, v, name):
    rows, width = w.shape
    tr = _row_tile(rows)

    def body(p_ref, w_ref, m_ref, v_ref, g_ref, d_ref, nm_ref, nv_ref):
        g = _sum_parts(p_ref)
        g_ref[...] = g
        d_ref[...], nm_ref[...], nv_ref[...] = _adamw_math(g, w_ref[...], m_ref[...], v_ref[...])

    blk = pl.BlockSpec((tr, width), _row)
    return pl.pallas_call(
        body, grid=(rows // tr,),
        in_specs=[pl.BlockSpec((N_DEV, tr, width), lambda i: (0, i, 0)), blk, blk, blk],
        out_specs=[blk] * 4, out_shape=[SDS((rows, width), F32)] * 4,
        name="adamw_" + name, compiler_params=_params(("arbitrary",), 32))(parts, w, m, v)


def _reduce(parts, name):
    _, rows, width = parts.shape
    tr = _row_tile(rows)

    def body(p_ref, g_ref):
        g_ref[...] = _sum_parts(p_ref)

    return pl.pallas_call(
        body, grid=(rows // tr,),
        in_specs=[pl.BlockSpec((N_DEV, tr, width), lambda i: (0, i, 0))],
        out_specs=pl.BlockSpec((tr, width), _row), out_shape=SDS((rows, width), F32),
        name="sum_" + name, compiler_params=_params(("arbitrary",), 32))(parts)


def _adamw(g, w, m, v, name):
    rows, width = w.shape
    tr = _row_tile(rows)

    def body(g_ref, w_ref, m_ref, v_ref, d_ref, nm_ref, nv_ref):
        d_ref[...], nm_ref[...], nv_ref[...] = _adamw_math(g_ref[...], w_ref[...], m_ref[...], v_ref[...])

    blk = pl.BlockSpec((tr, width), _row)
    return pl.pallas_call(
        body, grid=(rows // tr,), in_specs=[blk] * 4, out_specs=[blk] * 3, out_shape=[SDS((rows, width), F32)] * 3,
        name="adamw_" + name, compiler_params=_params(("arbitrary",), 32))(g, w, m, v)


SMALL = ("w_spatial", "ln_pre_mix", "ln_post_mix", "ln_pre_ffn", "ln_post_ffn", "b_pe_gate",
         "attn_out_norm", "sgu_out_norm", "b_spatial", "sgu_ln_g", "sgu_ln_b")
ROW_SHARDED = ("w_out", "w_down", "w_pe_gate")
COL_SHARDED = ("w_in", "w_gate_up", "w_pe_proj")
WEIGHTS = ("ln_pre_mix", "w_in", "sgu_ln_g", "sgu_ln_b", "w_spatial", "b_spatial", "attn_out_norm", "sgu_out_norm",
           "w_out", "ln_post_mix", "ln_pre_ffn", "w_gate_up", "w_down", "ln_post_ffn", "w_pe_gate", "b_pe_gate",
           "w_pe_proj")


def _pack_small(t):
    rows = [t["w_spatial"].reshape(-1, D_MODEL)]
    for name in SMALL[1:]:
        flat = t[name].reshape(1, -1)
        rows.append(jnp.pad(flat, ((0, 0), (0, D_MODEL - flat.shape[1]))))
    used = sum(r.shape[0] for r in rows)
    rows.append(jnp.zeros((SMALL_ROWS - used, D_MODEL), F32))
    return jnp.concatenate(rows, axis=0)


def _unpack_small(packed, shapes):
    out = {"w_spatial": packed[:64].reshape(shapes["w_spatial"])}
    for i, name in enumerate(SMALL[1:]):
        size = math.prod(shapes[name])
        out[name] = packed[64 + i, :size].reshape(shapes[name])
    return out


def _slabs(full):
    return full.reshape(N_DEV, full.shape[0] // N_DEV, full.shape[1])


def kernel(x, p, ln_pre_mix, w_in, sgu_ln_g, sgu_ln_b, w_spatial, b_spatial, attn_out_norm, sgu_out_norm, w_out, ln_post_mix, ln_pre_ffn, w_gate_up, w_down, ln_post_ffn, w_pe_gate, b_pe_gate, w_pe_proj, loss_target, m_ln_pre_mix, m_w_in, m_sgu_ln_g, m_sgu_ln_b, m_w_spatial, m_b_spatial, m_attn_out_norm, m_sgu_out_norm, m_w_out, m_ln_post_mix, m_ln_pre_ffn, m_w_gate_up, m_w_down, m_ln_post_ffn, m_w_pe_gate, m_b_pe_gate, m_w_pe_proj, v_ln_pre_mix, v_w_in, v_sgu_ln_g, v_sgu_ln_b, v_w_spatial, v_b_spatial, v_attn_out_norm, v_sgu_out_norm, v_w_out, v_ln_post_mix, v_ln_pre_ffn, v_w_gate_up, v_w_down, v_ln_post_ffn, v_w_pe_gate, v_b_pe_gate, v_w_pe_proj):
    given = dict(locals())
    w = {n: given[n] for n in WEIGHTS}
    m = {n: given["m_" + n] for n in WEIGHTS}
    v = {n: given["v_" + n] for n in WEIGHTS}
    shapes = {n: w[n].shape for n in WEIGHTS}
    xs, ps, target = x[0], p[0, 0], loss_target[0]

    shard = {n: w[n][0].astype(BF) for n in ROW_SHARDED}
    shard.update({n: w[n][0].T.astype(BF) for n in COL_SHARDED})
    sm = {n: w[n][0] for n in SMALL}
    sm = {n: (a.reshape(1, -1) if a.ndim == 1 else a) for n, a in sm.items()}
    slopes = jnp.broadcast_to((2.0 ** -(jnp.arange(8, dtype=F32) + 1.0)).reshape(4, 2, 1), (4, 2, LANES))
    slopes = jnp.concatenate([slopes, jnp.zeros((4, 6, LANES), F32)], axis=1)
    b_st = jnp.pad(sm["b_spatial"].T, ((0, 0), (0, LANES - N_GROUPS)))

    def full(gathered):
        return gathered.reshape(-1, gathered.shape[-1])

    w_in_t = full(_all_gather(shard["w_in"], "gather_w_in"))
    proj, a = _in_proj(xs, sm["ln_pre_mix"], w_in_t)
    later = ("w_out", "w_gate_up", "w_down", "w_pe_gate", "w_pe_proj")
    attn, lse, gathered = _attn_fwd(proj, slopes, [shard[n] for n in later])
    w_out_f, w_gu_t, w_down_f, w_peg_f, w_pep_t = [full(g) for g in gathered]
    sgu = _sgu_fwd(proj, sm["sgu_ln_g"], sm["sgu_ln_b"], sm["w_spatial"], b_st)
    groups, mixed, h1, f = _out_proj(attn, sgu, xs, sm["attn_out_norm"], sm["sgu_out_norm"], w_out_f,
                                     sm["ln_post_mix"], sm["ln_pre_ffn"])
    g, u, act = _gate_up(f, w_gu_t)
    y, h2 = _down_proj(act, w_down_f, h1, sm["ln_post_ffn"])
    dh3, dpp, dpre, h2b, pb, loss_cols, db_peg = _pe_and_loss(h2, ps, target, w_peg_f, sm["b_pe_gate"], w_pep_t)
    loss = lax.psum(0.5 * jnp.sum(loss_cols) * (1.0 / D_MODEL), ("x", "y", "c"))

    arrived = {}
    g_pep_t = _weight_grad(dpp, pb, "w_pe_proj")
    g_peg = _weight_grad(h2b, dpre, "w_pe_gate")
    dh2, dy, d_pff = _pe_bwd(dpre, w_peg_f, dh3, y, sm["ln_post_ffn"])
    g_down = _weight_grad(act, dy, "w_down")
    dg, du, (arrived["w_pe_proj"], arrived["w_pe_gate"]) = _down_bwd(dy, w_down_f, g, u, [_slabs(g_pep_t), _slabs(g_peg)])
    g_gu_t = _weight_grad(dg, f, "w_gate", rows=2 * D_FF)
    g_gu_t = _weight_grad(du, f, "w_up", into=g_gu_t, row_tile=FF_TILES, rows=2 * D_FF)
    dh1, dmx, d_pf, d_pm, (arrived["w_down"],) = _ffn_in_bwd(dg, du, w_gu_t, h1, dh2, mixed, sm["ln_pre_ffn"],
                                                            sm["ln_post_mix"], [_slabs(g_down)])
    g_out = _weight_grad(groups, dmx, "w_out")
    dattn, dsgu, d_ga, d_gs = _out_bwd(dmx, w_out_f, attn, sgu, sm["attn_out_norm"], sm["sgu_out_norm"])
    dus, dzs, d_ws, d_bst, d_lg, d_lb = _sgu_bwd(proj, dsgu, sm["sgu_ln_g"], sm["sgu_ln_b"], sm["w_spatial"], b_st)
    dq, dk, dv, (arrived["w_gate_up"], arrived["w_out"]) = _attn_bwd(proj, dattn, attn, lse, slopes,
                                                                     [_slabs(g_gu_t), _slabs(g_out)])
    dparts = [dq, dk, dv, dus, dzs]
    g_in_t = None
    for i, part in enumerate(dparts):
        g_in_t = _weight_grad(part, a, "w_in_%d" % i, into=g_in_t, row_tile=i, rows=PROJ, out_dtype=BF)
    grad_x, d_g1, (arrived["w_in"],) = _in_bwd(dparts, w_in_t, xs, dh1, sm["ln_pre_mix"], [_slabs(g_in_t)])
    gs = dict(ln_pre_mix=d_g1, sgu_ln_g=d_lg, sgu_ln_b=d_lb, w_spatial=d_ws, b_spatial=d_bst[:, :N_GROUPS].T,
              attn_out_norm=d_ga, sgu_out_norm=d_gs, ln_post_mix=d_pm, ln_pre_ffn=d_pf, ln_post_ffn=d_pff,
              b_pe_gate=db_peg)
    (arrived_small,) = _scatter_call([_pack_small(gs)], [False], "small_grad_exchange")

    res = {}
    for n in ROW_SHARDED:
        res[n] = _reduce_adamw(arrived[n], w[n][0], m[n][0], v[n][0], n)
    for n in COL_SHARDED:
        grad = _reduce(arrived[n], n).T
        res[n] = (grad, *_adamw(grad, w[n][0], m[n][0], v[n][0], n))
    small = _reduce_adamw(arrived_small, _pack_small(w), _pack_small(m), _pack_small(v), "small")
    small = [_unpack_small(t, shapes) for t in small]

    out = []
    for k in range(4):
        out += [res[n][k][None] if n in res else small[k][n] for n in WEIGHTS]
    return (loss, grad_x[None], *out)
```

```python
import math

import jax
import jax.numpy as jnp
from jax import lax
from jax.experimental import pallas as pl
from jax.experimental.pallas import tpu as pltpu

F32 = jnp.float32
BF = jnp.bfloat16


def SDS(shape, dtype):
    return pltpu.HBM(tuple(shape), dtype)

D_MODEL = 1024
ATTN_W = 512
SGU_W = 512
HEAD_DIM = 64
N_GROUPS = 4
GROUP_DIM = 128
CHUNK = 128
D_FF = 2816
PLE_DIM = 256
PROJ = 3 * ATTN_W + 2 * SGU_W
DILATIONS = (1, 4, 16)
QBLK = 128
EPS = 1e-6
NEG = -1e30
N_DEV = 8
LANES = 128

ADAM_LR = 0.001
ADAM_B1 = 0.9
ADAM_B2 = 0.999
ADAM_EPS = 1e-08
ADAM_WD = 0.01
ADAM_STEP = 10

SMALL_ROWS = 80
MIB = 2 ** 20
MESH_ID = pl.DeviceIdType.MESH
HBM = pl.BlockSpec(memory_space=pl.ANY)


def _params(sem, vmem_mib):
    return pltpu.CompilerParams(dimension_semantics=sem, vmem_limit_bytes=vmem_mib * MIB)


def _dot(a, b):
    return jnp.dot(a, b, preferred_element_type=F32)


def _dot_nt(a, b):
    return lax.dot_general(a, b, (((1,), (1,)), ((), ())), preferred_element_type=F32)


def _dot_tn(a, b):
    return lax.dot_general(a, b, (((0,), (0,)), ((), ())), preferred_element_type=F32)


def _rstd(x):
    return lax.rsqrt(jnp.mean(x * x, axis=-1, keepdims=True) + EPS)


def _rms_bwd(dy, x, g):
    r = _rstd(x)
    n = x * r
    dn = dy * g
    dx = r * (dn - n * jnp.mean(dn * n, axis=-1, keepdims=True))
    return dx, dy * n


def _colsum(v):
    return jnp.sum(v, axis=0, keepdims=True)


_G0 = math.sqrt(2.0 / math.pi)
_G1 = 0.044715


def _gelu(x):
    t = jnp.tanh(_G0 * (x + _G1 * x * x * x))
    return 0.5 * x * (1.0 + t), t


def _gelu_grad(x, t):
    return 0.5 * (1.0 + t) + 0.5 * x * (1.0 - t * t) * (_G0 * (1.0 + 3.0 * _G1 * x * x))


def _sigmoid(x):
    return 1.0 / (1.0 + jnp.exp(-x))


def _row(i):
    return (i, 0)


def _fixed(i):
    return (0, 0)


def _acc_init(step, *refs):
    @pl.when(step == 0)
    def _():
        for r in refs:
            r[...] = jnp.zeros_like(r)


FLIPS = [(dx, dy, dc) for dx in (0, 1) for dy in (0, 1) for dc in (0, 1)][1:]
DMA_SEMS = pltpu.SemaphoreType.DMA


def _mesh_pos():
    return lax.axis_index("x"), lax.axis_index("y"), lax.axis_index("c")


def _remote(src, dst, sems, n, to):
    return pltpu.make_async_remote_copy(src_ref=src, dst_ref=dst, send_sem=sems[0].at[n], recv_sem=sems[1].at[n],
                                        device_id=to, device_id_type=MESH_ID)


class _Scatter:
    def __init__(self, items, sems):
        x, y, c = _mesh_pos()
        me = 4 * x + 2 * y + c
        self.local, self.sends, self.arrivals = [], [], []
        for i, (src, dst, slabbed) in enumerate(items):
            self.local.append(pltpu.make_async_copy(src.at[me] if slabbed else src, dst.at[me], sems[2].at[i]))
            for k, (dx, dy, dc) in enumerate(FLIPS):
                to = (1 - x if dx else x, 1 - y if dy else y, 1 - c if dc else c)
                peer = 4 * to[0] + 2 * to[1] + to[2]
                out = src.at[peer] if slabbed else src
                self.sends.append(_remote(out, dst.at[me], sems, 7 * i + k, to))
                self.arrivals.append(_remote(out, dst.at[peer], sems, 7 * i + k, to))

    def start(self):
        for cp in self.local + self.sends:
            cp.start()

    def wait(self):
        for cp in self.arrivals:
            cp.wait_recv()
        for cp in self.sends:
            cp.wait_send()
        for cp in self.local:
            cp.wait()


def _scatter_sems(n):
    return [DMA_SEMS((7 * n,)), DMA_SEMS((7 * n,)), DMA_SEMS((n,))]


class _Gather:
    def __init__(self, items, sems):
        x, y, c = _mesh_pos()
        me, sibling = (x, y, c), (x, y, 1 - c)
        chips = [(1 - x, y), (x, 1 - y), (1 - x, 1 - y)]
        self.first, self.passed, self.from_chips, self.rest, self.local = [], [], [], [], []
        for i, (src, dst) in enumerate(items):
            def slot(p, dst=dst):
                return dst.at[4 * p[0] + 2 * p[1] + p[2]]

            def copy(k, block, to, own=False, i=i, src=src, slot=slot):
                return _remote(src if own else slot(block), slot(block), sems, 7 * i + k, to)

            self.local.append(pltpu.make_async_copy(src, slot(me), sems[2].at[i]))
            self.first.append(copy(0, me, sibling, own=True))
            self.first += [copy(1 + j, me, (*chip, c), own=True) for j, chip in enumerate(chips)]
            self.passed += [copy(4 + j, (*chip, c), sibling) for j, chip in enumerate(chips)]
            self.from_chips += [copy(1 + j, (*chip, c), me) for j, chip in enumerate(chips)]
            self.rest.append(copy(0, sibling, me))
            self.rest += [copy(4 + j, (*chip, 1 - c), me) for j, chip in enumerate(chips)]

    def start(self):
        for cp in self.local + self.first:
            cp.start()

    def forward(self):
        for arrived, onward in zip(self.from_chips, self.passed):
            arrived.wait_recv()
            onward.start()

    def finish(self):
        for cp in self.rest:
            cp.wait_recv()
        for cp in self.first + self.passed:
            cp.wait_send()
        for cp in self.local:
            cp.wait()


def _all_gather(shard, name):
    def body(x_ref, out_ref, *sems):
        g = _Gather([(x_ref, out_ref)], sems)
        g.start()
        g.forward()
        g.finish()

    return pl.pallas_call(
        body, out_shape=SDS((N_DEV,) + shard.shape, shard.dtype), in_specs=[HBM], out_specs=HBM,
        scratch_shapes=_scatter_sems(1), name=name)(shard)


def _scatter_call(srcs, slabbed, name):
    n = len(srcs)

    def body(*refs):
        sc = _Scatter(list(zip(refs[:n], refs[n:2 * n], slabbed)), refs[2 * n:])
        sc.start()
        sc.wait()

    shapes = [SDS(s.shape if sl else (N_DEV,) + s.shape, s.dtype) for s, sl in zip(srcs, slabbed)]
    return pl.pallas_call(body, out_shape=shapes, in_specs=[HBM] * n, out_specs=[HBM] * n,
                          scratch_shapes=_scatter_sems(n), name=name)(*srcs)


def _in_proj(x, g1, w_in_t, tm=256):
    T = x.shape[0]

    def body(x_ref, g_ref, w_ref, proj_ref, a_ref):
        xv = x_ref[...]
        a = (xv * _rstd(xv) * g_ref[...]).astype(BF)
        a_ref[...] = a
        proj_ref[...] = _dot_nt(a, w_ref[...])

    return pl.pallas_call(
        body, grid=(T // tm,),
        in_specs=[pl.BlockSpec((tm, D_MODEL), _row), pl.BlockSpec((1, D_MODEL), _fixed),
                  pl.BlockSpec((PROJ, D_MODEL), _fixed)],
        out_specs=[pl.BlockSpec((tm, PROJ), _row), pl.BlockSpec((tm, D_MODEL), _row)],
        out_shape=[SDS((T, PROJ), F32), SDS((T, D_MODEL), BF)],
        name="in_proj", compiler_params=_params(("arbitrary",), 40))(x, g1, w_in_t)


ATTN_GROUP = 16


def _attn_bias(sl_ref, bias_ref):
    qi = lax.broadcasted_iota(jnp.int32, (QBLK, QBLK), 0)
    kj = lax.broadcasted_iota(jnp.int32, (QBLK, QBLK), 1)
    step = qi - kj
    for di, d in enumerate(DILATIONS):
        for j in range(2):
            sl = sl_ref[0, j:j + 1, :]
            cur = jnp.where(step >= 0, -sl * (step * d).astype(F32), NEG)
            prev = jnp.where(step <= 0, -sl * ((step + QBLK) * d).astype(F32), NEG)
            rows = slice(j * QBLK, (j + 1) * QBLK)
            bias_ref[di * 2, rows, :QBLK] = prev
            bias_ref[di * 2, rows, QBLK:] = cur
            bias_ref[di * 2 + 1, rows, :QBLK] = jnp.full((QBLK, QBLK), NEG, F32)
            bias_ref[di * 2 + 1, rows, QBLK:] = cur


def _stack_heads(x, lo):
    return jnp.concatenate([jnp.where(lo, x, 0.0), jnp.where(lo, 0.0, x)], axis=0).astype(BF)


def _unstack_heads(x, lo):
    return jnp.where(lo, x[:QBLK], x[QBLK:])


def _attn_rows(start, d, blocks=1):
    if d == 1:
        return pl.ds(pl.multiple_of(start, QBLK), blocks * QBLK)
    return pl.ds(start, blocks * QBLK, stride=d)


def _attn_group_index(i, nblk, d, group):
    per = nblk // d // group
    r = i // per
    n0 = (i % per) * group
    start = r + (d * QBLK) * n0
    pstart = jnp.maximum(start - d * QBLK, r)
    return start, pstart, n0 == 0


def _attn_plan(nblk, d):
    group = min(ATTN_GROUP, nblk // d)
    return group, max(1, min(ATTN_GROUP // group, d))


def _attn_fwd(proj, slopes, to_gather):
    T = proj.shape[0]
    nblk = T // QBLK
    big = dict(pipeline_mode=pl.Buffered(1))
    n_g = len(to_gather)

    def body(q_ref, k_ref, v_ref, sl_ref, *rest):
        srcs, (o_ref, m_ref), dsts = rest[:n_g], rest[n_g:n_g + 2], rest[n_g + 2:2 * n_g + 2]
        sems, (l_ref, bias_ref) = rest[2 * n_g + 2:2 * n_g + 5], rest[2 * n_g + 5:]
        h = pl.program_id(0)

        @pl.when(h == 0)
        def _():
            _Gather(list(zip(srcs, dsts)), sems).start()

        @pl.when(h == 2)
        def _():
            _Gather(list(zip(srcs, dsts)), sems).forward()

        _attn_bias(sl_ref, bias_ref)
        lo = lax.broadcasted_iota(jnp.int32, (1, LANES), 1) < HEAD_DIM

        for di, d in enumerate(DILATIONS):
            group, segs = _attn_plan(nblk, d)

            def step(i, carry, segs=segs, **kw):
                for s in range(segs):
                    segment(i * segs + s, **kw)
                return carry

            def segment(i, d=d, di=di, group=group):
                start, pstart, first = _attn_group_index(i, nblk, d, group)
                rows, prows = _attn_rows(start, d, group), _attn_rows(pstart, d)
                q = q_ref[rows, :] * (HEAD_DIM ** -0.5)
                k = jnp.concatenate([k_ref[prows, :], k_ref[rows, :]], axis=0).astype(BF)
                v = jnp.concatenate([v_ref[prows, :], v_ref[rows, :]], axis=0).astype(BF)
                for b in range(group):
                    qb = slice(b * QBLK, (b + 1) * QBLK)
                    kb = slice(b * QBLK, (b + 2) * QBLK)
                    bias = bias_ref[di * 2 + first.astype(jnp.int32)] if b == 0 else bias_ref[di * 2]
                    s = _dot_nt(_stack_heads(q[qb], lo), k[kb]) + bias
                    m = jnp.max(s, axis=-1, keepdims=True)
                    pr = jnp.exp(s - m)
                    m_b = _unstack_heads(m, lo)
                    l_b = _unstack_heads(jnp.sum(pr, axis=-1, keepdims=True), lo)
                    o_b = _unstack_heads(_dot(pr.astype(BF), v[kb]), lo)
                    out = _attn_rows(start + b * (d * QBLK), d)
                    if di == 0:
                        m_ref[out, :] = m_b
                        l_ref[out, :] = l_b
                        o_ref[out, :] = o_b
                    else:
                        m_o = m_ref[out, :]
                        m_n = jnp.maximum(m_o, m_b)
                        wa, wb = jnp.exp(m_o - m_n), jnp.exp(m_b - m_n)
                        m_ref[out, :] = m_n
                        l_ref[out, :] = wa * l_ref[out, :] + wb * l_b
                        o_ref[out, :] = wa * o_ref[out, :] + wb * o_b

            lax.fori_loop(0, nblk // (group * segs), step, 0)

        def finish(i, carry):
            rows = pl.ds(pl.multiple_of(i * QBLK, QBLK), QBLK)
            l = l_ref[rows, :]
            o_ref[rows, :] = o_ref[rows, :] / l
            m_ref[rows, :] = m_ref[rows, :] + jnp.log(l)
            return carry

        lax.fori_loop(0, nblk, finish, 0)

        @pl.when(h == pl.num_programs(0) - 1)
        def _():
            _Gather(list(zip(srcs, dsts)), sems).finish()

    col = lambda base: pl.BlockSpec((T, LANES), lambda h: (0, base + h), **big)
    tok = pl.BlockSpec((T, LANES), lambda h: (0, h))
    outs = pl.pallas_call(
        body, grid=(4,),
        in_specs=[col(0), col(4), col(8), pl.BlockSpec((1, 8, LANES), lambda h: (h, 0, 0))] + [HBM] * n_g,
        out_specs=[tok, tok] + [HBM] * n_g,
        out_shape=[SDS((T, ATTN_W), F32), SDS((T, ATTN_W), F32)]
        + [SDS((N_DEV,) + g.shape, g.dtype) for g in to_gather],
        scratch_shapes=_scatter_sems(n_g) + [pltpu.VMEM((T, LANES), F32), pltpu.VMEM((6, 2 * QBLK, 2 * QBLK), F32)],
        name="attn_fwd", compiler_params=_params(("arbitrary",), 48))(proj, proj, proj, slopes, *to_gather)
    return outs[0], outs[1], outs[2:]


def _sgu_norm(zv, ln_g, ln_b):
    gz, tz = _gelu(zv)
    mu = jnp.mean(gz, axis=-1, keepdims=True)
    xc = gz - mu
    rs = lax.rsqrt(jnp.mean(xc * xc, axis=-1, keepdims=True) + EPS)
    xhat = xc * rs
    return xhat * ln_g + ln_b, xhat, rs, tz


def _causal(w):
    i = lax.broadcasted_iota(jnp.int32, (CHUNK, CHUNK), 0)
    j = lax.broadcasted_iota(jnp.int32, (CHUNK, CHUNK), 1)
    return jnp.where(i >= j, w, 0.0)


def _sgu_fwd(proj, ln_g, ln_b, w_s, b_st, tm=512):
    T = proj.shape[0]

    def body(u_ref, z_ref, g_ref, b_ref, w_ref, bs_ref, out_ref):
        for g in range(N_GROUPS):
            wm = _causal(w_ref[g]).astype(BF)
            cols = slice(g * GROUP_DIM, (g + 1) * GROUP_DIM)
            for c in range(tm // CHUNK):
                rows = slice(c * CHUNK, (c + 1) * CHUNK)
                zn, _, _, _ = _sgu_norm(z_ref[rows, cols], g_ref[...], b_ref[...])
                mixed = _dot(wm, zn.astype(BF)) + bs_ref[:, g:g + 1]
                gu, _ = _gelu(u_ref[rows, cols])
                out_ref[rows, cols] = gu * mixed

    return pl.pallas_call(
        body, grid=(T // tm,),
        in_specs=[pl.BlockSpec((tm, SGU_W), lambda i: (i, 3)), pl.BlockSpec((tm, SGU_W), lambda i: (i, 4)),
                  pl.BlockSpec((1, GROUP_DIM), _fixed), pl.BlockSpec((1, GROUP_DIM), _fixed),
                  pl.BlockSpec((N_GROUPS, CHUNK, CHUNK), lambda i: (0, 0, 0)), pl.BlockSpec((CHUNK, LANES), _fixed)],
        out_specs=pl.BlockSpec((tm, SGU_W), _row),
        out_shape=SDS((T, SGU_W), F32),
        name="sgu_fwd", compiler_params=_params(("arbitrary",), 32))(proj, proj, ln_g, ln_b, w_s, b_st)


def _out_proj(attn, sgu, x, g_a, g_s, w_out, g_pm, g_pf, tm=256):
    T = x.shape[0]

    def body(a_ref, s_ref, x_ref, ga_ref, gs_ref, w_ref, gpm_ref, gpf_ref, grp_ref, mixed_ref, h1_ref, f_ref):
        av, sv = a_ref[...], s_ref[...]
        an = (av * _rstd(av) * ga_ref[...]).astype(BF)
        sn = (sv * _rstd(sv) * gs_ref[...]).astype(BF)
        grp_ref[:, :ATTN_W] = an
        grp_ref[:, ATTN_W:] = sn
        mixed = _dot(an, w_ref[:ATTN_W, :]) + _dot(sn, w_ref[ATTN_W:, :])
        mixed_ref[...] = mixed
        h1 = x_ref[...] + mixed * _rstd(mixed) * gpm_ref[...]
        h1_ref[...] = h1
        f_ref[...] = (h1 * _rstd(h1) * gpf_ref[...]).astype(BF)

    tok = lambda w: pl.BlockSpec((tm, w), _row)
    vec = lambda w: pl.BlockSpec((1, w), _fixed)
    return pl.pallas_call(
        body, grid=(T // tm,),
        in_specs=[tok(ATTN_W), tok(SGU_W), tok(D_MODEL), vec(ATTN_W), vec(SGU_W),
                  pl.BlockSpec((D_MODEL, D_MODEL), _fixed), vec(D_MODEL), vec(D_MODEL)],
        out_specs=[tok(D_MODEL)] * 4,
        out_shape=[SDS((T, D_MODEL), BF), SDS((T, D_MODEL), F32), SDS((T, D_MODEL), F32), SDS((T, D_MODEL), BF)],
        name="out_proj", compiler_params=_params(("arbitrary",), 32))(attn, sgu, x, g_a, g_s, w_out, g_pm, g_pf)


FF_TILE = 1408
FF_TILES = D_FF // FF_TILE


def _gate_up(f, w_gu_t, tm=512):
    T = f.shape[0]
    tn = FF_TILE

    def body(f_ref, wg_ref, wu_ref, g_ref, u_ref, act_ref):
        fv = f_ref[...]
        g = _dot_nt(fv, wg_ref[...])
        u = _dot_nt(fv, wu_ref[...])
        g_ref[...] = g.astype(BF)
        u_ref[...] = u.astype(BF)
        act_ref[...] = (g * _sigmoid(g) * u).astype(BF)

    ospec = pl.BlockSpec((tm, tn), lambda j, i: (i, j))
    return pl.pallas_call(
        body, grid=(FF_TILES, T // tm),
        in_specs=[pl.BlockSpec((tm, D_MODEL), lambda j, i: (i, 0)), pl.BlockSpec((tn, D_MODEL), lambda j, i: (j, 0)),
                  pl.BlockSpec((tn, D_MODEL), lambda j, i: (j + FF_TILES, 0))],
        out_specs=[ospec] * 3, out_shape=[SDS((T, D_FF), BF)] * 3,
        name="gate_up", compiler_params=_params(("arbitrary", "arbitrary"), 40))(f, w_gu_t, w_gu_t)


def _down_proj(act, w_down, h1, g_pff, tm=256):
    T = act.shape[0]

    def body(a_ref, w_ref, h1_ref, g_ref, y_ref, h2_ref):
        y = _dot(a_ref[...], w_ref[...])
        y_ref[...] = y
        h2_ref[...] = h1_ref[...] + y * _rstd(y) * g_ref[...]

    return pl.pallas_call(
        body, grid=(T // tm,),
        in_specs=[pl.BlockSpec((tm, D_FF), _row), pl.BlockSpec((D_FF, D_MODEL), _fixed),
                  pl.BlockSpec((tm, D_MODEL), _row), pl.BlockSpec((1, D_MODEL), _fixed)],
        out_specs=[pl.BlockSpec((tm, D_MODEL), _row)] * 2,
        out_shape=[SDS((T, D_MODEL), F32)] * 2,
        name="down_proj", compiler_params=_params(("arbitrary",), 40))(act, w_down, h1, g_pff)


def _pe_and_loss(h2, p, target, w_peg, b_peg, w_pep_t, tm=256):
    T = h2.shape[0]

    def body(h2_ref, p_ref, t_ref, wg_ref, b_ref, wp_ref,
             dh3_ref, dpp_ref, dpre_ref, h2b_ref, pb_ref, loss_ref, db_ref):
        _acc_init(pl.program_id(0), loss_ref, db_ref)
        h2v = h2_ref[...]
        h2b = h2v.astype(BF)
        pb = p_ref[...].astype(BF)
        h2b_ref[...] = h2b
        pb_ref[...] = pb
        gate = _sigmoid(_dot(h2b, wg_ref[...]) + b_ref[...])
        pp = _dot_nt(pb, wp_ref[...])
        diff = h2v + gate * pp - t_ref[...]
        loss_ref[...] += _colsum(diff * diff)
        dh3 = diff * (1.0 / D_MODEL)
        dh3_ref[...] = dh3
        dpp_ref[...] = (dh3 * gate).astype(BF)
        dpre = dh3 * pp * (gate * (1.0 - gate))
        dpre_ref[...] = dpre.astype(BF)
        db_ref[...] += _colsum(dpre)

    tok = lambda w: pl.BlockSpec((tm, w), _row)
    vec = pl.BlockSpec((1, D_MODEL), _fixed)
    return pl.pallas_call(
        body, grid=(T // tm,),
        in_specs=[tok(D_MODEL), tok(PLE_DIM), tok(D_MODEL), pl.BlockSpec((D_MODEL, D_MODEL), _fixed), vec,
                  pl.BlockSpec((D_MODEL, PLE_DIM), _fixed)],
        out_specs=[tok(D_MODEL), tok(D_MODEL), tok(D_MODEL), tok(D_MODEL), tok(PLE_DIM), vec, vec],
        out_shape=[SDS((T, D_MODEL), F32), SDS((T, D_MODEL), BF), SDS((T, D_MODEL), BF), SDS((T, D_MODEL), BF),
                   SDS((T, PLE_DIM), BF), SDS((1, D_MODEL), F32), SDS((1, D_MODEL), F32)],
        name="pe_and_loss", compiler_params=_params(("arbitrary",), 32))(h2, p, target, w_peg, b_peg, w_pep_t)


def _weight_grad(a, dy, name, into=None, row_tile=0, rows=None, out_dtype=F32, tk=512):
    T, ka = a.shape
    n = dy.shape[1]
    tka = FF_TILE if ka == D_FF else min(ka, 1024)
    tn = min(n, 1024)
    rows = ka if rows is None else rows
    direct = out_dtype == F32

    def body(a_ref, dy_ref, *rest):
        out_ref, acc_ref = (rest[-1], rest[-1]) if direct else rest[-2:]
        _acc_init(pl.program_id(2), acc_ref)
        acc_ref[...] += _dot_tn(a_ref[...].astype(BF), dy_ref[...].astype(BF))
        if not direct:
            @pl.when(pl.program_id(2) == pl.num_programs(2) - 1)
            def _():
                out_ref[...] = acc_ref[...].astype(out_dtype)

    carried = [] if into is None else [into]
    return pl.pallas_call(
        body, grid=(ka // tka, n // tn, T // tk),
        in_specs=[pl.BlockSpec((tk, tka), lambda i, j, k: (k, i)), pl.BlockSpec((tk, tn), lambda i, j, k: (k, j))]
        + [HBM] * len(carried),
        out_specs=pl.BlockSpec((tka, tn), lambda i, j, k: (i + row_tile, j)),
        out_shape=SDS((rows, n), out_dtype), input_output_aliases={2: 0} if carried else {},
        scratch_shapes=[] if direct else [pltpu.VMEM((tka, tn), F32)],
        name="grad_" + name, compiler_params=_params(("arbitrary",) * 3, 40))(a, dy, *carried)


def _pe_bwd(dpre, w_peg, dh3, y, g_pff, tm=256):
    T = y.shape[0]

    def body(dp_ref, w_ref, dh3_ref, y_ref, g_ref, dh2_ref, dy_ref, dg_ref):
        _acc_init(pl.program_id(0), dg_ref)
        dh2 = dh3_ref[...] + _dot_nt(dp_ref[...], w_ref[...])
        dh2_ref[...] = dh2
        dy, dg = _rms_bwd(dh2, y_ref[...], g_ref[...])
        dy_ref[...] = dy.astype(BF)
        dg_ref[...] += _colsum(dg)

    tok = pl.BlockSpec((tm, D_MODEL), _row)
    vec = pl.BlockSpec((1, D_MODEL), _fixed)
    return pl.pallas_call(
        body, grid=(T // tm,),
        in_specs=[tok, pl.BlockSpec((D_MODEL, D_MODEL), _fixed), tok, tok, vec],
        out_specs=[tok, tok, vec],
        out_shape=[SDS((T, D_MODEL), F32), SDS((T, D_MODEL), BF), SDS((1, D_MODEL), F32)],
        name="pe_bwd", compiler_params=_params(("arbitrary",), 32))(dpre, w_peg, dh3, y, g_pff)


def _down_bwd(dy, w_down, g, u, to_send, tm=512):
    T = dy.shape[0]
    tn = FF_TILE
    n_s = len(to_send)

    def body(dy_ref, w_ref, g_ref, u_ref, *rest):
        srcs, (dg_ref, du_ref), dsts, sems = rest[:n_s], rest[n_s:n_s + 2], rest[n_s + 2:2 * n_s + 2], rest[2 * n_s + 2:]
        j, i = pl.program_id(0), pl.program_id(1)
        items = list(zip(srcs, dsts, [True] * n_s))

        @pl.when((j == 0) & (i == 0))
        def _():
            _Scatter(items, sems).start()

        dact = _dot_nt(dy_ref[...], w_ref[...])
        gv = g_ref[...].astype(F32)
        uv = u_ref[...].astype(F32)
        s = _sigmoid(gv)
        dg_ref[...] = (dact * uv * (s * (1.0 + gv * (1.0 - s)))).astype(BF)
        du_ref[...] = (dact * (gv * s)).astype(BF)

        @pl.when((j == pl.num_programs(0) - 1) & (i == pl.num_programs(1) - 1))
        def _():
            _Scatter(items, sems).wait()

    tile = pl.BlockSpec((tm, tn), lambda j, i: (i, j))
    outs = pl.pallas_call(
        body, grid=(FF_TILES, T // tm),
        in_specs=[pl.BlockSpec((tm, D_MODEL), lambda j, i: (i, 0)), pl.BlockSpec((tn, D_MODEL), lambda j, i: (j, 0)),
                  tile, tile] + [HBM] * n_s,
        out_specs=[tile, tile] + [HBM] * n_s,
        out_shape=[SDS((T, D_FF), BF)] * 2 + [SDS(s.shape, s.dtype) for s in to_send],
        scratch_shapes=_scatter_sems(n_s),
        name="down_bwd", compiler_params=_params(("arbitrary", "arbitrary"), 40))(dy, w_down, g, u, *to_send)
    return outs[0], outs[1], outs[2:]


def _ffn_in_bwd(dg, du, w_gu_t, h1, dh2, mixed, g_pf, g_pm, to_send, tm=256):
    T = h1.shape[0]
    n_s = len(to_send)

    def body(dg_ref, du_ref, wg_ref, wu_ref, h1_ref, dh2_ref, mx_ref, gpf_ref, gpm_ref, *rest):
        srcs, outs, dsts, sems = rest[:n_s], rest[n_s:n_s + 4], rest[n_s + 4:2 * n_s + 4], rest[2 * n_s + 4:]
        dh1_ref, dmx_ref, dgpf_ref, dgpm_ref = outs
        i = pl.program_id(0)
        items = list(zip(srcs, dsts, [True] * n_s))
        _acc_init(i, dgpf_ref, dgpm_ref)

        @pl.when(i == 0)
        def _():
            _Scatter(items, sems).start()

        df = _dot(dg_ref[...], wg_ref[...]) + _dot(du_ref[...], wu_ref[...])
        dx, dgf = _rms_bwd(df, h1_ref[...], gpf_ref[...])
        dh1 = dh2_ref[...] + dx
        dh1_ref[...] = dh1
        dmx, dgm = _rms_bwd(dh1, mx_ref[...], gpm_ref[...])
        dmx_ref[...] = dmx.astype(BF)
        dgpf_ref[...] += _colsum(dgf)
        dgpm_ref[...] += _colsum(dgm)

        @pl.when(i == pl.num_programs(0) - 1)
        def _():
            _Scatter(items, sems).wait()

    tok = lambda w: pl.BlockSpec((tm, w), _row)
    vec = pl.BlockSpec((1, D_MODEL), _fixed)
    outs = pl.pallas_call(
        body, grid=(T // tm,),
        in_specs=[tok(D_FF), tok(D_FF), pl.BlockSpec((D_FF, D_MODEL), lambda i: (0, 0)),
                  pl.BlockSpec((D_FF, D_MODEL), lambda i: (1, 0)), tok(D_MODEL), tok(D_MODEL), tok(D_MODEL), vec, vec]
        + [HBM] * n_s,
        out_specs=[tok(D_MODEL), tok(D_MODEL), vec, vec] + [HBM] * n_s,
        out_shape=[SDS((T, D_MODEL), F32), SDS((T, D_MODEL), BF), SDS((1, D_MODEL), F32), SDS((1, D_MODEL), F32)]
        + [SDS(s.shape, s.dtype) for s in to_send],
        scratch_shapes=_scatter_sems(n_s),
        name="ffn_in_bwd", compiler_params=_params(("arbitrary",), 48))(
            dg, du, w_gu_t, w_gu_t, h1, dh2, mixed, g_pf, g_pm, *to_send)
    return outs[0], outs[1], outs[2], outs[3], outs[4:]


def _out_bwd(dmx, w_out, attn, sgu, g_a, g_s, tm=256):
    T = attn.shape[0]

    def body(dm_ref, w_ref, a_ref, s_ref, ga_ref, gs_ref, da_ref, ds_ref, dga_ref, dgs_ref):
        _acc_init(pl.program_id(0), dga_ref, dgs_ref)
        dgr = _dot_nt(dm_ref[...], w_ref[...])
        da, dga = _rms_bwd(dgr[:, :ATTN_W], a_ref[...], ga_ref[...])
        ds, dgs = _rms_bwd(dgr[:, ATTN_W:], s_ref[...], gs_ref[...])
        da_ref[...] = da
        ds_ref[...] = ds
        dga_ref[...] += _colsum(dga)
        dgs_ref[...] += _colsum(dgs)

    tok = lambda w: pl.BlockSpec((tm, w), _row)
    vec = lambda w: pl.BlockSpec((1, w), _fixed)
    return pl.pallas_call(
        body, grid=(T // tm,),
        in_specs=[tok(D_MODEL), pl.BlockSpec((D_MODEL, D_MODEL), _fixed), tok(ATTN_W), tok(SGU_W), vec(ATTN_W), vec(SGU_W)],
        out_specs=[tok(ATTN_W), tok(SGU_W), vec(ATTN_W), vec(SGU_W)],
        out_shape=[SDS((T, ATTN_W), F32), SDS((T, SGU_W), F32), SDS((1, ATTN_W), F32), SDS((1, SGU_W), F32)],
        name="out_bwd", compiler_params=_params(("arbitrary",), 32))(dmx, w_out, attn, sgu, g_a, g_s)


def _sgu_bwd(proj, dsgu, ln_g, ln_b, w_s, b_st, tm=512):
    T = proj.shape[0]

    def body(u_ref, z_ref, ds_ref, g_ref, b_ref, w_ref, bs_ref,
             du_ref, dz_ref, dw_ref, dbs_ref, dlg_ref, dlb_ref, dbacc_ref):
        step = pl.program_id(0)
        _acc_init(step, dw_ref, dbs_ref, dlg_ref, dlb_ref, dbacc_ref)
        lng, lnb = g_ref[...], b_ref[...]
        for g in range(N_GROUPS):
            wm = _causal(w_ref[g]).astype(BF)
            cols = slice(g * GROUP_DIM, (g + 1) * GROUP_DIM)
            for c in range(tm // CHUNK):
                rows = slice(c * CHUNK, (c + 1) * CHUNK)
                zv, uv, dout = z_ref[rows, cols], u_ref[rows, cols], ds_ref[rows, cols]
                zn, xhat, rs, tz = _sgu_norm(zv, lng, lnb)
                znb = zn.astype(BF)
                mixed = _dot(wm, znb) + bs_ref[:, g:g + 1]
                gu, tu = _gelu(uv)
                du_ref[rows, cols] = (dout * mixed * _gelu_grad(uv, tu)).astype(BF)
                dmix = dout * gu
                dmb = dmix.astype(BF)
                dw_ref[g] += _causal(_dot_nt(dmb, znb))
                dbacc_ref[g] += dmix
                dzn = _dot_tn(wm, dmb)
                dlg_ref[...] += _colsum(dzn * xhat)
                dlb_ref[...] += _colsum(dzn)
                dxh = dzn * lng
                dgz = rs * (dxh - jnp.mean(dxh, axis=-1, keepdims=True)
                            - xhat * jnp.mean(dxh * xhat, axis=-1, keepdims=True))
                dz_ref[rows, cols] = (dgz * _gelu_grad(zv, tz)).astype(BF)

        @pl.when(step == pl.num_programs(0) - 1)
        def _():
            lane = lax.broadcasted_iota(jnp.int32, (CHUNK, LANES), 1)
            acc = jnp.zeros((CHUNK, LANES), F32)
            for g in range(N_GROUPS):
                acc = jnp.where(lane == g, jnp.sum(dbacc_ref[g], axis=-1, keepdims=True), acc)
            dbs_ref[...] = acc

    tok = pl.BlockSpec((tm, SGU_W), _row)
    vec = pl.BlockSpec((1, GROUP_DIM), _fixed)
    wsp = pl.BlockSpec((N_GROUPS, CHUNK, CHUNK), lambda i: (0, 0, 0))
    sq = pl.BlockSpec((CHUNK, LANES), _fixed)
    return pl.pallas_call(
        body, grid=(T // tm,),
        in_specs=[pl.BlockSpec((tm, SGU_W), lambda i: (i, 3)), pl.BlockSpec((tm, SGU_W), lambda i: (i, 4)), tok,
                  vec, vec, wsp, sq],
        out_specs=[tok, tok, wsp, sq, vec, vec],
        out_shape=[SDS((T, SGU_W), BF), SDS((T, SGU_W), BF), SDS((N_GROUPS, CHUNK, CHUNK), F32),
                   SDS((CHUNK, LANES), F32), SDS((1, GROUP_DIM), F32), SDS((1, GROUP_DIM), F32)],
        scratch_shapes=[pltpu.VMEM((N_GROUPS, CHUNK, LANES), F32)],
        name="sgu_bwd", compiler_params=_params(("arbitrary",), 32))(proj, proj, dsgu, ln_g, ln_b, w_s, b_st)


def _attn_bwd(proj, do, o, lse, slopes, to_send):
    T = proj.shape[0]
    nblk = T // QBLK
    big = dict(pipeline_mode=pl.Buffered(1))
    n_s = len(to_send)

    def body(q_ref, k_ref, v_ref, do_ref, o_ref, l_ref, sl_ref, *rest):
        srcs, (dq_ref, dk_ref, dv_ref), dsts = rest[:n_s], rest[n_s:n_s + 3], rest[n_s + 3:2 * n_s + 3]
        sems, bias_ref = rest[2 * n_s + 3:2 * n_s + 6], rest[2 * n_s + 6]
        h = pl.program_id(0)
        items = list(zip(srcs, dsts, [True] * n_s))

        @pl.when(h == 0)
        def _():
            _Scatter(items, sems).start()

        _attn_bias(sl_ref, bias_ref)
        lo = lax.broadcasted_iota(jnp.int32, (1, LANES), 1) < HEAD_DIM
        scale = HEAD_DIM ** -0.5
        dq_ref[...] = jnp.zeros_like(dq_ref)
        dk_ref[...] = jnp.zeros_like(dk_ref)
        dv_ref[...] = jnp.zeros_like(dv_ref)

        for di, d in enumerate(DILATIONS):
            group, segs = _attn_plan(nblk, d)

            def step(i, carry, segs=segs, **kw):
                for s in range(segs):
                    segment(i * segs + s, **kw)
                return carry

            def segment(i, d=d, di=di, group=group):
                start, pstart, first = _attn_group_index(i, nblk, d, group)
                rows, prows = _attn_rows(start, d, group), _attn_rows(pstart, d)
                q = q_ref[rows, :] * scale
                k = jnp.concatenate([k_ref[prows, :], k_ref[rows, :]], axis=0).astype(BF)
                v = jnp.concatenate([v_ref[prows, :], v_ref[rows, :]], axis=0).astype(BF)
                dov = do_ref[rows, :]
                prod = dov * o_ref[rows, :]
                lse_g = l_ref[rows, :]
                masks = [lo, ~lo]
                qm = [jnp.where(masks[j], q, 0.0).astype(BF) for j in range(2)]
                dom = [jnp.where(masks[j], dov, 0.0).astype(BF) for j in range(2)]
                for b in range(group):
                    qb = slice(b * QBLK, (b + 1) * QBLK)
                    kb = slice(b * QBLK, (b + 2) * QBLK)
                    which = di * 2 + first.astype(jnp.int32) if b == 0 else di * 2
                    dq_parts = []
                    dk_b = dv_b = None
                    for j in range(2):
                        bias = bias_ref[which, j * QBLK:(j + 1) * QBLK, :]
                        delta = jnp.sum(jnp.where(masks[j], prod[qb], 0.0), axis=-1, keepdims=True)
                        lj = lse_g[qb, j * HEAD_DIM:j * HEAD_DIM + 1]
                        pr = jnp.exp(_dot_nt(qm[j][qb], k[kb]) + bias - lj)
                        ds = (pr * (_dot_nt(dom[j][qb], v[kb]) - delta)).astype(BF)
                        dq_parts.append(_dot(ds, k[kb]))
                        t_k, t_v = _dot_tn(ds, qm[j][qb]), _dot_tn(pr.astype(BF), dom[j][qb])
                        dk_b = t_k if j == 0 else dk_b + t_k
                        dv_b = t_v if j == 0 else dv_b + t_v
                    own = _attn_rows(start + b * (d * QBLK), d)
                    dq_ref[own, :] += jnp.where(lo, dq_parts[0], dq_parts[1]) * scale
                    if b == 0:
                        dk_ref[prows, :] += dk_b[:QBLK]
                        dv_ref[prows, :] += dv_b[:QBLK]
                        dk_ref[own, :] += dk_b[QBLK:]
                        dv_ref[own, :] += dv_b[QBLK:]
                    else:
                        two = _attn_rows(start + (b - 1) * (d * QBLK), d, 2)
                        dk_ref[two, :] += dk_b
                        dv_ref[two, :] += dv_b

            lax.fori_loop(0, nblk // (group * segs), step, 0)

        @pl.when(h == pl.num_programs(0) - 1)
        def _():
            _Scatter(items, sems).wait()

    col = lambda base: pl.BlockSpec((T, LANES), lambda h: (0, base + h), **big)
    out = pl.BlockSpec((T, LANES), lambda h: (0, h))
    outs = pl.pallas_call(
        body, grid=(4,),
        in_specs=[col(0), col(4), col(8), col(0), col(0), col(0), pl.BlockSpec((1, 8, LANES), lambda h: (h, 0, 0))]
        + [HBM] * n_s,
        out_specs=[out, out, out] + [HBM] * n_s,
        out_shape=[SDS((T, ATTN_W), F32)] * 3 + [SDS(s.shape, s.dtype) for s in to_send],
        scratch_shapes=_scatter_sems(n_s) + [pltpu.VMEM((6, 2 * QBLK, 2 * QBLK), F32)],
        name="attn_bwd", compiler_params=_params(("arbitrary",), 60))(proj, proj, proj, do, o, lse, slopes, *to_send)
    return outs[0], outs[1], outs[2], outs[3:]


def _in_bwd(dparts, w_in_t, x, dh1, g1, to_send, tm=256):
    T = x.shape[0]
    n = len(dparts)
    n_s = len(to_send)
    w = ATTN_W

    def body(*refs):
        d_refs, (w_ref, x_ref, dh1_ref, g_ref), rest = refs[:n], refs[n:n + 4], refs[n + 4:]
        srcs, (dx_ref, dg_ref), dsts, sems = rest[:n_s], rest[n_s:n_s + 2], rest[n_s + 2:2 * n_s + 2], rest[2 * n_s + 2:]
        step = pl.program_id(0)
        items = list(zip(srcs, dsts, [True] * n_s))
        _acc_init(step, dg_ref)

        @pl.when(step == 0)
        def _():
            _Scatter(items, sems).start()

        da = None
        for i, r in enumerate(d_refs):
            t = _dot(r[...].astype(BF), w_ref[i * w:(i + 1) * w, :])
            da = t if da is None else da + t
        dx, dg = _rms_bwd(da, x_ref[...], g_ref[...])
        dx_ref[...] = dh1_ref[...] + dx
        dg_ref[...] += _colsum(dg)

        @pl.when(step == pl.num_programs(0) - 1)
        def _():
            _Scatter(items, sems).wait()

    tok = lambda c: pl.BlockSpec((tm, c), _row)
    vec = pl.BlockSpec((1, D_MODEL), _fixed)
    outs = pl.pallas_call(
        body, grid=(T // tm,),
        in_specs=[tok(w)] * n + [pl.BlockSpec((PROJ, D_MODEL), _fixed), tok(D_MODEL), tok(D_MODEL), vec] + [HBM] * n_s,
        out_specs=[tok(D_MODEL), vec] + [HBM] * n_s,
        out_shape=[SDS((T, D_MODEL), F32), SDS((1, D_MODEL), F32)] + [SDS(s.shape, s.dtype) for s in to_send],
        scratch_shapes=_scatter_sems(n_s),
        name="in_bwd", compiler_params=_params(("arbitrary",), 40))(*dparts, w_in_t, x, dh1, g1, *to_send)
    return outs[0], outs[1], outs[2:]


def _sum_parts(p_ref):
    g = p_ref[0].astype(F32)
    for s in range(1, N_DEV):
        g = g + p_ref[s].astype(F32)
    return g


def _adamw_math(g, w, m, v):
    nm = ADAM_B1 * m + (1.0 - ADAM_B1) * g
    nv = ADAM_B2 * v + (1.0 - ADAM_B2) * (g * g)
    m_hat = nm / (1.0 - ADAM_B1 ** ADAM_STEP)
    v_hat = nv / (1.0 - ADAM_B2 ** ADAM_STEP)
    return -ADAM_LR * (m_hat / (jnp.sqrt(v_hat) + ADAM_EPS) + ADAM_WD * w), nm, nv


def _row_tile(rows):
    for t in (256, 176, 128, 80):
        if rows % t == 0:
            return t
    raise ValueError(rows)


def _reduce_adamw(parts, w, m, v, name):
    rows, width = w.shape
    tr = _row_tile(rows)

    def body(p_ref, w_ref, m_ref, v_ref, g_ref, d_ref, nm_ref, nv_ref):
        g = _sum_parts(p_ref)
        g_ref[...] = g
        d_ref[...], nm_ref[...], nv_ref[...] = _adamw_math(g, w_ref[...], m_ref[...], v_ref[...])

    blk = pl.BlockSpec((tr, width), _row)
    return pl.pallas_call(
        body, grid=(rows // tr,),
        in_specs=[pl.BlockSpec((N_DEV, tr, width), lambda i: (0, i, 0)), blk, blk, blk],
        out_specs=[blk] * 4, out_shape=[SDS((rows, width), F32)] * 4,
        name="adamw_" + name, compiler_params=_params(("arbitrary",), 32))(parts, w, m, v)


def _reduce(parts, name):
    _, rows, width = parts.shape
    tr = _row_tile(rows)

    def body(p_ref, g_ref):
        g_ref[...] = _sum_parts(p_ref)

    return pl.pallas_call(
        body, grid=(rows // tr,),
        in_specs=[pl.BlockSpec((N_DEV, tr, width), lambda i: (0, i, 0))],
        out_specs=pl.BlockSpec((tr, width), _row), out_shape=SDS((rows, width), F32),
        name="sum_" + name, compiler_params=_params(("arbitrary",), 32))(parts)


def _adamw(g, w, m, v, name):
    rows, width = w.shape
    tr = _row_tile(rows)

    def body(g_ref, w_ref, m_ref, v_ref, d_ref, nm_ref, nv_ref):
        d_ref[...], nm_ref[...], nv_ref[...] = _adamw_math(g_ref[...], w_ref[...], m_ref[...], v_ref[...])

    blk = pl.BlockSpec((tr, width), _row)
    return pl.pallas_call(
        body, grid=(rows // tr,), in_specs=[blk] * 4, out_specs=[blk] * 3, out_shape=[SDS((rows, width), F32)] * 3,
        name="adamw_" + name, compiler_params=_params(("arbitrary",), 32))(g, w, m, v)


SMALL = ("w_spatial", "ln_pre_mix", "ln_post_mix", "ln_pre_ffn", "ln_post_ffn", "b_pe_gate",
         "attn_out_norm", "sgu_out_norm", "b_spatial", "sgu_ln_g", "sgu_ln_b")
ROW_SHARDED = ("w_out", "w_down", "w_pe_gate")
COL_SHARDED = ("w_in", "w_gate_up", "w_pe_proj")
WEIGHTS = ("ln_pre_mix", "w_in", "sgu_ln_g", "sgu_ln_b", "w_spatial", "b_spatial", "attn_out_norm", "sgu_out_norm",
           "w_out", "ln_post_mix", "ln_pre_ffn", "w_gate_up", "w_down", "ln_post_ffn", "w_pe_gate", "b_pe_gate",
           "w_pe_proj")


def _pack_small(t):
    rows = [t["w_spatial"].reshape(-1, D_MODEL)]
    for name in SMALL[1:]:
        flat = t[name].reshape(1, -1)
        rows.append(jnp.pad(flat, ((0, 0), (0, D_MODEL - flat.shape[1]))))
    used = sum(r.shape[0] for r in rows)
    rows.append(jnp.zeros((SMALL_ROWS - used, D_MODEL), F32))
    return jnp.concatenate(rows, axis=0)


def _unpack_small(packed, shapes):
    out = {"w_spatial": packed[:64].reshape(shapes["w_spatial"])}
    for i, name in enumerate(SMALL[1:]):
        size = math.prod(shapes[name])
        out[name] = packed[64 + i, :size].reshape(shapes[name])
    return out


def _slabs(full):
    return full.reshape(N_DEV, full.shape[0] // N_DEV, full.shape[1])


def kernel(x, p, ln_pre_mix, w_in, sgu_ln_g, sgu_ln_b, w_spatial, b_spatial, attn_out_norm, sgu_out_norm, w_out, ln_post_mix, ln_pre_ffn, w_gate_up, w_down, ln_post_ffn, w_pe_gate, b_pe_gate, w_pe_proj, loss_target, m_ln_pre_mix, m_w_in, m_sgu_ln_g, m_sgu_ln_b, m_w_spatial, m_b_spatial, m_attn_out_norm, m_sgu_out_norm, m_w_out, m_ln_post_mix, m_ln_pre_ffn, m_w_gate_up, m_w_down, m_ln_post_ffn, m_w_pe_gate, m_b_pe_gate, m_w_pe_proj, v_ln_pre_mix, v_w_in, v_sgu_ln_g, v_sgu_ln_b, v_w_spatial, v_b_spatial, v_attn_out_norm, v_sgu_out_norm, v_w_out, v_ln_post_mix, v_ln_pre_ffn, v_w_gate_up, v_w_down, v_ln_post_ffn, v_w_pe_gate, v_b_pe_gate, v_w_pe_proj):
    given = dict(locals())
    w = {n: given[n] for n in WEIGHTS}
    m = {n: given["m_" + n] for n in WEIGHTS}
    v = {n: given["v_" + n] for n in WEIGHTS}
    shapes = {n: w[n].shape for n in WEIGHTS}
    xs, ps, target = x[0], p[0, 0], loss_target[0]

    shard = {n: w[n][0].astype(BF) for n in ROW_SHARDED}
    shard.update({n: w[n][0].T.astype(BF) for n in COL_SHARDED})
    sm = {n: w[n][0] for n in SMALL}
    sm = {n: (a.reshape(1, -1) if a.ndim == 1 else a) for n, a in sm.items()}
    slopes = jnp.broadcast_to((2.0 ** -(jnp.arange(8, dtype=F32) + 1.0)).reshape(4, 2, 1), (4, 2, LANES))
    slopes = jnp.concatenate([slopes, jnp.zeros((4, 6, LANES), F32)], axis=1)
    b_st = jnp.pad(sm["b_spatial"].T, ((0, 0), (0, LANES - N_GROUPS)))

    def full(gathered):
        return gathered.reshape(-1, gathered.shape[-1])

    w_in_t = full(_all_gather(shard["w_in"], "gather_w_in"))
    proj, a = _in_proj(xs, sm["ln_pre_mix"], w_in_t)
    later = ("w_out", "w_gate_up", "w_down", "w_pe_gate", "w_pe_proj")
    attn, lse, gathered = _attn_fwd(proj, slopes, [shard[n] for n in later])
    w_out_f, w_gu_t, w_down_f, w_peg_f, w_pep_t = [full(g) for g in gathered]
    sgu = _sgu_fwd(proj, sm["sgu_ln_g"], sm["sgu_ln_b"], sm["w_spatial"], b_st)
    groups, mixed, h1, f = _out_proj(attn, sgu, xs, sm["attn_out_norm"], sm["sgu_out_norm"], w_out_f,
                                     sm["ln_post_mix"], sm["ln_pre_ffn"])
    g, u, act = _gate_up(f, w_gu_t)
    y, h2 = _down_proj(act, w_down_f, h1, sm["ln_post_ffn"])
    dh3, dpp, dpre, h2b, pb, loss_cols, db_peg = _pe_and_loss(h2, ps, target, w_peg_f, sm["b_pe_gate"], w_pep_t)
    loss = lax.psum(0.5 * jnp.sum(loss_cols) * (1.0 / D_MODEL), ("x", "y", "c"))

    arrived = {}
    g_pep_t = _weight_grad(dpp, pb, "w_pe_proj")
    g_peg = _weight_grad(h2b, dpre, "w_pe_gate")
    dh2, dy, d_pff = _pe_bwd(dpre, w_peg_f, dh3, y, sm["ln_post_ffn"])
    g_down = _weight_grad(act, dy, "w_down")
    dg, du, (arrived["w_pe_proj"], arrived["w_pe_gate"]) = _down_bwd(dy, w_down_f, g, u, [_slabs(g_pep_t), _slabs(g_peg)])
    g_gu_t = _weight_grad(dg, f, "w_gate", rows=2 * D_FF)
    g_gu_t = _weight_grad(du, f, "w_up", into=g_gu_t, row_tile=FF_TILES, rows=2 * D_FF)
    dh1, dmx, d_pf, d_pm, (arrived["w_down"],) = _ffn_in_bwd(dg, du, w_gu_t, h1, dh2, mixed, sm["ln_pre_ffn"],
                                                            sm["ln_post_mix"], [_slabs(g_down)])
    g_out = _weight_grad(groups, dmx, "w_out")
    dattn, dsgu, d_ga, d_gs = _out_bwd(dmx, w_out_f, attn, sgu, sm["attn_out_norm"], sm["sgu_out_norm"])
    dus, dzs, d_ws, d_bst, d_lg, d_lb = _sgu_bwd(proj, dsgu, sm["sgu_ln_g"], sm["sgu_ln_b"], sm["w_spatial"], b_st)
    dq, dk, dv, (arrived["w_gate_up"], arrived["w_out"]) = _attn_bwd(proj, dattn, attn, lse, slopes,
                                                                     [_slabs(g_gu_t), _slabs(g_out)])
    dparts = [dq, dk, dv, dus, dzs]
    g_in_t = None
    for i, part in enumerate(dparts):
        g_in_t = _weight_grad(part, a, "w_in_%d" % i, into=g_in_t, row_tile=i, rows=PROJ, out_dtype=BF)
    grad_x, d_g1, (arrived["w_in"],) = _in_bwd(dparts, w_in_t, xs, dh1, sm["ln_pre_mix"], [_slabs(g_in_t)])
    gs = dict(ln_pre_mix=d_g1, sgu_ln_g=d_lg, sgu_ln_b=d_lb, w_spatial=d_ws, b_spatial=d_bst[:, :N_GROUPS].T,
              attn_out_norm=d_ga, sgu_out_norm=d_gs, ln_post_mix=d_pm, ln_pre_ffn=d_pf, ln_post_ffn=d_pff,
              b_pe_gate=db_peg)
    (arrived_small,) = _scatter_call([_pack_small(gs)], [False], "small_grad_exchange")

    res = {}
    for n in ROW_SHARDED:
        res[n] = _reduce_adamw(arrived[n], w[n][0], m[n][0], v[n][0], n)
    for n in ("w_in", "w_gate_up"):
        res[n] = [t.T for t in _reduce_adamw(arrived[n], w[n][0].T, m[n][0].T, v[n][0].T, n)]
    for n in ("w_pe_proj",):
        grad = _reduce(arrived[n], n).T
        res[n] = (grad, *_adamw(grad, w[n][0], m[n][0], v[n][0], n))
    small = _reduce_adamw(arrived_small, _pack_small(w), _pack_small(m), _pack_small(v), "small")
    small = [_unpack_small(t, shapes) for t in small]

    out = []
    for k in range(4):
        out += [res[n][k][None] if n in res else small[k][n] for n in WEIGHTS]
    return (loss, grad_x[None], *out)
```

```python
import math

import jax
import jax.numpy as jnp
from jax import lax
from jax.experimental import pallas as pl
from jax.experimental.pallas import tpu as pltpu

F32 = jnp.float32
BF = jnp.bfloat16


def SDS(shape, dtype):
    return pltpu.HBM(tuple(shape), dtype)

D_MODEL = 1024
ATTN_W = 512
SGU_W = 512
HEAD_DIM = 64
N_GROUPS = 4
GROUP_DIM = 128
CHUNK = 128
D_FF = 2816
PLE_DIM = 256
PROJ = 3 * ATTN_W + 2 * SGU_W
DILATIONS = (1, 4, 16)
QBLK = 128
EPS = 1e-6
NEG = -1e30
N_DEV = 8
LANES = 128

ADAM_LR = 0.001
ADAM_B1 = 0.9
ADAM_B2 = 0.999
ADAM_EPS = 1e-08
ADAM_WD = 0.01
ADAM_STEP = 10

SMALL_ROWS = 80
MIB = 2 ** 20
MESH_ID = pl.DeviceIdType.MESH
HBM = pl.BlockSpec(memory_space=pl.ANY)


def _params(sem, vmem_mib):
    return pltpu.CompilerParams(dimension_semantics=sem, vmem_limit_bytes=vmem_mib * MIB)


def _dot(a, b):
    return jnp.dot(a, b, preferred_element_type=F32)


def _dot_nt(a, b):
    return lax.dot_general(a, b, (((1,), (1,)), ((), ())), preferred_element_type=F32)


def _dot_tn(a, b):
    return lax.dot_general(a, b, (((0,), (0,)), ((), ())), preferred_element_type=F32)


def _rstd(x):
    return lax.rsqrt(jnp.mean(x * x, axis=-1, keepdims=True) + EPS)


def _rms_bwd(dy, x, g):
    r = _rstd(x)
    n = x * r
    dn = dy * g
    dx = r * (dn - n * jnp.mean(dn * n, axis=-1, keepdims=True))
    return dx, dy * n


def _colsum(v):
    return jnp.sum(v, axis=0, keepdims=True)


_G0 = math.sqrt(2.0 / math.pi)
_G1 = 0.044715


def _gelu(x):
    t = jnp.tanh(_G0 * (x + _G1 * x * x * x))
    return 0.5 * x * (1.0 + t), t


def _gelu_grad(x, t):
    return 0.5 * (1.0 + t) + 0.5 * x * (1.0 - t * t) * (_G0 * (1.0 + 3.0 * _G1 * x * x))


def _sigmoid(x):
    return 1.0 / (1.0 + jnp.exp(-x))


def _row(i):
    return (i, 0)


def _fixed(i):
    return (0, 0)


def _acc_init(step, *refs):
    @pl.when(step == 0)
    def _():
        for r in refs:
            r[...] = jnp.zeros_like(r)


FLIPS = [(dx, dy, dc) for dx in (0, 1) for dy in (0, 1) for dc in (0, 1)][1:]
DMA_SEMS = pltpu.SemaphoreType.DMA


def _mesh_pos():
    return lax.axis_index("x"), lax.axis_index("y"), lax.axis_index("c")


def _remote(src, dst, sems, n, to):
    return pltpu.make_async_remote_copy(src_ref=src, dst_ref=dst, send_sem=sems[0].at[n], recv_sem=sems[1].at[n],
                                        device_id=to, device_id_type=MESH_ID)


class _Scatter:
    def __init__(self, items, sems):
        x, y, c = _mesh_pos()
        me = 4 * x + 2 * y + c
        self.local, self.sends, self.arrivals = [], [], []
        for i, (src, dst, slabbed) in enumerate(items):
            self.local.append(pltpu.make_async_copy(src.at[me] if slabbed else src, dst.at[me], sems[2].at[i]))
            for k, (dx, dy, dc) in enumerate(FLIPS):
                to = (1 - x if dx else x, 1 - y if dy else y, 1 - c if dc else c)
                peer = 4 * to[0] + 2 * to[1] + to[2]
                out = src.at[peer] if slabbed else src
                self.sends.append(_remote(out, dst.at[me], sems, 7 * i + k, to))
                self.arrivals.append(_remote(out, dst.at[peer], sems, 7 * i + k, to))

    def start(self):
        for cp in self.local + self.sends:
            cp.start()

    def wait(self):
        for cp in self.arrivals:
            cp.wait_recv()
        for cp in self.sends:
            cp.wait_send()
        for cp in self.local:
            cp.wait()


def _scatter_sems(n):
    return [DMA_SEMS((7 * n,)), DMA_SEMS((7 * n,)), DMA_SEMS((n,))]


class _Gather:
    def __init__(self, items, sems):
        x, y, c = _mesh_pos()
        me, sibling = (x, y, c), (x, y, 1 - c)
        chips = [(1 - x, y), (x, 1 - y), (1 - x, 1 - y)]
        self.first, self.passed, self.from_chips, self.rest, self.local = [], [], [], [], []
        for i, (src, dst) in enumerate(items):
            def slot(p, dst=dst):
                return dst.at[4 * p[0] + 2 * p[1] + p[2]]

            def copy(k, block, to, own=False, i=i, src=src, slot=slot):
                return _remote(src if own else slot(block), slot(block), sems, 7 * i + k, to)

            self.local.append(pltpu.make_async_copy(src, slot(me), sems[2].at[i]))
            self.first.append(copy(0, me, sibling, own=True))
            self.first += [copy(1 + j, me, (*chip, c), own=True) for j, chip in enumerate(chips)]
            self.passed += [copy(4 + j, (*chip, c), sibling) for j, chip in enumerate(chips)]
            self.from_chips += [copy(1 + j, (*chip, c), me) for j, chip in enumerate(chips)]
            self.rest.append(copy(0, sibling, me))
            self.rest += [copy(4 + j, (*chip, 1 - c), me) for j, chip in enumerate(chips)]

    def start(self):
        for cp in self.local + self.first:
            cp.start()

    def forward(self):
        for arrived, onward in zip(self.from_chips, self.passed):
            arrived.wait_recv()
            onward.start()

    def finish(self):
        for cp in self.rest:
            cp.wait_recv()
        for cp in self.first + self.passed:
            cp.wait_send()
        for cp in self.local:
            cp.wait()


def _all_gather(shard, name):
    def body(x_ref, out_ref, *sems):
        g = _Gather([(x_ref, out_ref)], sems)
        g.start()
        g.forward()
        g.finish()

    return pl.pallas_call(
        body, out_shape=SDS((N_DEV,) + shard.shape, shard.dtype), in_specs=[HBM], out_specs=HBM,
        scratch_shapes=_scatter_sems(1), name=name)(shard)


def _scatter_call(srcs, slabbed, name):
    n = len(srcs)

    def body(*refs):
        sc = _Scatter(list(zip(refs[:n], refs[n:2 * n], slabbed)), refs[2 * n:])
        sc.start()
        sc.wait()

    shapes = [SDS(s.shape if sl else (N_DEV,) + s.shape, s.dtype) for s, sl in zip(srcs, slabbed)]
    return pl.pallas_call(body, out_shape=shapes, in_specs=[HBM] * n, out_specs=[HBM] * n,
                          scratch_shapes=_scatter_sems(n), name=name)(*srcs)


def _in_proj(x, g1, w_in_t, tm=512):
    T = x.shape[0]

    def body(x_ref, g_ref, w_ref, proj_ref, a_ref):
        xv = x_ref[...]
        a = (xv * _rstd(xv) * g_ref[...]).astype(BF)
        a_ref[...] = a
        proj_ref[...] = _dot_nt(a, w_ref[...])

    return pl.pallas_call(
        body, grid=(T // tm,),
        in_specs=[pl.BlockSpec((tm, D_MODEL), _row), pl.BlockSpec((1, D_MODEL), _fixed),
                  pl.BlockSpec((PROJ, D_MODEL), _fixed)],
        out_specs=[pl.BlockSpec((tm, PROJ), _row), pl.BlockSpec((tm, D_MODEL), _row)],
        out_shape=[SDS((T, PROJ), F32), SDS((T, D_MODEL), BF)],
        name="in_proj", compiler_params=_params(("arbitrary",), 48))(x, g1, w_in_t)


ATTN_GROUP = 16


def _attn_bias(sl_ref, bias_ref):
    qi = lax.broadcasted_iota(jnp.int32, (QBLK, QBLK), 0)
    kj = lax.broadcasted_iota(jnp.int32, (QBLK, QBLK), 1)
    step = qi - kj
    for di, d in enumerate(DILATIONS):
        for j in range(2):
            sl = sl_ref[0, j:j + 1, :]
            cur = jnp.where(step >= 0, -sl * (step * d).astype(F32), NEG)
            prev = jnp.where(step <= 0, -sl * ((step + QBLK) * d).astype(F32), NEG)
            rows = slice(j * QBLK, (j + 1) * QBLK)
            bias_ref[di * 2, rows, :QBLK] = prev
            bias_ref[di * 2, rows, QBLK:] = cur
            bias_ref[di * 2 + 1, rows, :QBLK] = jnp.full((QBLK, QBLK), NEG, F32)
            bias_ref[di * 2 + 1, rows, QBLK:] = cur


def _stack_heads(x, lo):
    return jnp.concatenate([jnp.where(lo, x, 0.0), jnp.where(lo, 0.0, x)], axis=0).astype(BF)


def _unstack_heads(x, lo):
    return jnp.where(lo, x[:QBLK], x[QBLK:])


def _attn_rows(start, d, blocks=1):
    if d == 1:
        return pl.ds(pl.multiple_of(start, QBLK), blocks * QBLK)
    return pl.ds(start, blocks * QBLK, stride=d)


def _attn_group_index(i, nblk, d, group):
    per = nblk // d // group
    r = i // per
    n0 = (i % per) * group
    start = r + (d * QBLK) * n0
    pstart = jnp.maximum(start - d * QBLK, r)
    return start, pstart, n0 == 0


def _attn_plan(nblk, d):
    group = min(ATTN_GROUP, nblk // d)
    return group, max(1, min(ATTN_GROUP // group, d))


def _attn_fwd(proj, slopes, to_gather):
    T = proj.shape[0]
    nblk = T // QBLK
    big = dict(pipeline_mode=pl.Buffered(1))
    n_g = len(to_gather)

    def body(q_ref, k_ref, v_ref, sl_ref, *rest):
        srcs, (o_ref, m_ref), dsts = rest[:n_g], rest[n_g:n_g + 2], rest[n_g + 2:2 * n_g + 2]
        sems, (l_ref, bias_ref) = rest[2 * n_g + 2:2 * n_g + 5], rest[2 * n_g + 5:]
        h = pl.program_id(0)

        @pl.when(h == 0)
        def _():
            _Gather(list(zip(srcs, dsts)), sems).start()

        @pl.when(h == 2)
        def _():
            _Gather(list(zip(srcs, dsts)), sems).forward()

        _attn_bias(sl_ref, bias_ref)
        lo = lax.broadcasted_iota(jnp.int32, (1, LANES), 1) < HEAD_DIM

        for di, d in enumerate(DILATIONS):
            group, segs = _attn_plan(nblk, d)

            def step(i, carry, segs=segs, **kw):
                for s in range(segs):
                    segment(i * segs + s, **kw)
                return carry

            def segment(i, d=d, di=di, group=group):
                start, pstart, first = _attn_group_index(i, nblk, d, group)
                rows, prows = _attn_rows(start, d, group), _attn_rows(pstart, d)
                q = q_ref[rows, :] * (HEAD_DIM ** -0.5)
                k = jnp.concatenate([k_ref[prows, :], k_ref[rows, :]], axis=0).astype(BF)
                v = jnp.concatenate([v_ref[prows, :], v_ref[rows, :]], axis=0).astype(BF)
                for b in range(group):
                    qb = slice(b * QBLK, (b + 1) * QBLK)
                    kb = slice(b * QBLK, (b + 2) * QBLK)
                    bias = bias_ref[di * 2 + first.astype(jnp.int32)] if b == 0 else bias_ref[di * 2]
                    s = _dot_nt(_stack_heads(q[qb], lo), k[kb]) + bias
                    m = jnp.max(s, axis=-1, keepdims=True)
                    pr = jnp.exp(s - m)
                    m_b = _unstack_heads(m, lo)
                    l_b = _unstack_heads(jnp.sum(pr, axis=-1, keepdims=True), lo)
                    o_b = _unstack_heads(_dot(pr.astype(BF), v[kb]), lo)
                    out = _attn_rows(start + b * (d * QBLK), d)
                    if di == 0:
                        m_ref[out, :] = m_b
                        l_ref[out, :] = l_b
                        o_ref[out, :] = o_b
                    else:
                        m_o = m_ref[out, :]
                        m_n = jnp.maximum(m_o, m_b)
                        wa, wb = jnp.exp(m_o - m_n), jnp.exp(m_b - m_n)
                        m_ref[out, :] = m_n
                        l_ref[out, :] = wa * l_ref[out, :] + wb * l_b
                        o_ref[out, :] = wa * o_ref[out, :] + wb * o_b

            lax.fori_loop(0, nblk // (group * segs), step, 0)

        def finish(i, carry):
            rows = pl.ds(pl.multiple_of(i * QBLK, QBLK), QBLK)
            l = l_ref[rows, :]
            o_ref[rows, :] = o_ref[rows, :] / l
            m_ref[rows, :] = m_ref[rows, :] + jnp.log(l)
            return carry

        lax.fori_loop(0, nblk, finish, 0)

        @pl.when(h == pl.num_programs(0) - 1)
        def _():
            _Gather(list(zip(srcs, dsts)), sems).finish()

    col = lambda base: pl.BlockSpec((T, LANES), lambda h: (0, base + h), **big)
    tok = pl.BlockSpec((T, LANES), lambda h: (0, h))
    outs = pl.pallas_call(
        body, grid=(4,),
        in_specs=[col(0), col(4), col(8), pl.BlockSpec((1, 8, LANES), lambda h: (h, 0, 0))] + [HBM] * n_g,
        out_specs=[tok, tok] + [HBM] * n_g,
        out_shape=[SDS((T, ATTN_W), F32), SDS((T, ATTN_W), F32)]
        + [SDS((N_DEV,) + g.shape, g.dtype) for g in to_gather],
        scratch_shapes=_scatter_sems(n_g) + [pltpu.VMEM((T, LANES), F32), pltpu.VMEM((6, 2 * QBLK, 2 * QBLK), F32)],
        name="attn_fwd", compiler_params=_params(("arbitrary",), 48))(proj, proj, proj, slopes, *to_gather)
    return outs[0], outs[1], outs[2:]


def _sgu_norm(zv, ln_g, ln_b):
    gz, tz = _gelu(zv)
    mu = jnp.mean(gz, axis=-1, keepdims=True)
    xc = gz - mu
    rs = lax.rsqrt(jnp.mean(xc * xc, axis=-1, keepdims=True) + EPS)
    xhat = xc * rs
    return xhat * ln_g + ln_b, xhat, rs, tz


def _causal(w):
    i = lax.broadcasted_iota(jnp.int32, (CHUNK, CHUNK), 0)
    j = lax.broadcasted_iota(jnp.int32, (CHUNK, CHUNK), 1)
    return jnp.where(i >= j, w, 0.0)


def _sgu_fwd(proj, ln_g, ln_b, w_s, b_st, tm=512):
    T = proj.shape[0]

    def body(u_ref, z_ref, g_ref, b_ref, w_ref, bs_ref, out_ref):
        for g in range(N_GROUPS):
            wm = _causal(w_ref[g]).astype(BF)
            cols = slice(g * GROUP_DIM, (g + 1) * GROUP_DIM)
            for c in range(tm // CHUNK):
                rows = slice(c * CHUNK, (c + 1) * CHUNK)
                zn, _, _, _ = _sgu_norm(z_ref[rows, cols], g_ref[...], b_ref[...])
                mixed = _dot(wm, zn.astype(BF)) + bs_ref[:, g:g + 1]
                gu, _ = _gelu(u_ref[rows, cols])
                out_ref[rows, cols] = gu * mixed

    return pl.pallas_call(
        body, grid=(T // tm,),
        in_specs=[pl.BlockSpec((tm, SGU_W), lambda i: (i, 3)), pl.BlockSpec((tm, SGU_W), lambda i: (i, 4)),
                  pl.BlockSpec((1, GROUP_DIM), _fixed), pl.BlockSpec((1, GROUP_DIM), _fixed),
                  pl.BlockSpec((N_GROUPS, CHUNK, CHUNK), lambda i: (0, 0, 0)), pl.BlockSpec((CHUNK, LANES), _fixed)],
        out_specs=pl.BlockSpec((tm, SGU_W), _row),
        out_shape=SDS((T, SGU_W), F32),
        name="sgu_fwd", compiler_params=_params(("arbitrary",), 32))(proj, proj, ln_g, ln_b, w_s, b_st)


def _out_proj(attn, sgu, x, g_a, g_s, w_out, g_pm, g_pf, tm=512):
    T = x.shape[0]

    def body(a_ref, s_ref, x_ref, ga_ref, gs_ref, w_ref, gpm_ref, gpf_ref, grp_ref, mixed_ref, h1_ref, f_ref):
        av, sv = a_ref[...], s_ref[...]
        an = (av * _rstd(av) * ga_ref[...]).astype(BF)
        sn = (sv * _rstd(sv) * gs_ref[...]).astype(BF)
        grp_ref[:, :ATTN_W] = an
        grp_ref[:, ATTN_W:] = sn
        mixed = _dot(an, w_ref[:ATTN_W, :]) + _dot(sn, w_ref[ATTN_W:, :])
        mixed_ref[...] = mixed
        h1 = x_ref[...] + mixed * _rstd(mixed) * gpm_ref[...]
        h1_ref[...] = h1
        f_ref[...] = (h1 * _rstd(h1) * gpf_ref[...]).astype(BF)

    tok = lambda w: pl.BlockSpec((tm, w), _row)
    vec = lambda w: pl.BlockSpec((1, w), _fixed)
    return pl.pallas_call(
        body, grid=(T // tm,),
        in_specs=[tok(ATTN_W), tok(SGU_W), tok(D_MODEL), vec(ATTN_W), vec(SGU_W),
                  pl.BlockSpec((D_MODEL, D_MODEL), _fixed), vec(D_MODEL), vec(D_MODEL)],
        out_specs=[tok(D_MODEL)] * 4,
        out_shape=[SDS((T, D_MODEL), BF), SDS((T, D_MODEL), F32), SDS((T, D_MODEL), F32), SDS((T, D_MODEL), BF)],
        name="out_proj", compiler_params=_params(("arbitrary",), 48))(attn, sgu, x, g_a, g_s, w_out, g_pm, g_pf)


FF_TILE = 1408
FF_TILES = D_FF // FF_TILE


def _gate_up(f, w_gu_t, tm=512):
    T = f.shape[0]
    tn = FF_TILE

    def body(f_ref, wg_ref, wu_ref, g_ref, u_ref, act_ref):
        fv = f_ref[...]
        g = _dot_nt(fv, wg_ref[...])
        u = _dot_nt(fv, wu_ref[...])
        g_ref[...] = g.astype(BF)
        u_ref[...] = u.astype(BF)
        act_ref[...] = (g * _sigmoid(g) * u).astype(BF)

    ospec = pl.BlockSpec((tm, tn), lambda j, i: (i, j))
    return pl.pallas_call(
        body, grid=(FF_TILES, T // tm),
        in_specs=[pl.BlockSpec((tm, D_MODEL), lambda j, i: (i, 0)), pl.BlockSpec((tn, D_MODEL), lambda j, i: (j, 0)),
                  pl.BlockSpec((tn, D_MODEL), lambda j, i: (j + FF_TILES, 0))],
        out_specs=[ospec] * 3, out_shape=[SDS((T, D_FF), BF)] * 3,
        name="gate_up", compiler_params=_params(("arbitrary", "arbitrary"), 40))(f, w_gu_t, w_gu_t)


def _down_proj(act, w_down, h1, g_pff, tm=512):
    T = act.shape[0]

    def body(a_ref, w_ref, h1_ref, g_ref, y_ref, h2_ref):
        y = _dot(a_ref[...], w_ref[...])
        y_ref[...] = y
        h2_ref[...] = h1_ref[...] + y * _rstd(y) * g_ref[...]

    return pl.pallas_call(
        body, grid=(T // tm,),
        in_specs=[pl.BlockSpec((tm, D_FF), _row), pl.BlockSpec((D_FF, D_MODEL), _fixed),
                  pl.BlockSpec((tm, D_MODEL), _row), pl.BlockSpec((1, D_MODEL), _fixed)],
        out_specs=[pl.BlockSpec((tm, D_MODEL), _row)] * 2,
        out_shape=[SDS((T, D_MODEL), F32)] * 2,
        name="down_proj", compiler_params=_params(("arbitrary",), 48))(act, w_down, h1, g_pff)


def _pe_and_loss(h2, p, target, w_peg, b_peg, w_pep_t, tm=512):
    T = h2.shape[0]

    def body(h2_ref, p_ref, t_ref, wg_ref, b_ref, wp_ref,
             dh3_ref, dpp_ref, dpre_ref, h2b_ref, pb_ref, loss_ref, db_ref):
        _acc_init(pl.program_id(0), loss_ref, db_ref)
        h2v = h2_ref[...]
        h2b = h2v.astype(BF)
        pb = p_ref[...].astype(BF)
        h2b_ref[...] = h2b
        pb_ref[...] = pb
        gate = _sigmoid(_dot(h2b, wg_ref[...]) + b_ref[...])
        pp = _dot_nt(pb, wp_ref[...])
        diff = h2v + gate * pp - t_ref[...]
        loss_ref[...] += _colsum(diff * diff)
        dh3 = diff * (1.0 / D_MODEL)
        dh3_ref[...] = dh3
        dpp_ref[...] = (dh3 * gate).astype(BF)
        dpre = dh3 * pp * (gate * (1.0 - gate))
        dpre_ref[...] = dpre.astype(BF)
        db_ref[...] += _colsum(dpre)

    tok = lambda w: pl.BlockSpec((tm, w), _row)
    vec = pl.BlockSpec((1, D_MODEL), _fixed)
    return pl.pallas_call(
        body, grid=(T // tm,),
        in_specs=[tok(D_MODEL), tok(PLE_DIM), tok(D_MODEL), pl.BlockSpec((D_MODEL, D_MODEL), _fixed), vec,
                  pl.BlockSpec((D_MODEL, PLE_DIM), _fixed)],
        out_specs=[tok(D_MODEL), tok(D_MODEL), tok(D_MODEL), tok(D_MODEL), tok(PLE_DIM), vec, vec],
        out_shape=[SDS((T, D_MODEL), F32), SDS((T, D_MODEL), BF), SDS((T, D_MODEL), BF), SDS((T, D_MODEL), BF),
                   SDS((T, PLE_DIM), BF), SDS((1, D_MODEL), F32), SDS((1, D_MODEL), F32)],
        name="pe_and_loss", compiler_params=_params(("arbitrary",), 48))(h2, p, target, w_peg, b_peg, w_pep_t)


def _weight_grad(a, dy, name, into=None, row_tile=0, rows=None, out_dtype=F32, tk=512):
    n = dy.shape[1]
    tn = min(n, 1024)
    if a.ndim == 3:
        parts, T, tka = a.shape
        ka = parts * tka
        a_spec = pl.BlockSpec((None, tk, tka), lambda i, j, k: (i, k, 0))
    else:
        T, ka = a.shape
        tka = FF_TILE if ka == D_FF else min(ka, 1024)
        a_spec = pl.BlockSpec((tk, tka), lambda i, j, k: (k, i))
    rows = ka if rows is None else rows
    direct = out_dtype == F32

    def body(a_ref, dy_ref, *rest):
        out_ref, acc_ref = (rest[-1], rest[-1]) if direct else rest[-2:]
        _acc_init(pl.program_id(2), acc_ref)
        acc_ref[...] += _dot_tn(a_ref[...].astype(BF), dy_ref[...].astype(BF))
        if not direct:
            @pl.when(pl.program_id(2) == pl.num_programs(2) - 1)
            def _():
                out_ref[...] = acc_ref[...].astype(out_dtype)

    carried = [] if into is None else [into]
    return pl.pallas_call(
        body, grid=(ka // tka, n // tn, T // tk),
        in_specs=[a_spec, pl.BlockSpec((tk, tn), lambda i, j, k: (k, j))] + [HBM] * len(carried),
        out_specs=pl.BlockSpec((tka, tn), lambda i, j, k: (i + row_tile, j)),
        out_shape=SDS((rows, n), out_dtype), input_output_aliases={2: 0} if carried else {},
        scratch_shapes=[] if direct else [pltpu.VMEM((tka, tn), F32)],
        name="grad_" + name, compiler_params=_params(("arbitrary",) * 3, 40))(a, dy, *carried)


def _pe_bwd(dpre, w_peg, dh3, y, g_pff, tm=512):
    T = y.shape[0]

    def body(dp_ref, w_ref, dh3_ref, y_ref, g_ref, dh2_ref, dy_ref, dg_ref):
        _acc_init(pl.program_id(0), dg_ref)
        dh2 = dh3_ref[...] + _dot_nt(dp_ref[...], w_ref[...])
        dh2_ref[...] = dh2
        dy, dg = _rms_bwd(dh2, y_ref[...], g_ref[...])
        dy_ref[...] = dy.astype(BF)
        dg_ref[...] += _colsum(dg)

    tok = pl.BlockSpec((tm, D_MODEL), _row)
    vec = pl.BlockSpec((1, D_MODEL), _fixed)
    return pl.pallas_call(
        body, grid=(T // tm,),
        in_specs=[tok, pl.BlockSpec((D_MODEL, D_MODEL), _fixed), tok, tok, vec],
        out_specs=[tok, tok, vec],
        out_shape=[SDS((T, D_MODEL), F32), SDS((T, D_MODEL), BF), SDS((1, D_MODEL), F32)],
        name="pe_bwd", compiler_params=_params(("arbitrary",), 40))(dpre, w_peg, dh3, y, g_pff)


def _down_bwd(dy, w_down, g, u, to_send, tm=512):
    T = dy.shape[0]
    tn = FF_TILE
    n_s = len(to_send)

    def body(dy_ref, w_ref, g_ref, u_ref, *rest):
        srcs, (dg_ref, du_ref), dsts, sems = rest[:n_s], rest[n_s:n_s + 2], rest[n_s + 2:2 * n_s + 2], rest[2 * n_s + 2:]
        j, i = pl.program_id(0), pl.program_id(1)
        items = list(zip(srcs, dsts, [True] * n_s))

        @pl.when((j == 0) & (i == 0))
        def _():
            _Scatter(items, sems).start()

        dact = _dot_nt(dy_ref[...], w_ref[...])
        gv = g_ref[...].astype(F32)
        uv = u_ref[...].astype(F32)
        s = _sigmoid(gv)
        dg_ref[...] = (dact * uv * (s * (1.0 + gv * (1.0 - s)))).astype(BF)
        du_ref[...] = (dact * (gv * s)).astype(BF)

        @pl.when((j == pl.num_programs(0) - 1) & (i == pl.num_programs(1) - 1))
        def _():
            _Scatter(items, sems).wait()

    tile = pl.BlockSpec((tm, tn), lambda j, i: (i, j))
    outs = pl.pallas_call(
        body, grid=(FF_TILES, T // tm),
        in_specs=[pl.BlockSpec((tm, D_MODEL), lambda j, i: (i, 0)), pl.BlockSpec((tn, D_MODEL), lambda j, i: (j, 0)),
                  tile, tile] + [HBM] * n_s,
        out_specs=[tile, tile] + [HBM] * n_s,
        out_shape=[SDS((T, D_FF), BF)] * 2 + [SDS(s.shape, s.dtype) for s in to_send],
        scratch_shapes=_scatter_sems(n_s),
        name="down_bwd", compiler_params=_params(("arbitrary", "arbitrary"), 40))(dy, w_down, g, u, *to_send)
    return outs[0], outs[1], outs[2:]


def _ffn_in_bwd(dg, du, w_gu_t, h1, dh2, mixed, g_pf, g_pm, to_send, tm=512):
    T = h1.shape[0]
    n_s = len(to_send)

    def body(dg_ref, du_ref, wg_ref, wu_ref, h1_ref, dh2_ref, mx_ref, gpf_ref, gpm_ref, *rest):
        srcs, outs, dsts, sems = rest[:n_s], rest[n_s:n_s + 4], rest[n_s + 4:2 * n_s + 4], rest[2 * n_s + 4:]
        dh1_ref, dmx_ref, dgpf_ref, dgpm_ref = outs
        i = pl.program_id(0)
        items = list(zip(srcs, dsts, [True] * n_s))
        _acc_init(i, dgpf_ref, dgpm_ref)

        @pl.when(i == 0)
        def _():
            _Scatter(items, sems).start()

        df = _dot(dg_ref[...], wg_ref[...]) + _dot(du_ref[...], wu_ref[...])
        dx, dgf = _rms_bwd(df, h1_ref[...], gpf_ref[...])
        dh1 = dh2_ref[...] + dx
        dh1_ref[...] = dh1
        dmx, dgm = _rms_bwd(dh1, mx_ref[...], gpm_ref[...])
        dmx_ref[...] = dmx.astype(BF)
        dgpf_ref[...] += _colsum(dgf)
        dgpm_ref[...] += _colsum(dgm)

        @pl.when(i == pl.num_programs(0) - 1)
        def _():
            _Scatter(items, sems).wait()

    tok = lambda w: pl.BlockSpec((tm, w), _row)
    vec = pl.BlockSpec((1, D_MODEL), _fixed)
    outs = pl.pallas_call(
        body, grid=(T // tm,),
        in_specs=[tok(D_FF), tok(D_FF), pl.BlockSpec((D_FF, D_MODEL), lambda i: (0, 0), pipeline_mode=pl.Buffered(1)),
                  pl.BlockSpec((D_FF, D_MODEL), lambda i: (1, 0), pipeline_mode=pl.Buffered(1)),
                  tok(D_MODEL), tok(D_MODEL), tok(D_MODEL), vec, vec]
        + [HBM] * n_s,
        out_specs=[tok(D_MODEL), tok(D_MODEL), vec, vec] + [HBM] * n_s,
        out_shape=[SDS((T, D_MODEL), F32), SDS((T, D_MODEL), BF), SDS((1, D_MODEL), F32), SDS((1, D_MODEL), F32)]
        + [SDS(s.shape, s.dtype) for s in to_send],
        scratch_shapes=_scatter_sems(n_s),
        name="ffn_in_bwd", compiler_params=_params(("arbitrary",), 56))(
            dg, du, w_gu_t, w_gu_t, h1, dh2, mixed, g_pf, g_pm, *to_send)
    return outs[0], outs[1], outs[2], outs[3], outs[4:]


def _out_bwd(dmx, w_out, attn, sgu, g_a, g_s, tm=512):
    T = attn.shape[0]

    def body(dm_ref, w_ref, a_ref, s_ref, ga_ref, gs_ref, da_ref, ds_ref, dga_ref, dgs_ref):
        _acc_init(pl.program_id(0), dga_ref, dgs_ref)
        dgr = _dot_nt(dm_ref[...], w_ref[...])
        da, dga = _rms_bwd(dgr[:, :ATTN_W], a_ref[...], ga_ref[...])
        ds, dgs = _rms_bwd(dgr[:, ATTN_W:], s_ref[...], gs_ref[...])
        da_ref[...] = da
        ds_ref[...] = ds
        dga_ref[...] += _colsum(dga)
        dgs_ref[...] += _colsum(dgs)

    tok = lambda w: pl.BlockSpec((tm, w), _row)
    vec = lambda w: pl.BlockSpec((1, w), _fixed)
    return pl.pallas_call(
        body, grid=(T // tm,),
        in_specs=[tok(D_MODEL), pl.BlockSpec((D_MODEL, D_MODEL), _fixed), tok(ATTN_W), tok(SGU_W), vec(ATTN_W), vec(SGU_W)],
        out_specs=[tok(ATTN_W), tok(SGU_W), vec(ATTN_W), vec(SGU_W)],
        out_shape=[SDS((T, ATTN_W), F32), SDS((T, SGU_W), F32), SDS((1, ATTN_W), F32), SDS((1, SGU_W), F32)],
        name="out_bwd", compiler_params=_params(("arbitrary",), 40))(dmx, w_out, attn, sgu, g_a, g_s)


def _sgu_bwd(proj, dsgu, ln_g, ln_b, w_s, b_st, tm=512):
    T = proj.shape[0]

    def body(u_ref, z_ref, ds_ref, g_ref, b_ref, w_ref, bs_ref,
             duz_ref, dw_ref, dbs_ref, dlg_ref, dlb_ref, dbacc_ref):
        du_ref, dz_ref = duz_ref.at[0], duz_ref.at[1]
        step = pl.program_id(0)
        _acc_init(step, dw_ref, dbs_ref, dlg_ref, dlb_ref, dbacc_ref)
        lng, lnb = g_ref[...], b_ref[...]
        for g in range(N_GROUPS):
            wm = _causal(w_ref[g]).astype(BF)
            cols = slice(g * GROUP_DIM, (g + 1) * GROUP_DIM)
            for c in range(tm // CHUNK):
                rows = slice(c * CHUNK, (c + 1) * CHUNK)
                zv, uv, dout = z_ref[rows, cols], u_ref[rows, cols], ds_ref[rows, cols]
                zn, xhat, rs, tz = _sgu_norm(zv, lng, lnb)
                znb = zn.astype(BF)
                mixed = _dot(wm, znb) + bs_ref[:, g:g + 1]
                gu, tu = _gelu(uv)
                du_ref[rows, cols] = (dout * mixed * _gelu_grad(uv, tu)).astype(BF)
                dmix = dout * gu
                dmb = dmix.astype(BF)
                dw_ref[g] += _causal(_dot_nt(dmb, znb))
                dbacc_ref[g] += dmix
                dzn = _dot_tn(wm, dmb)
                dlg_ref[...] += _colsum(dzn * xhat)
                dlb_ref[...] += _colsum(dzn)
                dxh = dzn * lng
                dgz = rs * (dxh - jnp.mean(dxh, axis=-1, keepdims=True)
                            - xhat * jnp.mean(dxh * xhat, axis=-1, keepdims=True))
                dz_ref[rows, cols] = (dgz * _gelu_grad(zv, tz)).astype(BF)

        @pl.when(step == pl.num_programs(0) - 1)
        def _():
            lane = lax.broadcasted_iota(jnp.int32, (CHUNK, LANES), 1)
            acc = jnp.zeros((CHUNK, LANES), F32)
            for g in range(N_GROUPS):
                acc = jnp.where(lane == g, jnp.sum(dbacc_ref[g], axis=-1, keepdims=True), acc)
            dbs_ref[...] = acc

    tok = pl.BlockSpec((tm, SGU_W), _row)
    vec = pl.BlockSpec((1, GROUP_DIM), _fixed)
    wsp = pl.BlockSpec((N_GROUPS, CHUNK, CHUNK), lambda i: (0, 0, 0))
    sq = pl.BlockSpec((CHUNK, LANES), _fixed)
    return pl.pallas_call(
        body, grid=(T // tm,),
        in_specs=[pl.BlockSpec((tm, SGU_W), lambda i: (i, 3)), pl.BlockSpec((tm, SGU_W), lambda i: (i, 4)), tok,
                  vec, vec, wsp, sq],
        out_specs=[pl.BlockSpec((2, tm, SGU_W), lambda i: (0, i, 0)), wsp, sq, vec, vec],
        out_shape=[SDS((2, T, SGU_W), BF), SDS((N_GROUPS, CHUNK, CHUNK), F32),
                   SDS((CHUNK, LANES), F32), SDS((1, GROUP_DIM), F32), SDS((1, GROUP_DIM), F32)],
        scratch_shapes=[pltpu.VMEM((N_GROUPS, CHUNK, LANES), F32)],
        name="sgu_bwd", compiler_params=_params(("arbitrary",), 32))(proj, proj, dsgu, ln_g, ln_b, w_s, b_st)


def _attn_bwd(proj, do, o, lse, slopes, to_send):
    T = proj.shape[0]
    nblk = T // QBLK
    big = dict(pipeline_mode=pl.Buffered(1))
    n_s = len(to_send)

    def body(q_ref, k_ref, v_ref, do_ref, o_ref, l_ref, sl_ref, *rest):
        srcs, d_ref, dsts = rest[:n_s], rest[n_s], rest[n_s + 1:2 * n_s + 1]
        sems, bias_ref = rest[2 * n_s + 1:2 * n_s + 4], rest[2 * n_s + 4]
        dq_ref, dk_ref, dv_ref = d_ref.at[0], d_ref.at[1], d_ref.at[2]
        h = pl.program_id(0)
        items = list(zip(srcs, dsts, [True] * n_s))

        @pl.when(h == 0)
        def _():
            _Scatter(items, sems).start()

        _attn_bias(sl_ref, bias_ref)
        lo = lax.broadcasted_iota(jnp.int32, (1, LANES), 1) < HEAD_DIM
        scale = HEAD_DIM ** -0.5
        d_ref[...] = jnp.zeros_like(d_ref)

        for di, d in enumerate(DILATIONS):
            group, segs = _attn_plan(nblk, d)

            def step(i, carry, segs=segs, **kw):
                for s in range(segs):
                    segment(i * segs + s, **kw)
                return carry

            def segment(i, d=d, di=di, group=group):
                start, pstart, first = _attn_group_index(i, nblk, d, group)
                rows, prows = _attn_rows(start, d, group), _attn_rows(pstart, d)
                q = q_ref[rows, :] * scale
                k = jnp.concatenate([k_ref[prows, :], k_ref[rows, :]], axis=0).astype(BF)
                v = jnp.concatenate([v_ref[prows, :], v_ref[rows, :]], axis=0).astype(BF)
                dov = do_ref[rows, :]
                prod = dov * o_ref[rows, :]
                lse_g = l_ref[rows, :]
                masks = [lo, ~lo]
                qm = [jnp.where(masks[j], q, 0.0).astype(BF) for j in range(2)]
                dom = [jnp.where(masks[j], dov, 0.0).astype(BF) for j in range(2)]
                for b in range(group):
                    qb = slice(b * QBLK, (b + 1) * QBLK)
                    kb = slice(b * QBLK, (b + 2) * QBLK)
                    which = di * 2 + first.astype(jnp.int32) if b == 0 else di * 2
                    dq_parts, prs, dss = [], [], []
                    for j in range(2):
                        bias = bias_ref[which, j * QBLK:(j + 1) * QBLK, :]
                        delta = jnp.sum(jnp.where(masks[j], prod[qb], 0.0), axis=-1, keepdims=True)
                        lj = lse_g[qb, j * HEAD_DIM:j * HEAD_DIM + 1]
                        pr = jnp.exp(_dot_nt(qm[j][qb], k[kb]) + bias - lj)
                        ds = (pr * (_dot_nt(dom[j][qb], v[kb]) - delta)).astype(BF)
                        dq_parts.append(_dot(ds, k[kb]))
                        prs.append(pr.astype(BF))
                        dss.append(ds)
                    dk_b = _dot_tn(jnp.concatenate(dss, axis=0), jnp.concatenate([qm[0][qb], qm[1][qb]], axis=0))
                    dv_b = _dot_tn(jnp.concatenate(prs, axis=0), jnp.concatenate([dom[0][qb], dom[1][qb]], axis=0))
                    own = _attn_rows(start + b * (d * QBLK), d)
                    dq_ref[own, :] += jnp.where(lo, dq_parts[0], dq_parts[1]) * scale
                    if b == 0:
                        dk_ref[prows, :] += dk_b[:QBLK]
                        dv_ref[prows, :] += dv_b[:QBLK]
                        dk_ref[own, :] += dk_b[QBLK:]
                        dv_ref[own, :] += dv_b[QBLK:]
                    else:
                        two = _attn_rows(start + (b - 1) * (d * QBLK), d, 2)
                        dk_ref[two, :] += dk_b
                        dv_ref[two, :] += dv_b

            lax.fori_loop(0, nblk // (group * segs), step, 0)

        @pl.when(h == pl.num_programs(0) - 1)
        def _():
            _Scatter(items, sems).wait()

    col = lambda base: pl.BlockSpec((T, LANES), lambda h: (0, base + h), **big)
    outs = pl.pallas_call(
        body, grid=(4,),
        in_specs=[col(0), col(4), col(8), col(0), col(0), col(0), pl.BlockSpec((1, 8, LANES), lambda h: (h, 0, 0))]
        + [HBM] * n_s,
        out_specs=[pl.BlockSpec((3, T, LANES), lambda h: (0, 0, h))] + [HBM] * n_s,
        out_shape=[SDS((3, T, ATTN_W), F32)] + [SDS(s.shape, s.dtype) for s in to_send],
        scratch_shapes=_scatter_sems(n_s) + [pltpu.VMEM((6, 2 * QBLK, 2 * QBLK), F32)],
        name="attn_bwd", compiler_params=_params(("arbitrary",), 60))(proj, proj, proj, do, o, lse, slopes, *to_send)
    return outs[0], outs[1:]


def _in_bwd(dparts, w_in_t, x, dh1, g1, to_send, tm=512):
    T = x.shape[0]
    n = len(dparts)
    n_s = len(to_send)
    w = ATTN_W

    def body(*refs):
        d_refs, (w_ref, x_ref, dh1_ref, g_ref), rest = refs[:n], refs[n:n + 4], refs[n + 4:]
        srcs, (dx_ref, dg_ref), dsts, sems = rest[:n_s], rest[n_s:n_s + 2], rest[n_s + 2:2 * n_s + 2], rest[2 * n_s + 2:]
        step = pl.program_id(0)
        items = list(zip(srcs, dsts, [True] * n_s))
        _acc_init(step, dg_ref)

        @pl.when(step == 0)
        def _():
            _Scatter(items, sems).start()

        da = None
        col = 0
        for r in d_refs:
            for part in range(r.shape[0]):
                t = _dot(r[part].astype(BF), w_ref[col * w:(col + 1) * w, :])
                da = t if da is None else da + t
                col += 1
        dx, dg = _rms_bwd(da, x_ref[...], g_ref[...])
        dx_ref[...] = dh1_ref[...] + dx
        dg_ref[...] += _colsum(dg)

        @pl.when(step == pl.num_programs(0) - 1)
        def _():
            _Scatter(items, sems).wait()

    tok = lambda c: pl.BlockSpec((tm, c), _row)
    vec = pl.BlockSpec((1, D_MODEL), _fixed)
    outs = pl.pallas_call(
        body, grid=(T // tm,),
        in_specs=[pl.BlockSpec((d.shape[0], tm, w), lambda i: (0, i, 0)) for d in dparts]
        + [pl.BlockSpec((PROJ, D_MODEL), _fixed), tok(D_MODEL), tok(D_MODEL), vec] + [HBM] * n_s,
        out_specs=[tok(D_MODEL), vec] + [HBM] * n_s,
        out_shape=[SDS((T, D_MODEL), F32), SDS((1, D_MODEL), F32)] + [SDS(s.shape, s.dtype) for s in to_send],
        scratch_shapes=_scatter_sems(n_s),
        name="in_bwd", compiler_params=_params(("arbitrary",), 52))(*dparts, w_in_t, x, dh1, g1, *to_send)
    return outs[0], outs[1], outs[2:]


def _sum_parts(p_ref):
    g = p_ref[0].astype(F32)
    for s in range(1, N_DEV):
        g = g + p_ref[s].astype(F32)
    return g


def _adamw_math(g, w, m, v):
    nm = ADAM_B1 * m + (1.0 - ADAM_B1) * g
    nv = ADAM_B2 * v + (1.0 - ADAM_B2) * (g * g)
    m_hat = nm / (1.0 - ADAM_B1 ** ADAM_STEP)
    v_hat = nv / (1.0 - ADAM_B2 ** ADAM_STEP)
    return -ADAM_LR * (m_hat / (jnp.sqrt(v_hat) + ADAM_EPS) + ADAM_WD * w), nm, nv


def _row_tile(rows):
    for t in (256, 176, 128, 80):
        if rows % t == 0:
            return t
    raise ValueError(rows)


def _reduce_adamw(parts, w, m, v, name):
    rows, width = w.shape
    tr = _row_tile(rows)

    def body(p_ref, w_ref, m_ref, v_ref, g_ref, d_ref, nm_ref, nv_ref):
        g = _sum_parts(p_ref)
        g_ref[...] = g
        d_ref[...], nm_ref[...], nv_ref[...] = _adamw_math(g, w_ref[...], m_ref[...], v_ref[...])

    blk = pl.BlockSpec((tr, width), _row)
    return pl.pallas_call(
        body, grid=(rows // tr,),
        in_specs=[pl.BlockSpec((N_DEV, tr, width), lambda i: (0, i, 0)), blk, blk, blk],
        out_specs=[blk] * 4, out_shape=[SDS((rows, width), F32)] * 4,
        name="adamw_" + name, compiler_params=_params(("arbitrary",), 32))(parts, w, m, v)


def _reduce(parts, name):
    _, rows, width = parts.shape
    tr = _row_tile(rows)

    def body(p_ref, g_ref):
        g_ref[...] = _sum_parts(p_ref)

    return pl.pallas_call(
        body, grid=(rows // tr,),
        in_specs=[pl.BlockSpec((N_DEV, tr, width), lambda i: (0, i, 0))],
        out_specs=pl.BlockSpec((tr, width), _row), out_shape=SDS((rows, width), F32),
        name="sum_" + name, compiler_params=_params(("arbitrary",), 32))(parts)


def _adamw(g, w, m, v, name):
    rows, width = w.shape
    tr = _row_tile(rows)

    def body(g_ref, w_ref, m_ref, v_ref, d_ref, nm_ref, nv_ref):
        d_ref[...], nm_ref[...], nv_ref[...] = _adamw_math(g_ref[...], w_ref[...], m_ref[...], v_ref[...])

    blk = pl.BlockSpec((tr, width), _row)
    return pl.pallas_call(
        body, grid=(rows // tr,), in_specs=[blk] * 4, out_specs=[blk] * 3, out_shape=[SDS((rows, width), F32)] * 3,
        name="adamw_" + name, compiler_params=_params(("arbitrary",), 32))(g, w, m, v)


SMALL = ("w_spatial", "ln_pre_mix", "ln_post_mix", "ln_pre_ffn", "ln_post_ffn", "b_pe_gate",
         "attn_out_norm", "sgu_out_norm", "b_spatial", "sgu_ln_g", "sgu_ln_b")
ROW_SHARDED = ("w_out", "w_down", "w_pe_gate")
COL_SHARDED = ("w_in", "w_gate_up", "w_pe_proj")
WEIGHTS = ("ln_pre_mix", "w_in", "sgu_ln_g", "sgu_ln_b", "w_spatial", "b_spatial", "attn_out_norm", "sgu_out_norm",
           "w_out", "ln_post_mix", "ln_pre_ffn", "w_gate_up", "w_down", "ln_post_ffn", "w_pe_gate", "b_pe_gate",
           "w_pe_proj")


def _pack_small(t):
    rows = [t["w_spatial"].reshape(-1, D_MODEL)]
    for name in SMALL[1:]:
        flat = t[name].reshape(1, -1)
        rows.append(jnp.pad(flat, ((0, 0), (0, D_MODEL - flat.shape[1]))))
    used = sum(r.shape[0] for r in rows)
    rows.append(jnp.zeros((SMALL_ROWS - used, D_MODEL), F32))
    return jnp.concatenate(rows, axis=0)


def _unpack_small(packed, shapes):
    out = {"w_spatial": packed[:64].reshape(shapes["w_spatial"])}
    for i, name in enumerate(SMALL[1:]):
        size = math.prod(shapes[name])
        out[name] = packed[64 + i, :size].reshape(shapes[name])
    return out


def _slabs(full):
    return full.reshape(N_DEV, full.shape[0] // N_DEV, full.shape[1])


def kernel(x, p, ln_pre_mix, w_in, sgu_ln_g, sgu_ln_b, w_spatial, b_spatial, attn_out_norm, sgu_out_norm, w_out, ln_post_mix, ln_pre_ffn, w_gate_up, w_down, ln_post_ffn, w_pe_gate, b_pe_gate, w_pe_proj, loss_target, m_ln_pre_mix, m_w_in, m_sgu_ln_g, m_sgu_ln_b, m_w_spatial, m_b_spatial, m_attn_out_norm, m_sgu_out_norm, m_w_out, m_ln_post_mix, m_ln_pre_ffn, m_w_gate_up, m_w_down, m_ln_post_ffn, m_w_pe_gate, m_b_pe_gate, m_w_pe_proj, v_ln_pre_mix, v_w_in, v_sgu_ln_g, v_sgu_ln_b, v_w_spatial, v_b_spatial, v_attn_out_norm, v_sgu_out_norm, v_w_out, v_ln_post_mix, v_ln_pre_ffn, v_w_gate_up, v_w_down, v_ln_post_ffn, v_w_pe_gate, v_b_pe_gate, v_w_pe_proj):
    given = dict(locals())
    w = {n: given[n] for n in WEIGHTS}
    m = {n: given["m_" + n] for n in WEIGHTS}
    v = {n: given["v_" + n] for n in WEIGHTS}
    shapes = {n: w[n].shape for n in WEIGHTS}
    xs, ps, target = x[0], p[0, 0], loss_target[0]

    shard = {n: w[n][0].astype(BF) for n in ROW_SHARDED}
    shard.update({n: w[n][0].T.astype(BF) for n in COL_SHARDED})
    sm = {n: w[n][0] for n in SMALL}
    sm = {n: (a.reshape(1, -1) if a.ndim == 1 else a) for n, a in sm.items()}
    slopes = jnp.broadcast_to((2.0 ** -(jnp.arange(8, dtype=F32) + 1.0)).reshape(4, 2, 1), (4, 2, LANES))
    slopes = jnp.concatenate([slopes, jnp.zeros((4, 6, LANES), F32)], axis=1)
    b_st = jnp.pad(sm["b_spatial"].T, ((0, 0), (0, LANES - N_GROUPS)))

    def full(gathered):
        return gathered.reshape(-1, gathered.shape[-1])

    w_in_t = full(_all_gather(shard["w_in"], "gather_w_in"))
    proj, a = _in_proj(xs, sm["ln_pre_mix"], w_in_t)
    later = ("w_out", "w_gate_up", "w_down", "w_pe_gate", "w_pe_proj")
    attn, lse, gathered = _attn_fwd(proj, slopes, [shard[n] for n in later])
    w_out_f, w_gu_t, w_down_f, w_peg_f, w_pep_t = [full(g) for g in gathered]
    sgu = _sgu_fwd(proj, sm["sgu_ln_g"], sm["sgu_ln_b"], sm["w_spatial"], b_st)
    groups, mixed, h1, f = _out_proj(attn, sgu, xs, sm["attn_out_norm"], sm["sgu_out_norm"], w_out_f,
                                     sm["ln_post_mix"], sm["ln_pre_ffn"])
    g, u, act = _gate_up(f, w_gu_t)
    y, h2 = _down_proj(act, w_down_f, h1, sm["ln_post_ffn"])
    dh3, dpp, dpre, h2b, pb, loss_cols, db_peg = _pe_and_loss(h2, ps, target, w_peg_f, sm["b_pe_gate"], w_pep_t)
    loss = lax.psum(0.5 * jnp.sum(loss_cols) * (1.0 / D_MODEL), ("x", "y", "c"))

    arrived = {}
    g_pep_t = _weight_grad(dpp, pb, "w_pe_proj")
    g_peg = _weight_grad(h2b, dpre, "w_pe_gate")
    dh2, dy, d_pff = _pe_bwd(dpre, w_peg_f, dh3, y, sm["ln_post_ffn"])
    g_down = _weight_grad(act, dy, "w_down")
    dg, du, (arrived["w_pe_proj"], arrived["w_pe_gate"]) = _down_bwd(dy, w_down_f, g, u, [_slabs(g_pep_t), _slabs(g_peg)])
    g_gu_t = _weight_grad(dg, f, "w_gate", rows=2 * D_FF)
    g_gu_t = _weight_grad(du, f, "w_up", into=g_gu_t, row_tile=FF_TILES, rows=2 * D_FF)
    dh1, dmx, d_pf, d_pm, (arrived["w_down"],) = _ffn_in_bwd(dg, du, w_gu_t, h1, dh2, mixed, sm["ln_pre_ffn"],
                                                            sm["ln_post_mix"], [_slabs(g_down)])
    g_out = _weight_grad(groups, dmx, "w_out")
    dattn, dsgu, d_ga, d_gs = _out_bwd(dmx, w_out_f, attn, sgu, sm["attn_out_norm"], sm["sgu_out_norm"])
    duz, d_ws, d_bst, d_lg, d_lb = _sgu_bwd(proj, dsgu, sm["sgu_ln_g"], sm["sgu_ln_b"], sm["w_spatial"], b_st)
    dqkv, (arrived["w_gate_up"], arrived["w_out"]) = _attn_bwd(proj, dattn, attn, lse, slopes,
                                                               [_slabs(g_gu_t), _slabs(g_out)])
    g_in_t = _weight_grad(duz, a, "w_in_uz", row_tile=3, rows=PROJ, out_dtype=BF)
    g_in_t = _weight_grad(dqkv, a, "w_in_qkv", into=g_in_t, rows=PROJ, out_dtype=BF)
    grad_x, d_g1, (arrived["w_in"],) = _in_bwd([dqkv, duz], w_in_t, xs, dh1, sm["ln_pre_mix"], [_slabs(g_in_t)])
    gs = dict(ln_pre_mix=d_g1, sgu_ln_g=d_lg, sgu_ln_b=d_lb, w_spatial=d_ws, b_spatial=d_bst[:, :N_GROUPS].T,
              attn_out_norm=d_ga, sgu_out_norm=d_gs, ln_post_mix=d_pm, ln_pre_ffn=d_pf, ln_post_ffn=d_pff,
              b_pe_gate=db_peg)
    (arrived_small,) = _scatter_call([_pack_small(gs)], [False], "small_grad_exchange")

    res = {}
    for n in ROW_SHARDED:
        res[n] = _reduce_adamw(arrived[n], w[n][0], m[n][0], v[n][0], n)
    for n in ("w_in", "w_gate_up"):
        res[n] = [t.T for t in _reduce_adamw(arrived[n], w[n][0].T, m[n][0].T, v[n][0].T, n)]
    for n in ("w_pe_proj",):
        grad = _reduce(arrived[n], n).T
        res[n] = (grad, *_adamw(grad, w[n][0], m[n][0], v[n][0], n))
    small = _reduce_adamw(arrived_small, _pack_small(w), _pack_small(m), _pack_small(v), "small")
    small = [_unpack_small(t, shapes) for t in small]

    out = []
    for k in range(4):
        out += [res[n][k][None] if n in res else small[k][n] for n in WEIGHTS]
    return (loss, grad_x[None], *out)
```

```python
import math

import jax
import jax.numpy as jnp
from jax import lax
from jax.experimental import pallas as pl
from jax.experimental.pallas import tpu as pltpu

F32 = jnp.float32
BF = jnp.bfloat16


def SDS(shape, dtype):
    return pltpu.HBM(tuple(shape), dtype)

D_MODEL = 1024
ATTN_W = 512
SGU_W = 512
HEAD_DIM = 64
N_GROUPS = 4
GROUP_DIM = 128
CHUNK = 128
D_FF = 2816
PLE_DIM = 256
PROJ = 3 * ATTN_W + 2 * SGU_W
DILATIONS = (1, 4, 16)
QBLK = 128
EPS = 1e-6
NEG = -1e30
N_DEV = 8
LANES = 128

ADAM_LR = 0.001
ADAM_B1 = 0.9
ADAM_B2 = 0.999
ADAM_EPS = 1e-08
ADAM_WD = 0.01
ADAM_STEP = 10

SMALL_ROWS = 80
MIB = 2 ** 20
MESH_ID = pl.DeviceIdType.MESH
HBM = pl.BlockSpec(memory_space=pl.ANY)


def _params(sem, vmem_mib):
    return pltpu.CompilerParams(dimension_semantics=sem, vmem_limit_bytes=vmem_mib * MIB)


def _dot(a, b):
    return jnp.dot(a, b, preferred_element_type=F32)


def _dot_nt(a, b):
    return lax.dot_general(a, b, (((1,), (1,)), ((), ())), preferred_element_type=F32)


def _dot_tn(a, b):
    return lax.dot_general(a, b, (((0,), (0,)), ((), ())), preferred_element_type=F32)


def _rstd(x):
    return lax.rsqrt(jnp.mean(x * x, axis=-1, keepdims=True) + EPS)


def _rms_bwd(dy, x, g):
    r = _rstd(x)
    n = x * r
    dn = dy * g
    dx = r * (dn - n * jnp.mean(dn * n, axis=-1, keepdims=True))
    return dx, dy * n


def _colsum(v):
    return jnp.sum(v, axis=0, keepdims=True)


_G0 = math.sqrt(2.0 / math.pi)
_G1 = 0.044715


def _gelu(x):
    t = jnp.tanh(_G0 * (x + _G1 * x * x * x))
    return 0.5 * x * (1.0 + t), t


def _gelu_grad(x, t):
    return 0.5 * (1.0 + t) + 0.5 * x * (1.0 - t * t) * (_G0 * (1.0 + 3.0 * _G1 * x * x))


def _sigmoid(x):
    return 1.0 / (1.0 + jnp.exp(-x))


def _row(i):
    return (i, 0)


def _fixed(i):
    return (0, 0)


def _acc_init(step, *refs):
    @pl.when(step == 0)
    def _():
        for r in refs:
            r[...] = jnp.zeros_like(r)


FLIPS = [(dx, dy, dc) for dx in (0, 1) for dy in (0, 1) for dc in (0, 1)][1:]
DMA_SEMS = pltpu.SemaphoreType.DMA


def _mesh_pos():
    return lax.axis_index("x"), lax.axis_index("y"), lax.axis_index("c")


def _remote(src, dst, sems, n, to):
    return pltpu.make_async_remote_copy(src_ref=src, dst_ref=dst, send_sem=sems[0].at[n], recv_sem=sems[1].at[n],
                                        device_id=to, device_id_type=MESH_ID)


class _Scatter:
    def __init__(self, items, sems):
        x, y, c = _mesh_pos()
        me = 4 * x + 2 * y + c
        self.local, self.sends, self.arrivals = [], [], []
        for i, (src, dst, slabbed) in enumerate(items):
            self.local.append(pltpu.make_async_copy(src.at[me] if slabbed else src, dst.at[me], sems[2].at[i]))
            for k, (dx, dy, dc) in enumerate(FLIPS):
                to = (1 - x if dx else x, 1 - y if dy else y, 1 - c if dc else c)
                peer = 4 * to[0] + 2 * to[1] + to[2]
                out = src.at[peer] if slabbed else src
                self.sends.append(_remote(out, dst.at[me], sems, 7 * i + k, to))
                self.arrivals.append(_remote(out, dst.at[peer], sems, 7 * i + k, to))

    def start(self):
        for cp in self.local + self.sends:
            cp.start()

    def wait(self):
        for cp in self.arrivals:
            cp.wait_recv()
        for cp in self.sends:
            cp.wait_send()
        for cp in self.local:
            cp.wait()


def _scatter_sems(n):
    return [DMA_SEMS((7 * n,)), DMA_SEMS((7 * n,)), DMA_SEMS((n,))]


class _Gather:
    def __init__(self, items, sems):
        x, y, c = _mesh_pos()
        me, sibling = (x, y, c), (x, y, 1 - c)
        chips = [(1 - x, y), (x, 1 - y), (1 - x, 1 - y)]
        self.first, self.passed, self.from_chips, self.rest, self.local = [], [], [], [], []
        for i, (src, dst) in enumerate(items):
            def slot(p, dst=dst):
                return dst.at[4 * p[0] + 2 * p[1] + p[2]]

            def copy(k, block, to, own=False, i=i, src=src, slot=slot):
                return _remote(src if own else slot(block), slot(block), sems, 7 * i + k, to)

            self.local.append(pltpu.make_async_copy(src, slot(me), sems[2].at[i]))
            self.first.append(copy(0, me, sibling, own=True))
            self.first += [copy(1 + j, me, (*chip, c), own=True) for j, chip in enumerate(chips)]
            self.passed += [copy(4 + j, (*chip, c), sibling) for j, chip in enumerate(chips)]
            self.from_chips += [copy(1 + j, (*chip, c), me) for j, chip in enumerate(chips)]
            self.rest.append(copy(0, sibling, me))
            self.rest += [copy(4 + j, (*chip, 1 - c), me) for j, chip in enumerate(chips)]

    def start(self):
        for cp in self.local + self.first:
            cp.start()

    def forward(self):
        for arrived, onward in zip(self.from_chips, self.passed):
            arrived.wait_recv()
            onward.start()

    def finish(self):
        for cp in self.rest:
            cp.wait_recv()
        for cp in self.first + self.passed:
            cp.wait_send()
        for cp in self.local:
            cp.wait()


def _all_gather(shard, name):
    def body(x_ref, out_ref, *sems):
        g = _Gather([(x_ref, out_ref)], sems)
        g.start()
        g.forward()
        g.finish()

    return pl.pallas_call(
        body, out_shape=SDS((N_DEV,) + shard.shape, shard.dtype), in_specs=[HBM], out_specs=HBM,
        scratch_shapes=_scatter_sems(1), name=name)(shard)


def _scatter_call(srcs, slabbed, name):
    n = len(srcs)

    def body(*refs):
        sc = _Scatter(list(zip(refs[:n], refs[n:2 * n], slabbed)), refs[2 * n:])
        sc.start()
        sc.wait()

    shapes = [SDS(s.shape if sl else (N_DEV,) + s.shape, s.dtype) for s, sl in zip(srcs, slabbed)]
    return pl.pallas_call(body, out_shape=shapes, in_specs=[HBM] * n, out_specs=[HBM] * n,
                          scratch_shapes=_scatter_sems(n), name=name)(*srcs)


def _in_proj(x, g1, w_in_t, tm=512):
    T = x.shape[0]

    def body(x_ref, g_ref, w_ref, proj_ref, a_ref):
        xv = x_ref[...]
        a = (xv * _rstd(xv) * g_ref[...]).astype(BF)
        a_ref[...] = a
        proj_ref[...] = _dot_nt(a, w_ref[...])

    return pl.pallas_call(
        body, grid=(T // tm,),
        in_specs=[pl.BlockSpec((tm, D_MODEL), _row), pl.BlockSpec((1, D_MODEL), _fixed),
                  pl.BlockSpec((PROJ, D_MODEL), _fixed)],
        out_specs=[pl.BlockSpec((tm, PROJ), _row), pl.BlockSpec((tm, D_MODEL), _row)],
        out_shape=[SDS((T, PROJ), F32), SDS((T, D_MODEL), BF)],
        name="in_proj", compiler_params=_params(("arbitrary",), 48))(x, g1, w_in_t)


ATTN_GROUP = 16


def _attn_bias(sl_ref, bias_ref):
    qi = lax.broadcasted_iota(jnp.int32, (QBLK, QBLK), 0)
    kj = lax.broadcasted_iota(jnp.int32, (QBLK, QBLK), 1)
    step = qi - kj
    for di, d in enumerate(DILATIONS):
        for j in range(2):
            sl = sl_ref[0, j:j + 1, :]
            cur = jnp.where(step >= 0, -sl * (step * d).astype(F32), NEG)
            prev = jnp.where(step <= 0, -sl * ((step + QBLK) * d).astype(F32), NEG)
            rows = slice(j * QBLK, (j + 1) * QBLK)
            bias_ref[di * 2, rows, :QBLK] = prev
            bias_ref[di * 2, rows, QBLK:] = cur
            bias_ref[di * 2 + 1, rows, :QBLK] = jnp.full((QBLK, QBLK), NEG, F32)
            bias_ref[di * 2 + 1, rows, QBLK:] = cur


def _stack_heads(x, lo):
    return jnp.concatenate([jnp.where(lo, x, 0.0), jnp.where(lo, 0.0, x)], axis=0).astype(BF)


def _unstack_heads(x, lo):
    return jnp.where(lo, x[:QBLK], x[QBLK:])


def _attn_rows(start, d, blocks=1):
    if d == 1:
        return pl.ds(pl.multiple_of(start, QBLK), blocks * QBLK)
    return pl.ds(start, blocks * QBLK, stride=d)


def _attn_group_index(i, nblk, d, group):
    per = nblk // d // group
    r = i // per
    n0 = (i % per) * group
    start = r + (d * QBLK) * n0
    pstart = jnp.maximum(start - d * QBLK, r)
    return start, pstart, n0 == 0


def _attn_plan(nblk, d):
    group = min(ATTN_GROUP, nblk // d)
    return group, max(1, min(ATTN_GROUP // group, d))


def _attn_fwd(proj, slopes, to_gather):
    T = proj.shape[0]
    nblk = T // QBLK
    big = dict(pipeline_mode=pl.Buffered(1))
    n_g = len(to_gather)

    def body(q_ref, k_ref, v_ref, sl_ref, *rest):
        srcs, (o_ref, m_ref), dsts = rest[:n_g], rest[n_g:n_g + 2], rest[n_g + 2:2 * n_g + 2]
        sems, (l_ref, bias_ref) = rest[2 * n_g + 2:2 * n_g + 5], rest[2 * n_g + 5:]
        h = pl.program_id(0)

        @pl.when(h == 0)
        def _():
            _Gather(list(zip(srcs, dsts)), sems).start()

        @pl.when(h == 2)
        def _():
            _Gather(list(zip(srcs, dsts)), sems).forward()

        _attn_bias(sl_ref, bias_ref)
        lo = lax.broadcasted_iota(jnp.int32, (1, LANES), 1) < HEAD_DIM

        for di, d in enumerate(DILATIONS):
            group, segs = _attn_plan(nblk, d)

            def step(i, carry, segs=segs, **kw):
                for s in range(segs):
                    segment(i * segs + s, **kw)
                return carry

            def segment(i, d=d, di=di, group=group):
                start, pstart, first = _attn_group_index(i, nblk, d, group)
                rows, prows = _attn_rows(start, d, group), _attn_rows(pstart, d)
                q = q_ref[rows, :] * (HEAD_DIM ** -0.5)
                k = jnp.concatenate([k_ref[prows, :], k_ref[rows, :]], axis=0).astype(BF)
                v = jnp.concatenate([v_ref[prows, :], v_ref[rows, :]], axis=0).astype(BF)
                for b in range(group):
                    qb = slice(b * QBLK, (b + 1) * QBLK)
                    kb = slice(b * QBLK, (b + 2) * QBLK)
                    bias = bias_ref[di * 2 + first.astype(jnp.int32)] if b == 0 else bias_ref[di * 2]
                    s = _dot_nt(_stack_heads(q[qb], lo), k[kb]) + bias
                    m = jnp.max(s, axis=-1, keepdims=True)
                    pr = jnp.exp(s - m)
                    m_b = _unstack_heads(m, lo)
                    l_b = _unstack_heads(jnp.sum(pr, axis=-1, keepdims=True), lo)
                    o_b = _unstack_heads(_dot(pr.astype(BF), v[kb]), lo)
                    out = _attn_rows(start + b * (d * QBLK), d)
                    if di == 0:
                        m_ref[out, :] = m_b
                        l_ref[out, :] = l_b
                        o_ref[out, :] = o_b
                    else:
                        m_o = m_ref[out, :]
                        m_n = jnp.maximum(m_o, m_b)
                        wa, wb = jnp.exp(m_o - m_n), jnp.exp(m_b - m_n)
                        m_ref[out, :] = m_n
                        l_ref[out, :] = wa * l_ref[out, :] + wb * l_b
                        o_ref[out, :] = wa * o_ref[out, :] + wb * o_b

            lax.fori_loop(0, nblk // (group * segs), step, 0)

        def finish(i, carry):
            rows = pl.ds(pl.multiple_of(i * QBLK, QBLK), QBLK)
            l = l_ref[rows, :]
            o_ref[rows, :] = o_ref[rows, :] / l
            m_ref[rows, :] = m_ref[rows, :] + jnp.log(l)
            return carry

        lax.fori_loop(0, nblk, finish, 0)

        @pl.when(h == pl.num_programs(0) - 1)
        def _():
            _Gather(list(zip(srcs, dsts)), sems).finish()

    col = lambda base: pl.BlockSpec((T, LANES), lambda h: (0, base + h), **big)
    tok = pl.BlockSpec((T, LANES), lambda h: (0, h))
    outs = pl.pallas_call(
        body, grid=(4,),
        in_specs=[col(0), col(4), col(8), pl.BlockSpec((1, 8, LANES), lambda h: (h, 0, 0))] + [HBM] * n_g,
        out_specs=[tok, tok] + [HBM] * n_g,
        out_shape=[SDS((T, ATTN_W), F32), SDS((T, ATTN_W), F32)]
        + [SDS((N_DEV,) + g.shape, g.dtype) for g in to_gather],
        scratch_shapes=_scatter_sems(n_g) + [pltpu.VMEM((T, LANES), F32), pltpu.VMEM((6, 2 * QBLK, 2 * QBLK), F32)],
        name="attn_fwd", compiler_params=_params(("arbitrary",), 48))(proj, proj, proj, slopes, *to_gather)
    return outs[0], outs[1], outs[2:]


def _sgu_norm(zv, ln_g, ln_b):
    gz, tz = _gelu(zv)
    mu = jnp.mean(gz, axis=-1, keepdims=True)
    xc = gz - mu
    rs = lax.rsqrt(jnp.mean(xc * xc, axis=-1, keepdims=True) + EPS)
    xhat = xc * rs
    return xhat * ln_g + ln_b, xhat, rs, tz


def _causal(w):
    i = lax.broadcasted_iota(jnp.int32, (CHUNK, CHUNK), 0)
    j = lax.broadcasted_iota(jnp.int32, (CHUNK, CHUNK), 1)
    return jnp.where(i >= j, w, 0.0)


def _sgu_fwd(proj, ln_g, ln_b, w_s, b_st, tm=512):
    T = proj.shape[0]

    def body(u_ref, z_ref, g_ref, b_ref, w_ref, bs_ref, out_ref):
        for g in range(N_GROUPS):
            wm = _causal(w_ref[g]).astype(BF)
            cols = slice(g * GROUP_DIM, (g + 1) * GROUP_DIM)
            for c in range(tm // CHUNK):
                rows = slice(c * CHUNK, (c + 1) * CHUNK)
                zn, _, _, _ = _sgu_norm(z_ref[rows, cols], g_ref[...], b_ref[...])
                mixed = _dot(wm, zn.astype(BF)) + bs_ref[:, g:g + 1]
                gu, _ = _gelu(u_ref[rows, cols])
                out_ref[rows, cols] = gu * mixed

    return pl.pallas_call(
        body, grid=(T // tm,),
        in_specs=[pl.BlockSpec((tm, SGU_W), lambda i: (i, 3)), pl.BlockSpec((tm, SGU_W), lambda i: (i, 4)),
                  pl.BlockSpec((1, GROUP_DIM), _fixed), pl.BlockSpec((1, GROUP_DIM), _fixed),
                  pl.BlockSpec((N_GROUPS, CHUNK, CHUNK), lambda i: (0, 0, 0)), pl.BlockSpec((CHUNK, LANES), _fixed)],
        out_specs=pl.BlockSpec((tm, SGU_W), _row),
        out_shape=SDS((T, SGU_W), F32),
        name="sgu_fwd", compiler_params=_params(("arbitrary",), 32))(proj, proj, ln_g, ln_b, w_s, b_st)


def _out_proj(attn, sgu, x, g_a, g_s, w_out, g_pm, g_pf, tm=512):
    T = x.shape[0]

    def body(a_ref, s_ref, x_ref, ga_ref, gs_ref, w_ref, gpm_ref, gpf_ref, grp_ref, mixed_ref, h1_ref, f_ref):
        av, sv = a_ref[...], s_ref[...]
        an = (av * _rstd(av) * ga_ref[...]).astype(BF)
        sn = (sv * _rstd(sv) * gs_ref[...]).astype(BF)
        grp_ref[:, :ATTN_W] = an
        grp_ref[:, ATTN_W:] = sn
        mixed = _dot(an, w_ref[:ATTN_W, :]) + _dot(sn, w_ref[ATTN_W:, :])
        mixed_ref[...] = mixed
        h1 = x_ref[...] + mixed * _rstd(mixed) * gpm_ref[...]
        h1_ref[...] = h1
        f_ref[...] = (h1 * _rstd(h1) * gpf_ref[...]).astype(BF)

    tok = lambda w: pl.BlockSpec((tm, w), _row)
    vec = lambda w: pl.BlockSpec((1, w), _fixed)
    return pl.pallas_call(
        body, grid=(T // tm,),
        in_specs=[tok(ATTN_W), tok(SGU_W), tok(D_MODEL), vec(ATTN_W), vec(SGU_W),
                  pl.BlockSpec((D_MODEL, D_MODEL), _fixed), vec(D_MODEL), vec(D_MODEL)],
        out_specs=[tok(D_MODEL)] * 4,
        out_shape=[SDS((T, D_MODEL), BF), SDS((T, D_MODEL), F32), SDS((T, D_MODEL), F32), SDS((T, D_MODEL), BF)],
        name="out_proj", compiler_params=_params(("arbitrary",), 48))(attn, sgu, x, g_a, g_s, w_out, g_pm, g_pf)


FF_TILE = 1408
FF_TILES = D_FF // FF_TILE


def _gate_up(f, w_gu_t, tm=512):
    T = f.shape[0]
    tn = FF_TILE

    def body(f_ref, wg_ref, wu_ref, g_ref, u_ref, act_ref):
        fv = f_ref[...]
        g = _dot_nt(fv, wg_ref[...])
        u = _dot_nt(fv, wu_ref[...])
        g_ref[...] = g.astype(BF)
        u_ref[...] = u.astype(BF)
        act_ref[...] = (g * _sigmoid(g) * u).astype(BF)

    ospec = pl.BlockSpec((tm, tn), lambda j, i: (i, j))
    return pl.pallas_call(
        body, grid=(FF_TILES, T // tm),
        in_specs=[pl.BlockSpec((tm, D_MODEL), lambda j, i: (i, 0)), pl.BlockSpec((tn, D_MODEL), lambda j, i: (j, 0)),
                  pl.BlockSpec((tn, D_MODEL), lambda j, i: (j + FF_TILES, 0))],
        out_specs=[ospec] * 3, out_shape=[SDS((T, D_FF), BF)] * 3,
        name="gate_up", compiler_params=_params(("arbitrary", "arbitrary"), 40))(f, w_gu_t, w_gu_t)


def _down_proj(act, w_down, h1, g_pff, tm=512):
    T = act.shape[0]

    def body(a_ref, w_ref, h1_ref, g_ref, y_ref, h2_ref):
        y = _dot(a_ref[...], w_ref[...])
        y_ref[...] = y
        h2_ref[...] = h1_ref[...] + y * _rstd(y) * g_ref[...]

    return pl.pallas_call(
        body, grid=(T // tm,),
        in_specs=[pl.BlockSpec((tm, D_FF), _row), pl.BlockSpec((D_FF, D_MODEL), _fixed),
                  pl.BlockSpec((tm, D_MODEL), _row), pl.BlockSpec((1, D_MODEL), _fixed)],
        out_specs=[pl.BlockSpec((tm, D_MODEL), _row)] * 2,
        out_shape=[SDS((T, D_MODEL), F32)] * 2,
        name="down_proj", compiler_params=_params(("arbitrary",), 48))(act, w_down, h1, g_pff)


def _pe_and_loss(h2, p, target, w_peg, b_peg, w_pep_t, tm=512):
    T = h2.shape[0]

    def body(h2_ref, p_ref, t_ref, wg_ref, b_ref, wp_ref,
             dh3_ref, dpp_ref, dpre_ref, h2b_ref, pb_ref, loss_ref, db_ref):
        _acc_init(pl.program_id(0), loss_ref, db_ref)
        h2v = h2_ref[...]
        h2b = h2v.astype(BF)
        pb = p_ref[...].astype(BF)
        h2b_ref[...] = h2b
        pb_ref[...] = pb
        gate = _sigmoid(_dot(h2b, wg_ref[...]) + b_ref[...])
        pp = _dot_nt(pb, wp_ref[...])
        diff = h2v + gate * pp - t_ref[...]
        loss_ref[...] += _colsum(diff * diff)
        dh3 = diff * (1.0 / D_MODEL)
        dh3_ref[...] = dh3
        dpp_ref[...] = (dh3 * gate).astype(BF)
        dpre = dh3 * pp * (gate * (1.0 - gate))
        dpre_ref[...] = dpre.astype(BF)
        db_ref[...] += _colsum(dpre)

    tok = lambda w: pl.BlockSpec((tm, w), _row)
    vec = pl.BlockSpec((1, D_MODEL), _fixed)
    return pl.pallas_call(
        body, grid=(T // tm,),
        in_specs=[tok(D_MODEL), tok(PLE_DIM), tok(D_MODEL), pl.BlockSpec((D_MODEL, D_MODEL), _fixed), vec,
                  pl.BlockSpec((D_MODEL, PLE_DIM), _fixed)],
        out_specs=[tok(D_MODEL), tok(D_MODEL), tok(D_MODEL), tok(D_MODEL), tok(PLE_DIM), vec, vec],
        out_shape=[SDS((T, D_MODEL), F32), SDS((T, D_MODEL), BF), SDS((T, D_MODEL), BF), SDS((T, D_MODEL), BF),
                   SDS((T, PLE_DIM), BF), SDS((1, D_MODEL), F32), SDS((1, D_MODEL), F32)],
        name="pe_and_loss", compiler_params=_params(("arbitrary",), 48))(h2, p, target, w_peg, b_peg, w_pep_t)


def _weight_grad(a, dy, name, into=None, row_tile=0, rows=None, out_dtype=F32, tk=512):
    n = dy.shape[1]
    tn = min(n, 1024)
    T, ka = a.shape
    tka = FF_TILE if ka == D_FF else min(ka, 1024)
    rows = ka if rows is None else rows
    direct = out_dtype == F32

    def body(a_ref, dy_ref, *rest):
        out_ref, acc_ref = (rest[-1], rest[-1]) if direct else rest[-2:]
        _acc_init(pl.program_id(2), acc_ref)
        acc_ref[...] += _dot_tn(a_ref[...].astype(BF), dy_ref[...].astype(BF))
        if not direct:
            @pl.when(pl.program_id(2) == pl.num_programs(2) - 1)
            def _():
                out_ref[...] = acc_ref[...].astype(out_dtype)

    carried = [] if into is None else [into]
    return pl.pallas_call(
        body, grid=(ka // tka, n // tn, T // tk),
        in_specs=[pl.BlockSpec((tk, tka), lambda i, j, k: (k, i)), pl.BlockSpec((tk, tn), lambda i, j, k: (k, j))]
        + [HBM] * len(carried),
        out_specs=pl.BlockSpec((tka, tn), lambda i, j, k: (i + row_tile, j)),
        out_shape=SDS((rows, n), out_dtype), input_output_aliases={2: 0} if carried else {},
        scratch_shapes=[] if direct else [pltpu.VMEM((tka, tn), F32)],
        name="grad_" + name, compiler_params=_params(("arbitrary",) * 3, 40))(a, dy, *carried)


def _grad_w_in(dparts, a, tk=512):
    T = a.shape[0]

    def body(*refs):
        d_refs, a_ref, out_ref, acc_ref = refs[:len(dparts)], refs[-3], refs[-2], refs[-1]
        k = pl.program_id(0)
        _acc_init(k, acc_ref)
        cols = [r[part].astype(BF) for r in d_refs for part in range(r.shape[0])]
        acc_ref[...] += _dot_tn(jnp.concatenate(cols, axis=1), a_ref[...])

        @pl.when(k == pl.num_programs(0) - 1)
        def _():
            out_ref[...] = acc_ref[...].astype(BF)

    return pl.pallas_call(
        body, grid=(T // tk,),
        in_specs=[pl.BlockSpec((d.shape[0], tk, d.shape[2]), lambda k: (0, k, 0)) for d in dparts]
        + [pl.BlockSpec((tk, D_MODEL), lambda k: (k, 0))],
        out_specs=pl.BlockSpec((PROJ, D_MODEL), lambda k: (0, 0)),
        out_shape=SDS((PROJ, D_MODEL), BF), scratch_shapes=[pltpu.VMEM((PROJ, D_MODEL), F32)],
        name="grad_w_in", compiler_params=_params(("arbitrary",), 48))(*dparts, a)


def _pe_bwd(dpre, w_peg, dh3, y, g_pff, tm=512):
    T = y.shape[0]

    def body(dp_ref, w_ref, dh3_ref, y_ref, g_ref, dh2_ref, dy_ref, dg_ref):
        _acc_init(pl.program_id(0), dg_ref)
        dh2 = dh3_ref[...] + _dot_nt(dp_ref[...], w_ref[...])
        dh2_ref[...] = dh2
        dy, dg = _rms_bwd(dh2, y_ref[...], g_ref[...])
        dy_ref[...] = dy.astype(BF)
        dg_ref[...] += _colsum(dg)

    tok = pl.BlockSpec((tm, D_MODEL), _row)
    vec = pl.BlockSpec((1, D_MODEL), _fixed)
    return pl.pallas_call(
        body, grid=(T // tm,),
        in_specs=[tok, pl.BlockSpec((D_MODEL, D_MODEL), _fixed), tok, tok, vec],
        out_specs=[tok, tok, vec],
        out_shape=[SDS((T, D_MODEL), F32), SDS((T, D_MODEL), BF), SDS((1, D_MODEL), F32)],
        name="pe_bwd", compiler_params=_params(("arbitrary",), 40))(dpre, w_peg, dh3, y, g_pff)


def _down_bwd(dy, w_down, g, u, to_send, tm=512):
    T = dy.shape[0]
    tn = FF_TILE
    n_s = len(to_send)

    def body(dy_ref, w_ref, g_ref, u_ref, *rest):
        srcs, (dg_ref, du_ref), dsts, sems = rest[:n_s], rest[n_s:n_s + 2], rest[n_s + 2:2 * n_s + 2], rest[2 * n_s + 2:]
        j, i = pl.program_id(0), pl.program_id(1)
        items = list(zip(srcs, dsts, [True] * n_s))

        @pl.when((j == 0) & (i == 0))
        def _():
            _Scatter(items, sems).start()

        dact = _dot_nt(dy_ref[...], w_ref[...])
        gv = g_ref[...].astype(F32)
        uv = u_ref[...].astype(F32)
        s = _sigmoid(gv)
        dg_ref[...] = (dact * uv * (s * (1.0 + gv * (1.0 - s)))).astype(BF)
        du_ref[...] = (dact * (gv * s)).astype(BF)

        @pl.when((j == pl.num_programs(0) - 1) & (i == pl.num_programs(1) - 1))
        def _():
            _Scatter(items, sems).wait()

    tile = pl.BlockSpec((tm, tn), lambda j, i: (i, j))
    outs = pl.pallas_call(
        body, grid=(FF_TILES, T // tm),
        in_specs=[pl.BlockSpec((tm, D_MODEL), lambda j, i: (i, 0)), pl.BlockSpec((tn, D_MODEL), lambda j, i: (j, 0)),
                  tile, tile] + [HBM] * n_s,
        out_specs=[tile, tile] + [HBM] * n_s,
        out_shape=[SDS((T, D_FF), BF)] * 2 + [SDS(s.shape, s.dtype) for s in to_send],
        scratch_shapes=_scatter_sems(n_s),
        name="down_bwd", compiler_params=_params(("arbitrary", "arbitrary"), 40))(dy, w_down, g, u, *to_send)
    return outs[0], outs[1], outs[2:]


def _ffn_in_bwd(dg, du, w_gu_t, h1, dh2, mixed, g_pf, g_pm, to_send, tm=512):
    T = h1.shape[0]
    n_s = len(to_send)

    def body(dg_ref, du_ref, wg_ref, wu_ref, h1_ref, dh2_ref, mx_ref, gpf_ref, gpm_ref, *rest):
        srcs, outs, dsts, sems = rest[:n_s], rest[n_s:n_s + 4], rest[n_s + 4:2 * n_s + 4], rest[2 * n_s + 4:]
        dh1_ref, dmx_ref, dgpf_ref, dgpm_ref = outs
        i = pl.program_id(0)
        items = list(zip(srcs, dsts, [True] * n_s))
        _acc_init(i, dgpf_ref, dgpm_ref)

        @pl.when(i == 0)
        def _():
            _Scatter(items, sems).start()

        df = _dot(dg_ref[...], wg_ref[...]) + _dot(du_ref[...], wu_ref[...])
        dx, dgf = _rms_bwd(df, h1_ref[...], gpf_ref[...])
        dh1 = dh2_ref[...] + dx
        dh1_ref[...] = dh1
        dmx, dgm = _rms_bwd(dh1, mx_ref[...], gpm_ref[...])
        dmx_ref[...] = dmx.astype(BF)
        dgpf_ref[...] += _colsum(dgf)
        dgpm_ref[...] += _colsum(dgm)

        @pl.when(i == pl.num_programs(0) - 1)
        def _():
            _Scatter(items, sems).wait()

    tok = lambda w: pl.BlockSpec((tm, w), _row)
    vec = pl.BlockSpec((1, D_MODEL), _fixed)
    outs = pl.pallas_call(
        body, grid=(T // tm,),
        in_specs=[tok(D_FF), tok(D_FF), pl.BlockSpec((D_FF, D_MODEL), lambda i: (0, 0), pipeline_mode=pl.Buffered(1)),
                  pl.BlockSpec((D_FF, D_MODEL), lambda i: (1, 0), pipeline_mode=pl.Buffered(1)),
                  tok(D_MODEL), tok(D_MODEL), tok(D_MODEL), vec, vec]
        + [HBM] * n_s,
        out_specs=[tok(D_MODEL), tok(D_MODEL), vec, vec] + [HBM] * n_s,
        out_shape=[SDS((T, D_MODEL), F32), SDS((T, D_MODEL), BF), SDS((1, D_MODEL), F32), SDS((1, D_MODEL), F32)]
        + [SDS(s.shape, s.dtype) for s in to_send],
        scratch_shapes=_scatter_sems(n_s),
        name="ffn_in_bwd", compiler_params=_params(("arbitrary",), 56))(
            dg, du, w_gu_t, w_gu_t, h1, dh2, mixed, g_pf, g_pm, *to_send)
    return outs[0], outs[1], outs[2], outs[3], outs[4:]


def _out_bwd(dmx, w_out, attn, sgu, g_a, g_s, tm=512):
    T = attn.shape[0]

    def body(dm_ref, w_ref, a_ref, s_ref, ga_ref, gs_ref, da_ref, ds_ref, dga_ref, dgs_ref):
        _acc_init(pl.program_id(0), dga_ref, dgs_ref)
        dgr = _dot_nt(dm_ref[...], w_ref[...])
        da, dga = _rms_bwd(dgr[:, :ATTN_W], a_ref[...], ga_ref[...])
        ds, dgs = _rms_bwd(dgr[:, ATTN_W:], s_ref[...], gs_ref[...])
        da_ref[...] = da
        ds_ref[...] = ds
        dga_ref[...] += _colsum(dga)
        dgs_ref[...] += _colsum(dgs)

    tok = lambda w: pl.BlockSpec((tm, w), _row)
    vec = lambda w: pl.BlockSpec((1, w), _fixed)
    return pl.pallas_call(
        body, grid=(T // tm,),
        in_specs=[tok(D_MODEL), pl.BlockSpec((D_MODEL, D_MODEL), _fixed), tok(ATTN_W), tok(SGU_W), vec(ATTN_W), vec(SGU_W)],
        out_specs=[tok(ATTN_W), tok(SGU_W), vec(ATTN_W), vec(SGU_W)],
        out_shape=[SDS((T, ATTN_W), F32), SDS((T, SGU_W), F32), SDS((1, ATTN_W), F32), SDS((1, SGU_W), F32)],
        name="out_bwd", compiler_params=_params(("arbitrary",), 40))(dmx, w_out, attn, sgu, g_a, g_s)


def _sgu_bwd(proj, dsgu, ln_g, ln_b, w_s, b_st, tm=512):
    T = proj.shape[0]

    def body(u_ref, z_ref, ds_ref, g_ref, b_ref, w_ref, bs_ref,
             duz_ref, dw_ref, dbs_ref, dlg_ref, dlb_ref, dbacc_ref):
        du_ref, dz_ref = duz_ref.at[0], duz_ref.at[1]
        step = pl.program_id(0)
        _acc_init(step, dw_ref, dbs_ref, dlg_ref, dlb_ref, dbacc_ref)
        lng, lnb = g_ref[...], b_ref[...]
        for g in range(N_GROUPS):
            wm = _causal(w_ref[g]).astype(BF)
            cols = slice(g * GROUP_DIM, (g + 1) * GROUP_DIM)
            for c in range(tm // CHUNK):
                rows = slice(c * CHUNK, (c + 1) * CHUNK)
                zv, uv, dout = z_ref[rows, cols], u_ref[rows, cols], ds_ref[rows, cols]
                zn, xhat, rs, tz = _sgu_norm(zv, lng, lnb)
                znb = zn.astype(BF)
                mixed = _dot(wm, znb) + bs_ref[:, g:g + 1]
                gu, tu = _gelu(uv)
                du_ref[rows, cols] = (dout * mixed * _gelu_grad(uv, tu)).astype(BF)
                dmix = dout * gu
                dmb = dmix.astype(BF)
                dw_ref[g] += _causal(_dot_nt(dmb, znb))
                dbacc_ref[g] += dmix
                dzn = _dot_tn(wm, dmb)
                dlg_ref[...] += _colsum(dzn * xhat)
                dlb_ref[...] += _colsum(dzn)
                dxh = dzn * lng
                dgz = rs * (dxh - jnp.mean(dxh, axis=-1, keepdims=True)
                            - xhat * jnp.mean(dxh * xhat, axis=-1, keepdims=True))
                dz_ref[rows, cols] = (dgz * _gelu_grad(zv, tz)).astype(BF)

        @pl.when(step == pl.num_programs(0) - 1)
        def _():
            lane = lax.broadcasted_iota(jnp.int32, (CHUNK, LANES), 1)
            acc = jnp.zeros((CHUNK, LANES), F32)
            for g in range(N_GROUPS):
                acc = jnp.where(lane == g, jnp.sum(dbacc_ref[g], axis=-1, keepdims=True), acc)
            dbs_ref[...] = acc

    tok = pl.BlockSpec((tm, SGU_W), _row)
    vec = pl.BlockSpec((1, GROUP_DIM), _fixed)
    wsp = pl.BlockSpec((N_GROUPS, CHUNK, CHUNK), lambda i: (0, 0, 0))
    sq = pl.BlockSpec((CHUNK, LANES), _fixed)
    return pl.pallas_call(
        body, grid=(T // tm,),
        in_specs=[pl.BlockSpec((tm, SGU_W), lambda i: (i, 3)), pl.BlockSpec((tm, SGU_W), lambda i: (i, 4)), tok,
                  vec, vec, wsp, sq],
        out_specs=[pl.BlockSpec((2, tm, SGU_W), lambda i: (0, i, 0)), wsp, sq, vec, vec],
        out_shape=[SDS((2, T, SGU_W), BF), SDS((N_GROUPS, CHUNK, CHUNK), F32),
                   SDS((CHUNK, LANES), F32), SDS((1, GROUP_DIM), F32), SDS((1, GROUP_DIM), F32)],
        scratch_shapes=[pltpu.VMEM((N_GROUPS, CHUNK, LANES), F32)],
        name="sgu_bwd", compiler_params=_params(("arbitrary",), 32))(proj, proj, dsgu, ln_g, ln_b, w_s, b_st)


def _attn_bwd(proj, do, o, lse, slopes, to_send):
    T = proj.shape[0]
    nblk = T // QBLK
    big = dict(pipeline_mode=pl.Buffered(1))
    n_s = len(to_send)

    def body(q_ref, k_ref, v_ref, do_ref, o_ref, l_ref, sl_ref, *rest):
        srcs, d_ref, dsts = rest[:n_s], rest[n_s], rest[n_s + 1:2 * n_s + 1]
        sems, bias_ref = rest[2 * n_s + 1:2 * n_s + 4], rest[2 * n_s + 4]
        dq_ref, dk_ref, dv_ref = d_ref.at[0], d_ref.at[1], d_ref.at[2]
        h = pl.program_id(0)
        items = list(zip(srcs, dsts, [True] * n_s))

        @pl.when(h == 0)
        def _():
            _Scatter(items, sems).start()

        _attn_bias(sl_ref, bias_ref)
        lo = lax.broadcasted_iota(jnp.int32, (1, LANES), 1) < HEAD_DIM
        scale = HEAD_DIM ** -0.5
        d_ref[...] = jnp.zeros_like(d_ref)

        for di, d in enumerate(DILATIONS):
            group, segs = _attn_plan(nblk, d)

            def step(i, carry, segs=segs, **kw):
                for s in range(segs):
                    segment(i * segs + s, **kw)
                return carry

            def segment(i, d=d, di=di, group=group):
                start, pstart, first = _attn_group_index(i, nblk, d, group)
                rows, prows = _attn_rows(start, d, group), _attn_rows(pstart, d)
                q = q_ref[rows, :] * scale
                k = jnp.concatenate([k_ref[prows, :], k_ref[rows, :]], axis=0).astype(BF)
                v = jnp.concatenate([v_ref[prows, :], v_ref[rows, :]], axis=0).astype(BF)
                dov = do_ref[rows, :]
                prod = dov * o_ref[rows, :]
                lse_g = l_ref[rows, :]
                masks = [lo, ~lo]
                qm = [jnp.where(masks[j], q, 0.0).astype(BF) for j in range(2)]
                dom = [jnp.where(masks[j], dov, 0.0).astype(BF) for j in range(2)]
                for b in range(group):
                    qb = slice(b * QBLK, (b + 1) * QBLK)
                    kb = slice(b * QBLK, (b + 2) * QBLK)
                    which = di * 2 + first.astype(jnp.int32) if b == 0 else di * 2
                    dq_parts, prs, dss = [], [], []
                    for j in range(2):
                        bias = bias_ref[which, j * QBLK:(j + 1) * QBLK, :]
                        delta = jnp.sum(jnp.where(masks[j], prod[qb], 0.0), axis=-1, keepdims=True)
                        lj = lse_g[qb, j * HEAD_DIM:j * HEAD_DIM + 1]
                        pr = jnp.exp(_dot_nt(qm[j][qb], k[kb]) + bias - lj)
                        ds = (pr * (_dot_nt(dom[j][qb], v[kb]) - delta)).astype(BF)
                        dq_parts.append(_dot(ds, k[kb]))
                        prs.append(pr.astype(BF))
                        dss.append(ds)
                    dk_b = _dot_tn(jnp.concatenate(dss, axis=0), jnp.concatenate([qm[0][qb], qm[1][qb]], axis=0))
                    dv_b = _dot_tn(jnp.concatenate(prs, axis=0), jnp.concatenate([dom[0][qb], dom[1][qb]], axis=0))
                    own = _attn_rows(start + b * (d * QBLK), d)
                    dq_ref[own, :] += jnp.where(lo, dq_parts[0], dq_parts[1]) * scale
                    if b == 0:
                        dk_ref[prows, :] += dk_b[:QBLK]
                        dv_ref[prows, :] += dv_b[:QBLK]
                        dk_ref[own, :] += dk_b[QBLK:]
                        dv_ref[own, :] += dv_b[QBLK:]
                    else:
                        two = _attn_rows(start + (b - 1) * (d * QBLK), d, 2)
                        dk_ref[two, :] += dk_b
                        dv_ref[two, :] += dv_b

            lax.fori_loop(0, nblk // (group * segs), step, 0)

        @pl.when(h == pl.num_programs(0) - 1)
        def _():
            _Scatter(items, sems).wait()

    col = lambda base: pl.BlockSpec((T, LANES), lambda h: (0, base + h), **big)
    outs = pl.pallas_call(
        body, grid=(4,),
        in_specs=[col(0), col(4), col(8), col(0), col(0), col(0), pl.BlockSpec((1, 8, LANES), lambda h: (h, 0, 0))]
        + [HBM] * n_s,
        out_specs=[pl.BlockSpec((3, T, LANES), lambda h: (0, 0, h))] + [HBM] * n_s,
        out_shape=[SDS((3, T, ATTN_W), F32)] + [SDS(s.shape, s.dtype) for s in to_send],
        scratch_shapes=_scatter_sems(n_s) + [pltpu.VMEM((6, 2 * QBLK, 2 * QBLK), F32)],
        name="attn_bwd", compiler_params=_params(("arbitrary",), 60))(proj, proj, proj, do, o, lse, slopes, *to_send)
    return outs[0], outs[1:]


def _in_bwd(dparts, w_in_t, x, dh1, g1, to_send, slabbed, tm=512):
    T = x.shape[0]
    n = len(dparts)
    n_s = len(to_send)
    w = ATTN_W

    def body(*refs):
        d_refs, (w_ref, x_ref, dh1_ref, g_ref), rest = refs[:n], refs[n:n + 4], refs[n + 4:]
        srcs, (dx_ref, dg_ref), dsts, sems = rest[:n_s], rest[n_s:n_s + 2], rest[n_s + 2:2 * n_s + 2], rest[2 * n_s + 2:]
        step = pl.program_id(0)
        items = list(zip(srcs, dsts, slabbed))
        _acc_init(step, dg_ref)

        @pl.when(step == 0)
        def _():
            _Scatter(items, sems).start()

        da = None
        col = 0
        for r in d_refs:
            for part in range(r.shape[0]):
                t = _dot(r[part].astype(BF), w_ref[col * w:(col + 1) * w, :])
                da = t if da is None else da + t
                col += 1
        dx, dg = _rms_bwd(da, x_ref[...], g_ref[...])
        dx_ref[...] = dh1_ref[...] + dx
        dg_ref[...] += _colsum(dg)

        @pl.when(step == pl.num_programs(0) - 1)
        def _():
            _Scatter(items, sems).wait()

    tok = lambda c: pl.BlockSpec((tm, c), _row)
    vec = pl.BlockSpec((1, D_MODEL), _fixed)
    outs = pl.pallas_call(
        body, grid=(T // tm,),
        in_specs=[pl.BlockSpec((d.shape[0], tm, w), lambda i: (0, i, 0)) for d in dparts]
        + [pl.BlockSpec((PROJ, D_MODEL), _fixed), tok(D_MODEL), tok(D_MODEL), vec] + [HBM] * n_s,
        out_specs=[tok(D_MODEL), vec] + [HBM] * n_s,
        out_shape=[SDS((T, D_MODEL), F32), SDS((1, D_MODEL), F32)]
        + [SDS(s.shape if sl else (N_DEV,) + s.shape, s.dtype) for s, sl in zip(to_send, slabbed)],
        scratch_shapes=_scatter_sems(n_s),
        name="in_bwd", compiler_params=_params(("arbitrary",), 52))(*dparts, w_in_t, x, dh1, g1, *to_send)
    return outs[0], outs[1], outs[2:]


def _sum_parts(p_ref):
    g = p_ref[0].astype(F32)
    for s in range(1, N_DEV):
        g = g + p_ref[s].astype(F32)
    return g


def _adamw_math(g, w, m, v):
    nm = ADAM_B1 * m + (1.0 - ADAM_B1) * g
    nv = ADAM_B2 * v + (1.0 - ADAM_B2) * (g * g)
    m_hat = nm / (1.0 - ADAM_B1 ** ADAM_STEP)
    v_hat = nv / (1.0 - ADAM_B2 ** ADAM_STEP)
    return -ADAM_LR * (m_hat / (jnp.sqrt(v_hat) + ADAM_EPS) + ADAM_WD * w), nm, nv


def _row_tile(rows):
    for t in (256, 176, 128, 80):
        if rows % t == 0:
            return t
    raise ValueError(rows)


def _reduce_adamw(parts, w, m, v, name):
    rows, width = w.shape
    tr = _row_tile(rows)

    def body(p_ref, w_ref, m_ref, v_ref, g_ref, d_ref, nm_ref, nv_ref):
        g = _sum_parts(p_ref)
        g_ref[...] = g
        d_ref[...], nm_ref[...], nv_ref[...] = _adamw_math(g, w_ref[...], m_ref[...], v_ref[...])

    blk = pl.BlockSpec((tr, width), _row)
    return pl.pallas_call(
        body, grid=(rows // tr,),
        in_specs=[pl.BlockSpec((N_DEV, tr, width), lambda i: (0, i, 0)), blk, blk, blk],
        out_specs=[blk] * 4, out_shape=[SDS((rows, width), F32)] * 4,
        name="adamw_" + name, compiler_params=_params(("arbitrary",), 32))(parts, w, m, v)


def _reduce(parts, name):
    _, rows, width = parts.shape
    tr = _row_tile(rows)

    def body(p_ref, g_ref):
        g_ref[...] = _sum_parts(p_ref)

    return pl.pallas_call(
        body, grid=(rows // tr,),
        in_specs=[pl.BlockSpec((N_DEV, tr, width), lambda i: (0, i, 0))],
        out_specs=pl.BlockSpec((tr, width), _row), out_shape=SDS((rows, width), F32),
        name="sum_" + name, compiler_params=_params(("arbitrary",), 32))(parts)


def _adamw(g, w, m, v, name):
    rows, width = w.shape
    tr = _row_tile(rows)

    def body(g_ref, w_ref, m_ref, v_ref, d_ref, nm_ref, nv_ref):
        d_ref[...], nm_ref[...], nv_ref[...] = _adamw_math(g_ref[...], w_ref[...], m_ref[...], v_ref[...])

    blk = pl.BlockSpec((tr, width), _row)
    return pl.pallas_call(
        body, grid=(rows // tr,), in_specs=[blk] * 4, out_specs=[blk] * 3, out_shape=[SDS((rows, width), F32)] * 3,
        name="adamw_" + name, compiler_params=_params(("arbitrary",), 32))(g, w, m, v)


SMALL = ("w_spatial", "ln_pre_mix", "ln_post_mix", "ln_pre_ffn", "ln_post_ffn", "b_pe_gate",
         "attn_out_norm", "sgu_out_norm", "b_spatial", "sgu_ln_g", "sgu_ln_b")
SMALL_G1_ROW = 64 + SMALL.index("ln_pre_mix") - 1
ROW_SHARDED = ("w_out", "w_down", "w_pe_gate")
COL_SHARDED = ("w_in", "w_gate_up", "w_pe_proj")
WEIGHTS = ("ln_pre_mix", "w_in", "sgu_ln_g", "sgu_ln_b", "w_spatial", "b_spatial", "attn_out_norm", "sgu_out_norm",
           "w_out", "ln_post_mix", "ln_pre_ffn", "w_gate_up", "w_down", "ln_post_ffn", "w_pe_gate", "b_pe_gate",
           "w_pe_proj")


def _pack_small(t):
    rows = [t["w_spatial"].reshape(-1, D_MODEL)]
    for name in SMALL[1:]:
        flat = t[name].reshape(1, -1)
        rows.append(jnp.pad(flat, ((0, 0), (0, D_MODEL - flat.shape[1]))))
    used = sum(r.shape[0] for r in rows)
    rows.append(jnp.zeros((SMALL_ROWS - used, D_MODEL), F32))
    return jnp.concatenate(rows, axis=0)


def _unpack_small(packed, shapes):
    out = {"w_spatial": packed[:64].reshape(shapes["w_spatial"])}
    for i, name in enumerate(SMALL[1:]):
        size = math.prod(shapes[name])
        out[name] = packed[64 + i, :size].reshape(shapes[name])
    return out


def _slabs(full):
    return full.reshape(N_DEV, full.shape[0] // N_DEV, full.shape[1])


def kernel(x, p, ln_pre_mix, w_in, sgu_ln_g, sgu_ln_b, w_spatial, b_spatial, attn_out_norm, sgu_out_norm, w_out, ln_post_mix, ln_pre_ffn, w_gate_up, w_down, ln_post_ffn, w_pe_gate, b_pe_gate, w_pe_proj, loss_target, m_ln_pre_mix, m_w_in, m_sgu_ln_g, m_sgu_ln_b, m_w_spatial, m_b_spatial, m_attn_out_norm, m_sgu_out_norm, m_w_out, m_ln_post_mix, m_ln_pre_ffn, m_w_gate_up, m_w_down, m_ln_post_ffn, m_w_pe_gate, m_b_pe_gate, m_w_pe_proj, v_ln_pre_mix, v_w_in, v_sgu_ln_g, v_sgu_ln_b, v_w_spatial, v_b_spatial, v_attn_out_norm, v_sgu_out_norm, v_w_out, v_ln_post_mix, v_ln_pre_ffn, v_w_gate_up, v_w_down, v_ln_post_ffn, v_w_pe_gate, v_b_pe_gate, v_w_pe_proj):
    given = dict(locals())
    w = {n: given[n] for n in WEIGHTS}
    m = {n: given["m_" + n] for n in WEIGHTS}
    v = {n: given["v_" + n] for n in WEIGHTS}
    shapes = {n: w[n].shape for n in WEIGHTS}
    xs, ps, target = x[0], p[0, 0], loss_target[0]

    shard = {n: w[n][0].astype(BF) for n in ROW_SHARDED}
    shard.update({n: w[n][0].T.astype(BF) for n in COL_SHARDED})
    sm = {n: w[n][0] for n in SMALL}
    sm = {n: (a.reshape(1, -1) if a.ndim == 1 else a) for n, a in sm.items()}
    slopes = jnp.broadcast_to((2.0 ** -(jnp.arange(8, dtype=F32) + 1.0)).reshape(4, 2, 1), (4, 2, LANES))
    slopes = jnp.concatenate([slopes, jnp.zeros((4, 6, LANES), F32)], axis=1)
    b_st = jnp.pad(sm["b_spatial"].T, ((0, 0), (0, LANES - N_GROUPS)))

    def full(gathered):
        return gathered.reshape(-1, gathered.shape[-1])

    w_in_t = full(_all_gather(shard["w_in"], "gather_w_in"))
    proj, a = _in_proj(xs, sm["ln_pre_mix"], w_in_t)
    later = ("w_out", "w_gate_up", "w_down", "w_pe_gate", "w_pe_proj")
    attn, lse, gathered = _attn_fwd(proj, slopes, [shard[n] for n in later])
    w_out_f, w_gu_t, w_down_f, w_peg_f, w_pep_t = [full(g) for g in gathered]
    sgu = _sgu_fwd(proj, sm["sgu_ln_g"], sm["sgu_ln_b"], sm["w_spatial"], b_st)
    groups, mixed, h1, f = _out_proj(attn, sgu, xs, sm["attn_out_norm"], sm["sgu_out_norm"], w_out_f,
                                     sm["ln_post_mix"], sm["ln_pre_ffn"])
    g, u, act = _gate_up(f, w_gu_t)
    y, h2 = _down_proj(act, w_down_f, h1, sm["ln_post_ffn"])
    dh3, dpp, dpre, h2b, pb, loss_cols, db_peg = _pe_and_loss(h2, ps, target, w_peg_f, sm["b_pe_gate"], w_pep_t)
    loss = lax.psum(0.5 * jnp.sum(loss_cols) * (1.0 / D_MODEL), ("x", "y", "c"))

    arrived = {}
    g_pep_t = _weight_grad(dpp, pb, "w_pe_proj")
    g_peg = _weight_grad(h2b, dpre, "w_pe_gate")
    dh2, dy, d_pff = _pe_bwd(dpre, w_peg_f, dh3, y, sm["ln_post_ffn"])
    g_down = _weight_grad(act, dy, "w_down")
    dg, du, (arrived["w_pe_proj"], arrived["w_pe_gate"]) = _down_bwd(dy, w_down_f, g, u, [_slabs(g_pep_t), _slabs(g_peg)])
    g_gu_t = _weight_grad(dg, f, "w_gate", rows=2 * D_FF)
    g_gu_t = _weight_grad(du, f, "w_up", into=g_gu_t, row_tile=FF_TILES, rows=2 * D_FF)
    dh1, dmx, d_pf, d_pm, (arrived["w_down"],) = _ffn_in_bwd(dg, du, w_gu_t, h1, dh2, mixed, sm["ln_pre_ffn"],
                                                            sm["ln_post_mix"], [_slabs(g_down)])
    g_out = _weight_grad(groups, dmx, "w_out")
    dattn, dsgu, d_ga, d_gs = _out_bwd(dmx, w_out_f, attn, sgu, sm["attn_out_norm"], sm["sgu_out_norm"])
    duz, d_ws, d_bst, d_lg, d_lb = _sgu_bwd(proj, dsgu, sm["sgu_ln_g"], sm["sgu_ln_b"], sm["w_spatial"], b_st)
    dqkv, (arrived["w_gate_up"], arrived["w_out"]) = _attn_bwd(proj, dattn, attn, lse, slopes,
                                                               [_slabs(g_gu_t), _slabs(g_out)])
    g_in_t = _grad_w_in([dqkv, duz], a)
    gs = dict(ln_pre_mix=jnp.zeros_like(sm["ln_pre_mix"]), sgu_ln_g=d_lg, sgu_ln_b=d_lb, w_spatial=d_ws,
              b_spatial=d_bst[:, :N_GROUPS].T, attn_out_norm=d_ga, sgu_out_norm=d_gs, ln_post_mix=d_pm,
              ln_pre_ffn=d_pf, ln_post_ffn=d_pff, b_pe_gate=db_peg)
    grad_x, d_g1, (arrived["w_in"], arrived_small) = _in_bwd(
        [dqkv, duz], w_in_t, xs, dh1, sm["ln_pre_mix"], [_slabs(g_in_t), _pack_small(gs)], [True, False])
    (arrived_g1,) = _scatter_call([jnp.pad(d_g1, ((0, 7), (0, 0)))], [False], "ln_pre_mix_grad_exchange")
    arrived_small = lax.dynamic_update_slice(arrived_small, arrived_g1[:, :1], (0, SMALL_G1_ROW, 0))

    res = {}
    for n in ROW_SHARDED:
        res[n] = _reduce_adamw(arrived[n], w[n][0], m[n][0], v[n][0], n)
    for n in ("w_in", "w_gate_up"):
        res[n] = [t.T for t in _reduce_adamw(arrived[n], w[n][0].T, m[n][0].T, v[n][0].T, n)]
    for n in ("w_pe_proj",):
        grad = _reduce(arrived[n], n).T
        res[n] = (grad, *_adamw(grad, w[n][0], m[n][0], v[n][0], n))
    small = _reduce_adamw(arrived_small, _pack_small(w), _pack_small(m), _pack_small(v), "small")
    small = [_unpack_small(t, shapes) for t in small]

    out = []
    for k in range(4):
        out += [res[n][k][None] if n in res else small[k][n] for n in WEIGHTS]
    return (loss, grad_x[None], *out)
```

```python
import math

import jax
import jax.numpy as jnp
from jax import lax
from jax.experimental import pallas as pl
from jax.experimental.pallas import tpu as pltpu

F32 = jnp.float32
BF = jnp.bfloat16


def SDS(shape, dtype):
    return pltpu.HBM(tuple(shape), dtype)

D_MODEL = 1024
ATTN_W = 512
SGU_W = 512
HEAD_DIM = 64
N_GROUPS = 4
GROUP_DIM = 128
CHUNK = 128
D_FF = 2816
PLE_DIM = 256
PROJ = 3 * ATTN_W + 2 * SGU_W
DILATIONS = (1, 4, 16)
QBLK = 128
EPS = 1e-6
NEG = -1e30
N_DEV = 8
LANES = 128

ADAM_LR = 0.001
ADAM_B1 = 0.9
ADAM_B2 = 0.999
ADAM_EPS = 1e-08
ADAM_WD = 0.01
ADAM_STEP = 10

SMALL_ROWS = 80
MIB = 2 ** 20
MESH_ID = pl.DeviceIdType.MESH
HBM = pl.BlockSpec(memory_space=pl.ANY)


def _params(sem, vmem_mib):
    return pltpu.CompilerParams(dimension_semantics=sem, vmem_limit_bytes=vmem_mib * MIB)


def _dot(a, b):
    return jnp.dot(a, b, preferred_element_type=F32)


def _dot_nt(a, b):
    return lax.dot_general(a, b, (((1,), (1,)), ((), ())), preferred_element_type=F32)


def _dot_tn(a, b):
    return lax.dot_general(a, b, (((0,), (0,)), ((), ())), preferred_element_type=F32)


def _rstd(x):
    return lax.rsqrt(jnp.mean(x * x, axis=-1, keepdims=True) + EPS)


def _rms_bwd(dy, x, g):
    r = _rstd(x)
    n = x * r
    dn = dy * g
    dx = r * (dn - n * jnp.mean(dn * n, axis=-1, keepdims=True))
    return dx, dy * n


def _colsum(v):
    return jnp.sum(v, axis=0, keepdims=True)


_G0 = math.sqrt(2.0 / math.pi)
_G1 = 0.044715


def _gelu(x):
    t = jnp.tanh(_G0 * (x + _G1 * x * x * x))
    return 0.5 * x * (1.0 + t), t


def _gelu_grad(x, t):
    return 0.5 * (1.0 + t) + 0.5 * x * (1.0 - t * t) * (_G0 * (1.0 + 3.0 * _G1 * x * x))


def _sigmoid(x):
    return 0.5 * jnp.tanh(0.5 * x) + 0.5


def _row(i):
    return (i, 0)


def _fixed(i):
    return (0, 0)


def _acc_init(step, *refs):
    @pl.when(step == 0)
    def _():
        for r in refs:
            r[...] = jnp.zeros_like(r)


FLIPS = [(dx, dy, dc) for dx in (0, 1) for dy in (0, 1) for dc in (0, 1)][1:]
DMA_SEMS = pltpu.SemaphoreType.DMA


def _mesh_pos():
    return lax.axis_index("x"), lax.axis_index("y"), lax.axis_index("c")


def _remote(src, dst, sems, n, to):
    return pltpu.make_async_remote_copy(src_ref=src, dst_ref=dst, send_sem=sems[0].at[n], recv_sem=sems[1].at[n],
                                        device_id=to, device_id_type=MESH_ID)


class _Scatter:
    def __init__(self, items, sems):
        x, y, c = _mesh_pos()
        me = 4 * x + 2 * y + c
        self.local, self.sends, self.arrivals = [], [], []
        for i, (src, dst, slabbed) in enumerate(items):
            self.local.append(pltpu.make_async_copy(src.at[me] if slabbed else src, dst.at[me], sems[2].at[i]))
            for k, (dx, dy, dc) in enumerate(FLIPS):
                to = (1 - x if dx else x, 1 - y if dy else y, 1 - c if dc else c)
                peer = 4 * to[0] + 2 * to[1] + to[2]
                out = src.at[peer] if slabbed else src
                self.sends.append(_remote(out, dst.at[me], sems, 7 * i + k, to))
                self.arrivals.append(_remote(out, dst.at[peer], sems, 7 * i + k, to))

    def start(self):
        for cp in self.local + self.sends:
            cp.start()

    def wait(self):
        for cp in self.arrivals:
            cp.wait_recv()
        for cp in self.sends:
            cp.wait_send()
        for cp in self.local:
            cp.wait()


def _scatter_sems(n):
    return [DMA_SEMS((7 * n,)), DMA_SEMS((7 * n,)), DMA_SEMS((n,))]


class _Gather:
    def __init__(self, items, sems):
        x, y, c = _mesh_pos()
        me, sibling = (x, y, c), (x, y, 1 - c)
        chips = [(1 - x, y), (x, 1 - y), (1 - x, 1 - y)]
        self.first, self.passed, self.from_chips, self.rest, self.local = [], [], [], [], []
        for i, (src, dst) in enumerate(items):
            def slot(p, dst=dst):
                return dst.at[4 * p[0] + 2 * p[1] + p[2]]

            def copy(k, block, to, own=False, i=i, src=src, slot=slot):
                return _remote(src if own else slot(block), slot(block), sems, 7 * i + k, to)

            self.local.append(pltpu.make_async_copy(src, slot(me), sems[2].at[i]))
            self.first.append(copy(0, me, sibling, own=True))
            self.first += [copy(1 + j, me, (*chip, c), own=True) for j, chip in enumerate(chips)]
            self.passed += [copy(4 + j, (*chip, c), sibling) for j, chip in enumerate(chips)]
            self.from_chips += [copy(1 + j, (*chip, c), me) for j, chip in enumerate(chips)]
            self.rest.append(copy(0, sibling, me))
            self.rest += [copy(4 + j, (*chip, 1 - c), me) for j, chip in enumerate(chips)]

    def start(self):
        for cp in self.local + self.first:
            cp.start()

    def forward(self):
        for arrived, onward in zip(self.from_chips, self.passed):
            arrived.wait_recv()
            onward.start()

    def finish(self):
        for cp in self.rest:
            cp.wait_recv()
        for cp in self.first + self.passed:
            cp.wait_send()
        for cp in self.local:
            cp.wait()


def _all_gather(shard, name):
    def body(x_ref, out_ref, *sems):
        g = _Gather([(x_ref, out_ref)], sems)
        g.start()
        g.forward()
        g.finish()

    return pl.pallas_call(
        body, out_shape=SDS((N_DEV,) + shard.shape, shard.dtype), in_specs=[HBM], out_specs=HBM,
        scratch_shapes=_scatter_sems(1), name=name)(shard)


def _scatter_call(srcs, slabbed, name):
    n = len(srcs)

    def body(*refs):
        sc = _Scatter(list(zip(refs[:n], refs[n:2 * n], slabbed)), refs[2 * n:])
        sc.start()
        sc.wait()

    shapes = [SDS(s.shape if sl else (N_DEV,) + s.shape, s.dtype) for s, sl in zip(srcs, slabbed)]
    return pl.pallas_call(body, out_shape=shapes, in_specs=[HBM] * n, out_specs=[HBM] * n,
                          scratch_shapes=_scatter_sems(n), name=name)(*srcs)


def _in_proj(x, g1, w_in_t, tm=512):
    T = x.shape[0]

    def body(x_ref, g_ref, w_ref, proj_ref, a_ref):
        xv = x_ref[...]
        a = (xv * _rstd(xv) * g_ref[...]).astype(BF)
        a_ref[...] = a
        proj_ref[...] = _dot_nt(a, w_ref[...])

    return pl.pallas_call(
        body, grid=(T // tm,),
        in_specs=[pl.BlockSpec((tm, D_MODEL), _row), pl.BlockSpec((1, D_MODEL), _fixed),
                  pl.BlockSpec((PROJ, D_MODEL), _fixed)],
        out_specs=[pl.BlockSpec((tm, PROJ), _row), pl.BlockSpec((tm, D_MODEL), _row)],
        out_shape=[SDS((T, PROJ), F32), SDS((T, D_MODEL), BF)],
        name="in_proj", compiler_params=_params(("arbitrary",), 48))(x, g1, w_in_t)


ATTN_GROUP = 16


def _attn_bias(sl_ref, bias_ref):
    qi = lax.broadcasted_iota(jnp.int32, (QBLK, QBLK), 0)
    kj = lax.broadcasted_iota(jnp.int32, (QBLK, QBLK), 1)
    step = qi - kj
    for di, d in enumerate(DILATIONS):
        for j in range(2):
            sl = sl_ref[0, j:j + 1, :]
            cur = jnp.where(step >= 0, -sl * (step * d).astype(F32), NEG)
            prev = jnp.where(step <= 0, -sl * ((step + QBLK) * d).astype(F32), NEG)
            rows = slice(j * QBLK, (j + 1) * QBLK)
            bias_ref[di * 2, rows, :QBLK] = prev
            bias_ref[di * 2, rows, QBLK:] = cur
            bias_ref[di * 2 + 1, rows, :QBLK] = jnp.full((QBLK, QBLK), NEG, F32)
            bias_ref[di * 2 + 1, rows, QBLK:] = cur


def _stack_heads(x, lo):
    return jnp.concatenate([jnp.where(lo, x, 0.0), jnp.where(lo, 0.0, x)], axis=0).astype(BF)


def _unstack_heads(x, lo):
    return jnp.where(lo, x[:QBLK], x[QBLK:])


def _attn_rows(start, d, blocks=1):
    if d == 1:
        return pl.ds(pl.multiple_of(start, QBLK), blocks * QBLK)
    return pl.ds(start, blocks * QBLK, stride=d)


def _attn_group_index(i, nblk, d, group):
    per = nblk // d // group
    r = i // per
    n0 = (i % per) * group
    start = r + (d * QBLK) * n0
    pstart = jnp.maximum(start - d * QBLK, r)
    return start, pstart, n0 == 0


def _attn_plan(nblk, d):
    group = min(ATTN_GROUP, nblk // d)
    return group, max(1, min(ATTN_GROUP // group, d))


def _attn_fwd(proj, slopes, to_gather):
    T = proj.shape[0]
    nblk = T // QBLK
    big = dict(pipeline_mode=pl.Buffered(1))
    n_g = len(to_gather)

    def body(q_ref, k_ref, v_ref, sl_ref, *rest):
        srcs, (o_ref, m_ref), dsts = rest[:n_g], rest[n_g:n_g + 2], rest[n_g + 2:2 * n_g + 2]
        sems, (l_ref, bias_ref) = rest[2 * n_g + 2:2 * n_g + 5], rest[2 * n_g + 5:]
        h = pl.program_id(0)

        @pl.when(h == 0)
        def _():
            _Gather(list(zip(srcs, dsts)), sems).start()

        @pl.when(h == 2)
        def _():
            _Gather(list(zip(srcs, dsts)), sems).forward()

        _attn_bias(sl_ref, bias_ref)
        lo = lax.broadcasted_iota(jnp.int32, (1, LANES), 1) < HEAD_DIM

        for di, d in enumerate(DILATIONS):
            group, segs = _attn_plan(nblk, d)

            def step(i, carry, segs=segs, **kw):
                for s in range(segs):
                    segment(i * segs + s, **kw)
                return carry

            def segment(i, d=d, di=di, group=group):
                start, pstart, first = _attn_group_index(i, nblk, d, group)
                rows, prows = _attn_rows(start, d, group), _attn_rows(pstart, d)
                q = q_ref[rows, :] * (HEAD_DIM ** -0.5)
                k = jnp.concatenate([k_ref[prows, :], k_ref[rows, :]], axis=0).astype(BF)
                v = jnp.concatenate([v_ref[prows, :], v_ref[rows, :]], axis=0).astype(BF)
                for b in range(group):
                    qb = slice(b * QBLK, (b + 1) * QBLK)
                    kb = slice(b * QBLK, (b + 2) * QBLK)
                    bias = bias_ref[di * 2 + first.astype(jnp.int32)] if b == 0 else bias_ref[di * 2]
                    s = _dot_nt(_stack_heads(q[qb], lo), k[kb]) + bias
                    m = jnp.max(s, axis=-1, keepdims=True)
                    pr = jnp.exp(s - m)
                    m_b = _unstack_heads(m, lo)
                    l_b = _unstack_heads(jnp.sum(pr, axis=-1, keepdims=True), lo)
                    o_b = _unstack_heads(_dot(pr.astype(BF), v[kb]), lo)
                    out = _attn_rows(start + b * (d * QBLK), d)
                    if di == 0:
                        m_ref[out, :] = m_b
                        l_ref[out, :] = l_b
                        o_ref[out, :] = o_b
                    else:
                        m_o = m_ref[out, :]
                        m_n = jnp.maximum(m_o, m_b)
                        wa, wb = jnp.exp(m_o - m_n), jnp.exp(m_b - m_n)
                        m_ref[out, :] = m_n
                        l_ref[out, :] = wa * l_ref[out, :] + wb * l_b
                        o_ref[out, :] = wa * o_ref[out, :] + wb * o_b

            lax.fori_loop(0, nblk // (group * segs), step, 0)

        def finish(i, carry):
            rows = pl.ds(pl.multiple_of(i * QBLK, QBLK), QBLK)
            l = l_ref[rows, :]
            o_ref[rows, :] = o_ref[rows, :] / l
            m_ref[rows, :] = m_ref[rows, :] + jnp.log(l)
            return carry

        lax.fori_loop(0, nblk, finish, 0)

        @pl.when(h == pl.num_programs(0) - 1)
        def _():
            _Gather(list(zip(srcs, dsts)), sems).finish()

    col = lambda base: pl.BlockSpec((T, LANES), lambda h: (0, base + h), **big)
    tok = pl.BlockSpec((T, LANES), lambda h: (0, h))
    outs = pl.pallas_call(
        body, grid=(4,),
        in_specs=[col(0), col(4), col(8), pl.BlockSpec((1, 8, LANES), lambda h: (h, 0, 0))] + [HBM] * n_g,
        out_specs=[tok, tok] + [HBM] * n_g,
        out_shape=[SDS((T, ATTN_W), F32), SDS((T, ATTN_W), F32)]
        + [SDS((N_DEV,) + g.shape, g.dtype) for g in to_gather],
        scratch_shapes=_scatter_sems(n_g) + [pltpu.VMEM((T, LANES), F32), pltpu.VMEM((6, 2 * QBLK, 2 * QBLK), F32)],
        name="attn_fwd", compiler_params=_params(("arbitrary",), 48))(proj, proj, proj, slopes, *to_gather)
    return outs[0], outs[1], outs[2:]


def _sgu_norm(zv, ln_g, ln_b):
    gz, tz = _gelu(zv)
    mu = jnp.mean(gz, axis=-1, keepdims=True)
    xc = gz - mu
    rs = lax.rsqrt(jnp.mean(xc * xc, axis=-1, keepdims=True) + EPS)
    xhat = xc * rs
    return xhat * ln_g + ln_b, xhat, rs, tz


def _causal(w):
    i = lax.broadcasted_iota(jnp.int32, (CHUNK, CHUNK), 0)
    j = lax.broadcasted_iota(jnp.int32, (CHUNK, CHUNK), 1)
    return jnp.where(i >= j, w, 0.0)


def _sgu_fwd(proj, ln_g, ln_b, w_s, b_st, tm=512):
    T = proj.shape[0]

    def body(u_ref, z_ref, g_ref, b_ref, w_ref, bs_ref, out_ref):
        for g in range(N_GROUPS):
            wm = _causal(w_ref[g]).astype(BF)
            cols = slice(g * GROUP_DIM, (g + 1) * GROUP_DIM)
            for c in range(tm // CHUNK):
                rows = slice(c * CHUNK, (c + 1) * CHUNK)
                zn, _, _, _ = _sgu_norm(z_ref[rows, cols], g_ref[...], b_ref[...])
                mixed = _dot(wm, zn.astype(BF)) + bs_ref[:, g:g + 1]
                gu, _ = _gelu(u_ref[rows, cols])
                out_ref[rows, cols] = gu * mixed

    return pl.pallas_call(
        body, grid=(T // tm,),
        in_specs=[pl.BlockSpec((tm, SGU_W), lambda i: (i, 3)), pl.BlockSpec((tm, SGU_W), lambda i: (i, 4)),
                  pl.BlockSpec((1, GROUP_DIM), _fixed), pl.BlockSpec((1, GROUP_DIM), _fixed),
                  pl.BlockSpec((N_GROUPS, CHUNK, CHUNK), lambda i: (0, 0, 0)), pl.BlockSpec((CHUNK, LANES), _fixed)],
        out_specs=pl.BlockSpec((tm, SGU_W), _row),
        out_shape=SDS((T, SGU_W), F32),
        name="sgu_fwd", compiler_params=_params(("arbitrary",), 32))(proj, proj, ln_g, ln_b, w_s, b_st)


def _out_proj(attn, sgu, x, g_a, g_s, w_out, g_pm, g_pf, tm=512):
    T = x.shape[0]

    def body(a_ref, s_ref, x_ref, ga_ref, gs_ref, w_ref, gpm_ref, gpf_ref, grp_ref, mixed_ref, h1_ref, f_ref):
        av, sv = a_ref[...], s_ref[...]
        an = (av * _rstd(av) * ga_ref[...]).astype(BF)
        sn = (sv * _rstd(sv) * gs_ref[...]).astype(BF)
        grp_ref[:, :ATTN_W] = an
        grp_ref[:, ATTN_W:] = sn
        mixed = _dot(an, w_ref[:ATTN_W, :]) + _dot(sn, w_ref[ATTN_W:, :])
        mixed_ref[...] = mixed
        h1 = x_ref[...] + mixed * _rstd(mixed) * gpm_ref[...]
        h1_ref[...] = h1
        f_ref[...] = (h1 * _rstd(h1) * gpf_ref[...]).astype(BF)

    tok = lambda w: pl.BlockSpec((tm, w), _row)
    vec = lambda w: pl.BlockSpec((1, w), _fixed)
    return pl.pallas_call(
        body, grid=(T // tm,),
        in_specs=[tok(ATTN_W), tok(SGU_W), tok(D_MODEL), vec(ATTN_W), vec(SGU_W),
                  pl.BlockSpec((D_MODEL, D_MODEL), _fixed), vec(D_MODEL), vec(D_MODEL)],
        out_specs=[tok(D_MODEL)] * 4,
        out_shape=[SDS((T, D_MODEL), BF), SDS((T, D_MODEL), F32), SDS((T, D_MODEL), F32), SDS((T, D_MODEL), BF)],
        name="out_proj", compiler_params=_params(("arbitrary",), 48))(attn, sgu, x, g_a, g_s, w_out, g_pm, g_pf)


FF_TILE = 1408
FF_TILES = D_FF // FF_TILE
FF_CHUNK = 256


def _gate_up(f, w_gu_t, tm=512):
    T = f.shape[0]
    tn = FF_TILE

    def body(f_ref, wg_ref, wu_ref, g_ref, u_ref, act_ref):
        fv = f_ref[...]
        g = _dot_nt(fv, wg_ref[...])
        u = _dot_nt(fv, wu_ref[...])
        g_ref[...] = g.astype(BF)
        u_ref[...] = u.astype(BF)
        act_ref[...] = (g * _sigmoid(g) * u).astype(BF)

    ospec = pl.BlockSpec((tm, tn), lambda j, i: (i, j))
    return pl.pallas_call(
        body, grid=(FF_TILES, T // tm),
        in_specs=[pl.BlockSpec((tm, D_MODEL), lambda j, i: (i, 0)), pl.BlockSpec((tn, D_MODEL), lambda j, i: (j, 0)),
                  pl.BlockSpec((tn, D_MODEL), lambda j, i: (j + FF_TILES, 0))],
        out_specs=[ospec] * 3, out_shape=[SDS((T, D_FF), BF)] * 3,
        name="gate_up", compiler_params=_params(("arbitrary", "arbitrary"), 40))(f, w_gu_t, w_gu_t)


def _down_proj(act, w_down, h1, g_pff, tm=512):
    T = act.shape[0]

    def body(a_ref, w_ref, h1_ref, g_ref, y_ref, h2_ref):
        y = _dot(a_ref[...], w_ref[...])
        y_ref[...] = y
        h2_ref[...] = h1_ref[...] + y * _rstd(y) * g_ref[...]

    return pl.pallas_call(
        body, grid=(T // tm,),
        in_specs=[pl.BlockSpec((tm, D_FF), _row), pl.BlockSpec((D_FF, D_MODEL), _fixed),
                  pl.BlockSpec((tm, D_MODEL), _row), pl.BlockSpec((1, D_MODEL), _fixed)],
        out_specs=[pl.BlockSpec((tm, D_MODEL), _row)] * 2,
        out_shape=[SDS((T, D_MODEL), F32)] * 2,
        name="down_proj", compiler_params=_params(("arbitrary",), 48))(act, w_down, h1, g_pff)


def _pe_and_loss(h2, p, target, w_peg, b_peg, w_pep_t, tm=512):
    T = h2.shape[0]

    def body(h2_ref, p_ref, t_ref, wg_ref, b_ref, wp_ref,
             dh3_ref, dpp_ref, dpre_ref, h2b_ref, pb_ref, loss_ref, db_ref):
        _acc_init(pl.program_id(0), loss_ref, db_ref)
        h2v = h2_ref[...]
        h2b = h2v.astype(BF)
        pb = p_ref[...].astype(BF)
        h2b_ref[...] = h2b
        pb_ref[...] = pb
        gate = _sigmoid(_dot(h2b, wg_ref[...]) + b_ref[...])
        pp = _dot_nt(pb, wp_ref[...])
        diff = h2v + gate * pp - t_ref[...]
        loss_ref[...] += _colsum(diff * diff)
        dh3 = diff * (1.0 / D_MODEL)
        dh3_ref[...] = dh3
        dpp_ref[...] = (dh3 * gate).astype(BF)
        dpre = dh3 * pp * (gate * (1.0 - gate))
        dpre_ref[...] = dpre.astype(BF)
        db_ref[...] += _colsum(dpre)

    tok = lambda w: pl.BlockSpec((tm, w), _row)
    vec = pl.BlockSpec((1, D_MODEL), _fixed)
    return pl.pallas_call(
        body, grid=(T // tm,),
        in_specs=[tok(D_MODEL), tok(PLE_DIM), tok(D_MODEL), pl.BlockSpec((D_MODEL, D_MODEL), _fixed), vec,
                  pl.BlockSpec((D_MODEL, PLE_DIM), _fixed)],
        out_specs=[tok(D_MODEL), tok(D_MODEL), tok(D_MODEL), tok(D_MODEL), tok(PLE_DIM), vec, vec],
        out_shape=[SDS((T, D_MODEL), F32), SDS((T, D_MODEL), BF), SDS((T, D_MODEL), BF), SDS((T, D_MODEL), BF),
                   SDS((T, PLE_DIM), BF), SDS((1, D_MODEL), F32), SDS((1, D_MODEL), F32)],
        name="pe_and_loss", compiler_params=_params(("arbitrary",), 48))(h2, p, target, w_peg, b_peg, w_pep_t)


def _weight_grad(a, dy, name, into=None, row_tile=0, rows=None, out_dtype=F32, tk=512):
    n = dy.shape[1]
    tn = min(n, 1024)
    T, ka = a.shape
    tka = FF_TILE if ka == D_FF else min(ka, 1024)
    rows = ka if rows is None else rows
    direct = out_dtype == F32

    def body(a_ref, dy_ref, *rest):
        out_ref, acc_ref = (rest[-1], rest[-1]) if direct else rest[-2:]
        _acc_init(pl.program_id(2), acc_ref)
        acc_ref[...] += _dot_tn(a_ref[...].astype(BF), dy_ref[...].astype(BF))
        if not direct:
            @pl.when(pl.program_id(2) == pl.num_programs(2) - 1)
            def _():
                out_ref[...] = acc_ref[...].astype(out_dtype)

    carried = [] if into is None else [into]
    return pl.pallas_call(
        body, grid=(ka // tka, n // tn, T // tk),
        in_specs=[pl.BlockSpec((tk, tka), lambda i, j, k: (k, i)), pl.BlockSpec((tk, tn), lambda i, j, k: (k, j))]
        + [HBM] * len(carried),
        out_specs=pl.BlockSpec((tka, tn), lambda i, j, k: (i + row_tile, j)),
        out_shape=SDS((rows, n), out_dtype), input_output_aliases={2: 0} if carried else {},
        scratch_shapes=[] if direct else [pltpu.VMEM((tka, tn), F32)],
        name="grad_" + name, compiler_params=_params(("arbitrary",) * 3, 40))(a, dy, *carried)


def _grad_w_in(dparts, a, tk=512):
    T = a.shape[0]

    def body(*refs):
        d_refs, a_ref, out_ref, acc_ref = refs[:len(dparts)], refs[-3], refs[-2], refs[-1]
        k = pl.program_id(0)
        _acc_init(k, acc_ref)
        cols = [r[part].astype(BF) for r in d_refs for part in range(r.shape[0])]
        acc_ref[...] += _dot_tn(jnp.concatenate(cols, axis=1), a_ref[...])

        @pl.when(k == pl.num_programs(0) - 1)
        def _():
            out_ref[...] = acc_ref[...].astype(BF)

    return pl.pallas_call(
        body, grid=(T // tk,),
        in_specs=[pl.BlockSpec((d.shape[0], tk, d.shape[2]), lambda k: (0, k, 0)) for d in dparts]
        + [pl.BlockSpec((tk, D_MODEL), lambda k: (k, 0))],
        out_specs=pl.BlockSpec((PROJ, D_MODEL), lambda k: (0, 0)),
        out_shape=SDS((PROJ, D_MODEL), BF), scratch_shapes=[pltpu.VMEM((PROJ, D_MODEL), F32)],
        name="grad_w_in", compiler_params=_params(("arbitrary",), 48))(*dparts, a)


def _pe_bwd(dpre, w_peg, dh3, y, g_pff, tm=512):
    T = y.shape[0]

    def body(dp_ref, w_ref, dh3_ref, y_ref, g_ref, dh2_ref, dy_ref, dg_ref):
        _acc_init(pl.program_id(0), dg_ref)
        dh2 = dh3_ref[...] + _dot_nt(dp_ref[...], w_ref[...])
        dh2_ref[...] = dh2
        dy, dg = _rms_bwd(dh2, y_ref[...], g_ref[...])
        dy_ref[...] = dy.astype(BF)
        dg_ref[...] += _colsum(dg)

    tok = pl.BlockSpec((tm, D_MODEL), _row)
    vec = pl.BlockSpec((1, D_MODEL), _fixed)
    return pl.pallas_call(
        body, grid=(T // tm,),
        in_specs=[tok, pl.BlockSpec((D_MODEL, D_MODEL), _fixed), tok, tok, vec],
        out_specs=[tok, tok, vec],
        out_shape=[SDS((T, D_MODEL), F32), SDS((T, D_MODEL), BF), SDS((1, D_MODEL), F32)],
        name="pe_bwd", compiler_params=_params(("arbitrary",), 40))(dpre, w_peg, dh3, y, g_pff)


def _down_bwd(dy, w_down, g, u, to_send, tm=512):
    T = dy.shape[0]
    n_s = len(to_send)

    def body(dy_ref, w_ref, g_ref, u_ref, *rest):
        srcs, (dg_ref, du_ref), dsts, sems = rest[:n_s], rest[n_s:n_s + 2], rest[n_s + 2:2 * n_s + 2], rest[2 * n_s + 2:]
        i = pl.program_id(0)
        items = list(zip(srcs, dsts, [True] * n_s))

        @pl.when(i == 0)
        def _():
            _Scatter(items, sems).start()

        dyv = dy_ref[...]
        for c in range(D_FF // FF_CHUNK):
            cols = slice(c * FF_CHUNK, (c + 1) * FF_CHUNK)
            dact = _dot_nt(dyv, w_ref[cols, :])
            gv = g_ref[:, cols].astype(F32)
            uv = u_ref[:, cols].astype(F32)
            s = _sigmoid(gv)
            ds = dact * s
            dg_ref[:, cols] = (ds * uv * (1.0 + gv * (1.0 - s))).astype(BF)
            du_ref[:, cols] = (ds * gv).astype(BF)

        @pl.when(i == pl.num_programs(0) - 1)
        def _():
            _Scatter(items, sems).wait()

    tile = pl.BlockSpec((tm, D_FF), _row)
    outs = pl.pallas_call(
        body, grid=(T // tm,),
        in_specs=[pl.BlockSpec((tm, D_MODEL), _row),
                  pl.BlockSpec((D_FF, D_MODEL), _fixed, pipeline_mode=pl.Buffered(1)), tile, tile] + [HBM] * n_s,
        out_specs=[tile, tile] + [HBM] * n_s,
        out_shape=[SDS((T, D_FF), BF)] * 2 + [SDS(s.shape, s.dtype) for s in to_send],
        scratch_shapes=_scatter_sems(n_s),
        name="down_bwd", compiler_params=_params(("arbitrary",), 48))(dy, w_down, g, u, *to_send)
    return outs[0], outs[1], outs[2:]


def _ffn_in_bwd(dg, du, w_gu_t, h1, dh2, mixed, g_pf, g_pm, to_send, tm=512):
    T = h1.shape[0]
    n_s = len(to_send)

    def body(dg_ref, du_ref, wg_ref, wu_ref, h1_ref, dh2_ref, mx_ref, gpf_ref, gpm_ref, *rest):
        srcs, outs, dsts, sems = rest[:n_s], rest[n_s:n_s + 4], rest[n_s + 4:2 * n_s + 4], rest[2 * n_s + 4:]
        dh1_ref, dmx_ref, dgpf_ref, dgpm_ref = outs
        i = pl.program_id(0)
        items = list(zip(srcs, dsts, [True] * n_s))
        _acc_init(i, dgpf_ref, dgpm_ref)

        @pl.when(i == 0)
        def _():
            _Scatter(items, sems).start()

        df = _dot(dg_ref[...], wg_ref[...]) + _dot(du_ref[...], wu_ref[...])
        dx, dgf = _rms_bwd(df, h1_ref[...], gpf_ref[...])
        dh1 = dh2_ref[...] + dx
        dh1_ref[...] = dh1
        dmx, dgm = _rms_bwd(dh1, mx_ref[...], gpm_ref[...])
        dmx_ref[...] = dmx.astype(BF)
        dgpf_ref[...] += _colsum(dgf)
        dgpm_ref[...] += _colsum(dgm)

        @pl.when(i == pl.num_programs(0) - 1)
        def _():
            _Scatter(items, sems).wait()

    tok = lambda w: pl.BlockSpec((tm, w), _row)
    vec = pl.BlockSpec((1, D_MODEL), _fixed)
    outs = pl.pallas_call(
        body, grid=(T // tm,),
        in_specs=[tok(D_FF), tok(D_FF), pl.BlockSpec((D_FF, D_MODEL), lambda i: (0, 0), pipeline_mode=pl.Buffered(1)),
                  pl.BlockSpec((D_FF, D_MODEL), lambda i: (1, 0), pipeline_mode=pl.Buffered(1)),
                  tok(D_MODEL), tok(D_MODEL), tok(D_MODEL), vec, vec]
        + [HBM] * n_s,
        out_specs=[tok(D_MODEL), tok(D_MODEL), vec, vec] + [HBM] * n_s,
        out_shape=[SDS((T, D_MODEL), F32), SDS((T, D_MODEL), BF), SDS((1, D_MODEL), F32), SDS((1, D_MODEL), F32)]
        + [SDS(s.shape, s.dtype) for s in to_send],
        scratch_shapes=_scatter_sems(n_s),
        name="ffn_in_bwd", compiler_params=_params(("arbitrary",), 56))(
            dg, du, w_gu_t, w_gu_t, h1, dh2, mixed, g_pf, g_pm, *to_send)
    return outs[0], outs[1], outs[2], outs[3], outs[4:]


def _out_bwd(dmx, w_out, attn, sgu, g_a, g_s, tm=512):
    T = attn.shape[0]

    def body(dm_ref, w_ref, a_ref, s_ref, ga_ref, gs_ref, da_ref, ds_ref, dga_ref, dgs_ref):
        _acc_init(pl.program_id(0), dga_ref, dgs_ref)
        dgr = _dot_nt(dm_ref[...], w_ref[...])
        da, dga = _rms_bwd(dgr[:, :ATTN_W], a_ref[...], ga_ref[...])
        ds, dgs = _rms_bwd(dgr[:, ATTN_W:], s_ref[...], gs_ref[...])
        da_ref[...] = da
        ds_ref[...] = ds
        dga_ref[...] += _colsum(dga)
        dgs_ref[...] += _colsum(dgs)

    tok = lambda w: pl.BlockSpec((tm, w), _row)
    vec = lambda w: pl.BlockSpec((1, w), _fixed)
    return pl.pallas_call(
        body, grid=(T // tm,),
        in_specs=[tok(D_MODEL), pl.BlockSpec((D_MODEL, D_MODEL), _fixed), tok(ATTN_W), tok(SGU_W), vec(ATTN_W), vec(SGU_W)],
        out_specs=[tok(ATTN_W), tok(SGU_W), vec(ATTN_W), vec(SGU_W)],
        out_shape=[SDS((T, ATTN_W), F32), SDS((T, SGU_W), F32), SDS((1, ATTN_W), F32), SDS((1, SGU_W), F32)],
        name="out_bwd", compiler_params=_params(("arbitrary",), 40))(dmx, w_out, attn, sgu, g_a, g_s)


def _sgu_bwd(proj, dsgu, ln_g, ln_b, w_s, b_st, tm=512):
    T = proj.shape[0]

    def body(u_ref, z_ref, ds_ref, g_ref, b_ref, w_ref, bs_ref,
             duz_ref, dw_ref, dbs_ref, dlg_ref, dlb_ref, dbacc_ref):
        du_ref, dz_ref = duz_ref.at[0], duz_ref.at[1]
        step = pl.program_id(0)
        _acc_init(step, dw_ref, dbs_ref, dlg_ref, dlb_ref, dbacc_ref)
        lng, lnb = g_ref[...], b_ref[...]
        for g in range(N_GROUPS):
            wm = _causal(w_ref[g]).astype(BF)
            cols = slice(g * GROUP_DIM, (g + 1) * GROUP_DIM)
            for c in range(tm // CHUNK):
                rows = slice(c * CHUNK, (c + 1) * CHUNK)
                zv, uv, dout = z_ref[rows, cols], u_ref[rows, cols], ds_ref[rows, cols]
                zn, xhat, rs, tz = _sgu_norm(zv, lng, lnb)
                znb = zn.astype(BF)
                mixed = _dot(wm, znb) + bs_ref[:, g:g + 1]
                gu, tu = _gelu(uv)
                du_ref[rows, cols] = (dout * mixed * _gelu_grad(uv, tu)).astype(BF)
                dmix = dout * gu
                dmb = dmix.astype(BF)
                dw_ref[g] += _causal(_dot_nt(dmb, znb))
                dbacc_ref[g] += dmix
                dzn = _dot_tn(wm, dmb)
                dlg_ref[...] += _colsum(dzn * xhat)
                dlb_ref[...] += _colsum(dzn)
                dxh = dzn * lng
                dgz = rs * (dxh - jnp.mean(dxh, axis=-1, keepdims=True)
                            - xhat * jnp.mean(dxh * xhat, axis=-1, keepdims=True))
                dz_ref[rows, cols] = (dgz * _gelu_grad(zv, tz)).astype(BF)

        @pl.when(step == pl.num_programs(0) - 1)
        def _():
            lane = lax.broadcasted_iota(jnp.int32, (CHUNK, LANES), 1)
            acc = jnp.zeros((CHUNK, LANES), F32)
            for g in range(N_GROUPS):
                acc = jnp.where(lane == g, jnp.sum(dbacc_ref[g], axis=-1, keepdims=True), acc)
            dbs_ref[...] = acc

    tok = pl.BlockSpec((tm, SGU_W), _row)
    vec = pl.BlockSpec((1, GROUP_DIM), _fixed)
    wsp = pl.BlockSpec((N_GROUPS, CHUNK, CHUNK), lambda i: (0, 0, 0))
    sq = pl.BlockSpec((CHUNK, LANES), _fixed)
    return pl.pallas_call(
        body, grid=(T // tm,),
        in_specs=[pl.BlockSpec((tm, SGU_W), lambda i: (i, 3)), pl.BlockSpec((tm, SGU_W), lambda i: (i, 4)), tok,
                  vec, vec, wsp, sq],
        out_specs=[pl.BlockSpec((2, tm, SGU_W), lambda i: (0, i, 0)), wsp, sq, vec, vec],
        out_shape=[SDS((2, T, SGU_W), BF), SDS((N_GROUPS, CHUNK, CHUNK), F32),
                   SDS((CHUNK, LANES), F32), SDS((1, GROUP_DIM), F32), SDS((1, GROUP_DIM), F32)],
        scratch_shapes=[pltpu.VMEM((N_GROUPS, CHUNK, LANES), F32)],
        name="sgu_bwd", compiler_params=_params(("arbitrary",), 32))(proj, proj, dsgu, ln_g, ln_b, w_s, b_st)


def _attn_bwd(proj, do, o, lse, slopes, to_send, slabbed):
    T = proj.shape[0]
    nblk = T // QBLK
    big = dict(pipeline_mode=pl.Buffered(1))
    n_s = len(to_send)

    def body(q_ref, k_ref, v_ref, do_ref, o_ref, l_ref, sl_ref, *rest):
        srcs, d_ref, dsts = rest[:n_s], rest[n_s], rest[n_s + 1:2 * n_s + 1]
        sems, bias_ref = rest[2 * n_s + 1:2 * n_s + 4], rest[2 * n_s + 4]
        dq_ref, dk_ref, dv_ref = d_ref.at[0], d_ref.at[1], d_ref.at[2]
        h = pl.program_id(0)
        items = list(zip(srcs, dsts, slabbed))

        @pl.when(h == 0)
        def _():
            _Scatter(items, sems).start()

        _attn_bias(sl_ref, bias_ref)
        lo = lax.broadcasted_iota(jnp.int32, (1, LANES), 1) < HEAD_DIM
        scale = HEAD_DIM ** -0.5
        d_ref[...] = jnp.zeros_like(d_ref)

        for di, d in enumerate(DILATIONS):
            group, segs = _attn_plan(nblk, d)

            def step(i, carry, segs=segs, **kw):
                for s in range(segs):
                    segment(i * segs + s, **kw)
                return carry

            def segment(i, d=d, di=di, group=group):
                start, pstart, first = _attn_group_index(i, nblk, d, group)
                rows, prows = _attn_rows(start, d, group), _attn_rows(pstart, d)
                q = q_ref[rows, :] * scale
                k = jnp.concatenate([k_ref[prows, :], k_ref[rows, :]], axis=0).astype(BF)
                v = jnp.concatenate([v_ref[prows, :], v_ref[rows, :]], axis=0).astype(BF)
                dov = do_ref[rows, :]
                prod = dov * o_ref[rows, :]
                lse_g = l_ref[rows, :]
                masks = [lo, ~lo]
                qm = [jnp.where(masks[j], q, 0.0).astype(BF) for j in range(2)]
                dom = [jnp.where(masks[j], dov, 0.0).astype(BF) for j in range(2)]
                for b in range(group):
                    qb = slice(b * QBLK, (b + 1) * QBLK)
                    kb = slice(b * QBLK, (b + 2) * QBLK)
                    which = di * 2 + first.astype(jnp.int32) if b == 0 else di * 2
                    dq_parts, prs, dss = [], [], []
                    for j in range(2):
                        bias = bias_ref[which, j * QBLK:(j + 1) * QBLK, :]
                        delta = jnp.sum(jnp.where(masks[j], prod[qb], 0.0), axis=-1, keepdims=True)
                        lj = lse_g[qb, j * HEAD_DIM:j * HEAD_DIM + 1]
                        pr = jnp.exp(_dot_nt(qm[j][qb], k[kb]) + bias - lj)
                        ds = (pr * (_dot_nt(dom[j][qb], v[kb]) - delta)).astype(BF)
                        dq_parts.append(_dot(ds, k[kb]))
                        prs.append(pr.astype(BF))
                        dss.append(ds)
                    dk_b = _dot_tn(jnp.concatenate(dss, axis=0), jnp.concatenate([qm[0][qb], qm[1][qb]], axis=0))
                    dv_b = _dot_tn(jnp.concatenate(prs, axis=0), jnp.concatenate([dom[0][qb], dom[1][qb]], axis=0))
                    own = _attn_rows(start + b * (d * QBLK), d)
                    dq_ref[own, :] += jnp.where(lo, dq_parts[0], dq_parts[1]) * scale
                    if b == 0:
                        dk_ref[prows, :] += dk_b[:QBLK]
                        dv_ref[prows, :] += dv_b[:QBLK]
                        dk_ref[own, :] += dk_b[QBLK:]
                        dv_ref[own, :] += dv_b[QBLK:]
                    else:
                        two = _attn_rows(start + (b - 1) * (d * QBLK), d, 2)
                        dk_ref[two, :] += dk_b
                        dv_ref[two, :] += dv_b

            lax.fori_loop(0, nblk // (group * segs), step, 0)

        @pl.when(h == pl.num_programs(0) - 1)
        def _():
            _Scatter(items, sems).wait()

    col = lambda base: pl.BlockSpec((T, LANES), lambda h: (0, base + h), **big)
    outs = pl.pallas_call(
        body, grid=(4,),
        in_specs=[col(0), col(4), col(8), col(0), col(0), col(0), pl.BlockSpec((1, 8, LANES), lambda h: (h, 0, 0))]
        + [HBM] * n_s,
        out_specs=[pl.BlockSpec((3, T, LANES), lambda h: (0, 0, h))] + [HBM] * n_s,
        out_shape=[SDS((3, T, ATTN_W), F32)]
        + [SDS(s.shape if sl else (N_DEV,) + s.shape, s.dtype) for s, sl in zip(to_send, slabbed)],
        scratch_shapes=_scatter_sems(n_s) + [pltpu.VMEM((6, 2 * QBLK, 2 * QBLK), F32)],
        name="attn_bwd", compiler_params=_params(("arbitrary",), 60))(proj, proj, proj, do, o, lse, slopes, *to_send)
    return outs[0], outs[1:]


def _in_bwd(dparts, w_in_t, x, dh1, g1, to_send, slabbed, tm=512):
    T = x.shape[0]
    n = len(dparts)
    n_s = len(to_send)
    w = ATTN_W

    def body(*refs):
        d_refs, (w_ref, x_ref, dh1_ref, g_ref), rest = refs[:n], refs[n:n + 4], refs[n + 4:]
        srcs, (dx_ref, dg_ref), dsts, sems = rest[:n_s], rest[n_s:n_s + 2], rest[n_s + 2:2 * n_s + 2], rest[2 * n_s + 2:]
        step = pl.program_id(0)
        items = list(zip(srcs, dsts, slabbed))
        _acc_init(step, dg_ref)

        @pl.when(step == 0)
        def _():
            _Scatter(items, sems).start()

        da = None
        col = 0
        for r in d_refs:
            for part in range(r.shape[0]):
                t = _dot(r[part].astype(BF), w_ref[col * w:(col + 1) * w, :])
                da = t if da is None else da + t
                col += 1
        dx, dg = _rms_bwd(da, x_ref[...], g_ref[...])
        dx_ref[...] = dh1_ref[...] + dx
        dg_ref[...] += _colsum(dg)

        @pl.when(step == pl.num_programs(0) - 1)
        def _():
            _Scatter(items, sems).wait()

    tok = lambda c: pl.BlockSpec((tm, c), _row)
    vec = pl.BlockSpec((1, D_MODEL), _fixed)
    outs = pl.pallas_call(
        body, grid=(T // tm,),
        in_specs=[pl.BlockSpec((d.shape[0], tm, w), lambda i: (0, i, 0)) for d in dparts]
        + [pl.BlockSpec((PROJ, D_MODEL), _fixed), tok(D_MODEL), tok(D_MODEL), vec] + [HBM] * n_s,
        out_specs=[tok(D_MODEL), vec] + [HBM] * n_s,
        out_shape=[SDS((T, D_MODEL), F32), SDS((1, D_MODEL), F32)]
        + [SDS(s.shape if sl else (N_DEV,) + s.shape, s.dtype) for s, sl in zip(to_send, slabbed)],
        scratch_shapes=_scatter_sems(n_s),
        name="in_bwd", compiler_params=_params(("arbitrary",), 52))(*dparts, w_in_t, x, dh1, g1, *to_send)
    return outs[0], outs[1], outs[2:]


def _sum_parts(p_ref):
    g = p_ref[0].astype(F32)
    for s in range(1, N_DEV):
        g = g + p_ref[s].astype(F32)
    return g


def _adamw_math(g, w, m, v):
    nm = ADAM_B1 * m + (1.0 - ADAM_B1) * g
    nv = ADAM_B2 * v + (1.0 - ADAM_B2) * (g * g)
    m_hat = nm / (1.0 - ADAM_B1 ** ADAM_STEP)
    v_hat = nv / (1.0 - ADAM_B2 ** ADAM_STEP)
    return -ADAM_LR * (m_hat / (jnp.sqrt(v_hat) + ADAM_EPS) + ADAM_WD * w), nm, nv


def _row_tile(rows):
    for t in (256, 176, 128, 80):
        if rows % t == 0:
            return t
    raise ValueError(rows)


def _reduce_adamw(parts, w, m, v, name):
    rows, width = w.shape
    tr = _row_tile(rows)

    def body(p_ref, w_ref, m_ref, v_ref, g_ref, d_ref, nm_ref, nv_ref):
        g = _sum_parts(p_ref)
        g_ref[...] = g
        d_ref[...], nm_ref[...], nv_ref[...] = _adamw_math(g, w_ref[...], m_ref[...], v_ref[...])

    blk = pl.BlockSpec((tr, width), _row)
    return pl.pallas_call(
        body, grid=(rows // tr,),
        in_specs=[pl.BlockSpec((N_DEV, tr, width), lambda i: (0, i, 0)), blk, blk, blk],
        out_specs=[blk] * 4, out_shape=[SDS((rows, width), F32)] * 4,
        name="adamw_" + name, compiler_params=_params(("arbitrary",), 32))(parts, w, m, v)


def _reduce(parts, name):
    _, rows, width = parts.shape
    tr = _row_tile(rows)

    def body(p_ref, g_ref):
        g_ref[...] = _sum_parts(p_ref)

    return pl.pallas_call(
        body, grid=(rows // tr,),
        in_specs=[pl.BlockSpec((N_DEV, tr, width), lambda i: (0, i, 0))],
        out_specs=pl.BlockSpec((tr, width), _row), out_shape=SDS((rows, width), F32),
        name="sum_" + name, compiler_params=_params(("arbitrary",), 32))(parts)


def _adamw(g, w, m, v, name):
    rows, width = w.shape
    tr = _row_tile(rows)

    def body(g_ref, w_ref, m_ref, v_ref, d_ref, nm_ref, nv_ref):
        d_ref[...], nm_ref[...], nv_ref[...] = _adamw_math(g_ref[...], w_ref[...], m_ref[...], v_ref[...])

    blk = pl.BlockSpec((tr, width), _row)
    return pl.pallas_call(
        body, grid=(rows // tr,), in_specs=[blk] * 4, out_specs=[blk] * 3, out_shape=[SDS((rows, width), F32)] * 3,
        name="adamw_" + name, compiler_params=_params(("arbitrary",), 32))(g, w, m, v)


SMALL = ("w_spatial", "ln_pre_mix", "ln_post_mix", "ln_pre_ffn", "ln_post_ffn", "b_pe_gate",
         "attn_out_norm", "sgu_out_norm", "b_spatial", "sgu_ln_g", "sgu_ln_b")
SMALL_G1_ROW = 64 + SMALL.index("ln_pre_mix") - 1
ROW_SHARDED = ("w_out", "w_down", "w_pe_gate")
COL_SHARDED = ("w_in", "w_gate_up", "w_pe_proj")
WEIGHTS = ("ln_pre_mix", "w_in", "sgu_ln_g", "sgu_ln_b", "w_spatial", "b_spatial", "attn_out_norm", "sgu_out_norm",
           "w_out", "ln_post_mix", "ln_pre_ffn", "w_gate_up", "w_down", "ln_post_ffn", "w_pe_gate", "b_pe_gate",
           "w_pe_proj")


def _pack_small(t):
    rows = [t["w_spatial"].reshape(-1, D_MODEL)]
    for name in SMALL[1:]:
        flat = t[name].reshape(1, -1)
        rows.append(jnp.pad(flat, ((0, 0), (0, D_MODEL - flat.shape[1]))))
    used = sum(r.shape[0] for r in rows)
    rows.append(jnp.zeros((SMALL_ROWS - used, D_MODEL), F32))
    return jnp.concatenate(rows, axis=0)


def _unpack_small(packed, shapes):
    out = {"w_spatial": packed[:64].reshape(shapes["w_spatial"])}
    for i, name in enumerate(SMALL[1:]):
        size = math.prod(shapes[name])
        out[name] = packed[64 + i, :size].reshape(shapes[name])
    return out


def _slabs(full):
    return full.reshape(N_DEV, full.shape[0] // N_DEV, full.shape[1])


def kernel(x, p, ln_pre_mix, w_in, sgu_ln_g, sgu_ln_b, w_spatial, b_spatial, attn_out_norm, sgu_out_norm, w_out, ln_post_mix, ln_pre_ffn, w_gate_up, w_down, ln_post_ffn, w_pe_gate, b_pe_gate, w_pe_proj, loss_target, m_ln_pre_mix, m_w_in, m_sgu_ln_g, m_sgu_ln_b, m_w_spatial, m_b_spatial, m_attn_out_norm, m_sgu_out_norm, m_w_out, m_ln_post_mix, m_ln_pre_ffn, m_w_gate_up, m_w_down, m_ln_post_ffn, m_w_pe_gate, m_b_pe_gate, m_w_pe_proj, v_ln_pre_mix, v_w_in, v_sgu_ln_g, v_sgu_ln_b, v_w_spatial, v_b_spatial, v_attn_out_norm, v_sgu_out_norm, v_w_out, v_ln_post_mix, v_ln_pre_ffn, v_w_gate_up, v_w_down, v_ln_post_ffn, v_w_pe_gate, v_b_pe_gate, v_w_pe_proj):
    given = dict(locals())
    w = {n: given[n] for n in WEIGHTS}
    m = {n: given["m_" + n] for n in WEIGHTS}
    v = {n: given["v_" + n] for n in WEIGHTS}
    shapes = {n: w[n].shape for n in WEIGHTS}
    xs, ps, target = x[0], p[0, 0], loss_target[0]

    shard = {n: w[n][0].astype(BF) for n in ROW_SHARDED}
    shard.update({n: w[n][0].T.astype(BF) for n in COL_SHARDED})
    sm = {n: w[n][0] for n in SMALL}
    sm = {n: (a.reshape(1, -1) if a.ndim == 1 else a) for n, a in sm.items()}
    slopes = jnp.broadcast_to((2.0 ** -(jnp.arange(8, dtype=F32) + 1.0)).reshape(4, 2, 1), (4, 2, LANES))
    slopes = jnp.concatenate([slopes, jnp.zeros((4, 6, LANES), F32)], axis=1)
    b_st = jnp.pad(sm["b_spatial"].T, ((0, 0), (0, LANES - N_GROUPS)))

    def full(gathered):
        return gathered.reshape(-1, gathered.shape[-1])

    w_in_t = full(_all_gather(shard["w_in"], "gather_w_in"))
    proj, a = _in_proj(xs, sm["ln_pre_mix"], w_in_t)
    later = ("w_out", "w_gate_up", "w_down", "w_pe_gate", "w_pe_proj")
    attn, lse, gathered = _attn_fwd(proj, slopes, [shard[n] for n in later])
    w_out_f, w_gu_t, w_down_f, w_peg_f, w_pep_t = [full(g) for g in gathered]
    sgu = _sgu_fwd(proj, sm["sgu_ln_g"], sm["sgu_ln_b"], sm["w_spatial"], b_st)
    groups, mixed, h1, f = _out_proj(attn, sgu, xs, sm["attn_out_norm"], sm["sgu_out_norm"], w_out_f,
                                     sm["ln_post_mix"], sm["ln_pre_ffn"])
    g, u, act = _gate_up(f, w_gu_t)
    y, h2 = _down_proj(act, w_down_f, h1, sm["ln_post_ffn"])
    dh3, dpp, dpre, h2b, pb, loss_cols, db_peg = _pe_and_loss(h2, ps, target, w_peg_f, sm["b_pe_gate"], w_pep_t)
    loss = lax.psum(0.5 * jnp.sum(loss_cols) * (1.0 / D_MODEL), ("x", "y", "c"))

    arrived = {}
    g_pep_t = _weight_grad(dpp, pb, "w_pe_proj")
    g_peg = _weight_grad(h2b, dpre, "w_pe_gate")
    dh2, dy, d_pff = _pe_bwd(dpre, w_peg_f, dh3, y, sm["ln_post_ffn"])
    g_down = _weight_grad(act, dy, "w_down")
    dg, du, (arrived["w_pe_proj"], arrived["w_pe_gate"]) = _down_bwd(dy, w_down_f, g, u, [_slabs(g_pep_t), _slabs(g_peg)])
    g_gu_t = _weight_grad(dg, f, "w_gate", rows=2 * D_FF)
    g_gu_t = _weight_grad(du, f, "w_up", into=g_gu_t, row_tile=FF_TILES, rows=2 * D_FF)
    dh1, dmx, d_pf, d_pm, (arrived["w_down"],) = _ffn_in_bwd(dg, du, w_gu_t, h1, dh2, mixed, sm["ln_pre_ffn"],
                                                            sm["ln_post_mix"], [_slabs(g_down)])
    g_out = _weight_grad(groups, dmx, "w_out")
    dattn, dsgu, d_ga, d_gs = _out_bwd(dmx, w_out_f, attn, sgu, sm["attn_out_norm"], sm["sgu_out_norm"])
    duz, d_ws, d_bst, d_lg, d_lb = _sgu_bwd(proj, dsgu, sm["sgu_ln_g"], sm["sgu_ln_b"], sm["w_spatial"], b_st)
    gs = dict(ln_pre_mix=jnp.zeros_like(sm["ln_pre_mix"]), sgu_ln_g=d_lg, sgu_ln_b=d_lb, w_spatial=d_ws,
              b_spatial=d_bst[:, :N_GROUPS].T, attn_out_norm=d_ga, sgu_out_norm=d_gs, ln_post_mix=d_pm,
              ln_pre_ffn=d_pf, ln_post_ffn=d_pff, b_pe_gate=db_peg)
    dqkv, (arrived["w_gate_up"], arrived["w_out"], arrived_small) = _attn_bwd(
        proj, dattn, attn, lse, slopes, [_slabs(g_gu_t), _slabs(g_out), _pack_small(gs)], [True, True, False])
    g_in_t = _grad_w_in([dqkv, duz], a)
    grad_x, d_g1, (arrived["w_in"],) = _in_bwd(
        [dqkv, duz], w_in_t, xs, dh1, sm["ln_pre_mix"], [_slabs(g_in_t)], [True])
    (arrived_g1,) = _scatter_call([jnp.pad(d_g1, ((0, 7), (0, 0)))], [False], "ln_pre_mix_grad_exchange")
    arrived_small = lax.dynamic_update_slice(arrived_small, arrived_g1[:, :1], (0, SMALL_G1_ROW, 0))

    res = {}
    for n in ROW_SHARDED:
        res[n] = _reduce_adamw(arrived[n], w[n][0], m[n][0], v[n][0], n)
    for n in ("w_in", "w_gate_up"):
        res[n] = [t.T for t in _reduce_adamw(arrived[n], w[n][0].T, m[n][0].T, v[n][0].T, n)]
    for n in ("w_pe_proj",):
        grad = _reduce(arrived[n], n).T
        res[n] = (grad, *_adamw(grad, w[n][0], m[n][0], v[n][0], n))
    small = _reduce_adamw(arrived_small, _pack_small(w), _pack_small(m), _pack_small(v), "small")
    small = [_unpack_small(t, shapes) for t in small]

    out = []
    for k in range(4):
        out += [res[n][k][None] if n in res else small[k][n] for n in WEIGHTS]
    return (loss, grad_x[None], *out)
```

```python
import math

import jax
import jax.numpy as jnp
from jax import lax
from jax.experimental import pallas as pl
from jax.experimental.pallas import tpu as pltpu

F32 = jnp.float32
BF = jnp.bfloat16


def SDS(shape, dtype):
    return pltpu.HBM(tuple(shape), dtype)

D_MODEL = 1024
ATTN_W = 512
SGU_W = 512
HEAD_DIM = 64
N_GROUPS = 4
GROUP_DIM = 128
CHUNK = 128
D_FF = 2816
PLE_DIM = 256
PROJ = 3 * ATTN_W + 2 * SGU_W
DILATIONS = (1, 4, 16)
QBLK = 128
EPS = 1e-6
NEG = -1e30
N_DEV = 8
LANES = 128

ADAM_LR = 0.001
ADAM_B1 = 0.9
ADAM_B2 = 0.999
ADAM_EPS = 1e-08
ADAM_WD = 0.01
ADAM_STEP = 10

SMALL_ROWS = 80
MIB = 2 ** 20
MESH_ID = pl.DeviceIdType.MESH
HBM = pl.BlockSpec(memory_space=pl.ANY)


def _params(sem, vmem_mib):
    return pltpu.CompilerParams(dimension_semantics=sem, vmem_limit_bytes=vmem_mib * MIB)


def _dot(a, b):
    return jnp.dot(a, b, preferred_element_type=F32)


def _dot_nt(a, b):
    return lax.dot_general(a, b, (((1,), (1,)), ((), ())), preferred_element_type=F32)


def _dot_tn(a, b):
    return lax.dot_general(a, b, (((0,), (0,)), ((), ())), preferred_element_type=F32)


def _rstd(x):
    return lax.rsqrt(jnp.mean(x * x, axis=-1, keepdims=True) + EPS)


def _rms_bwd(dy, x, g):
    r = _rstd(x)
    n = x * r
    dn = dy * g
    dx = r * (dn - n * jnp.mean(dn * n, axis=-1, keepdims=True))
    return dx, dy * n


def _colsum(v):
    return jnp.sum(v, axis=0, keepdims=True)


_G0 = math.sqrt(2.0 / math.pi)
_G1 = 0.044715


def _gelu(x):
    t = jnp.tanh(_G0 * (x + _G1 * x * x * x))
    return 0.5 * x * (1.0 + t), t


def _gelu_grad(x, t):
    return 0.5 * (1.0 + t) + 0.5 * x * (1.0 - t * t) * (_G0 * (1.0 + 3.0 * _G1 * x * x))


def _sigmoid(x):
    return 0.5 * jnp.tanh(0.5 * x) + 0.5


def _row(i):
    return (i, 0)


def _fixed(i):
    return (0, 0)


def _acc_init(step, *refs):
    @pl.when(step == 0)
    def _():
        for r in refs:
            r[...] = jnp.zeros_like(r)


FLIPS = [(dx, dy, dc) for dx in (0, 1) for dy in (0, 1) for dc in (0, 1)][1:]
DMA_SEMS = pltpu.SemaphoreType.DMA


def _mesh_pos():
    return lax.axis_index("x"), lax.axis_index("y"), lax.axis_index("c")


def _remote(src, dst, sems, n, to):
    return pltpu.make_async_remote_copy(src_ref=src, dst_ref=dst, send_sem=sems[0].at[n], recv_sem=sems[1].at[n],
                                        device_id=to, device_id_type=MESH_ID)


class _Scatter:
    def __init__(self, items, sems):
        x, y, c = _mesh_pos()
        me = 4 * x + 2 * y + c
        self.local, self.sends, self.arrivals = [], [], []
        for i, (src, dst, slabbed) in enumerate(items):
            self.local.append(pltpu.make_async_copy(src.at[me] if slabbed else src, dst.at[me], sems[2].at[i]))
            for k, (dx, dy, dc) in enumerate(FLIPS):
                to = (1 - x if dx else x, 1 - y if dy else y, 1 - c if dc else c)
                peer = 4 * to[0] + 2 * to[1] + to[2]
                out = src.at[peer] if slabbed else src
                self.sends.append(_remote(out, dst.at[me], sems, 7 * i + k, to))
                self.arrivals.append(_remote(out, dst.at[peer], sems, 7 * i + k, to))

    def start(self):
        for cp in self.local + self.sends:
            cp.start()

    def wait(self):
        for cp in self.arrivals:
            cp.wait_recv()
        for cp in self.sends:
            cp.wait_send()
        for cp in self.local:
            cp.wait()


def _scatter_sems(n):
    return [DMA_SEMS((7 * n,)), DMA_SEMS((7 * n,)), DMA_SEMS((n,))]


class _Gather:
    def __init__(self, items, sems):
        x, y, c = _mesh_pos()
        me, sibling = (x, y, c), (x, y, 1 - c)
        chips = [(1 - x, y), (x, 1 - y), (1 - x, 1 - y)]
        self.first, self.passed, self.from_chips, self.rest, self.local = [], [], [], [], []
        for i, (src, dst) in enumerate(items):
            def slot(p, dst=dst):
                return dst.at[4 * p[0] + 2 * p[1] + p[2]]

            def copy(k, block, to, own=False, i=i, src=src, slot=slot):
                return _remote(src if own else slot(block), slot(block), sems, 7 * i + k, to)

            self.local.append(pltpu.make_async_copy(src, slot(me), sems[2].at[i]))
            self.first.append(copy(0, me, sibling, own=True))
            self.first += [copy(1 + j, me, (*chip, c), own=True) for j, chip in enumerate(chips)]
            self.passed += [copy(4 + j, (*chip, c), sibling) for j, chip in enumerate(chips)]
            self.from_chips += [copy(1 + j, (*chip, c), me) for j, chip in enumerate(chips)]
            self.rest.append(copy(0, sibling, me))
            self.rest += [copy(4 + j, (*chip, 1 - c), me) for j, chip in enumerate(chips)]

    def start(self):
        for cp in self.local + self.first:
            cp.start()

    def forward(self):
        for arrived, onward in zip(self.from_chips, self.passed):
            arrived.wait_recv()
            onward.start()

    def finish(self):
        for cp in self.rest:
            cp.wait_recv()
        for cp in self.first + self.passed:
            cp.wait_send()
        for cp in self.local:
            cp.wait()


def _all_gather(shard, name):
    def body(x_ref, out_ref, *sems):
        g = _Gather([(x_ref, out_ref)], sems)
        g.start()
        g.forward()
        g.finish()

    return pl.pallas_call(
        body, out_shape=SDS((N_DEV,) + shard.shape, shard.dtype), in_specs=[HBM], out_specs=HBM,
        scratch_shapes=_scatter_sems(1), name=name)(shard)


def _scatter_call(srcs, slabbed, name):
    n = len(srcs)

    def body(*refs):
        sc = _Scatter(list(zip(refs[:n], refs[n:2 * n], slabbed)), refs[2 * n:])
        sc.start()
        sc.wait()

    shapes = [SDS(s.shape if sl else (N_DEV,) + s.shape, s.dtype) for s, sl in zip(srcs, slabbed)]
    return pl.pallas_call(body, out_shape=shapes, in_specs=[HBM] * n, out_specs=[HBM] * n,
                          scratch_shapes=_scatter_sems(n), name=name)(*srcs)


def _in_proj(x, g1, w_in_t, tm=512):
    T = x.shape[0]

    def body(x_ref, g_ref, w_ref, proj_ref, a_ref):
        xv = x_ref[...]
        a = (xv * _rstd(xv) * g_ref[...]).astype(BF)
        a_ref[...] = a
        proj_ref[...] = _dot_nt(a, w_ref[...])

    return pl.pallas_call(
        body, grid=(T // tm,),
        in_specs=[pl.BlockSpec((tm, D_MODEL), _row), pl.BlockSpec((1, D_MODEL), _fixed),
                  pl.BlockSpec((PROJ, D_MODEL), _fixed)],
        out_specs=[pl.BlockSpec((tm, PROJ), _row), pl.BlockSpec((tm, D_MODEL), _row)],
        out_shape=[SDS((T, PROJ), F32), SDS((T, D_MODEL), BF)],
        name="in_proj", compiler_params=_params(("arbitrary",), 48))(x, g1, w_in_t)


ATTN_GROUP = 16


def _attn_bias(sl_ref, bias_ref):
    qi = lax.broadcasted_iota(jnp.int32, (QBLK, QBLK), 0)
    kj = lax.broadcasted_iota(jnp.int32, (QBLK, QBLK), 1)
    step = qi - kj
    for di, d in enumerate(DILATIONS):
        for j in range(2):
            sl = sl_ref[0, j:j + 1, :]
            cur = jnp.where(step >= 0, -sl * (step * d).astype(F32), NEG)
            prev = jnp.where(step <= 0, -sl * ((step + QBLK) * d).astype(F32), NEG)
            rows = slice(j * QBLK, (j + 1) * QBLK)
            bias_ref[di * 2, rows, :QBLK] = prev
            bias_ref[di * 2, rows, QBLK:] = cur
            bias_ref[di * 2 + 1, rows, :QBLK] = jnp.full((QBLK, QBLK), NEG, F32)
            bias_ref[di * 2 + 1, rows, QBLK:] = cur


def _stack_heads(x, lo):
    return jnp.concatenate([jnp.where(lo, x, 0.0), jnp.where(lo, 0.0, x)], axis=0).astype(BF)


def _unstack_heads(x, lo):
    return jnp.where(lo, x[:QBLK], x[QBLK:])


def _attn_rows(start, d, blocks=1):
    if d == 1:
        return pl.ds(pl.multiple_of(start, QBLK), blocks * QBLK)
    return pl.ds(start, blocks * QBLK, stride=d)


def _attn_group_index(i, nblk, d, group):
    per = nblk // d // group
    r = i // per
    n0 = (i % per) * group
    start = r + (d * QBLK) * n0
    pstart = jnp.maximum(start - d * QBLK, r)
    return start, pstart, n0 == 0


def _attn_plan(nblk, d):
    group = min(ATTN_GROUP, nblk // d)
    return group, max(1, min(ATTN_GROUP // group, d))


def _attn_fwd(proj, slopes, to_gather):
    T = proj.shape[0]
    nblk = T // QBLK
    big = dict(pipeline_mode=pl.Buffered(1))
    n_g = len(to_gather)

    def body(q_ref, k_ref, v_ref, sl_ref, *rest):
        srcs, (o_ref, m_ref), dsts = rest[:n_g], rest[n_g:n_g + 2], rest[n_g + 2:2 * n_g + 2]
        sems, (l_ref, bias_ref) = rest[2 * n_g + 2:2 * n_g + 5], rest[2 * n_g + 5:]
        h = pl.program_id(0)

        @pl.when(h == 0)
        def _():
            _Gather(list(zip(srcs, dsts)), sems).start()

        @pl.when(h == 2)
        def _():
            _Gather(list(zip(srcs, dsts)), sems).forward()

        _attn_bias(sl_ref, bias_ref)
        lo = lax.broadcasted_iota(jnp.int32, (1, LANES), 1) < HEAD_DIM

        order = list(enumerate(DILATIONS))[::-1]
        for di, d in order:
            group, segs = _attn_plan(nblk, d)
            fresh, last = di == order[0][0], di == order[-1][0]

            def step(i, carry, segs=segs, **kw):
                for s in range(segs):
                    segment(i * segs + s, **kw)
                return carry

            def segment(i, d=d, di=di, group=group, fresh=fresh, last=last):
                start, pstart, first = _attn_group_index(i, nblk, d, group)
                prows = _attn_rows(pstart, d)
                k_prev, v_prev = k_ref[prows, :].astype(BF), v_ref[prows, :].astype(BF)
                for b in range(group):
                    out = _attn_rows(start + b * (d * QBLK), d)
                    k_own, v_own = k_ref[out, :].astype(BF), v_ref[out, :].astype(BF)
                    k2, v2 = jnp.concatenate([k_prev, k_own], axis=0), jnp.concatenate([v_prev, v_own], axis=0)
                    k_prev, v_prev = k_own, v_own
                    bias = bias_ref[di * 2 + first.astype(jnp.int32)] if b == 0 else bias_ref[di * 2]
                    s = _dot_nt(_stack_heads(q_ref[out, :] * (HEAD_DIM ** -0.5), lo), k2) + bias
                    m = jnp.max(s, axis=-1, keepdims=True)
                    pr = jnp.exp(s - m)
                    m_b = _unstack_heads(m, lo)
                    l_b = _unstack_heads(jnp.sum(pr, axis=-1, keepdims=True), lo)
                    o_b = _unstack_heads(_dot(pr.astype(BF), v2), lo)
                    if fresh:
                        m_ref[out, :] = m_b
                        l_ref[out, :] = l_b
                        o_ref[out, :] = o_b
                        continue
                    m_o = m_ref[out, :]
                    m_n = jnp.maximum(m_o, m_b)
                    wa, wb = jnp.exp(m_o - m_n), jnp.exp(m_b - m_n)
                    l_n = wa * l_ref[out, :] + wb * l_b
                    o_n = wa * o_ref[out, :] + wb * o_b
                    if last:
                        m_ref[out, :] = m_n + jnp.log(l_n)
                        o_ref[out, :] = o_n / l_n
                    else:
                        m_ref[out, :] = m_n
                        l_ref[out, :] = l_n
                        o_ref[out, :] = o_n

            lax.fori_loop(0, nblk // (group * segs), step, 0)

        @pl.when(h == pl.num_programs(0) - 1)
        def _():
            _Gather(list(zip(srcs, dsts)), sems).finish()

    col = lambda base: pl.BlockSpec((T, LANES), lambda h: (0, base + h), **big)
    tok = pl.BlockSpec((T, LANES), lambda h: (0, h))
    outs = pl.pallas_call(
        body, grid=(4,),
        in_specs=[col(0), col(4), col(8), pl.BlockSpec((1, 8, LANES), lambda h: (h, 0, 0))] + [HBM] * n_g,
        out_specs=[tok, tok] + [HBM] * n_g,
        out_shape=[SDS((T, ATTN_W), F32), SDS((T, ATTN_W), F32)]
        + [SDS((N_DEV,) + g.shape, g.dtype) for g in to_gather],
        scratch_shapes=_scatter_sems(n_g) + [pltpu.VMEM((T, LANES), F32), pltpu.VMEM((6, 2 * QBLK, 2 * QBLK), F32)],
        name="attn_fwd", compiler_params=_params(("arbitrary",), 48))(proj, proj, proj, slopes, *to_gather)
    return outs[0], outs[1], outs[2:]


def _sgu_norm(zv, ln_g, ln_b):
    gz, tz = _gelu(zv)
    mu = jnp.mean(gz, axis=-1, keepdims=True)
    xc = gz - mu
    rs = lax.rsqrt(jnp.mean(xc * xc, axis=-1, keepdims=True) + EPS)
    xhat = xc * rs
    return xhat * ln_g + ln_b, xhat, rs, tz


def _causal(w):
    i = lax.broadcasted_iota(jnp.int32, (CHUNK, CHUNK), 0)
    j = lax.broadcasted_iota(jnp.int32, (CHUNK, CHUNK), 1)
    return jnp.where(i >= j, w, 0.0)


def _sgu_fwd(proj, ln_g, ln_b, w_s, b_st, tm=512):
    T = proj.shape[0]

    def body(u_ref, z_ref, g_ref, b_ref, w_ref, bs_ref, out_ref):
        for g in range(N_GROUPS):
            wm = _causal(w_ref[g]).astype(BF)
            cols = slice(g * GROUP_DIM, (g + 1) * GROUP_DIM)
            for c in range(tm // CHUNK):
                rows = slice(c * CHUNK, (c + 1) * CHUNK)
                zn, _, _, _ = _sgu_norm(z_ref[rows, cols], g_ref[...], b_ref[...])
                mixed = _dot(wm, zn.astype(BF)) + bs_ref[:, g:g + 1]
                gu, _ = _gelu(u_ref[rows, cols])
                out_ref[rows, cols] = gu * mixed

    return pl.pallas_call(
        body, grid=(T // tm,),
        in_specs=[pl.BlockSpec((tm, SGU_W), lambda i: (i, 3)), pl.BlockSpec((tm, SGU_W), lambda i: (i, 4)),
                  pl.BlockSpec((1, GROUP_DIM), _fixed), pl.BlockSpec((1, GROUP_DIM), _fixed),
                  pl.BlockSpec((N_GROUPS, CHUNK, CHUNK), lambda i: (0, 0, 0)), pl.BlockSpec((CHUNK, LANES), _fixed)],
        out_specs=pl.BlockSpec((tm, SGU_W), _row),
        out_shape=SDS((T, SGU_W), F32),
        name="sgu_fwd", compiler_params=_params(("arbitrary",), 32))(proj, proj, ln_g, ln_b, w_s, b_st)


def _out_proj(attn, sgu, x, g_a, g_s, w_out, g_pm, g_pf, tm=512):
    T = x.shape[0]

    def body(a_ref, s_ref, x_ref, ga_ref, gs_ref, w_ref, gpm_ref, gpf_ref, grp_ref, mixed_ref, h1_ref, f_ref):
        av, sv = a_ref[...], s_ref[...]
        an = (av * _rstd(av) * ga_ref[...]).astype(BF)
        sn = (sv * _rstd(sv) * gs_ref[...]).astype(BF)
        grp_ref[:, :ATTN_W] = an
        grp_ref[:, ATTN_W:] = sn
        mixed = _dot(an, w_ref[:ATTN_W, :]) + _dot(sn, w_ref[ATTN_W:, :])
        mixed_ref[...] = mixed
        h1 = x_ref[...] + mixed * _rstd(mixed) * gpm_ref[...]
        h1_ref[...] = h1
        f_ref[...] = (h1 * _rstd(h1) * gpf_ref[...]).astype(BF)

    tok = lambda w: pl.BlockSpec((tm, w), _row)
    vec = lambda w: pl.BlockSpec((1, w), _fixed)
    return pl.pallas_call(
        body, grid=(T // tm,),
        in_specs=[tok(ATTN_W), tok(SGU_W), tok(D_MODEL), vec(ATTN_W), vec(SGU_W),
                  pl.BlockSpec((D_MODEL, D_MODEL), _fixed), vec(D_MODEL), vec(D_MODEL)],
        out_specs=[tok(D_MODEL)] * 4,
        out_shape=[SDS((T, D_MODEL), BF), SDS((T, D_MODEL), F32), SDS((T, D_MODEL), F32), SDS((T, D_MODEL), BF)],
        name="out_proj", compiler_params=_params(("arbitrary",), 48))(attn, sgu, x, g_a, g_s, w_out, g_pm, g_pf)


FF_TILE = 1408
FF_TILES = D_FF // FF_TILE
FF_CHUNK = 256


def _gate_up(f, w_gu_t, tm=512):
    T = f.shape[0]
    tn = FF_TILE

    def body(f_ref, wg_ref, wu_ref, g_ref, u_ref, act_ref):
        fv = f_ref[...]
        g = _dot_nt(fv, wg_ref[...])
        u = _dot_nt(fv, wu_ref[...])
        g_ref[...] = g.astype(BF)
        u_ref[...] = u.astype(BF)
        act_ref[...] = (g * _sigmoid(g) * u).astype(BF)

    ospec = pl.BlockSpec((tm, tn), lambda j, i: (i, j))
    return pl.pallas_call(
        body, grid=(FF_TILES, T // tm),
        in_specs=[pl.BlockSpec((tm, D_MODEL), lambda j, i: (i, 0)), pl.BlockSpec((tn, D_MODEL), lambda j, i: (j, 0)),
                  pl.BlockSpec((tn, D_MODEL), lambda j, i: (j + FF_TILES, 0))],
        out_specs=[ospec] * 3, out_shape=[SDS((T, D_FF), BF)] * 3,
        name="gate_up", compiler_params=_params(("arbitrary", "arbitrary"), 40))(f, w_gu_t, w_gu_t)


def _down_proj(act, w_down, h1, g_pff, tm=512):
    T = act.shape[0]

    def body(a_ref, w_ref, h1_ref, g_ref, y_ref, h2_ref):
        y = _dot(a_ref[...], w_ref[...])
        y_ref[...] = y
        h2_ref[...] = h1_ref[...] + y * _rstd(y) * g_ref[...]

    return pl.pallas_call(
        body, grid=(T // tm,),
        in_specs=[pl.BlockSpec((tm, D_FF), _row), pl.BlockSpec((D_FF, D_MODEL), _fixed),
                  pl.BlockSpec((tm, D_MODEL), _row), pl.BlockSpec((1, D_MODEL), _fixed)],
        out_specs=[pl.BlockSpec((tm, D_MODEL), _row)] * 2,
        out_shape=[SDS((T, D_MODEL), F32)] * 2,
        name="down_proj", compiler_params=_params(("arbitrary",), 48))(act, w_down, h1, g_pff)


def _pe_loss_and_bwd(h2, p, target, y, w_peg, b_peg, w_pep_t, g_pff, tm=512):
    T = h2.shape[0]

    def body(h2_ref, p_ref, t_ref, y_ref, wg_ref, b_ref, wp_ref, g_ref,
             dh2_ref, dy_ref, dpp_ref, dpre_ref, h2b_ref, pb_ref, loss_ref, db_ref, dg_ref):
        _acc_init(pl.program_id(0), loss_ref, db_ref, dg_ref)
        h2v = h2_ref[...]
        h2b = h2v.astype(BF)
        pb = p_ref[...].astype(BF)
        h2b_ref[...] = h2b
        pb_ref[...] = pb
        gate = _sigmoid(_dot(h2b, wg_ref[...]) + b_ref[...])
        pp = _dot_nt(pb, wp_ref[...])
        diff = h2v + gate * pp - t_ref[...]
        loss_ref[...] += _colsum(diff * diff)
        dh3 = diff * (1.0 / D_MODEL)
        dpp_ref[...] = (dh3 * gate).astype(BF)
        dpre = dh3 * pp * (gate * (1.0 - gate))
        dpre_b = dpre.astype(BF)
        dpre_ref[...] = dpre_b
        db_ref[...] += _colsum(dpre)
        dh2 = dh3 + _dot_nt(dpre_b, wg_ref[...])
        dh2_ref[...] = dh2
        dy, dg = _rms_bwd(dh2, y_ref[...], g_ref[...])
        dy_ref[...] = dy.astype(BF)
        dg_ref[...] += _colsum(dg)

    tok = lambda w: pl.BlockSpec((tm, w), _row)
    vec = pl.BlockSpec((1, D_MODEL), _fixed)
    return pl.pallas_call(
        body, grid=(T // tm,),
        in_specs=[tok(D_MODEL), tok(PLE_DIM), tok(D_MODEL), tok(D_MODEL), pl.BlockSpec((D_MODEL, D_MODEL), _fixed), vec,
                  pl.BlockSpec((D_MODEL, PLE_DIM), _fixed), vec],
        out_specs=[tok(D_MODEL)] * 5 + [tok(PLE_DIM), vec, vec, vec],
        out_shape=[SDS((T, D_MODEL), F32)] + [SDS((T, D_MODEL), BF)] * 4
        + [SDS((T, PLE_DIM), BF)] + [SDS((1, D_MODEL), F32)] * 3,
        name="pe_loss_and_bwd", compiler_params=_params(("arbitrary",), 56))(
            h2, p, target, y, w_peg, b_peg, w_pep_t, g_pff)


def _weight_grad(a, dy, name, into=None, row_tile=0, rows=None, out_dtype=F32, tk=512):
    n = dy.shape[1]
    tn = min(n, 1024)
    T, ka = a.shape
    tka = FF_TILE if ka == D_FF else min(ka, 1024)
    rows = ka if rows is None else rows
    direct = out_dtype == F32

    def body(a_ref, dy_ref, *rest):
        out_ref, acc_ref = (rest[-1], rest[-1]) if direct else rest[-2:]
        _acc_init(pl.program_id(2), acc_ref)
        acc_ref[...] += _dot_tn(a_ref[...].astype(BF), dy_ref[...].astype(BF))
        if not direct:
            @pl.when(pl.program_id(2) == pl.num_programs(2) - 1)
            def _():
                out_ref[...] = acc_ref[...].astype(out_dtype)

    carried = [] if into is None else [into]
    return pl.pallas_call(
        body, grid=(ka // tka, n // tn, T // tk),
        in_specs=[pl.BlockSpec((tk, tka), lambda i, j, k: (k, i)), pl.BlockSpec((tk, tn), lambda i, j, k: (k, j))]
        + [HBM] * len(carried),
        out_specs=pl.BlockSpec((tka, tn), lambda i, j, k: (i + row_tile, j)),
        out_shape=SDS((rows, n), out_dtype), input_output_aliases={2: 0} if carried else {},
        scratch_shapes=[] if direct else [pltpu.VMEM((tka, tn), F32)],
        name="grad_" + name, compiler_params=_params(("arbitrary",) * 3, 40))(a, dy, *carried)


def _grad_w_in(dparts, a, tk=512):
    T = a.shape[0]

    def body(*refs):
        d_refs, a_ref, out_ref, acc_ref = refs[:len(dparts)], refs[-3], refs[-2], refs[-1]
        k = pl.program_id(0)
        _acc_init(k, acc_ref)
        cols = [r[part].astype(BF) for r in d_refs for part in range(r.shape[0])]
        acc_ref[...] += _dot_tn(jnp.concatenate(cols, axis=1), a_ref[...])

        @pl.when(k == pl.num_programs(0) - 1)
        def _():
            out_ref[...] = acc_ref[...].astype(BF)

    return pl.pallas_call(
        body, grid=(T // tk,),
        in_specs=[pl.BlockSpec((d.shape[0], tk, d.shape[2]), lambda k: (0, k, 0)) for d in dparts]
        + [pl.BlockSpec((tk, D_MODEL), lambda k: (k, 0))],
        out_specs=pl.BlockSpec((PROJ, D_MODEL), lambda k: (0, 0)),
        out_shape=SDS((PROJ, D_MODEL), BF), scratch_shapes=[pltpu.VMEM((PROJ, D_MODEL), F32)],
        name="grad_w_in", compiler_params=_params(("arbitrary",), 48))(*dparts, a)


def _down_bwd(dy, w_down, g, u, to_send, tm=512):
    T = dy.shape[0]
    n_s = len(to_send)

    def body(dy_ref, w_ref, g_ref, u_ref, *rest):
        srcs, (dg_ref, du_ref), dsts, sems = rest[:n_s], rest[n_s:n_s + 2], rest[n_s + 2:2 * n_s + 2], rest[2 * n_s + 2:]
        i = pl.program_id(0)
        items = list(zip(srcs, dsts, [True] * n_s))

        @pl.when(i == 0)
        def _():
            _Scatter(items, sems).start()

        dyv = dy_ref[...]
        for c in range(D_FF // FF_CHUNK):
            cols = slice(c * FF_CHUNK, (c + 1) * FF_CHUNK)
            dact = _dot_nt(dyv, w_ref[cols, :])
            gv = g_ref[:, cols].astype(F32)
            uv = u_ref[:, cols].astype(F32)
            s = _sigmoid(gv)
            ds = dact * s
            dg_ref[:, cols] = (ds * uv * (1.0 + gv * (1.0 - s))).astype(BF)
            du_ref[:, cols] = (ds * gv).astype(BF)

        @pl.when(i == pl.num_programs(0) - 1)
        def _():
            _Scatter(items, sems).wait()

    tile = pl.BlockSpec((tm, D_FF), _row)
    outs = pl.pallas_call(
        body, grid=(T // tm,),
        in_specs=[pl.BlockSpec((tm, D_MODEL), _row),
                  pl.BlockSpec((D_FF, D_MODEL), _fixed, pipeline_mode=pl.Buffered(1)), tile, tile] + [HBM] * n_s,
        out_specs=[tile, tile] + [HBM] * n_s,
        out_shape=[SDS((T, D_FF), BF)] * 2 + [SDS(s.shape, s.dtype) for s in to_send],
        scratch_shapes=_scatter_sems(n_s),
        name="down_bwd", compiler_params=_params(("arbitrary",), 48))(dy, w_down, g, u, *to_send)
    return outs[0], outs[1], outs[2:]


def _ffn_in_bwd(dg, du, w_gu_t, h1, dh2, mixed, g_pf, g_pm, to_send, tm=512):
    T = h1.shape[0]
    n_s = len(to_send)

    def body(dg_ref, du_ref, wg_ref, wu_ref, h1_ref, dh2_ref, mx_ref, gpf_ref, gpm_ref, *rest):
        srcs, outs, dsts, sems = rest[:n_s], rest[n_s:n_s + 4], rest[n_s + 4:2 * n_s + 4], rest[2 * n_s + 4:]
        dh1_ref, dmx_ref, dgpf_ref, dgpm_ref = outs
        i = pl.program_id(0)
        items = list(zip(srcs, dsts, [True] * n_s))
        _acc_init(i, dgpf_ref, dgpm_ref)

        @pl.when(i == 0)
        def _():
            _Scatter(items, sems).start()

        df = _dot(dg_ref[...], wg_ref[...]) + _dot(du_ref[...], wu_ref[...])
        dx, dgf = _rms_bwd(df, h1_ref[...], gpf_ref[...])
        dh1 = dh2_ref[...] + dx
        dh1_ref[...] = dh1
        dmx, dgm = _rms_bwd(dh1, mx_ref[...], gpm_ref[...])
        dmx_ref[...] = dmx.astype(BF)
        dgpf_ref[...] += _colsum(dgf)
        dgpm_ref[...] += _colsum(dgm)

        @pl.when(i == pl.num_programs(0) - 1)
        def _():
            _Scatter(items, sems).wait()

    tok = lambda w: pl.BlockSpec((tm, w), _row)
    vec = pl.BlockSpec((1, D_MODEL), _fixed)
    outs = pl.pallas_call(
        body, grid=(T // tm,),
        in_specs=[tok(D_FF), tok(D_FF), pl.BlockSpec((D_FF, D_MODEL), lambda i: (0, 0), pipeline_mode=pl.Buffered(1)),
                  pl.BlockSpec((D_FF, D_MODEL), lambda i: (1, 0), pipeline_mode=pl.Buffered(1)),
                  tok(D_MODEL), tok(D_MODEL), tok(D_MODEL), vec, vec]
        + [HBM] * n_s,
        out_specs=[tok(D_MODEL), tok(D_MODEL), vec, vec] + [HBM] * n_s,
        out_shape=[SDS((T, D_MODEL), F32), SDS((T, D_MODEL), BF), SDS((1, D_MODEL), F32), SDS((1, D_MODEL), F32)]
        + [SDS(s.shape, s.dtype) for s in to_send],
        scratch_shapes=_scatter_sems(n_s),
        name="ffn_in_bwd", compiler_params=_params(("arbitrary",), 56))(
            dg, du, w_gu_t, w_gu_t, h1, dh2, mixed, g_pf, g_pm, *to_send)
    return outs[0], outs[1], outs[2], outs[3], outs[4:]


def _out_bwd(dmx, w_out, attn, sgu, g_a, g_s, tm=512):
    T = attn.shape[0]

    def body(dm_ref, w_ref, a_ref, s_ref, ga_ref, gs_ref, da_ref, ds_ref, dga_ref, dgs_ref):
        _acc_init(pl.program_id(0), dga_ref, dgs_ref)
        dgr = _dot_nt(dm_ref[...], w_ref[...])
        da, dga = _rms_bwd(dgr[:, :ATTN_W], a_ref[...], ga_ref[...])
        ds, dgs = _rms_bwd(dgr[:, ATTN_W:], s_ref[...], gs_ref[...])
        da_ref[...] = da
        ds_ref[...] = ds
        dga_ref[...] += _colsum(dga)
        dgs_ref[...] += _colsum(dgs)

    tok = lambda w: pl.BlockSpec((tm, w), _row)
    vec = lambda w: pl.BlockSpec((1, w), _fixed)
    return pl.pallas_call(
        body, grid=(T // tm,),
        in_specs=[tok(D_MODEL), pl.BlockSpec((D_MODEL, D_MODEL), _fixed), tok(ATTN_W), tok(SGU_W), vec(ATTN_W), vec(SGU_W)],
        out_specs=[tok(ATTN_W), tok(SGU_W), vec(ATTN_W), vec(SGU_W)],
        out_shape=[SDS((T, ATTN_W), F32), SDS((T, SGU_W), F32), SDS((1, ATTN_W), F32), SDS((1, SGU_W), F32)],
        name="out_bwd", compiler_params=_params(("arbitrary",), 40))(dmx, w_out, attn, sgu, g_a, g_s)


def _sgu_bwd(proj, dsgu, ln_g, ln_b, w_s, b_st, tm=512):
    T = proj.shape[0]

    def body(u_ref, z_ref, ds_ref, g_ref, b_ref, w_ref, bs_ref,
             duz_ref, dw_ref, dbs_ref, dlg_ref, dlb_ref, dbacc_ref):
        du_ref, dz_ref = duz_ref.at[0], duz_ref.at[1]
        step = pl.program_id(0)
        _acc_init(step, dw_ref, dbs_ref, dlg_ref, dlb_ref, dbacc_ref)
        lng, lnb = g_ref[...], b_ref[...]
        for g in range(N_GROUPS):
            wm = _causal(w_ref[g]).astype(BF)
            cols = slice(g * GROUP_DIM, (g + 1) * GROUP_DIM)
            for c in range(tm // CHUNK):
                rows = slice(c * CHUNK, (c + 1) * CHUNK)
                zv, uv, dout = z_ref[rows, cols], u_ref[rows, cols], ds_ref[rows, cols]
                zn, xhat, rs, tz = _sgu_norm(zv, lng, lnb)
                znb = zn.astype(BF)
                mixed = _dot(wm, znb) + bs_ref[:, g:g + 1]
                gu, tu = _gelu(uv)
                du_ref[rows, cols] = (dout * mixed * _gelu_grad(uv, tu)).astype(BF)
                dmix = dout * gu
                dmb = dmix.astype(BF)
                dw_ref[g] += _causal(_dot_nt(dmb, znb))
                dbacc_ref[g] += dmix
                dzn = _dot_tn(wm, dmb)
                dlg_ref[...] += _colsum(dzn * xhat)
                dlb_ref[...] += _colsum(dzn)
                dxh = dzn * lng
                dgz = rs * (dxh - jnp.mean(dxh, axis=-1, keepdims=True)
                            - xhat * jnp.mean(dxh * xhat, axis=-1, keepdims=True))
                dz_ref[rows, cols] = (dgz * _gelu_grad(zv, tz)).astype(BF)

        @pl.when(step == pl.num_programs(0) - 1)
        def _():
            lane = lax.broadcasted_iota(jnp.int32, (CHUNK, LANES), 1)
            acc = jnp.zeros((CHUNK, LANES), F32)
            for g in range(N_GROUPS):
                acc = jnp.where(lane == g, jnp.sum(dbacc_ref[g], axis=-1, keepdims=True), acc)
            dbs_ref[...] = acc

    tok = pl.BlockSpec((tm, SGU_W), _row)
    vec = pl.BlockSpec((1, GROUP_DIM), _fixed)
    wsp = pl.BlockSpec((N_GROUPS, CHUNK, CHUNK), lambda i: (0, 0, 0))
    sq = pl.BlockSpec((CHUNK, LANES), _fixed)
    return pl.pallas_call(
        body, grid=(T // tm,),
        in_specs=[pl.BlockSpec((tm, SGU_W), lambda i: (i, 3)), pl.BlockSpec((tm, SGU_W), lambda i: (i, 4)), tok,
                  vec, vec, wsp, sq],
        out_specs=[pl.BlockSpec((2, tm, SGU_W), lambda i: (0, i, 0)), wsp, sq, vec, vec],
        out_shape=[SDS((2, T, SGU_W), BF), SDS((N_GROUPS, CHUNK, CHUNK), F32),
                   SDS((CHUNK, LANES), F32), SDS((1, GROUP_DIM), F32), SDS((1, GROUP_DIM), F32)],
        scratch_shapes=[pltpu.VMEM((N_GROUPS, CHUNK, LANES), F32)],
        name="sgu_bwd", compiler_params=_params(("arbitrary",), 32))(proj, proj, dsgu, ln_g, ln_b, w_s, b_st)


def _attn_bwd(proj, do, o, lse, slopes, to_send, slabbed):
    T = proj.shape[0]
    nblk = T // QBLK
    big = dict(pipeline_mode=pl.Buffered(1))
    n_s = len(to_send)

    def body(q_ref, k_ref, v_ref, do_ref, o_ref, l_ref, sl_ref, *rest):
        srcs, d_ref, dsts = rest[:n_s], rest[n_s], rest[n_s + 1:2 * n_s + 1]
        sems, bias_ref = rest[2 * n_s + 1:2 * n_s + 4], rest[2 * n_s + 4]
        dq_ref, dk_ref, dv_ref = d_ref.at[0], d_ref.at[1], d_ref.at[2]
        h = pl.program_id(0)
        items = list(zip(srcs, dsts, slabbed))

        @pl.when(h == 0)
        def _():
            _Scatter(items, sems).start()

        _attn_bias(sl_ref, bias_ref)
        lo = lax.broadcasted_iota(jnp.int32, (1, LANES), 1) < HEAD_DIM
        scale = HEAD_DIM ** -0.5
        d_ref[...] = jnp.zeros_like(d_ref)

        for di, d in enumerate(DILATIONS):
            group, segs = _attn_plan(nblk, d)

            def step(i, carry, segs=segs, **kw):
                for s in range(segs):
                    segment(i * segs + s, **kw)
                return carry

            def segment(i, d=d, di=di, group=group):
                start, pstart, first = _attn_group_index(i, nblk, d, group)
                rows, prows = _attn_rows(start, d, group), _attn_rows(pstart, d)
                q = q_ref[rows, :] * scale
                k = jnp.concatenate([k_ref[prows, :], k_ref[rows, :]], axis=0).astype(BF)
                v = jnp.concatenate([v_ref[prows, :], v_ref[rows, :]], axis=0).astype(BF)
                dov = do_ref[rows, :]
                prod = dov * o_ref[rows, :]
                lse_g = l_ref[rows, :]
                masks = [lo, ~lo]
                qm = [jnp.where(masks[j], q, 0.0).astype(BF) for j in range(2)]
                dom = [jnp.where(masks[j], dov, 0.0).astype(BF) for j in range(2)]
                for b in range(group):
                    qb = slice(b * QBLK, (b + 1) * QBLK)
                    kb = slice(b * QBLK, (b + 2) * QBLK)
                    which = di * 2 + first.astype(jnp.int32) if b == 0 else di * 2
                    dq_parts, prs, dss = [], [], []
                    for j in range(2):
                        bias = bias_ref[which, j * QBLK:(j + 1) * QBLK, :]
                        delta = jnp.sum(jnp.where(masks[j], prod[qb], 0.0), axis=-1, keepdims=True)
                        lj = lse_g[qb, j * HEAD_DIM:j * HEAD_DIM + 1]
                        pr = jnp.exp(_dot_nt(qm[j][qb], k[kb]) + bias - lj)
                        ds = (pr * (_dot_nt(dom[j][qb], v[kb]) - delta)).astype(BF)
                        dq_parts.append(_dot(ds, k[kb]))
                        prs.append(pr.astype(BF))
                        dss.append(ds)
                    dk_b = _dot_tn(jnp.concatenate(dss, axis=0), jnp.concatenate([qm[0][qb], qm[1][qb]], axis=0))
                    dv_b = _dot_tn(jnp.concatenate(prs, axis=0), jnp.concatenate([dom[0][qb], dom[1][qb]], axis=0))
                    own = _attn_rows(start + b * (d * QBLK), d)
                    dq_ref[own, :] += jnp.where(lo, dq_parts[0], dq_parts[1]) * scale
                    if b == 0:
                        dk_ref[prows, :] += dk_b[:QBLK]
                        dv_ref[prows, :] += dv_b[:QBLK]
                        dk_ref[own, :] += dk_b[QBLK:]
                        dv_ref[own, :] += dv_b[QBLK:]
                    else:
                        two = _attn_rows(start + (b - 1) * (d * QBLK), d, 2)
                        dk_ref[two, :] += dk_b
                        dv_ref[two, :] += dv_b

            lax.fori_loop(0, nblk // (group * segs), step, 0)

        @pl.when(h == pl.num_programs(0) - 1)
        def _():
            _Scatter(items, sems).wait()

    col = lambda base: pl.BlockSpec((T, LANES), lambda h: (0, base + h), **big)
    outs = pl.pallas_call(
        body, grid=(4,),
        in_specs=[col(0), col(4), col(8), col(0), col(0), col(0), pl.BlockSpec((1, 8, LANES), lambda h: (h, 0, 0))]
        + [HBM] * n_s,
        out_specs=[pl.BlockSpec((3, T, LANES), lambda h: (0, 0, h))] + [HBM] * n_s,
        out_shape=[SDS((3, T, ATTN_W), F32)]
        + [SDS(s.shape if sl else (N_DEV,) + s.shape, s.dtype) for s, sl in zip(to_send, slabbed)],
        scratch_shapes=_scatter_sems(n_s) + [pltpu.VMEM((6, 2 * QBLK, 2 * QBLK), F32)],
        name="attn_bwd", compiler_params=_params(("arbitrary",), 60))(proj, proj, proj, do, o, lse, slopes, *to_send)
    return outs[0], outs[1:]


def _in_bwd(dparts, w_in_t, x, dh1, g1, to_send, slabbed, tm=512):
    T = x.shape[0]
    n = len(dparts)
    n_s = len(to_send)
    w = ATTN_W

    def body(*refs):
        d_refs, (w_ref, x_ref, dh1_ref, g_ref), rest = refs[:n], refs[n:n + 4], refs[n + 4:]
        srcs, (dx_ref, dg_ref), dsts, sems = rest[:n_s], rest[n_s:n_s + 2], rest[n_s + 2:2 * n_s + 2], rest[2 * n_s + 2:]
        step = pl.program_id(0)
        items = list(zip(srcs, dsts, slabbed))
        _acc_init(step, dg_ref)

        @pl.when(step == 0)
        def _():
            _Scatter(items, sems).start()

        da = None
        col = 0
        for r in d_refs:
            for part in range(r.shape[0]):
                t = _dot(r[part].astype(BF), w_ref[col * w:(col + 1) * w, :])
                da = t if da is None else da + t
                col += 1
        dx, dg = _rms_bwd(da, x_ref[...], g_ref[...])
        dx_ref[...] = dh1_ref[...] + dx
        dg_ref[...] += _colsum(dg)

        @pl.when(step == pl.num_programs(0) - 1)
        def _():
            _Scatter(items, sems).wait()

    tok = lambda c: pl.BlockSpec((tm, c), _row)
    vec = pl.BlockSpec((1, D_MODEL), _fixed)
    outs = pl.pallas_call(
        body, grid=(T // tm,),
        in_specs=[pl.BlockSpec((d.shape[0], tm, w), lambda i: (0, i, 0)) for d in dparts]
        + [pl.BlockSpec((PROJ, D_MODEL), _fixed), tok(D_MODEL), tok(D_MODEL), vec] + [HBM] * n_s,
        out_specs=[tok(D_MODEL), vec] + [HBM] * n_s,
        out_shape=[SDS((T, D_MODEL), F32), SDS((1, D_MODEL), F32)]
        + [SDS(s.shape if sl else (N_DEV,) + s.shape, s.dtype) for s, sl in zip(to_send, slabbed)],
        scratch_shapes=_scatter_sems(n_s),
        name="in_bwd", compiler_params=_params(("arbitrary",), 52))(*dparts, w_in_t, x, dh1, g1, *to_send)
    return outs[0], outs[1], outs[2:]


def _sum_parts(p_ref):
    g = p_ref[0].astype(F32)
    for s in range(1, N_DEV):
        g = g + p_ref[s].astype(F32)
    return g


def _adamw_math(g, w, m, v):
    nm = ADAM_B1 * m + (1.0 - ADAM_B1) * g
    nv = ADAM_B2 * v + (1.0 - ADAM_B2) * (g * g)
    m_hat = nm / (1.0 - ADAM_B1 ** ADAM_STEP)
    v_hat = nv / (1.0 - ADAM_B2 ** ADAM_STEP)
    return -ADAM_LR * (m_hat / (jnp.sqrt(v_hat) + ADAM_EPS) + ADAM_WD * w), nm, nv


def _row_tile(rows):
    for t in (256, 176, 128, 80):
        if rows % t == 0:
            return t
    raise ValueError(rows)


def _reduce_adamw(parts, w, m, v, name):
    rows, width = w.shape
    tr = _row_tile(rows)

    def body(p_ref, w_ref, m_ref, v_ref, g_ref, d_ref, nm_ref, nv_ref):
        g = _sum_parts(p_ref)
        g_ref[...] = g
        d_ref[...], nm_ref[...], nv_ref[...] = _adamw_math(g, w_ref[...], m_ref[...], v_ref[...])

    blk = pl.BlockSpec((tr, width), _row)
    return pl.pallas_call(
        body, grid=(rows // tr,),
        in_specs=[pl.BlockSpec((N_DEV, tr, width), lambda i: (0, i, 0)), blk, blk, blk],
        out_specs=[blk] * 4, out_shape=[SDS((rows, width), F32)] * 4,
        name="adamw_" + name, compiler_params=_params(("arbitrary",), 32))(parts, w, m, v)


def _reduce(parts, name):
    _, rows, width = parts.shape
    tr = _row_tile(rows)

    def body(p_ref, g_ref):
        g_ref[...] = _sum_parts(p_ref)

    return pl.pallas_call(
        body, grid=(rows // tr,),
        in_specs=[pl.BlockSpec((N_DEV, tr, width), lambda i: (0, i, 0))],
        out_specs=pl.BlockSpec((tr, width), _row), out_shape=SDS((rows, width), F32),
        name="sum_" + name, compiler_params=_params(("arbitrary",), 32))(parts)


def _adamw(g, w, m, v, name):
    rows, width = w.shape
    tr = _row_tile(rows)

    def body(g_ref, w_ref, m_ref, v_ref, d_ref, nm_ref, nv_ref):
        d_ref[...], nm_ref[...], nv_ref[...] = _adamw_math(g_ref[...], w_ref[...], m_ref[...], v_ref[...])

    blk = pl.BlockSpec((tr, width), _row)
    return pl.pallas_call(
        body, grid=(rows // tr,), in_specs=[blk] * 4, out_specs=[blk] * 3, out_shape=[SDS((rows, width), F32)] * 3,
        name="adamw_" + name, compiler_params=_params(("arbitrary",), 32))(g, w, m, v)


SMALL = ("w_spatial", "ln_pre_mix", "ln_post_mix", "ln_pre_ffn", "ln_post_ffn", "b_pe_gate",
         "attn_out_norm", "sgu_out_norm", "b_spatial", "sgu_ln_g", "sgu_ln_b")
SMALL_G1_ROW = 64 + SMALL.index("ln_pre_mix") - 1
ROW_SHARDED = ("w_out", "w_down", "w_pe_gate")
COL_SHARDED = ("w_in", "w_gate_up", "w_pe_proj")
WEIGHTS = ("ln_pre_mix", "w_in", "sgu_ln_g", "sgu_ln_b", "w_spatial", "b_spatial", "attn_out_norm", "sgu_out_norm",
           "w_out", "ln_post_mix", "ln_pre_ffn", "w_gate_up", "w_down", "ln_post_ffn", "w_pe_gate", "b_pe_gate",
           "w_pe_proj")


def _pack_small(t):
    rows = [t["w_spatial"].reshape(-1, D_MODEL)]
    for name in SMALL[1:]:
        flat = t[name].reshape(1, -1)
        rows.append(jnp.pad(flat, ((0, 0), (0, D_MODEL - flat.shape[1]))))
    used = sum(r.shape[0] for r in rows)
    rows.append(jnp.zeros((SMALL_ROWS - used, D_MODEL), F32))
    return jnp.concatenate(rows, axis=0)


def _unpack_small(packed, shapes):
    out = {"w_spatial": packed[:64].reshape(shapes["w_spatial"])}
    for i, name in enumerate(SMALL[1:]):
        size = math.prod(shapes[name])
        out[name] = packed[64 + i, :size].reshape(shapes[name])
    return out


def _slabs(full):
    return full.reshape(N_DEV, full.shape[0] // N_DEV, full.shape[1])


def kernel(x, p, ln_pre_mix, w_in, sgu_ln_g, sgu_ln_b, w_spatial, b_spatial, attn_out_norm, sgu_out_norm, w_out, ln_post_mix, ln_pre_ffn, w_gate_up, w_down, ln_post_ffn, w_pe_gate, b_pe_gate, w_pe_proj, loss_target, m_ln_pre_mix, m_w_in, m_sgu_ln_g, m_sgu_ln_b, m_w_spatial, m_b_spatial, m_attn_out_norm, m_sgu_out_norm, m_w_out, m_ln_post_mix, m_ln_pre_ffn, m_w_gate_up, m_w_down, m_ln_post_ffn, m_w_pe_gate, m_b_pe_gate, m_w_pe_proj, v_ln_pre_mix, v_w_in, v_sgu_ln_g, v_sgu_ln_b, v_w_spatial, v_b_spatial, v_attn_out_norm, v_sgu_out_norm, v_w_out, v_ln_post_mix, v_ln_pre_ffn, v_w_gate_up, v_w_down, v_ln_post_ffn, v_w_pe_gate, v_b_pe_gate, v_w_pe_proj):
    given = dict(locals())
    w = {n: given[n] for n in WEIGHTS}
    m = {n: given["m_" + n] for n in WEIGHTS}
    v = {n: given["v_" + n] for n in WEIGHTS}
    shapes = {n: w[n].shape for n in WEIGHTS}
    xs, ps, target = x[0], p[0, 0], loss_target[0]

    shard = {n: w[n][0].astype(BF) for n in ROW_SHARDED}
    shard.update({n: w[n][0].T.astype(BF) for n in COL_SHARDED})
    sm = {n: w[n][0] for n in SMALL}
    sm = {n: (a.reshape(1, -1) if a.ndim == 1 else a) for n, a in sm.items()}
    slopes = jnp.broadcast_to((2.0 ** -(jnp.arange(8, dtype=F32) + 1.0)).reshape(4, 2, 1), (4, 2, LANES))
    slopes = jnp.concatenate([slopes, jnp.zeros((4, 6, LANES), F32)], axis=1)
    b_st = jnp.pad(sm["b_spatial"].T, ((0, 0), (0, LANES - N_GROUPS)))

    def full(gathered):
        return gathered.reshape(-1, gathered.shape[-1])

    w_in_t = full(_all_gather(shard["w_in"], "gather_w_in"))
    proj, a = _in_proj(xs, sm["ln_pre_mix"], w_in_t)
    later = ("w_out", "w_gate_up", "w_down", "w_pe_gate", "w_pe_proj")
    attn, lse, gathered = _attn_fwd(proj, slopes, [shard[n] for n in later])
    w_out_f, w_gu_t, w_down_f, w_peg_f, w_pep_t = [full(g) for g in gathered]
    sgu = _sgu_fwd(proj, sm["sgu_ln_g"], sm["sgu_ln_b"], sm["w_spatial"], b_st)
    groups, mixed, h1, f = _out_proj(attn, sgu, xs, sm["attn_out_norm"], sm["sgu_out_norm"], w_out_f,
                                     sm["ln_post_mix"], sm["ln_pre_ffn"])
    g, u, act = _gate_up(f, w_gu_t)
    y, h2 = _down_proj(act, w_down_f, h1, sm["ln_post_ffn"])
    dh2, dy, dpp, dpre, h2b, pb, loss_cols, db_peg, d_pff = _pe_loss_and_bwd(
        h2, ps, target, y, w_peg_f, sm["b_pe_gate"], w_pep_t, sm["ln_post_ffn"])
    loss = lax.psum(0.5 * jnp.sum(loss_cols) * (1.0 / D_MODEL), ("x", "y", "c"))

    arrived = {}
    g_pep_t = _weight_grad(dpp, pb, "w_pe_proj")
    g_peg = _weight_grad(h2b, dpre, "w_pe_gate")
    g_down = _weight_grad(act, dy, "w_down")
    dg, du, (arrived["w_pe_proj"], arrived["w_pe_gate"]) = _down_bwd(dy, w_down_f, g, u, [_slabs(g_pep_t), _slabs(g_peg)])
    g_gu_t = _weight_grad(dg, f, "w_gate", rows=2 * D_FF)
    g_gu_t = _weight_grad(du, f, "w_up", into=g_gu_t, row_tile=FF_TILES, rows=2 * D_FF)
    dh1, dmx, d_pf, d_pm, (arrived["w_down"],) = _ffn_in_bwd(dg, du, w_gu_t, h1, dh2, mixed, sm["ln_pre_ffn"],
                                                            sm["ln_post_mix"], [_slabs(g_down)])
    g_out = _weight_grad(groups, dmx, "w_out")
    dattn, dsgu, d_ga, d_gs = _out_bwd(dmx, w_out_f, attn, sgu, sm["attn_out_norm"], sm["sgu_out_norm"])
    duz, d_ws, d_bst, d_lg, d_lb = _sgu_bwd(proj, dsgu, sm["sgu_ln_g"], sm["sgu_ln_b"], sm["w_spatial"], b_st)
    gs = dict(ln_pre_mix=jnp.zeros_like(sm["ln_pre_mix"]), sgu_ln_g=d_lg, sgu_ln_b=d_lb, w_spatial=d_ws,
              b_spatial=d_bst[:, :N_GROUPS].T, attn_out_norm=d_ga, sgu_out_norm=d_gs, ln_post_mix=d_pm,
              ln_pre_ffn=d_pf, ln_post_ffn=d_pff, b_pe_gate=db_peg)
    dqkv, (arrived["w_gate_up"], arrived["w_out"], arrived_small) = _attn_bwd(
        proj, dattn, attn, lse, slopes, [_slabs(g_gu_t), _slabs(g_out), _pack_small(gs)], [True, True, False])
    g_in_t = _grad_w_in([dqkv, duz], a)
    grad_x, d_g1, (arrived["w_in"],) = _in_bwd(
        [dqkv, duz], w_in_t, xs, dh1, sm["ln_pre_mix"], [_slabs(g_in_t)], [True])
    (arrived_g1,) = _scatter_call([jnp.pad(d_g1, ((0, 7), (0, 0)))], [False], "ln_pre_mix_grad_exchange")
    arrived_small = lax.dynamic_update_slice(arrived_small, arrived_g1[:, :1], (0, SMALL_G1_ROW, 0))

    res = {}
    for n in ROW_SHARDED:
        res[n] = _reduce_adamw(arrived[n], w[n][0], m[n][0], v[n][0], n)
    for n in ("w_in", "w_gate_up"):
        res[n] = [t.T for t in _reduce_adamw(arrived[n], w[n][0].T, m[n][0].T, v[n][0].T, n)]
    for n in ("w_pe_proj",):
        grad = _reduce(arrived[n], n).T
        res[n] = (grad, *_adamw(grad, w[n][0], m[n][0], v[n][0], n))
    small = _reduce_adamw(arrived_small, _pack_small(w), _pack_small(m), _pack_small(v), "small")
    small = [_unpack_small(t, shapes) for t in small]

    out = []
    for k in range(4):
        out += [res[n][k][None] if n in res else small[k][n] for n in WEIGHTS]
    return (loss, grad_x[None], *out)
```

```python
import math

import jax
import jax.numpy as jnp
from jax import lax
from jax.experimental import pallas as pl
from jax.experimental.pallas import tpu as pltpu

F32 = jnp.float32
BF = jnp.bfloat16


def SDS(shape, dtype):
    return pltpu.HBM(tuple(shape), dtype)

D_MODEL = 1024
ATTN_W = 512
SGU_W = 512
HEAD_DIM = 64
N_GROUPS = 4
GROUP_DIM = 128
CHUNK = 128
D_FF = 2816
PLE_DIM = 256
PROJ = 3 * ATTN_W + 2 * SGU_W
DILATIONS = (1, 4, 16)
QBLK = 128
EPS = 1e-6
NEG = -1e30
N_DEV = 8
LANES = 128

ADAM_LR = 0.001
ADAM_B1 = 0.9
ADAM_B2 = 0.999
ADAM_EPS = 1e-08
ADAM_WD = 0.01
ADAM_STEP = 10

SMALL_ROWS = 80
MIB = 2 ** 20
MESH_ID = pl.DeviceIdType.MESH
HBM = pl.BlockSpec(memory_space=pl.ANY)


def _params(sem, vmem_mib):
    return pltpu.CompilerParams(dimension_semantics=sem, vmem_limit_bytes=vmem_mib * MIB)


def _dot(a, b):
    return jnp.dot(a, b, preferred_element_type=F32)


def _dot_nt(a, b):
    return lax.dot_general(a, b, (((1,), (1,)), ((), ())), preferred_element_type=F32)


def _dot_tn(a, b):
    return lax.dot_general(a, b, (((0,), (0,)), ((), ())), preferred_element_type=F32)


def _rstd(x):
    return lax.rsqrt(jnp.mean(x * x, axis=-1, keepdims=True) + EPS)


def _rms_bwd(dy, x, g):
    r = _rstd(x)
    n = x * r
    dn = dy * g
    dx = r * (dn - n * jnp.mean(dn * n, axis=-1, keepdims=True))
    return dx, dy * n


def _colsum(v):
    return jnp.sum(v, axis=0, keepdims=True)


_G0 = math.sqrt(2.0 / math.pi)
_G1 = 0.044715


def _gelu(x):
    t = jnp.tanh(_G0 * (x + _G1 * x * x * x))
    return 0.5 * x * (1.0 + t), t


def _gelu_grad(x, t):
    return 0.5 * (1.0 + t) + 0.5 * x * (1.0 - t * t) * (_G0 * (1.0 + 3.0 * _G1 * x * x))


def _sigmoid(x):
    return 0.5 * jnp.tanh(0.5 * x) + 0.5


def _row(i):
    return (i, 0)


def _fixed(i):
    return (0, 0)


def _acc_init(step, *refs):
    @pl.when(step == 0)
    def _():
        for r in refs:
            r[...] = jnp.zeros_like(r)


FLIPS = [(dx, dy, dc) for dx in (0, 1) for dy in (0, 1) for dc in (0, 1)][1:]
DMA_SEMS = pltpu.SemaphoreType.DMA


def _mesh_pos():
    return lax.axis_index("x"), lax.axis_index("y"), lax.axis_index("c")


def _remote(src, dst, sems, n, to):
    return pltpu.make_async_remote_copy(src_ref=src, dst_ref=dst, send_sem=sems[0].at[n], recv_sem=sems[1].at[n],
                                        device_id=to, device_id_type=MESH_ID)


class _Scatter:
    def __init__(self, items, sems):
        x, y, c = _mesh_pos()
        me = 4 * x + 2 * y + c
        self.local, self.sends, self.arrivals = [], [], []
        for i, (src, dst, slabbed) in enumerate(items):
            self.local.append(pltpu.make_async_copy(src.at[me] if slabbed else src, dst.at[me], sems[2].at[i]))
            for k, (dx, dy, dc) in enumerate(FLIPS):
                to = (1 - x if dx else x, 1 - y if dy else y, 1 - c if dc else c)
                peer = 4 * to[0] + 2 * to[1] + to[2]
                out = src.at[peer] if slabbed else src
                self.sends.append(_remote(out, dst.at[me], sems, 7 * i + k, to))
                self.arrivals.append(_remote(out, dst.at[peer], sems, 7 * i + k, to))

    def start(self):
        for cp in self.local + self.sends:
            cp.start()

    def wait(self):
        for cp in self.arrivals:
            cp.wait_recv()
        for cp in self.sends:
            cp.wait_send()
        for cp in self.local:
            cp.wait()


def _scatter_sems(n):
    return [DMA_SEMS((7 * n,)), DMA_SEMS((7 * n,)), DMA_SEMS((n,))]


class _Gather:
    def __init__(self, items, sems):
        x, y, c = _mesh_pos()
        me, sibling = (x, y, c), (x, y, 1 - c)
        chips = [(1 - x, y), (x, 1 - y), (1 - x, 1 - y)]
        self.first, self.passed, self.from_chips, self.rest, self.local = [], [], [], [], []
        for i, (src, dst) in enumerate(items):
            def slot(p, dst=dst):
                return dst.at[4 * p[0] + 2 * p[1] + p[2]]

            def copy(k, block, to, own=False, i=i, src=src, slot=slot):
                return _remote(src if own else slot(block), slot(block), sems, 7 * i + k, to)

            self.local.append(pltpu.make_async_copy(src, slot(me), sems[2].at[i]))
            self.first.append(copy(0, me, sibling, own=True))
            self.first += [copy(1 + j, me, (*chip, c), own=True) for j, chip in enumerate(chips)]
            self.passed += [copy(4 + j, (*chip, c), sibling) for j, chip in enumerate(chips)]
            self.from_chips += [copy(1 + j, (*chip, c), me) for j, chip in enumerate(chips)]
            self.rest.append(copy(0, sibling, me))
            self.rest += [copy(4 + j, (*chip, 1 - c), me) for j, chip in enumerate(chips)]

    def start(self):
        for cp in self.local + self.first:
            cp.start()

    def forward(self):
        for arrived, onward in zip(self.from_chips, self.passed):
            arrived.wait_recv()
            onward.start()

    def finish(self):
        for cp in self.rest:
            cp.wait_recv()
        for cp in self.first + self.passed:
            cp.wait_send()
        for cp in self.local:
            cp.wait()


def _all_gather(shard, name):
    def body(x_ref, out_ref, *sems):
        g = _Gather([(x_ref, out_ref)], sems)
        g.start()
        g.forward()
        g.finish()

    return pl.pallas_call(
        body, out_shape=SDS((N_DEV,) + shard.shape, shard.dtype), in_specs=[HBM], out_specs=HBM,
        scratch_shapes=_scatter_sems(1), name=name)(shard)


def _scatter_call(srcs, slabbed, name):
    n = len(srcs)

    def body(*refs):
        sc = _Scatter(list(zip(refs[:n], refs[n:2 * n], slabbed)), refs[2 * n:])
        sc.start()
        sc.wait()

    shapes = [SDS(s.shape if sl else (N_DEV,) + s.shape, s.dtype) for s, sl in zip(srcs, slabbed)]
    return pl.pallas_call(body, out_shape=shapes, in_specs=[HBM] * n, out_specs=[HBM] * n,
                          scratch_shapes=_scatter_sems(n), name=name)(*srcs)


def _in_proj(x, g1, w_in_t, tm=512):
    T = x.shape[0]

    def body(x_ref, g_ref, w_ref, proj_ref, a_ref):
        xv = x_ref[...]
        a = (xv * _rstd(xv) * g_ref[...]).astype(BF)
        a_ref[...] = a
        proj_ref[...] = _dot_nt(a, w_ref[...])

    return pl.pallas_call(
        body, grid=(T // tm,),
        in_specs=[pl.BlockSpec((tm, D_MODEL), _row), pl.BlockSpec((1, D_MODEL), _fixed),
                  pl.BlockSpec((PROJ, D_MODEL), _fixed)],
        out_specs=[pl.BlockSpec((tm, PROJ), _row), pl.BlockSpec((tm, D_MODEL), _row)],
        out_shape=[SDS((T, PROJ), F32), SDS((T, D_MODEL), BF)],
        name="in_proj", compiler_params=_params(("arbitrary",), 48))(x, g1, w_in_t)


ATTN_GROUP = 16


def _attn_bias(sl_ref, bias_ref):
    qi = lax.broadcasted_iota(jnp.int32, (QBLK, QBLK), 0)
    kj = lax.broadcasted_iota(jnp.int32, (QBLK, QBLK), 1)
    step = qi - kj
    for di, d in enumerate(DILATIONS):
        for j in range(2):
            sl = sl_ref[0, j:j + 1, :]
            cur = jnp.where(step >= 0, -sl * (step * d).astype(F32), NEG)
            prev = jnp.where(step <= 0, -sl * ((step + QBLK) * d).astype(F32), NEG)
            rows = slice(j * QBLK, (j + 1) * QBLK)
            bias_ref[di * 2, rows, :QBLK] = prev
            bias_ref[di * 2, rows, QBLK:] = cur
            bias_ref[di * 2 + 1, rows, :QBLK] = jnp.full((QBLK, QBLK), NEG, F32)
            bias_ref[di * 2 + 1, rows, QBLK:] = cur


def _stack_heads(x, lo):
    return jnp.concatenate([jnp.where(lo, x, 0.0), jnp.where(lo, 0.0, x)], axis=0).astype(BF)


def _unstack_heads(x, lo):
    return jnp.where(lo, x[:QBLK], x[QBLK:])


def _attn_rows(start, d, blocks=1):
    if d == 1:
        return pl.ds(pl.multiple_of(start, QBLK), blocks * QBLK)
    return pl.ds(start, blocks * QBLK, stride=d)


def _attn_group_index(i, nblk, d, group):
    per = nblk // d // group
    r = i // per
    n0 = (i % per) * group
    start = r + (d * QBLK) * n0
    pstart = jnp.maximum(start - d * QBLK, r)
    return start, pstart, n0 == 0


def _attn_plan(nblk, d):
    group = min(ATTN_GROUP, nblk // d)
    return group, max(1, min(ATTN_GROUP // group, d))


def _attn_fwd(proj, slopes, to_gather):
    T = proj.shape[0]
    nblk = T // QBLK
    big = dict(pipeline_mode=pl.Buffered(1))
    n_g = len(to_gather)

    def body(q_ref, k_ref, v_ref, sl_ref, *rest):
        srcs, (o_ref, m_ref), dsts = rest[:n_g], rest[n_g:n_g + 2], rest[n_g + 2:2 * n_g + 2]
        sems, (l_ref, bias_ref) = rest[2 * n_g + 2:2 * n_g + 5], rest[2 * n_g + 5:]
        h = pl.program_id(0)

        @pl.when(h == 0)
        def _():
            _Gather(list(zip(srcs, dsts)), sems).start()

        @pl.when(h == 2)
        def _():
            _Gather(list(zip(srcs, dsts)), sems).forward()

        _attn_bias(sl_ref, bias_ref)
        lo = lax.broadcasted_iota(jnp.int32, (1, LANES), 1) < HEAD_DIM

        order = list(enumerate(DILATIONS))[::-1]
        for di, d in order:
            group, segs = _attn_plan(nblk, d)
            fresh, last = di == order[0][0], di == order[-1][0]

            def step(i, carry, segs=segs, **kw):
                for s in range(segs):
                    segment(i * segs + s, **kw)
                return carry

            def segment(i, d=d, di=di, group=group, fresh=fresh, last=last):
                start, pstart, first = _attn_group_index(i, nblk, d, group)
                prows = _attn_rows(pstart, d)
                k_prev, v_prev = k_ref[prows, :].astype(BF), v_ref[prows, :].astype(BF)
                for b in range(group):
                    out = _attn_rows(start + b * (d * QBLK), d)
                    k_own, v_own = k_ref[out, :].astype(BF), v_ref[out, :].astype(BF)
                    k2, v2 = jnp.concatenate([k_prev, k_own], axis=0), jnp.concatenate([v_prev, v_own], axis=0)
                    k_prev, v_prev = k_own, v_own
                    bias = bias_ref[di * 2 + first.astype(jnp.int32)] if b == 0 else bias_ref[di * 2]
                    s = _dot_nt(_stack_heads(q_ref[out, :] * (HEAD_DIM ** -0.5), lo), k2) + bias
                    m = jnp.max(s, axis=-1, keepdims=True)
                    pr = jnp.exp(s - m)
                    m_b = _unstack_heads(m, lo)
                    l_b = _unstack_heads(jnp.sum(pr, axis=-1, keepdims=True), lo)
                    o_b = _unstack_heads(_dot(pr.astype(BF), v2), lo)
                    if fresh:
                        m_ref[out, :] = m_b
                        l_ref[out, :] = l_b
                        o_ref[out, :] = o_b
                        continue
                    m_o = m_ref[out, :]
                    m_n = jnp.maximum(m_o, m_b)
                    wa, wb = jnp.exp(m_o - m_n), jnp.exp(m_b - m_n)
                    l_n = wa * l_ref[out, :] + wb * l_b
                    o_n = wa * o_ref[out, :] + wb * o_b
                    if last:
                        m_ref[out, :] = m_n + jnp.log(l_n)
                        o_ref[out, :] = o_n / l_n
                    else:
                        m_ref[out, :] = m_n
                        l_ref[out, :] = l_n
                        o_ref[out, :] = o_n

            lax.fori_loop(0, nblk // (group * segs), step, 0)

        @pl.when(h == pl.num_programs(0) - 1)
        def _():
            _Gather(list(zip(srcs, dsts)), sems).finish()

    col = lambda base: pl.BlockSpec((T, LANES), lambda h: (0, base + h), **big)
    tok = pl.BlockSpec((T, LANES), lambda h: (0, h))
    outs = pl.pallas_call(
        body, grid=(4,),
        in_specs=[col(0), col(4), col(8), pl.BlockSpec((1, 8, LANES), lambda h: (h, 0, 0))] + [HBM] * n_g,
        out_specs=[tok, tok] + [HBM] * n_g,
        out_shape=[SDS((T, ATTN_W), F32), SDS((T, ATTN_W), F32)]
        + [SDS((N_DEV,) + g.shape, g.dtype) for g in to_gather],
        scratch_shapes=_scatter_sems(n_g) + [pltpu.VMEM((T, LANES), F32), pltpu.VMEM((6, 2 * QBLK, 2 * QBLK), F32)],
        name="attn_fwd", compiler_params=_params(("arbitrary",), 48))(proj, proj, proj, slopes, *to_gather)
    return outs[0], outs[1], outs[2:]


def _sgu_norm(zv, ln_g, ln_b):
    gz, tz = _gelu(zv)
    mu = jnp.mean(gz, axis=-1, keepdims=True)
    xc = gz - mu
    rs = lax.rsqrt(jnp.mean(xc * xc, axis=-1, keepdims=True) + EPS)
    xhat = xc * rs
    return xhat * ln_g + ln_b, xhat, rs, tz


def _causal(w):
    i = lax.broadcasted_iota(jnp.int32, (CHUNK, CHUNK), 0)
    j = lax.broadcasted_iota(jnp.int32, (CHUNK, CHUNK), 1)
    return jnp.where(i >= j, w, 0.0)


def _sgu_fwd(proj, ln_g, ln_b, w_s, b_st, tm=512):
    T = proj.shape[0]

    def body(u_ref, z_ref, g_ref, b_ref, w_ref, bs_ref, out_ref):
        for g in range(N_GROUPS):
            wm = _causal(w_ref[g]).astype(BF)
            cols = slice(g * GROUP_DIM, (g + 1) * GROUP_DIM)
            for c in range(tm // CHUNK):
                rows = slice(c * CHUNK, (c + 1) * CHUNK)
                zn, _, _, _ = _sgu_norm(z_ref[rows, cols], g_ref[...], b_ref[...])
                mixed = _dot(wm, zn.astype(BF)) + bs_ref[:, g:g + 1]
                gu, _ = _gelu(u_ref[rows, cols])
                out_ref[rows, cols] = gu * mixed

    return pl.pallas_call(
        body, grid=(T // tm,),
        in_specs=[pl.BlockSpec((tm, SGU_W), lambda i: (i, 3)), pl.BlockSpec((tm, SGU_W), lambda i: (i, 4)),
                  pl.BlockSpec((1, GROUP_DIM), _fixed), pl.BlockSpec((1, GROUP_DIM), _fixed),
                  pl.BlockSpec((N_GROUPS, CHUNK, CHUNK), lambda i: (0, 0, 0)), pl.BlockSpec((CHUNK, LANES), _fixed)],
        out_specs=pl.BlockSpec((tm, SGU_W), _row),
        out_shape=SDS((T, SGU_W), F32),
        name="sgu_fwd", compiler_params=_params(("arbitrary",), 32))(proj, proj, ln_g, ln_b, w_s, b_st)


def _out_proj(attn, sgu, x, g_a, g_s, w_out, g_pm, g_pf, tm=512):
    T = x.shape[0]

    def body(a_ref, s_ref, x_ref, ga_ref, gs_ref, w_ref, gpm_ref, gpf_ref, grp_ref, mixed_ref, h1_ref, f_ref):
        av, sv = a_ref[...], s_ref[...]
        an = (av * _rstd(av) * ga_ref[...]).astype(BF)
        sn = (sv * _rstd(sv) * gs_ref[...]).astype(BF)
        grp_ref[:, :ATTN_W] = an
        grp_ref[:, ATTN_W:] = sn
        mixed = _dot(an, w_ref[:ATTN_W, :]) + _dot(sn, w_ref[ATTN_W:, :])
        mixed_ref[...] = mixed
        h1 = x_ref[...] + mixed * _rstd(mixed) * gpm_ref[...]
        h1_ref[...] = h1
        f_ref[...] = (h1 * _rstd(h1) * gpf_ref[...]).astype(BF)

    tok = lambda w: pl.BlockSpec((tm, w), _row)
    vec = lambda w: pl.BlockSpec((1, w), _fixed)
    return pl.pallas_call(
        body, grid=(T // tm,),
        in_specs=[tok(ATTN_W), tok(SGU_W), tok(D_MODEL), vec(ATTN_W), vec(SGU_W),
                  pl.BlockSpec((D_MODEL, D_MODEL), _fixed), vec(D_MODEL), vec(D_MODEL)],
        out_specs=[tok(D_MODEL)] * 4,
        out_shape=[SDS((T, D_MODEL), BF), SDS((T, D_MODEL), F32), SDS((T, D_MODEL), F32), SDS((T, D_MODEL), BF)],
        name="out_proj", compiler_params=_params(("arbitrary",), 48))(attn, sgu, x, g_a, g_s, w_out, g_pm, g_pf)


FF_TILE = 1408
FF_TILES = D_FF // FF_TILE
FF_CHUNK = 256


def _gate_up(f, w_gu_t, tm=512):
    T = f.shape[0]
    tn = FF_TILE

    def body(f_ref, wg_ref, wu_ref, g_ref, u_ref, act_ref):
        fv = f_ref[...]
        g = _dot_nt(fv, wg_ref[...])
        u = _dot_nt(fv, wu_ref[...])
        g_ref[...] = g.astype(BF)
        u_ref[...] = u.astype(BF)
        act_ref[...] = (g * _sigmoid(g) * u).astype(BF)

    ospec = pl.BlockSpec((tm, tn), lambda j, i: (i, j))
    return pl.pallas_call(
        body, grid=(FF_TILES, T // tm),
        in_specs=[pl.BlockSpec((tm, D_MODEL), lambda j, i: (i, 0)), pl.BlockSpec((tn, D_MODEL), lambda j, i: (j, 0)),
                  pl.BlockSpec((tn, D_MODEL), lambda j, i: (j + FF_TILES, 0))],
        out_specs=[ospec] * 3, out_shape=[SDS((T, D_FF), BF)] * 3,
        name="gate_up", compiler_params=_params(("arbitrary", "arbitrary"), 40))(f, w_gu_t, w_gu_t)


def _down_proj(act, w_down, h1, g_pff, tm=512):
    T = act.shape[0]

    def body(a_ref, w_ref, h1_ref, g_ref, y_ref, h2_ref):
        y = _dot(a_ref[...], w_ref[...])
        y_ref[...] = y
        h2_ref[...] = h1_ref[...] + y * _rstd(y) * g_ref[...]

    return pl.pallas_call(
        body, grid=(T // tm,),
        in_specs=[pl.BlockSpec((tm, D_FF), _row), pl.BlockSpec((D_FF, D_MODEL), _fixed),
                  pl.BlockSpec((tm, D_MODEL), _row), pl.BlockSpec((1, D_MODEL), _fixed)],
        out_specs=[pl.BlockSpec((tm, D_MODEL), _row)] * 2,
        out_shape=[SDS((T, D_MODEL), F32)] * 2,
        name="down_proj", compiler_params=_params(("arbitrary",), 48))(act, w_down, h1, g_pff)


def _pe_loss_and_bwd(h2, p, target, y, w_peg, b_peg, w_pep_t, g_pff, tm=512):
    T = h2.shape[0]

    def body(h2_ref, p_ref, t_ref, y_ref, wg_ref, b_ref, wp_ref, g_ref,
             dh2_ref, dy_ref, dpp_ref, dpre_ref, h2b_ref, pb_ref, loss_ref, db_ref, dg_ref):
        _acc_init(pl.program_id(0), loss_ref, db_ref, dg_ref)
        h2v = h2_ref[...]
        h2b = h2v.astype(BF)
        pb = p_ref[...].astype(BF)
        h2b_ref[...] = h2b
        pb_ref[...] = pb
        gate = _sigmoid(_dot(h2b, wg_ref[...]) + b_ref[...])
        pp = _dot_nt(pb, wp_ref[...])
        diff = h2v + gate * pp - t_ref[...]
        loss_ref[...] += _colsum(diff * diff)
        dh3 = diff * (1.0 / D_MODEL)
        dpp_ref[...] = (dh3 * gate).astype(BF)
        dpre = dh3 * pp * (gate * (1.0 - gate))
        dpre_b = dpre.astype(BF)
        dpre_ref[...] = dpre_b
        db_ref[...] += _colsum(dpre)
        dh2 = dh3 + _dot_nt(dpre_b, wg_ref[...])
        dh2_ref[...] = dh2
        dy, dg = _rms_bwd(dh2, y_ref[...], g_ref[...])
        dy_ref[...] = dy.astype(BF)
        dg_ref[...] += _colsum(dg)

    tok = lambda w: pl.BlockSpec((tm, w), _row)
    vec = pl.BlockSpec((1, D_MODEL), _fixed)
    return pl.pallas_call(
        body, grid=(T // tm,),
        in_specs=[tok(D_MODEL), tok(PLE_DIM), tok(D_MODEL), tok(D_MODEL), pl.BlockSpec((D_MODEL, D_MODEL), _fixed), vec,
                  pl.BlockSpec((D_MODEL, PLE_DIM), _fixed), vec],
        out_specs=[tok(D_MODEL)] * 5 + [tok(PLE_DIM), vec, vec, vec],
        out_shape=[SDS((T, D_MODEL), F32)] + [SDS((T, D_MODEL), BF)] * 4
        + [SDS((T, PLE_DIM), BF)] + [SDS((1, D_MODEL), F32)] * 3,
        name="pe_loss_and_bwd", compiler_params=_params(("arbitrary",), 56))(
            h2, p, target, y, w_peg, b_peg, w_pep_t, g_pff)


def _weight_grad(a, dy, name, into=None, row_tile=0, rows=None, out_dtype=F32, tk=512):
    n = dy.shape[1]
    tn = min(n, 1024)
    T, ka = a.shape
    tka = FF_TILE if ka == D_FF else min(ka, 1024)
    rows = ka if rows is None else rows
    direct = out_dtype == F32

    def body(a_ref, dy_ref, *rest):
        out_ref, acc_ref = (rest[-1], rest[-1]) if direct else rest[-2:]
        _acc_init(pl.program_id(2), acc_ref)
        acc_ref[...] += _dot_tn(a_ref[...].astype(BF), dy_ref[...].astype(BF))
        if not direct:
            @pl.when(pl.program_id(2) == pl.num_programs(2) - 1)
            def _():
                out_ref[...] = acc_ref[...].astype(out_dtype)

    carried = [] if into is None else [into]
    return pl.pallas_call(
        body, grid=(ka // tka, n // tn, T // tk),
        in_specs=[pl.BlockSpec((tk, tka), lambda i, j, k: (k, i)), pl.BlockSpec((tk, tn), lambda i, j, k: (k, j))]
        + [HBM] * len(carried),
        out_specs=pl.BlockSpec((tka, tn), lambda i, j, k: (i + row_tile, j)),
        out_shape=SDS((rows, n), out_dtype), input_output_aliases={2: 0} if carried else {},
        scratch_shapes=[] if direct else [pltpu.VMEM((tka, tn), F32)],
        name="grad_" + name, compiler_params=_params(("arbitrary",) * 3, 40))(a, dy, *carried)


def _grad_w_in(dparts, a, tk=512):
    T = a.shape[0]

    def body(*refs):
        d_refs, a_ref, out_ref, acc_ref = refs[:len(dparts)], refs[-3], refs[-2], refs[-1]
        k = pl.program_id(0)
        _acc_init(k, acc_ref)
        cols = [r[part].astype(BF) for r in d_refs for part in range(r.shape[0])]
        acc_ref[...] += _dot_tn(jnp.concatenate(cols, axis=1), a_ref[...])

        @pl.when(k == pl.num_programs(0) - 1)
        def _():
            out_ref[...] = acc_ref[...].astype(BF)

    return pl.pallas_call(
        body, grid=(T // tk,),
        in_specs=[pl.BlockSpec((d.shape[0], tk, d.shape[2]), lambda k: (0, k, 0)) for d in dparts]
        + [pl.BlockSpec((tk, D_MODEL), lambda k: (k, 0))],
        out_specs=pl.BlockSpec((PROJ, D_MODEL), lambda k: (0, 0)),
        out_shape=SDS((PROJ, D_MODEL), BF), scratch_shapes=[pltpu.VMEM((PROJ, D_MODEL), F32)],
        name="grad_w_in", compiler_params=_params(("arbitrary",), 48))(*dparts, a)


def _down_bwd(dy, w_down, g, u, to_send, tm=512):
    T = dy.shape[0]
    n_s = len(to_send)

    def body(dy_ref, w_ref, g_ref, u_ref, *rest):
        srcs, (dg_ref, du_ref), dsts, sems = rest[:n_s], rest[n_s:n_s + 2], rest[n_s + 2:2 * n_s + 2], rest[2 * n_s + 2:]
        i = pl.program_id(0)
        items = list(zip(srcs, dsts, [True] * n_s))

        @pl.when(i == 0)
        def _():
            _Scatter(items, sems).start()

        dyv = dy_ref[...]
        for c in range(D_FF // FF_CHUNK):
            cols = slice(c * FF_CHUNK, (c + 1) * FF_CHUNK)
            dact = _dot_nt(dyv, w_ref[cols, :])
            gv = g_ref[:, cols].astype(F32)
            uv = u_ref[:, cols].astype(F32)
            s = _sigmoid(gv)
            ds = dact * s
            dg_ref[:, cols] = (ds * uv * (1.0 + gv * (1.0 - s))).astype(BF)
            du_ref[:, cols] = (ds * gv).astype(BF)

        @pl.when(i == pl.num_programs(0) - 1)
        def _():
            _Scatter(items, sems).wait()

    tile = pl.BlockSpec((tm, D_FF), _row)
    outs = pl.pallas_call(
        body, grid=(T // tm,),
        in_specs=[pl.BlockSpec((tm, D_MODEL), _row),
                  pl.BlockSpec((D_FF, D_MODEL), _fixed, pipeline_mode=pl.Buffered(1)), tile, tile] + [HBM] * n_s,
        out_specs=[tile, tile] + [HBM] * n_s,
        out_shape=[SDS((T, D_FF), BF)] * 2 + [SDS(s.shape, s.dtype) for s in to_send],
        scratch_shapes=_scatter_sems(n_s),
        name="down_bwd", compiler_params=_params(("arbitrary",), 48))(dy, w_down, g, u, *to_send)
    return outs[0], outs[1], outs[2:]


def _ffn_in_bwd(dg, du, w_gu_t, h1, dh2, mixed, g_pf, g_pm, to_send, tm=512):
    T = h1.shape[0]
    n_s = len(to_send)

    def body(dg_ref, du_ref, wg_ref, wu_ref, h1_ref, dh2_ref, mx_ref, gpf_ref, gpm_ref, *rest):
        srcs, outs, dsts, sems = rest[:n_s], rest[n_s:n_s + 4], rest[n_s + 4:2 * n_s + 4], rest[2 * n_s + 4:]
        dh1_ref, dmx_ref, dgpf_ref, dgpm_ref = outs
        i = pl.program_id(0)
        items = list(zip(srcs, dsts, [True] * n_s))
        _acc_init(i, dgpf_ref, dgpm_ref)

        @pl.when(i == 0)
        def _():
            _Scatter(items, sems).start()

        df = _dot(dg_ref[...], wg_ref[...]) + _dot(du_ref[...], wu_ref[...])
        dx, dgf = _rms_bwd(df, h1_ref[...], gpf_ref[...])
        dh1 = dh2_ref[...] + dx
        dh1_ref[...] = dh1
        dmx, dgm = _rms_bwd(dh1, mx_ref[...], gpm_ref[...])
        dmx_ref[...] = dmx.astype(BF)
        dgpf_ref[...] += _colsum(dgf)
        dgpm_ref[...] += _colsum(dgm)

        @pl.when(i == pl.num_programs(0) - 1)
        def _():
            _Scatter(items, sems).wait()

    tok = lambda w: pl.BlockSpec((tm, w), _row)
    vec = pl.BlockSpec((1, D_MODEL), _fixed)
    outs = pl.pallas_call(
        body, grid=(T // tm,),
        in_specs=[tok(D_FF), tok(D_FF), pl.BlockSpec((D_FF, D_MODEL), lambda i: (0, 0), pipeline_mode=pl.Buffered(1)),
                  pl.BlockSpec((D_FF, D_MODEL), lambda i: (1, 0), pipeline_mode=pl.Buffered(1)),
                  tok(D_MODEL), tok(D_MODEL), tok(D_MODEL), vec, vec]
        + [HBM] * n_s,
        out_specs=[tok(D_MODEL), tok(D_MODEL), vec, vec] + [HBM] * n_s,
        out_shape=[SDS((T, D_MODEL), F32), SDS((T, D_MODEL), BF), SDS((1, D_MODEL), F32), SDS((1, D_MODEL), F32)]
        + [SDS(s.shape, s.dtype) for s in to_send],
        scratch_shapes=_scatter_sems(n_s),
        name="ffn_in_bwd", compiler_params=_params(("arbitrary",), 56))(
            dg, du, w_gu_t, w_gu_t, h1, dh2, mixed, g_pf, g_pm, *to_send)
    return outs[0], outs[1], outs[2], outs[3], outs[4:]


def _out_bwd(dmx, w_out, attn, sgu, g_a, g_s, tm=512):
    T = attn.shape[0]

    def body(dm_ref, w_ref, a_ref, s_ref, ga_ref, gs_ref, da_ref, ds_ref, dga_ref, dgs_ref):
        _acc_init(pl.program_id(0), dga_ref, dgs_ref)
        dgr = _dot_nt(dm_ref[...], w_ref[...])
        da, dga = _rms_bwd(dgr[:, :ATTN_W], a_ref[...], ga_ref[...])
        ds, dgs = _rms_bwd(dgr[:, ATTN_W:], s_ref[...], gs_ref[...])
        da_ref[...] = da
        ds_ref[...] = ds
        dga_ref[...] += _colsum(dga)
        dgs_ref[...] += _colsum(dgs)

    tok = lambda w: pl.BlockSpec((tm, w), _row)
    vec = lambda w: pl.BlockSpec((1, w), _fixed)
    return pl.pallas_call(
        body, grid=(T // tm,),
        in_specs=[tok(D_MODEL), pl.BlockSpec((D_MODEL, D_MODEL), _fixed), tok(ATTN_W), tok(SGU_W), vec(ATTN_W), vec(SGU_W)],
        out_specs=[tok(ATTN_W), tok(SGU_W), vec(ATTN_W), vec(SGU_W)],
        out_shape=[SDS((T, ATTN_W), F32), SDS((T, SGU_W), F32), SDS((1, ATTN_W), F32), SDS((1, SGU_W), F32)],
        name="out_bwd", compiler_params=_params(("arbitrary",), 40))(dmx, w_out, attn, sgu, g_a, g_s)


def _sgu_bwd(proj, dsgu, ln_g, ln_b, w_s, b_st, tm=512):
    T = proj.shape[0]

    def body(u_ref, z_ref, ds_ref, g_ref, b_ref, w_ref, bs_ref,
             duz_ref, dw_ref, dbs_ref, dlg_ref, dlb_ref, dbacc_ref):
        du_ref, dz_ref = duz_ref.at[0], duz_ref.at[1]
        step = pl.program_id(0)
        _acc_init(step, dw_ref, dbs_ref, dlg_ref, dlb_ref, dbacc_ref)
        lng, lnb = g_ref[...], b_ref[...]
        for g in range(N_GROUPS):
            wm = _causal(w_ref[g]).astype(BF)
            cols = slice(g * GROUP_DIM, (g + 1) * GROUP_DIM)
            for c in range(tm // CHUNK):
                rows = slice(c * CHUNK, (c + 1) * CHUNK)
                zv, uv, dout = z_ref[rows, cols], u_ref[rows, cols], ds_ref[rows, cols]
                zn, xhat, rs, tz = _sgu_norm(zv, lng, lnb)
                znb = zn.astype(BF)
                mixed = _dot(wm, znb) + bs_ref[:, g:g + 1]
                gu, tu = _gelu(uv)
                du_ref[rows, cols] = (dout * mixed * _gelu_grad(uv, tu)).astype(BF)
                dmix = dout * gu
                dmb = dmix.astype(BF)
                dw_ref[g] += _causal(_dot_nt(dmb, znb))
                dbacc_ref[g] += dmix
                dzn = _dot_tn(wm, dmb)
                dlg_ref[...] += _colsum(dzn * xhat)
                dlb_ref[...] += _colsum(dzn)
                dxh = dzn * lng
                dgz = rs * (dxh - jnp.mean(dxh, axis=-1, keepdims=True)
                            - xhat * jnp.mean(dxh * xhat, axis=-1, keepdims=True))
                dz_ref[rows, cols] = (dgz * _gelu_grad(zv, tz)).astype(BF)

        @pl.when(step == pl.num_programs(0) - 1)
        def _():
            lane = lax.broadcasted_iota(jnp.int32, (CHUNK, LANES), 1)
            acc = jnp.zeros((CHUNK, LANES), F32)
            for g in range(N_GROUPS):
                acc = jnp.where(lane == g, jnp.sum(dbacc_ref[g], axis=-1, keepdims=True), acc)
            dbs_ref[...] = acc

    tok = pl.BlockSpec((tm, SGU_W), _row)
    vec = pl.BlockSpec((1, GROUP_DIM), _fixed)
    wsp = pl.BlockSpec((N_GROUPS, CHUNK, CHUNK), lambda i: (0, 0, 0))
    sq = pl.BlockSpec((CHUNK, LANES), _fixed)
    return pl.pallas_call(
        body, grid=(T // tm,),
        in_specs=[pl.BlockSpec((tm, SGU_W), lambda i: (i, 3)), pl.BlockSpec((tm, SGU_W), lambda i: (i, 4)), tok,
                  vec, vec, wsp, sq],
        out_specs=[pl.BlockSpec((2, tm, SGU_W), lambda i: (0, i, 0)), wsp, sq, vec, vec],
        out_shape=[SDS((2, T, SGU_W), BF), SDS((N_GROUPS, CHUNK, CHUNK), F32),
                   SDS((CHUNK, LANES), F32), SDS((1, GROUP_DIM), F32), SDS((1, GROUP_DIM), F32)],
        scratch_shapes=[pltpu.VMEM((N_GROUPS, CHUNK, LANES), F32)],
        name="sgu_bwd", compiler_params=_params(("arbitrary",), 32))(proj, proj, dsgu, ln_g, ln_b, w_s, b_st)


def _attn_bwd(proj, do, o, lse, slopes, to_send, slabbed):
    T = proj.shape[0]
    nblk = T // QBLK
    big = dict(pipeline_mode=pl.Buffered(1))
    n_s = len(to_send)

    def body(q_ref, k_ref, v_ref, do_ref, o_ref, l_ref, sl_ref, *rest):
        srcs, d_ref, dsts = rest[:n_s], rest[n_s], rest[n_s + 1:2 * n_s + 1]
        sems, bias_ref = rest[2 * n_s + 1:2 * n_s + 4], rest[2 * n_s + 4]
        dq_ref, dk_ref, dv_ref = d_ref.at[0], d_ref.at[1], d_ref.at[2]
        h = pl.program_id(0)
        items = list(zip(srcs, dsts, slabbed))

        @pl.when(h == 0)
        def _():
            _Scatter(items, sems).start()

        _attn_bias(sl_ref, bias_ref)
        lo = lax.broadcasted_iota(jnp.int32, (1, LANES), 1) < HEAD_DIM
        scale = HEAD_DIM ** -0.5
        d_ref[...] = jnp.zeros_like(d_ref)

        for di, d in enumerate(DILATIONS):
            group, segs = _attn_plan(nblk, d)

            def step(i, carry, segs=segs, **kw):
                for s in range(segs):
                    segment(i * segs + s, **kw)
                return carry

            def segment(i, d=d, di=di, group=group):
                start, pstart, first = _attn_group_index(i, nblk, d, group)
                rows, prows = _attn_rows(start, d, group), _attn_rows(pstart, d)
                q = q_ref[rows, :] * scale
                k = jnp.concatenate([k_ref[prows, :], k_ref[rows, :]], axis=0).astype(BF)
                v = jnp.concatenate([v_ref[prows, :], v_ref[rows, :]], axis=0).astype(BF)
                dov = do_ref[rows, :]
                prod = dov * o_ref[rows, :]
                lse_g = l_ref[rows, :]
                masks = [lo, ~lo]
                qm = [jnp.where(masks[j], q, 0.0).astype(BF) for j in range(2)]
                dom = [jnp.where(masks[j], dov, 0.0).astype(BF) for j in range(2)]
                for b in range(group):
                    qb = slice(b * QBLK, (b + 1) * QBLK)
                    kb = slice(b * QBLK, (b + 2) * QBLK)
                    which = di * 2 + first.astype(jnp.int32) if b == 0 else di * 2
                    dq_parts, prs, dss = [], [], []
                    for j in range(2):
                        bias = bias_ref[which, j * QBLK:(j + 1) * QBLK, :]
                        delta = jnp.sum(jnp.where(masks[j], prod[qb], 0.0), axis=-1, keepdims=True)
                        lj = lse_g[qb, j * HEAD_DIM:j * HEAD_DIM + 1]
                        pr = jnp.exp(_dot_nt(qm[j][qb], k[kb]) + bias - lj)
                        ds = (pr * (_dot_nt(dom[j][qb], v[kb]) - delta)).astype(BF)
                        dq_parts.append(_dot(ds, k[kb]))
                        prs.append(pr.astype(BF))
                        dss.append(ds)
                    dk_b = _dot_tn(jnp.concatenate(dss, axis=0), jnp.concatenate([qm[0][qb], qm[1][qb]], axis=0))
                    dv_b = _dot_tn(jnp.concatenate(prs, axis=0), jnp.concatenate([dom[0][qb], dom[1][qb]], axis=0))
                    own = _attn_rows(start + b * (d * QBLK), d)
                    dq_ref[own, :] += jnp.where(lo, dq_parts[0], dq_parts[1]) * scale
                    if b == 0:
                        dk_ref[prows, :] += dk_b[:QBLK]
                        dv_ref[prows, :] += dv_b[:QBLK]
                        dk_ref[own, :] += dk_b[QBLK:]
                        dv_ref[own, :] += dv_b[QBLK:]
                    else:
                        two = _attn_rows(start + (b - 1) * (d * QBLK), d, 2)
                        dk_ref[two, :] += dk_b
                        dv_ref[two, :] += dv_b

            lax.fori_loop(0, nblk // (group * segs), step, 0)

        @pl.when(h == pl.num_programs(0) - 1)
        def _():
            _Scatter(items, sems).wait()

    col = lambda base: pl.BlockSpec((T, LANES), lambda h: (0, base + h), **big)
    outs = pl.pallas_call(
        body, grid=(4,),
        in_specs=[col(0), col(4), col(8), col(0), col(0), col(0), pl.BlockSpec((1, 8, LANES), lambda h: (h, 0, 0))]
        + [HBM] * n_s,
        out_specs=[pl.BlockSpec((3, T, LANES), lambda h: (0, 0, h))] + [HBM] * n_s,
        out_shape=[SDS((3, T, ATTN_W), F32)]
        + [SDS(s.shape if sl else (N_DEV,) + s.shape, s.dtype) for s, sl in zip(to_send, slabbed)],
        scratch_shapes=_scatter_sems(n_s) + [pltpu.VMEM((6, 2 * QBLK, 2 * QBLK), F32)],
        name="attn_bwd", compiler_params=_params(("arbitrary",), 60))(proj, proj, proj, do, o, lse, slopes, *to_send)
    return outs[0], outs[1:]


def _in_bwd(dparts, w_in_t, x, dh1, g1, to_send, slabbed, tm=512):
    T = x.shape[0]
    n = len(dparts)
    n_s = len(to_send)
    w = ATTN_W

    def body(*refs):
        d_refs, (w_ref, x_ref, dh1_ref, g_ref), rest = refs[:n], refs[n:n + 4], refs[n + 4:]
        srcs, (dx_ref, dg_ref), dsts, sems = rest[:n_s], rest[n_s:n_s + 2], rest[n_s + 2:2 * n_s + 2], rest[2 * n_s + 2:]
        step = pl.program_id(0)
        items = list(zip(srcs, dsts, slabbed))
        _acc_init(step, dg_ref)

        @pl.when(step == 0)
        def _():
            _Scatter(items, sems).start()

        da = None
        col = 0
        for r in d_refs:
            for part in range(r.shape[0]):
                t = _dot(r[part].astype(BF), w_ref[col * w:(col + 1) * w, :])
                da = t if da is None else da + t
                col += 1
        dx, dg = _rms_bwd(da, x_ref[...], g_ref[...])
        dx_ref[...] = dh1_ref[...] + dx
        dg_ref[...] += _colsum(dg)

        @pl.when(step == pl.num_programs(0) - 1)
        def _():
            _Scatter(items, sems).wait()

    tok = lambda c: pl.BlockSpec((tm, c), _row)
    vec = pl.BlockSpec((1, D_MODEL), _fixed)
    outs = pl.pallas_call(
        body, grid=(T // tm,),
        in_specs=[pl.BlockSpec((d.shape[0], tm, w), lambda i: (0, i, 0)) for d in dparts]
        + [pl.BlockSpec((PROJ, D_MODEL), _fixed), tok(D_MODEL), tok(D_MODEL), vec] + [HBM] * n_s,
        out_specs=[tok(D_MODEL), vec] + [HBM] * n_s,
        out_shape=[SDS((T, D_MODEL), F32), SDS((1, D_MODEL), F32)]
        + [SDS(s.shape if sl else (N_DEV,) + s.shape, s.dtype) for s, sl in zip(to_send, slabbed)],
        scratch_shapes=_scatter_sems(n_s),
        name="in_bwd", compiler_params=_params(("arbitrary",), 52))(*dparts, w_in_t, x, dh1, g1, *to_send)
    return outs[0], outs[1], outs[2:]


def _sum_parts(p_ref):
    g = p_ref[0].astype(F32)
    for s in range(1, N_DEV):
        g = g + p_ref[s].astype(F32)
    return g


def _adamw_math(g, w, m, v):
    nm = ADAM_B1 * m + (1.0 - ADAM_B1) * g
    nv = ADAM_B2 * v + (1.0 - ADAM_B2) * (g * g)
    m_hat = nm / (1.0 - ADAM_B1 ** ADAM_STEP)
    v_hat = nv / (1.0 - ADAM_B2 ** ADAM_STEP)
    return -ADAM_LR * (m_hat / (jnp.sqrt(v_hat) + ADAM_EPS) + ADAM_WD * w), nm, nv


def _row_tile(rows):
    for t in (256, 176, 128, 80):
        if rows % t == 0:
            return t
    raise ValueError(rows)


def _reduce_adamw(parts, w, m, v, name):
    rows, width = w.shape
    tr = _row_tile(rows)

    def body(p_ref, w_ref, m_ref, v_ref, g_ref, d_ref, nm_ref, nv_ref):
        g = _sum_parts(p_ref)
        g_ref[...] = g
        d_ref[...], nm_ref[...], nv_ref[...] = _adamw_math(g, w_ref[...], m_ref[...], v_ref[...])

    blk = pl.BlockSpec((tr, width), _row)
    return pl.pallas_call(
        body, grid=(rows // tr,),
        in_specs=[pl.BlockSpec((N_DEV, tr, width), lambda i: (0, i, 0)), blk, blk, blk],
        out_specs=[blk] * 4, out_shape=[SDS((rows, width), F32)] * 4,
        name="adamw_" + name, compiler_params=_params(("arbitrary",), 32))(parts, w, m, v)


def _reduce(parts, name):
    _, rows, width = parts.shape
    tr = _row_tile(rows)

    def body(p_ref, g_ref):
        g_ref[...] = _sum_parts(p_ref)

    return pl.pallas_call(
        body, grid=(rows // tr,),
        in_specs=[pl.BlockSpec((N_DEV, tr, width), lambda i: (0, i, 0))],
        out_specs=pl.BlockSpec((tr, width), _row), out_shape=SDS((rows, width), F32),
        name="sum_" + name, compiler_params=_params(("arbitrary",), 32))(parts)


def _adamw(g, w, m, v, name):
    rows, width = w.shape
    tr = _row_tile(rows)

    def body(g_ref, w_ref, m_ref, v_ref, d_ref, nm_ref, nv_ref):
        d_ref[...], nm_ref[...], nv_ref[...] = _adamw_math(g_ref[...], w_ref[...], m_ref[...], v_ref[...])

    blk = pl.BlockSpec((tr, width), _row)
    return pl.pallas_call(
        body, grid=(rows // tr,), in_specs=[blk] * 4, out_specs=[blk] * 3, out_shape=[SDS((rows, width), F32)] * 3,
        name="adamw_" + name, compiler_params=_params(("arbitrary",), 32))(g, w, m, v)


SMALL = ("w_spatial", "ln_pre_mix", "ln_post_mix", "ln_pre_ffn", "ln_post_ffn", "b_pe_gate",
         "attn_out_norm", "sgu_out_norm", "b_spatial", "sgu_ln_g", "sgu_ln_b")
SMALL_GROUPS = ((128, ("w_spatial", "b_spatial", "sgu_ln_g", "sgu_ln_b")),
                (512, ("attn_out_norm", "sgu_out_norm")),
                (1024, ("ln_post_mix", "ln_pre_ffn", "ln_post_ffn", "b_pe_gate")))
SMALL_LATE = "ln_pre_mix"
SMALL_SIZE = dict(w_spatial=N_GROUPS * CHUNK * CHUNK, b_spatial=N_GROUPS * CHUNK, sgu_ln_g=GROUP_DIM, sgu_ln_b=GROUP_DIM,
                  attn_out_norm=ATTN_W, sgu_out_norm=SGU_W, ln_pre_mix=D_MODEL, ln_post_mix=D_MODEL, ln_pre_ffn=D_MODEL,
                  ln_post_ffn=D_MODEL, b_pe_gate=D_MODEL)
SUBLANES = 8
ROW_SHARDED = ("w_out", "w_down", "w_pe_gate")
COL_SHARDED = ("w_in", "w_gate_up", "w_pe_proj")
WEIGHTS = ("ln_pre_mix", "w_in", "sgu_ln_g", "sgu_ln_b", "w_spatial", "b_spatial", "attn_out_norm", "sgu_out_norm",
           "w_out", "ln_post_mix", "ln_pre_ffn", "w_gate_up", "w_down", "ln_post_ffn", "w_pe_gate", "b_pe_gate",
           "w_pe_proj")


def _group_rows(width, names, extra=0):
    rows = sum(SMALL_SIZE[n] // width for n in names) + extra
    return -(-rows // SUBLANES) * SUBLANES


def _pack_small_grads(gs, loss_term):
    packed = []
    for width, names in SMALL_GROUPS:
        rows = [gs[n].reshape(-1, width) for n in names]
        extra = int(width == D_MODEL)
        if extra:
            rows.append(jnp.full((1, width), loss_term, F32))
        used = sum(r.shape[0] for r in rows)
        rows.append(jnp.zeros((_group_rows(width, names, extra) - used, width), F32))
        packed.append(jnp.concatenate(rows, axis=0))
    return packed


def _small_adamw(arrived, arrived_late, w, m, v):
    names = [n for _, ns in SMALL_GROUPS for n in ns] + [SMALL_LATE]
    n_groups = len(SMALL_GROUPS)

    def body(*refs):
        group_refs, late_ref = refs[:n_groups], refs[n_groups]
        state = refs[n_groups + 1:n_groups + 1 + 3 * len(names)]
        outs = refs[n_groups + 1 + 3 * len(names):]
        sums = [_sum_parts(r) for r in group_refs]

        def update(name, g):
            i = names.index(name)
            w_ref, m_ref, v_ref = state[3 * i:3 * i + 3]
            delta, nm, nv = _adamw_math(g, w_ref[...].reshape(g.shape), m_ref[...].reshape(g.shape),
                                        v_ref[...].reshape(g.shape))
            for o_ref, val in zip(outs[4 * i:4 * i + 4], (g, delta, nm, nv)):
                o_ref[...] = val.reshape(o_ref.shape)

        for (width, group), total in zip(SMALL_GROUPS, sums):
            row = 0
            for name in group:
                rows = SMALL_SIZE[name] // width
                update(name, total[row:row + rows, :])
                row += rows
            if width == D_MODEL:
                outs[-1][...] = total[row:row + 1, :LANES]
        update(SMALL_LATE, _sum_parts(late_ref)[:1, :])

    state = [t[n] for n in names for t in (w, m, v)]
    plain = jax.ShapeDtypeStruct
    out_shape = [plain(w[n].shape, F32) for n in names for _ in range(4)] + [plain((1, LANES), F32)]
    outs = pl.pallas_call(body, out_shape=out_shape, name="adamw_small",
                          compiler_params=pltpu.CompilerParams(vmem_limit_bytes=32 * MIB))(*arrived, arrived_late, *state)
    return {n: outs[4 * i:4 * i + 4] for i, n in enumerate(names)}, outs[-1]


def _slabs(full):
    return full.reshape(N_DEV, full.shape[0] // N_DEV, full.shape[1])


def kernel(x, p, ln_pre_mix, w_in, sgu_ln_g, sgu_ln_b, w_spatial, b_spatial, attn_out_norm, sgu_out_norm, w_out, ln_post_mix, ln_pre_ffn, w_gate_up, w_down, ln_post_ffn, w_pe_gate, b_pe_gate, w_pe_proj, loss_target, m_ln_pre_mix, m_w_in, m_sgu_ln_g, m_sgu_ln_b, m_w_spatial, m_b_spatial, m_attn_out_norm, m_sgu_out_norm, m_w_out, m_ln_post_mix, m_ln_pre_ffn, m_w_gate_up, m_w_down, m_ln_post_ffn, m_w_pe_gate, m_b_pe_gate, m_w_pe_proj, v_ln_pre_mix, v_w_in, v_sgu_ln_g, v_sgu_ln_b, v_w_spatial, v_b_spatial, v_attn_out_norm, v_sgu_out_norm, v_w_out, v_ln_post_mix, v_ln_pre_ffn, v_w_gate_up, v_w_down, v_ln_post_ffn, v_w_pe_gate, v_b_pe_gate, v_w_pe_proj):
    given = dict(locals())
    w = {n: given[n] for n in WEIGHTS}
    m = {n: given["m_" + n] for n in WEIGHTS}
    v = {n: given["v_" + n] for n in WEIGHTS}
    shapes = {n: w[n].shape for n in WEIGHTS}
    xs, ps, target = x[0], p[0, 0], loss_target[0]

    shard = {n: w[n][0].astype(BF) for n in ROW_SHARDED}
    shard.update({n: w[n][0].T.astype(BF) for n in COL_SHARDED})
    sm = {n: w[n][0] for n in SMALL}
    sm = {n: (a.reshape(1, -1) if a.ndim == 1 else a) for n, a in sm.items()}
    slopes = jnp.broadcast_to((2.0 ** -(jnp.arange(8, dtype=F32) + 1.0)).reshape(4, 2, 1), (4, 2, LANES))
    slopes = jnp.concatenate([slopes, jnp.zeros((4, 6, LANES), F32)], axis=1)
    b_st = jnp.pad(sm["b_spatial"].T, ((0, 0), (0, LANES - N_GROUPS)))

    def full(gathered):
        return gathered.reshape(-1, gathered.shape[-1])

    w_in_t = full(_all_gather(shard["w_in"], "gather_w_in"))
    proj, a = _in_proj(xs, sm["ln_pre_mix"], w_in_t)
    later = ("w_out", "w_gate_up", "w_down", "w_pe_gate", "w_pe_proj")
    attn, lse, gathered = _attn_fwd(proj, slopes, [shard[n] for n in later])
    w_out_f, w_gu_t, w_down_f, w_peg_f, w_pep_t = [full(g) for g in gathered]
    sgu = _sgu_fwd(proj, sm["sgu_ln_g"], sm["sgu_ln_b"], sm["w_spatial"], b_st)
    groups, mixed, h1, f = _out_proj(attn, sgu, xs, sm["attn_out_norm"], sm["sgu_out_norm"], w_out_f,
                                     sm["ln_post_mix"], sm["ln_pre_ffn"])
    g, u, act = _gate_up(f, w_gu_t)
    y, h2 = _down_proj(act, w_down_f, h1, sm["ln_post_ffn"])
    dh2, dy, dpp, dpre, h2b, pb, loss_cols, db_peg, d_pff = _pe_loss_and_bwd(
        h2, ps, target, y, w_peg_f, sm["b_pe_gate"], w_pep_t, sm["ln_post_ffn"])
    loss_term = 0.5 * jnp.sum(loss_cols) * (1.0 / D_MODEL)

    arrived = {}
    g_pep_t = _weight_grad(dpp, pb, "w_pe_proj")
    g_peg = _weight_grad(h2b, dpre, "w_pe_gate")
    g_down = _weight_grad(act, dy, "w_down")
    dg, du, (arrived["w_pe_proj"], arrived["w_pe_gate"]) = _down_bwd(dy, w_down_f, g, u, [_slabs(g_pep_t), _slabs(g_peg)])
    g_gu_t = _weight_grad(dg, f, "w_gate", rows=2 * D_FF)
    g_gu_t = _weight_grad(du, f, "w_up", into=g_gu_t, row_tile=FF_TILES, rows=2 * D_FF)
    dh1, dmx, d_pf, d_pm, (arrived["w_down"],) = _ffn_in_bwd(dg, du, w_gu_t, h1, dh2, mixed, sm["ln_pre_ffn"],
                                                            sm["ln_post_mix"], [_slabs(g_down)])
    g_out = _weight_grad(groups, dmx, "w_out")
    dattn, dsgu, d_ga, d_gs = _out_bwd(dmx, w_out_f, attn, sgu, sm["attn_out_norm"], sm["sgu_out_norm"])
    duz, d_ws, d_bst, d_lg, d_lb = _sgu_bwd(proj, dsgu, sm["sgu_ln_g"], sm["sgu_ln_b"], sm["w_spatial"], b_st)
    gs = dict(sgu_ln_g=d_lg, sgu_ln_b=d_lb, w_spatial=d_ws, b_spatial=d_bst[:, :N_GROUPS].T, attn_out_norm=d_ga,
              sgu_out_norm=d_gs, ln_post_mix=d_pm, ln_pre_ffn=d_pf, ln_post_ffn=d_pff, b_pe_gate=db_peg)
    small_grads = _pack_small_grads(gs, loss_term)
    dqkv, (arrived["w_gate_up"], arrived["w_out"], *arrived_small) = _attn_bwd(
        proj, dattn, attn, lse, slopes, [_slabs(g_gu_t), _slabs(g_out), *small_grads],
        [True, True] + [False] * len(small_grads))
    g_in_t = _grad_w_in([dqkv, duz], a)
    grad_x, d_g1, (arrived["w_in"],) = _in_bwd(
        [dqkv, duz], w_in_t, xs, dh1, sm["ln_pre_mix"], [_slabs(g_in_t)], [True])
    (arrived_late,) = _scatter_call([jnp.pad(d_g1, ((0, SUBLANES - 1), (0, 0)))], [False], "ln_pre_mix_grad_exchange")

    res = {}
    for n in ROW_SHARDED:
        res[n] = _reduce_adamw(arrived[n], w[n][0], m[n][0], v[n][0], n)
    for n in ("w_in", "w_gate_up"):
        res[n] = [t.T for t in _reduce_adamw(arrived[n], w[n][0].T, m[n][0].T, v[n][0].T, n)]
    for n in ("w_pe_proj",):
        grad = _reduce(arrived[n], n).T
        res[n] = (grad, *_adamw(grad, w[n][0], m[n][0], v[n][0], n))
    small, loss_row = _small_adamw(arrived_small, arrived_late, w, m, v)

    out = []
    for k in range(4):
        out += [res[n][k][None] if n in res else small[n][k] for n in WEIGHTS]
    return (loss_row[0, 0], grad_x[None], *out)
```

```python
import math

import jax
import jax.numpy as jnp
from jax import lax
from jax.experimental import pallas as pl
from jax.experimental.pallas import tpu as pltpu

F32 = jnp.float32
BF = jnp.bfloat16


def SDS(shape, dtype):
    return pltpu.HBM(tuple(shape), dtype)

D_MODEL = 1024
ATTN_W = 512
SGU_W = 512
HEAD_DIM = 64
N_GROUPS = 4
GROUP_DIM = 128
CHUNK = 128
D_FF = 2816
PLE_DIM = 256
PROJ = 3 * ATTN_W + 2 * SGU_W
DILATIONS = (1, 4, 16)
QBLK = 128
EPS = 1e-6
NEG = -1e30
N_DEV = 8
LANES = 128

ADAM_LR = 0.001
ADAM_B1 = 0.9
ADAM_B2 = 0.999
ADAM_EPS = 1e-08
ADAM_WD = 0.01
ADAM_STEP = 10

SMALL_ROWS = 80
MIB = 2 ** 20
MESH_ID = pl.DeviceIdType.MESH
HBM = pl.BlockSpec(memory_space=pl.ANY)


def _params(sem, vmem_mib):
    return pltpu.CompilerParams(dimension_semantics=sem, vmem_limit_bytes=vmem_mib * MIB)


def _dot(a, b):
    return jnp.dot(a, b, preferred_element_type=F32)


def _dot_nt(a, b):
    return lax.dot_general(a, b, (((1,), (1,)), ((), ())), preferred_element_type=F32)


def _dot_tn(a, b):
    return lax.dot_general(a, b, (((0,), (0,)), ((), ())), preferred_element_type=F32)


def _rstd(x):
    return lax.rsqrt(jnp.mean(x * x, axis=-1, keepdims=True) + EPS)


def _rms_bwd(dy, x, g):
    r = _rstd(x)
    n = x * r
    dn = dy * g
    dx = r * (dn - n * jnp.mean(dn * n, axis=-1, keepdims=True))
    return dx, dy * n


def _colsum(v):
    return jnp.sum(v, axis=0, keepdims=True)


_G0 = math.sqrt(2.0 / math.pi)
_G1 = 0.044715


def _gelu(x):
    t = jnp.tanh(_G0 * (x + _G1 * x * x * x))
    return 0.5 * x * (1.0 + t), t


def _gelu_grad(x, t):
    return 0.5 * (1.0 + t) + 0.5 * x * (1.0 - t * t) * (_G0 * (1.0 + 3.0 * _G1 * x * x))


def _sigmoid(x):
    return 0.5 * jnp.tanh(0.5 * x) + 0.5


def _row(i):
    return (i, 0)


def _fixed(i):
    return (0, 0)


def _acc_init(step, *refs):
    @pl.when(step == 0)
    def _():
        for r in refs:
            r[...] = jnp.zeros_like(r)


FLIPS = [(dx, dy, dc) for dx in (0, 1) for dy in (0, 1) for dc in (0, 1)][1:]
DMA_SEMS = pltpu.SemaphoreType.DMA


def _mesh_pos():
    return lax.axis_index("x"), lax.axis_index("y"), lax.axis_index("c")


def _remote(src, dst, sems, n, to):
    return pltpu.make_async_remote_copy(src_ref=src, dst_ref=dst, send_sem=sems[0].at[n], recv_sem=sems[1].at[n],
                                        device_id=to, device_id_type=MESH_ID)


class _Scatter:
    def __init__(self, items, sems):
        x, y, c = _mesh_pos()
        me = 4 * x + 2 * y + c
        self.local, self.sends, self.arrivals = [], [], []
        for i, (src, dst, slabbed) in enumerate(items):
            self.local.append(pltpu.make_async_copy(src.at[me] if slabbed else src, dst.at[me], sems[2].at[i]))
            for k, (dx, dy, dc) in enumerate(FLIPS):
                to = (1 - x if dx else x, 1 - y if dy else y, 1 - c if dc else c)
                peer = 4 * to[0] + 2 * to[1] + to[2]
                out = src.at[peer] if slabbed else src
                self.sends.append(_remote(out, dst.at[me], sems, 7 * i + k, to))
                self.arrivals.append(_remote(out, dst.at[peer], sems, 7 * i + k, to))

    def start(self):
        for cp in self.local + self.sends:
            cp.start()

    def wait(self):
        for cp in self.arrivals:
            cp.wait_recv()
        for cp in self.sends:
            cp.wait_send()
        for cp in self.local:
            cp.wait()


def _scatter_sems(n):
    return [DMA_SEMS((7 * n,)), DMA_SEMS((7 * n,)), DMA_SEMS((n,))]


class _Gather:
    def __init__(self, items, sems):
        x, y, c = _mesh_pos()
        me, sibling = (x, y, c), (x, y, 1 - c)
        chips = [(1 - x, y), (x, 1 - y), (1 - x, 1 - y)]
        self.first, self.passed, self.from_chips, self.rest, self.local = [], [], [], [], []
        for i, (src, dst) in enumerate(items):
            def slot(p, dst=dst):
                return dst.at[4 * p[0] + 2 * p[1] + p[2]]

            def copy(k, block, to, own=False, i=i, src=src, slot=slot):
                return _remote(src if own else slot(block), slot(block), sems, 7 * i + k, to)

            self.local.append(pltpu.make_async_copy(src, slot(me), sems[2].at[i]))
            self.first.append(copy(0, me, sibling, own=True))
            self.first += [copy(1 + j, me, (*chip, c), own=True) for j, chip in enumerate(chips)]
            self.passed += [copy(4 + j, (*chip, c), sibling) for j, chip in enumerate(chips)]
            self.from_chips += [copy(1 + j, (*chip, c), me) for j, chip in enumerate(chips)]
            self.rest.append(copy(0, sibling, me))
            self.rest += [copy(4 + j, (*chip, 1 - c), me) for j, chip in enumerate(chips)]

    def start(self):
        for cp in self.local + self.first:
            cp.start()

    def forward(self):
        for arrived, onward in zip(self.from_chips, self.passed):
            arrived.wait_recv()
            onward.start()

    def finish(self):
        for cp in self.rest:
            cp.wait_recv()
        for cp in self.first + self.passed:
            cp.wait_send()
        for cp in self.local:
            cp.wait()


def _all_gather(shard, name):
    def body(x_ref, out_ref, *sems):
        g = _Gather([(x_ref, out_ref)], sems)
        g.start()
        g.forward()
        g.finish()

    return pl.pallas_call(
        body, out_shape=SDS((N_DEV,) + shard.shape, shard.dtype), in_specs=[HBM], out_specs=HBM,
        scratch_shapes=_scatter_sems(1), name=name)(shard)


def _scatter_call(srcs, slabbed, name):
    n = len(srcs)

    def body(*refs):
        sc = _Scatter(list(zip(refs[:n], refs[n:2 * n], slabbed)), refs[2 * n:])
        sc.start()
        sc.wait()

    shapes = [SDS(s.shape if sl else (N_DEV,) + s.shape, s.dtype) for s, sl in zip(srcs, slabbed)]
    return pl.pallas_call(body, out_shape=shapes, in_specs=[HBM] * n, out_specs=[HBM] * n,
                          scratch_shapes=_scatter_sems(n), name=name)(*srcs)


def _in_proj(x, g1, w_in_t, tm=512):
    T = x.shape[0]

    def body(x_ref, g_ref, w_ref, proj_ref, a_ref):
        xv = x_ref[...]
        a = (xv * _rstd(xv) * g_ref[...]).astype(BF)
        a_ref[...] = a
        proj_ref[...] = _dot_nt(a, w_ref[...])

    return pl.pallas_call(
        body, grid=(T // tm,),
        in_specs=[pl.BlockSpec((tm, D_MODEL), _row), pl.BlockSpec((1, D_MODEL), _fixed),
                  pl.BlockSpec((PROJ, D_MODEL), _fixed)],
        out_specs=[pl.BlockSpec((tm, PROJ), _row), pl.BlockSpec((tm, D_MODEL), _row)],
        out_shape=[SDS((T, PROJ), F32), SDS((T, D_MODEL), BF)],
        name="in_proj", compiler_params=_params(("arbitrary",), 48))(x, g1, w_in_t)


ATTN_GROUP = 16


def _attn_bias(sl_ref, bias_ref):
    qi = lax.broadcasted_iota(jnp.int32, (QBLK, QBLK), 0)
    kj = lax.broadcasted_iota(jnp.int32, (QBLK, QBLK), 1)
    step = qi - kj
    for di, d in enumerate(DILATIONS):
        for j in range(2):
            sl = sl_ref[0, j:j + 1, :]
            cur = jnp.where(step >= 0, -sl * (step * d).astype(F32), NEG)
            prev = jnp.where(step <= 0, -sl * ((step + QBLK) * d).astype(F32), NEG)
            rows = slice(j * QBLK, (j + 1) * QBLK)
            bias_ref[di * 2, rows, :QBLK] = prev
            bias_ref[di * 2, rows, QBLK:] = cur
            bias_ref[di * 2 + 1, rows, :QBLK] = jnp.full((QBLK, QBLK), NEG, F32)
            bias_ref[di * 2 + 1, rows, QBLK:] = cur


def _stack_heads(x, lo):
    return jnp.concatenate([jnp.where(lo, x, 0.0), jnp.where(lo, 0.0, x)], axis=0).astype(BF)


def _unstack_heads(x, lo):
    return jnp.where(lo, x[:QBLK], x[QBLK:])


def _attn_rows(start, d, blocks=1):
    if d == 1:
        return pl.ds(pl.multiple_of(start, QBLK), blocks * QBLK)
    return pl.ds(start, blocks * QBLK, stride=d)


def _attn_group_index(i, nblk, d, group):
    per = nblk // d // group
    r = i // per
    n0 = (i % per) * group
    start = r + (d * QBLK) * n0
    pstart = jnp.maximum(start - d * QBLK, r)
    return start, pstart, n0 == 0


def _attn_plan(nblk, d):
    group = min(ATTN_GROUP, nblk // d)
    return group, max(1, min(ATTN_GROUP // group, d))


def _attn_fwd(proj, slopes, to_gather):
    T = proj.shape[0]
    nblk = T // QBLK
    n_g = len(to_gather)

    def body(q_ref, k_ref, v_ref, sl_ref, *rest):
        srcs, (o_ref, m_ref), dsts = rest[:n_g], rest[n_g:n_g + 2], rest[n_g + 2:2 * n_g + 2]
        sems, (l_ref, bias_ref) = rest[2 * n_g + 2:2 * n_g + 5], rest[2 * n_g + 5:]
        h = pl.program_id(0)

        @pl.when(h == 0)
        def _():
            _Gather(list(zip(srcs, dsts)), sems).start()

        @pl.when(h == pl.num_programs(0) - 1)
        def _():
            _Gather(list(zip(srcs, dsts)), sems).forward()

        _attn_bias(sl_ref, bias_ref)
        lo = lax.broadcasted_iota(jnp.int32, (1, LANES), 1) < HEAD_DIM

        order = list(enumerate(DILATIONS))[::-1]
        for di, d in order:
            group, segs = _attn_plan(nblk, d)
            fresh, last = di == order[0][0], di == order[-1][0]

            def step(i, carry, segs=segs, **kw):
                for s in range(segs):
                    segment(i * segs + s, **kw)
                return carry

            def segment(i, d=d, di=di, group=group, fresh=fresh, last=last):
                start, pstart, first = _attn_group_index(i, nblk, d, group)
                prows = _attn_rows(pstart, d)
                k_prev, v_prev = k_ref[prows, :].astype(BF), v_ref[prows, :].astype(BF)
                for b in range(group):
                    out = _attn_rows(start + b * (d * QBLK), d)
                    k_own, v_own = k_ref[out, :].astype(BF), v_ref[out, :].astype(BF)
                    k2, v2 = jnp.concatenate([k_prev, k_own], axis=0), jnp.concatenate([v_prev, v_own], axis=0)
                    k_prev, v_prev = k_own, v_own
                    bias = bias_ref[di * 2 + first.astype(jnp.int32)] if b == 0 else bias_ref[di * 2]
                    s = _dot_nt(_stack_heads(q_ref[out, :] * (HEAD_DIM ** -0.5), lo), k2) + bias
                    m = jnp.max(s, axis=-1, keepdims=True)
                    pr = jnp.exp(s - m)
                    m_b = _unstack_heads(m, lo)
                    l_b = _unstack_heads(jnp.sum(pr, axis=-1, keepdims=True), lo)
                    o_b = _unstack_heads(_dot(pr.astype(BF), v2), lo)
                    if fresh:
                        m_ref[out, :] = m_b
                        l_ref[out, :] = l_b
                        o_ref[out, :] = o_b
                        continue
                    m_o = m_ref[out, :]
                    m_n = jnp.maximum(m_o, m_b)
                    wa, wb = jnp.exp(m_o - m_n), jnp.exp(m_b - m_n)
                    l_n = wa * l_ref[out, :] + wb * l_b
                    o_n = wa * o_ref[out, :] + wb * o_b
                    if last:
                        m_ref[out, :] = m_n + jnp.log(l_n)
                        o_ref[out, :] = o_n / l_n
                    else:
                        m_ref[out, :] = m_n
                        l_ref[out, :] = l_n
                        o_ref[out, :] = o_n

            lax.fori_loop(0, nblk // (group * segs), step, 0)

        @pl.when(h == pl.num_programs(0) - 1)
        def _():
            _Gather(list(zip(srcs, dsts)), sems).finish()

    col = lambda base: pl.BlockSpec((T, LANES), lambda h: (0, base + h))
    tok = pl.BlockSpec((T, LANES), lambda h: (0, h))
    outs = pl.pallas_call(
        body, grid=(4,),
        in_specs=[col(0), col(4), col(8), pl.BlockSpec((1, 8, LANES), lambda h: (h, 0, 0))] + [HBM] * n_g,
        out_specs=[tok, tok] + [HBM] * n_g,
        out_shape=[SDS((T, ATTN_W), F32), SDS((T, ATTN_W), F32)]
        + [SDS((N_DEV,) + g.shape, g.dtype) for g in to_gather],
        scratch_shapes=_scatter_sems(n_g) + [pltpu.VMEM((T, LANES), F32), pltpu.VMEM((6, 2 * QBLK, 2 * QBLK), F32)],
        name="attn_fwd", compiler_params=_params(("arbitrary",), 56))(proj, proj, proj, slopes, *to_gather)
    return outs[0], outs[1], outs[2:]


def _sgu_norm(zv, ln_g, ln_b):
    gz, tz = _gelu(zv)
    mu = jnp.mean(gz, axis=-1, keepdims=True)
    xc = gz - mu
    rs = lax.rsqrt(jnp.mean(xc * xc, axis=-1, keepdims=True) + EPS)
    xhat = xc * rs
    return xhat * ln_g + ln_b, xhat, rs, tz


def _causal(w):
    i = lax.broadcasted_iota(jnp.int32, (CHUNK, CHUNK), 0)
    j = lax.broadcasted_iota(jnp.int32, (CHUNK, CHUNK), 1)
    return jnp.where(i >= j, w, 0.0)


def _sgu_fwd(proj, ln_g, ln_b, w_s, b_st, tm=512):
    T = proj.shape[0]

    def body(u_ref, z_ref, g_ref, b_ref, w_ref, bs_ref, out_ref):
        for g in range(N_GROUPS):
            wm = _causal(w_ref[g]).astype(BF)
            cols = slice(g * GROUP_DIM, (g + 1) * GROUP_DIM)
            for c in range(tm // CHUNK):
                rows = slice(c * CHUNK, (c + 1) * CHUNK)
                zn, _, _, _ = _sgu_norm(z_ref[rows, cols], g_ref[...], b_ref[...])
                mixed = _dot(wm, zn.astype(BF)) + bs_ref[:, g:g + 1]
                gu, _ = _gelu(u_ref[rows, cols])
                out_ref[rows, cols] = gu * mixed

    return pl.pallas_call(
        body, grid=(T // tm,),
        in_specs=[pl.BlockSpec((tm, SGU_W), lambda i: (i, 3)), pl.BlockSpec((tm, SGU_W), lambda i: (i, 4)),
                  pl.BlockSpec((1, GROUP_DIM), _fixed), pl.BlockSpec((1, GROUP_DIM), _fixed),
                  pl.BlockSpec((N_GROUPS, CHUNK, CHUNK), lambda i: (0, 0, 0)), pl.BlockSpec((CHUNK, LANES), _fixed)],
        out_specs=pl.BlockSpec((tm, SGU_W), _row),
        out_shape=SDS((T, SGU_W), F32),
        name="sgu_fwd", compiler_params=_params(("arbitrary",), 32))(proj, proj, ln_g, ln_b, w_s, b_st)


def _out_proj(attn, sgu, x, g_a, g_s, w_out, g_pm, g_pf, tm=512):
    T = x.shape[0]

    def body(a_ref, s_ref, x_ref, ga_ref, gs_ref, w_ref, gpm_ref, gpf_ref, grp_ref, mixed_ref, h1_ref, f_ref):
        av, sv = a_ref[...], s_ref[...]
        an = (av * _rstd(av) * ga_ref[...]).astype(BF)
        sn = (sv * _rstd(sv) * gs_ref[...]).astype(BF)
        grp_ref[:, :ATTN_W] = an
        grp_ref[:, ATTN_W:] = sn
        mixed = _dot(an, w_ref[:ATTN_W, :]) + _dot(sn, w_ref[ATTN_W:, :])
        mixed_ref[...] = mixed
        h1 = x_ref[...] + mixed * _rstd(mixed) * gpm_ref[...]
        h1_ref[...] = h1
        f_ref[...] = (h1 * _rstd(h1) * gpf_ref[...]).astype(BF)

    tok = lambda w: pl.BlockSpec((tm, w), _row)
    vec = lambda w: pl.BlockSpec((1, w), _fixed)
    return pl.pallas_call(
        body, grid=(T // tm,),
        in_specs=[tok(ATTN_W), tok(SGU_W), tok(D_MODEL), vec(ATTN_W), vec(SGU_W),
                  pl.BlockSpec((D_MODEL, D_MODEL), _fixed), vec(D_MODEL), vec(D_MODEL)],
        out_specs=[tok(D_MODEL)] * 4,
        out_shape=[SDS((T, D_MODEL), BF), SDS((T, D_MODEL), F32), SDS((T, D_MODEL), F32), SDS((T, D_MODEL), BF)],
        name="out_proj", compiler_params=_params(("arbitrary",), 48))(attn, sgu, x, g_a, g_s, w_out, g_pm, g_pf)


FF_TILE = 1408
FF_TILES = D_FF // FF_TILE
FF_CHUNK = 256


def _gate_up(f, w_gu_t, tm=512):
    T = f.shape[0]
    tn = FF_TILE

    def body(f_ref, wg_ref, wu_ref, g_ref, u_ref, act_ref):
        fv = f_ref[...]
        g = _dot_nt(fv, wg_ref[...])
        u = _dot_nt(fv, wu_ref[...])
        g_ref[...] = g.astype(BF)
        u_ref[...] = u.astype(BF)
        act_ref[...] = (g * _sigmoid(g) * u).astype(BF)

    ospec = pl.BlockSpec((tm, tn), lambda j, i: (i, j))
    return pl.pallas_call(
        body, grid=(FF_TILES, T // tm),
        in_specs=[pl.BlockSpec((tm, D_MODEL), lambda j, i: (i, 0)), pl.BlockSpec((tn, D_MODEL), lambda j, i: (j, 0)),
                  pl.BlockSpec((tn, D_MODEL), lambda j, i: (j + FF_TILES, 0))],
        out_specs=[ospec] * 3, out_shape=[SDS((T, D_FF), BF)] * 3,
        name="gate_up", compiler_params=_params(("arbitrary", "arbitrary"), 40))(f, w_gu_t, w_gu_t)


def _down_proj(act, w_down, h1, g_pff, tm=512):
    T = act.shape[0]

    def body(a_ref, w_ref, h1_ref, g_ref, y_ref, h2_ref):
        y = _dot(a_ref[...], w_ref[...])
        y_ref[...] = y
        h2_ref[...] = h1_ref[...] + y * _rstd(y) * g_ref[...]

    return pl.pallas_call(
        body, grid=(T // tm,),
        in_specs=[pl.BlockSpec((tm, D_FF), _row), pl.BlockSpec((D_FF, D_MODEL), _fixed),
                  pl.BlockSpec((tm, D_MODEL), _row), pl.BlockSpec((1, D_MODEL), _fixed)],
        out_specs=[pl.BlockSpec((tm, D_MODEL), _row)] * 2,
        out_shape=[SDS((T, D_MODEL), F32)] * 2,
        name="down_proj", compiler_params=_params(("arbitrary",), 48))(act, w_down, h1, g_pff)


def _pe_loss_and_bwd(h2, p, target, y, w_peg, b_peg, w_pep_t, g_pff, tm=512):
    T = h2.shape[0]

    def body(h2_ref, p_ref, t_ref, y_ref, wg_ref, b_ref, wp_ref, g_ref,
             dh2_ref, dy_ref, dpp_ref, dpre_ref, h2b_ref, pb_ref, loss_ref, db_ref, dg_ref):
        _acc_init(pl.program_id(0), loss_ref, db_ref, dg_ref)
        h2v = h2_ref[...]
        h2b = h2v.astype(BF)
        pb = p_ref[...].astype(BF)
        h2b_ref[...] = h2b
        pb_ref[...] = pb
        gate = _sigmoid(_dot(h2b, wg_ref[...]) + b_ref[...])
        pp = _dot_nt(pb, wp_ref[...])
        diff = h2v + gate * pp - t_ref[...]
        loss_ref[...] += _colsum(diff * diff)
        dh3 = diff * (1.0 / D_MODEL)
        dpp_ref[...] = (dh3 * gate).astype(BF)
        dpre = dh3 * pp * (gate * (1.0 - gate))
        dpre_b = dpre.astype(BF)
        dpre_ref[...] = dpre_b
        db_ref[...] += _colsum(dpre)
        dh2 = dh3 + _dot_nt(dpre_b, wg_ref[...])
        dh2_ref[...] = dh2
        dy, dg = _rms_bwd(dh2, y_ref[...], g_ref[...])
        dy_ref[...] = dy.astype(BF)
        dg_ref[...] += _colsum(dg)

    tok = lambda w: pl.BlockSpec((tm, w), _row)
    vec = pl.BlockSpec((1, D_MODEL), _fixed)
    return pl.pallas_call(
        body, grid=(T // tm,),
        in_specs=[tok(D_MODEL), tok(PLE_DIM), tok(D_MODEL), tok(D_MODEL), pl.BlockSpec((D_MODEL, D_MODEL), _fixed), vec,
                  pl.BlockSpec((D_MODEL, PLE_DIM), _fixed), vec],
        out_specs=[tok(D_MODEL)] * 5 + [tok(PLE_DIM), vec, vec, vec],
        out_shape=[SDS((T, D_MODEL), F32)] + [SDS((T, D_MODEL), BF)] * 4
        + [SDS((T, PLE_DIM), BF)] + [SDS((1, D_MODEL), F32)] * 3,
        name="pe_loss_and_bwd", compiler_params=_params(("arbitrary",), 56))(
            h2, p, target, y, w_peg, b_peg, w_pep_t, g_pff)


def _weight_grad(a, dy, name, into=None, row_tile=0, rows=None, out_dtype=F32, tk=512):
    n = dy.shape[1]
    tn = min(n, 1024)
    T, ka = a.shape
    tka = FF_TILE if ka == D_FF else min(ka, 1024)
    rows = ka if rows is None else rows
    direct = out_dtype == F32

    def body(a_ref, dy_ref, *rest):
        out_ref, acc_ref = (rest[-1], rest[-1]) if direct else rest[-2:]
        _acc_init(pl.program_id(2), acc_ref)
        acc_ref[...] += _dot_tn(a_ref[...].astype(BF), dy_ref[...].astype(BF))
        if not direct:
            @pl.when(pl.program_id(2) == pl.num_programs(2) - 1)
            def _():
                out_ref[...] = acc_ref[...].astype(out_dtype)

    carried = [] if into is None else [into]
    return pl.pallas_call(
        body, grid=(ka // tka, n // tn, T // tk),
        in_specs=[pl.BlockSpec((tk, tka), lambda i, j, k: (k, i)), pl.BlockSpec((tk, tn), lambda i, j, k: (k, j))]
        + [HBM] * len(carried),
        out_specs=pl.BlockSpec((tka, tn), lambda i, j, k: (i + row_tile, j)),
        out_shape=SDS((rows, n), out_dtype), input_output_aliases={2: 0} if carried else {},
        scratch_shapes=[] if direct else [pltpu.VMEM((tka, tn), F32)],
        name="grad_" + name, compiler_params=_params(("arbitrary",) * 3, 40))(a, dy, *carried)


def _grad_w_in(dparts, a, tk=512):
    T = a.shape[0]

    def body(*refs):
        d_refs, a_ref, out_ref, acc_ref = refs[:len(dparts)], refs[-3], refs[-2], refs[-1]
        k = pl.program_id(0)
        _acc_init(k, acc_ref)
        cols = [r[part].astype(BF) for r in d_refs for part in range(r.shape[0])]
        acc_ref[...] += _dot_tn(jnp.concatenate(cols, axis=1), a_ref[...])

        @pl.when(k == pl.num_programs(0) - 1)
        def _():
            out_ref[...] = acc_ref[...].astype(BF)

    return pl.pallas_call(
        body, grid=(T // tk,),
        in_specs=[pl.BlockSpec((d.shape[0], tk, d.shape[2]), lambda k: (0, k, 0)) for d in dparts]
        + [pl.BlockSpec((tk, D_MODEL), lambda k: (k, 0))],
        out_specs=pl.BlockSpec((PROJ, D_MODEL), lambda k: (0, 0)),
        out_shape=SDS((PROJ, D_MODEL), BF), scratch_shapes=[pltpu.VMEM((PROJ, D_MODEL), F32)],
        name="grad_w_in", compiler_params=_params(("arbitrary",), 48))(*dparts, a)


def _down_bwd(dy, w_down, g, u, to_send, tm=512):
    T = dy.shape[0]
    n_s = len(to_send)

    def body(dy_ref, w_ref, g_ref, u_ref, *rest):
        srcs, (dg_ref, du_ref), dsts, sems = rest[:n_s], rest[n_s:n_s + 2], rest[n_s + 2:2 * n_s + 2], rest[2 * n_s + 2:]
        i = pl.program_id(0)
        items = list(zip(srcs, dsts, [True] * n_s))

        @pl.when(i == 0)
        def _():
            _Scatter(items, sems).start()

        dyv = dy_ref[...]
        for c in range(D_FF // FF_CHUNK):
            cols = slice(c * FF_CHUNK, (c + 1) * FF_CHUNK)
            dact = _dot_nt(dyv, w_ref[cols, :])
            gv = g_ref[:, cols].astype(F32)
            uv = u_ref[:, cols].astype(F32)
            s = _sigmoid(gv)
            ds = dact * s
            dg_ref[:, cols] = (ds * uv * (1.0 + gv * (1.0 - s))).astype(BF)
            du_ref[:, cols] = (ds * gv).astype(BF)

        @pl.when(i == pl.num_programs(0) - 1)
        def _():
            _Scatter(items, sems).wait()

    tile = pl.BlockSpec((tm, D_FF), _row)
    outs = pl.pallas_call(
        body, grid=(T // tm,),
        in_specs=[pl.BlockSpec((tm, D_MODEL), _row),
                  pl.BlockSpec((D_FF, D_MODEL), _fixed, pipeline_mode=pl.Buffered(1)), tile, tile] + [HBM] * n_s,
        out_specs=[tile, tile] + [HBM] * n_s,
        out_shape=[SDS((T, D_FF), BF)] * 2 + [SDS(s.shape, s.dtype) for s in to_send],
        scratch_shapes=_scatter_sems(n_s),
        name="down_bwd", compiler_params=_params(("arbitrary",), 48))(dy, w_down, g, u, *to_send)
    return outs[0], outs[1], outs[2:]


def _ffn_in_bwd(dg, du, w_gu_t, h1, dh2, mixed, g_pf, g_pm, to_send, tm=512):
    T = h1.shape[0]
    n_s = len(to_send)

    def body(dg_ref, du_ref, wg_ref, wu_ref, h1_ref, dh2_ref, mx_ref, gpf_ref, gpm_ref, *rest):
        srcs, outs, dsts, sems = rest[:n_s], rest[n_s:n_s + 4], rest[n_s + 4:2 * n_s + 4], rest[2 * n_s + 4:]
        dh1_ref, dmx_ref, dgpf_ref, dgpm_ref = outs
        i = pl.program_id(0)
        items = list(zip(srcs, dsts, [True] * n_s))
        _acc_init(i, dgpf_ref, dgpm_ref)

        @pl.when(i == 0)
        def _():
            _Scatter(items, sems).start()

        df = _dot(dg_ref[...], wg_ref[...]) + _dot(du_ref[...], wu_ref[...])
        dx, dgf = _rms_bwd(df, h1_ref[...], gpf_ref[...])
        dh1 = dh2_ref[...] + dx
        dh1_ref[...] = dh1
        dmx, dgm = _rms_bwd(dh1, mx_ref[...], gpm_ref[...])
        dmx_ref[...] = dmx.astype(BF)
        dgpf_ref[...] += _colsum(dgf)
        dgpm_ref[...] += _colsum(dgm)

        @pl.when(i == pl.num_programs(0) - 1)
        def _():
            _Scatter(items, sems).wait()

    tok = lambda w: pl.BlockSpec((tm, w), _row)
    vec = pl.BlockSpec((1, D_MODEL), _fixed)
    outs = pl.pallas_call(
        body, grid=(T // tm,),
        in_specs=[tok(D_FF), tok(D_FF), pl.BlockSpec((D_FF, D_MODEL), lambda i: (0, 0), pipeline_mode=pl.Buffered(1)),
                  pl.BlockSpec((D_FF, D_MODEL), lambda i: (1, 0), pipeline_mode=pl.Buffered(1)),
                  tok(D_MODEL), tok(D_MODEL), tok(D_MODEL), vec, vec]
        + [HBM] * n_s,
        out_specs=[tok(D_MODEL), tok(D_MODEL), vec, vec] + [HBM] * n_s,
        out_shape=[SDS((T, D_MODEL), F32), SDS((T, D_MODEL), BF), SDS((1, D_MODEL), F32), SDS((1, D_MODEL), F32)]
        + [SDS(s.shape, s.dtype) for s in to_send],
        scratch_shapes=_scatter_sems(n_s),
        name="ffn_in_bwd", compiler_params=_params(("arbitrary",), 56))(
            dg, du, w_gu_t, w_gu_t, h1, dh2, mixed, g_pf, g_pm, *to_send)
    return outs[0], outs[1], outs[2], outs[3], outs[4:]


STAT_LANES = HEAD_DIM // 2


def _out_bwd(dmx, w_out, attn, lse, sgu, g_a, g_s, tm=512):
    T = attn.shape[0]

    def body(dm_ref, w_ref, a_ref, l_ref, s_ref, ga_ref, gs_ref, da_ref, st_ref, ds_ref, dga_ref, dgs_ref):
        _acc_init(pl.program_id(0), dga_ref, dgs_ref)
        dgr = _dot_nt(dm_ref[...], w_ref[...])
        av = a_ref[...]
        da, dga = _rms_bwd(dgr[:, :ATTN_W], av, ga_ref[...])
        ds, dgs = _rms_bwd(dgr[:, ATTN_W:], s_ref[...], gs_ref[...])
        da_ref[...] = da
        ds_ref[...] = ds
        dga_ref[...] += _colsum(dga)
        dgs_ref[...] += _colsum(dgs)
        lane = lax.broadcasted_iota(jnp.int32, (1, LANES), 1)
        lo = lane < HEAD_DIM
        first = (lane % HEAD_DIM) < STAT_LANES
        prod = da * av
        for c in range(ATTN_W // LANES):
            cols = slice(c * LANES, (c + 1) * LANES)
            pc = prod[:, cols]
            delta = jnp.where(lo, jnp.sum(jnp.where(lo, pc, 0.0), axis=-1, keepdims=True),
                              jnp.sum(jnp.where(lo, 0.0, pc), axis=-1, keepdims=True))
            st_ref[:, cols] = jnp.where(first, l_ref[:, cols], delta)

    tok = lambda w: pl.BlockSpec((tm, w), _row)
    vec = lambda w: pl.BlockSpec((1, w), _fixed)
    return pl.pallas_call(
        body, grid=(T // tm,),
        in_specs=[tok(D_MODEL), pl.BlockSpec((D_MODEL, D_MODEL), _fixed), tok(ATTN_W), tok(ATTN_W), tok(SGU_W),
                  vec(ATTN_W), vec(SGU_W)],
        out_specs=[tok(ATTN_W), tok(ATTN_W), tok(SGU_W), vec(ATTN_W), vec(SGU_W)],
        out_shape=[SDS((T, ATTN_W), F32), SDS((T, ATTN_W), F32), SDS((T, SGU_W), F32), SDS((1, ATTN_W), F32),
                   SDS((1, SGU_W), F32)],
        name="out_bwd", compiler_params=_params(("arbitrary",), 48))(dmx, w_out, attn, lse, sgu, g_a, g_s)


def _sgu_bwd(proj, dsgu, ln_g, ln_b, w_s, b_st, tm=512):
    T = proj.shape[0]

    def body(u_ref, z_ref, ds_ref, g_ref, b_ref, w_ref, bs_ref,
             duz_ref, dw_ref, dbs_ref, dlg_ref, dlb_ref, dbacc_ref):
        du_ref, dz_ref = duz_ref.at[0], duz_ref.at[1]
        step = pl.program_id(0)
        _acc_init(step, dw_ref, dbs_ref, dlg_ref, dlb_ref, dbacc_ref)
        lng, lnb = g_ref[...], b_ref[...]
        for g in range(N_GROUPS):
            wm = _causal(w_ref[g]).astype(BF)
            cols = slice(g * GROUP_DIM, (g + 1) * GROUP_DIM)
            for c in range(tm // CHUNK):
                rows = slice(c * CHUNK, (c + 1) * CHUNK)
                zv, uv, dout = z_ref[rows, cols], u_ref[rows, cols], ds_ref[rows, cols]
                zn, xhat, rs, tz = _sgu_norm(zv, lng, lnb)
                znb = zn.astype(BF)
                mixed = _dot(wm, znb) + bs_ref[:, g:g + 1]
                gu, tu = _gelu(uv)
                du_ref[rows, cols] = (dout * mixed * _gelu_grad(uv, tu)).astype(BF)
                dmix = dout * gu
                dmb = dmix.astype(BF)
                dw_ref[g] += _causal(_dot_nt(dmb, znb))
                dbacc_ref[g] += dmix
                dzn = _dot_tn(wm, dmb)
                dlg_ref[...] += _colsum(dzn * xhat)
                dlb_ref[...] += _colsum(dzn)
                dxh = dzn * lng
                dgz = rs * (dxh - jnp.mean(dxh, axis=-1, keepdims=True)
                            - xhat * jnp.mean(dxh * xhat, axis=-1, keepdims=True))
                dz_ref[rows, cols] = (dgz * _gelu_grad(zv, tz)).astype(BF)

        @pl.when(step == pl.num_programs(0) - 1)
        def _():
            lane = lax.broadcasted_iota(jnp.int32, (CHUNK, LANES), 1)
            acc = jnp.zeros((CHUNK, LANES), F32)
            for g in range(N_GROUPS):
                acc = jnp.where(lane == g, jnp.sum(dbacc_ref[g], axis=-1, keepdims=True), acc)
            dbs_ref[...] = acc

    tok = pl.BlockSpec((tm, SGU_W), _row)
    vec = pl.BlockSpec((1, GROUP_DIM), _fixed)
    wsp = pl.BlockSpec((N_GROUPS, CHUNK, CHUNK), lambda i: (0, 0, 0))
    sq = pl.BlockSpec((CHUNK, LANES), _fixed)
    return pl.pallas_call(
        body, grid=(T // tm,),
        in_specs=[pl.BlockSpec((tm, SGU_W), lambda i: (i, 3)), pl.BlockSpec((tm, SGU_W), lambda i: (i, 4)), tok,
                  vec, vec, wsp, sq],
        out_specs=[pl.BlockSpec((2, tm, SGU_W), lambda i: (0, i, 0)), wsp, sq, vec, vec],
        out_shape=[SDS((2, T, SGU_W), BF), SDS((N_GROUPS, CHUNK, CHUNK), F32),
                   SDS((CHUNK, LANES), F32), SDS((1, GROUP_DIM), F32), SDS((1, GROUP_DIM), F32)],
        scratch_shapes=[pltpu.VMEM((N_GROUPS, CHUNK, LANES), F32)],
        name="sgu_bwd", compiler_params=_params(("arbitrary",), 32))(proj, proj, dsgu, ln_g, ln_b, w_s, b_st)


def _attn_bwd(proj, do, stats, slopes, to_send, slabbed):
    T = proj.shape[0]
    nblk = T // QBLK
    n_s = len(to_send)

    def body(q_ref, k_ref, v_ref, do_ref, st_ref, sl_ref, *rest):
        srcs, d_ref, dsts = rest[:n_s], rest[n_s], rest[n_s + 1:2 * n_s + 1]
        sems, bias_ref = rest[2 * n_s + 1:2 * n_s + 4], rest[2 * n_s + 4]
        dq_ref, dk_ref, dv_ref = d_ref.at[0], d_ref.at[1], d_ref.at[2]
        h = pl.program_id(0)
        items = list(zip(srcs, dsts, slabbed))

        @pl.when(h == 0)
        def _():
            _Scatter(items, sems).start()

        _attn_bias(sl_ref, bias_ref)
        lo = lax.broadcasted_iota(jnp.int32, (1, LANES), 1) < HEAD_DIM
        scale = HEAD_DIM ** -0.5
        d_ref[...] = jnp.zeros_like(d_ref)

        for di, d in enumerate(DILATIONS):
            group, segs = _attn_plan(nblk, d)

            def step(i, carry, segs=segs, **kw):
                for s in range(segs):
                    segment(i * segs + s, **kw)
                return carry

            def segment(i, d=d, di=di, group=group):
                start, pstart, first = _attn_group_index(i, nblk, d, group)
                rows, prows = _attn_rows(start, d, group), _attn_rows(pstart, d)
                q = q_ref[rows, :] * scale
                k = jnp.concatenate([k_ref[prows, :], k_ref[rows, :]], axis=0).astype(BF)
                v = jnp.concatenate([v_ref[prows, :], v_ref[rows, :]], axis=0).astype(BF)
                dov = do_ref[rows, :]
                stats = st_ref[rows, :]
                masks = [lo, ~lo]
                qm = [jnp.where(masks[j], q, 0.0).astype(BF) for j in range(2)]
                dom = [jnp.where(masks[j], dov, 0.0).astype(BF) for j in range(2)]
                for b in range(group):
                    qb = slice(b * QBLK, (b + 1) * QBLK)
                    kb = slice(b * QBLK, (b + 2) * QBLK)
                    which = di * 2 + first.astype(jnp.int32) if b == 0 else di * 2
                    dq_parts, prs, dss = [], [], []
                    for j in range(2):
                        bias = bias_ref[which, j * QBLK:(j + 1) * QBLK, :]
                        lj = stats[qb, j * HEAD_DIM:j * HEAD_DIM + 1]
                        delta = stats[qb, j * HEAD_DIM + STAT_LANES:j * HEAD_DIM + STAT_LANES + 1]
                        pr = jnp.exp(_dot_nt(qm[j][qb], k[kb]) + bias - lj)
                        ds = (pr * (_dot_nt(dom[j][qb], v[kb]) - delta)).astype(BF)
                        dq_parts.append(_dot(ds, k[kb]))
                        prs.append(pr.astype(BF))
                        dss.append(ds)
                    dk_b = _dot_tn(jnp.concatenate(dss, axis=0), jnp.concatenate([qm[0][qb], qm[1][qb]], axis=0))
                    dv_b = _dot_tn(jnp.concatenate(prs, axis=0), jnp.concatenate([dom[0][qb], dom[1][qb]], axis=0))
                    own = _attn_rows(start + b * (d * QBLK), d)
                    dq_ref[own, :] += jnp.where(lo, dq_parts[0], dq_parts[1]) * scale
                    if b == 0:
                        dk_ref[prows, :] += dk_b[:QBLK]
                        dv_ref[prows, :] += dv_b[:QBLK]
                        dk_ref[own, :] += dk_b[QBLK:]
                        dv_ref[own, :] += dv_b[QBLK:]
                    else:
                        two = _attn_rows(start + (b - 1) * (d * QBLK), d, 2)
                        dk_ref[two, :] += dk_b
                        dv_ref[two, :] += dv_b

            lax.fori_loop(0, nblk // (group * segs), step, 0)

        @pl.when(h == pl.num_programs(0) - 1)
        def _():
            _Scatter(items, sems).wait()

    col = lambda base: pl.BlockSpec((T, LANES), lambda h: (0, base + h))
    outs = pl.pallas_call(
        body, grid=(4,),
        in_specs=[col(0), col(4), col(8), col(0), col(0), pl.BlockSpec((1, 8, LANES), lambda h: (h, 0, 0))]
        + [HBM] * n_s,
        out_specs=[pl.BlockSpec((3, T, LANES), lambda h: (0, 0, h), pipeline_mode=pl.Buffered(1))] + [HBM] * n_s,
        out_shape=[SDS((3, T, ATTN_W), F32)]
        + [SDS(s.shape if sl else (N_DEV,) + s.shape, s.dtype) for s, sl in zip(to_send, slabbed)],
        scratch_shapes=_scatter_sems(n_s) + [pltpu.VMEM((6, 2 * QBLK, 2 * QBLK), F32)],
        name="attn_bwd", compiler_params=_params(("arbitrary",), 60))(proj, proj, proj, do, stats, slopes, *to_send)
    return outs[0], outs[1:]


def _in_bwd(dparts, w_in_t, x, dh1, g1, to_send, slabbed, tm=512):
    T = x.shape[0]
    n = len(dparts)
    n_s = len(to_send)
    w = ATTN_W

    def body(*refs):
        d_refs, (w_ref, x_ref, dh1_ref, g_ref), rest = refs[:n], refs[n:n + 4], refs[n + 4:]
        srcs, (dx_ref, dg_ref), dsts, sems = rest[:n_s], rest[n_s:n_s + 2], rest[n_s + 2:2 * n_s + 2], rest[2 * n_s + 2:]
        step = pl.program_id(0)
        items = list(zip(srcs, dsts, slabbed))
        _acc_init(step, dg_ref)

        @pl.when(step == 0)
        def _():
            _Scatter(items, sems).start()

        da = None
        col = 0
        for r in d_refs:
            for part in range(r.shape[0]):
                t = _dot(r[part].astype(BF), w_ref[col * w:(col + 1) * w, :])
                da = t if da is None else da + t
                col += 1
        dx, dg = _rms_bwd(da, x_ref[...], g_ref[...])
        dx_ref[...] = dh1_ref[...] + dx
        dg_ref[...] += _colsum(dg)

        @pl.when(step == pl.num_programs(0) - 1)
        def _():
            _Scatter(items, sems).wait()

    tok = lambda c: pl.BlockSpec((tm, c), _row)
    vec = pl.BlockSpec((1, D_MODEL), _fixed)
    outs = pl.pallas_call(
        body, grid=(T // tm,),
        in_specs=[pl.BlockSpec((d.shape[0], tm, w), lambda i: (0, i, 0)) for d in dparts]
        + [pl.BlockSpec((PROJ, D_MODEL), _fixed), tok(D_MODEL), tok(D_MODEL), vec] + [HBM] * n_s,
        out_specs=[tok(D_MODEL), vec] + [HBM] * n_s,
        out_shape=[SDS((T, D_MODEL), F32), SDS((1, D_MODEL), F32)]
        + [SDS(s.shape if sl else (N_DEV,) + s.shape, s.dtype) for s, sl in zip(to_send, slabbed)],
        scratch_shapes=_scatter_sems(n_s),
        name="in_bwd", compiler_params=_params(("arbitrary",), 52))(*dparts, w_in_t, x, dh1, g1, *to_send)
    return outs[0], outs[1], outs[2:]


def _sum_parts(p_ref):
    g = p_ref[0].astype(F32)
    for s in range(1, N_DEV):
        g = g + p_ref[s].astype(F32)
    return g


def _adamw_math(g, w, m, v):
    nm = ADAM_B1 * m + (1.0 - ADAM_B1) * g
    nv = ADAM_B2 * v + (1.0 - ADAM_B2) * (g * g)
    m_hat = nm / (1.0 - ADAM_B1 ** ADAM_STEP)
    v_hat = nv / (1.0 - ADAM_B2 ** ADAM_STEP)
    return -ADAM_LR * (m_hat / (jnp.sqrt(v_hat) + ADAM_EPS) + ADAM_WD * w), nm, nv


def _row_tile(rows):
    for t in (256, 176, 128, 80):
        if rows % t == 0:
            return t
    raise ValueError(rows)


def _reduce_adamw(parts, w, m, v, name):
    rows, width = w.shape
    tr = _row_tile(rows)

    def body(p_ref, w_ref, m_ref, v_ref, g_ref, d_ref, nm_ref, nv_ref):
        g = _sum_parts(p_ref)
        g_ref[...] = g
        d_ref[...], nm_ref[...], nv_ref[...] = _adamw_math(g, w_ref[...], m_ref[...], v_ref[...])

    blk = pl.BlockSpec((tr, width), _row)
    return pl.pallas_call(
        body, grid=(rows // tr,),
        in_specs=[pl.BlockSpec((N_DEV, tr, width), lambda i: (0, i, 0)), blk, blk, blk],
        out_specs=[blk] * 4, out_shape=[SDS((rows, width), F32)] * 4,
        name="adamw_" + name, compiler_params=_params(("arbitrary",), 32))(parts, w, m, v)


def _reduce(parts, name):
    _, rows, width = parts.shape
    tr = _row_tile(rows)

    def body(p_ref, g_ref):
        g_ref[...] = _sum_parts(p_ref)

    return pl.pallas_call(
        body, grid=(rows // tr,),
        in_specs=[pl.BlockSpec((N_DEV, tr, width), lambda i: (0, i, 0))],
        out_specs=pl.BlockSpec((tr, width), _row), out_shape=SDS((rows, width), F32),
        name="sum_" + name, compiler_params=_params(("arbitrary",), 32))(parts)


def _adamw(g, w, m, v, name):
    rows, width = w.shape
    tr = _row_tile(rows)

    def body(g_ref, w_ref, m_ref, v_ref, d_ref, nm_ref, nv_ref):
        d_ref[...], nm_ref[...], nv_ref[...] = _adamw_math(g_ref[...], w_ref[...], m_ref[...], v_ref[...])

    blk = pl.BlockSpec((tr, width), _row)
    return pl.pallas_call(
        body, grid=(rows // tr,), in_specs=[blk] * 4, out_specs=[blk] * 3, out_shape=[SDS((rows, width), F32)] * 3,
        name="adamw_" + name, compiler_params=_params(("arbitrary",), 32))(g, w, m, v)


SMALL = ("w_spatial", "ln_pre_mix", "ln_post_mix", "ln_pre_ffn", "ln_post_ffn", "b_pe_gate",
         "attn_out_norm", "sgu_out_norm", "b_spatial", "sgu_ln_g", "sgu_ln_b")
SMALL_GROUPS = ((128, ("w_spatial", "b_spatial", "sgu_ln_g", "sgu_ln_b")),
                (512, ("attn_out_norm", "sgu_out_norm")),
                (1024, ("ln_post_mix", "ln_pre_ffn", "ln_post_ffn", "b_pe_gate")))
SMALL_LATE = "ln_pre_mix"
SMALL_SIZE = dict(w_spatial=N_GROUPS * CHUNK * CHUNK, b_spatial=N_GROUPS * CHUNK, sgu_ln_g=GROUP_DIM, sgu_ln_b=GROUP_DIM,
                  attn_out_norm=ATTN_W, sgu_out_norm=SGU_W, ln_pre_mix=D_MODEL, ln_post_mix=D_MODEL, ln_pre_ffn=D_MODEL,
                  ln_post_ffn=D_MODEL, b_pe_gate=D_MODEL)
SUBLANES = 8
ROW_SHARDED = ("w_out", "w_down", "w_pe_gate")
COL_SHARDED = ("w_in", "w_gate_up", "w_pe_proj")
WEIGHTS = ("ln_pre_mix", "w_in", "sgu_ln_g", "sgu_ln_b", "w_spatial", "b_spatial", "attn_out_norm", "sgu_out_norm",
           "w_out", "ln_post_mix", "ln_pre_ffn", "w_gate_up", "w_down", "ln_post_ffn", "w_pe_gate", "b_pe_gate",
           "w_pe_proj")


def _group_rows(width, names, extra=0):
    rows = sum(SMALL_SIZE[n] // width for n in names) + extra
    return -(-rows // SUBLANES) * SUBLANES


def _pack_small_grads(gs, loss_term):
    packed = []
    for width, names in SMALL_GROUPS:
        rows = [gs[n].reshape(-1, width) for n in names]
        extra = int(width == D_MODEL)
        if extra:
            rows.append(jnp.full((1, width), loss_term, F32))
        used = sum(r.shape[0] for r in rows)
        rows.append(jnp.zeros((_group_rows(width, names, extra) - used, width), F32))
        packed.append(jnp.concatenate(rows, axis=0))
    return packed


def _small_adamw(arrived, arrived_late, w, m, v):
    names = [n for _, ns in SMALL_GROUPS for n in ns] + [SMALL_LATE]
    n_groups = len(SMALL_GROUPS)

    def body(*refs):
        group_refs, late_ref = refs[:n_groups], refs[n_groups]
        state = refs[n_groups + 1:n_groups + 1 + 3 * len(names)]
        outs = refs[n_groups + 1 + 3 * len(names):]
        sums = [_sum_parts(r) for r in group_refs]

        def update(name, g):
            i = names.index(name)
            w_ref, m_ref, v_ref = state[3 * i:3 * i + 3]
            delta, nm, nv = _adamw_math(g, w_ref[...].reshape(g.shape), m_ref[...].reshape(g.shape),
                                        v_ref[...].reshape(g.shape))
            for o_ref, val in zip(outs[4 * i:4 * i + 4], (g, delta, nm, nv)):
                o_ref[...] = val.reshape(o_ref.shape)

        for (width, group), total in zip(SMALL_GROUPS, sums):
            row = 0
            for name in group:
                rows = SMALL_SIZE[name] // width
                update(name, total[row:row + rows, :])
                row += rows
            if width == D_MODEL:
                outs[-1][...] = total[row:row + 1, :LANES]
        update(SMALL_LATE, _sum_parts(late_ref)[:1, :])

    state = [t[n] for n in names for t in (w, m, v)]
    plain = jax.ShapeDtypeStruct
    out_shape = [plain(w[n].shape, F32) for n in names for _ in range(4)] + [plain((1, LANES), F32)]
    outs = pl.pallas_call(body, out_shape=out_shape, name="adamw_small",
                          compiler_params=pltpu.CompilerParams(vmem_limit_bytes=32 * MIB))(*arrived, arrived_late, *state)
    return {n: outs[4 * i:4 * i + 4] for i, n in enumerate(names)}, outs[-1]


def _slabs(full):
    return full.reshape(N_DEV, full.shape[0] // N_DEV, full.shape[1])


def kernel(x, p, ln_pre_mix, w_in, sgu_ln_g, sgu_ln_b, w_spatial, b_spatial, attn_out_norm, sgu_out_norm, w_out, ln_post_mix, ln_pre_ffn, w_gate_up, w_down, ln_post_ffn, w_pe_gate, b_pe_gate, w_pe_proj, loss_target, m_ln_pre_mix, m_w_in, m_sgu_ln_g, m_sgu_ln_b, m_w_spatial, m_b_spatial, m_attn_out_norm, m_sgu_out_norm, m_w_out, m_ln_post_mix, m_ln_pre_ffn, m_w_gate_up, m_w_down, m_ln_post_ffn, m_w_pe_gate, m_b_pe_gate, m_w_pe_proj, v_ln_pre_mix, v_w_in, v_sgu_ln_g, v_sgu_ln_b, v_w_spatial, v_b_spatial, v_attn_out_norm, v_sgu_out_norm, v_w_out, v_ln_post_mix, v_ln_pre_ffn, v_w_gate_up, v_w_down, v_ln_post_ffn, v_w_pe_gate, v_b_pe_gate, v_w_pe_proj):
    given = dict(locals())
    w = {n: given[n] for n in WEIGHTS}
    m = {n: given["m_" + n] for n in WEIGHTS}
    v = {n: given["v_" + n] for n in WEIGHTS}
    shapes = {n: w[n].shape for n in WEIGHTS}
    xs, ps, target = x[0], p[0, 0], loss_target[0]

    shard = {n: w[n][0].astype(BF) for n in ROW_SHARDED}
    shard.update({n: w[n][0].T.astype(BF) for n in COL_SHARDED})
    sm = {n: w[n][0] for n in SMALL}
    sm = {n: (a.reshape(1, -1) if a.ndim == 1 else a) for n, a in sm.items()}
    slopes = jnp.broadcast_to((2.0 ** -(jnp.arange(8, dtype=F32) + 1.0)).reshape(4, 2, 1), (4, 2, LANES))
    slopes = jnp.concatenate([slopes, jnp.zeros((4, 6, LANES), F32)], axis=1)
    b_st = jnp.pad(sm["b_spatial"].T, ((0, 0), (0, LANES - N_GROUPS)))

    def full(gathered):
        return gathered.reshape(-1, gathered.shape[-1])

    w_in_t = full(_all_gather(shard["w_in"], "gather_w_in"))
    proj, a = _in_proj(xs, sm["ln_pre_mix"], w_in_t)
    later = ("w_out", "w_gate_up", "w_down", "w_pe_gate", "w_pe_proj")
    attn, lse, gathered = _attn_fwd(proj, slopes, [shard[n] for n in later])
    w_out_f, w_gu_t, w_down_f, w_peg_f, w_pep_t = [full(g) for g in gathered]
    sgu = _sgu_fwd(proj, sm["sgu_ln_g"], sm["sgu_ln_b"], sm["w_spatial"], b_st)
    groups, mixed, h1, f = _out_proj(attn, sgu, xs, sm["attn_out_norm"], sm["sgu_out_norm"], w_out_f,
                                     sm["ln_post_mix"], sm["ln_pre_ffn"])
    g, u, act = _gate_up(f, w_gu_t)
    y, h2 = _down_proj(act, w_down_f, h1, sm["ln_post_ffn"])
    dh2, dy, dpp, dpre, h2b, pb, loss_cols, db_peg, d_pff = _pe_loss_and_bwd(
        h2, ps, target, y, w_peg_f, sm["b_pe_gate"], w_pep_t, sm["ln_post_ffn"])
    loss_term = 0.5 * jnp.sum(loss_cols) * (1.0 / D_MODEL)

    arrived = {}
    g_pep_t = _weight_grad(dpp, pb, "w_pe_proj")
    g_peg = _weight_grad(h2b, dpre, "w_pe_gate")
    g_down = _weight_grad(act, dy, "w_down")
    dg, du, (arrived["w_pe_proj"], arrived["w_pe_gate"]) = _down_bwd(dy, w_down_f, g, u, [_slabs(g_pep_t), _slabs(g_peg)])
    g_gu_t = _weight_grad(dg, f, "w_gate", rows=2 * D_FF)
    g_gu_t = _weight_grad(du, f, "w_up", into=g_gu_t, row_tile=FF_TILES, rows=2 * D_FF)
    dh1, dmx, d_pf, d_pm, (arrived["w_down"],) = _ffn_in_bwd(dg, du, w_gu_t, h1, dh2, mixed, sm["ln_pre_ffn"],
                                                            sm["ln_post_mix"], [_slabs(g_down)])
    g_out = _weight_grad(groups, dmx, "w_out")
    dattn, stats, dsgu, d_ga, d_gs = _out_bwd(dmx, w_out_f, attn, lse, sgu, sm["attn_out_norm"], sm["sgu_out_norm"])
    duz, d_ws, d_bst, d_lg, d_lb = _sgu_bwd(proj, dsgu, sm["sgu_ln_g"], sm["sgu_ln_b"], sm["w_spatial"], b_st)
    gs = dict(sgu_ln_g=d_lg, sgu_ln_b=d_lb, w_spatial=d_ws, b_spatial=d_bst[:, :N_GROUPS].T, attn_out_norm=d_ga,
              sgu_out_norm=d_gs, ln_post_mix=d_pm, ln_pre_ffn=d_pf, ln_post_ffn=d_pff, b_pe_gate=db_peg)
    small_grads = _pack_small_grads(gs, loss_term)
    dqkv, (arrived["w_gate_up"], arrived["w_out"], *arrived_small) = _attn_bwd(
        proj, dattn, stats, slopes, [_slabs(g_gu_t), _slabs(g_out), *small_grads],
        [True, True] + [False] * len(small_grads))
    g_in_t = _grad_w_in([dqkv, duz], a)
    grad_x, d_g1, (arrived["w_in"],) = _in_bwd(
        [dqkv, duz], w_in_t, xs, dh1, sm["ln_pre_mix"], [_slabs(g_in_t)], [True])
    (arrived_late,) = _scatter_call([jnp.pad(d_g1, ((0, SUBLANES - 1), (0, 0)))], [False], "ln_pre_mix_grad_exchange")

    res = {}
    for n in ROW_SHARDED:
        res[n] = _reduce_adamw(arrived[n], w[n][0], m[n][0], v[n][0], n)
    for n in ("w_in", "w_gate_up"):
        res[n] = [t.T for t in _reduce_adamw(arrived[n], w[n][0].T, m[n][0].T, v[n][0].T, n)]
    for n in ("w_pe_proj",):
        grad = _reduce(arrived[n], n).T
        res[n] = (grad, *_adamw(grad, w[n][0], m[n][0], v[n][0], n))
    small, loss_row = _small_adamw(arrived_small, arrived_late, w, m, v)

    out = []
    for k in range(4):
        out += [res[n][k][None] if n in res else small[n][k] for n in WEIGHTS]
    return (loss_row[0, 0], grad_x[None], *out)
```

```python
import math

import jax
import jax.numpy as jnp
from jax import lax
from jax.experimental import pallas as pl
from jax.experimental.pallas import tpu as pltpu

F32 = jnp.float32
BF = jnp.bfloat16


def SDS(shape, dtype):
    return pltpu.HBM(tuple(shape), dtype)

D_MODEL = 1024
ATTN_W = 512
SGU_W = 512
HEAD_DIM = 64
N_GROUPS = 4
GROUP_DIM = 128
CHUNK = 128
D_FF = 2816
PLE_DIM = 256
PROJ = 3 * ATTN_W + 2 * SGU_W
DILATIONS = (1, 4, 16)
QBLK = 128
EPS = 1e-6
NEG = -1e30
N_DEV = 8
LANES = 128

ADAM_LR = 0.001
ADAM_B1 = 0.9
ADAM_B2 = 0.999
ADAM_EPS = 1e-08
ADAM_WD = 0.01
ADAM_STEP = 10

MIB = 2 ** 20
MESH_ID = pl.DeviceIdType.MESH
HBM = pl.BlockSpec(memory_space=pl.ANY)


def _params(sem, vmem_mib):
    return pltpu.CompilerParams(dimension_semantics=sem, vmem_limit_bytes=vmem_mib * MIB)


def _dot(a, b):
    return jnp.dot(a, b, preferred_element_type=F32)


def _dot_nt(a, b):
    return lax.dot_general(a, b, (((1,), (1,)), ((), ())), preferred_element_type=F32)


def _dot_tn(a, b):
    return lax.dot_general(a, b, (((0,), (0,)), ((), ())), preferred_element_type=F32)


def _rstd(x):
    return lax.rsqrt(jnp.mean(x * x, axis=-1, keepdims=True) + EPS)


def _rms_bwd(dy, x, g):
    r = _rstd(x)
    n = x * r
    dn = dy * g
    dx = r * (dn - n * jnp.mean(dn * n, axis=-1, keepdims=True))
    return dx, dy * n


def _colsum(v):
    return jnp.sum(v, axis=0, keepdims=True)


_G0 = math.sqrt(2.0 / math.pi)
_G1 = 0.044715


def _gelu(x):
    t = jnp.tanh(_G0 * (x + _G1 * x * x * x))
    return 0.5 * x * (1.0 + t), t


def _gelu_grad(x, t):
    return 0.5 * (1.0 + t) + 0.5 * x * (1.0 - t * t) * (_G0 * (1.0 + 3.0 * _G1 * x * x))


def _sigmoid(x):
    return 0.5 * jnp.tanh(0.5 * x) + 0.5


def _row(i):
    return (i, 0)


def _fixed(i):
    return (0, 0)


def _acc_init(step, *refs):
    @pl.when(step == 0)
    def _():
        for r in refs:
            r[...] = jnp.zeros_like(r)


FLIPS = [(dx, dy, dc) for dx in (0, 1) for dy in (0, 1) for dc in (0, 1)][1:]
DMA_SEMS = pltpu.SemaphoreType.DMA


def _mesh_pos():
    return lax.axis_index("x"), lax.axis_index("y"), lax.axis_index("c")


def _remote(src, dst, sems, n, to):
    return pltpu.make_async_remote_copy(src_ref=src, dst_ref=dst, send_sem=sems[0].at[n], recv_sem=sems[1].at[n],
                                        device_id=to, device_id_type=MESH_ID)


class _Scatter:
    def __init__(self, items, sems):
        x, y, c = _mesh_pos()
        me = 4 * x + 2 * y + c
        self.local, self.sends, self.arrivals = [], [], []
        for i, (src, dst, slabbed) in enumerate(items):
            self.local.append(pltpu.make_async_copy(src.at[me] if slabbed else src, dst.at[me], sems[2].at[i]))
            for k, (dx, dy, dc) in enumerate(FLIPS):
                to = (1 - x if dx else x, 1 - y if dy else y, 1 - c if dc else c)
                peer = 4 * to[0] + 2 * to[1] + to[2]
                out = src.at[peer] if slabbed else src
                self.sends.append(_remote(out, dst.at[me], sems, 7 * i + k, to))
                self.arrivals.append(_remote(out, dst.at[peer], sems, 7 * i + k, to))

    def start(self):
        for cp in self.local + self.sends:
            cp.start()

    def wait(self):
        for cp in self.arrivals:
            cp.wait_recv()
        for cp in self.sends:
            cp.wait_send()
        for cp in self.local:
            cp.wait()


def _scatter_sems(n):
    return [DMA_SEMS((7 * n,)), DMA_SEMS((7 * n,)), DMA_SEMS((n,))]


class _Gather:
    def __init__(self, items, sems):
        x, y, c = _mesh_pos()
        me, sibling = (x, y, c), (x, y, 1 - c)
        chips = [(1 - x, y), (x, 1 - y), (1 - x, 1 - y)]
        self.first, self.passed, self.from_chips, self.rest, self.local = [], [], [], [], []
        for i, (src, dst) in enumerate(items):
            def slot(p, dst=dst):
                return dst.at[4 * p[0] + 2 * p[1] + p[2]]

            def copy(k, block, to, own=False, i=i, src=src, slot=slot):
                return _remote(src if own else slot(block), slot(block), sems, 7 * i + k, to)

            self.local.append(pltpu.make_async_copy(src, slot(me), sems[2].at[i]))
            self.first.append(copy(0, me, sibling, own=True))
            self.first += [copy(1 + j, me, (*chip, c), own=True) for j, chip in enumerate(chips)]
            self.passed += [copy(4 + j, (*chip, c), sibling) for j, chip in enumerate(chips)]
            self.from_chips += [copy(1 + j, (*chip, c), me) for j, chip in enumerate(chips)]
            self.rest.append(copy(0, sibling, me))
            self.rest += [copy(4 + j, (*chip, 1 - c), me) for j, chip in enumerate(chips)]

    def start(self):
        for cp in self.local + self.first:
            cp.start()

    def forward(self):
        for arrived, onward in zip(self.from_chips, self.passed):
            arrived.wait_recv()
            onward.start()

    def finish(self):
        for cp in self.rest:
            cp.wait_recv()
        for cp in self.first + self.passed:
            cp.wait_send()
        for cp in self.local:
            cp.wait()


def _all_gather(shard, name):
    def body(x_ref, out_ref, *sems):
        g = _Gather([(x_ref, out_ref)], sems)
        g.start()
        g.forward()
        g.finish()

    return pl.pallas_call(
        body, out_shape=SDS((N_DEV,) + shard.shape, shard.dtype), in_specs=[HBM], out_specs=HBM,
        scratch_shapes=_scatter_sems(1), name=name)(shard)


def _scatter_call(srcs, slabbed, name):
    n = len(srcs)

    def body(*refs):
        sc = _Scatter(list(zip(refs[:n], refs[n:2 * n], slabbed)), refs[2 * n:])
        sc.start()
        sc.wait()

    shapes = [SDS(s.shape if sl else (N_DEV,) + s.shape, s.dtype) for s, sl in zip(srcs, slabbed)]
    return pl.pallas_call(body, out_shape=shapes, in_specs=[HBM] * n, out_specs=[HBM] * n,
                          scratch_shapes=_scatter_sems(n), name=name)(*srcs)


def _in_proj(x, g1, w_in_t, tm=512):
    T = x.shape[0]

    def body(x_ref, g_ref, w_ref, proj_ref, a_ref):
        xv = x_ref[...]
        a = (xv * _rstd(xv) * g_ref[...]).astype(BF)
        a_ref[...] = a
        proj_ref[...] = _dot_nt(a, w_ref[...])

    return pl.pallas_call(
        body, grid=(T // tm,),
        in_specs=[pl.BlockSpec((tm, D_MODEL), _row), pl.BlockSpec((1, D_MODEL), _fixed),
                  pl.BlockSpec((PROJ, D_MODEL), _fixed)],
        out_specs=[pl.BlockSpec((tm, PROJ), _row), pl.BlockSpec((tm, D_MODEL), _row)],
        out_shape=[SDS((T, PROJ), F32), SDS((T, D_MODEL), BF)],
        name="in_proj", compiler_params=_params(("arbitrary",), 48))(x, g1, w_in_t)


ATTN_GROUP = 16


def _attn_bias(sl_ref, bias_ref):
    qi = lax.broadcasted_iota(jnp.int32, (QBLK, QBLK), 0)
    kj = lax.broadcasted_iota(jnp.int32, (QBLK, QBLK), 1)
    step = qi - kj
    for di, d in enumerate(DILATIONS):
        for j in range(2):
            sl = sl_ref[0, j:j + 1, :]
            cur = jnp.where(step >= 0, -sl * (step * d).astype(F32), NEG)
            prev = jnp.where(step <= 0, -sl * ((step + QBLK) * d).astype(F32), NEG)
            rows = slice(j * QBLK, (j + 1) * QBLK)
            bias_ref[di * 2, rows, :QBLK] = prev
            bias_ref[di * 2, rows, QBLK:] = cur
            bias_ref[di * 2 + 1, rows, :QBLK] = jnp.full((QBLK, QBLK), NEG, F32)
            bias_ref[di * 2 + 1, rows, QBLK:] = cur


def _stack_heads(x, lo):
    return jnp.concatenate([jnp.where(lo, x, 0.0), jnp.where(lo, 0.0, x)], axis=0).astype(BF)


def _unstack_heads(x, lo):
    return jnp.where(lo, x[:QBLK], x[QBLK:])


def _attn_rows(start, d, blocks=1):
    if d == 1:
        return pl.ds(pl.multiple_of(start, QBLK), blocks * QBLK)
    return pl.ds(start, blocks * QBLK, stride=d)


def _attn_group_index(i, nblk, d, group):
    per = nblk // d // group
    r = i // per
    n0 = (i % per) * group
    start = r + (d * QBLK) * n0
    pstart = jnp.maximum(start - d * QBLK, r)
    return start, pstart, n0 == 0


def _attn_plan(nblk, d):
    group = min(ATTN_GROUP, nblk // d)
    return group, max(1, min(ATTN_GROUP // group, d))


def _attn_fwd(proj, slopes, to_gather):
    T = proj.shape[0]
    nblk = T // QBLK
    n_g = len(to_gather)

    def body(q_ref, k_ref, v_ref, sl_ref, *rest):
        srcs, (o_ref, m_ref), dsts = rest[:n_g], rest[n_g:n_g + 2], rest[n_g + 2:2 * n_g + 2]
        sems, (l_ref, bias_ref) = rest[2 * n_g + 2:2 * n_g + 5], rest[2 * n_g + 5:]
        h = pl.program_id(0)

        @pl.when(h == 0)
        def _():
            _Gather(list(zip(srcs, dsts)), sems).start()

        @pl.when(h == pl.num_programs(0) - 1)
        def _():
            _Gather(list(zip(srcs, dsts)), sems).forward()

        _attn_bias(sl_ref, bias_ref)
        lo = lax.broadcasted_iota(jnp.int32, (1, LANES), 1) < HEAD_DIM

        order = list(enumerate(DILATIONS))[::-1]
        for di, d in order:
            group, segs = _attn_plan(nblk, d)
            fresh, last = di == order[0][0], di == order[-1][0]

            def step(i, carry, segs=segs, **kw):
                for s in range(segs):
                    segment(i * segs + s, **kw)
                return carry

            def segment(i, d=d, di=di, group=group, fresh=fresh, last=last):
                start, pstart, first = _attn_group_index(i, nblk, d, group)
                prows = _attn_rows(pstart, d)
                k_prev, v_prev = k_ref[prows, :].astype(BF), v_ref[prows, :].astype(BF)
                for b in range(group):
                    out = _attn_rows(start + b * (d * QBLK), d)
                    k_own, v_own = k_ref[out, :].astype(BF), v_ref[out, :].astype(BF)
                    k2, v2 = jnp.concatenate([k_prev, k_own], axis=0), jnp.concatenate([v_prev, v_own], axis=0)
                    k_prev, v_prev = k_own, v_own
                    bias = bias_ref[di * 2 + first.astype(jnp.int32)] if b == 0 else bias_ref[di * 2]
                    s = _dot_nt(_stack_heads(q_ref[out, :] * (HEAD_DIM ** -0.5), lo), k2) + bias
                    m = jnp.max(s, axis=-1, keepdims=True)
                    pr = jnp.exp(s - m)
                    m_b = _unstack_heads(m, lo)
                    l_b = _unstack_heads(jnp.sum(pr, axis=-1, keepdims=True), lo)
                    o_b = _unstack_heads(_dot(pr.astype(BF), v2), lo)
                    if fresh:
                        m_ref[out, :] = m_b
                        l_ref[out, :] = l_b
                        o_ref[out, :] = o_b
                        continue
                    m_o = m_ref[out, :]
                    m_n = jnp.maximum(m_o, m_b)
                    wa, wb = jnp.exp(m_o - m_n), jnp.exp(m_b - m_n)
                    l_n = wa * l_ref[out, :] + wb * l_b
                    o_n = wa * o_ref[out, :] + wb * o_b
                    if last:
                        m_ref[out, :] = m_n + jnp.log(l_n)
                        o_ref[out, :] = o_n / l_n
                    else:
                        m_ref[out, :] = m_n
                        l_ref[out, :] = l_n
                        o_ref[out, :] = o_n

            lax.fori_loop(0, nblk // (group * segs), step, 0)

        @pl.when(h == pl.num_programs(0) - 1)
        def _():
            _Gather(list(zip(srcs, dsts)), sems).finish()

    col = lambda base: pl.BlockSpec((T, LANES), lambda h: (0, base + h))
    tok = pl.BlockSpec((T, LANES), lambda h: (0, h))
    outs = pl.pallas_call(
        body, grid=(4,),
        in_specs=[col(0), col(4), col(8), pl.BlockSpec((1, 8, LANES), lambda h: (h, 0, 0))] + [HBM] * n_g,
        out_specs=[tok, tok] + [HBM] * n_g,
        out_shape=[SDS((T, ATTN_W), F32), SDS((T, ATTN_W), F32)]
        + [SDS((N_DEV,) + g.shape, g.dtype) for g in to_gather],
        scratch_shapes=_scatter_sems(n_g) + [pltpu.VMEM((T, LANES), F32), pltpu.VMEM((6, 2 * QBLK, 2 * QBLK), F32)],
        name="attn_fwd", compiler_params=_params(("arbitrary",), 56))(proj, proj, proj, slopes, *to_gather)
    return outs[0], outs[1], outs[2:]


def _sgu_norm(zv, ln_g, ln_b):
    gz, tz = _gelu(zv)
    mu = jnp.mean(gz, axis=-1, keepdims=True)
    xc = gz - mu
    rs = lax.rsqrt(jnp.mean(xc * xc, axis=-1, keepdims=True) + EPS)
    xhat = xc * rs
    return xhat * ln_g + ln_b, xhat, rs, tz


def _causal(w):
    i = lax.broadcasted_iota(jnp.int32, (CHUNK, CHUNK), 0)
    j = lax.broadcasted_iota(jnp.int32, (CHUNK, CHUNK), 1)
    return jnp.where(i >= j, w, 0.0)


def _sgu_fwd(proj, ln_g, ln_b, w_s, b_st, tm=512):
    T = proj.shape[0]

    def body(u_ref, z_ref, g_ref, b_ref, w_ref, bs_ref, out_ref):
        for g in range(N_GROUPS):
            wm = _causal(w_ref[g]).astype(BF)
            cols = slice(g * GROUP_DIM, (g + 1) * GROUP_DIM)
            for c in range(tm // CHUNK):
                rows = slice(c * CHUNK, (c + 1) * CHUNK)
                zn, _, _, _ = _sgu_norm(z_ref[rows, cols], g_ref[...], b_ref[...])
                mixed = _dot(wm, zn.astype(BF)) + bs_ref[:, g:g + 1]
                gu, _ = _gelu(u_ref[rows, cols])
                out_ref[rows, cols] = gu * mixed

    return pl.pallas_call(
        body, grid=(T // tm,),
        in_specs=[pl.BlockSpec((tm, SGU_W), lambda i: (i, 3)), pl.BlockSpec((tm, SGU_W), lambda i: (i, 4)),
                  pl.BlockSpec((1, GROUP_DIM), _fixed), pl.BlockSpec((1, GROUP_DIM), _fixed),
                  pl.BlockSpec((N_GROUPS, CHUNK, CHUNK), lambda i: (0, 0, 0)), pl.BlockSpec((CHUNK, LANES), _fixed)],
        out_specs=pl.BlockSpec((tm, SGU_W), _row),
        out_shape=SDS((T, SGU_W), F32),
        name="sgu_fwd", compiler_params=_params(("arbitrary",), 32))(proj, proj, ln_g, ln_b, w_s, b_st)


def _out_proj(attn, sgu, x, g_a, g_s, w_out, g_pm, g_pf, tm=512):
    T = x.shape[0]

    def body(a_ref, s_ref, x_ref, ga_ref, gs_ref, w_ref, gpm_ref, gpf_ref, grp_ref, mixed_ref, h1_ref, f_ref):
        av, sv = a_ref[...], s_ref[...]
        an = (av * _rstd(av) * ga_ref[...]).astype(BF)
        sn = (sv * _rstd(sv) * gs_ref[...]).astype(BF)
        grp_ref[:, :ATTN_W] = an
        grp_ref[:, ATTN_W:] = sn
        mixed = _dot(an, w_ref[:ATTN_W, :]) + _dot(sn, w_ref[ATTN_W:, :])
        mixed_ref[...] = mixed
        h1 = x_ref[...] + mixed * _rstd(mixed) * gpm_ref[...]
        h1_ref[...] = h1
        f_ref[...] = (h1 * _rstd(h1) * gpf_ref[...]).astype(BF)

    tok = lambda w: pl.BlockSpec((tm, w), _row)
    vec = lambda w: pl.BlockSpec((1, w), _fixed)
    return pl.pallas_call(
        body, grid=(T // tm,),
        in_specs=[tok(ATTN_W), tok(SGU_W), tok(D_MODEL), vec(ATTN_W), vec(SGU_W),
                  pl.BlockSpec((D_MODEL, D_MODEL), _fixed), vec(D_MODEL), vec(D_MODEL)],
        out_specs=[tok(D_MODEL)] * 4,
        out_shape=[SDS((T, D_MODEL), BF), SDS((T, D_MODEL), F32), SDS((T, D_MODEL), F32), SDS((T, D_MODEL), BF)],
        name="out_proj", compiler_params=_params(("arbitrary",), 48))(attn, sgu, x, g_a, g_s, w_out, g_pm, g_pf)


FF_TILE = 1408
FF_TILES = D_FF // FF_TILE
FF_CHUNK = 256


def _gate_up(f, w_gu_t, tm=512):
    T = f.shape[0]
    tn = FF_TILE

    def body(f_ref, wg_ref, wu_ref, g_ref, u_ref, act_ref):
        fv = f_ref[...]
        g = _dot_nt(fv, wg_ref[...])
        u = _dot_nt(fv, wu_ref[...])
        g_ref[...] = g.astype(BF)
        u_ref[...] = u.astype(BF)
        act_ref[...] = (g * _sigmoid(g) * u).astype(BF)

    ospec = pl.BlockSpec((tm, tn), lambda j, i: (i, j))
    return pl.pallas_call(
        body, grid=(FF_TILES, T // tm),
        in_specs=[pl.BlockSpec((tm, D_MODEL), lambda j, i: (i, 0)), pl.BlockSpec((tn, D_MODEL), lambda j, i: (j, 0)),
                  pl.BlockSpec((tn, D_MODEL), lambda j, i: (j + FF_TILES, 0))],
        out_specs=[ospec] * 3, out_shape=[SDS((T, D_FF), BF)] * 3,
        name="gate_up", compiler_params=_params(("arbitrary", "arbitrary"), 40))(f, w_gu_t, w_gu_t)


def _down_proj(act, w_down, h1, g_pff, tm=512):
    T = act.shape[0]

    def body(a_ref, w_ref, h1_ref, g_ref, y_ref, h2_ref):
        y = _dot(a_ref[...], w_ref[...])
        y_ref[...] = y
        h2_ref[...] = h1_ref[...] + y * _rstd(y) * g_ref[...]

    return pl.pallas_call(
        body, grid=(T // tm,),
        in_specs=[pl.BlockSpec((tm, D_FF), _row), pl.BlockSpec((D_FF, D_MODEL), _fixed),
                  pl.BlockSpec((tm, D_MODEL), _row), pl.BlockSpec((1, D_MODEL), _fixed)],
        out_specs=[pl.BlockSpec((tm, D_MODEL), _row)] * 2,
        out_shape=[SDS((T, D_MODEL), F32)] * 2,
        name="down_proj", compiler_params=_params(("arbitrary",), 48))(act, w_down, h1, g_pff)


def _pe_loss_and_bwd(h2, p, target, y, w_peg, b_peg, w_pep_t, g_pff, tm=512):
    T = h2.shape[0]

    def body(h2_ref, p_ref, t_ref, y_ref, wg_ref, b_ref, wp_ref, g_ref,
             dh2_ref, dy_ref, gpeg_ref, gpep_ref, loss_ref, db_ref, dg_ref):
        _acc_init(pl.program_id(0), gpeg_ref, gpep_ref, loss_ref, db_ref, dg_ref)
        h2v = h2_ref[...]
        h2b = h2v.astype(BF)
        pb = p_ref[...].astype(BF)
        gate = _sigmoid(_dot(h2b, wg_ref[...]) + b_ref[...])
        pp = _dot_nt(pb, wp_ref[...])
        diff = h2v + gate * pp - t_ref[...]
        loss_ref[...] += _colsum(diff * diff)
        dh3 = diff * (1.0 / D_MODEL)
        dpre = dh3 * pp * (gate * (1.0 - gate))
        dpre_b = dpre.astype(BF)
        db_ref[...] += _colsum(dpre)
        gpeg_ref[...] += _dot_tn(h2b, dpre_b)
        gpep_ref[...] += _dot_tn((dh3 * gate).astype(BF), pb)
        dh2 = dh3 + _dot_nt(dpre_b, wg_ref[...])
        dh2_ref[...] = dh2
        dy, dg = _rms_bwd(dh2, y_ref[...], g_ref[...])
        dy_ref[...] = dy.astype(BF)
        dg_ref[...] += _colsum(dg)

    tok = lambda w: pl.BlockSpec((tm, w), _row)
    vec = pl.BlockSpec((1, D_MODEL), _fixed)
    wg = pl.BlockSpec((D_MODEL, D_MODEL), _fixed)
    wp = pl.BlockSpec((D_MODEL, PLE_DIM), _fixed)
    return pl.pallas_call(
        body, grid=(T // tm,),
        in_specs=[tok(D_MODEL), tok(PLE_DIM), tok(D_MODEL), tok(D_MODEL), wg, vec, wp, vec],
        out_specs=[tok(D_MODEL), tok(D_MODEL), wg, wp, vec, vec, vec],
        out_shape=[SDS((T, D_MODEL), F32), SDS((T, D_MODEL), BF), SDS((D_MODEL, D_MODEL), F32),
                   SDS((D_MODEL, PLE_DIM), F32)] + [SDS((1, D_MODEL), F32)] * 3,
        name="pe_loss_and_bwd", compiler_params=_params(("arbitrary",), 56))(
            h2, p, target, y, w_peg, b_peg, w_pep_t, g_pff)


def _weight_grad(a, dy, name, into=None, row_tile=0, rows=None, tk=512):
    n = dy.shape[1]
    tn = min(n, 1024)
    T, ka = a.shape
    tka = FF_TILE if ka == D_FF else min(ka, 1024)
    rows = ka if rows is None else rows

    def body(a_ref, dy_ref, *rest):
        out_ref = rest[-1]
        _acc_init(pl.program_id(2), out_ref)
        out_ref[...] += _dot_tn(a_ref[...].astype(BF), dy_ref[...].astype(BF))

    carried = [] if into is None else [into]
    return pl.pallas_call(
        body, grid=(ka // tka, n // tn, T // tk),
        in_specs=[pl.BlockSpec((tk, tka), lambda i, j, k: (k, i)), pl.BlockSpec((tk, tn), lambda i, j, k: (k, j))]
        + [HBM] * len(carried),
        out_specs=pl.BlockSpec((tka, tn), lambda i, j, k: (i + row_tile, j)),
        out_shape=SDS((rows, n), F32), input_output_aliases={2: 0} if carried else {},
        name="grad_" + name, compiler_params=_params(("arbitrary",) * 3, 40))(a, dy, *carried)


def _grad_w_in(dparts, a, tk=512):
    T = a.shape[0]

    def body(*refs):
        d_refs, a_ref, out_ref, acc_ref = refs[:len(dparts)], refs[-3], refs[-2], refs[-1]
        k = pl.program_id(0)
        _acc_init(k, acc_ref)
        cols = [r[part].astype(BF) for r in d_refs for part in range(r.shape[0])]
        acc_ref[...] += _dot_tn(jnp.concatenate(cols, axis=1), a_ref[...])

        @pl.when(k == pl.num_programs(0) - 1)
        def _():
            out_ref[...] = acc_ref[...].astype(BF)

    return pl.pallas_call(
        body, grid=(T // tk,),
        in_specs=[pl.BlockSpec((d.shape[0], tk, d.shape[2]), lambda k: (0, k, 0)) for d in dparts]
        + [pl.BlockSpec((tk, D_MODEL), lambda k: (k, 0))],
        out_specs=pl.BlockSpec((PROJ, D_MODEL), lambda k: (0, 0)),
        out_shape=SDS((PROJ, D_MODEL), BF), scratch_shapes=[pltpu.VMEM((PROJ, D_MODEL), F32)],
        name="grad_w_in", compiler_params=_params(("arbitrary",), 48))(*dparts, a)


def _down_bwd(dy, w_down, g, u, to_send, tm=512):
    T = dy.shape[0]
    n_s = len(to_send)

    def body(dy_ref, w_ref, g_ref, u_ref, *rest):
        srcs, (dg_ref, du_ref), dsts, sems = rest[:n_s], rest[n_s:n_s + 2], rest[n_s + 2:2 * n_s + 2], rest[2 * n_s + 2:]
        i = pl.program_id(0)
        items = list(zip(srcs, dsts, [True] * n_s))

        @pl.when(i == 0)
        def _():
            _Scatter(items, sems).start()

        dyv = dy_ref[...]
        for c in range(D_FF // FF_CHUNK):
            cols = slice(c * FF_CHUNK, (c + 1) * FF_CHUNK)
            dact = _dot_nt(dyv, w_ref[cols, :])
            gv = g_ref[:, cols].astype(F32)
            uv = u_ref[:, cols].astype(F32)
            s = _sigmoid(gv)
            ds = dact * s
            dg_ref[:, cols] = (ds * uv * (1.0 + gv * (1.0 - s))).astype(BF)
            du_ref[:, cols] = (ds * gv).astype(BF)

        @pl.when(i == pl.num_programs(0) - 1)
        def _():
            _Scatter(items, sems).wait()

    tile = pl.BlockSpec((tm, D_FF), _row)
    outs = pl.pallas_call(
        body, grid=(T // tm,),
        in_specs=[pl.BlockSpec((tm, D_MODEL), _row),
                  pl.BlockSpec((D_FF, D_MODEL), _fixed, pipeline_mode=pl.Buffered(1)), tile, tile] + [HBM] * n_s,
        out_specs=[tile, tile] + [HBM] * n_s,
        out_shape=[SDS((T, D_FF), BF)] * 2 + [SDS(s.shape, s.dtype) for s in to_send],
        scratch_shapes=_scatter_sems(n_s),
        name="down_bwd", compiler_params=_params(("arbitrary",), 48))(dy, w_down, g, u, *to_send)
    return outs[0], outs[1], outs[2:]


def _ffn_in_bwd(dg, du, w_gu_t, h1, dh2, mixed, g_pf, g_pm, to_send, tm=512):
    T = h1.shape[0]
    n_s = len(to_send)

    def body(dg_ref, du_ref, wg_ref, wu_ref, h1_ref, dh2_ref, mx_ref, gpf_ref, gpm_ref, *rest):
        srcs, outs, dsts, sems = rest[:n_s], rest[n_s:n_s + 4], rest[n_s + 4:2 * n_s + 4], rest[2 * n_s + 4:]
        dh1_ref, dmx_ref, dgpf_ref, dgpm_ref = outs
        i = pl.program_id(0)
        items = list(zip(srcs, dsts, [True] * n_s))
        _acc_init(i, dgpf_ref, dgpm_ref)

        @pl.when(i == 0)
        def _():
            _Scatter(items, sems).start()

        df = _dot(dg_ref[...], wg_ref[...]) + _dot(du_ref[...], wu_ref[...])
        dx, dgf = _rms_bwd(df, h1_ref[...], gpf_ref[...])
        dh1 = dh2_ref[...] + dx
        dh1_ref[...] = dh1
        dmx, dgm = _rms_bwd(dh1, mx_ref[...], gpm_ref[...])
        dmx_ref[...] = dmx.astype(BF)
        dgpf_ref[...] += _colsum(dgf)
        dgpm_ref[...] += _colsum(dgm)

        @pl.when(i == pl.num_programs(0) - 1)
        def _():
            _Scatter(items, sems).wait()

    tok = lambda w: pl.BlockSpec((tm, w), _row)
    vec = pl.BlockSpec((1, D_MODEL), _fixed)
    outs = pl.pallas_call(
        body, grid=(T // tm,),
        in_specs=[tok(D_FF), tok(D_FF), pl.BlockSpec((D_FF, D_MODEL), lambda i: (0, 0), pipeline_mode=pl.Buffered(1)),
                  pl.BlockSpec((D_FF, D_MODEL), lambda i: (1, 0), pipeline_mode=pl.Buffered(1)),
                  tok(D_MODEL), tok(D_MODEL), tok(D_MODEL), vec, vec]
        + [HBM] * n_s,
        out_specs=[tok(D_MODEL), tok(D_MODEL), vec, vec] + [HBM] * n_s,
        out_shape=[SDS((T, D_MODEL), F32), SDS((T, D_MODEL), BF), SDS((1, D_MODEL), F32), SDS((1, D_MODEL), F32)]
        + [SDS(s.shape, s.dtype) for s in to_send],
        scratch_shapes=_scatter_sems(n_s),
        name="ffn_in_bwd", compiler_params=_params(("arbitrary",), 56))(
            dg, du, w_gu_t, w_gu_t, h1, dh2, mixed, g_pf, g_pm, *to_send)
    return outs[0], outs[1], outs[2], outs[3], outs[4:]


STAT_LANES = HEAD_DIM // 2


def _out_bwd(dmx, w_out, attn, lse, sgu, g_a, g_s, tm=512):
    T = attn.shape[0]

    def body(dm_ref, w_ref, a_ref, l_ref, s_ref, ga_ref, gs_ref, da_ref, st_ref, ds_ref, dga_ref, dgs_ref):
        _acc_init(pl.program_id(0), dga_ref, dgs_ref)
        dgr = _dot_nt(dm_ref[...], w_ref[...])
        av = a_ref[...]
        da, dga = _rms_bwd(dgr[:, :ATTN_W], av, ga_ref[...])
        ds, dgs = _rms_bwd(dgr[:, ATTN_W:], s_ref[...], gs_ref[...])
        da_ref[...] = da
        ds_ref[...] = ds
        dga_ref[...] += _colsum(dga)
        dgs_ref[...] += _colsum(dgs)
        lane = lax.broadcasted_iota(jnp.int32, (1, LANES), 1)
        lo = lane < HEAD_DIM
        first = (lane % HEAD_DIM) < STAT_LANES
        prod = da * av
        for c in range(ATTN_W // LANES):
            cols = slice(c * LANES, (c + 1) * LANES)
            pc = prod[:, cols]
            delta = jnp.where(lo, jnp.sum(jnp.where(lo, pc, 0.0), axis=-1, keepdims=True),
                              jnp.sum(jnp.where(lo, 0.0, pc), axis=-1, keepdims=True))
            st_ref[:, cols] = jnp.where(first, l_ref[:, cols], delta)

    tok = lambda w: pl.BlockSpec((tm, w), _row)
    vec = lambda w: pl.BlockSpec((1, w), _fixed)
    return pl.pallas_call(
        body, grid=(T // tm,),
        in_specs=[tok(D_MODEL), pl.BlockSpec((D_MODEL, D_MODEL), _fixed), tok(ATTN_W), tok(ATTN_W), tok(SGU_W),
                  vec(ATTN_W), vec(SGU_W)],
        out_specs=[tok(ATTN_W), tok(ATTN_W), tok(SGU_W), vec(ATTN_W), vec(SGU_W)],
        out_shape=[SDS((T, ATTN_W), F32), SDS((T, ATTN_W), F32), SDS((T, SGU_W), F32), SDS((1, ATTN_W), F32),
                   SDS((1, SGU_W), F32)],
        name="out_bwd", compiler_params=_params(("arbitrary",), 48))(dmx, w_out, attn, lse, sgu, g_a, g_s)


def _sgu_bwd(proj, dsgu, ln_g, ln_b, w_s, b_st, tm=512):
    T = proj.shape[0]

    def body(u_ref, z_ref, ds_ref, g_ref, b_ref, w_ref, bs_ref,
             duz_ref, dw_ref, dbs_ref, dlg_ref, dlb_ref, dbacc_ref):
        du_ref, dz_ref = duz_ref.at[0], duz_ref.at[1]
        step = pl.program_id(0)
        _acc_init(step, dw_ref, dbs_ref, dlg_ref, dlb_ref, dbacc_ref)
        lng, lnb = g_ref[...], b_ref[...]
        for g in range(N_GROUPS):
            wm = _causal(w_ref[g]).astype(BF)
            cols = slice(g * GROUP_DIM, (g + 1) * GROUP_DIM)
            for c in range(tm // CHUNK):
                rows = slice(c * CHUNK, (c + 1) * CHUNK)
                zv, uv, dout = z_ref[rows, cols], u_ref[rows, cols], ds_ref[rows, cols]
                zn, xhat, rs, tz = _sgu_norm(zv, lng, lnb)
                znb = zn.astype(BF)
                mixed = _dot(wm, znb) + bs_ref[:, g:g + 1]
                gu, tu = _gelu(uv)
                du_ref[rows, cols] = (dout * mixed * _gelu_grad(uv, tu)).astype(BF)
                dmix = dout * gu
                dmb = dmix.astype(BF)
                dw_ref[g] += _causal(_dot_nt(dmb, znb))
                dbacc_ref[g] += dmix
                dzn = _dot_tn(wm, dmb)
                dlg_ref[...] += _colsum(dzn * xhat)
                dlb_ref[...] += _colsum(dzn)
                dxh = dzn * lng
                dgz = rs * (dxh - jnp.mean(dxh, axis=-1, keepdims=True)
                            - xhat * jnp.mean(dxh * xhat, axis=-1, keepdims=True))
                dz_ref[rows, cols] = (dgz * _gelu_grad(zv, tz)).astype(BF)

        @pl.when(step == pl.num_programs(0) - 1)
        def _():
            lane = lax.broadcasted_iota(jnp.int32, (CHUNK, LANES), 1)
            acc = jnp.zeros((CHUNK, LANES), F32)
            for g in range(N_GROUPS):
                acc = jnp.where(lane == g, jnp.sum(dbacc_ref[g], axis=-1, keepdims=True), acc)
            dbs_ref[...] = acc

    tok = pl.BlockSpec((tm, SGU_W), _row)
    vec = pl.BlockSpec((1, GROUP_DIM), _fixed)
    wsp = pl.BlockSpec((N_GROUPS, CHUNK, CHUNK), lambda i: (0, 0, 0))
    sq = pl.BlockSpec((CHUNK, LANES), _fixed)
    return pl.pallas_call(
        body, grid=(T // tm,),
        in_specs=[pl.BlockSpec((tm, SGU_W), lambda i: (i, 3)), pl.BlockSpec((tm, SGU_W), lambda i: (i, 4)), tok,
                  vec, vec, wsp, sq],
        out_specs=[pl.BlockSpec((2, tm, SGU_W), lambda i: (0, i, 0)), wsp, sq, vec, vec],
        out_shape=[SDS((2, T, SGU_W), BF), SDS((N_GROUPS, CHUNK, CHUNK), F32),
                   SDS((CHUNK, LANES), F32), SDS((1, GROUP_DIM), F32), SDS((1, GROUP_DIM), F32)],
        scratch_shapes=[pltpu.VMEM((N_GROUPS, CHUNK, LANES), F32)],
        name="sgu_bwd", compiler_params=_params(("arbitrary",), 32))(proj, proj, dsgu, ln_g, ln_b, w_s, b_st)


def _attn_bwd(proj, do, stats, slopes, to_send, slabbed):
    T = proj.shape[0]
    nblk = T // QBLK
    n_s = len(to_send)

    def body(q_ref, k_ref, v_ref, do_ref, st_ref, sl_ref, *rest):
        srcs, d_ref, dsts = rest[:n_s], rest[n_s], rest[n_s + 1:2 * n_s + 1]
        sems, bias_ref = rest[2 * n_s + 1:2 * n_s + 4], rest[2 * n_s + 4]
        dq_ref, dk_ref, dv_ref = d_ref.at[0], d_ref.at[1], d_ref.at[2]
        h = pl.program_id(0)
        items = list(zip(srcs, dsts, slabbed))

        @pl.when(h == 0)
        def _():
            _Scatter(items, sems).start()

        _attn_bias(sl_ref, bias_ref)
        lo = lax.broadcasted_iota(jnp.int32, (1, LANES), 1) < HEAD_DIM
        scale = HEAD_DIM ** -0.5
        d_ref[...] = jnp.zeros_like(d_ref)

        for di, d in enumerate(DILATIONS):
            group, segs = _attn_plan(nblk, d)

            def step(i, carry, segs=segs, **kw):
                for s in range(segs):
                    segment(i * segs + s, **kw)
                return carry

            def segment(i, d=d, di=di, group=group):
                start, pstart, first = _attn_group_index(i, nblk, d, group)
                rows, prows = _attn_rows(start, d, group), _attn_rows(pstart, d)
                q = q_ref[rows, :] * scale
                k = jnp.concatenate([k_ref[prows, :], k_ref[rows, :]], axis=0).astype(BF)
                v = jnp.concatenate([v_ref[prows, :], v_ref[rows, :]], axis=0).astype(BF)
                dov = do_ref[rows, :]
                stats = st_ref[rows, :]
                masks = [lo, ~lo]
                qm = [jnp.where(masks[j], q, 0.0).astype(BF) for j in range(2)]
                dom = [jnp.where(masks[j], dov, 0.0).astype(BF) for j in range(2)]
                for b in range(group):
                    qb = slice(b * QBLK, (b + 1) * QBLK)
                    kb = slice(b * QBLK, (b + 2) * QBLK)
                    which = di * 2 + first.astype(jnp.int32) if b == 0 else di * 2
                    dq_parts, prs, dss = [], [], []
                    for j in range(2):
                        bias = bias_ref[which, j * QBLK:(j + 1) * QBLK, :]
                        lj = stats[qb, j * HEAD_DIM:j * HEAD_DIM + 1]
                        delta = stats[qb, j * HEAD_DIM + STAT_LANES:j * HEAD_DIM + STAT_LANES + 1]
                        pr = jnp.exp(_dot_nt(qm[j][qb], k[kb]) + bias - lj)
                        ds = (pr * (_dot_nt(dom[j][qb], v[kb]) - delta)).astype(BF)
                        dq_parts.append(_dot(ds, k[kb]))
                        prs.append(pr.astype(BF))
                        dss.append(ds)
                    dk_b = _dot_tn(jnp.concatenate(dss, axis=0), jnp.concatenate([qm[0][qb], qm[1][qb]], axis=0))
                    dv_b = _dot_tn(jnp.concatenate(prs, axis=0), jnp.concatenate([dom[0][qb], dom[1][qb]], axis=0))
                    own = _attn_rows(start + b * (d * QBLK), d)
                    dq_ref[own, :] += jnp.where(lo, dq_parts[0], dq_parts[1]) * scale
                    if b == 0:
                        dk_ref[prows, :] += dk_b[:QBLK]
                        dv_ref[prows, :] += dv_b[:QBLK]
                        dk_ref[own, :] += dk_b[QBLK:]
                        dv_ref[own, :] += dv_b[QBLK:]
                    else:
                        two = _attn_rows(start + (b - 1) * (d * QBLK), d, 2)
                        dk_ref[two, :] += dk_b
                        dv_ref[two, :] += dv_b

            lax.fori_loop(0, nblk // (group * segs), step, 0)

        @pl.when(h == pl.num_programs(0) - 1)
        def _():
            _Scatter(items, sems).wait()

    col = lambda base: pl.BlockSpec((T, LANES), lambda h: (0, base + h))
    outs = pl.pallas_call(
        body, grid=(4,),
        in_specs=[col(0), col(4), col(8), col(0), col(0), pl.BlockSpec((1, 8, LANES), lambda h: (h, 0, 0))]
        + [HBM] * n_s,
        out_specs=[pl.BlockSpec((3, T, LANES), lambda h: (0, 0, h), pipeline_mode=pl.Buffered(1))] + [HBM] * n_s,
        out_shape=[SDS((3, T, ATTN_W), F32)]
        + [SDS(s.shape if sl else (N_DEV,) + s.shape, s.dtype) for s, sl in zip(to_send, slabbed)],
        scratch_shapes=_scatter_sems(n_s) + [pltpu.VMEM((6, 2 * QBLK, 2 * QBLK), F32)],
        name="attn_bwd", compiler_params=_params(("arbitrary",), 60))(proj, proj, proj, do, stats, slopes, *to_send)
    return outs[0], outs[1:]


def _in_bwd(dparts, w_in_t, x, dh1, g1, to_send, slabbed, tm=512):
    T = x.shape[0]
    n = len(dparts)
    n_s = len(to_send)
    w = ATTN_W

    def body(*refs):
        d_refs, (w_ref, x_ref, dh1_ref, g_ref), rest = refs[:n], refs[n:n + 4], refs[n + 4:]
        srcs, (dx_ref, dg_ref), dsts, sems = rest[:n_s], rest[n_s:n_s + 2], rest[n_s + 2:2 * n_s + 2], rest[2 * n_s + 2:]
        step = pl.program_id(0)
        items = list(zip(srcs, dsts, slabbed))
        _acc_init(step, dg_ref)

        @pl.when(step == 0)
        def _():
            _Scatter(items, sems).start()

        da = None
        col = 0
        for r in d_refs:
            for part in range(r.shape[0]):
                t = _dot(r[part].astype(BF), w_ref[col * w:(col + 1) * w, :])
                da = t if da is None else da + t
                col += 1
        dx, dg = _rms_bwd(da, x_ref[...], g_ref[...])
        dx_ref[...] = dh1_ref[...] + dx
        dg_ref[...] += _colsum(dg)

        @pl.when(step == pl.num_programs(0) - 1)
        def _():
            _Scatter(items, sems).wait()

    tok = lambda c: pl.BlockSpec((tm, c), _row)
    vec = pl.BlockSpec((1, D_MODEL), _fixed)
    outs = pl.pallas_call(
        body, grid=(T // tm,),
        in_specs=[pl.BlockSpec((d.shape[0], tm, w), lambda i: (0, i, 0)) for d in dparts]
        + [pl.BlockSpec((PROJ, D_MODEL), _fixed), tok(D_MODEL), tok(D_MODEL), vec] + [HBM] * n_s,
        out_specs=[tok(D_MODEL), vec] + [HBM] * n_s,
        out_shape=[SDS((T, D_MODEL), F32), SDS((1, D_MODEL), F32)]
        + [SDS(s.shape if sl else (N_DEV,) + s.shape, s.dtype) for s, sl in zip(to_send, slabbed)],
        scratch_shapes=_scatter_sems(n_s),
        name="in_bwd", compiler_params=_params(("arbitrary",), 52))(*dparts, w_in_t, x, dh1, g1, *to_send)
    return outs[0], outs[1], outs[2:]


def _sum_parts(p_ref):
    g = p_ref[0].astype(F32)
    for s in range(1, N_DEV):
        g = g + p_ref[s].astype(F32)
    return g


def _adamw_math(g, w, m, v):
    nm = ADAM_B1 * m + (1.0 - ADAM_B1) * g
    nv = ADAM_B2 * v + (1.0 - ADAM_B2) * (g * g)
    m_hat = nm / (1.0 - ADAM_B1 ** ADAM_STEP)
    v_hat = nv / (1.0 - ADAM_B2 ** ADAM_STEP)
    return -ADAM_LR * (m_hat / (jnp.sqrt(v_hat) + ADAM_EPS) + ADAM_WD * w), nm, nv


def _row_tile(rows):
    for t in (256, 176, 128, 80):
        if rows % t == 0:
            return t
    raise ValueError(rows)


def _reduce_adamw(parts, w, m, v, name):
    rows, width = w.shape
    tr = _row_tile(rows)

    def body(p_ref, w_ref, m_ref, v_ref, g_ref, d_ref, nm_ref, nv_ref):
        g = _sum_parts(p_ref)
        g_ref[...] = g
        d_ref[...], nm_ref[...], nv_ref[...] = _adamw_math(g, w_ref[...], m_ref[...], v_ref[...])

    blk = pl.BlockSpec((tr, width), _row)
    return pl.pallas_call(
        body, grid=(rows // tr,),
        in_specs=[pl.BlockSpec((N_DEV, tr, width), lambda i: (0, i, 0)), blk, blk, blk],
        out_specs=[blk] * 4, out_shape=[SDS((rows, width), F32)] * 4,
        name="adamw_" + name, compiler_params=_params(("arbitrary",), 32))(parts, w, m, v)


def _reduce(parts, name):
    _, rows, width = parts.shape
    tr = _row_tile(rows)

    def body(p_ref, g_ref):
        g_ref[...] = _sum_parts(p_ref)

    return pl.pallas_call(
        body, grid=(rows // tr,),
        in_specs=[pl.BlockSpec((N_DEV, tr, width), lambda i: (0, i, 0))],
        out_specs=pl.BlockSpec((tr, width), _row), out_shape=SDS((rows, width), F32),
        name="sum_" + name, compiler_params=_params(("arbitrary",), 32))(parts)


def _adamw(g, w, m, v, name):
    rows, width = w.shape
    tr = _row_tile(rows)

    def body(g_ref, w_ref, m_ref, v_ref, d_ref, nm_ref, nv_ref):
        d_ref[...], nm_ref[...], nv_ref[...] = _adamw_math(g_ref[...], w_ref[...], m_ref[...], v_ref[...])

    blk = pl.BlockSpec((tr, width), _row)
    return pl.pallas_call(
        body, grid=(rows // tr,), in_specs=[blk] * 4, out_specs=[blk] * 3, out_shape=[SDS((rows, width), F32)] * 3,
        name="adamw_" + name, compiler_params=_params(("arbitrary",), 32))(g, w, m, v)


SMALL = ("w_spatial", "ln_pre_mix", "ln_post_mix", "ln_pre_ffn", "ln_post_ffn", "b_pe_gate",
         "attn_out_norm", "sgu_out_norm", "b_spatial", "sgu_ln_g", "sgu_ln_b")
SMALL_GROUPS = ((128, ("w_spatial", "b_spatial", "sgu_ln_g", "sgu_ln_b")),
                (512, ("attn_out_norm", "sgu_out_norm")),
                (1024, ("ln_post_mix", "ln_pre_ffn", "ln_post_ffn", "b_pe_gate")))
SMALL_LATE = "ln_pre_mix"
SMALL_SIZE = dict(w_spatial=N_GROUPS * CHUNK * CHUNK, b_spatial=N_GROUPS * CHUNK, sgu_ln_g=GROUP_DIM, sgu_ln_b=GROUP_DIM,
                  attn_out_norm=ATTN_W, sgu_out_norm=SGU_W, ln_pre_mix=D_MODEL, ln_post_mix=D_MODEL, ln_pre_ffn=D_MODEL,
                  ln_post_ffn=D_MODEL, b_pe_gate=D_MODEL)
SUBLANES = 8
ROW_SHARDED = ("w_out", "w_down", "w_pe_gate")
COL_SHARDED = ("w_in", "w_gate_up", "w_pe_proj")
WEIGHTS = ("ln_pre_mix", "w_in", "sgu_ln_g", "sgu_ln_b", "w_spatial", "b_spatial", "attn_out_norm", "sgu_out_norm",
           "w_out", "ln_post_mix", "ln_pre_ffn", "w_gate_up", "w_down", "ln_post_ffn", "w_pe_gate", "b_pe_gate",
           "w_pe_proj")


def _group_rows(width, names, extra=0):
    rows = sum(SMALL_SIZE[n] // width for n in names) + extra
    return -(-rows // SUBLANES) * SUBLANES


def _pack_small_grads(gs, loss_term):
    packed = []
    for width, names in SMALL_GROUPS:
        rows = [gs[n].reshape(-1, width) for n in names]
        extra = int(width == D_MODEL)
        if extra:
            rows.append(jnp.full((1, width), loss_term, F32))
        used = sum(r.shape[0] for r in rows)
        rows.append(jnp.zeros((_group_rows(width, names, extra) - used, width), F32))
        packed.append(jnp.concatenate(rows, axis=0))
    return packed


def _small_adamw(arrived, arrived_late, w, m, v):
    names = [n for _, ns in SMALL_GROUPS for n in ns] + [SMALL_LATE]
    n_groups = len(SMALL_GROUPS)

    def body(*refs):
        group_refs, late_ref = refs[:n_groups], refs[n_groups]
        state = refs[n_groups + 1:n_groups + 1 + 3 * len(names)]
        outs = refs[n_groups + 1 + 3 * len(names):]
        sums = [_sum_parts(r) for r in group_refs]

        def update(name, g):
            i = names.index(name)
            w_ref, m_ref, v_ref = state[3 * i:3 * i + 3]
            delta, nm, nv = _adamw_math(g, w_ref[...].reshape(g.shape), m_ref[...].reshape(g.shape),
                                        v_ref[...].reshape(g.shape))
            for o_ref, val in zip(outs[4 * i:4 * i + 4], (g, delta, nm, nv)):
                o_ref[...] = val.reshape(o_ref.shape)

        for (width, group), total in zip(SMALL_GROUPS, sums):
            row = 0
            for name in group:
                rows = SMALL_SIZE[name] // width
                update(name, total[row:row + rows, :])
                row += rows
            if width == D_MODEL:
                outs[-1][...] = total[row:row + 1, :LANES]
        update(SMALL_LATE, _sum_parts(late_ref)[:1, :])

    state = [t[n] for n in names for t in (w, m, v)]
    plain = jax.ShapeDtypeStruct
    out_shape = [plain(w[n].shape, F32) for n in names for _ in range(4)] + [plain((1, LANES), F32)]
    outs = pl.pallas_call(body, out_shape=out_shape, name="adamw_small",
                          compiler_params=pltpu.CompilerParams(vmem_limit_bytes=32 * MIB))(*arrived, arrived_late, *state)
    return {n: outs[4 * i:4 * i + 4] for i, n in enumerate(names)}, outs[-1]


def _slabs(full):
    return full.reshape(N_DEV, full.shape[0] // N_DEV, full.shape[1])


def kernel(x, p, ln_pre_mix, w_in, sgu_ln_g, sgu_ln_b, w_spatial, b_spatial, attn_out_norm, sgu_out_norm, w_out, ln_post_mix, ln_pre_ffn, w_gate_up, w_down, ln_post_ffn, w_pe_gate, b_pe_gate, w_pe_proj, loss_target, m_ln_pre_mix, m_w_in, m_sgu_ln_g, m_sgu_ln_b, m_w_spatial, m_b_spatial, m_attn_out_norm, m_sgu_out_norm, m_w_out, m_ln_post_mix, m_ln_pre_ffn, m_w_gate_up, m_w_down, m_ln_post_ffn, m_w_pe_gate, m_b_pe_gate, m_w_pe_proj, v_ln_pre_mix, v_w_in, v_sgu_ln_g, v_sgu_ln_b, v_w_spatial, v_b_spatial, v_attn_out_norm, v_sgu_out_norm, v_w_out, v_ln_post_mix, v_ln_pre_ffn, v_w_gate_up, v_w_down, v_ln_post_ffn, v_w_pe_gate, v_b_pe_gate, v_w_pe_proj):
    given = dict(locals())
    w = {n: given[n] for n in WEIGHTS}
    m = {n: given["m_" + n] for n in WEIGHTS}
    v = {n: given["v_" + n] for n in WEIGHTS}
    xs, ps, target = x[0], p[0, 0], loss_target[0]

    shard = {n: w[n][0].astype(BF) for n in ROW_SHARDED}
    shard.update({n: w[n][0].T.astype(BF) for n in COL_SHARDED})
    sm = {n: w[n][0] for n in SMALL}
    sm = {n: (a.reshape(1, -1) if a.ndim == 1 else a) for n, a in sm.items()}
    slopes = jnp.broadcast_to((2.0 ** -(jnp.arange(8, dtype=F32) + 1.0)).reshape(4, 2, 1), (4, 2, LANES))
    slopes = jnp.concatenate([slopes, jnp.zeros((4, 6, LANES), F32)], axis=1)
    b_st = jnp.pad(sm["b_spatial"].T, ((0, 0), (0, LANES - N_GROUPS)))

    def full(gathered):
        return gathered.reshape(-1, gathered.shape[-1])

    w_in_t = full(_all_gather(shard["w_in"], "gather_w_in"))
    proj, a = _in_proj(xs, sm["ln_pre_mix"], w_in_t)
    later = ("w_out", "w_gate_up", "w_down", "w_pe_gate", "w_pe_proj")
    attn, lse, gathered = _attn_fwd(proj, slopes, [shard[n] for n in later])
    w_out_f, w_gu_t, w_down_f, w_peg_f, w_pep_t = [full(g) for g in gathered]
    sgu = _sgu_fwd(proj, sm["sgu_ln_g"], sm["sgu_ln_b"], sm["w_spatial"], b_st)
    groups, mixed, h1, f = _out_proj(attn, sgu, xs, sm["attn_out_norm"], sm["sgu_out_norm"], w_out_f,
                                     sm["ln_post_mix"], sm["ln_pre_ffn"])
    g, u, act = _gate_up(f, w_gu_t)
    y, h2 = _down_proj(act, w_down_f, h1, sm["ln_post_ffn"])
    dh2, dy, g_peg, g_pep_t, loss_cols, db_peg, d_pff = _pe_loss_and_bwd(
        h2, ps, target, y, w_peg_f, sm["b_pe_gate"], w_pep_t, sm["ln_post_ffn"])
    loss_term = 0.5 * jnp.sum(loss_cols) * (1.0 / D_MODEL)

    arrived = {}
    g_down = _weight_grad(act, dy, "w_down")
    dg, du, (arrived["w_pe_proj"], arrived["w_pe_gate"]) = _down_bwd(dy, w_down_f, g, u, [_slabs(g_pep_t), _slabs(g_peg)])
    g_gu_t = _weight_grad(dg, f, "w_gate", rows=2 * D_FF)
    g_gu_t = _weight_grad(du, f, "w_up", into=g_gu_t, row_tile=FF_TILES, rows=2 * D_FF)
    dh1, dmx, d_pf, d_pm, (arrived["w_down"],) = _ffn_in_bwd(dg, du, w_gu_t, h1, dh2, mixed, sm["ln_pre_ffn"],
                                                            sm["ln_post_mix"], [_slabs(g_down)])
    g_out = _weight_grad(groups, dmx, "w_out")
    dattn, stats, dsgu, d_ga, d_gs = _out_bwd(dmx, w_out_f, attn, lse, sgu, sm["attn_out_norm"], sm["sgu_out_norm"])
    duz, d_ws, d_bst, d_lg, d_lb = _sgu_bwd(proj, dsgu, sm["sgu_ln_g"], sm["sgu_ln_b"], sm["w_spatial"], b_st)
    gs = dict(sgu_ln_g=d_lg, sgu_ln_b=d_lb, w_spatial=d_ws, b_spatial=d_bst[:, :N_GROUPS].T, attn_out_norm=d_ga,
              sgu_out_norm=d_gs, ln_post_mix=d_pm, ln_pre_ffn=d_pf, ln_post_ffn=d_pff, b_pe_gate=db_peg)
    small_grads = _pack_small_grads(gs, loss_term)
    dqkv, (arrived["w_gate_up"], arrived["w_out"], *arrived_small) = _attn_bwd(
        proj, dattn, stats, slopes, [_slabs(g_gu_t), _slabs(g_out), *small_grads],
        [True, True] + [False] * len(small_grads))
    g_in_t = _grad_w_in([dqkv, duz], a)
    grad_x, d_g1, (arrived["w_in"],) = _in_bwd(
        [dqkv, duz], w_in_t, xs, dh1, sm["ln_pre_mix"], [_slabs(g_in_t)], [True])
    (arrived_late,) = _scatter_call([jnp.pad(d_g1, ((0, SUBLANES - 1), (0, 0)))], [False], "ln_pre_mix_grad_exchange")

    res = {}
    for n in ROW_SHARDED:
        res[n] = _reduce_adamw(arrived[n], w[n][0], m[n][0], v[n][0], n)
    for n in ("w_in", "w_gate_up"):
        res[n] = [t.T for t in _reduce_adamw(arrived[n], w[n][0].T, m[n][0].T, v[n][0].T, n)]
    for n in ("w_pe_proj",):
        grad = _reduce(arrived[n], n).T
        res[n] = (grad, *_adamw(grad, w[n][0], m[n][0], v[n][0], n))
    small, loss_row = _small_adamw(arrived_small, arrived_late, w, m, v)

    out = []
    for k in range(4):
        out += [res[n][k][None] if n in res else small[n][k] for n in WEIGHTS]
    return (loss_row[0, 0], grad_x[None], *out)
```

```python
import math

import jax
import jax.numpy as jnp
from jax import lax
from jax.experimental import pallas as pl
from jax.experimental.pallas import tpu as pltpu

F32 = jnp.float32
BF = jnp.bfloat16


def SDS(shape, dtype):
    return pltpu.HBM(tuple(shape), dtype)

D_MODEL = 1024
ATTN_W = 512
SGU_W = 512
HEAD_DIM = 64
N_GROUPS = 4
GROUP_DIM = 128
CHUNK = 128
D_FF = 2816
PLE_DIM = 256
PROJ = 3 * ATTN_W + 2 * SGU_W
DILATIONS = (1, 4, 16)
QBLK = 128
EPS = 1e-6
NEG = -1e30
N_DEV = 8
LANES = 128

ADAM_LR = 0.001
ADAM_B1 = 0.9
ADAM_B2 = 0.999
ADAM_EPS = 1e-08
ADAM_WD = 0.01
ADAM_STEP = 10

MIB = 2 ** 20
MESH_ID = pl.DeviceIdType.MESH
HBM = pl.BlockSpec(memory_space=pl.ANY)


def _params(sem, vmem_mib):
    return pltpu.CompilerParams(dimension_semantics=sem, vmem_limit_bytes=vmem_mib * MIB)


def _dot(a, b):
    return jnp.dot(a, b, preferred_element_type=F32)


def _dot_nt(a, b):
    return lax.dot_general(a, b, (((1,), (1,)), ((), ())), preferred_element_type=F32)


def _dot_tn(a, b):
    return lax.dot_general(a, b, (((0,), (0,)), ((), ())), preferred_element_type=F32)


def _rstd(x):
    return lax.rsqrt(jnp.mean(x * x, axis=-1, keepdims=True) + EPS)


def _rms_bwd(dy, x, g):
    r = _rstd(x)
    n = x * r
    dn = dy * g
    dx = r * (dn - n * jnp.mean(dn * n, axis=-1, keepdims=True))
    return dx, dy * n


def _colsum(v):
    return jnp.sum(v, axis=0, keepdims=True)


_G0 = math.sqrt(2.0 / math.pi)
_G1 = 0.044715


def _gelu(x):
    t = jnp.tanh(_G0 * (x + _G1 * x * x * x))
    return 0.5 * x * (1.0 + t), t


def _gelu_grad(x, t):
    return 0.5 * (1.0 + t) + 0.5 * x * (1.0 - t * t) * (_G0 * (1.0 + 3.0 * _G1 * x * x))


def _sigmoid(x):
    return 0.5 * jnp.tanh(0.5 * x) + 0.5


def _row(i):
    return (i, 0)


def _fixed(i):
    return (0, 0)


def _acc_init(step, *refs):
    @pl.when(step == 0)
    def _():
        for r in refs:
            r[...] = jnp.zeros_like(r)


FLIPS = [(dx, dy, dc) for dx in (0, 1) for dy in (0, 1) for dc in (0, 1)][1:]
DMA_SEMS = pltpu.SemaphoreType.DMA


def _mesh_pos():
    return lax.axis_index("x"), lax.axis_index("y"), lax.axis_index("c")


def _remote(src, dst, sems, n, to):
    return pltpu.make_async_remote_copy(src_ref=src, dst_ref=dst, send_sem=sems[0].at[n], recv_sem=sems[1].at[n],
                                        device_id=to, device_id_type=MESH_ID)


class _Scatter:
    def __init__(self, items, sems):
        x, y, c = _mesh_pos()
        me = 4 * x + 2 * y + c
        self.local, self.sends, self.arrivals = [], [], []
        for i, (src, dst, slabbed) in enumerate(items):
            self.local.append(pltpu.make_async_copy(src.at[me] if slabbed else src, dst.at[me], sems[2].at[i]))
            for k, (dx, dy, dc) in enumerate(FLIPS):
                to = (1 - x if dx else x, 1 - y if dy else y, 1 - c if dc else c)
                peer = 4 * to[0] + 2 * to[1] + to[2]
                out = src.at[peer] if slabbed else src
                self.sends.append(_remote(out, dst.at[me], sems, 7 * i + k, to))
                self.arrivals.append(_remote(out, dst.at[peer], sems, 7 * i + k, to))

    def start(self):
        for cp in self.local + self.sends:
            cp.start()

    def wait(self):
        for cp in self.arrivals:
            cp.wait_recv()
        for cp in self.sends:
            cp.wait_send()
        for cp in self.local:
            cp.wait()


def _scatter_sems(n):
    return [DMA_SEMS((7 * n,)), DMA_SEMS((7 * n,)), DMA_SEMS((n,))]


class _Gather:
    def __init__(self, items, sems):
        x, y, c = _mesh_pos()
        me, sibling = (x, y, c), (x, y, 1 - c)
        chips = [(1 - x, y), (x, 1 - y), (1 - x, 1 - y)]
        self.first, self.passed, self.from_chips, self.rest, self.local = [], [], [], [], []
        for i, (src, dst) in enumerate(items):
            def slot(p, dst=dst):
                return dst.at[4 * p[0] + 2 * p[1] + p[2]]

            def copy(k, block, to, own=False, i=i, src=src, slot=slot):
                return _remote(src if own else slot(block), slot(block), sems, 7 * i + k, to)

            self.local.append(pltpu.make_async_copy(src, slot(me), sems[2].at[i]))
            self.first.append(copy(0, me, sibling, own=True))
            self.first += [copy(1 + j, me, (*chip, c), own=True) for j, chip in enumerate(chips)]
            self.passed += [copy(4 + j, (*chip, c), sibling) for j, chip in enumerate(chips)]
            self.from_chips += [copy(1 + j, (*chip, c), me) for j, chip in enumerate(chips)]
            self.rest.append(copy(0, sibling, me))
            self.rest += [copy(4 + j, (*chip, 1 - c), me) for j, chip in enumerate(chips)]

    def start(self):
        for cp in self.local + self.first:
            cp.start()

    def forward(self):
        for arrived, onward in zip(self.from_chips, self.passed):
            arrived.wait_recv()
            onward.start()

    def finish(self):
        for cp in self.rest:
            cp.wait_recv()
        for cp in self.first + self.passed:
            cp.wait_send()
        for cp in self.local:
            cp.wait()


def _all_gather(shard, name):
    def body(x_ref, out_ref, *sems):
        g = _Gather([(x_ref, out_ref)], sems)
        g.start()
        g.forward()
        g.finish()

    return pl.pallas_call(
        body, out_shape=SDS((N_DEV,) + shard.shape, shard.dtype), in_specs=[HBM], out_specs=HBM,
        scratch_shapes=_scatter_sems(1), name=name)(shard)


def _scatter_call(srcs, slabbed, name):
    n = len(srcs)

    def body(*refs):
        sc = _Scatter(list(zip(refs[:n], refs[n:2 * n], slabbed)), refs[2 * n:])
        sc.start()
        sc.wait()

    shapes = [SDS(s.shape if sl else (N_DEV,) + s.shape, s.dtype) for s, sl in zip(srcs, slabbed)]
    return pl.pallas_call(body, out_shape=shapes, in_specs=[HBM] * n, out_specs=[HBM] * n,
                          scratch_shapes=_scatter_sems(n), name=name)(*srcs)


def _in_proj(x, g1, w_in_t, tm=512):
    T = x.shape[0]

    def body(x_ref, g_ref, w_ref, proj_ref, a_ref):
        xv = x_ref[...]
        a = (xv * _rstd(xv) * g_ref[...]).astype(BF)
        a_ref[...] = a
        proj_ref[...] = _dot_nt(a, w_ref[...])

    return pl.pallas_call(
        body, grid=(T // tm,),
        in_specs=[pl.BlockSpec((tm, D_MODEL), _row), pl.BlockSpec((1, D_MODEL), _fixed),
                  pl.BlockSpec((PROJ, D_MODEL), _fixed)],
        out_specs=[pl.BlockSpec((tm, PROJ), _row), pl.BlockSpec((tm, D_MODEL), _row)],
        out_shape=[SDS((T, PROJ), F32), SDS((T, D_MODEL), BF)],
        name="in_proj", compiler_params=_params(("arbitrary",), 48))(x, g1, w_in_t)


ATTN_GROUP = 16


def _attn_bias(sl_ref, bias_ref):
    qi = lax.broadcasted_iota(jnp.int32, (QBLK, QBLK), 0)
    kj = lax.broadcasted_iota(jnp.int32, (QBLK, QBLK), 1)
    step = qi - kj
    for di, d in enumerate(DILATIONS):
        for j in range(2):
            sl = sl_ref[0, j:j + 1, :]
            cur = jnp.where(step >= 0, -sl * (step * d).astype(F32), NEG)
            prev = jnp.where(step <= 0, -sl * ((step + QBLK) * d).astype(F32), NEG)
            rows = slice(j * QBLK, (j + 1) * QBLK)
            bias_ref[di * 2, rows, :QBLK] = prev
            bias_ref[di * 2, rows, QBLK:] = cur
            bias_ref[di * 2 + 1, rows, :QBLK] = jnp.full((QBLK, QBLK), NEG, F32)
            bias_ref[di * 2 + 1, rows, QBLK:] = cur


def _stack_heads(x, lo):
    return jnp.concatenate([jnp.where(lo, x, 0.0), jnp.where(lo, 0.0, x)], axis=0).astype(BF)


def _unstack_heads(x, lo):
    return jnp.where(lo, x[:QBLK], x[QBLK:])


def _attn_rows(start, d, blocks=1):
    if d == 1:
        return pl.ds(pl.multiple_of(start, QBLK), blocks * QBLK)
    return pl.ds(start, blocks * QBLK, stride=d)


def _attn_group_index(i, nblk, d, group):
    per = nblk // d // group
    r = i // per
    n0 = (i % per) * group
    start = r + (d * QBLK) * n0
    pstart = jnp.maximum(start - d * QBLK, r)
    return start, pstart, n0 == 0


def _attn_plan(nblk, d):
    group = min(ATTN_GROUP, nblk // d)
    return group, max(1, min(ATTN_GROUP // group, d))


def _attn_fwd(proj, slopes, to_gather):
    T = proj.shape[0]
    nblk = T // QBLK
    n_g = len(to_gather)

    def body(q_ref, k_ref, v_ref, sl_ref, *rest):
        srcs, (o_ref, m_ref), dsts = rest[:n_g], rest[n_g:n_g + 2], rest[n_g + 2:2 * n_g + 2]
        sems, (l_ref, bias_ref) = rest[2 * n_g + 2:2 * n_g + 5], rest[2 * n_g + 5:]
        h = pl.program_id(0)

        @pl.when(h == 0)
        def _():
            _Gather(list(zip(srcs, dsts)), sems).start()

        @pl.when(h == pl.num_programs(0) - 1)
        def _():
            _Gather(list(zip(srcs, dsts)), sems).forward()

        _attn_bias(sl_ref, bias_ref)
        lo = lax.broadcasted_iota(jnp.int32, (1, LANES), 1) < HEAD_DIM

        order = list(enumerate(DILATIONS))[::-1]
        for di, d in order:
            group, segs = _attn_plan(nblk, d)
            fresh, last = di == order[0][0], di == order[-1][0]

            def step(i, carry, segs=segs, **kw):
                for s in range(segs):
                    segment(i * segs + s, **kw)
                return carry

            def segment(i, d=d, di=di, group=group, fresh=fresh, last=last):
                start, pstart, first = _attn_group_index(i, nblk, d, group)
                prows = _attn_rows(pstart, d)
                k_prev, v_prev = k_ref[prows, :].astype(BF), v_ref[prows, :].astype(BF)
                for b in range(group):
                    out = _attn_rows(start + b * (d * QBLK), d)
                    k_own, v_own = k_ref[out, :].astype(BF), v_ref[out, :].astype(BF)
                    k2, v2 = jnp.concatenate([k_prev, k_own], axis=0), jnp.concatenate([v_prev, v_own], axis=0)
                    k_prev, v_prev = k_own, v_own
                    bias = bias_ref[di * 2 + first.astype(jnp.int32)] if b == 0 else bias_ref[di * 2]
                    s = _dot_nt(_stack_heads(q_ref[out, :] * (HEAD_DIM ** -0.5), lo), k2) + bias
                    m = jnp.max(s, axis=-1, keepdims=True)
                    pr = jnp.exp(s - m)
                    m_b = _unstack_heads(m, lo)
                    l_b = _unstack_heads(jnp.sum(pr, axis=-1, keepdims=True), lo)
                    o_b = _unstack_heads(_dot(pr.astype(BF), v2), lo)
                    if fresh:
                        m_ref[out, :] = m_b
                        l_ref[out, :] = l_b
                        o_ref[out, :] = o_b
                        continue
                    m_o = m_ref[out, :]
                    m_n = jnp.maximum(m_o, m_b)
                    wa, wb = jnp.exp(m_o - m_n), jnp.exp(m_b - m_n)
                    l_n = wa * l_ref[out, :] + wb * l_b
                    o_n = wa * o_ref[out, :] + wb * o_b
                    if last:
                        m_ref[out, :] = m_n + jnp.log(l_n)
                        o_ref[out, :] = o_n / l_n
                    else:
                        m_ref[out, :] = m_n
                        l_ref[out, :] = l_n
                        o_ref[out, :] = o_n

            lax.fori_loop(0, nblk // (group * segs), step, 0)

        @pl.when(h == pl.num_programs(0) - 1)
        def _():
            _Gather(list(zip(srcs, dsts)), sems).finish()

    col = lambda base: pl.BlockSpec((T, LANES), lambda h: (0, base + h))
    tok = pl.BlockSpec((T, LANES), lambda h: (0, h))
    outs = pl.pallas_call(
        body, grid=(4,),
        in_specs=[col(0), col(4), col(8), pl.BlockSpec((1, 8, LANES), lambda h: (h, 0, 0))] + [HBM] * n_g,
        out_specs=[tok, tok] + [HBM] * n_g,
        out_shape=[SDS((T, ATTN_W), F32), SDS((T, ATTN_W), F32)]
        + [SDS((N_DEV,) + g.shape, g.dtype) for g in to_gather],
        scratch_shapes=_scatter_sems(n_g) + [pltpu.VMEM((T, LANES), F32), pltpu.VMEM((6, 2 * QBLK, 2 * QBLK), F32)],
        name="attn_fwd", compiler_params=_params(("arbitrary",), 56))(proj, proj, proj, slopes, *to_gather)
    return outs[0], outs[1], outs[2:]


def _sgu_norm(zv, ln_g, ln_b):
    gz, tz = _gelu(zv)
    mu = jnp.mean(gz, axis=-1, keepdims=True)
    xc = gz - mu
    rs = lax.rsqrt(jnp.mean(xc * xc, axis=-1, keepdims=True) + EPS)
    xhat = xc * rs
    return xhat * ln_g + ln_b, xhat, rs, tz


def _causal(w):
    i = lax.broadcasted_iota(jnp.int32, (CHUNK, CHUNK), 0)
    j = lax.broadcasted_iota(jnp.int32, (CHUNK, CHUNK), 1)
    return jnp.where(i >= j, w, 0.0)


def _sgu_fwd(proj, ln_g, ln_b, w_s, b_st, tm=512):
    T = proj.shape[0]

    def body(u_ref, z_ref, g_ref, b_ref, w_ref, bs_ref, out_ref):
        for g in range(N_GROUPS):
            wm = _causal(w_ref[g]).astype(BF)
            cols = slice(g * GROUP_DIM, (g + 1) * GROUP_DIM)
            for c in range(tm // CHUNK):
                rows = slice(c * CHUNK, (c + 1) * CHUNK)
                zn, _, _, _ = _sgu_norm(z_ref[rows, cols], g_ref[...], b_ref[...])
                mixed = _dot(wm, zn.astype(BF)) + bs_ref[:, g:g + 1]
                gu, _ = _gelu(u_ref[rows, cols])
                out_ref[rows, cols] = gu * mixed

    return pl.pallas_call(
        body, grid=(T // tm,),
        in_specs=[pl.BlockSpec((tm, SGU_W), lambda i: (i, 3)), pl.BlockSpec((tm, SGU_W), lambda i: (i, 4)),
                  pl.BlockSpec((1, GROUP_DIM), _fixed), pl.BlockSpec((1, GROUP_DIM), _fixed),
                  pl.BlockSpec((N_GROUPS, CHUNK, CHUNK), lambda i: (0, 0, 0)), pl.BlockSpec((CHUNK, LANES), _fixed)],
        out_specs=pl.BlockSpec((tm, SGU_W), _row),
        out_shape=SDS((T, SGU_W), F32),
        name="sgu_fwd", compiler_params=_params(("arbitrary",), 32))(proj, proj, ln_g, ln_b, w_s, b_st)


def _out_proj(attn, sgu, x, g_a, g_s, w_out, g_pm, g_pf, tm=512):
    T = x.shape[0]

    def body(a_ref, s_ref, x_ref, ga_ref, gs_ref, w_ref, gpm_ref, gpf_ref, grp_ref, mixed_ref, h1_ref, f_ref):
        av, sv = a_ref[...], s_ref[...]
        an = (av * _rstd(av) * ga_ref[...]).astype(BF)
        sn = (sv * _rstd(sv) * gs_ref[...]).astype(BF)
        grp_ref[:, :ATTN_W] = an
        grp_ref[:, ATTN_W:] = sn
        mixed = _dot(an, w_ref[:ATTN_W, :]) + _dot(sn, w_ref[ATTN_W:, :])
        mixed_ref[...] = mixed
        h1 = x_ref[...] + mixed * _rstd(mixed) * gpm_ref[...]
        h1_ref[...] = h1
        f_ref[...] = (h1 * _rstd(h1) * gpf_ref[...]).astype(BF)

    tok = lambda w: pl.BlockSpec((tm, w), _row)
    vec = lambda w: pl.BlockSpec((1, w), _fixed)
    return pl.pallas_call(
        body, grid=(T // tm,),
        in_specs=[tok(ATTN_W), tok(SGU_W), tok(D_MODEL), vec(ATTN_W), vec(SGU_W),
                  pl.BlockSpec((D_MODEL, D_MODEL), _fixed), vec(D_MODEL), vec(D_MODEL)],
        out_specs=[tok(D_MODEL)] * 4,
        out_shape=[SDS((T, D_MODEL), BF), SDS((T, D_MODEL), F32), SDS((T, D_MODEL), F32), SDS((T, D_MODEL), BF)],
        name="out_proj", compiler_params=_params(("arbitrary",), 48))(attn, sgu, x, g_a, g_s, w_out, g_pm, g_pf)


FF_TILE = 1408
FF_TILES = D_FF // FF_TILE
FF_CHUNK = 256


def _gate_up(f, w_gu_t, tm=512):
    T = f.shape[0]
    tn = FF_TILE

    def body(f_ref, wg_ref, wu_ref, g_ref, u_ref, act_ref):
        fv = f_ref[...]
        g = _dot_nt(fv, wg_ref[...])
        u = _dot_nt(fv, wu_ref[...])
        g_ref[...] = g.astype(BF)
        u_ref[...] = u.astype(BF)
        act_ref[...] = (g * _sigmoid(g) * u).astype(BF)

    ospec = pl.BlockSpec((tm, tn), lambda j, i: (i, j))
    return pl.pallas_call(
        body, grid=(FF_TILES, T // tm),
        in_specs=[pl.BlockSpec((tm, D_MODEL), lambda j, i: (i, 0)), pl.BlockSpec((tn, D_MODEL), lambda j, i: (j, 0)),
                  pl.BlockSpec((tn, D_MODEL), lambda j, i: (j + FF_TILES, 0))],
        out_specs=[ospec] * 3, out_shape=[SDS((T, D_FF), BF)] * 3,
        name="gate_up", compiler_params=_params(("arbitrary", "arbitrary"), 40))(f, w_gu_t, w_gu_t)


def _down_proj(act, w_down, h1, g_pff, tm=512):
    T = act.shape[0]

    def body(a_ref, w_ref, h1_ref, g_ref, y_ref, h2_ref):
        y = _dot(a_ref[...], w_ref[...])
        y_ref[...] = y
        h2_ref[...] = h1_ref[...] + y * _rstd(y) * g_ref[...]

    return pl.pallas_call(
        body, grid=(T // tm,),
        in_specs=[pl.BlockSpec((tm, D_FF), _row), pl.BlockSpec((D_FF, D_MODEL), _fixed),
                  pl.BlockSpec((tm, D_MODEL), _row), pl.BlockSpec((1, D_MODEL), _fixed)],
        out_specs=[pl.BlockSpec((tm, D_MODEL), _row)] * 2,
        out_shape=[SDS((T, D_MODEL), F32)] * 2,
        name="down_proj", compiler_params=_params(("arbitrary",), 48))(act, w_down, h1, g_pff)


def _pe_loss_and_bwd(h2, p, target, y, w_peg, b_peg, w_pep_t, g_pff, tm=512):
    T = h2.shape[0]

    def body(h2_ref, p_ref, t_ref, y_ref, wg_ref, b_ref, wp_ref, g_ref,
             dh2_ref, dy_ref, gpeg_ref, gpep_ref, loss_ref, db_ref, dg_ref):
        _acc_init(pl.program_id(0), gpeg_ref, gpep_ref, loss_ref, db_ref, dg_ref)
        h2v = h2_ref[...]
        h2b = h2v.astype(BF)
        pb = p_ref[...].astype(BF)
        gate = _sigmoid(_dot(h2b, wg_ref[...]) + b_ref[...])
        pp = _dot_nt(pb, wp_ref[...])
        diff = h2v + gate * pp - t_ref[...]
        loss_ref[...] += _colsum(diff * diff)
        dh3 = diff * (1.0 / D_MODEL)
        dpre = dh3 * pp * (gate * (1.0 - gate))
        dpre_b = dpre.astype(BF)
        db_ref[...] += _colsum(dpre)
        gpeg_ref[...] += _dot_tn(h2b, dpre_b)
        gpep_ref[...] += _dot_tn((dh3 * gate).astype(BF), pb)
        dh2 = dh3 + _dot_nt(dpre_b, wg_ref[...])
        dh2_ref[...] = dh2
        dy, dg = _rms_bwd(dh2, y_ref[...], g_ref[...])
        dy_ref[...] = dy.astype(BF)
        dg_ref[...] += _colsum(dg)

    tok = lambda w: pl.BlockSpec((tm, w), _row)
    vec = pl.BlockSpec((1, D_MODEL), _fixed)
    wg = pl.BlockSpec((D_MODEL, D_MODEL), _fixed)
    wp = pl.BlockSpec((D_MODEL, PLE_DIM), _fixed)
    return pl.pallas_call(
        body, grid=(T // tm,),
        in_specs=[tok(D_MODEL), tok(PLE_DIM), tok(D_MODEL), tok(D_MODEL), wg, vec, wp, vec],
        out_specs=[tok(D_MODEL), tok(D_MODEL), wg, wp, vec, vec, vec],
        out_shape=[SDS((T, D_MODEL), F32), SDS((T, D_MODEL), BF), SDS((D_MODEL, D_MODEL), F32),
                   SDS((D_MODEL, PLE_DIM), F32)] + [SDS((1, D_MODEL), F32)] * 3,
        name="pe_loss_and_bwd", compiler_params=_params(("arbitrary",), 56))(
            h2, p, target, y, w_peg, b_peg, w_pep_t, g_pff)


def _weight_grad(a, dy, name, into=None, row_tile=0, rows=None, tk=512):
    n = dy.shape[1]
    tn = min(n, 1024)
    T, ka = a.shape
    tka = FF_TILE if ka == D_FF else min(ka, 1024)
    rows = ka if rows is None else rows

    def body(a_ref, dy_ref, *rest):
        out_ref = rest[-1]
        _acc_init(pl.program_id(2), out_ref)
        out_ref[...] += _dot_tn(a_ref[...].astype(BF), dy_ref[...].astype(BF))

    carried = [] if into is None else [into]
    return pl.pallas_call(
        body, grid=(ka // tka, n // tn, T // tk),
        in_specs=[pl.BlockSpec((tk, tka), lambda i, j, k: (k, i)), pl.BlockSpec((tk, tn), lambda i, j, k: (k, j))]
        + [HBM] * len(carried),
        out_specs=pl.BlockSpec((tka, tn), lambda i, j, k: (i + row_tile, j)),
        out_shape=SDS((rows, n), F32), input_output_aliases={2: 0} if carried else {},
        name="grad_" + name, compiler_params=_params(("arbitrary",) * 3, 40))(a, dy, *carried)


def _grad_w_in(dparts, a, tk=512):
    T = a.shape[0]

    def body(*refs):
        d_refs, a_ref, out_ref, acc_ref = refs[:len(dparts)], refs[-3], refs[-2], refs[-1]
        k = pl.program_id(0)
        _acc_init(k, acc_ref)
        cols = [r[part].astype(BF) for r in d_refs for part in range(r.shape[0])]
        acc_ref[...] += _dot_tn(jnp.concatenate(cols, axis=1), a_ref[...])

        @pl.when(k == pl.num_programs(0) - 1)
        def _():
            out_ref[...] = acc_ref[...].astype(BF)

    return pl.pallas_call(
        body, grid=(T // tk,),
        in_specs=[pl.BlockSpec((d.shape[0], tk, d.shape[2]), lambda k: (0, k, 0)) for d in dparts]
        + [pl.BlockSpec((tk, D_MODEL), lambda k: (k, 0))],
        out_specs=pl.BlockSpec((PROJ, D_MODEL), lambda k: (0, 0)),
        out_shape=SDS((PROJ, D_MODEL), BF), scratch_shapes=[pltpu.VMEM((PROJ, D_MODEL), F32)],
        name="grad_w_in", compiler_params=_params(("arbitrary",), 48))(*dparts, a)


def _down_bwd(dy, w_down, g, u, to_send, tm=512):
    T = dy.shape[0]
    n_s = len(to_send)

    def body(dy_ref, w_ref, g_ref, u_ref, *rest):
        srcs, (dg_ref, du_ref), dsts, sems = rest[:n_s], rest[n_s:n_s + 2], rest[n_s + 2:2 * n_s + 2], rest[2 * n_s + 2:]
        i = pl.program_id(0)
        items = list(zip(srcs, dsts, [True] * n_s))

        @pl.when(i == 0)
        def _():
            _Scatter(items, sems).start()

        dyv = dy_ref[...]
        for c in range(D_FF // FF_CHUNK):
            cols = slice(c * FF_CHUNK, (c + 1) * FF_CHUNK)
            dact = _dot_nt(dyv, w_ref[cols, :])
            gv = g_ref[:, cols].astype(F32)
            uv = u_ref[:, cols].astype(F32)
            s = _sigmoid(gv)
            ds = dact * s
            dg_ref[:, cols] = (ds * uv * (1.0 + gv * (1.0 - s))).astype(BF)
            du_ref[:, cols] = (ds * gv).astype(BF)

        @pl.when(i == pl.num_programs(0) - 1)
        def _():
            _Scatter(items, sems).wait()

    tile = pl.BlockSpec((tm, D_FF), _row)
    outs = pl.pallas_call(
        body, grid=(T // tm,),
        in_specs=[pl.BlockSpec((tm, D_MODEL), _row),
                  pl.BlockSpec((D_FF, D_MODEL), _fixed, pipeline_mode=pl.Buffered(1)), tile, tile] + [HBM] * n_s,
        out_specs=[tile, tile] + [HBM] * n_s,
        out_shape=[SDS((T, D_FF), BF)] * 2 + [SDS(s.shape, s.dtype) for s in to_send],
        scratch_shapes=_scatter_sems(n_s),
        name="down_bwd", compiler_params=_params(("arbitrary",), 48))(dy, w_down, g, u, *to_send)
    return outs[0], outs[1], outs[2:]


def _ffn_in_bwd(dg, du, w_gu_t, h1, dh2, mixed, g_pf, g_pm, to_send, tm=512):
    T = h1.shape[0]
    n_s = len(to_send)

    def body(dg_ref, du_ref, wg_ref, wu_ref, h1_ref, dh2_ref, mx_ref, gpf_ref, gpm_ref, *rest):
        srcs, outs, dsts, sems = rest[:n_s], rest[n_s:n_s + 4], rest[n_s + 4:2 * n_s + 4], rest[2 * n_s + 4:]
        dh1_ref, dmx_ref, dgpf_ref, dgpm_ref = outs
        i = pl.program_id(0)
        items = list(zip(srcs, dsts, [True] * n_s))
        _acc_init(i, dgpf_ref, dgpm_ref)

        @pl.when(i == 0)
        def _():
            _Scatter(items, sems).start()

        df = _dot(dg_ref[...], wg_ref[...]) + _dot(du_ref[...], wu_ref[...])
        dx, dgf = _rms_bwd(df, h1_ref[...], gpf_ref[...])
        dh1 = dh2_ref[...] + dx
        dh1_ref[...] = dh1
        dmx, dgm = _rms_bwd(dh1, mx_ref[...], gpm_ref[...])
        dmx_ref[...] = dmx.astype(BF)
        dgpf_ref[...] += _colsum(dgf)
        dgpm_ref[...] += _colsum(dgm)

        @pl.when(i == pl.num_programs(0) - 1)
        def _():
            _Scatter(items, sems).wait()

    tok = lambda w: pl.BlockSpec((tm, w), _row)
    vec = pl.BlockSpec((1, D_MODEL), _fixed)
    outs = pl.pallas_call(
        body, grid=(T // tm,),
        in_specs=[tok(D_FF), tok(D_FF), pl.BlockSpec((D_FF, D_MODEL), lambda i: (0, 0), pipeline_mode=pl.Buffered(1)),
                  pl.BlockSpec((D_FF, D_MODEL), lambda i: (1, 0), pipeline_mode=pl.Buffered(1)),
                  tok(D_MODEL), tok(D_MODEL), tok(D_MODEL), vec, vec]
        + [HBM] * n_s,
        out_specs=[tok(D_MODEL), tok(D_MODEL), vec, vec] + [HBM] * n_s,
        out_shape=[SDS((T, D_MODEL), F32), SDS((T, D_MODEL), BF), SDS((1, D_MODEL), F32), SDS((1, D_MODEL), F32)]
        + [SDS(s.shape, s.dtype) for s in to_send],
        scratch_shapes=_scatter_sems(n_s),
        name="ffn_in_bwd", compiler_params=_params(("arbitrary",), 56))(
            dg, du, w_gu_t, w_gu_t, h1, dh2, mixed, g_pf, g_pm, *to_send)
    return outs[0], outs[1], outs[2], outs[3], outs[4:]


STAT_LANES = HEAD_DIM // 2


def _ff_grad_spec(half):
    return pl.BlockSpec((D_FF, D_MODEL), lambda i: (half, 0), pipeline_mode=pl.Buffered(1))


def _out_bwd(dmx, w_out, attn, lse, sgu, g_a, g_s, dg, f, tm=512):
    T = attn.shape[0]

    def body(dm_ref, w_ref, a_ref, l_ref, s_ref, ga_ref, gs_ref, dgate_ref, f_ref,
             da_ref, st_ref, ds_ref, dga_ref, dgs_ref, ggu_ref):
        _acc_init(pl.program_id(0), dga_ref, dgs_ref, ggu_ref)
        ggu_ref[...] += _dot_tn(dgate_ref[...], f_ref[...])
        dgr = _dot_nt(dm_ref[...], w_ref[...])
        av = a_ref[...]
        da, dga = _rms_bwd(dgr[:, :ATTN_W], av, ga_ref[...])
        ds, dgs = _rms_bwd(dgr[:, ATTN_W:], s_ref[...], gs_ref[...])
        da_ref[...] = da
        ds_ref[...] = ds
        dga_ref[...] += _colsum(dga)
        dgs_ref[...] += _colsum(dgs)
        lane = lax.broadcasted_iota(jnp.int32, (1, LANES), 1)
        lo = lane < HEAD_DIM
        first = (lane % HEAD_DIM) < STAT_LANES
        prod = da * av
        for c in range(ATTN_W // LANES):
            cols = slice(c * LANES, (c + 1) * LANES)
            pc = prod[:, cols]
            delta = jnp.where(lo, jnp.sum(jnp.where(lo, pc, 0.0), axis=-1, keepdims=True),
                              jnp.sum(jnp.where(lo, 0.0, pc), axis=-1, keepdims=True))
            st_ref[:, cols] = jnp.where(first, l_ref[:, cols], delta)

    tok = lambda w: pl.BlockSpec((tm, w), _row)
    vec = lambda w: pl.BlockSpec((1, w), _fixed)
    return pl.pallas_call(
        body, grid=(T // tm,),
        in_specs=[tok(D_MODEL), pl.BlockSpec((D_MODEL, D_MODEL), _fixed), tok(ATTN_W), tok(ATTN_W), tok(SGU_W),
                  vec(ATTN_W), vec(SGU_W), tok(D_FF), tok(D_MODEL)],
        out_specs=[tok(ATTN_W), tok(ATTN_W), tok(SGU_W), vec(ATTN_W), vec(SGU_W), _ff_grad_spec(0)],
        out_shape=[SDS((T, ATTN_W), F32), SDS((T, ATTN_W), F32), SDS((T, SGU_W), F32), SDS((1, ATTN_W), F32),
                   SDS((1, SGU_W), F32), SDS((2 * D_FF, D_MODEL), F32)],
        name="out_bwd", compiler_params=_params(("arbitrary",), 56))(dmx, w_out, attn, lse, sgu, g_a, g_s, dg, f)


def _sgu_bwd(proj, dsgu, ln_g, ln_b, w_s, b_st, groups, dmx, d_up, f, g_gu_t, tm=512):
    T = proj.shape[0]

    def body(u_ref, z_ref, ds_ref, g_ref, b_ref, w_ref, bs_ref, grp_ref, dmx_ref, dup_ref, f_ref, _,
             duz_ref, dw_ref, dbs_ref, dlg_ref, dlb_ref, gout_ref, ggu_ref, dbacc_ref):
        du_ref, dz_ref = duz_ref.at[0], duz_ref.at[1]
        step = pl.program_id(0)
        _acc_init(step, dw_ref, dbs_ref, dlg_ref, dlb_ref, gout_ref, ggu_ref, dbacc_ref)
        gout_ref[...] += _dot_tn(grp_ref[...], dmx_ref[...])
        ggu_ref[...] += _dot_tn(dup_ref[...], f_ref[...])
        lng, lnb = g_ref[...], b_ref[...]
        for g in range(N_GROUPS):
            wm = _causal(w_ref[g]).astype(BF)
            cols = slice(g * GROUP_DIM, (g + 1) * GROUP_DIM)
            for c in range(tm // CHUNK):
                rows = slice(c * CHUNK, (c + 1) * CHUNK)
                zv, uv, dout = z_ref[rows, cols], u_ref[rows, cols], ds_ref[rows, cols]
                zn, xhat, rs, tz = _sgu_norm(zv, lng, lnb)
                znb = zn.astype(BF)
                mixed = _dot(wm, znb) + bs_ref[:, g:g + 1]
                gu, tu = _gelu(uv)
                du_ref[rows, cols] = (dout * mixed * _gelu_grad(uv, tu)).astype(BF)
                dmix = dout * gu
                dmb = dmix.astype(BF)
                dw_ref[g] += _causal(_dot_nt(dmb, znb))
                dbacc_ref[g] += dmix
                dzn = _dot_tn(wm, dmb)
                dlg_ref[...] += _colsum(dzn * xhat)
                dlb_ref[...] += _colsum(dzn)
                dxh = dzn * lng
                dgz = rs * (dxh - jnp.mean(dxh, axis=-1, keepdims=True)
                            - xhat * jnp.mean(dxh * xhat, axis=-1, keepdims=True))
                dz_ref[rows, cols] = (dgz * _gelu_grad(zv, tz)).astype(BF)

        @pl.when(step == pl.num_programs(0) - 1)
        def _():
            lane = lax.broadcasted_iota(jnp.int32, (CHUNK, LANES), 1)
            acc = jnp.zeros((CHUNK, LANES), F32)
            for g in range(N_GROUPS):
                acc = jnp.where(lane == g, jnp.sum(dbacc_ref[g], axis=-1, keepdims=True), acc)
            dbs_ref[...] = acc

    tok = pl.BlockSpec((tm, SGU_W), _row)
    vec = pl.BlockSpec((1, GROUP_DIM), _fixed)
    wsp = pl.BlockSpec((N_GROUPS, CHUNK, CHUNK), lambda i: (0, 0, 0))
    sq = pl.BlockSpec((CHUNK, LANES), _fixed)
    wide = pl.BlockSpec((tm, D_MODEL), _row)
    return pl.pallas_call(
        body, grid=(T // tm,),
        in_specs=[pl.BlockSpec((tm, SGU_W), lambda i: (i, 3)), pl.BlockSpec((tm, SGU_W), lambda i: (i, 4)), tok,
                  vec, vec, wsp, sq, wide, wide, pl.BlockSpec((tm, D_FF), _row), wide, HBM],
        out_specs=[pl.BlockSpec((2, tm, SGU_W), lambda i: (0, i, 0)), wsp, sq, vec, vec,
                   pl.BlockSpec((D_MODEL, D_MODEL), _fixed), _ff_grad_spec(1)],
        out_shape=[SDS((2, T, SGU_W), BF), SDS((N_GROUPS, CHUNK, CHUNK), F32),
                   SDS((CHUNK, LANES), F32), SDS((1, GROUP_DIM), F32), SDS((1, GROUP_DIM), F32),
                   SDS((D_MODEL, D_MODEL), F32), SDS((2 * D_FF, D_MODEL), F32)],
        input_output_aliases={11: 6},
        scratch_shapes=[pltpu.VMEM((N_GROUPS, CHUNK, LANES), F32)],
        name="sgu_bwd", compiler_params=_params(("arbitrary",), 56))(
            proj, proj, dsgu, ln_g, ln_b, w_s, b_st, groups, dmx, d_up, f, g_gu_t)


def _attn_bwd(proj, do, stats, slopes, to_send, slabbed):
    T = proj.shape[0]
    nblk = T // QBLK
    n_s = len(to_send)

    def body(q_ref, k_ref, v_ref, do_ref, st_ref, sl_ref, *rest):
        srcs, d_ref, dsts = rest[:n_s], rest[n_s], rest[n_s + 1:2 * n_s + 1]
        sems, bias_ref = rest[2 * n_s + 1:2 * n_s + 4], rest[2 * n_s + 4]
        dq_ref, dk_ref, dv_ref = d_ref.at[0], d_ref.at[1], d_ref.at[2]
        h = pl.program_id(0)
        items = list(zip(srcs, dsts, slabbed))

        @pl.when(h == 0)
        def _():
            _Scatter(items, sems).start()

        _attn_bias(sl_ref, bias_ref)
        lo = lax.broadcasted_iota(jnp.int32, (1, LANES), 1) < HEAD_DIM
        scale = HEAD_DIM ** -0.5
        d_ref[...] = jnp.zeros_like(d_ref)

        for di, d in enumerate(DILATIONS):
            group, segs = _attn_plan(nblk, d)

            def step(i, carry, segs=segs, **kw):
                for s in range(segs):
                    segment(i * segs + s, **kw)
                return carry

            def segment(i, d=d, di=di, group=group):
                start, pstart, first = _attn_group_index(i, nblk, d, group)
                rows, prows = _attn_rows(start, d, group), _attn_rows(pstart, d)
                q = q_ref[rows, :] * scale
                k = jnp.concatenate([k_ref[prows, :], k_ref[rows, :]], axis=0).astype(BF)
                v = jnp.concatenate([v_ref[prows, :], v_ref[rows, :]], axis=0).astype(BF)
                dov = do_ref[rows, :]
                stats = st_ref[rows, :]
                masks = [lo, ~lo]
                qm = [jnp.where(masks[j], q, 0.0).astype(BF) for j in range(2)]
                dom = [jnp.where(masks[j], dov, 0.0).astype(BF) for j in range(2)]
                for b in range(group):
                    qb = slice(b * QBLK, (b + 1) * QBLK)
                    kb = slice(b * QBLK, (b + 2) * QBLK)
                    which = di * 2 + first.astype(jnp.int32) if b == 0 else di * 2
                    dq_parts, prs, dss = [], [], []
                    for j in range(2):
                        bias = bias_ref[which, j * QBLK:(j + 1) * QBLK, :]
                        lj = stats[qb, j * HEAD_DIM:j * HEAD_DIM + 1]
                        delta = stats[qb, j * HEAD_DIM + STAT_LANES:j * HEAD_DIM + STAT_LANES + 1]
                        pr = jnp.exp(_dot_nt(qm[j][qb], k[kb]) + bias - lj)
                        ds = (pr * (_dot_nt(dom[j][qb], v[kb]) - delta)).astype(BF)
                        dq_parts.append(_dot(ds, k[kb]))
                        prs.append(pr.astype(BF))
                        dss.append(ds)
                    dk_b = _dot_tn(jnp.concatenate(dss, axis=0), jnp.concatenate([qm[0][qb], qm[1][qb]], axis=0))
                    dv_b = _dot_tn(jnp.concatenate(prs, axis=0), jnp.concatenate([dom[0][qb], dom[1][qb]], axis=0))
                    own = _attn_rows(start + b * (d * QBLK), d)
                    dq_ref[own, :] += jnp.where(lo, dq_parts[0], dq_parts[1]) * scale
                    if b == 0:
                        dk_ref[prows, :] += dk_b[:QBLK]
                        dv_ref[prows, :] += dv_b[:QBLK]
                        dk_ref[own, :] += dk_b[QBLK:]
                        dv_ref[own, :] += dv_b[QBLK:]
                    else:
                        two = _attn_rows(start + (b - 1) * (d * QBLK), d, 2)
                        dk_ref[two, :] += dk_b
                        dv_ref[two, :] += dv_b

            lax.fori_loop(0, nblk // (group * segs), step, 0)

        @pl.when(h == pl.num_programs(0) - 1)
        def _():
            _Scatter(items, sems).wait()

    col = lambda base: pl.BlockSpec((T, LANES), lambda h: (0, base + h))
    outs = pl.pallas_call(
        body, grid=(4,),
        in_specs=[col(0), col(4), col(8), col(0), col(0), pl.BlockSpec((1, 8, LANES), lambda h: (h, 0, 0))]
        + [HBM] * n_s,
        out_specs=[pl.BlockSpec((3, T, LANES), lambda h: (0, 0, h), pipeline_mode=pl.Buffered(1))] + [HBM] * n_s,
        out_shape=[SDS((3, T, ATTN_W), F32)]
        + [SDS(s.shape if sl else (N_DEV,) + s.shape, s.dtype) for s, sl in zip(to_send, slabbed)],
        scratch_shapes=_scatter_sems(n_s) + [pltpu.VMEM((6, 2 * QBLK, 2 * QBLK), F32)],
        name="attn_bwd", compiler_params=_params(("arbitrary",), 60))(proj, proj, proj, do, stats, slopes, *to_send)
    return outs[0], outs[1:]


def _in_bwd(dparts, w_in_t, x, dh1, g1, to_send, slabbed, tm=512):
    T = x.shape[0]
    n = len(dparts)
    n_s = len(to_send)
    w = ATTN_W

    def body(*refs):
        d_refs, (w_ref, x_ref, dh1_ref, g_ref), rest = refs[:n], refs[n:n + 4], refs[n + 4:]
        srcs, (dx_ref, dg_ref), dsts, sems = rest[:n_s], rest[n_s:n_s + 2], rest[n_s + 2:2 * n_s + 2], rest[2 * n_s + 2:]
        step = pl.program_id(0)
        items = list(zip(srcs, dsts, slabbed))
        _acc_init(step, dg_ref)

        @pl.when(step == 0)
        def _():
            _Scatter(items, sems).start()

        da = None
        col = 0
        for r in d_refs:
            for part in range(r.shape[0]):
                t = _dot(r[part].astype(BF), w_ref[col * w:(col + 1) * w, :])
                da = t if da is None else da + t
                col += 1
        dx, dg = _rms_bwd(da, x_ref[...], g_ref[...])
        dx_ref[...] = dh1_ref[...] + dx
        dg_ref[...] += _colsum(dg)

        @pl.when(step == pl.num_programs(0) - 1)
        def _():
            _Scatter(items, sems).wait()

    tok = lambda c: pl.BlockSpec((tm, c), _row)
    vec = pl.BlockSpec((1, D_MODEL), _fixed)
    outs = pl.pallas_call(
        body, grid=(T // tm,),
        in_specs=[pl.BlockSpec((d.shape[0], tm, w), lambda i: (0, i, 0)) for d in dparts]
        + [pl.BlockSpec((PROJ, D_MODEL), _fixed), tok(D_MODEL), tok(D_MODEL), vec] + [HBM] * n_s,
        out_specs=[tok(D_MODEL), vec] + [HBM] * n_s,
        out_shape=[SDS((T, D_MODEL), F32), SDS((1, D_MODEL), F32)]
        + [SDS(s.shape if sl else (N_DEV,) + s.shape, s.dtype) for s, sl in zip(to_send, slabbed)],
        scratch_shapes=_scatter_sems(n_s),
        name="in_bwd", compiler_params=_params(("arbitrary",), 52))(*dparts, w_in_t, x, dh1, g1, *to_send)
    return outs[0], outs[1], outs[2:]


def _sum_parts(p_ref):
    g = p_ref[0].astype(F32)
    for s in range(1, N_DEV):
        g = g + p_ref[s].astype(F32)
    return g


def _adamw_math(g, w, m, v):
    nm = ADAM_B1 * m + (1.0 - ADAM_B1) * g
    nv = ADAM_B2 * v + (1.0 - ADAM_B2) * (g * g)
    m_hat = nm / (1.0 - ADAM_B1 ** ADAM_STEP)
    v_hat = nv / (1.0 - ADAM_B2 ** ADAM_STEP)
    return -ADAM_LR * (m_hat / (jnp.sqrt(v_hat) + ADAM_EPS) + ADAM_WD * w), nm, nv


def _row_tile(rows):
    for t in (256, 176, 128, 80):
        if rows % t == 0:
            return t
    raise ValueError(rows)


def _reduce_adamw(parts, w, m, v, name):
    rows, width = w.shape
    tr = _row_tile(rows)

    def body(p_ref, w_ref, m_ref, v_ref, g_ref, d_ref, nm_ref, nv_ref):
        g = _sum_parts(p_ref)
        g_ref[...] = g
        d_ref[...], nm_ref[...], nv_ref[...] = _adamw_math(g, w_ref[...], m_ref[...], v_ref[...])

    blk = pl.BlockSpec((tr, width), _row)
    return pl.pallas_call(
        body, grid=(rows // tr,),
        in_specs=[pl.BlockSpec((N_DEV, tr, width), lambda i: (0, i, 0)), blk, blk, blk],
        out_specs=[blk] * 4, out_shape=[SDS((rows, width), F32)] * 4,
        name="adamw_" + name, compiler_params=_params(("arbitrary",), 32))(parts, w, m, v)


def _reduce(parts, name):
    _, rows, width = parts.shape
    tr = _row_tile(rows)

    def body(p_ref, g_ref):
        g_ref[...] = _sum_parts(p_ref)

    return pl.pallas_call(
        body, grid=(rows // tr,),
        in_specs=[pl.BlockSpec((N_DEV, tr, width), lambda i: (0, i, 0))],
        out_specs=pl.BlockSpec((tr, width), _row), out_shape=SDS((rows, width), F32),
        name="sum_" + name, compiler_params=_params(("arbitrary",), 32))(parts)


def _adamw(g, w, m, v, name):
    rows, width = w.shape
    tr = _row_tile(rows)

    def body(g_ref, w_ref, m_ref, v_ref, d_ref, nm_ref, nv_ref):
        d_ref[...], nm_ref[...], nv_ref[...] = _adamw_math(g_ref[...], w_ref[...], m_ref[...], v_ref[...])

    blk = pl.BlockSpec((tr, width), _row)
    return pl.pallas_call(
        body, grid=(rows // tr,), in_specs=[blk] * 4, out_specs=[blk] * 3, out_shape=[SDS((rows, width), F32)] * 3,
        name="adamw_" + name, compiler_params=_params(("arbitrary",), 32))(g, w, m, v)


SMALL = ("w_spatial", "ln_pre_mix", "ln_post_mix", "ln_pre_ffn", "ln_post_ffn", "b_pe_gate",
         "attn_out_norm", "sgu_out_norm", "b_spatial", "sgu_ln_g", "sgu_ln_b")
SMALL_GROUPS = ((128, ("w_spatial", "b_spatial", "sgu_ln_g", "sgu_ln_b")),
                (512, ("attn_out_norm", "sgu_out_norm")),
                (1024, ("ln_post_mix", "ln_pre_ffn", "ln_post_ffn", "b_pe_gate")))
SMALL_LATE = "ln_pre_mix"
SMALL_SIZE = dict(w_spatial=N_GROUPS * CHUNK * CHUNK, b_spatial=N_GROUPS * CHUNK, sgu_ln_g=GROUP_DIM, sgu_ln_b=GROUP_DIM,
                  attn_out_norm=ATTN_W, sgu_out_norm=SGU_W, ln_pre_mix=D_MODEL, ln_post_mix=D_MODEL, ln_pre_ffn=D_MODEL,
                  ln_post_ffn=D_MODEL, b_pe_gate=D_MODEL)
SUBLANES = 8
ROW_SHARDED = ("w_out", "w_down", "w_pe_gate")
COL_SHARDED = ("w_in", "w_gate_up", "w_pe_proj")
WEIGHTS = ("ln_pre_mix", "w_in", "sgu_ln_g", "sgu_ln_b", "w_spatial", "b_spatial", "attn_out_norm", "sgu_out_norm",
           "w_out", "ln_post_mix", "ln_pre_ffn", "w_gate_up", "w_down", "ln_post_ffn", "w_pe_gate", "b_pe_gate",
           "w_pe_proj")


def _group_rows(width, names, extra=0):
    rows = sum(SMALL_SIZE[n] // width for n in names) + extra
    return -(-rows // SUBLANES) * SUBLANES


def _pack_small_grads(gs, loss_term):
    packed = []
    for width, names in SMALL_GROUPS:
        rows = [gs[n].reshape(-1, width) for n in names]
        extra = int(width == D_MODEL)
        if extra:
            rows.append(jnp.full((1, width), loss_term, F32))
        used = sum(r.shape[0] for r in rows)
        rows.append(jnp.zeros((_group_rows(width, names, extra) - used, width), F32))
        packed.append(jnp.concatenate(rows, axis=0))
    return packed


def _small_adamw(arrived, arrived_late, w, m, v):
    names = [n for _, ns in SMALL_GROUPS for n in ns] + [SMALL_LATE]
    n_groups = len(SMALL_GROUPS)

    def body(*refs):
        group_refs, late_ref = refs[:n_groups], refs[n_groups]
        state = refs[n_groups + 1:n_groups + 1 + 3 * len(names)]
        outs = refs[n_groups + 1 + 3 * len(names):]
        sums = [_sum_parts(r) for r in group_refs]

        def update(name, g):
            i = names.index(name)
            w_ref, m_ref, v_ref = state[3 * i:3 * i + 3]
            delta, nm, nv = _adamw_math(g, w_ref[...].reshape(g.shape), m_ref[...].reshape(g.shape),
                                        v_ref[...].reshape(g.shape))
            for o_ref, val in zip(outs[4 * i:4 * i + 4], (g, delta, nm, nv)):
                o_ref[...] = val.reshape(o_ref.shape)

        for (width, group), total in zip(SMALL_GROUPS, sums):
            row = 0
            for name in group:
                rows = SMALL_SIZE[name] // width
                update(name, total[row:row + rows, :])
                row += rows
            if width == D_MODEL:
                outs[-1][...] = total[row:row + 1, :LANES]
        update(SMALL_LATE, _sum_parts(late_ref)[:1, :])

    state = [t[n] for n in names for t in (w, m, v)]
    plain = jax.ShapeDtypeStruct
    out_shape = [plain(w[n].shape, F32) for n in names for _ in range(4)] + [plain((1, LANES), F32)]
    outs = pl.pallas_call(body, out_shape=out_shape, name="adamw_small",
                          compiler_params=pltpu.CompilerParams(vmem_limit_bytes=32 * MIB))(*arrived, arrived_late, *state)
    return {n: outs[4 * i:4 * i + 4] for i, n in enumerate(names)}, outs[-1]


def _slabs(full):
    return full.reshape(N_DEV, full.shape[0] // N_DEV, full.shape[1])


def kernel(x, p, ln_pre_mix, w_in, sgu_ln_g, sgu_ln_b, w_spatial, b_spatial, attn_out_norm, sgu_out_norm, w_out, ln_post_mix, ln_pre_ffn, w_gate_up, w_down, ln_post_ffn, w_pe_gate, b_pe_gate, w_pe_proj, loss_target, m_ln_pre_mix, m_w_in, m_sgu_ln_g, m_sgu_ln_b, m_w_spatial, m_b_spatial, m_attn_out_norm, m_sgu_out_norm, m_w_out, m_ln_post_mix, m_ln_pre_ffn, m_w_gate_up, m_w_down, m_ln_post_ffn, m_w_pe_gate, m_b_pe_gate, m_w_pe_proj, v_ln_pre_mix, v_w_in, v_sgu_ln_g, v_sgu_ln_b, v_w_spatial, v_b_spatial, v_attn_out_norm, v_sgu_out_norm, v_w_out, v_ln_post_mix, v_ln_pre_ffn, v_w_gate_up, v_w_down, v_ln_post_ffn, v_w_pe_gate, v_b_pe_gate, v_w_pe_proj):
    given = dict(locals())
    w = {n: given[n] for n in WEIGHTS}
    m = {n: given["m_" + n] for n in WEIGHTS}
    v = {n: given["v_" + n] for n in WEIGHTS}
    xs, ps, target = x[0], p[0, 0], loss_target[0]

    shard = {n: w[n][0].astype(BF) for n in ROW_SHARDED}
    shard.update({n: w[n][0].T.astype(BF) for n in COL_SHARDED})
    sm = {n: w[n][0] for n in SMALL}
    sm = {n: (a.reshape(1, -1) if a.ndim == 1 else a) for n, a in sm.items()}
    slopes = jnp.broadcast_to((2.0 ** -(jnp.arange(8, dtype=F32) + 1.0)).reshape(4, 2, 1), (4, 2, LANES))
    slopes = jnp.concatenate([slopes, jnp.zeros((4, 6, LANES), F32)], axis=1)
    b_st = jnp.pad(sm["b_spatial"].T, ((0, 0), (0, LANES - N_GROUPS)))

    def full(gathered):
        return gathered.reshape(-1, gathered.shape[-1])

    w_in_t = full(_all_gather(shard["w_in"], "gather_w_in"))
    proj, a = _in_proj(xs, sm["ln_pre_mix"], w_in_t)
    later = ("w_out", "w_gate_up", "w_down", "w_pe_gate", "w_pe_proj")
    attn, lse, gathered = _attn_fwd(proj, slopes, [shard[n] for n in later])
    w_out_f, w_gu_t, w_down_f, w_peg_f, w_pep_t = [full(g) for g in gathered]
    sgu = _sgu_fwd(proj, sm["sgu_ln_g"], sm["sgu_ln_b"], sm["w_spatial"], b_st)
    groups, mixed, h1, f = _out_proj(attn, sgu, xs, sm["attn_out_norm"], sm["sgu_out_norm"], w_out_f,
                                     sm["ln_post_mix"], sm["ln_pre_ffn"])
    g, u, act = _gate_up(f, w_gu_t)
    y, h2 = _down_proj(act, w_down_f, h1, sm["ln_post_ffn"])
    dh2, dy, g_peg, g_pep_t, loss_cols, db_peg, d_pff = _pe_loss_and_bwd(
        h2, ps, target, y, w_peg_f, sm["b_pe_gate"], w_pep_t, sm["ln_post_ffn"])
    loss_term = 0.5 * jnp.sum(loss_cols) * (1.0 / D_MODEL)

    arrived = {}
    g_down = _weight_grad(act, dy, "w_down")
    dg, du, (arrived["w_pe_proj"], arrived["w_pe_gate"]) = _down_bwd(dy, w_down_f, g, u, [_slabs(g_pep_t), _slabs(g_peg)])
    dh1, dmx, d_pf, d_pm, (arrived["w_down"],) = _ffn_in_bwd(dg, du, w_gu_t, h1, dh2, mixed, sm["ln_pre_ffn"],
                                                            sm["ln_post_mix"], [_slabs(g_down)])
    dattn, stats, dsgu, d_ga, d_gs, g_gu_t = _out_bwd(dmx, w_out_f, attn, lse, sgu, sm["attn_out_norm"],
                                                      sm["sgu_out_norm"], dg, f)
    duz, d_ws, d_bst, d_lg, d_lb, g_out, g_gu_t = _sgu_bwd(proj, dsgu, sm["sgu_ln_g"], sm["sgu_ln_b"],
                                                           sm["w_spatial"], b_st, groups, dmx, du, f, g_gu_t)
    gs = dict(sgu_ln_g=d_lg, sgu_ln_b=d_lb, w_spatial=d_ws, b_spatial=d_bst[:, :N_GROUPS].T, attn_out_norm=d_ga,
              sgu_out_norm=d_gs, ln_post_mix=d_pm, ln_pre_ffn=d_pf, ln_post_ffn=d_pff, b_pe_gate=db_peg)
    small_grads = _pack_small_grads(gs, loss_term)
    dqkv, (arrived["w_gate_up"], arrived["w_out"], *arrived_small) = _attn_bwd(
        proj, dattn, stats, slopes, [_slabs(g_gu_t), _slabs(g_out), *small_grads],
        [True, True] + [False] * len(small_grads))
    g_in_t = _grad_w_in([dqkv, duz], a)
    grad_x, d_g1, (arrived["w_in"],) = _in_bwd(
        [dqkv, duz], w_in_t, xs, dh1, sm["ln_pre_mix"], [_slabs(g_in_t)], [True])
    (arrived_late,) = _scatter_call([jnp.pad(d_g1, ((0, SUBLANES - 1), (0, 0)))], [False], "ln_pre_mix_grad_exchange")

    res = {}
    for n in ROW_SHARDED:
        res[n] = _reduce_adamw(arrived[n], w[n][0], m[n][0], v[n][0], n)
    for n in ("w_in", "w_gate_up"):
        res[n] = [t.T for t in _reduce_adamw(arrived[n], w[n][0].T, m[n][0].T, v[n][0].T, n)]
    for n in ("w_pe_proj",):
        grad = _reduce(arrived[n], n).T
        res[n] = (grad, *_adamw(grad, w[n][0], m[n][0], v[n][0], n))
    small, loss_row = _small_adamw(arrived_small, arrived_late, w, m, v)

    out = []
    for k in range(4):
        out += [res[n][k][None] if n in res else small[n][k] for n in WEIGHTS]
    return (loss_row[0, 0], grad_x[None], *out)
```

```python
import math

import jax
import jax.numpy as jnp
from jax import lax
from jax.experimental import pallas as pl
from jax.experimental.pallas import tpu as pltpu

F32 = jnp.float32
BF = jnp.bfloat16


def SDS(shape, dtype):
    return pltpu.HBM(tuple(shape), dtype)

D_MODEL = 1024
ATTN_W = 512
SGU_W = 512
HEAD_DIM = 64
N_GROUPS = 4
GROUP_DIM = 128
CHUNK = 128
D_FF = 2816
PLE_DIM = 256
PROJ = 3 * ATTN_W + 2 * SGU_W
DILATIONS = (1, 4, 16)
QBLK = 128
EPS = 1e-6
NEG = -1e30
N_DEV = 8
LANES = 128

ADAM_LR = 0.001
ADAM_B1 = 0.9
ADAM_B2 = 0.999
ADAM_EPS = 1e-08
ADAM_WD = 0.01
ADAM_STEP = 10

MIB = 2 ** 20
MESH_ID = pl.DeviceIdType.MESH
HBM = pl.BlockSpec(memory_space=pl.ANY)


def _params(sem, vmem_mib):
    return pltpu.CompilerParams(dimension_semantics=sem, vmem_limit_bytes=vmem_mib * MIB)


def _dot(a, b):
    return jnp.dot(a, b, preferred_element_type=F32)


def _dot_nt(a, b):
    return lax.dot_general(a, b, (((1,), (1,)), ((), ())), preferred_element_type=F32)


def _dot_tn(a, b):
    return lax.dot_general(a, b, (((0,), (0,)), ((), ())), preferred_element_type=F32)


def _rstd(x):
    return lax.rsqrt(jnp.mean(x * x, axis=-1, keepdims=True) + EPS)


def _rms_bwd(dy, x, g):
    r = _rstd(x)
    n = x * r
    dn = dy * g
    dx = r * (dn - n * jnp.mean(dn * n, axis=-1, keepdims=True))
    return dx, dy * n


def _colsum(v):
    return jnp.sum(v, axis=0, keepdims=True)


_G0 = math.sqrt(2.0 / math.pi)
_G1 = 0.044715


def _gelu(x):
    t = jnp.tanh(_G0 * (x + _G1 * x * x * x))
    return 0.5 * x * (1.0 + t), t


def _gelu_grad(x, t):
    return 0.5 * (1.0 + t) + 0.5 * x * (1.0 - t * t) * (_G0 * (1.0 + 3.0 * _G1 * x * x))


def _sigmoid(x):
    return 0.5 * jnp.tanh(0.5 * x) + 0.5


def _row(i):
    return (i, 0)


def _fixed(i):
    return (0, 0)


def _acc_init(step, *refs):
    @pl.when(step == 0)
    def _():
        for r in refs:
            r[...] = jnp.zeros_like(r)


FLIPS = [(dx, dy, dc) for dx in (0, 1) for dy in (0, 1) for dc in (0, 1)][1:]
DMA_SEMS = pltpu.SemaphoreType.DMA


def _mesh_pos():
    return lax.axis_index("x"), lax.axis_index("y"), lax.axis_index("c")


def _remote(src, dst, sems, n, to):
    return pltpu.make_async_remote_copy(src_ref=src, dst_ref=dst, send_sem=sems[0].at[n], recv_sem=sems[1].at[n],
                                        device_id=to, device_id_type=MESH_ID)


class _Scatter:
    def __init__(self, items, sems):
        x, y, c = _mesh_pos()
        me = 4 * x + 2 * y + c
        self.local, self.sends, self.arrivals = [], [], []
        for i, (src, dst, slabbed) in enumerate(items):
            self.local.append(pltpu.make_async_copy(src.at[me] if slabbed else src, dst.at[me], sems[2].at[i]))
            for k, (dx, dy, dc) in enumerate(FLIPS):
                to = (1 - x if dx else x, 1 - y if dy else y, 1 - c if dc else c)
                peer = 4 * to[0] + 2 * to[1] + to[2]
                out = src.at[peer] if slabbed else src
                self.sends.append(_remote(out, dst.at[me], sems, 7 * i + k, to))
                self.arrivals.append(_remote(out, dst.at[peer], sems, 7 * i + k, to))

    def start(self):
        for cp in self.local + self.sends:
            cp.start()

    def wait(self):
        for cp in self.arrivals:
            cp.wait_recv()
        for cp in self.sends:
            cp.wait_send()
        for cp in self.local:
            cp.wait()


def _scatter_sems(n):
    return [DMA_SEMS((7 * n,)), DMA_SEMS((7 * n,)), DMA_SEMS((n,))]


class _Gather:
    def __init__(self, items, sems):
        x, y, c = _mesh_pos()
        me, sibling = (x, y, c), (x, y, 1 - c)
        chips = [(1 - x, y), (x, 1 - y), (1 - x, 1 - y)]
        self.first, self.passed, self.from_chips, self.rest, self.local = [], [], [], [], []
        for i, (src, dst) in enumerate(items):
            def slot(p, dst=dst):
                return dst.at[4 * p[0] + 2 * p[1] + p[2]]

            def copy(k, block, to, own=False, i=i, src=src, slot=slot):
                return _remote(src if own else slot(block), slot(block), sems, 7 * i + k, to)

            self.local.append(pltpu.make_async_copy(src, slot(me), sems[2].at[i]))
            self.first.append(copy(0, me, sibling, own=True))
            self.first += [copy(1 + j, me, (*chip, c), own=True) for j, chip in enumerate(chips)]
            self.passed += [copy(4 + j, (*chip, c), sibling) for j, chip in enumerate(chips)]
            self.from_chips += [copy(1 + j, (*chip, c), me) for j, chip in enumerate(chips)]
            self.rest.append(copy(0, sibling, me))
            self.rest += [copy(4 + j, (*chip, 1 - c), me) for j, chip in enumerate(chips)]

    def start(self):
        for cp in self.local + self.first:
            cp.start()

    def forward(self):
        for arrived, onward in zip(self.from_chips, self.passed):
            arrived.wait_recv()
            onward.start()

    def finish(self):
        for cp in self.rest:
            cp.wait_recv()
        for cp in self.first + self.passed:
            cp.wait_send()
        for cp in self.local:
            cp.wait()


def _all_gather(shard, name):
    def body(x_ref, out_ref, *sems):
        g = _Gather([(x_ref, out_ref)], sems)
        g.start()
        g.forward()
        g.finish()

    return pl.pallas_call(
        body, out_shape=SDS((N_DEV,) + shard.shape, shard.dtype), in_specs=[HBM], out_specs=HBM,
        scratch_shapes=_scatter_sems(1), name=name)(shard)


def _scatter_call(srcs, slabbed, name):
    n = len(srcs)

    def body(*refs):
        sc = _Scatter(list(zip(refs[:n], refs[n:2 * n], slabbed)), refs[2 * n:])
        sc.start()
        sc.wait()

    shapes = [SDS(s.shape if sl else (N_DEV,) + s.shape, s.dtype) for s, sl in zip(srcs, slabbed)]
    return pl.pallas_call(body, out_shape=shapes, in_specs=[HBM] * n, out_specs=[HBM] * n,
                          scratch_shapes=_scatter_sems(n), name=name)(*srcs)


def _in_proj(x, g1, w_in_t, tm=512):
    T = x.shape[0]

    def body(x_ref, g_ref, w_ref, proj_ref, a_ref):
        xv = x_ref[...]
        a = (xv * _rstd(xv) * g_ref[...]).astype(BF)
        a_ref[...] = a
        proj_ref[...] = _dot_nt(a, w_ref[...])

    return pl.pallas_call(
        body, grid=(T // tm,),
        in_specs=[pl.BlockSpec((tm, D_MODEL), _row), pl.BlockSpec((1, D_MODEL), _fixed),
                  pl.BlockSpec((PROJ, D_MODEL), _fixed)],
        out_specs=[pl.BlockSpec((tm, PROJ), _row), pl.BlockSpec((tm, D_MODEL), _row)],
        out_shape=[SDS((T, PROJ), F32), SDS((T, D_MODEL), BF)],
        name="in_proj", compiler_params=_params(("arbitrary",), 48))(x, g1, w_in_t)


ATTN_GROUP = 16


def _attn_bias(sl_ref, bias_ref):
    qi = lax.broadcasted_iota(jnp.int32, (QBLK, QBLK), 0)
    kj = lax.broadcasted_iota(jnp.int32, (QBLK, QBLK), 1)
    step = qi - kj
    for di, d in enumerate(DILATIONS):
        for j in range(2):
            sl = sl_ref[0, j:j + 1, :]
            cur = jnp.where(step >= 0, -sl * (step * d).astype(F32), NEG)
            prev = jnp.where(step <= 0, -sl * ((step + QBLK) * d).astype(F32), NEG)
            rows = slice(j * QBLK, (j + 1) * QBLK)
            bias_ref[di * 2, rows, :QBLK] = prev
            bias_ref[di * 2, rows, QBLK:] = cur
            bias_ref[di * 2 + 1, rows, :QBLK] = jnp.full((QBLK, QBLK), NEG, F32)
            bias_ref[di * 2 + 1, rows, QBLK:] = cur


def _stack_heads(x, lo):
    return jnp.concatenate([jnp.where(lo, x, 0.0), jnp.where(lo, 0.0, x)], axis=0).astype(BF)


def _unstack_heads(x, lo):
    return jnp.where(lo, x[:QBLK], x[QBLK:])


def _attn_rows(start, d, blocks=1):
    if d == 1:
        return pl.ds(pl.multiple_of(start, QBLK), blocks * QBLK)
    return pl.ds(start, blocks * QBLK, stride=d)


def _attn_group_index(i, nblk, d, group):
    per = nblk // d // group
    r = i // per
    n0 = (i % per) * group
    start = r + (d * QBLK) * n0
    pstart = jnp.maximum(start - d * QBLK, r)
    return start, pstart, n0 == 0


def _attn_plan(nblk, d):
    group = min(ATTN_GROUP, nblk // d)
    return group, max(1, min(ATTN_GROUP // group, d))


def _attn_fwd(proj, slopes, to_gather):
    T = proj.shape[0]
    nblk = T // QBLK
    n_g = len(to_gather)

    def body(q_ref, k_ref, v_ref, sl_ref, *rest):
        srcs, (o_ref, m_ref), dsts = rest[:n_g], rest[n_g:n_g + 2], rest[n_g + 2:2 * n_g + 2]
        sems, (l_ref, bias_ref) = rest[2 * n_g + 2:2 * n_g + 5], rest[2 * n_g + 5:]
        h = pl.program_id(0)

        @pl.when(h == 0)
        def _():
            _Gather(list(zip(srcs, dsts)), sems).start()

        @pl.when(h == pl.num_programs(0) - 1)
        def _():
            _Gather(list(zip(srcs, dsts)), sems).forward()

        _attn_bias(sl_ref, bias_ref)
        lo = lax.broadcasted_iota(jnp.int32, (1, LANES), 1) < HEAD_DIM

        order = list(enumerate(DILATIONS))[::-1]
        for di, d in order:
            group, segs = _attn_plan(nblk, d)
            fresh, last = di == order[0][0], di == order[-1][0]

            def step(i, carry, segs=segs, **kw):
                for s in range(segs):
                    segment(i * segs + s, **kw)
                return carry

            def segment(i, d=d, di=di, group=group, fresh=fresh, last=last):
                start, pstart, first = _attn_group_index(i, nblk, d, group)
                prows = _attn_rows(pstart, d)
                k_prev, v_prev = k_ref[prows, :].astype(BF), v_ref[prows, :].astype(BF)
                for b in range(group):
                    out = _attn_rows(start + b * (d * QBLK), d)
                    k_own, v_own = k_ref[out, :].astype(BF), v_ref[out, :].astype(BF)
                    k2, v2 = jnp.concatenate([k_prev, k_own], axis=0), jnp.concatenate([v_prev, v_own], axis=0)
                    k_prev, v_prev = k_own, v_own
                    bias = bias_ref[di * 2 + first.astype(jnp.int32)] if b == 0 else bias_ref[di * 2]
                    s = _dot_nt(_stack_heads(q_ref[out, :] * (HEAD_DIM ** -0.5), lo), k2) + bias
                    m = jnp.max(s, axis=-1, keepdims=True)
                    pr = jnp.exp(s - m)
                    m_b = _unstack_heads(m, lo)
                    l_b = _unstack_heads(jnp.sum(pr, axis=-1, keepdims=True), lo)
                    o_b = _unstack_heads(_dot(pr.astype(BF), v2), lo)
                    if fresh:
                        m_ref[out, :] = m_b
                        l_ref[out, :] = l_b
                        o_ref[out, :] = o_b
                        continue
                    m_o = m_ref[out, :]
                    m_n = jnp.maximum(m_o, m_b)
                    wa, wb = jnp.exp(m_o - m_n), jnp.exp(m_b - m_n)
                    l_n = wa * l_ref[out, :] + wb * l_b
                    o_n = wa * o_ref[out, :] + wb * o_b
                    if last:
                        m_ref[out, :] = m_n + jnp.log(l_n)
                        o_ref[out, :] = o_n / l_n
                    else:
                        m_ref[out, :] = m_n
                        l_ref[out, :] = l_n
                        o_ref[out, :] = o_n

            lax.fori_loop(0, nblk // (group * segs), step, 0)

        @pl.when(h == pl.num_programs(0) - 1)
        def _():
            _Gather(list(zip(srcs, dsts)), sems).finish()

    col = lambda base: pl.BlockSpec((T, LANES), lambda h: (0, base + h))
    tok = pl.BlockSpec((T, LANES), lambda h: (0, h))
    outs = pl.pallas_call(
        body, grid=(4,),
        in_specs=[col(0), col(4), col(8), pl.BlockSpec((1, 8, LANES), lambda h: (h, 0, 0))] + [HBM] * n_g,
        out_specs=[tok, tok] + [HBM] * n_g,
        out_shape=[SDS((T, ATTN_W), F32), SDS((T, ATTN_W), F32)]
        + [SDS((N_DEV,) + g.shape, g.dtype) for g in to_gather],
        scratch_shapes=_scatter_sems(n_g) + [pltpu.VMEM((T, LANES), F32), pltpu.VMEM((6, 2 * QBLK, 2 * QBLK), F32)],
        name="attn_fwd", compiler_params=_params(("arbitrary",), 56))(proj, proj, proj, slopes, *to_gather)
    return outs[0], outs[1], outs[2:]


def _sgu_norm(zv, ln_g, ln_b):
    gz, tz = _gelu(zv)
    mu = jnp.mean(gz, axis=-1, keepdims=True)
    xc = gz - mu
    rs = lax.rsqrt(jnp.mean(xc * xc, axis=-1, keepdims=True) + EPS)
    xhat = xc * rs
    return xhat * ln_g + ln_b, xhat, rs, tz


def _causal(w):
    i = lax.broadcasted_iota(jnp.int32, (CHUNK, CHUNK), 0)
    j = lax.broadcasted_iota(jnp.int32, (CHUNK, CHUNK), 1)
    return jnp.where(i >= j, w, 0.0)


def _sgu_fwd(proj, ln_g, ln_b, w_s, b_st, tm=512):
    T = proj.shape[0]

    def body(u_ref, z_ref, g_ref, b_ref, w_ref, bs_ref, out_ref):
        for g in range(N_GROUPS):
            wm = _causal(w_ref[g]).astype(BF)
            cols = slice(g * GROUP_DIM, (g + 1) * GROUP_DIM)
            for c in range(tm // CHUNK):
                rows = slice(c * CHUNK, (c + 1) * CHUNK)
                zn, _, _, _ = _sgu_norm(z_ref[rows, cols], g_ref[...], b_ref[...])
                mixed = _dot(wm, zn.astype(BF)) + bs_ref[:, g:g + 1]
                gu, _ = _gelu(u_ref[rows, cols])
                out_ref[rows, cols] = gu * mixed

    return pl.pallas_call(
        body, grid=(T // tm,),
        in_specs=[pl.BlockSpec((tm, SGU_W), lambda i: (i, 3)), pl.BlockSpec((tm, SGU_W), lambda i: (i, 4)),
                  pl.BlockSpec((1, GROUP_DIM), _fixed), pl.BlockSpec((1, GROUP_DIM), _fixed),
                  pl.BlockSpec((N_GROUPS, CHUNK, CHUNK), lambda i: (0, 0, 0)), pl.BlockSpec((CHUNK, LANES), _fixed)],
        out_specs=pl.BlockSpec((tm, SGU_W), _row),
        out_shape=SDS((T, SGU_W), F32),
        name="sgu_fwd", compiler_params=_params(("arbitrary",), 32))(proj, proj, ln_g, ln_b, w_s, b_st)


def _out_proj(attn, sgu, x, g_a, g_s, w_out, g_pm, g_pf, tm=512):
    T = x.shape[0]

    def body(a_ref, s_ref, x_ref, ga_ref, gs_ref, w_ref, gpm_ref, gpf_ref, grp_ref, mixed_ref, h1_ref, f_ref):
        av, sv = a_ref[...], s_ref[...]
        an = (av * _rstd(av) * ga_ref[...]).astype(BF)
        sn = (sv * _rstd(sv) * gs_ref[...]).astype(BF)
        grp_ref[:, :ATTN_W] = an
        grp_ref[:, ATTN_W:] = sn
        mixed = _dot(an, w_ref[:ATTN_W, :]) + _dot(sn, w_ref[ATTN_W:, :])
        mixed_ref[...] = mixed
        h1 = x_ref[...] + mixed * _rstd(mixed) * gpm_ref[...]
        h1_ref[...] = h1
        f_ref[...] = (h1 * _rstd(h1) * gpf_ref[...]).astype(BF)

    tok = lambda w: pl.BlockSpec((tm, w), _row)
    vec = lambda w: pl.BlockSpec((1, w), _fixed)
    return pl.pallas_call(
        body, grid=(T // tm,),
        in_specs=[tok(ATTN_W), tok(SGU_W), tok(D_MODEL), vec(ATTN_W), vec(SGU_W),
                  pl.BlockSpec((D_MODEL, D_MODEL), _fixed), vec(D_MODEL), vec(D_MODEL)],
        out_specs=[tok(D_MODEL)] * 4,
        out_shape=[SDS((T, D_MODEL), BF), SDS((T, D_MODEL), F32), SDS((T, D_MODEL), F32), SDS((T, D_MODEL), BF)],
        name="out_proj", compiler_params=_params(("arbitrary",), 48))(attn, sgu, x, g_a, g_s, w_out, g_pm, g_pf)


FF_TILE = 1408
FF_TILES = D_FF // FF_TILE
FF_CHUNK = 256


def _gate_up(f, w_gu_t, tm=512):
    T = f.shape[0]
    tn = FF_TILE

    def body(f_ref, wg_ref, wu_ref, g_ref, u_ref, act_ref):
        fv = f_ref[...]
        g = _dot_nt(fv, wg_ref[...])
        u = _dot_nt(fv, wu_ref[...])
        g_ref[...] = g.astype(BF)
        u_ref[...] = u.astype(BF)
        act_ref[...] = (g * _sigmoid(g) * u).astype(BF)

    ospec = pl.BlockSpec((tm, tn), lambda j, i: (i, j))
    return pl.pallas_call(
        body, grid=(FF_TILES, T // tm),
        in_specs=[pl.BlockSpec((tm, D_MODEL), lambda j, i: (i, 0)), pl.BlockSpec((tn, D_MODEL), lambda j, i: (j, 0)),
                  pl.BlockSpec((tn, D_MODEL), lambda j, i: (j + FF_TILES, 0))],
        out_specs=[ospec] * 3, out_shape=[SDS((T, D_FF), BF)] * 3,
        name="gate_up", compiler_params=_params(("arbitrary", "arbitrary"), 40))(f, w_gu_t, w_gu_t)


def _down_proj(act, w_down, h1, g_pff, tm=512):
    T = act.shape[0]

    def body(a_ref, w_ref, h1_ref, g_ref, y_ref, h2_ref):
        y = _dot(a_ref[...], w_ref[...])
        y_ref[...] = y
        h2_ref[...] = h1_ref[...] + y * _rstd(y) * g_ref[...]

    return pl.pallas_call(
        body, grid=(T // tm,),
        in_specs=[pl.BlockSpec((tm, D_FF), _row), pl.BlockSpec((D_FF, D_MODEL), _fixed),
                  pl.BlockSpec((tm, D_MODEL), _row), pl.BlockSpec((1, D_MODEL), _fixed)],
        out_specs=[pl.BlockSpec((tm, D_MODEL), _row)] * 2,
        out_shape=[SDS((T, D_MODEL), F32)] * 2,
        name="down_proj", compiler_params=_params(("arbitrary",), 48))(act, w_down, h1, g_pff)


def _pe_loss_and_bwd(h2, p, target, y, w_peg, b_peg, w_pep_t, g_pff, tm=512):
    T = h2.shape[0]

    def body(h2_ref, p_ref, t_ref, y_ref, wg_ref, b_ref, wp_ref, g_ref,
             dh2_ref, dy_ref, gpeg_ref, gpep_ref, loss_ref, db_ref, dg_ref):
        _acc_init(pl.program_id(0), gpeg_ref, gpep_ref, loss_ref, db_ref, dg_ref)
        h2v = h2_ref[...]
        h2b = h2v.astype(BF)
        pb = p_ref[...].astype(BF)
        gate = _sigmoid(_dot(h2b, wg_ref[...]) + b_ref[...])
        pp = _dot_nt(pb, wp_ref[...])
        diff = h2v + gate * pp - t_ref[...]
        loss_ref[...] += _colsum(diff * diff)
        dh3 = diff * (1.0 / D_MODEL)
        dpre = dh3 * pp * (gate * (1.0 - gate))
        dpre_b = dpre.astype(BF)
        db_ref[...] += _colsum(dpre)
        gpeg_ref[...] += _dot_tn(h2b, dpre_b)
        gpep_ref[...] += _dot_tn((dh3 * gate).astype(BF), pb)
        dh2 = dh3 + _dot_nt(dpre_b, wg_ref[...])
        dh2_ref[...] = dh2
        dy, dg = _rms_bwd(dh2, y_ref[...], g_ref[...])
        dy_ref[...] = dy.astype(BF)
        dg_ref[...] += _colsum(dg)

    tok = lambda w: pl.BlockSpec((tm, w), _row)
    vec = pl.BlockSpec((1, D_MODEL), _fixed)
    wg = pl.BlockSpec((D_MODEL, D_MODEL), _fixed)
    wp = pl.BlockSpec((D_MODEL, PLE_DIM), _fixed)
    return pl.pallas_call(
        body, grid=(T // tm,),
        in_specs=[tok(D_MODEL), tok(PLE_DIM), tok(D_MODEL), tok(D_MODEL), wg, vec, wp, vec],
        out_specs=[tok(D_MODEL), tok(D_MODEL), wg, wp, vec, vec, vec],
        out_shape=[SDS((T, D_MODEL), F32), SDS((T, D_MODEL), BF), SDS((D_MODEL, D_MODEL), F32),
                   SDS((D_MODEL, PLE_DIM), F32)] + [SDS((1, D_MODEL), F32)] * 3,
        name="pe_loss_and_bwd", compiler_params=_params(("arbitrary",), 56))(
            h2, p, target, y, w_peg, b_peg, w_pep_t, g_pff)


def _weight_grad(a, dy, name, into=None, row_tile=0, rows=None, tk=512):
    n = dy.shape[1]
    tn = min(n, 1024)
    T, ka = a.shape
    tka = FF_TILE if ka == D_FF else min(ka, 1024)
    rows = ka if rows is None else rows

    def body(a_ref, dy_ref, *rest):
        out_ref = rest[-1]
        _acc_init(pl.program_id(2), out_ref)
        out_ref[...] += _dot_tn(a_ref[...].astype(BF), dy_ref[...].astype(BF))

    carried = [] if into is None else [into]
    return pl.pallas_call(
        body, grid=(ka // tka, n // tn, T // tk),
        in_specs=[pl.BlockSpec((tk, tka), lambda i, j, k: (k, i)), pl.BlockSpec((tk, tn), lambda i, j, k: (k, j))]
        + [HBM] * len(carried),
        out_specs=pl.BlockSpec((tka, tn), lambda i, j, k: (i + row_tile, j)),
        out_shape=SDS((rows, n), F32), input_output_aliases={2: 0} if carried else {},
        name="grad_" + name, compiler_params=_params(("arbitrary",) * 3, 40))(a, dy, *carried)


def _grad_w_in(dparts, a, tk=512):
    T = a.shape[0]

    def body(*refs):
        d_refs, a_ref, out_ref, acc_ref = refs[:len(dparts)], refs[-3], refs[-2], refs[-1]
        k = pl.program_id(0)
        _acc_init(k, acc_ref)
        cols = [r[part].astype(BF) for r in d_refs for part in range(r.shape[0])]
        acc_ref[...] += _dot_tn(jnp.concatenate(cols, axis=1), a_ref[...])

        @pl.when(k == pl.num_programs(0) - 1)
        def _():
            out_ref[...] = acc_ref[...].astype(BF)

    return pl.pallas_call(
        body, grid=(T // tk,),
        in_specs=[pl.BlockSpec((d.shape[0], tk, d.shape[2]), lambda k: (0, k, 0)) for d in dparts]
        + [pl.BlockSpec((tk, D_MODEL), lambda k: (k, 0))],
        out_specs=pl.BlockSpec((PROJ, D_MODEL), lambda k: (0, 0)),
        out_shape=SDS((PROJ, D_MODEL), BF), scratch_shapes=[pltpu.VMEM((PROJ, D_MODEL), F32)],
        name="grad_w_in", compiler_params=_params(("arbitrary",), 48))(*dparts, a)


def _down_bwd(dy, w_down, g, u, to_send, tm=512):
    T = dy.shape[0]
    n_s = len(to_send)

    def body(dy_ref, w_ref, g_ref, u_ref, *rest):
        srcs, (dg_ref, du_ref), dsts, sems = rest[:n_s], rest[n_s:n_s + 2], rest[n_s + 2:2 * n_s + 2], rest[2 * n_s + 2:]
        i = pl.program_id(0)
        items = list(zip(srcs, dsts, [True] * n_s))

        @pl.when(i == 0)
        def _():
            _Scatter(items, sems).start()

        dyv = dy_ref[...]
        for c in range(D_FF // FF_CHUNK):
            cols = slice(c * FF_CHUNK, (c + 1) * FF_CHUNK)
            dact = _dot_nt(dyv, w_ref[cols, :]).astype(BF)
            gv, uv = g_ref[:, cols], u_ref[:, cols]
            s = _sigmoid(gv)
            ds = dact * s
            dg_ref[:, cols] = ds * uv * (1.0 + gv * (1.0 - s))
            du_ref[:, cols] = ds * gv

        @pl.when(i == pl.num_programs(0) - 1)
        def _():
            _Scatter(items, sems).wait()

    tile = pl.BlockSpec((tm, D_FF), _row)
    outs = pl.pallas_call(
        body, grid=(T // tm,),
        in_specs=[pl.BlockSpec((tm, D_MODEL), _row),
                  pl.BlockSpec((D_FF, D_MODEL), _fixed, pipeline_mode=pl.Buffered(1)), tile, tile] + [HBM] * n_s,
        out_specs=[tile, tile] + [HBM] * n_s,
        out_shape=[SDS((T, D_FF), BF)] * 2 + [SDS(s.shape, s.dtype) for s in to_send],
        scratch_shapes=_scatter_sems(n_s),
        name="down_bwd", compiler_params=_params(("arbitrary",), 48))(dy, w_down, g, u, *to_send)
    return outs[0], outs[1], outs[2:]


def _ffn_in_bwd(dg, du, w_gu_t, h1, dh2, mixed, g_pf, g_pm, to_send, tm=512):
    T = h1.shape[0]
    n_s = len(to_send)

    def body(dg_ref, du_ref, wg_ref, wu_ref, h1_ref, dh2_ref, mx_ref, gpf_ref, gpm_ref, *rest):
        srcs, outs, dsts, sems = rest[:n_s], rest[n_s:n_s + 4], rest[n_s + 4:2 * n_s + 4], rest[2 * n_s + 4:]
        dh1_ref, dmx_ref, dgpf_ref, dgpm_ref = outs
        i = pl.program_id(0)
        items = list(zip(srcs, dsts, [True] * n_s))
        _acc_init(i, dgpf_ref, dgpm_ref)

        @pl.when(i == 0)
        def _():
            _Scatter(items, sems).start()

        df = _dot(dg_ref[...], wg_ref[...]) + _dot(du_ref[...], wu_ref[...])
        dx, dgf = _rms_bwd(df, h1_ref[...], gpf_ref[...])
        dh1 = dh2_ref[...] + dx
        dh1_ref[...] = dh1
        dmx, dgm = _rms_bwd(dh1, mx_ref[...], gpm_ref[...])
        dmx_ref[...] = dmx.astype(BF)
        dgpf_ref[...] += _colsum(dgf)
        dgpm_ref[...] += _colsum(dgm)

        @pl.when(i == pl.num_programs(0) - 1)
        def _():
            _Scatter(items, sems).wait()

    tok = lambda w: pl.BlockSpec((tm, w), _row)
    vec = pl.BlockSpec((1, D_MODEL), _fixed)
    outs = pl.pallas_call(
        body, grid=(T // tm,),
        in_specs=[tok(D_FF), tok(D_FF), pl.BlockSpec((D_FF, D_MODEL), lambda i: (0, 0), pipeline_mode=pl.Buffered(1)),
                  pl.BlockSpec((D_FF, D_MODEL), lambda i: (1, 0), pipeline_mode=pl.Buffered(1)),
                  tok(D_MODEL), tok(D_MODEL), tok(D_MODEL), vec, vec]
        + [HBM] * n_s,
        out_specs=[tok(D_MODEL), tok(D_MODEL), vec, vec] + [HBM] * n_s,
        out_shape=[SDS((T, D_MODEL), F32), SDS((T, D_MODEL), BF), SDS((1, D_MODEL), F32), SDS((1, D_MODEL), F32)]
        + [SDS(s.shape, s.dtype) for s in to_send],
        scratch_shapes=_scatter_sems(n_s),
        name="ffn_in_bwd", compiler_params=_params(("arbitrary",), 56))(
            dg, du, w_gu_t, w_gu_t, h1, dh2, mixed, g_pf, g_pm, *to_send)
    return outs[0], outs[1], outs[2], outs[3], outs[4:]


STAT_LANES = HEAD_DIM // 2


def _ff_grad_spec(half):
    return pl.BlockSpec((D_FF, D_MODEL), lambda i: (half, 0), pipeline_mode=pl.Buffered(1))


def _out_bwd(dmx, w_out, attn, lse, sgu, g_a, g_s, dg, f, tm=512):
    T = attn.shape[0]

    def body(dm_ref, w_ref, a_ref, l_ref, s_ref, ga_ref, gs_ref, dgate_ref, f_ref,
             da_ref, st_ref, ds_ref, dga_ref, dgs_ref, ggu_ref):
        _acc_init(pl.program_id(0), dga_ref, dgs_ref, ggu_ref)
        ggu_ref[...] += _dot_tn(dgate_ref[...], f_ref[...])
        dgr = _dot_nt(dm_ref[...], w_ref[...])
        av = a_ref[...]
        da, dga = _rms_bwd(dgr[:, :ATTN_W], av, ga_ref[...])
        ds, dgs = _rms_bwd(dgr[:, ATTN_W:], s_ref[...], gs_ref[...])
        da_ref[...] = da
        ds_ref[...] = ds
        dga_ref[...] += _colsum(dga)
        dgs_ref[...] += _colsum(dgs)
        lane = lax.broadcasted_iota(jnp.int32, (1, LANES), 1)
        lo = lane < HEAD_DIM
        first = (lane % HEAD_DIM) < STAT_LANES
        prod = da * av
        for c in range(ATTN_W // LANES):
            cols = slice(c * LANES, (c + 1) * LANES)
            pc = prod[:, cols]
            delta = jnp.where(lo, jnp.sum(jnp.where(lo, pc, 0.0), axis=-1, keepdims=True),
                              jnp.sum(jnp.where(lo, 0.0, pc), axis=-1, keepdims=True))
            st_ref[:, cols] = jnp.where(first, l_ref[:, cols], delta)

    tok = lambda w: pl.BlockSpec((tm, w), _row)
    vec = lambda w: pl.BlockSpec((1, w), _fixed)
    return pl.pallas_call(
        body, grid=(T // tm,),
        in_specs=[tok(D_MODEL), pl.BlockSpec((D_MODEL, D_MODEL), _fixed), tok(ATTN_W), tok(ATTN_W), tok(SGU_W),
                  vec(ATTN_W), vec(SGU_W), tok(D_FF), tok(D_MODEL)],
        out_specs=[tok(ATTN_W), tok(ATTN_W), tok(SGU_W), vec(ATTN_W), vec(SGU_W), _ff_grad_spec(0)],
        out_shape=[SDS((T, ATTN_W), F32), SDS((T, ATTN_W), F32), SDS((T, SGU_W), F32), SDS((1, ATTN_W), F32),
                   SDS((1, SGU_W), F32), SDS((2 * D_FF, D_MODEL), F32)],
        name="out_bwd", compiler_params=_params(("arbitrary",), 56))(dmx, w_out, attn, lse, sgu, g_a, g_s, dg, f)


def _sgu_bwd(proj, dsgu, ln_g, ln_b, w_s, b_st, groups, dmx, d_up, f, g_gu_t, tm=512):
    T = proj.shape[0]

    def body(u_ref, z_ref, ds_ref, g_ref, b_ref, w_ref, bs_ref, grp_ref, dmx_ref, dup_ref, f_ref, _,
             duz_ref, dw_ref, dbs_ref, dlg_ref, dlb_ref, gout_ref, ggu_ref, dbacc_ref):
        du_ref, dz_ref = duz_ref.at[0], duz_ref.at[1]
        step = pl.program_id(0)
        _acc_init(step, dw_ref, dbs_ref, dlg_ref, dlb_ref, gout_ref, ggu_ref, dbacc_ref)
        gout_ref[...] += _dot_tn(grp_ref[...], dmx_ref[...])
        ggu_ref[...] += _dot_tn(dup_ref[...], f_ref[...])
        lng, lnb = g_ref[...], b_ref[...]
        for g in range(N_GROUPS):
            wm = _causal(w_ref[g]).astype(BF)
            cols = slice(g * GROUP_DIM, (g + 1) * GROUP_DIM)
            for c in range(tm // CHUNK):
                rows = slice(c * CHUNK, (c + 1) * CHUNK)
                zv, uv, dout = z_ref[rows, cols], u_ref[rows, cols], ds_ref[rows, cols]
                zn, xhat, rs, tz = _sgu_norm(zv, lng, lnb)
                znb = zn.astype(BF)
                mixed = _dot(wm, znb) + bs_ref[:, g:g + 1]
                gu, tu = _gelu(uv)
                du_ref[rows, cols] = (dout * mixed * _gelu_grad(uv, tu)).astype(BF)
                dmix = dout * gu
                dmb = dmix.astype(BF)
                dw_ref[g] += _causal(_dot_nt(dmb, znb))
                dbacc_ref[g] += dmix
                dzn = _dot_tn(wm, dmb)
                dlg_ref[...] += _colsum(dzn * xhat)
                dlb_ref[...] += _colsum(dzn)
                dxh = dzn * lng
                dgz = rs * (dxh - jnp.mean(dxh, axis=-1, keepdims=True)
                            - xhat * jnp.mean(dxh * xhat, axis=-1, keepdims=True))
                dz_ref[rows, cols] = (dgz * _gelu_grad(zv, tz)).astype(BF)

        @pl.when(step == pl.num_programs(0) - 1)
        def _():
            lane = lax.broadcasted_iota(jnp.int32, (CHUNK, LANES), 1)
            acc = jnp.zeros((CHUNK, LANES), F32)
            for g in range(N_GROUPS):
                acc = jnp.where(lane == g, jnp.sum(dbacc_ref[g], axis=-1, keepdims=True), acc)
            dbs_ref[...] = acc

    tok = pl.BlockSpec((tm, SGU_W), _row)
    vec = pl.BlockSpec((1, GROUP_DIM), _fixed)
    wsp = pl.BlockSpec((N_GROUPS, CHUNK, CHUNK), lambda i: (0, 0, 0))
    sq = pl.BlockSpec((CHUNK, LANES), _fixed)
    wide = pl.BlockSpec((tm, D_MODEL), _row)
    return pl.pallas_call(
        body, grid=(T // tm,),
        in_specs=[pl.BlockSpec((tm, SGU_W), lambda i: (i, 3)), pl.BlockSpec((tm, SGU_W), lambda i: (i, 4)), tok,
                  vec, vec, wsp, sq, wide, wide, pl.BlockSpec((tm, D_FF), _row), wide, HBM],
        out_specs=[pl.BlockSpec((2, tm, SGU_W), lambda i: (0, i, 0)), wsp, sq, vec, vec,
                   pl.BlockSpec((D_MODEL, D_MODEL), _fixed), _ff_grad_spec(1)],
        out_shape=[SDS((2, T, SGU_W), BF), SDS((N_GROUPS, CHUNK, CHUNK), F32),
                   SDS((CHUNK, LANES), F32), SDS((1, GROUP_DIM), F32), SDS((1, GROUP_DIM), F32),
                   SDS((D_MODEL, D_MODEL), F32), SDS((2 * D_FF, D_MODEL), F32)],
        input_output_aliases={11: 6},
        scratch_shapes=[pltpu.VMEM((N_GROUPS, CHUNK, LANES), F32)],
        name="sgu_bwd", compiler_params=_params(("arbitrary",), 56))(
            proj, proj, dsgu, ln_g, ln_b, w_s, b_st, groups, dmx, d_up, f, g_gu_t)


def _attn_bwd(proj, do, stats, slopes, to_send, slabbed):
    T = proj.shape[0]
    nblk = T // QBLK
    n_s = len(to_send)

    def body(q_ref, k_ref, v_ref, do_ref, st_ref, sl_ref, *rest):
        srcs, d_ref, dsts = rest[:n_s], rest[n_s], rest[n_s + 1:2 * n_s + 1]
        sems, bias_ref = rest[2 * n_s + 1:2 * n_s + 4], rest[2 * n_s + 4]
        dq_ref, dk_ref, dv_ref = d_ref.at[0], d_ref.at[1], d_ref.at[2]
        h = pl.program_id(0)
        items = list(zip(srcs, dsts, slabbed))

        @pl.when(h == 0)
        def _():
            _Scatter(items, sems).start()

        _attn_bias(sl_ref, bias_ref)
        lo = lax.broadcasted_iota(jnp.int32, (1, LANES), 1) < HEAD_DIM
        scale = HEAD_DIM ** -0.5
        d_ref[...] = jnp.zeros_like(d_ref)

        for di, d in enumerate(DILATIONS):
            group, segs = _attn_plan(nblk, d)

            def step(i, carry, segs=segs, **kw):
                for s in range(segs):
                    segment(i * segs + s, **kw)
                return carry

            def segment(i, d=d, di=di, group=group):
                start, pstart, first = _attn_group_index(i, nblk, d, group)
                rows, prows = _attn_rows(start, d, group), _attn_rows(pstart, d)
                q = q_ref[rows, :] * scale
                k = jnp.concatenate([k_ref[prows, :], k_ref[rows, :]], axis=0).astype(BF)
                v = jnp.concatenate([v_ref[prows, :], v_ref[rows, :]], axis=0).astype(BF)
                dov = do_ref[rows, :]
                stats = st_ref[rows, :]
                masks = [lo, ~lo]
                qm = [jnp.where(masks[j], q, 0.0).astype(BF) for j in range(2)]
                dom = [jnp.where(masks[j], dov, 0.0).astype(BF) for j in range(2)]
                for b in range(group):
                    qb = slice(b * QBLK, (b + 1) * QBLK)
                    kb = slice(b * QBLK, (b + 2) * QBLK)
                    which = di * 2 + first.astype(jnp.int32) if b == 0 else di * 2
                    dq_parts, prs, dss = [], [], []
                    for j in range(2):
                        bias = bias_ref[which, j * QBLK:(j + 1) * QBLK, :]
                        lj = stats[qb, j * HEAD_DIM:j * HEAD_DIM + 1]
                        delta = stats[qb, j * HEAD_DIM + STAT_LANES:j * HEAD_DIM + STAT_LANES + 1]
                        pr = jnp.exp(_dot_nt(qm[j][qb], k[kb]) + bias - lj)
                        ds = (pr * (_dot_nt(dom[j][qb], v[kb]) - delta)).astype(BF)
                        dq_parts.append(_dot(ds, k[kb]))
                        prs.append(pr.astype(BF))
                        dss.append(ds)
                    dk_b = _dot_tn(jnp.concatenate(dss, axis=0), jnp.concatenate([qm[0][qb], qm[1][qb]], axis=0))
                    dv_b = _dot_tn(jnp.concatenate(prs, axis=0), jnp.concatenate([dom[0][qb], dom[1][qb]], axis=0))
                    own = _attn_rows(start + b * (d * QBLK), d)
                    dq_ref[own, :] += jnp.where(lo, dq_parts[0], dq_parts[1]) * scale
                    if b == 0:
                        dk_ref[prows, :] += dk_b[:QBLK]
                        dv_ref[prows, :] += dv_b[:QBLK]
                        dk_ref[own, :] += dk_b[QBLK:]
                        dv_ref[own, :] += dv_b[QBLK:]
                    else:
                        two = _attn_rows(start + (b - 1) * (d * QBLK), d, 2)
                        dk_ref[two, :] += dk_b
                        dv_ref[two, :] += dv_b

            lax.fori_loop(0, nblk // (group * segs), step, 0)

        @pl.when(h == pl.num_programs(0) - 1)
        def _():
            _Scatter(items, sems).wait()

    col = lambda base: pl.BlockSpec((T, LANES), lambda h: (0, base + h))
    outs = pl.pallas_call(
        body, grid=(4,),
        in_specs=[col(0), col(4), col(8), col(0), col(0), pl.BlockSpec((1, 8, LANES), lambda h: (h, 0, 0))]
        + [HBM] * n_s,
        out_specs=[pl.BlockSpec((3, T, LANES), lambda h: (0, 0, h), pipeline_mode=pl.Buffered(1))] + [HBM] * n_s,
        out_shape=[SDS((3, T, ATTN_W), F32)]
        + [SDS(s.shape if sl else (N_DEV,) + s.shape, s.dtype) for s, sl in zip(to_send, slabbed)],
        scratch_shapes=_scatter_sems(n_s) + [pltpu.VMEM((6, 2 * QBLK, 2 * QBLK), F32)],
        name="attn_bwd", compiler_params=_params(("arbitrary",), 60))(proj, proj, proj, do, stats, slopes, *to_send)
    return outs[0], outs[1:]


def _in_bwd(dparts, w_in_t, x, dh1, g1, to_send, slabbed, tm=512):
    T = x.shape[0]
    n = len(dparts)
    n_s = len(to_send)
    w = ATTN_W

    def body(*refs):
        d_refs, (w_ref, x_ref, dh1_ref, g_ref), rest = refs[:n], refs[n:n + 4], refs[n + 4:]
        srcs, (dx_ref, dg_ref), dsts, sems = rest[:n_s], rest[n_s:n_s + 2], rest[n_s + 2:2 * n_s + 2], rest[2 * n_s + 2:]
        step = pl.program_id(0)
        items = list(zip(srcs, dsts, slabbed))
        _acc_init(step, dg_ref)

        @pl.when(step == 0)
        def _():
            _Scatter(items, sems).start()

        da = None
        col = 0
        for r in d_refs:
            for part in range(r.shape[0]):
                t = _dot(r[part].astype(BF), w_ref[col * w:(col + 1) * w, :])
                da = t if da is None else da + t
                col += 1
        dx, dg = _rms_bwd(da, x_ref[...], g_ref[...])
        dx_ref[...] = dh1_ref[...] + dx
        dg_ref[...] += _colsum(dg)

        @pl.when(step == pl.num_programs(0) - 1)
        def _():
            _Scatter(items, sems).wait()

    tok = lambda c: pl.BlockSpec((tm, c), _row)
    vec = pl.BlockSpec((1, D_MODEL), _fixed)
    outs = pl.pallas_call(
        body, grid=(T // tm,),
        in_specs=[pl.BlockSpec((d.shape[0], tm, w), lambda i: (0, i, 0)) for d in dparts]
        + [pl.BlockSpec((PROJ, D_MODEL), _fixed), tok(D_MODEL), tok(D_MODEL), vec] + [HBM] * n_s,
        out_specs=[tok(D_MODEL), vec] + [HBM] * n_s,
        out_shape=[SDS((T, D_MODEL), F32), SDS((1, D_MODEL), F32)]
        + [SDS(s.shape if sl else (N_DEV,) + s.shape, s.dtype) for s, sl in zip(to_send, slabbed)],
        scratch_shapes=_scatter_sems(n_s),
        name="in_bwd", compiler_params=_params(("arbitrary",), 52))(*dparts, w_in_t, x, dh1, g1, *to_send)
    return outs[0], outs[1], outs[2:]


def _sum_parts(p_ref):
    g = p_ref[0].astype(F32)
    for s in range(1, N_DEV):
        g = g + p_ref[s].astype(F32)
    return g


def _adamw_math(g, w, m, v):
    nm = ADAM_B1 * m + (1.0 - ADAM_B1) * g
    nv = ADAM_B2 * v + (1.0 - ADAM_B2) * (g * g)
    m_hat = nm / (1.0 - ADAM_B1 ** ADAM_STEP)
    v_hat = nv / (1.0 - ADAM_B2 ** ADAM_STEP)
    return -ADAM_LR * (m_hat / (jnp.sqrt(v_hat) + ADAM_EPS) + ADAM_WD * w), nm, nv


def _row_tile(rows):
    for t in (256, 176, 128, 80):
        if rows % t == 0:
            return t
    raise ValueError(rows)


def _reduce_adamw(parts, w, m, v, name):
    rows, width = w.shape
    tr = _row_tile(rows)

    def body(p_ref, w_ref, m_ref, v_ref, g_ref, d_ref, nm_ref, nv_ref):
        g = _sum_parts(p_ref)
        g_ref[...] = g
        d_ref[...], nm_ref[...], nv_ref[...] = _adamw_math(g, w_ref[...], m_ref[...], v_ref[...])

    blk = pl.BlockSpec((tr, width), _row)
    return pl.pallas_call(
        body, grid=(rows // tr,),
        in_specs=[pl.BlockSpec((N_DEV, tr, width), lambda i: (0, i, 0)), blk, blk, blk],
        out_specs=[blk] * 4, out_shape=[SDS((rows, width), F32)] * 4,
        name="adamw_" + name, compiler_params=_params(("arbitrary",), 32))(parts, w, m, v)


def _reduce(parts, name):
    _, rows, width = parts.shape
    tr = _row_tile(rows)

    def body(p_ref, g_ref):
        g_ref[...] = _sum_parts(p_ref)

    return pl.pallas_call(
        body, grid=(rows // tr,),
        in_specs=[pl.BlockSpec((N_DEV, tr, width), lambda i: (0, i, 0))],
        out_specs=pl.BlockSpec((tr, width), _row), out_shape=SDS((rows, width), F32),
        name="sum_" + name, compiler_params=_params(("arbitrary",), 32))(parts)


def _adamw(g, w, m, v, name):
    rows, width = w.shape
    tr = _row_tile(rows)

    def body(g_ref, w_ref, m_ref, v_ref, d_ref, nm_ref, nv_ref):
        d_ref[...], nm_ref[...], nv_ref[...] = _adamw_math(g_ref[...], w_ref[...], m_ref[...], v_ref[...])

    blk = pl.BlockSpec((tr, width), _row)
    return pl.pallas_call(
        body, grid=(rows // tr,), in_specs=[blk] * 4, out_specs=[blk] * 3, out_shape=[SDS((rows, width), F32)] * 3,
        name="adamw_" + name, compiler_params=_params(("arbitrary",), 32))(g, w, m, v)


SMALL = ("w_spatial", "ln_pre_mix", "ln_post_mix", "ln_pre_ffn", "ln_post_ffn", "b_pe_gate",
         "attn_out_norm", "sgu_out_norm", "b_spatial", "sgu_ln_g", "sgu_ln_b")
SMALL_GROUPS = ((128, ("w_spatial", "b_spatial", "sgu_ln_g", "sgu_ln_b")),
                (512, ("attn_out_norm", "sgu_out_norm")),
                (1024, ("ln_post_mix", "ln_pre_ffn", "ln_post_ffn", "b_pe_gate")))
SMALL_LATE = "ln_pre_mix"
SMALL_SIZE = dict(w_spatial=N_GROUPS * CHUNK * CHUNK, b_spatial=N_GROUPS * CHUNK, sgu_ln_g=GROUP_DIM, sgu_ln_b=GROUP_DIM,
                  attn_out_norm=ATTN_W, sgu_out_norm=SGU_W, ln_pre_mix=D_MODEL, ln_post_mix=D_MODEL, ln_pre_ffn=D_MODEL,
                  ln_post_ffn=D_MODEL, b_pe_gate=D_MODEL)
SUBLANES = 8
ROW_SHARDED = ("w_out", "w_down", "w_pe_gate")
COL_SHARDED = ("w_in", "w_gate_up", "w_pe_proj")
WEIGHTS = ("ln_pre_mix", "w_in", "sgu_ln_g", "sgu_ln_b", "w_spatial", "b_spatial", "attn_out_norm", "sgu_out_norm",
           "w_out", "ln_post_mix", "ln_pre_ffn", "w_gate_up", "w_down", "ln_post_ffn", "w_pe_gate", "b_pe_gate",
           "w_pe_proj")


def _group_rows(width, names, extra=0):
    rows = sum(SMALL_SIZE[n] // width for n in names) + extra
    return -(-rows // SUBLANES) * SUBLANES


def _pack_small_grads(gs, loss_term):
    packed = []
    for width, names in SMALL_GROUPS:
        rows = [gs[n].reshape(-1, width) for n in names]
        extra = int(width == D_MODEL)
        if extra:
            rows.append(jnp.full((1, width), loss_term, F32))
        used = sum(r.shape[0] for r in rows)
        rows.append(jnp.zeros((_group_rows(width, names, extra) - used, width), F32))
        packed.append(jnp.concatenate(rows, axis=0))
    return packed


def _small_adamw(arrived, arrived_late, w, m, v):
    names = [n for _, ns in SMALL_GROUPS for n in ns] + [SMALL_LATE]
    n_groups = len(SMALL_GROUPS)

    def body(*refs):
        group_refs, late_ref = refs[:n_groups], refs[n_groups]
        state = refs[n_groups + 1:n_groups + 1 + 3 * len(names)]
        outs = refs[n_groups + 1 + 3 * len(names):]
        sums = [_sum_parts(r) for r in group_refs]

        def update(name, g):
            i = names.index(name)
            w_ref, m_ref, v_ref = state[3 * i:3 * i + 3]
            delta, nm, nv = _adamw_math(g, w_ref[...].reshape(g.shape), m_ref[...].reshape(g.shape),
                                        v_ref[...].reshape(g.shape))
            for o_ref, val in zip(outs[4 * i:4 * i + 4], (g, delta, nm, nv)):
                o_ref[...] = val.reshape(o_ref.shape)

        for (width, group), total in zip(SMALL_GROUPS, sums):
            row = 0
            for name in group:
                rows = SMALL_SIZE[name] // width
                update(name, total[row:row + rows, :])
                row += rows
            if width == D_MODEL:
                outs[-1][...] = total[row:row + 1, :LANES]
        update(SMALL_LATE, _sum_parts(late_ref)[:1, :])

    state = [t[n] for n in names for t in (w, m, v)]
    plain = jax.ShapeDtypeStruct
    out_shape = [plain(w[n].shape, F32) for n in names for _ in range(4)] + [plain((1, LANES), F32)]
    outs = pl.pallas_call(body, out_shape=out_shape, name="adamw_small",
                          compiler_params=pltpu.CompilerParams(vmem_limit_bytes=32 * MIB))(*arrived, arrived_late, *state)
    return {n: outs[4 * i:4 * i + 4] for i, n in enumerate(names)}, outs[-1]


def _slabs(full):
    return full.reshape(N_DEV, full.shape[0] // N_DEV, full.shape[1])


def kernel(x, p, ln_pre_mix, w_in, sgu_ln_g, sgu_ln_b, w_spatial, b_spatial, attn_out_norm, sgu_out_norm, w_out, ln_post_mix, ln_pre_ffn, w_gate_up, w_down, ln_post_ffn, w_pe_gate, b_pe_gate, w_pe_proj, loss_target, m_ln_pre_mix, m_w_in, m_sgu_ln_g, m_sgu_ln_b, m_w_spatial, m_b_spatial, m_attn_out_norm, m_sgu_out_norm, m_w_out, m_ln_post_mix, m_ln_pre_ffn, m_w_gate_up, m_w_down, m_ln_post_ffn, m_w_pe_gate, m_b_pe_gate, m_w_pe_proj, v_ln_pre_mix, v_w_in, v_sgu_ln_g, v_sgu_ln_b, v_w_spatial, v_b_spatial, v_attn_out_norm, v_sgu_out_norm, v_w_out, v_ln_post_mix, v_ln_pre_ffn, v_w_gate_up, v_w_down, v_ln_post_ffn, v_w_pe_gate, v_b_pe_gate, v_w_pe_proj):
    given = dict(locals())
    w = {n: given[n] for n in WEIGHTS}
    m = {n: given["m_" + n] for n in WEIGHTS}
    v = {n: given["v_" + n] for n in WEIGHTS}
    xs, ps, target = x[0], p[0, 0], loss_target[0]

    shard = {n: w[n][0].astype(BF) for n in ROW_SHARDED}
    shard.update({n: w[n][0].T.astype(BF) for n in COL_SHARDED})
    sm = {n: w[n][0] for n in SMALL}
    sm = {n: (a.reshape(1, -1) if a.ndim == 1 else a) for n, a in sm.items()}
    slopes = jnp.broadcast_to((2.0 ** -(jnp.arange(8, dtype=F32) + 1.0)).reshape(4, 2, 1), (4, 2, LANES))
    slopes = jnp.concatenate([slopes, jnp.zeros((4, 6, LANES), F32)], axis=1)
    b_st = jnp.pad(sm["b_spatial"].T, ((0, 0), (0, LANES - N_GROUPS)))

    def full(gathered):
        return gathered.reshape(-1, gathered.shape[-1])

    w_in_t = full(_all_gather(shard["w_in"], "gather_w_in"))
    proj, a = _in_proj(xs, sm["ln_pre_mix"], w_in_t)
    later = ("w_out", "w_gate_up", "w_down", "w_pe_gate", "w_pe_proj")
    attn, lse, gathered = _attn_fwd(proj, slopes, [shard[n] for n in later])
    w_out_f, w_gu_t, w_down_f, w_peg_f, w_pep_t = [full(g) for g in gathered]
    sgu = _sgu_fwd(proj, sm["sgu_ln_g"], sm["sgu_ln_b"], sm["w_spatial"], b_st)
    groups, mixed, h1, f = _out_proj(attn, sgu, xs, sm["attn_out_norm"], sm["sgu_out_norm"], w_out_f,
                                     sm["ln_post_mix"], sm["ln_pre_ffn"])
    g, u, act = _gate_up(f, w_gu_t)
    y, h2 = _down_proj(act, w_down_f, h1, sm["ln_post_ffn"])
    dh2, dy, g_peg, g_pep_t, loss_cols, db_peg, d_pff = _pe_loss_and_bwd(
        h2, ps, target, y, w_peg_f, sm["b_pe_gate"], w_pep_t, sm["ln_post_ffn"])
    loss_term = 0.5 * jnp.sum(loss_cols) * (1.0 / D_MODEL)

    arrived = {}
    g_down = _weight_grad(act, dy, "w_down")
    dg, du, (arrived["w_pe_proj"], arrived["w_pe_gate"]) = _down_bwd(dy, w_down_f, g, u, [_slabs(g_pep_t), _slabs(g_peg)])
    dh1, dmx, d_pf, d_pm, (arrived["w_down"],) = _ffn_in_bwd(dg, du, w_gu_t, h1, dh2, mixed, sm["ln_pre_ffn"],
                                                            sm["ln_post_mix"], [_slabs(g_down)])
    dattn, stats, dsgu, d_ga, d_gs, g_gu_t = _out_bwd(dmx, w_out_f, attn, lse, sgu, sm["attn_out_norm"],
                                                      sm["sgu_out_norm"], dg, f)
    duz, d_ws, d_bst, d_lg, d_lb, g_out, g_gu_t = _sgu_bwd(proj, dsgu, sm["sgu_ln_g"], sm["sgu_ln_b"],
                                                           sm["w_spatial"], b_st, groups, dmx, du, f, g_gu_t)
    gs = dict(sgu_ln_g=d_lg, sgu_ln_b=d_lb, w_spatial=d_ws, b_spatial=d_bst[:, :N_GROUPS].T, attn_out_norm=d_ga,
              sgu_out_norm=d_gs, ln_post_mix=d_pm, ln_pre_ffn=d_pf, ln_post_ffn=d_pff, b_pe_gate=db_peg)
    small_grads = _pack_small_grads(gs, loss_term)
    dqkv, (arrived["w_gate_up"], arrived["w_out"], *arrived_small) = _attn_bwd(
        proj, dattn, stats, slopes, [_slabs(g_gu_t), _slabs(g_out), *small_grads],
        [True, True] + [False] * len(small_grads))
    g_in_t = _grad_w_in([dqkv, duz], a)
    grad_x, d_g1, (arrived["w_in"],) = _in_bwd(
        [dqkv, duz], w_in_t, xs, dh1, sm["ln_pre_mix"], [_slabs(g_in_t)], [True])
    (arrived_late,) = _scatter_call([jnp.pad(d_g1, ((0, SUBLANES - 1), (0, 0)))], [False], "ln_pre_mix_grad_exchange")

    res = {}
    for n in ROW_SHARDED:
        res[n] = _reduce_adamw(arrived[n], w[n][0], m[n][0], v[n][0], n)
    for n in ("w_in", "w_gate_up"):
        res[n] = [t.T for t in _reduce_adamw(arrived[n], w[n][0].T, m[n][0].T, v[n][0].T, n)]
    for n in ("w_pe_proj",):
        grad = _reduce(arrived[n], n).T
        res[n] = (grad, *_adamw(grad, w[n][0], m[n][0], v[n][0], n))
    small, loss_row = _small_adamw(arrived_small, arrived_late, w, m, v)

    out = []
    for k in range(4):
        out += [res[n][k][None] if n in res else small[n][k] for n in WEIGHTS]
    return (loss_row[0, 0], grad_x[None], *out)
```

```python
import math

import jax
import jax.numpy as jnp
from jax import lax
from jax.experimental import pallas as pl
from jax.experimental.pallas import tpu as pltpu

F32 = jnp.float32
BF = jnp.bfloat16


def SDS(shape, dtype):
    return pltpu.HBM(tuple(shape), dtype)

D_MODEL = 1024
ATTN_W = 512
SGU_W = 512
HEAD_DIM = 64
N_GROUPS = 4
GROUP_DIM = 128
CHUNK = 128
D_FF = 2816
PLE_DIM = 256
PROJ = 3 * ATTN_W + 2 * SGU_W
DILATIONS = (1, 4, 16)
QBLK = 128
EPS = 1e-6
NEG = -1e30
N_DEV = 8
LANES = 128

ADAM_LR = 0.001
ADAM_B1 = 0.9
ADAM_B2 = 0.999
ADAM_EPS = 1e-08
ADAM_WD = 0.01
ADAM_STEP = 10

MIB = 2 ** 20
MESH_ID = pl.DeviceIdType.MESH
HBM = pl.BlockSpec(memory_space=pl.ANY)


def _params(sem, vmem_mib):
    return pltpu.CompilerParams(dimension_semantics=sem, vmem_limit_bytes=vmem_mib * MIB)


def _dot(a, b):
    return jnp.dot(a, b, preferred_element_type=F32)


def _dot_nt(a, b):
    return lax.dot_general(a, b, (((1,), (1,)), ((), ())), preferred_element_type=F32)


def _dot_tn(a, b):
    return lax.dot_general(a, b, (((0,), (0,)), ((), ())), preferred_element_type=F32)


def _rstd(x):
    return lax.rsqrt(jnp.mean(x * x, axis=-1, keepdims=True) + EPS)


def _rms_bwd(dy, x, g):
    r = _rstd(x)
    n = x * r
    dn = dy * g
    dx = r * (dn - n * jnp.mean(dn * n, axis=-1, keepdims=True))
    return dx, dy * n


def _colsum(v):
    return jnp.sum(v, axis=0, keepdims=True)


_G0 = math.sqrt(2.0 / math.pi)
_G1 = 0.044715


def _gelu(x):
    t = jnp.tanh(_G0 * (x + _G1 * x * x * x))
    return 0.5 * x * (1.0 + t), t


def _gelu_grad(x, t):
    return 0.5 * (1.0 + t) + 0.5 * x * (1.0 - t * t) * (_G0 * (1.0 + 3.0 * _G1 * x * x))


def _sigmoid(x):
    return 0.5 * jnp.tanh(0.5 * x) + 0.5


def _row(i):
    return (i, 0)


def _fixed(i):
    return (0, 0)


def _acc_init(step, *refs):
    @pl.when(step == 0)
    def _():
        for r in refs:
            r[...] = jnp.zeros_like(r)


FLIPS = [(dx, dy, dc) for dx in (0, 1) for dy in (0, 1) for dc in (0, 1)][1:]
DMA_SEMS = pltpu.SemaphoreType.DMA


def _mesh_pos():
    return lax.axis_index("x"), lax.axis_index("y"), lax.axis_index("c")


def _remote(src, dst, sems, n, to):
    return pltpu.make_async_remote_copy(src_ref=src, dst_ref=dst, send_sem=sems[0].at[n], recv_sem=sems[1].at[n],
                                        device_id=to, device_id_type=MESH_ID)


class _Scatter:
    def __init__(self, items, sems):
        x, y, c = _mesh_pos()
        me = 4 * x + 2 * y + c
        self.local, self.sends, self.arrivals = [], [], []
        for i, (src, dst, slabbed) in enumerate(items):
            self.local.append(pltpu.make_async_copy(src.at[me] if slabbed else src, dst.at[me], sems[2].at[i]))
            for k, (dx, dy, dc) in enumerate(FLIPS):
                to = (1 - x if dx else x, 1 - y if dy else y, 1 - c if dc else c)
                peer = 4 * to[0] + 2 * to[1] + to[2]
                out = src.at[peer] if slabbed else src
                self.sends.append(_remote(out, dst.at[me], sems, 7 * i + k, to))
                self.arrivals.append(_remote(out, dst.at[peer], sems, 7 * i + k, to))

    def start(self):
        for cp in self.local + self.sends:
            cp.start()

    def wait(self):
        for cp in self.arrivals:
            cp.wait_recv()
        for cp in self.sends:
            cp.wait_send()
        for cp in self.local:
            cp.wait()


def _scatter_sems(n):
    return [DMA_SEMS((7 * n,)), DMA_SEMS((7 * n,)), DMA_SEMS((n,))]


class _Gather:
    def __init__(self, items, sems):
        x, y, c = _mesh_pos()
        me, sibling = (x, y, c), (x, y, 1 - c)
        chips = [(1 - x, y), (x, 1 - y), (1 - x, 1 - y)]
        self.first, self.passed, self.from_chips, self.rest, self.local = [], [], [], [], []
        for i, (src, dst) in enumerate(items):
            def slot(p, dst=dst):
                return dst.at[4 * p[0] + 2 * p[1] + p[2]]

            def copy(k, block, to, own=False, i=i, src=src, slot=slot):
                return _remote(src if own else slot(block), slot(block), sems, 7 * i + k, to)

            self.local.append(pltpu.make_async_copy(src, slot(me), sems[2].at[i]))
            self.first.append(copy(0, me, sibling, own=True))
            self.first += [copy(1 + j, me, (*chip, c), own=True) for j, chip in enumerate(chips)]
            self.passed += [copy(4 + j, (*chip, c), sibling) for j, chip in enumerate(chips)]
            self.from_chips += [copy(1 + j, (*chip, c), me) for j, chip in enumerate(chips)]
            self.rest.append(copy(0, sibling, me))
            self.rest += [copy(4 + j, (*chip, 1 - c), me) for j, chip in enumerate(chips)]

    def start(self):
        for cp in self.local + self.first:
            cp.start()

    def forward(self):
        for arrived, onward in zip(self.from_chips, self.passed):
            arrived.wait_recv()
            onward.start()

    def finish(self):
        for cp in self.rest:
            cp.wait_recv()
        for cp in self.first + self.passed:
            cp.wait_send()
        for cp in self.local:
            cp.wait()


def _all_gather(shard, name):
    def body(x_ref, out_ref, *sems):
        g = _Gather([(x_ref, out_ref)], sems)
        g.start()
        g.forward()
        g.finish()

    return pl.pallas_call(
        body, out_shape=SDS((N_DEV,) + shard.shape, shard.dtype), in_specs=[HBM], out_specs=HBM,
        scratch_shapes=_scatter_sems(1), name=name)(shard)


def _scatter_call(srcs, slabbed, name):
    n = len(srcs)

    def body(*refs):
        sc = _Scatter(list(zip(refs[:n], refs[n:2 * n], slabbed)), refs[2 * n:])
        sc.start()
        sc.wait()

    shapes = [SDS(s.shape if sl else (N_DEV,) + s.shape, s.dtype) for s, sl in zip(srcs, slabbed)]
    return pl.pallas_call(body, out_shape=shapes, in_specs=[HBM] * n, out_specs=[HBM] * n,
                          scratch_shapes=_scatter_sems(n), name=name)(*srcs)


SEM = pl.BlockSpec(memory_space=pltpu.SEMAPHORE)
N_PEERS = len(FLIPS)


def _slab_copies(src_ref, land_ref, send_sems, recv_sems):
    x, y, c = _mesh_pos()
    me = 4 * x + 2 * y + c
    copies = []
    for k, (dx, dy, dc) in enumerate(FLIPS):
        to = (1 - x if dx else x, 1 - y if dy else y, 1 - c if dc else c)
        peer = 4 * to[0] + 2 * to[1] + to[2]
        sems = (send_sems, recv_sems)
        copies.append((_remote(src_ref.at[peer], land_ref.at[me], sems, k, to),
                       _remote(src_ref.at[peer], land_ref.at[peer], sems, k, to)))
    return copies


def _scatter_begin(src, name):
    def body(src_ref, land_ref, send_sems, recv_sems, src_thru, land_thru, token):
        for send, _ in _slab_copies(src_ref, land_ref, send_sems, recv_sems):
            send.start()
        token[...] = jnp.zeros_like(token)

    landing = lax.empty(src.shape, src.dtype)
    return pl.pallas_call(
        body, name=name,
        out_shape=(pltpu.SemaphoreType.DMA((N_PEERS,)), pltpu.SemaphoreType.DMA((N_PEERS,)),
                   pltpu.HBM(src.shape, src.dtype), pltpu.HBM(src.shape, src.dtype),
                   jax.ShapeDtypeStruct((SUBLANES, LANES), F32)),
        in_specs=(HBM, HBM), out_specs=(SEM, SEM, HBM, HBM, pl.BlockSpec(memory_space=pltpu.VMEM)),
        input_output_aliases={0: 2, 1: 3},
        compiler_params=pltpu.CompilerParams(has_side_effects=pltpu.SideEffectType.DATAFLOW_SIDE_EFFECTING))(
            pltpu.with_memory_space_constraint(src, pltpu.HBM), pltpu.with_memory_space_constraint(landing, pltpu.HBM))


def _scatter_end(send_sems, recv_sems, src_thru, land_thru, after, name):
    def body(src_ref, land_ref, send_sems, recv_sems, after_ref, src_dead, land_out):
        for send, arrival in _slab_copies(src_ref, land_ref, send_sems, recv_sems):
            send.wait_send()
            arrival.wait_recv()

    return pl.pallas_call(
        body, name=name,
        out_shape=(pltpu.HBM(src_thru.shape, src_thru.dtype), pltpu.HBM(land_thru.shape, land_thru.dtype)),
        in_specs=(HBM, HBM, SEM, SEM, HBM), out_specs=(HBM, HBM), input_output_aliases={0: 0, 1: 1},
        compiler_params=pltpu.CompilerParams(has_side_effects=pltpu.SideEffectType.DATAFLOW_SIDE_EFFECTING))(
            src_thru, land_thru, send_sems, recv_sems, after)


def _in_proj(x, g1, w_in_t, tm=512):
    T = x.shape[0]

    def body(x_ref, g_ref, w_ref, proj_ref, a_ref):
        xv = x_ref[...]
        a = (xv * _rstd(xv) * g_ref[...]).astype(BF)
        a_ref[...] = a
        proj_ref[...] = _dot_nt(a, w_ref[...])

    return pl.pallas_call(
        body, grid=(T // tm,),
        in_specs=[pl.BlockSpec((tm, D_MODEL), _row), pl.BlockSpec((1, D_MODEL), _fixed),
                  pl.BlockSpec((PROJ, D_MODEL), _fixed)],
        out_specs=[pl.BlockSpec((tm, PROJ), _row), pl.BlockSpec((tm, D_MODEL), _row)],
        out_shape=[SDS((T, PROJ), F32), SDS((T, D_MODEL), BF)],
        name="in_proj", compiler_params=_params(("arbitrary",), 48))(x, g1, w_in_t)


ATTN_GROUP = 16


def _attn_bias(sl_ref, bias_ref):
    qi = lax.broadcasted_iota(jnp.int32, (QBLK, QBLK), 0)
    kj = lax.broadcasted_iota(jnp.int32, (QBLK, QBLK), 1)
    step = qi - kj
    for di, d in enumerate(DILATIONS):
        for j in range(2):
            sl = sl_ref[0, j:j + 1, :]
            cur = jnp.where(step >= 0, -sl * (step * d).astype(F32), NEG)
            prev = jnp.where(step <= 0, -sl * ((step + QBLK) * d).astype(F32), NEG)
            rows = slice(j * QBLK, (j + 1) * QBLK)
            bias_ref[di * 2, rows, :QBLK] = prev
            bias_ref[di * 2, rows, QBLK:] = cur
            bias_ref[di * 2 + 1, rows, :QBLK] = jnp.full((QBLK, QBLK), NEG, F32)
            bias_ref[di * 2 + 1, rows, QBLK:] = cur


def _stack_heads(x, lo):
    return jnp.concatenate([jnp.where(lo, x, 0.0), jnp.where(lo, 0.0, x)], axis=0).astype(BF)


def _unstack_heads(x, lo):
    return jnp.where(lo, x[:QBLK], x[QBLK:])


def _attn_rows(start, d, blocks=1):
    if d == 1:
        return pl.ds(pl.multiple_of(start, QBLK), blocks * QBLK)
    return pl.ds(start, blocks * QBLK, stride=d)


def _attn_group_index(i, nblk, d, group):
    per = nblk // d // group
    r = i // per
    n0 = (i % per) * group
    start = r + (d * QBLK) * n0
    pstart = jnp.maximum(start - d * QBLK, r)
    return start, pstart, n0 == 0


def _attn_plan(nblk, d):
    group = min(ATTN_GROUP, nblk // d)
    return group, max(1, min(ATTN_GROUP // group, d))


def _attn_fwd(proj, slopes, to_gather):
    T = proj.shape[0]
    nblk = T // QBLK
    n_g = len(to_gather)

    def body(q_ref, k_ref, v_ref, sl_ref, *rest):
        srcs, (o_ref, m_ref), dsts = rest[:n_g], rest[n_g:n_g + 2], rest[n_g + 2:2 * n_g + 2]
        sems, (l_ref, bias_ref) = rest[2 * n_g + 2:2 * n_g + 5], rest[2 * n_g + 5:]
        h = pl.program_id(0)

        @pl.when(h == 0)
        def _():
            _Gather(list(zip(srcs, dsts)), sems).start()

        @pl.when(h == pl.num_programs(0) - 1)
        def _():
            _Gather(list(zip(srcs, dsts)), sems).forward()

        _attn_bias(sl_ref, bias_ref)
        lo = lax.broadcasted_iota(jnp.int32, (1, LANES), 1) < HEAD_DIM

        order = list(enumerate(DILATIONS))[::-1]
        for di, d in order:
            group, segs = _attn_plan(nblk, d)
            fresh, last = di == order[0][0], di == order[-1][0]

            def step(i, carry, segs=segs, **kw):
                for s in range(segs):
                    segment(i * segs + s, **kw)
                return carry

            def segment(i, d=d, di=di, group=group, fresh=fresh, last=last):
                start, pstart, first = _attn_group_index(i, nblk, d, group)
                prows = _attn_rows(pstart, d)
                k_prev, v_prev = k_ref[prows, :].astype(BF), v_ref[prows, :].astype(BF)
                for b in range(group):
                    out = _attn_rows(start + b * (d * QBLK), d)
                    k_own, v_own = k_ref[out, :].astype(BF), v_ref[out, :].astype(BF)
                    k2, v2 = jnp.concatenate([k_prev, k_own], axis=0), jnp.concatenate([v_prev, v_own], axis=0)
                    k_prev, v_prev = k_own, v_own
                    bias = bias_ref[di * 2 + first.astype(jnp.int32)] if b == 0 else bias_ref[di * 2]
                    s = _dot_nt(_stack_heads(q_ref[out, :] * (HEAD_DIM ** -0.5), lo), k2) + bias
                    m = jnp.max(s, axis=-1, keepdims=True)
                    pr = jnp.exp(s - m)
                    m_b = _unstack_heads(m, lo)
                    l_b = _unstack_heads(jnp.sum(pr, axis=-1, keepdims=True), lo)
                    o_b = _unstack_heads(_dot(pr.astype(BF), v2), lo)
                    if fresh:
                        m_ref[out, :] = m_b
                        l_ref[out, :] = l_b
                        o_ref[out, :] = o_b
                        continue
                    m_o = m_ref[out, :]
                    m_n = jnp.maximum(m_o, m_b)
                    wa, wb = jnp.exp(m_o - m_n), jnp.exp(m_b - m_n)
                    l_n = wa * l_ref[out, :] + wb * l_b
                    o_n = wa * o_ref[out, :] + wb * o_b
                    if last:
                        m_ref[out, :] = m_n + jnp.log(l_n)
                        o_ref[out, :] = o_n / l_n
                    else:
                        m_ref[out, :] = m_n
                        l_ref[out, :] = l_n
                        o_ref[out, :] = o_n

            lax.fori_loop(0, nblk // (group * segs), step, 0)

        @pl.when(h == pl.num_programs(0) - 1)
        def _():
            _Gather(list(zip(srcs, dsts)), sems).finish()

    col = lambda base: pl.BlockSpec((T, LANES), lambda h: (0, base + h))
    tok = pl.BlockSpec((T, LANES), lambda h: (0, h))
    outs = pl.pallas_call(
        body, grid=(4,),
        in_specs=[col(0), col(4), col(8), pl.BlockSpec((1, 8, LANES), lambda h: (h, 0, 0))] + [HBM] * n_g,
        out_specs=[tok, tok] + [HBM] * n_g,
        out_shape=[SDS((T, ATTN_W), F32), SDS((T, ATTN_W), F32)]
        + [SDS((N_DEV,) + g.shape, g.dtype) for g in to_gather],
        scratch_shapes=_scatter_sems(n_g) + [pltpu.VMEM((T, LANES), F32), pltpu.VMEM((6, 2 * QBLK, 2 * QBLK), F32)],
        name="attn_fwd", compiler_params=_params(("arbitrary",), 56))(proj, proj, proj, slopes, *to_gather)
    return outs[0], outs[1], outs[2:]


def _sgu_norm(zv, ln_g, ln_b):
    gz, tz = _gelu(zv)
    mu = jnp.mean(gz, axis=-1, keepdims=True)
    xc = gz - mu
    rs = lax.rsqrt(jnp.mean(xc * xc, axis=-1, keepdims=True) + EPS)
    xhat = xc * rs
    return xhat * ln_g + ln_b, xhat, rs, tz


def _causal(w):
    i = lax.broadcasted_iota(jnp.int32, (CHUNK, CHUNK), 0)
    j = lax.broadcasted_iota(jnp.int32, (CHUNK, CHUNK), 1)
    return jnp.where(i >= j, w, 0.0)


def _sgu_fwd(proj, ln_g, ln_b, w_s, b_st, tm=512):
    T = proj.shape[0]

    def body(u_ref, z_ref, g_ref, b_ref, w_ref, bs_ref, out_ref):
        for g in range(N_GROUPS):
            wm = _causal(w_ref[g]).astype(BF)
            cols = slice(g * GROUP_DIM, (g + 1) * GROUP_DIM)
            for c in range(tm // CHUNK):
                rows = slice(c * CHUNK, (c + 1) * CHUNK)
                zn, _, _, _ = _sgu_norm(z_ref[rows, cols], g_ref[...], b_ref[...])
                mixed = _dot(wm, zn.astype(BF)) + bs_ref[:, g:g + 1]
                gu, _ = _gelu(u_ref[rows, cols])
                out_ref[rows, cols] = gu * mixed

    return pl.pallas_call(
        body, grid=(T // tm,),
        in_specs=[pl.BlockSpec((tm, SGU_W), lambda i: (i, 3)), pl.BlockSpec((tm, SGU_W), lambda i: (i, 4)),
                  pl.BlockSpec((1, GROUP_DIM), _fixed), pl.BlockSpec((1, GROUP_DIM), _fixed),
                  pl.BlockSpec((N_GROUPS, CHUNK, CHUNK), lambda i: (0, 0, 0)), pl.BlockSpec((CHUNK, LANES), _fixed)],
        out_specs=pl.BlockSpec((tm, SGU_W), _row),
        out_shape=SDS((T, SGU_W), F32),
        name="sgu_fwd", compiler_params=_params(("arbitrary",), 32))(proj, proj, ln_g, ln_b, w_s, b_st)


def _out_proj(attn, sgu, x, g_a, g_s, w_out, g_pm, g_pf, tm=512):
    T = x.shape[0]

    def body(a_ref, s_ref, x_ref, ga_ref, gs_ref, w_ref, gpm_ref, gpf_ref, grp_ref, mixed_ref, h1_ref, f_ref):
        av, sv = a_ref[...], s_ref[...]
        an = (av * _rstd(av) * ga_ref[...]).astype(BF)
        sn = (sv * _rstd(sv) * gs_ref[...]).astype(BF)
        grp_ref[:, :ATTN_W] = an
        grp_ref[:, ATTN_W:] = sn
        mixed = _dot(an, w_ref[:ATTN_W, :]) + _dot(sn, w_ref[ATTN_W:, :])
        mixed_ref[...] = mixed
        h1 = x_ref[...] + mixed * _rstd(mixed) * gpm_ref[...]
        h1_ref[...] = h1
        f_ref[...] = (h1 * _rstd(h1) * gpf_ref[...]).astype(BF)

    tok = lambda w: pl.BlockSpec((tm, w), _row)
    vec = lambda w: pl.BlockSpec((1, w), _fixed)
    return pl.pallas_call(
        body, grid=(T // tm,),
        in_specs=[tok(ATTN_W), tok(SGU_W), tok(D_MODEL), vec(ATTN_W), vec(SGU_W),
                  pl.BlockSpec((D_MODEL, D_MODEL), _fixed), vec(D_MODEL), vec(D_MODEL)],
        out_specs=[tok(D_MODEL)] * 4,
        out_shape=[SDS((T, D_MODEL), BF), SDS((T, D_MODEL), F32), SDS((T, D_MODEL), F32), SDS((T, D_MODEL), BF)],
        name="out_proj", compiler_params=_params(("arbitrary",), 48))(attn, sgu, x, g_a, g_s, w_out, g_pm, g_pf)


FF_TILE = 1408
FF_TILES = D_FF // FF_TILE
FF_CHUNK = 256


def _gate_up(f, w_gu_t, tm=512):
    T = f.shape[0]
    tn = FF_TILE

    def body(f_ref, wg_ref, wu_ref, g_ref, u_ref, act_ref):
        fv = f_ref[...]
        g = _dot_nt(fv, wg_ref[...])
        u = _dot_nt(fv, wu_ref[...])
        g_ref[...] = g.astype(BF)
        u_ref[...] = u.astype(BF)
        act_ref[...] = (g * _sigmoid(g) * u).astype(BF)

    ospec = pl.BlockSpec((tm, tn), lambda j, i: (i, j))
    return pl.pallas_call(
        body, grid=(FF_TILES, T // tm),
        in_specs=[pl.BlockSpec((tm, D_MODEL), lambda j, i: (i, 0)), pl.BlockSpec((tn, D_MODEL), lambda j, i: (j, 0)),
                  pl.BlockSpec((tn, D_MODEL), lambda j, i: (j + FF_TILES, 0))],
        out_specs=[ospec] * 3, out_shape=[SDS((T, D_FF), BF)] * 3,
        name="gate_up", compiler_params=_params(("arbitrary", "arbitrary"), 40))(f, w_gu_t, w_gu_t)


def _down_proj(act, w_down, h1, g_pff, tm=512):
    T = act.shape[0]

    def body(a_ref, w_ref, h1_ref, g_ref, y_ref, h2_ref):
        y = _dot(a_ref[...], w_ref[...])
        y_ref[...] = y
        h2_ref[...] = h1_ref[...] + y * _rstd(y) * g_ref[...]

    return pl.pallas_call(
        body, grid=(T // tm,),
        in_specs=[pl.BlockSpec((tm, D_FF), _row), pl.BlockSpec((D_FF, D_MODEL), _fixed),
                  pl.BlockSpec((tm, D_MODEL), _row), pl.BlockSpec((1, D_MODEL), _fixed)],
        out_specs=[pl.BlockSpec((tm, D_MODEL), _row)] * 2,
        out_shape=[SDS((T, D_MODEL), F32)] * 2,
        name="down_proj", compiler_params=_params(("arbitrary",), 48))(act, w_down, h1, g_pff)


def _pe_loss_and_bwd(h2, p, target, y, w_peg, b_peg, w_pep_t, g_pff, tm=512):
    T = h2.shape[0]

    def body(h2_ref, p_ref, t_ref, y_ref, wg_ref, b_ref, wp_ref, g_ref,
             dh2_ref, dy_ref, gpeg_ref, gpep_ref, loss_ref, db_ref, dg_ref):
        _acc_init(pl.program_id(0), gpeg_ref, gpep_ref, loss_ref, db_ref, dg_ref)
        h2v = h2_ref[...]
        h2b = h2v.astype(BF)
        pb = p_ref[...].astype(BF)
        gate = _sigmoid(_dot(h2b, wg_ref[...]) + b_ref[...])
        pp = _dot_nt(pb, wp_ref[...])
        diff = h2v + gate * pp - t_ref[...]
        loss_ref[...] += _colsum(diff * diff)
        dh3 = diff * (1.0 / D_MODEL)
        dpre = dh3 * pp * (gate * (1.0 - gate))
        dpre_b = dpre.astype(BF)
        db_ref[...] += _colsum(dpre)
        gpeg_ref[...] += _dot_tn(h2b, dpre_b)
        gpep_ref[...] += _dot_tn((dh3 * gate).astype(BF), pb)
        dh2 = dh3 + _dot_nt(dpre_b, wg_ref[...])
        dh2_ref[...] = dh2
        dy, dg = _rms_bwd(dh2, y_ref[...], g_ref[...])
        dy_ref[...] = dy.astype(BF)
        dg_ref[...] += _colsum(dg)

    tok = lambda w: pl.BlockSpec((tm, w), _row)
    vec = pl.BlockSpec((1, D_MODEL), _fixed)
    wg = pl.BlockSpec((D_MODEL, D_MODEL), _fixed)
    wp = pl.BlockSpec((D_MODEL, PLE_DIM), _fixed)
    return pl.pallas_call(
        body, grid=(T // tm,),
        in_specs=[tok(D_MODEL), tok(PLE_DIM), tok(D_MODEL), tok(D_MODEL), wg, vec, wp, vec],
        out_specs=[tok(D_MODEL), tok(D_MODEL), wg, wp, vec, vec, vec],
        out_shape=[SDS((T, D_MODEL), F32), SDS((T, D_MODEL), BF), SDS((D_MODEL, D_MODEL), F32),
                   SDS((D_MODEL, PLE_DIM), F32)] + [SDS((1, D_MODEL), F32)] * 3,
        name="pe_loss_and_bwd", compiler_params=_params(("arbitrary",), 56))(
            h2, p, target, y, w_peg, b_peg, w_pep_t, g_pff)


def _weight_grad(a, dy, name, into=None, row_tile=0, rows=None, tk=512):
    n = dy.shape[1]
    tn = min(n, 1024)
    T, ka = a.shape
    tka = FF_TILE if ka == D_FF else min(ka, 1024)
    rows = ka if rows is None else rows

    def body(a_ref, dy_ref, *rest):
        out_ref = rest[-1]
        _acc_init(pl.program_id(2), out_ref)
        out_ref[...] += _dot_tn(a_ref[...].astype(BF), dy_ref[...].astype(BF))

    carried = [] if into is None else [into]
    return pl.pallas_call(
        body, grid=(ka // tka, n // tn, T // tk),
        in_specs=[pl.BlockSpec((tk, tka), lambda i, j, k: (k, i)), pl.BlockSpec((tk, tn), lambda i, j, k: (k, j))]
        + [HBM] * len(carried),
        out_specs=pl.BlockSpec((tka, tn), lambda i, j, k: (i + row_tile, j)),
        out_shape=SDS((rows, n), F32), input_output_aliases={2: 0} if carried else {},
        name="grad_" + name, compiler_params=_params(("arbitrary",) * 3, 40))(a, dy, *carried)


def _grad_w_in(dparts, a, tk=512):
    T = a.shape[0]

    def body(*refs):
        d_refs, a_ref, out_ref, acc_ref = refs[:len(dparts)], refs[-3], refs[-2], refs[-1]
        k = pl.program_id(0)
        _acc_init(k, acc_ref)
        cols = [r[part].astype(BF) for r in d_refs for part in range(r.shape[0])]
        acc_ref[...] += _dot_tn(jnp.concatenate(cols, axis=1), a_ref[...])

        @pl.when(k == pl.num_programs(0) - 1)
        def _():
            out_ref[...] = acc_ref[...].astype(BF)

    return pl.pallas_call(
        body, grid=(T // tk,),
        in_specs=[pl.BlockSpec((d.shape[0], tk, d.shape[2]), lambda k: (0, k, 0)) for d in dparts]
        + [pl.BlockSpec((tk, D_MODEL), lambda k: (k, 0))],
        out_specs=pl.BlockSpec((PROJ, D_MODEL), lambda k: (0, 0)),
        out_shape=SDS((PROJ, D_MODEL), BF), scratch_shapes=[pltpu.VMEM((PROJ, D_MODEL), F32)],
        name="grad_w_in", compiler_params=_params(("arbitrary",), 48))(*dparts, a)


def _down_bwd(dy, w_down, g, u, to_send, tm=512):
    T = dy.shape[0]
    n_s = len(to_send)

    def body(dy_ref, w_ref, g_ref, u_ref, *rest):
        srcs, (dg_ref, du_ref), dsts, sems = rest[:n_s], rest[n_s:n_s + 2], rest[n_s + 2:2 * n_s + 2], rest[2 * n_s + 2:]
        i = pl.program_id(0)
        items = list(zip(srcs, dsts, [True] * n_s))

        @pl.when(i == 0)
        def _():
            _Scatter(items, sems).start()

        dyv = dy_ref[...]
        for c in range(D_FF // FF_CHUNK):
            cols = slice(c * FF_CHUNK, (c + 1) * FF_CHUNK)
            dact = _dot_nt(dyv, w_ref[cols, :]).astype(BF)
            gv, uv = g_ref[:, cols], u_ref[:, cols]
            s = _sigmoid(gv)
            ds = dact * s
            dg_ref[:, cols] = ds * uv * (1.0 + gv * (1.0 - s))
            du_ref[:, cols] = ds * gv

        @pl.when(i == pl.num_programs(0) - 1)
        def _():
            _Scatter(items, sems).wait()

    tile = pl.BlockSpec((tm, D_FF), _row)
    outs = pl.pallas_call(
        body, grid=(T // tm,),
        in_specs=[pl.BlockSpec((tm, D_MODEL), _row),
                  pl.BlockSpec((D_FF, D_MODEL), _fixed, pipeline_mode=pl.Buffered(1)), tile, tile] + [HBM] * n_s,
        out_specs=[tile, tile] + [HBM] * n_s,
        out_shape=[SDS((T, D_FF), BF)] * 2 + [SDS(s.shape, s.dtype) for s in to_send],
        scratch_shapes=_scatter_sems(n_s),
        name="down_bwd", compiler_params=_params(("arbitrary",), 48))(dy, w_down, g, u, *to_send)
    return outs[0], outs[1], outs[2:]


def _ffn_in_bwd(dg, du, w_gu_t, h1, dh2, mixed, g_pf, g_pm, to_send, tm=512):
    T = h1.shape[0]
    n_s = len(to_send)

    def body(dg_ref, du_ref, wg_ref, wu_ref, h1_ref, dh2_ref, mx_ref, gpf_ref, gpm_ref, *rest):
        srcs, outs, dsts, sems = rest[:n_s], rest[n_s:n_s + 4], rest[n_s + 4:2 * n_s + 4], rest[2 * n_s + 4:]
        dh1_ref, dmx_ref, dgpf_ref, dgpm_ref = outs
        i = pl.program_id(0)
        items = list(zip(srcs, dsts, [True] * n_s))
        _acc_init(i, dgpf_ref, dgpm_ref)

        @pl.when(i == 0)
        def _():
            _Scatter(items, sems).start()

        df = _dot(dg_ref[...], wg_ref[...]) + _dot(du_ref[...], wu_ref[...])
        dx, dgf = _rms_bwd(df, h1_ref[...], gpf_ref[...])
        dh1 = dh2_ref[...] + dx
        dh1_ref[...] = dh1
        dmx, dgm = _rms_bwd(dh1, mx_ref[...], gpm_ref[...])
        dmx_ref[...] = dmx.astype(BF)
        dgpf_ref[...] += _colsum(dgf)
        dgpm_ref[...] += _colsum(dgm)

        @pl.when(i == pl.num_programs(0) - 1)
        def _():
            _Scatter(items, sems).wait()

    tok = lambda w: pl.BlockSpec((tm, w), _row)
    vec = pl.BlockSpec((1, D_MODEL), _fixed)
    outs = pl.pallas_call(
        body, grid=(T // tm,),
        in_specs=[tok(D_FF), tok(D_FF), pl.BlockSpec((D_FF, D_MODEL), lambda i: (0, 0), pipeline_mode=pl.Buffered(1)),
                  pl.BlockSpec((D_FF, D_MODEL), lambda i: (1, 0), pipeline_mode=pl.Buffered(1)),
                  tok(D_MODEL), tok(D_MODEL), tok(D_MODEL), vec, vec]
        + [HBM] * n_s,
        out_specs=[tok(D_MODEL), tok(D_MODEL), vec, vec] + [HBM] * n_s,
        out_shape=[SDS((T, D_MODEL), F32), SDS((T, D_MODEL), BF), SDS((1, D_MODEL), F32), SDS((1, D_MODEL), F32)]
        + [SDS(s.shape, s.dtype) for s in to_send],
        scratch_shapes=_scatter_sems(n_s),
        name="ffn_in_bwd", compiler_params=_params(("arbitrary",), 56))(
            dg, du, w_gu_t, w_gu_t, h1, dh2, mixed, g_pf, g_pm, *to_send)
    return outs[0], outs[1], outs[2], outs[3], outs[4:]


STAT_LANES = HEAD_DIM // 2


def _ff_grad_spec(half):
    return pl.BlockSpec((D_FF, D_MODEL), lambda i: (half, 0), pipeline_mode=pl.Buffered(1))


def _out_bwd(dmx, w_out, attn, lse, sgu, g_a, g_s, dg, f, tm=512):
    T = attn.shape[0]

    def body(dm_ref, w_ref, a_ref, l_ref, s_ref, ga_ref, gs_ref, dgate_ref, f_ref,
             da_ref, st_ref, ds_ref, dga_ref, dgs_ref, ggu_ref):
        _acc_init(pl.program_id(0), dga_ref, dgs_ref, ggu_ref)
        ggu_ref[...] += _dot_tn(dgate_ref[...], f_ref[...])
        dgr = _dot_nt(dm_ref[...], w_ref[...])
        av = a_ref[...]
        da, dga = _rms_bwd(dgr[:, :ATTN_W], av, ga_ref[...])
        ds, dgs = _rms_bwd(dgr[:, ATTN_W:], s_ref[...], gs_ref[...])
        da_ref[...] = da
        ds_ref[...] = ds
        dga_ref[...] += _colsum(dga)
        dgs_ref[...] += _colsum(dgs)
        lane = lax.broadcasted_iota(jnp.int32, (1, LANES), 1)
        lo = lane < HEAD_DIM
        first = (lane % HEAD_DIM) < STAT_LANES
        prod = da * av
        for c in range(ATTN_W // LANES):
            cols = slice(c * LANES, (c + 1) * LANES)
            pc = prod[:, cols]
            delta = jnp.where(lo, jnp.sum(jnp.where(lo, pc, 0.0), axis=-1, keepdims=True),
                              jnp.sum(jnp.where(lo, 0.0, pc), axis=-1, keepdims=True))
            st_ref[:, cols] = jnp.where(first, l_ref[:, cols], delta)

    tok = lambda w: pl.BlockSpec((tm, w), _row)
    vec = lambda w: pl.BlockSpec((1, w), _fixed)
    return pl.pallas_call(
        body, grid=(T // tm,),
        in_specs=[tok(D_MODEL), pl.BlockSpec((D_MODEL, D_MODEL), _fixed), tok(ATTN_W), tok(ATTN_W), tok(SGU_W),
                  vec(ATTN_W), vec(SGU_W), tok(D_FF), tok(D_MODEL)],
        out_specs=[tok(ATTN_W), tok(ATTN_W), tok(SGU_W), vec(ATTN_W), vec(SGU_W), _ff_grad_spec(0)],
        out_shape=[SDS((T, ATTN_W), F32), SDS((T, ATTN_W), F32), SDS((T, SGU_W), F32), SDS((1, ATTN_W), F32),
                   SDS((1, SGU_W), F32), SDS((2 * D_FF, D_MODEL), F32)],
        name="out_bwd", compiler_params=_params(("arbitrary",), 56))(dmx, w_out, attn, lse, sgu, g_a, g_s, dg, f)


def _sgu_bwd(proj, dsgu, ln_g, ln_b, w_s, b_st, groups, dmx, d_up, f, g_gu_t, tm=512):
    T = proj.shape[0]

    def body(u_ref, z_ref, ds_ref, g_ref, b_ref, w_ref, bs_ref, grp_ref, dmx_ref, dup_ref, f_ref, _,
             duz_ref, dw_ref, dbs_ref, dlg_ref, dlb_ref, gout_ref, ggu_ref, dbacc_ref):
        du_ref, dz_ref = duz_ref.at[0], duz_ref.at[1]
        step = pl.program_id(0)
        _acc_init(step, dw_ref, dbs_ref, dlg_ref, dlb_ref, gout_ref, ggu_ref, dbacc_ref)
        gout_ref[...] += _dot_tn(grp_ref[...], dmx_ref[...])
        ggu_ref[...] += _dot_tn(dup_ref[...], f_ref[...])
        lng, lnb = g_ref[...], b_ref[...]
        for g in range(N_GROUPS):
            wm = _causal(w_ref[g]).astype(BF)
            cols = slice(g * GROUP_DIM, (g + 1) * GROUP_DIM)
            for c in range(tm // CHUNK):
                rows = slice(c * CHUNK, (c + 1) * CHUNK)
                zv, uv, dout = z_ref[rows, cols], u_ref[rows, cols], ds_ref[rows, cols]
                zn, xhat, rs, tz = _sgu_norm(zv, lng, lnb)
                znb = zn.astype(BF)
                mixed = _dot(wm, znb) + bs_ref[:, g:g + 1]
                gu, tu = _gelu(uv)
                du_ref[rows, cols] = (dout * mixed * _gelu_grad(uv, tu)).astype(BF)
                dmix = dout * gu
                dmb = dmix.astype(BF)
                dw_ref[g] += _causal(_dot_nt(dmb, znb))
                dbacc_ref[g] += dmix
                dzn = _dot_tn(wm, dmb)
                dlg_ref[...] += _colsum(dzn * xhat)
                dlb_ref[...] += _colsum(dzn)
                dxh = dzn * lng
                dgz = rs * (dxh - jnp.mean(dxh, axis=-1, keepdims=True)
                            - xhat * jnp.mean(dxh * xhat, axis=-1, keepdims=True))
                dz_ref[rows, cols] = (dgz * _gelu_grad(zv, tz)).astype(BF)

        @pl.when(step == pl.num_programs(0) - 1)
        def _():
            lane = lax.broadcasted_iota(jnp.int32, (CHUNK, LANES), 1)
            acc = jnp.zeros((CHUNK, LANES), F32)
            for g in range(N_GROUPS):
                acc = jnp.where(lane == g, jnp.sum(dbacc_ref[g], axis=-1, keepdims=True), acc)
            dbs_ref[...] = acc

    tok = pl.BlockSpec((tm, SGU_W), _row)
    vec = pl.BlockSpec((1, GROUP_DIM), _fixed)
    wsp = pl.BlockSpec((N_GROUPS, CHUNK, CHUNK), lambda i: (0, 0, 0))
    sq = pl.BlockSpec((CHUNK, LANES), _fixed)
    wide = pl.BlockSpec((tm, D_MODEL), _row)
    return pl.pallas_call(
        body, grid=(T // tm,),
        in_specs=[pl.BlockSpec((tm, SGU_W), lambda i: (i, 3)), pl.BlockSpec((tm, SGU_W), lambda i: (i, 4)), tok,
                  vec, vec, wsp, sq, wide, wide, pl.BlockSpec((tm, D_FF), _row), wide, HBM],
        out_specs=[pl.BlockSpec((2, tm, SGU_W), lambda i: (0, i, 0)), wsp, sq, vec, vec,
                   pl.BlockSpec((D_MODEL, D_MODEL), _fixed), _ff_grad_spec(1)],
        out_shape=[SDS((2, T, SGU_W), BF), SDS((N_GROUPS, CHUNK, CHUNK), F32),
                   SDS((CHUNK, LANES), F32), SDS((1, GROUP_DIM), F32), SDS((1, GROUP_DIM), F32),
                   SDS((D_MODEL, D_MODEL), F32), SDS((2 * D_FF, D_MODEL), F32)],
        input_output_aliases={11: 6},
        scratch_shapes=[pltpu.VMEM((N_GROUPS, CHUNK, LANES), F32)],
        name="sgu_bwd", compiler_params=_params(("arbitrary",), 56))(
            proj, proj, dsgu, ln_g, ln_b, w_s, b_st, groups, dmx, d_up, f, g_gu_t)


def _attn_bwd(proj, do, stats, slopes, to_send, slabbed):
    T = proj.shape[0]
    nblk = T // QBLK
    n_s = len(to_send)

    def body(q_ref, k_ref, v_ref, do_ref, st_ref, sl_ref, *rest):
        srcs, d_ref, dsts = rest[:n_s], rest[n_s], rest[n_s + 1:2 * n_s + 1]
        sems, bias_ref = rest[2 * n_s + 1:2 * n_s + 4], rest[2 * n_s + 4]
        dq_ref, dk_ref, dv_ref = d_ref.at[0], d_ref.at[1], d_ref.at[2]
        h = pl.program_id(0)
        items = list(zip(srcs, dsts, slabbed))

        @pl.when(h == 0)
        def _():
            _Scatter(items, sems).start()

        _attn_bias(sl_ref, bias_ref)
        lo = lax.broadcasted_iota(jnp.int32, (1, LANES), 1) < HEAD_DIM
        scale = HEAD_DIM ** -0.5
        d_ref[...] = jnp.zeros_like(d_ref)

        for di, d in enumerate(DILATIONS):
            group, segs = _attn_plan(nblk, d)

            def step(i, carry, segs=segs, **kw):
                for s in range(segs):
                    segment(i * segs + s, **kw)
                return carry

            def segment(i, d=d, di=di, group=group):
                start, pstart, first = _attn_group_index(i, nblk, d, group)
                rows, prows = _attn_rows(start, d, group), _attn_rows(pstart, d)
                q = q_ref[rows, :] * scale
                k = jnp.concatenate([k_ref[prows, :], k_ref[rows, :]], axis=0).astype(BF)
                v = jnp.concatenate([v_ref[prows, :], v_ref[rows, :]], axis=0).astype(BF)
                dov = do_ref[rows, :]
                stats = st_ref[rows, :]
                masks = [lo, ~lo]
                qm = [jnp.where(masks[j], q, 0.0).astype(BF) for j in range(2)]
                dom = [jnp.where(masks[j], dov, 0.0).astype(BF) for j in range(2)]
                for b in range(group):
                    qb = slice(b * QBLK, (b + 1) * QBLK)
                    kb = slice(b * QBLK, (b + 2) * QBLK)
                    which = di * 2 + first.astype(jnp.int32) if b == 0 else di * 2
                    dq_parts, prs, dss = [], [], []
                    for j in range(2):
                        bias = bias_ref[which, j * QBLK:(j + 1) * QBLK, :]
                        lj = stats[qb, j * HEAD_DIM:j * HEAD_DIM + 1]
                        delta = stats[qb, j * HEAD_DIM + STAT_LANES:j * HEAD_DIM + STAT_LANES + 1]
                        pr = jnp.exp(_dot_nt(qm[j][qb], k[kb]) + bias - lj)
                        ds = (pr * (_dot_nt(dom[j][qb], v[kb]) - delta)).astype(BF)
                        dq_parts.append(_dot(ds, k[kb]))
                        prs.append(pr.astype(BF))
                        dss.append(ds)
                    dk_b = _dot_tn(jnp.concatenate(dss, axis=0), jnp.concatenate([qm[0][qb], qm[1][qb]], axis=0))
                    dv_b = _dot_tn(jnp.concatenate(prs, axis=0), jnp.concatenate([dom[0][qb], dom[1][qb]], axis=0))
                    own = _attn_rows(start + b * (d * QBLK), d)
                    dq_ref[own, :] += jnp.where(lo, dq_parts[0], dq_parts[1]) * scale
                    if b == 0:
                        dk_ref[prows, :] += dk_b[:QBLK]
                        dv_ref[prows, :] += dv_b[:QBLK]
                        dk_ref[own, :] += dk_b[QBLK:]
                        dv_ref[own, :] += dv_b[QBLK:]
                    else:
                        two = _attn_rows(start + (b - 1) * (d * QBLK), d, 2)
                        dk_ref[two, :] += dk_b
                        dv_ref[two, :] += dv_b

            lax.fori_loop(0, nblk // (group * segs), step, 0)

        @pl.when(h == pl.num_programs(0) - 1)
        def _():
            _Scatter(items, sems).wait()

    col = lambda base: pl.BlockSpec((T, LANES), lambda h: (0, base + h))
    outs = pl.pallas_call(
        body, grid=(4,),
        in_specs=[col(0), col(4), col(8), col(0), col(0), pl.BlockSpec((1, 8, LANES), lambda h: (h, 0, 0))]
        + [HBM] * n_s,
        out_specs=[pl.BlockSpec((3, T, LANES), lambda h: (0, 0, h), pipeline_mode=pl.Buffered(1))] + [HBM] * n_s,
        out_shape=[SDS((3, T, ATTN_W), F32)]
        + [SDS(s.shape if sl else (N_DEV,) + s.shape, s.dtype) for s, sl in zip(to_send, slabbed)],
        scratch_shapes=_scatter_sems(n_s) + [pltpu.VMEM((6, 2 * QBLK, 2 * QBLK), F32)],
        name="attn_bwd", compiler_params=_params(("arbitrary",), 60))(proj, proj, proj, do, stats, slopes, *to_send)
    return outs[0], outs[1:]


def _in_bwd(dparts, w_in_t, x, dh1, g1, tm=512):
    T = x.shape[0]
    n = len(dparts)
    w = ATTN_W

    def body(*refs):
        d_refs, (w_ref, x_ref, dh1_ref, g_ref, dx_ref, dg_ref) = refs[:n], refs[n:]
        _acc_init(pl.program_id(0), dg_ref)
        da = None
        col = 0
        for r in d_refs:
            for part in range(r.shape[0]):
                t = _dot(r[part].astype(BF), w_ref[col * w:(col + 1) * w, :])
                da = t if da is None else da + t
                col += 1
        dx, dg = _rms_bwd(da, x_ref[...], g_ref[...])
        dx_ref[...] = dh1_ref[...] + dx
        dg_ref[...] += _colsum(dg)

    tok = lambda c: pl.BlockSpec((tm, c), _row)
    vec = pl.BlockSpec((1, D_MODEL), _fixed)
    return pl.pallas_call(
        body, grid=(T // tm,),
        in_specs=[pl.BlockSpec((d.shape[0], tm, w), lambda i: (0, i, 0)) for d in dparts]
        + [pl.BlockSpec((PROJ, D_MODEL), _fixed), tok(D_MODEL), tok(D_MODEL), vec],
        out_specs=[tok(D_MODEL), vec],
        out_shape=[SDS((T, D_MODEL), F32), SDS((1, D_MODEL), F32)],
        name="in_bwd", compiler_params=_params(("arbitrary",), 52))(*dparts, w_in_t, x, dh1, g1)


def _sum_parts(p_ref):
    g = p_ref[0].astype(F32)
    for s in range(1, N_DEV):
        g = g + p_ref[s].astype(F32)
    return g


def _adamw_math(g, w, m, v):
    nm = ADAM_B1 * m + (1.0 - ADAM_B1) * g
    nv = ADAM_B2 * v + (1.0 - ADAM_B2) * (g * g)
    m_hat = nm / (1.0 - ADAM_B1 ** ADAM_STEP)
    v_hat = nv / (1.0 - ADAM_B2 ** ADAM_STEP)
    return -ADAM_LR * (m_hat / (jnp.sqrt(v_hat) + ADAM_EPS) + ADAM_WD * w), nm, nv


def _row_tile(rows):
    for t in (256, 176, 128, 80):
        if rows % t == 0:
            return t
    raise ValueError(rows)


def _reduce_adamw(parts, w, m, v, name):
    rows, width = w.shape
    tr = _row_tile(rows)

    def body(p_ref, w_ref, m_ref, v_ref, g_ref, d_ref, nm_ref, nv_ref):
        g = _sum_parts(p_ref)
        g_ref[...] = g
        d_ref[...], nm_ref[...], nv_ref[...] = _adamw_math(g, w_ref[...], m_ref[...], v_ref[...])

    blk = pl.BlockSpec((tr, width), _row)
    return pl.pallas_call(
        body, grid=(rows // tr,),
        in_specs=[pl.BlockSpec((N_DEV, tr, width), lambda i: (0, i, 0)), blk, blk, blk],
        out_specs=[blk] * 4, out_shape=[SDS((rows, width), F32)] * 4,
        name="adamw_" + name, compiler_params=_params(("arbitrary",), 32))(parts, w, m, v)


def _reduce(parts, name):
    _, rows, width = parts.shape
    tr = _row_tile(rows)

    def body(p_ref, g_ref):
        g_ref[...] = _sum_parts(p_ref)

    return pl.pallas_call(
        body, grid=(rows // tr,),
        in_specs=[pl.BlockSpec((N_DEV, tr, width), lambda i: (0, i, 0))],
        out_specs=pl.BlockSpec((tr, width), _row), out_shape=SDS((rows, width), F32),
        name="sum_" + name, compiler_params=_params(("arbitrary",), 32))(parts)


def _adamw(g, w, m, v, name):
    rows, width = w.shape
    tr = _row_tile(rows)

    def body(g_ref, w_ref, m_ref, v_ref, d_ref, nm_ref, nv_ref):
        d_ref[...], nm_ref[...], nv_ref[...] = _adamw_math(g_ref[...], w_ref[...], m_ref[...], v_ref[...])

    blk = pl.BlockSpec((tr, width), _row)
    return pl.pallas_call(
        body, grid=(rows // tr,), in_specs=[blk] * 4, out_specs=[blk] * 3, out_shape=[SDS((rows, width), F32)] * 3,
        name="adamw_" + name, compiler_params=_params(("arbitrary",), 32))(g, w, m, v)


SMALL = ("w_spatial", "ln_pre_mix", "ln_post_mix", "ln_pre_ffn", "ln_post_ffn", "b_pe_gate",
         "attn_out_norm", "sgu_out_norm", "b_spatial", "sgu_ln_g", "sgu_ln_b")
SMALL_GROUPS = ((128, ("w_spatial", "b_spatial", "sgu_ln_g", "sgu_ln_b")),
                (512, ("attn_out_norm", "sgu_out_norm")),
                (1024, ("ln_post_mix", "ln_pre_ffn", "ln_post_ffn", "b_pe_gate")))
SMALL_LATE = "ln_pre_mix"
SMALL_SIZE = dict(w_spatial=N_GROUPS * CHUNK * CHUNK, b_spatial=N_GROUPS * CHUNK, sgu_ln_g=GROUP_DIM, sgu_ln_b=GROUP_DIM,
                  attn_out_norm=ATTN_W, sgu_out_norm=SGU_W, ln_pre_mix=D_MODEL, ln_post_mix=D_MODEL, ln_pre_ffn=D_MODEL,
                  ln_post_ffn=D_MODEL, b_pe_gate=D_MODEL)
SUBLANES = 8
ROW_SHARDED = ("w_out", "w_down", "w_pe_gate")
COL_SHARDED = ("w_in", "w_gate_up", "w_pe_proj")
WEIGHTS = ("ln_pre_mix", "w_in", "sgu_ln_g", "sgu_ln_b", "w_spatial", "b_spatial", "attn_out_norm", "sgu_out_norm",
           "w_out", "ln_post_mix", "ln_pre_ffn", "w_gate_up", "w_down", "ln_post_ffn", "w_pe_gate", "b_pe_gate",
           "w_pe_proj")


def _group_rows(width, names, extra=0):
    rows = sum(SMALL_SIZE[n] // width for n in names) + extra
    return -(-rows // SUBLANES) * SUBLANES


def _pack_small_grads(gs, loss_term):
    packed = []
    for width, names in SMALL_GROUPS:
        rows = [gs[n].reshape(-1, width) for n in names]
        extra = int(width == D_MODEL)
        if extra:
            rows.append(jnp.full((1, width), loss_term, F32))
        used = sum(r.shape[0] for r in rows)
        rows.append(jnp.zeros((_group_rows(width, names, extra) - used, width), F32))
        packed.append(jnp.concatenate(rows, axis=0))
    return packed


def _small_adamw(arrived, arrived_late, w, m, v):
    names = [n for _, ns in SMALL_GROUPS for n in ns] + [SMALL_LATE]
    n_groups = len(SMALL_GROUPS)

    def body(*refs):
        group_refs, late_ref = refs[:n_groups], refs[n_groups]
        state = refs[n_groups + 1:n_groups + 1 + 3 * len(names)]
        outs = refs[n_groups + 1 + 3 * len(names):]
        sums = [_sum_parts(r) for r in group_refs]

        def update(name, g):
            i = names.index(name)
            w_ref, m_ref, v_ref = state[3 * i:3 * i + 3]
            delta, nm, nv = _adamw_math(g, w_ref[...].reshape(g.shape), m_ref[...].reshape(g.shape),
                                        v_ref[...].reshape(g.shape))
            for o_ref, val in zip(outs[4 * i:4 * i + 4], (g, delta, nm, nv)):
                o_ref[...] = val.reshape(o_ref.shape)

        for (width, group), total in zip(SMALL_GROUPS, sums):
            row = 0
            for name in group:
                rows = SMALL_SIZE[name] // width
                update(name, total[row:row + rows, :])
                row += rows
            if width == D_MODEL:
                outs[-1][...] = total[row:row + 1, :LANES]
        update(SMALL_LATE, _sum_parts(late_ref)[:1, :])

    state = [t[n] for n in names for t in (w, m, v)]
    plain = jax.ShapeDtypeStruct
    out_shape = [plain(w[n].shape, F32) for n in names for _ in range(4)] + [plain((1, LANES), F32)]
    outs = pl.pallas_call(body, out_shape=out_shape, name="adamw_small",
                          compiler_params=pltpu.CompilerParams(vmem_limit_bytes=32 * MIB))(*arrived, arrived_late, *state)
    return {n: outs[4 * i:4 * i + 4] for i, n in enumerate(names)}, outs[-1]


def _slabs(full):
    return full.reshape(N_DEV, full.shape[0] // N_DEV, full.shape[1])


def kernel(x, p, ln_pre_mix, w_in, sgu_ln_g, sgu_ln_b, w_spatial, b_spatial, attn_out_norm, sgu_out_norm, w_out, ln_post_mix, ln_pre_ffn, w_gate_up, w_down, ln_post_ffn, w_pe_gate, b_pe_gate, w_pe_proj, loss_target, m_ln_pre_mix, m_w_in, m_sgu_ln_g, m_sgu_ln_b, m_w_spatial, m_b_spatial, m_attn_out_norm, m_sgu_out_norm, m_w_out, m_ln_post_mix, m_ln_pre_ffn, m_w_gate_up, m_w_down, m_ln_post_ffn, m_w_pe_gate, m_b_pe_gate, m_w_pe_proj, v_ln_pre_mix, v_w_in, v_sgu_ln_g, v_sgu_ln_b, v_w_spatial, v_b_spatial, v_attn_out_norm, v_sgu_out_norm, v_w_out, v_ln_post_mix, v_ln_pre_ffn, v_w_gate_up, v_w_down, v_ln_post_ffn, v_w_pe_gate, v_b_pe_gate, v_w_pe_proj):
    given = dict(locals())
    w = {n: given[n] for n in WEIGHTS}
    m = {n: given["m_" + n] for n in WEIGHTS}
    v = {n: given["v_" + n] for n in WEIGHTS}
    xs, ps, target = x[0], p[0, 0], loss_target[0]

    shard = {n: w[n][0].astype(BF) for n in ROW_SHARDED}
    shard.update({n: w[n][0].T.astype(BF) for n in COL_SHARDED})
    sm = {n: w[n][0] for n in SMALL}
    sm = {n: (a.reshape(1, -1) if a.ndim == 1 else a) for n, a in sm.items()}
    slopes = jnp.broadcast_to((2.0 ** -(jnp.arange(8, dtype=F32) + 1.0)).reshape(4, 2, 1), (4, 2, LANES))
    slopes = jnp.concatenate([slopes, jnp.zeros((4, 6, LANES), F32)], axis=1)
    b_st = jnp.pad(sm["b_spatial"].T, ((0, 0), (0, LANES - N_GROUPS)))

    def full(gathered):
        return gathered.reshape(-1, gathered.shape[-1])

    w_in_t = full(_all_gather(shard["w_in"], "gather_w_in"))
    proj, a = _in_proj(xs, sm["ln_pre_mix"], w_in_t)
    later = ("w_out", "w_gate_up", "w_down", "w_pe_gate", "w_pe_proj")
    attn, lse, gathered = _attn_fwd(proj, slopes, [shard[n] for n in later])
    w_out_f, w_gu_t, w_down_f, w_peg_f, w_pep_t = [full(g) for g in gathered]
    sgu = _sgu_fwd(proj, sm["sgu_ln_g"], sm["sgu_ln_b"], sm["w_spatial"], b_st)
    groups, mixed, h1, f = _out_proj(attn, sgu, xs, sm["attn_out_norm"], sm["sgu_out_norm"], w_out_f,
                                     sm["ln_post_mix"], sm["ln_pre_ffn"])
    g, u, act = _gate_up(f, w_gu_t)
    y, h2 = _down_proj(act, w_down_f, h1, sm["ln_post_ffn"])
    dh2, dy, g_peg, g_pep_t, loss_cols, db_peg, d_pff = _pe_loss_and_bwd(
        h2, ps, target, y, w_peg_f, sm["b_pe_gate"], w_pep_t, sm["ln_post_ffn"])
    loss_term = 0.5 * jnp.sum(loss_cols) * (1.0 / D_MODEL)

    arrived = {}
    g_down = _weight_grad(act, dy, "w_down")
    dg, du, (arrived["w_pe_proj"], arrived["w_pe_gate"]) = _down_bwd(dy, w_down_f, g, u, [_slabs(g_pep_t), _slabs(g_peg)])
    dh1, dmx, d_pf, d_pm, (arrived["w_down"],) = _ffn_in_bwd(dg, du, w_gu_t, h1, dh2, mixed, sm["ln_pre_ffn"],
                                                            sm["ln_post_mix"], [_slabs(g_down)])
    dattn, stats, dsgu, d_ga, d_gs, g_gu_t = _out_bwd(dmx, w_out_f, attn, lse, sgu, sm["attn_out_norm"],
                                                      sm["sgu_out_norm"], dg, f)
    duz, d_ws, d_bst, d_lg, d_lb, g_out, g_gu_t = _sgu_bwd(proj, dsgu, sm["sgu_ln_g"], sm["sgu_ln_b"],
                                                           sm["w_spatial"], b_st, groups, dmx, du, f, g_gu_t)
    gs = dict(sgu_ln_g=d_lg, sgu_ln_b=d_lb, w_spatial=d_ws, b_spatial=d_bst[:, :N_GROUPS].T, attn_out_norm=d_ga,
              sgu_out_norm=d_gs, ln_post_mix=d_pm, ln_pre_ffn=d_pf, ln_post_ffn=d_pff, b_pe_gate=db_peg)
    small_grads = _pack_small_grads(gs, loss_term)
    dqkv, (arrived["w_gate_up"], arrived["w_out"], *arrived_small) = _attn_bwd(
        proj, dattn, stats, slopes, [_slabs(g_gu_t), _slabs(g_out), *small_grads],
        [True, True] + [False] * len(small_grads))
    send_sems, recv_sems, slabs, landing, token = _scatter_begin(_slabs(_grad_w_in([dqkv, duz], a)), "w_in_grad_send")
    grad_x, d_g1 = _in_bwd([dqkv, duz], w_in_t, xs, dh1, sm["ln_pre_mix"] + token[:1, :1])
    slabs, landing = _scatter_end(send_sems, recv_sems, slabs, landing, d_g1, "w_in_grad_arrive")
    me = 4 * lax.axis_index("x") + 2 * lax.axis_index("y") + lax.axis_index("c")
    own = lax.dynamic_slice_in_dim(slabs, me, 1, axis=0)
    arrived["w_in"] = lax.dynamic_update_slice_in_dim(landing, own, me, axis=0)
    (arrived_late,) = _scatter_call([jnp.pad(d_g1, ((0, SUBLANES - 1), (0, 0)))], [False], "ln_pre_mix_grad_exchange")

    res = {}
    for n in ROW_SHARDED:
        res[n] = _reduce_adamw(arrived[n], w[n][0], m[n][0], v[n][0], n)
    for n in ("w_in", "w_gate_up"):
        res[n] = [t.T for t in _reduce_adamw(arrived[n], w[n][0].T, m[n][0].T, v[n][0].T, n)]
    for n in ("w_pe_proj",):
        grad = _reduce(arrived[n], n).T
        res[n] = (grad, *_adamw(grad, w[n][0], m[n][0], v[n][0], n))
    small, loss_row = _small_adamw(arrived_small, arrived_late, w, m, v)

    out = []
    for k in range(4):
        out += [res[n][k][None] if n in res else small[n][k] for n in WEIGHTS]
    return (loss_row[0, 0], grad_x[None], *out)
```

```python
import math

import jax
import jax.numpy as jnp
from jax import lax
from jax.experimental import pallas as pl
from jax.experimental.pallas import tpu as pltpu

F32 = jnp.float32
BF = jnp.bfloat16


def SDS(shape, dtype):
    return pltpu.HBM(tuple(shape), dtype)

D_MODEL = 1024
ATTN_W = 512
SGU_W = 512
HEAD_DIM = 64
N_GROUPS = 4
GROUP_DIM = 128
CHUNK = 128
D_FF = 2816
PLE_DIM = 256
PROJ = 3 * ATTN_W + 2 * SGU_W
DILATIONS = (1, 4, 16)
QBLK = 128
EPS = 1e-6
NEG = -1e30
N_DEV = 8
LANES = 128

ADAM_LR = 0.001
ADAM_B1 = 0.9
ADAM_B2 = 0.999
ADAM_EPS = 1e-08
ADAM_WD = 0.01
ADAM_STEP = 10

MIB = 2 ** 20
MESH_ID = pl.DeviceIdType.MESH
HBM = pl.BlockSpec(memory_space=pl.ANY)


def _params(sem, vmem_mib):
    return pltpu.CompilerParams(dimension_semantics=sem, vmem_limit_bytes=vmem_mib * MIB)


def _dot(a, b):
    return jnp.dot(a, b, preferred_element_type=F32)


def _dot_nt(a, b):
    return lax.dot_general(a, b, (((1,), (1,)), ((), ())), preferred_element_type=F32)


def _dot_tn(a, b):
    return lax.dot_general(a, b, (((0,), (0,)), ((), ())), preferred_element_type=F32)


def _rstd(x):
    return lax.rsqrt(jnp.mean(x * x, axis=-1, keepdims=True) + EPS)


def _rms_bwd(dy, x, g):
    r = _rstd(x)
    n = x * r
    dn = dy * g
    dx = r * (dn - n * jnp.mean(dn * n, axis=-1, keepdims=True))
    return dx, dy * n


def _colsum(v):
    return jnp.sum(v, axis=0, keepdims=True)


_G0 = math.sqrt(2.0 / math.pi)
_G1 = 0.044715


def _gelu(x):
    t = jnp.tanh(_G0 * (x + _G1 * x * x * x))
    return 0.5 * x * (1.0 + t), t


def _gelu_grad(x, t):
    return 0.5 * (1.0 + t) + 0.5 * x * (1.0 - t * t) * (_G0 * (1.0 + 3.0 * _G1 * x * x))


def _sigmoid(x):
    return 0.5 * jnp.tanh(0.5 * x) + 0.5


def _row(i):
    return (i, 0)


def _fixed(i):
    return (0, 0)


def _acc_init(step, *refs):
    @pl.when(step == 0)
    def _():
        for r in refs:
            r[...] = jnp.zeros_like(r)


FLIPS = [(dx, dy, dc) for dx in (0, 1) for dy in (0, 1) for dc in (0, 1)][1:]
DMA_SEMS = pltpu.SemaphoreType.DMA


def _mesh_pos():
    return lax.axis_index("x"), lax.axis_index("y"), lax.axis_index("c")


def _remote(src, dst, sems, n, to):
    return pltpu.make_async_remote_copy(src_ref=src, dst_ref=dst, send_sem=sems[0].at[n], recv_sem=sems[1].at[n],
                                        device_id=to, device_id_type=MESH_ID)


class _Scatter:
    def __init__(self, items, sems):
        x, y, c = _mesh_pos()
        me = 4 * x + 2 * y + c
        self.local, self.sends, self.arrivals = [], [], []
        for i, (src, dst, slabbed) in enumerate(items):
            self.local.append(pltpu.make_async_copy(src.at[me] if slabbed else src, dst.at[me], sems[2].at[i]))
            for k, (dx, dy, dc) in enumerate(FLIPS):
                to = (1 - x if dx else x, 1 - y if dy else y, 1 - c if dc else c)
                peer = 4 * to[0] + 2 * to[1] + to[2]
                out = src.at[peer] if slabbed else src
                self.sends.append(_remote(out, dst.at[me], sems, 7 * i + k, to))
                self.arrivals.append(_remote(out, dst.at[peer], sems, 7 * i + k, to))

    def start(self):
        for cp in self.local + self.sends:
            cp.start()

    def wait(self):
        for cp in self.arrivals:
            cp.wait_recv()
        for cp in self.sends:
            cp.wait_send()
        for cp in self.local:
            cp.wait()


def _scatter_sems(n):
    return [DMA_SEMS((7 * n,)), DMA_SEMS((7 * n,)), DMA_SEMS((n,))]


class _Gather:
    def __init__(self, items, sems):
        x, y, c = _mesh_pos()
        me, sibling = (x, y, c), (x, y, 1 - c)
        chips = [(1 - x, y), (x, 1 - y), (1 - x, 1 - y)]
        self.first, self.passed, self.from_chips, self.rest, self.local = [], [], [], [], []
        for i, (src, dst) in enumerate(items):
            def slot(p, dst=dst):
                return dst.at[4 * p[0] + 2 * p[1] + p[2]]

            def copy(k, block, to, own=False, i=i, src=src, slot=slot):
                return _remote(src if own else slot(block), slot(block), sems, 7 * i + k, to)

            self.local.append(pltpu.make_async_copy(src, slot(me), sems[2].at[i]))
            self.first.append(copy(0, me, sibling, own=True))
            self.first += [copy(1 + j, me, (*chip, c), own=True) for j, chip in enumerate(chips)]
            self.passed += [copy(4 + j, (*chip, c), sibling) for j, chip in enumerate(chips)]
            self.from_chips += [copy(1 + j, (*chip, c), me) for j, chip in enumerate(chips)]
            self.rest.append(copy(0, sibling, me))
            self.rest += [copy(4 + j, (*chip, 1 - c), me) for j, chip in enumerate(chips)]

    def start(self):
        for cp in self.local + self.first:
            cp.start()

    def forward(self):
        for arrived, onward in zip(self.from_chips, self.passed):
            arrived.wait_recv()
            onward.start()

    def finish(self):
        for cp in self.rest:
            cp.wait_recv()
        for cp in self.first + self.passed:
            cp.wait_send()
        for cp in self.local:
            cp.wait()


def _all_gather(shard, name):
    def body(x_ref, out_ref, *sems):
        g = _Gather([(x_ref, out_ref)], sems)
        g.start()
        g.forward()
        g.finish()

    return pl.pallas_call(
        body, out_shape=SDS((N_DEV,) + shard.shape, shard.dtype), in_specs=[HBM], out_specs=HBM,
        scratch_shapes=_scatter_sems(1), name=name)(shard)


def _scatter_call(srcs, slabbed, name):
    n = len(srcs)

    def body(*refs):
        sc = _Scatter(list(zip(refs[:n], refs[n:2 * n], slabbed)), refs[2 * n:])
        sc.start()
        sc.wait()

    shapes = [SDS(s.shape if sl else (N_DEV,) + s.shape, s.dtype) for s, sl in zip(srcs, slabbed)]
    return pl.pallas_call(body, out_shape=shapes, in_specs=[HBM] * n, out_specs=[HBM] * n,
                          scratch_shapes=_scatter_sems(n), name=name)(*srcs)


SEM = pl.BlockSpec(memory_space=pltpu.SEMAPHORE)
N_PEERS = len(FLIPS)


def _slab_copies(src_ref, land_ref, send_sems, recv_sems):
    x, y, c = _mesh_pos()
    me = 4 * x + 2 * y + c
    copies = []
    for k, (dx, dy, dc) in enumerate(FLIPS):
        to = (1 - x if dx else x, 1 - y if dy else y, 1 - c if dc else c)
        peer = 4 * to[0] + 2 * to[1] + to[2]
        sems = (send_sems, recv_sems)
        copies.append((_remote(src_ref.at[peer], land_ref.at[me], sems, k, to),
                       _remote(src_ref.at[peer], land_ref.at[peer], sems, k, to)))
    return copies


def _scatter_begin(src, name):
    def body(src_ref, land_ref, send_sems, recv_sems, src_thru, land_thru, token):
        for send, _ in _slab_copies(src_ref, land_ref, send_sems, recv_sems):
            send.start()
        token[...] = jnp.zeros_like(token)

    landing = lax.empty(src.shape, src.dtype)
    return pl.pallas_call(
        body, name=name,
        out_shape=(pltpu.SemaphoreType.DMA((N_PEERS,)), pltpu.SemaphoreType.DMA((N_PEERS,)),
                   pltpu.HBM(src.shape, src.dtype), pltpu.HBM(src.shape, src.dtype),
                   jax.ShapeDtypeStruct((SUBLANES, LANES), F32)),
        in_specs=(HBM, HBM), out_specs=(SEM, SEM, HBM, HBM, pl.BlockSpec(memory_space=pltpu.VMEM)),
        input_output_aliases={0: 2, 1: 3},
        compiler_params=pltpu.CompilerParams(has_side_effects=pltpu.SideEffectType.DATAFLOW_SIDE_EFFECTING))(
            pltpu.with_memory_space_constraint(src, pltpu.HBM), pltpu.with_memory_space_constraint(landing, pltpu.HBM))


def _scatter_end(send_sems, recv_sems, src_thru, land_thru, after, name):
    def body(src_ref, land_ref, send_sems, recv_sems, after_ref, src_dead, land_out):
        for send, arrival in _slab_copies(src_ref, land_ref, send_sems, recv_sems):
            send.wait_send()
            arrival.wait_recv()

    return pl.pallas_call(
        body, name=name,
        out_shape=(pltpu.HBM(src_thru.shape, src_thru.dtype), pltpu.HBM(land_thru.shape, land_thru.dtype)),
        in_specs=(HBM, HBM, SEM, SEM, HBM), out_specs=(HBM, HBM), input_output_aliases={0: 0, 1: 1},
        compiler_params=pltpu.CompilerParams(has_side_effects=pltpu.SideEffectType.DATAFLOW_SIDE_EFFECTING))(
            src_thru, land_thru, send_sems, recv_sems, after)


def _in_proj(x, g1, w_in_t, tm=512):
    T = x.shape[0]

    def body(x_ref, g_ref, w_ref, proj_ref, a_ref):
        xv = x_ref[...]
        a = (xv * _rstd(xv) * g_ref[...]).astype(BF)
        a_ref[...] = a
        proj_ref[...] = _dot_nt(a, w_ref[...])

    return pl.pallas_call(
        body, grid=(T // tm,),
        in_specs=[pl.BlockSpec((tm, D_MODEL), _row), pl.BlockSpec((1, D_MODEL), _fixed),
                  pl.BlockSpec((PROJ, D_MODEL), _fixed)],
        out_specs=[pl.BlockSpec((tm, PROJ), _row), pl.BlockSpec((tm, D_MODEL), _row)],
        out_shape=[SDS((T, PROJ), F32), SDS((T, D_MODEL), BF)],
        name="in_proj", compiler_params=_params(("arbitrary",), 48))(x, g1, w_in_t)


ATTN_GROUP = 16


def _attn_bias(sl_ref, bias_ref):
    qi = lax.broadcasted_iota(jnp.int32, (QBLK, QBLK), 0)
    kj = lax.broadcasted_iota(jnp.int32, (QBLK, QBLK), 1)
    step = qi - kj
    for di, d in enumerate(DILATIONS):
        for j in range(2):
            sl = sl_ref[0, j:j + 1, :]
            cur = jnp.where(step >= 0, -sl * (step * d).astype(F32), NEG)
            prev = jnp.where(step <= 0, -sl * ((step + QBLK) * d).astype(F32), NEG)
            rows = slice(j * QBLK, (j + 1) * QBLK)
            bias_ref[di * 2, rows, :QBLK] = prev
            bias_ref[di * 2, rows, QBLK:] = cur
            bias_ref[di * 2 + 1, rows, :QBLK] = jnp.full((QBLK, QBLK), NEG, F32)
            bias_ref[di * 2 + 1, rows, QBLK:] = cur


def _stack_heads(x, lo):
    return jnp.concatenate([jnp.where(lo, x, 0.0), jnp.where(lo, 0.0, x)], axis=0).astype(BF)


def _unstack_heads(x, lo):
    return jnp.where(lo, x[:QBLK], x[QBLK:])


def _attn_rows(start, d, blocks=1):
    if d == 1:
        return pl.ds(pl.multiple_of(start, QBLK), blocks * QBLK)
    return pl.ds(start, blocks * QBLK, stride=d)


def _attn_group_index(i, nblk, d, group):
    per = nblk // d // group
    r = i // per
    n0 = (i % per) * group
    start = r + (d * QBLK) * n0
    pstart = jnp.maximum(start - d * QBLK, r)
    return start, pstart, n0 == 0


def _attn_plan(nblk, d):
    group = min(ATTN_GROUP, nblk // d)
    return group, max(1, min(ATTN_GROUP // group, d))


def _attn_fwd(proj, slopes, to_gather):
    T = proj.shape[0]
    nblk = T // QBLK
    n_g = len(to_gather)

    def body(q_ref, k_ref, v_ref, sl_ref, *rest):
        srcs, (o_ref, m_ref), dsts = rest[:n_g], rest[n_g:n_g + 2], rest[n_g + 2:2 * n_g + 2]
        sems, (l_ref, bias_ref) = rest[2 * n_g + 2:2 * n_g + 5], rest[2 * n_g + 5:]
        h = pl.program_id(0)

        @pl.when(h == 0)
        def _():
            _Gather(list(zip(srcs, dsts)), sems).start()

        @pl.when(h == pl.num_programs(0) - 1)
        def _():
            _Gather(list(zip(srcs, dsts)), sems).forward()

        _attn_bias(sl_ref, bias_ref)
        lo = lax.broadcasted_iota(jnp.int32, (1, LANES), 1) < HEAD_DIM

        order = list(enumerate(DILATIONS))[::-1]
        for di, d in order:
            group, segs = _attn_plan(nblk, d)
            fresh, last = di == order[0][0], di == order[-1][0]

            def step(i, carry, segs=segs, **kw):
                for s in range(segs):
                    segment(i * segs + s, **kw)
                return carry

            def segment(i, d=d, di=di, group=group, fresh=fresh, last=last):
                start, pstart, first = _attn_group_index(i, nblk, d, group)
                prows = _attn_rows(pstart, d)
                k_prev, v_prev = k_ref[prows, :].astype(BF), v_ref[prows, :].astype(BF)
                for b in range(group):
                    out = _attn_rows(start + b * (d * QBLK), d)
                    k_own, v_own = k_ref[out, :].astype(BF), v_ref[out, :].astype(BF)
                    k2, v2 = jnp.concatenate([k_prev, k_own], axis=0), jnp.concatenate([v_prev, v_own], axis=0)
                    k_prev, v_prev = k_own, v_own
                    bias = bias_ref[di * 2 + first.astype(jnp.int32)] if b == 0 else bias_ref[di * 2]
                    s = _dot_nt(_stack_heads(q_ref[out, :] * (HEAD_DIM ** -0.5), lo), k2) + bias
                    m = jnp.max(s, axis=-1, keepdims=True)
                    pr = jnp.exp(s - m)
                    m_b = _unstack_heads(m, lo)
                    l_b = _unstack_heads(jnp.sum(pr, axis=-1, keepdims=True), lo)
                    o_b = _unstack_heads(_dot(pr.astype(BF), v2), lo)
                    if fresh:
                        m_ref[out, :] = m_b
                        l_ref[out, :] = l_b
                        o_ref[out, :] = o_b
                        continue
                    m_o = m_ref[out, :]
                    m_n = jnp.maximum(m_o, m_b)
                    wa, wb = jnp.exp(m_o - m_n), jnp.exp(m_b - m_n)
                    l_n = wa * l_ref[out, :] + wb * l_b
                    o_n = wa * o_ref[out, :] + wb * o_b
                    if last:
                        m_ref[out, :] = m_n + jnp.log(l_n)
                        o_ref[out, :] = o_n / l_n
                    else:
                        m_ref[out, :] = m_n
                        l_ref[out, :] = l_n
                        o_ref[out, :] = o_n

            lax.fori_loop(0, nblk // (group * segs), step, 0)

        @pl.when(h == pl.num_programs(0) - 1)
        def _():
            _Gather(list(zip(srcs, dsts)), sems).finish()

    col = lambda base: pl.BlockSpec((T, LANES), lambda h: (0, base + h))
    tok = pl.BlockSpec((T, LANES), lambda h: (0, h))
    outs = pl.pallas_call(
        body, grid=(4,),
        in_specs=[col(0), col(4), col(8), pl.BlockSpec((1, 8, LANES), lambda h: (h, 0, 0))] + [HBM] * n_g,
        out_specs=[tok, tok] + [HBM] * n_g,
        out_shape=[SDS((T, ATTN_W), F32), SDS((T, ATTN_W), F32)]
        + [SDS((N_DEV,) + g.shape, g.dtype) for g in to_gather],
        scratch_shapes=_scatter_sems(n_g) + [pltpu.VMEM((T, LANES), F32), pltpu.VMEM((6, 2 * QBLK, 2 * QBLK), F32)],
        name="attn_fwd", compiler_params=_params(("arbitrary",), 56))(proj, proj, proj, slopes, *to_gather)
    return outs[0], outs[1], outs[2:]


def _sgu_norm(zv, ln_g, ln_b):
    gz, tz = _gelu(zv)
    mu = jnp.mean(gz, axis=-1, keepdims=True)
    xc = gz - mu
    rs = lax.rsqrt(jnp.mean(xc * xc, axis=-1, keepdims=True) + EPS)
    xhat = xc * rs
    return xhat * ln_g + ln_b, xhat, rs, tz


def _causal(w):
    i = lax.broadcasted_iota(jnp.int32, (CHUNK, CHUNK), 0)
    j = lax.broadcasted_iota(jnp.int32, (CHUNK, CHUNK), 1)
    return jnp.where(i >= j, w, 0.0)


def _sgu_fwd(proj, ln_g, ln_b, w_s, b_st, tm=512):
    T = proj.shape[0]

    def body(u_ref, z_ref, g_ref, b_ref, w_ref, bs_ref, out_ref):
        for g in range(N_GROUPS):
            wm = _causal(w_ref[g]).astype(BF)
            cols = slice(g * GROUP_DIM, (g + 1) * GROUP_DIM)
            for c in range(tm // CHUNK):
                rows = slice(c * CHUNK, (c + 1) * CHUNK)
                zn, _, _, _ = _sgu_norm(z_ref[rows, cols], g_ref[...], b_ref[...])
                mixed = _dot(wm, zn.astype(BF)) + bs_ref[:, g:g + 1]
                gu, _ = _gelu(u_ref[rows, cols])
                out_ref[rows, cols] = gu * mixed

    return pl.pallas_call(
        body, grid=(T // tm,),
        in_specs=[pl.BlockSpec((tm, SGU_W), lambda i: (i, 3)), pl.BlockSpec((tm, SGU_W), lambda i: (i, 4)),
                  pl.BlockSpec((1, GROUP_DIM), _fixed), pl.BlockSpec((1, GROUP_DIM), _fixed),
                  pl.BlockSpec((N_GROUPS, CHUNK, CHUNK), lambda i: (0, 0, 0)), pl.BlockSpec((CHUNK, LANES), _fixed)],
        out_specs=pl.BlockSpec((tm, SGU_W), _row),
        out_shape=SDS((T, SGU_W), F32),
        name="sgu_fwd", compiler_params=_params(("arbitrary",), 32))(proj, proj, ln_g, ln_b, w_s, b_st)


def _out_proj(attn, sgu, x, g_a, g_s, w_out, g_pm, g_pf, tm=512):
    T = x.shape[0]

    def body(a_ref, s_ref, x_ref, ga_ref, gs_ref, w_ref, gpm_ref, gpf_ref, grp_ref, mixed_ref, h1_ref, f_ref):
        av, sv = a_ref[...], s_ref[...]
        an = (av * _rstd(av) * ga_ref[...]).astype(BF)
        sn = (sv * _rstd(sv) * gs_ref[...]).astype(BF)
        grp_ref[:, :ATTN_W] = an
        grp_ref[:, ATTN_W:] = sn
        mixed = _dot(an, w_ref[:ATTN_W, :]) + _dot(sn, w_ref[ATTN_W:, :])
        mixed_ref[...] = mixed
        h1 = x_ref[...] + mixed * _rstd(mixed) * gpm_ref[...]
        h1_ref[...] = h1
        f_ref[...] = (h1 * _rstd(h1) * gpf_ref[...]).astype(BF)

    tok = lambda w: pl.BlockSpec((tm, w), _row)
    vec = lambda w: pl.BlockSpec((1, w), _fixed)
    return pl.pallas_call(
        body, grid=(T // tm,),
        in_specs=[tok(ATTN_W), tok(SGU_W), tok(D_MODEL), vec(ATTN_W), vec(SGU_W),
                  pl.BlockSpec((D_MODEL, D_MODEL), _fixed), vec(D_MODEL), vec(D_MODEL)],
        out_specs=[tok(D_MODEL)] * 4,
        out_shape=[SDS((T, D_MODEL), BF), SDS((T, D_MODEL), F32), SDS((T, D_MODEL), F32), SDS((T, D_MODEL), BF)],
        name="out_proj", compiler_params=_params(("arbitrary",), 48))(attn, sgu, x, g_a, g_s, w_out, g_pm, g_pf)


FF_TILE = 1408
FF_TILES = D_FF // FF_TILE
FF_CHUNK = 256


def _gate_up(f, w_gu_t, tm=512):
    T = f.shape[0]
    tn = FF_TILE

    def body(f_ref, wg_ref, wu_ref, g_ref, u_ref, act_ref):
        fv = f_ref[...]
        g = _dot_nt(fv, wg_ref[...])
        u = _dot_nt(fv, wu_ref[...])
        g_ref[...] = g.astype(BF)
        u_ref[...] = u.astype(BF)
        act_ref[...] = (g * _sigmoid(g) * u).astype(BF)

    ospec = pl.BlockSpec((tm, tn), lambda j, i: (i, j))
    return pl.pallas_call(
        body, grid=(FF_TILES, T // tm),
        in_specs=[pl.BlockSpec((tm, D_MODEL), lambda j, i: (i, 0)), pl.BlockSpec((tn, D_MODEL), lambda j, i: (j, 0)),
                  pl.BlockSpec((tn, D_MODEL), lambda j, i: (j + FF_TILES, 0))],
        out_specs=[ospec] * 3, out_shape=[SDS((T, D_FF), BF)] * 3,
        name="gate_up", compiler_params=_params(("arbitrary", "arbitrary"), 40))(f, w_gu_t, w_gu_t)


def _down_proj(act, w_down, h1, g_pff, tm=512):
    T = act.shape[0]

    def body(a_ref, w_ref, h1_ref, g_ref, y_ref, h2_ref):
        y = _dot(a_ref[...], w_ref[...])
        y_ref[...] = y
        h2_ref[...] = h1_ref[...] + y * _rstd(y) * g_ref[...]

    return pl.pallas_call(
        body, grid=(T // tm,),
        in_specs=[pl.BlockSpec((tm, D_FF), _row), pl.BlockSpec((D_FF, D_MODEL), _fixed),
                  pl.BlockSpec((tm, D_MODEL), _row), pl.BlockSpec((1, D_MODEL), _fixed)],
        out_specs=[pl.BlockSpec((tm, D_MODEL), _row)] * 2,
        out_shape=[SDS((T, D_MODEL), F32)] * 2,
        name="down_proj", compiler_params=_params(("arbitrary",), 48))(act, w_down, h1, g_pff)


def _pe_loss_and_bwd(h2, p, target, y, w_peg, b_peg, w_pep_t, g_pff, tm=512):
    T = h2.shape[0]

    def body(h2_ref, p_ref, t_ref, y_ref, wg_ref, b_ref, wp_ref, g_ref,
             dh2_ref, dy_ref, gpeg_ref, gpep_ref, loss_ref, db_ref, dg_ref):
        _acc_init(pl.program_id(0), gpeg_ref, gpep_ref, loss_ref, db_ref, dg_ref)
        h2v = h2_ref[...]
        h2b = h2v.astype(BF)
        pb = p_ref[...].astype(BF)
        gate = _sigmoid(_dot(h2b, wg_ref[...]) + b_ref[...])
        pp = _dot_nt(pb, wp_ref[...])
        diff = h2v + gate * pp - t_ref[...]
        loss_ref[...] += _colsum(diff * diff)
        dh3 = diff * (1.0 / D_MODEL)
        dpre = dh3 * pp * (gate * (1.0 - gate))
        dpre_b = dpre.astype(BF)
        db_ref[...] += _colsum(dpre)
        gpeg_ref[...] += _dot_tn(h2b, dpre_b)
        gpep_ref[...] += _dot_tn((dh3 * gate).astype(BF), pb)
        dh2 = dh3 + _dot_nt(dpre_b, wg_ref[...])
        dh2_ref[...] = dh2
        dy, dg = _rms_bwd(dh2, y_ref[...], g_ref[...])
        dy_ref[...] = dy.astype(BF)
        dg_ref[...] += _colsum(dg)

    tok = lambda w: pl.BlockSpec((tm, w), _row)
    vec = pl.BlockSpec((1, D_MODEL), _fixed)
    wg = pl.BlockSpec((D_MODEL, D_MODEL), _fixed)
    wp = pl.BlockSpec((D_MODEL, PLE_DIM), _fixed)
    return pl.pallas_call(
        body, grid=(T // tm,),
        in_specs=[tok(D_MODEL), tok(PLE_DIM), tok(D_MODEL), tok(D_MODEL), wg, vec, wp, vec],
        out_specs=[tok(D_MODEL), tok(D_MODEL), wg, wp, vec, vec, vec],
        out_shape=[SDS((T, D_MODEL), F32), SDS((T, D_MODEL), BF), SDS((D_MODEL, D_MODEL), F32),
                   SDS((D_MODEL, PLE_DIM), F32)] + [SDS((1, D_MODEL), F32)] * 3,
        name="pe_loss_and_bwd", compiler_params=_params(("arbitrary",), 56))(
            h2, p, target, y, w_peg, b_peg, w_pep_t, g_pff)


def _weight_grad(a, dy, name, into=None, row_tile=0, rows=None, tk=1024):
    n = dy.shape[1]
    tn = min(n, 1024)
    T, ka = a.shape
    tka = FF_TILE if ka == D_FF else min(ka, 1024)
    rows = ka if rows is None else rows

    def body(a_ref, dy_ref, *rest):
        out_ref = rest[-1]
        _acc_init(pl.program_id(2), out_ref)
        out_ref[...] += _dot_tn(a_ref[...].astype(BF), dy_ref[...].astype(BF))

    carried = [] if into is None else [into]
    return pl.pallas_call(
        body, grid=(ka // tka, n // tn, T // tk),
        in_specs=[pl.BlockSpec((tk, tka), lambda i, j, k: (k, i)), pl.BlockSpec((tk, tn), lambda i, j, k: (k, j))]
        + [HBM] * len(carried),
        out_specs=pl.BlockSpec((tka, tn), lambda i, j, k: (i + row_tile, j)),
        out_shape=SDS((rows, n), F32), input_output_aliases={2: 0} if carried else {},
        name="grad_" + name, compiler_params=_params(("arbitrary",) * 3, 40))(a, dy, *carried)


def _grad_w_in(dparts, a, tk=1024):
    T = a.shape[0]

    def body(*refs):
        d_refs, a_ref, out_ref, acc_ref = refs[:len(dparts)], refs[-3], refs[-2], refs[-1]
        k = pl.program_id(0)
        _acc_init(k, acc_ref)
        cols = [r[part].astype(BF) for r in d_refs for part in range(r.shape[0])]
        acc_ref[...] += _dot_tn(jnp.concatenate(cols, axis=1), a_ref[...])

        @pl.when(k == pl.num_programs(0) - 1)
        def _():
            out_ref[...] = acc_ref[...].astype(BF)

    return pl.pallas_call(
        body, grid=(T // tk,),
        in_specs=[pl.BlockSpec((d.shape[0], tk, d.shape[2]), lambda k: (0, k, 0)) for d in dparts]
        + [pl.BlockSpec((tk, D_MODEL), lambda k: (k, 0))],
        out_specs=pl.BlockSpec((PROJ, D_MODEL), lambda k: (0, 0)),
        out_shape=SDS((PROJ, D_MODEL), BF), scratch_shapes=[pltpu.VMEM((PROJ, D_MODEL), F32)],
        name="grad_w_in", compiler_params=_params(("arbitrary",), 48))(*dparts, a)


def _down_bwd(dy, w_down, g, u, to_send, tm=512):
    T = dy.shape[0]
    n_s = len(to_send)

    def body(dy_ref, w_ref, g_ref, u_ref, *rest):
        srcs, (dg_ref, du_ref), dsts, sems = rest[:n_s], rest[n_s:n_s + 2], rest[n_s + 2:2 * n_s + 2], rest[2 * n_s + 2:]
        i = pl.program_id(0)
        items = list(zip(srcs, dsts, [True] * n_s))

        @pl.when(i == 0)
        def _():
            _Scatter(items, sems).start()

        dyv = dy_ref[...]
        for c in range(D_FF // FF_CHUNK):
            cols = slice(c * FF_CHUNK, (c + 1) * FF_CHUNK)
            dact = _dot_nt(dyv, w_ref[cols, :]).astype(BF)
            gv, uv = g_ref[:, cols], u_ref[:, cols]
            s = _sigmoid(gv)
            ds = dact * s
            dg_ref[:, cols] = ds * uv * (1.0 + gv * (1.0 - s))
            du_ref[:, cols] = ds * gv

        @pl.when(i == pl.num_programs(0) - 1)
        def _():
            _Scatter(items, sems).wait()

    tile = pl.BlockSpec((tm, D_FF), _row)
    outs = pl.pallas_call(
        body, grid=(T // tm,),
        in_specs=[pl.BlockSpec((tm, D_MODEL), _row),
                  pl.BlockSpec((D_FF, D_MODEL), _fixed, pipeline_mode=pl.Buffered(1)), tile, tile] + [HBM] * n_s,
        out_specs=[tile, tile] + [HBM] * n_s,
        out_shape=[SDS((T, D_FF), BF)] * 2 + [SDS(s.shape, s.dtype) for s in to_send],
        scratch_shapes=_scatter_sems(n_s),
        name="down_bwd", compiler_params=_params(("arbitrary",), 48))(dy, w_down, g, u, *to_send)
    return outs[0], outs[1], outs[2:]


def _ffn_in_bwd(dg, du, w_gu_t, h1, dh2, mixed, g_pf, g_pm, to_send, tm=512):
    T = h1.shape[0]
    n_s = len(to_send)

    def body(dg_ref, du_ref, wg_ref, wu_ref, h1_ref, dh2_ref, mx_ref, gpf_ref, gpm_ref, *rest):
        srcs, outs, dsts, sems = rest[:n_s], rest[n_s:n_s + 4], rest[n_s + 4:2 * n_s + 4], rest[2 * n_s + 4:]
        dh1_ref, dmx_ref, dgpf_ref, dgpm_ref = outs
        i = pl.program_id(0)
        items = list(zip(srcs, dsts, [True] * n_s))
        _acc_init(i, dgpf_ref, dgpm_ref)

        @pl.when(i == 0)
        def _():
            _Scatter(items, sems).start()

        df = _dot(dg_ref[...], wg_ref[...]) + _dot(du_ref[...], wu_ref[...])
        dx, dgf = _rms_bwd(df, h1_ref[...], gpf_ref[...])
        dh1 = dh2_ref[...] + dx
        dh1_ref[...] = dh1
        dmx, dgm = _rms_bwd(dh1, mx_ref[...], gpm_ref[...])
        dmx_ref[...] = dmx.astype(BF)
        dgpf_ref[...] += _colsum(dgf)
        dgpm_ref[...] += _colsum(dgm)

        @pl.when(i == pl.num_programs(0) - 1)
        def _():
            _Scatter(items, sems).wait()

    tok = lambda w: pl.BlockSpec((tm, w), _row)
    vec = pl.BlockSpec((1, D_MODEL), _fixed)
    outs = pl.pallas_call(
        body, grid=(T // tm,),
        in_specs=[tok(D_FF), tok(D_FF), pl.BlockSpec((D_FF, D_MODEL), lambda i: (0, 0), pipeline_mode=pl.Buffered(1)),
                  pl.BlockSpec((D_FF, D_MODEL), lambda i: (1, 0), pipeline_mode=pl.Buffered(1)),
                  tok(D_MODEL), tok(D_MODEL), tok(D_MODEL), vec, vec]
        + [HBM] * n_s,
        out_specs=[tok(D_MODEL), tok(D_MODEL), vec, vec] + [HBM] * n_s,
        out_shape=[SDS((T, D_MODEL), F32), SDS((T, D_MODEL), BF), SDS((1, D_MODEL), F32), SDS((1, D_MODEL), F32)]
        + [SDS(s.shape, s.dtype) for s in to_send],
        scratch_shapes=_scatter_sems(n_s),
        name="ffn_in_bwd", compiler_params=_params(("arbitrary",), 56))(
            dg, du, w_gu_t, w_gu_t, h1, dh2, mixed, g_pf, g_pm, *to_send)
    return outs[0], outs[1], outs[2], outs[3], outs[4:]


STAT_LANES = HEAD_DIM // 2


def _ff_grad_spec(half):
    return pl.BlockSpec((D_FF, D_MODEL), lambda i: (half, 0), pipeline_mode=pl.Buffered(1))


def _out_bwd(dmx, w_out, attn, lse, sgu, g_a, g_s, dg, f, tm=512):
    T = attn.shape[0]

    def body(dm_ref, w_ref, a_ref, l_ref, s_ref, ga_ref, gs_ref, dgate_ref, f_ref,
             da_ref, st_ref, ds_ref, dga_ref, dgs_ref, ggu_ref):
        _acc_init(pl.program_id(0), dga_ref, dgs_ref, ggu_ref)
        ggu_ref[...] += _dot_tn(dgate_ref[...], f_ref[...])
        dgr = _dot_nt(dm_ref[...], w_ref[...])
        av = a_ref[...]
        da, dga = _rms_bwd(dgr[:, :ATTN_W], av, ga_ref[...])
        ds, dgs = _rms_bwd(dgr[:, ATTN_W:], s_ref[...], gs_ref[...])
        da_ref[...] = da
        ds_ref[...] = ds
        dga_ref[...] += _colsum(dga)
        dgs_ref[...] += _colsum(dgs)
        lane = lax.broadcasted_iota(jnp.int32, (1, LANES), 1)
        lo = lane < HEAD_DIM
        first = (lane % HEAD_DIM) < STAT_LANES
        prod = da * av
        for c in range(ATTN_W // LANES):
            cols = slice(c * LANES, (c + 1) * LANES)
            pc = prod[:, cols]
            delta = jnp.where(lo, jnp.sum(jnp.where(lo, pc, 0.0), axis=-1, keepdims=True),
                              jnp.sum(jnp.where(lo, 0.0, pc), axis=-1, keepdims=True))
            st_ref[:, cols] = jnp.where(first, l_ref[:, cols], delta)

    tok = lambda w: pl.BlockSpec((tm, w), _row)
    vec = lambda w: pl.BlockSpec((1, w), _fixed)
    return pl.pallas_call(
        body, grid=(T // tm,),
        in_specs=[tok(D_MODEL), pl.BlockSpec((D_MODEL, D_MODEL), _fixed), tok(ATTN_W), tok(ATTN_W), tok(SGU_W),
                  vec(ATTN_W), vec(SGU_W), tok(D_FF), tok(D_MODEL)],
        out_specs=[tok(ATTN_W), tok(ATTN_W), tok(SGU_W), vec(ATTN_W), vec(SGU_W), _ff_grad_spec(0)],
        out_shape=[SDS((T, ATTN_W), F32), SDS((T, ATTN_W), F32), SDS((T, SGU_W), F32), SDS((1, ATTN_W), F32),
                   SDS((1, SGU_W), F32), SDS((2 * D_FF, D_MODEL), F32)],
        name="out_bwd", compiler_params=_params(("arbitrary",), 56))(dmx, w_out, attn, lse, sgu, g_a, g_s, dg, f)


def _sgu_bwd(proj, dsgu, ln_g, ln_b, w_s, b_st, groups, dmx, d_up, f, g_gu_t, tm=512):
    T = proj.shape[0]

    def body(u_ref, z_ref, ds_ref, g_ref, b_ref, w_ref, bs_ref, grp_ref, dmx_ref, dup_ref, f_ref, _,
             duz_ref, dw_ref, dbs_ref, dlg_ref, dlb_ref, gout_ref, ggu_ref, dbacc_ref):
        du_ref, dz_ref = duz_ref.at[0], duz_ref.at[1]
        step = pl.program_id(0)
        _acc_init(step, dw_ref, dbs_ref, dlg_ref, dlb_ref, gout_ref, ggu_ref, dbacc_ref)
        gout_ref[...] += _dot_tn(grp_ref[...], dmx_ref[...])
        ggu_ref[...] += _dot_tn(dup_ref[...], f_ref[...])
        lng, lnb = g_ref[...], b_ref[...]
        for g in range(N_GROUPS):
            wm = _causal(w_ref[g]).astype(BF)
            cols = slice(g * GROUP_DIM, (g + 1) * GROUP_DIM)
            for c in range(tm // CHUNK):
                rows = slice(c * CHUNK, (c + 1) * CHUNK)
                zv, uv, dout = z_ref[rows, cols], u_ref[rows, cols], ds_ref[rows, cols]
                zn, xhat, rs, tz = _sgu_norm(zv, lng, lnb)
                znb = zn.astype(BF)
                mixed = _dot(wm, znb) + bs_ref[:, g:g + 1]
                gu, tu = _gelu(uv)
                du_ref[rows, cols] = (dout * mixed * _gelu_grad(uv, tu)).astype(BF)
                dmix = dout * gu
                dmb = dmix.astype(BF)
                dw_ref[g] += _causal(_dot_nt(dmb, znb))
                dbacc_ref[g] += dmix
                dzn = _dot_tn(wm, dmb)
                dlg_ref[...] += _colsum(dzn * xhat)
                dlb_ref[...] += _colsum(dzn)
                dxh = dzn * lng
                dgz = rs * (dxh - jnp.mean(dxh, axis=-1, keepdims=True)
                            - xhat * jnp.mean(dxh * xhat, axis=-1, keepdims=True))
                dz_ref[rows, cols] = (dgz * _gelu_grad(zv, tz)).astype(BF)

        @pl.when(step == pl.num_programs(0) - 1)
        def _():
            lane = lax.broadcasted_iota(jnp.int32, (CHUNK, LANES), 1)
            acc = jnp.zeros((CHUNK, LANES), F32)
            for g in range(N_GROUPS):
                acc = jnp.where(lane == g, jnp.sum(dbacc_ref[g], axis=-1, keepdims=True), acc)
            dbs_ref[...] = acc

    tok = pl.BlockSpec((tm, SGU_W), _row)
    vec = pl.BlockSpec((1, GROUP_DIM), _fixed)
    wsp = pl.BlockSpec((N_GROUPS, CHUNK, CHUNK), lambda i: (0, 0, 0))
    sq = pl.BlockSpec((CHUNK, LANES), _fixed)
    wide = pl.BlockSpec((tm, D_MODEL), _row)
    return pl.pallas_call(
        body, grid=(T // tm,),
        in_specs=[pl.BlockSpec((tm, SGU_W), lambda i: (i, 3)), pl.BlockSpec((tm, SGU_W), lambda i: (i, 4)), tok,
                  vec, vec, wsp, sq, wide, wide, pl.BlockSpec((tm, D_FF), _row), wide, HBM],
        out_specs=[pl.BlockSpec((2, tm, SGU_W), lambda i: (0, i, 0)), wsp, sq, vec, vec,
                   pl.BlockSpec((D_MODEL, D_MODEL), _fixed), _ff_grad_spec(1)],
        out_shape=[SDS((2, T, SGU_W), BF), SDS((N_GROUPS, CHUNK, CHUNK), F32),
                   SDS((CHUNK, LANES), F32), SDS((1, GROUP_DIM), F32), SDS((1, GROUP_DIM), F32),
                   SDS((D_MODEL, D_MODEL), F32), SDS((2 * D_FF, D_MODEL), F32)],
        input_output_aliases={11: 6},
        scratch_shapes=[pltpu.VMEM((N_GROUPS, CHUNK, LANES), F32)],
        name="sgu_bwd", compiler_params=_params(("arbitrary",), 56))(
            proj, proj, dsgu, ln_g, ln_b, w_s, b_st, groups, dmx, d_up, f, g_gu_t)


def _attn_bwd(proj, do, stats, slopes, to_send, slabbed):
    T = proj.shape[0]
    nblk = T // QBLK
    n_s = len(to_send)

    def body(q_ref, k_ref, v_ref, do_ref, st_ref, sl_ref, *rest):
        srcs, d_ref, dsts = rest[:n_s], rest[n_s], rest[n_s + 1:2 * n_s + 1]
        sems, bias_ref = rest[2 * n_s + 1:2 * n_s + 4], rest[2 * n_s + 4]
        dq_ref, dk_ref, dv_ref = d_ref.at[0], d_ref.at[1], d_ref.at[2]
        h = pl.program_id(0)
        items = list(zip(srcs, dsts, slabbed))

        @pl.when(h == 0)
        def _():
            _Scatter(items, sems).start()

        _attn_bias(sl_ref, bias_ref)
        lo = lax.broadcasted_iota(jnp.int32, (1, LANES), 1) < HEAD_DIM
        scale = HEAD_DIM ** -0.5
        d_ref[...] = jnp.zeros_like(d_ref)

        for di, d in enumerate(DILATIONS):
            group, segs = _attn_plan(nblk, d)

            def step(i, carry, segs=segs, **kw):
                for s in range(segs):
                    segment(i * segs + s, **kw)
                return carry

            def segment(i, d=d, di=di, group=group):
                start, pstart, first = _attn_group_index(i, nblk, d, group)
                rows, prows = _attn_rows(start, d, group), _attn_rows(pstart, d)
                q = q_ref[rows, :] * scale
                k = jnp.concatenate([k_ref[prows, :], k_ref[rows, :]], axis=0).astype(BF)
                v = jnp.concatenate([v_ref[prows, :], v_ref[rows, :]], axis=0).astype(BF)
                dov = do_ref[rows, :]
                stats = st_ref[rows, :]
                masks = [lo, ~lo]
                qm = [jnp.where(masks[j], q, 0.0).astype(BF) for j in range(2)]
                dom = [jnp.where(masks[j], dov, 0.0).astype(BF) for j in range(2)]
                for b in range(group):
                    qb = slice(b * QBLK, (b + 1) * QBLK)
                    kb = slice(b * QBLK, (b + 2) * QBLK)
                    which = di * 2 + first.astype(jnp.int32) if b == 0 else di * 2
                    dq_parts, prs, dss = [], [], []
                    for j in range(2):
                        bias = bias_ref[which, j * QBLK:(j + 1) * QBLK, :]
                        lj = stats[qb, j * HEAD_DIM:j * HEAD_DIM + 1]
                        delta = stats[qb, j * HEAD_DIM + STAT_LANES:j * HEAD_DIM + STAT_LANES + 1]
                        pr = jnp.exp(_dot_nt(qm[j][qb], k[kb]) + bias - lj)
                        ds = (pr * (_dot_nt(dom[j][qb], v[kb]) - delta)).astype(BF)
                        dq_parts.append(_dot(ds, k[kb]))
                        prs.append(pr.astype(BF))
                        dss.append(ds)
                    dk_b = _dot_tn(jnp.concatenate(dss, axis=0), jnp.concatenate([qm[0][qb], qm[1][qb]], axis=0))
                    dv_b = _dot_tn(jnp.concatenate(prs, axis=0), jnp.concatenate([dom[0][qb], dom[1][qb]], axis=0))
                    own = _attn_rows(start + b * (d * QBLK), d)
                    dq_ref[own, :] += jnp.where(lo, dq_parts[0], dq_parts[1]) * scale
                    if b == 0:
                        dk_ref[prows, :] += dk_b[:QBLK]
                        dv_ref[prows, :] += dv_b[:QBLK]
                        dk_ref[own, :] += dk_b[QBLK:]
                        dv_ref[own, :] += dv_b[QBLK:]
                    else:
                        two = _attn_rows(start + (b - 1) * (d * QBLK), d, 2)
                        dk_ref[two, :] += dk_b
                        dv_ref[two, :] += dv_b

            lax.fori_loop(0, nblk // (group * segs), step, 0)

        @pl.when(h == pl.num_programs(0) - 1)
        def _():
            _Scatter(items, sems).wait()

    col = lambda base: pl.BlockSpec((T, LANES), lambda h: (0, base + h))
    outs = pl.pallas_call(
        body, grid=(4,),
        in_specs=[col(0), col(4), col(8), col(0), col(0), pl.BlockSpec((1, 8, LANES), lambda h: (h, 0, 0))]
        + [HBM] * n_s,
        out_specs=[pl.BlockSpec((3, T, LANES), lambda h: (0, 0, h), pipeline_mode=pl.Buffered(1))] + [HBM] * n_s,
        out_shape=[SDS((3, T, ATTN_W), F32)]
        + [SDS(s.shape if sl else (N_DEV,) + s.shape, s.dtype) for s, sl in zip(to_send, slabbed)],
        scratch_shapes=_scatter_sems(n_s) + [pltpu.VMEM((6, 2 * QBLK, 2 * QBLK), F32)],
        name="attn_bwd", compiler_params=_params(("arbitrary",), 60))(proj, proj, proj, do, stats, slopes, *to_send)
    return outs[0], outs[1:]


def _in_bwd(dparts, w_in_t, x, dh1, g1, tm=512):
    T = x.shape[0]
    n = len(dparts)
    w = ATTN_W

    def body(*refs):
        d_refs, (w_ref, x_ref, dh1_ref, g_ref, dx_ref, dg_ref) = refs[:n], refs[n:]
        _acc_init(pl.program_id(0), dg_ref)
        da = None
        col = 0
        for r in d_refs:
            for part in range(r.shape[0]):
                t = _dot(r[part].astype(BF), w_ref[col * w:(col + 1) * w, :])
                da = t if da is None else da + t
                col += 1
        dx, dg = _rms_bwd(da, x_ref[...], g_ref[...])
        dx_ref[...] = dh1_ref[...] + dx
        dg_ref[...] += _colsum(dg)

    tok = lambda c: pl.BlockSpec((tm, c), _row)
    vec = pl.BlockSpec((1, D_MODEL), _fixed)
    return pl.pallas_call(
        body, grid=(T // tm,),
        in_specs=[pl.BlockSpec((d.shape[0], tm, w), lambda i: (0, i, 0)) for d in dparts]
        + [pl.BlockSpec((PROJ, D_MODEL), _fixed), tok(D_MODEL), tok(D_MODEL), vec],
        out_specs=[tok(D_MODEL), vec],
        out_shape=[SDS((T, D_MODEL), F32), SDS((1, D_MODEL), F32)],
        name="in_bwd", compiler_params=_params(("arbitrary",), 52))(*dparts, w_in_t, x, dh1, g1)


def _sum_parts(p_ref):
    g = p_ref[0].astype(F32)
    for s in range(1, N_DEV):
        g = g + p_ref[s].astype(F32)
    return g


def _adamw_math(g, w, m, v):
    nm = ADAM_B1 * m + (1.0 - ADAM_B1) * g
    nv = ADAM_B2 * v + (1.0 - ADAM_B2) * (g * g)
    m_hat = nm / (1.0 - ADAM_B1 ** ADAM_STEP)
    v_hat = nv / (1.0 - ADAM_B2 ** ADAM_STEP)
    return -ADAM_LR * (m_hat / (jnp.sqrt(v_hat) + ADAM_EPS) + ADAM_WD * w), nm, nv


def _row_tile(rows):
    for t in (256, 176, 128, 80):
        if rows % t == 0:
            return t
    raise ValueError(rows)


def _reduce_adamw(parts, w, m, v, name):
    rows, width = w.shape
    tr = _row_tile(rows)

    def body(p_ref, w_ref, m_ref, v_ref, g_ref, d_ref, nm_ref, nv_ref):
        g = _sum_parts(p_ref)
        g_ref[...] = g
        d_ref[...], nm_ref[...], nv_ref[...] = _adamw_math(g, w_ref[...], m_ref[...], v_ref[...])

    blk = pl.BlockSpec((tr, width), _row)
    return pl.pallas_call(
        body, grid=(rows // tr,),
        in_specs=[pl.BlockSpec((N_DEV, tr, width), lambda i: (0, i, 0)), blk, blk, blk],
        out_specs=[blk] * 4, out_shape=[SDS((rows, width), F32)] * 4,
        name="adamw_" + name, compiler_params=_params(("arbitrary",), 32))(parts, w, m, v)


def _reduce(parts, name):
    _, rows, width = parts.shape
    tr = _row_tile(rows)

    def body(p_ref, g_ref):
        g_ref[...] = _sum_parts(p_ref)

    return pl.pallas_call(
        body, grid=(rows // tr,),
        in_specs=[pl.BlockSpec((N_DEV, tr, width), lambda i: (0, i, 0))],
        out_specs=pl.BlockSpec((tr, width), _row), out_shape=SDS((rows, width), F32),
        name="sum_" + name, compiler_params=_params(("arbitrary",), 32))(parts)


def _adamw(g, w, m, v, name):
    rows, width = w.shape
    tr = _row_tile(rows)

    def body(g_ref, w_ref, m_ref, v_ref, d_ref, nm_ref, nv_ref):
        d_ref[...], nm_ref[...], nv_ref[...] = _adamw_math(g_ref[...], w_ref[...], m_ref[...], v_ref[...])

    blk = pl.BlockSpec((tr, width), _row)
    return pl.pallas_call(
        body, grid=(rows // tr,), in_specs=[blk] * 4, out_specs=[blk] * 3, out_shape=[SDS((rows, width), F32)] * 3,
        name="adamw_" + name, compiler_params=_params(("arbitrary",), 32))(g, w, m, v)


SMALL = ("w_spatial", "ln_pre_mix", "ln_post_mix", "ln_pre_ffn", "ln_post_ffn", "b_pe_gate",
         "attn_out_norm", "sgu_out_norm", "b_spatial", "sgu_ln_g", "sgu_ln_b")
SMALL_GROUPS = ((128, ("w_spatial", "b_spatial", "sgu_ln_g", "sgu_ln_b")),
                (512, ("attn_out_norm", "sgu_out_norm")),
                (1024, ("ln_post_mix", "ln_pre_ffn", "ln_post_ffn", "b_pe_gate")))
SMALL_LATE = "ln_pre_mix"
SMALL_SIZE = dict(w_spatial=N_GROUPS * CHUNK * CHUNK, b_spatial=N_GROUPS * CHUNK, sgu_ln_g=GROUP_DIM, sgu_ln_b=GROUP_DIM,
                  attn_out_norm=ATTN_W, sgu_out_norm=SGU_W, ln_pre_mix=D_MODEL, ln_post_mix=D_MODEL, ln_pre_ffn=D_MODEL,
                  ln_post_ffn=D_MODEL, b_pe_gate=D_MODEL)
SUBLANES = 8
ROW_SHARDED = ("w_out", "w_down", "w_pe_gate")
COL_SHARDED = ("w_in", "w_gate_up", "w_pe_proj")
WEIGHTS = ("ln_pre_mix", "w_in", "sgu_ln_g", "sgu_ln_b", "w_spatial", "b_spatial", "attn_out_norm", "sgu_out_norm",
           "w_out", "ln_post_mix", "ln_pre_ffn", "w_gate_up", "w_down", "ln_post_ffn", "w_pe_gate", "b_pe_gate",
           "w_pe_proj")


def _group_rows(width, names, extra=0):
    rows = sum(SMALL_SIZE[n] // width for n in names) + extra
    return -(-rows // SUBLANES) * SUBLANES


def _pack_small_grads(gs, loss_term):
    packed = []
    for width, names in SMALL_GROUPS:
        rows = [gs[n].reshape(-1, width) for n in names]
        extra = int(width == D_MODEL)
        if extra:
            rows.append(jnp.full((1, width), loss_term, F32))
        used = sum(r.shape[0] for r in rows)
        rows.append(jnp.zeros((_group_rows(width, names, extra) - used, width), F32))
        packed.append(jnp.concatenate(rows, axis=0))
    return packed


def _small_adamw(arrived, arrived_late, w, m, v):
    names = [n for _, ns in SMALL_GROUPS for n in ns] + [SMALL_LATE]
    n_groups = len(SMALL_GROUPS)

    def body(*refs):
        group_refs, late_ref = refs[:n_groups], refs[n_groups]
        state = refs[n_groups + 1:n_groups + 1 + 3 * len(names)]
        outs = refs[n_groups + 1 + 3 * len(names):]
        sums = [_sum_parts(r) for r in group_refs]

        def update(name, g):
            i = names.index(name)
            w_ref, m_ref, v_ref = state[3 * i:3 * i + 3]
            delta, nm, nv = _adamw_math(g, w_ref[...].reshape(g.shape), m_ref[...].reshape(g.shape),
                                        v_ref[...].reshape(g.shape))
            for o_ref, val in zip(outs[4 * i:4 * i + 4], (g, delta, nm, nv)):
                o_ref[...] = val.reshape(o_ref.shape)

        for (width, group), total in zip(SMALL_GROUPS, sums):
            row = 0
            for name in group:
                rows = SMALL_SIZE[name] // width
                update(name, total[row:row + rows, :])
                row += rows
            if width == D_MODEL:
                outs[-1][...] = total[row:row + 1, :LANES]
        update(SMALL_LATE, _sum_parts(late_ref)[:1, :])

    state = [t[n] for n in names for t in (w, m, v)]
    plain = jax.ShapeDtypeStruct
    out_shape = [plain(w[n].shape, F32) for n in names for _ in range(4)] + [plain((1, LANES), F32)]
    outs = pl.pallas_call(body, out_shape=out_shape, name="adamw_small",
                          compiler_params=pltpu.CompilerParams(vmem_limit_bytes=32 * MIB))(*arrived, arrived_late, *state)
    return {n: outs[4 * i:4 * i + 4] for i, n in enumerate(names)}, outs[-1]


def _slabs(full):
    return full.reshape(N_DEV, full.shape[0] // N_DEV, full.shape[1])


def kernel(x, p, ln_pre_mix, w_in, sgu_ln_g, sgu_ln_b, w_spatial, b_spatial, attn_out_norm, sgu_out_norm, w_out, ln_post_mix, ln_pre_ffn, w_gate_up, w_down, ln_post_ffn, w_pe_gate, b_pe_gate, w_pe_proj, loss_target, m_ln_pre_mix, m_w_in, m_sgu_ln_g, m_sgu_ln_b, m_w_spatial, m_b_spatial, m_attn_out_norm, m_sgu_out_norm, m_w_out, m_ln_post_mix, m_ln_pre_ffn, m_w_gate_up, m_w_down, m_ln_post_ffn, m_w_pe_gate, m_b_pe_gate, m_w_pe_proj, v_ln_pre_mix, v_w_in, v_sgu_ln_g, v_sgu_ln_b, v_w_spatial, v_b_spatial, v_attn_out_norm, v_sgu_out_norm, v_w_out, v_ln_post_mix, v_ln_pre_ffn, v_w_gate_up, v_w_down, v_ln_post_ffn, v_w_pe_gate, v_b_pe_gate, v_w_pe_proj):
    given = dict(locals())
    w = {n: given[n] for n in WEIGHTS}
    m = {n: given["m_" + n] for n in WEIGHTS}
    v = {n: given["v_" + n] for n in WEIGHTS}
    xs, ps, target = x[0], p[0, 0], loss_target[0]

    shard = {n: w[n][0].astype(BF) for n in ROW_SHARDED}
    shard.update({n: w[n][0].T.astype(BF) for n in COL_SHARDED})
    sm = {n: w[n][0] for n in SMALL}
    sm = {n: (a.reshape(1, -1) if a.ndim == 1 else a) for n, a in sm.items()}
    slopes = jnp.broadcast_to((2.0 ** -(jnp.arange(8, dtype=F32) + 1.0)).reshape(4, 2, 1), (4, 2, LANES))
    slopes = jnp.concatenate([slopes, jnp.zeros((4, 6, LANES), F32)], axis=1)
    b_st = jnp.pad(sm["b_spatial"].T, ((0, 0), (0, LANES - N_GROUPS)))

    def full(gathered):
        return gathered.reshape(-1, gathered.shape[-1])

    w_in_t = full(_all_gather(shard["w_in"], "gather_w_in"))
    proj, a = _in_proj(xs, sm["ln_pre_mix"], w_in_t)
    later = ("w_out", "w_gate_up", "w_down", "w_pe_gate", "w_pe_proj")
    attn, lse, gathered = _attn_fwd(proj, slopes, [shard[n] for n in later])
    w_out_f, w_gu_t, w_down_f, w_peg_f, w_pep_t = [full(g) for g in gathered]
    sgu = _sgu_fwd(proj, sm["sgu_ln_g"], sm["sgu_ln_b"], sm["w_spatial"], b_st)
    groups, mixed, h1, f = _out_proj(attn, sgu, xs, sm["attn_out_norm"], sm["sgu_out_norm"], w_out_f,
                                     sm["ln_post_mix"], sm["ln_pre_ffn"])
    g, u, act = _gate_up(f, w_gu_t)
    y, h2 = _down_proj(act, w_down_f, h1, sm["ln_post_ffn"])
    dh2, dy, g_peg, g_pep_t, loss_cols, db_peg, d_pff = _pe_loss_and_bwd(
        h2, ps, target, y, w_peg_f, sm["b_pe_gate"], w_pep_t, sm["ln_post_ffn"])
    loss_term = 0.5 * jnp.sum(loss_cols) * (1.0 / D_MODEL)

    arrived = {}
    g_down = _weight_grad(act, dy, "w_down")
    dg, du, (arrived["w_pe_proj"], arrived["w_pe_gate"]) = _down_bwd(dy, w_down_f, g, u, [_slabs(g_pep_t), _slabs(g_peg)])
    dh1, dmx, d_pf, d_pm, (arrived["w_down"],) = _ffn_in_bwd(dg, du, w_gu_t, h1, dh2, mixed, sm["ln_pre_ffn"],
                                                            sm["ln_post_mix"], [_slabs(g_down)])
    dattn, stats, dsgu, d_ga, d_gs, g_gu_t = _out_bwd(dmx, w_out_f, attn, lse, sgu, sm["attn_out_norm"],
                                                      sm["sgu_out_norm"], dg, f)
    duz, d_ws, d_bst, d_lg, d_lb, g_out, g_gu_t = _sgu_bwd(proj, dsgu, sm["sgu_ln_g"], sm["sgu_ln_b"],
                                                           sm["w_spatial"], b_st, groups, dmx, du, f, g_gu_t)
    gs = dict(sgu_ln_g=d_lg, sgu_ln_b=d_lb, w_spatial=d_ws, b_spatial=d_bst[:, :N_GROUPS].T, attn_out_norm=d_ga,
              sgu_out_norm=d_gs, ln_post_mix=d_pm, ln_pre_ffn=d_pf, ln_post_ffn=d_pff, b_pe_gate=db_peg)
    small_grads = _pack_small_grads(gs, loss_term)
    dqkv, (arrived["w_gate_up"], arrived["w_out"], *arrived_small) = _attn_bwd(
        proj, dattn, stats, slopes, [_slabs(g_gu_t), _slabs(g_out), *small_grads],
        [True, True] + [False] * len(small_grads))
    send_sems, recv_sems, slabs, landing, token = _scatter_begin(_slabs(_grad_w_in([dqkv, duz], a)), "w_in_grad_send")
    grad_x, d_g1 = _in_bwd([dqkv, duz], w_in_t, xs, dh1, sm["ln_pre_mix"] + token[:1, :1])
    slabs, landing = _scatter_end(send_sems, recv_sems, slabs, landing, d_g1, "w_in_grad_arrive")
    me = 4 * lax.axis_index("x") + 2 * lax.axis_index("y") + lax.axis_index("c")
    own = lax.dynamic_slice_in_dim(slabs, me, 1, axis=0)
    arrived["w_in"] = lax.dynamic_update_slice_in_dim(landing, own, me, axis=0)
    (arrived_late,) = _scatter_call([jnp.pad(d_g1, ((0, SUBLANES - 1), (0, 0)))], [False], "ln_pre_mix_grad_exchange")

    res = {}
    for n in ROW_SHARDED:
        res[n] = _reduce_adamw(arrived[n], w[n][0], m[n][0], v[n][0], n)
    for n in ("w_in", "w_gate_up"):
        res[n] = [t.T for t in _reduce_adamw(arrived[n], w[n][0].T, m[n][0].T, v[n][0].T, n)]
    for n in ("w_pe_proj",):
        grad = _reduce(arrived[n], n).T
        res[n] = (grad, *_adamw(grad, w[n][0], m[n][0], v[n][0], n))
    small, loss_row = _small_adamw(arrived_small, arrived_late, w, m, v)

    out = []
    for k in range(4):
        out += [res[n][k][None] if n in res else small[n][k] for n in WEIGHTS]
    return (loss_row[0, 0], grad_x[None], *out)
```

```python
import math

import jax
import jax.numpy as jnp
from jax import lax
from jax.experimental import pallas as pl
from jax.experimental.pallas import tpu as pltpu

F32 = jnp.float32
BF = jnp.bfloat16


def SDS(shape, dtype):
    return pltpu.HBM(tuple(shape), dtype)

D_MODEL = 1024
ATTN_W = 512
SGU_W = 512
HEAD_DIM = 64
N_GROUPS = 4
GROUP_DIM = 128
CHUNK = 128
D_FF = 2816
PLE_DIM = 256
PROJ = 3 * ATTN_W + 2 * SGU_W
DILATIONS = (1, 4, 16)
QBLK = 128
EPS = 1e-6
NEG = -1e30
N_DEV = 8
LANES = 128

ADAM_LR = 0.001
ADAM_B1 = 0.9
ADAM_B2 = 0.999
ADAM_EPS = 1e-08
ADAM_WD = 0.01
ADAM_STEP = 10

MIB = 2 ** 20
MESH_ID = pl.DeviceIdType.MESH
HBM = pl.BlockSpec(memory_space=pl.ANY)


def _params(sem, vmem_mib):
    return pltpu.CompilerParams(dimension_semantics=sem, vmem_limit_bytes=vmem_mib * MIB)


def _dot(a, b):
    return jnp.dot(a, b, preferred_element_type=F32)


def _dot_nt(a, b):
    return lax.dot_general(a, b, (((1,), (1,)), ((), ())), preferred_element_type=F32)


def _dot_tn(a, b):
    return lax.dot_general(a, b, (((0,), (0,)), ((), ())), preferred_element_type=F32)


def _rstd(x):
    return lax.rsqrt(jnp.mean(x * x, axis=-1, keepdims=True) + EPS)


def _rms_bwd(dy, x, g):
    r = _rstd(x)
    n = x * r
    dn = dy * g
    dx = r * (dn - n * jnp.mean(dn * n, axis=-1, keepdims=True))
    return dx, dy * n


def _colsum(v):
    return jnp.sum(v, axis=0, keepdims=True)


_G0 = math.sqrt(2.0 / math.pi)
_G1 = 0.044715


def _gelu(x):
    t = jnp.tanh(_G0 * (x + _G1 * x * x * x))
    return 0.5 * x * (1.0 + t), t


def _gelu_grad(x, t):
    return 0.5 * (1.0 + t) + 0.5 * x * (1.0 - t * t) * (_G0 * (1.0 + 3.0 * _G1 * x * x))


def _sigmoid(x):
    return 0.5 * jnp.tanh(0.5 * x) + 0.5


def _row(i):
    return (i, 0)


def _fixed(i):
    return (0, 0)


def _acc_init(step, *refs):
    @pl.when(step == 0)
    def _():
        for r in refs:
            r[...] = jnp.zeros_like(r)


FLIPS = [(dx, dy, dc) for dx in (0, 1) for dy in (0, 1) for dc in (0, 1)][1:]
DMA_SEMS = pltpu.SemaphoreType.DMA


def _mesh_pos():
    return lax.axis_index("x"), lax.axis_index("y"), lax.axis_index("c")


def _remote(src, dst, sems, n, to):
    return pltpu.make_async_remote_copy(src_ref=src, dst_ref=dst, send_sem=sems[0].at[n], recv_sem=sems[1].at[n],
                                        device_id=to, device_id_type=MESH_ID)


class _Scatter:
    def __init__(self, items, sems):
        x, y, c = _mesh_pos()
        me = 4 * x + 2 * y + c
        self.local, self.sends, self.arrivals = [], [], []
        for i, (src, dst, slabbed) in enumerate(items):
            self.local.append(pltpu.make_async_copy(src.at[me] if slabbed else src, dst.at[me], sems[2].at[i]))
            for k, (dx, dy, dc) in enumerate(FLIPS):
                to = (1 - x if dx else x, 1 - y if dy else y, 1 - c if dc else c)
                peer = 4 * to[0] + 2 * to[1] + to[2]
                out = src.at[peer] if slabbed else src
                self.sends.append(_remote(out, dst.at[me], sems, 7 * i + k, to))
                self.arrivals.append(_remote(out, dst.at[peer], sems, 7 * i + k, to))

    def start(self):
        for cp in self.local + self.sends:
            cp.start()

    def wait(self):
        for cp in self.arrivals:
            cp.wait_recv()
        for cp in self.sends:
            cp.wait_send()
        for cp in self.local:
            cp.wait()


def _scatter_sems(n):
    return [DMA_SEMS((7 * n,)), DMA_SEMS((7 * n,)), DMA_SEMS((n,))]


class _Gather:
    def __init__(self, items, sems):
        x, y, c = _mesh_pos()
        me, sibling = (x, y, c), (x, y, 1 - c)
        chips = [(1 - x, y), (x, 1 - y), (1 - x, 1 - y)]
        self.first, self.passed, self.from_chips, self.rest, self.local = [], [], [], [], []
        for i, (src, dst) in enumerate(items):
            def slot(p, dst=dst):
                return dst.at[4 * p[0] + 2 * p[1] + p[2]]

            def copy(k, block, to, own=False, i=i, src=src, slot=slot):
                return _remote(src if own else slot(block), slot(block), sems, 7 * i + k, to)

            self.local.append(pltpu.make_async_copy(src, slot(me), sems[2].at[i]))
            self.first.append(copy(0, me, sibling, own=True))
            self.first += [copy(1 + j, me, (*chip, c), own=True) for j, chip in enumerate(chips)]
            self.passed += [copy(4 + j, (*chip, c), sibling) for j, chip in enumerate(chips)]
            self.from_chips += [copy(1 + j, (*chip, c), me) for j, chip in enumerate(chips)]
            self.rest.append(copy(0, sibling, me))
            self.rest += [copy(4 + j, (*chip, 1 - c), me) for j, chip in enumerate(chips)]

    def start(self):
        for cp in self.local + self.first:
            cp.start()

    def forward(self):
        for arrived, onward in zip(self.from_chips, self.passed):
            arrived.wait_recv()
            onward.start()

    def finish(self):
        for cp in self.rest:
            cp.wait_recv()
        for cp in self.first + self.passed:
            cp.wait_send()
        for cp in self.local:
            cp.wait()


def _scatter_call(srcs, slabbed, name):
    n = len(srcs)

    def body(*refs):
        sc = _Scatter(list(zip(refs[:n], refs[n:2 * n], slabbed)), refs[2 * n:])
        sc.start()
        sc.wait()

    shapes = [SDS(s.shape if sl else (N_DEV,) + s.shape, s.dtype) for s, sl in zip(srcs, slabbed)]
    return pl.pallas_call(body, out_shape=shapes, in_specs=[HBM] * n, out_specs=[HBM] * n,
                          scratch_shapes=_scatter_sems(n), name=name)(*srcs)


SEM = pl.BlockSpec(memory_space=pltpu.SEMAPHORE)
N_PEERS = len(FLIPS)


def _slab_copies(src_ref, land_ref, send_sems, recv_sems):
    x, y, c = _mesh_pos()
    me = 4 * x + 2 * y + c
    copies = []
    for k, (dx, dy, dc) in enumerate(FLIPS):
        to = (1 - x if dx else x, 1 - y if dy else y, 1 - c if dc else c)
        peer = 4 * to[0] + 2 * to[1] + to[2]
        sems = (send_sems, recv_sems)
        copies.append((_remote(src_ref.at[peer], land_ref.at[me], sems, k, to),
                       _remote(src_ref.at[peer], land_ref.at[peer], sems, k, to)))
    return copies


def _scatter_begin(src, name):
    def body(src_ref, land_ref, send_sems, recv_sems, src_thru, land_thru, token):
        for send, _ in _slab_copies(src_ref, land_ref, send_sems, recv_sems):
            send.start()
        token[...] = jnp.zeros_like(token)

    landing = lax.empty(src.shape, src.dtype)
    return pl.pallas_call(
        body, name=name,
        out_shape=(pltpu.SemaphoreType.DMA((N_PEERS,)), pltpu.SemaphoreType.DMA((N_PEERS,)),
                   pltpu.HBM(src.shape, src.dtype), pltpu.HBM(src.shape, src.dtype),
                   jax.ShapeDtypeStruct((SUBLANES, LANES), F32)),
        in_specs=(HBM, HBM), out_specs=(SEM, SEM, HBM, HBM, pl.BlockSpec(memory_space=pltpu.VMEM)),
        input_output_aliases={0: 2, 1: 3},
        compiler_params=pltpu.CompilerParams(has_side_effects=pltpu.SideEffectType.DATAFLOW_SIDE_EFFECTING))(
            pltpu.with_memory_space_constraint(src, pltpu.HBM), pltpu.with_memory_space_constraint(landing, pltpu.HBM))


def _scatter_end(send_sems, recv_sems, src_thru, land_thru, after, name):
    def body(src_ref, land_ref, send_sems, recv_sems, after_ref, src_dead, land_out):
        for send, arrival in _slab_copies(src_ref, land_ref, send_sems, recv_sems):
            send.wait_send()
            arrival.wait_recv()

    return pl.pallas_call(
        body, name=name,
        out_shape=(pltpu.HBM(src_thru.shape, src_thru.dtype), pltpu.HBM(land_thru.shape, land_thru.dtype)),
        in_specs=(HBM, HBM, SEM, SEM, HBM), out_specs=(HBM, HBM), input_output_aliases={0: 0, 1: 1},
        compiler_params=pltpu.CompilerParams(has_side_effects=pltpu.SideEffectType.DATAFLOW_SIDE_EFFECTING))(
            src_thru, land_thru, send_sems, recv_sems, after)


N_CHIPS = N_DEV // 2
CHIP_COLS = PROJ // N_CHIPS


def _in_proj(x, g1, shard, tm=512):
    T = x.shape[0]
    steps = T // tm
    rows = shard.shape[0]
    cx, cy = lax.axis_index("x"), lax.axis_index("y")
    order = jnp.stack([2 * cx + cy, 2 * (1 - cx) + cy, 2 * cx + (1 - cy), 2 * (1 - cx) + (1 - cy)]).astype(jnp.int32)

    def body(order_ref, x_ref, g_ref, shard_ref, proj_ref, a_ref, w_all_ref, send_sems, recv_sems, local_sems,
             a_keep, w_blk, w_sems):
        j, i = pl.program_id(0), pl.program_id(1)
        gather = _Gather([(shard_ref, w_all_ref)], (send_sems, recv_sems, local_sems))

        @pl.when((j == 0) & (i == 0))
        def _():
            gather.start()
            gather.local[0].wait()
            gather.rest[0].wait_recv()

        for k in range(1, N_CHIPS):
            @pl.when((j == k) & (i == 0))
            def _(k=k):
                gather.from_chips[k - 1].wait_recv()
                gather.passed[k - 1].start()
                gather.rest[k].wait_recv()

        @pl.when(i == 0)
        def _():
            chip = order_ref[j]
            loads = [pltpu.make_async_copy(w_all_ref.at[2 * chip + half], w_blk.at[pl.ds(half * rows, rows), :],
                                           w_sems.at[half]) for half in range(2)]
            for cp in loads:
                cp.start()
            for cp in loads:
                cp.wait()

        tile = pl.ds(pl.multiple_of(i * tm, tm), tm)

        @pl.when(j == 0)
        def _():
            xv = x_ref[...]
            a = (xv * _rstd(xv) * g_ref[...]).astype(BF)
            a_keep[tile, :] = a
            a_ref[...] = a

        proj_ref[...] = _dot_nt(a_keep[tile, :], w_blk[...])

        @pl.when((j == N_CHIPS - 1) & (i == steps - 1))
        def _():
            for cp in gather.first + gather.passed:
                cp.wait_send()

    moving = lambda j, i, order: (jnp.where(j == 0, i, steps - 1), 0)
    grid_spec = pltpu.PrefetchScalarGridSpec(
        num_scalar_prefetch=1, grid=(N_CHIPS, steps),
        in_specs=[pl.BlockSpec((tm, D_MODEL), moving), pl.BlockSpec((1, D_MODEL), lambda j, i, order: (0, 0)), HBM],
        out_specs=[pl.BlockSpec((tm, CHIP_COLS), lambda j, i, order: (i, order[j])),
                   pl.BlockSpec((tm, D_MODEL), moving), HBM],
        scratch_shapes=_scatter_sems(1) + [pltpu.VMEM((T, D_MODEL), BF), pltpu.VMEM((2 * rows, D_MODEL), BF),
                                           DMA_SEMS((2,))])
    return pl.pallas_call(
        body, grid_spec=grid_spec,
        out_shape=[SDS((T, PROJ), F32), SDS((T, D_MODEL), BF), SDS((N_DEV,) + shard.shape, shard.dtype)],
        name="in_proj", compiler_params=_params(("arbitrary", "arbitrary"), 48))(order, x, g1, shard)


ATTN_GROUP = 16


def _attn_bias(sl_ref, bias_ref):
    qi = lax.broadcasted_iota(jnp.int32, (QBLK, QBLK), 0)
    kj = lax.broadcasted_iota(jnp.int32, (QBLK, QBLK), 1)
    step = qi - kj
    for di, d in enumerate(DILATIONS):
        for j in range(2):
            sl = sl_ref[0, j:j + 1, :]
            cur = jnp.where(step >= 0, -sl * (step * d).astype(F32), NEG)
            prev = jnp.where(step <= 0, -sl * ((step + QBLK) * d).astype(F32), NEG)
            rows = slice(j * QBLK, (j + 1) * QBLK)
            bias_ref[di * 2, rows, :QBLK] = prev
            bias_ref[di * 2, rows, QBLK:] = cur
            bias_ref[di * 2 + 1, rows, :QBLK] = jnp.full((QBLK, QBLK), NEG, F32)
            bias_ref[di * 2 + 1, rows, QBLK:] = cur


def _stack_heads(x, lo):
    return jnp.concatenate([jnp.where(lo, x, 0.0), jnp.where(lo, 0.0, x)], axis=0).astype(BF)


def _unstack_heads(x, lo):
    return jnp.where(lo, x[:QBLK], x[QBLK:])


def _attn_rows(start, d, blocks=1):
    if d == 1:
        return pl.ds(pl.multiple_of(start, QBLK), blocks * QBLK)
    return pl.ds(start, blocks * QBLK, stride=d)


def _attn_group_index(i, nblk, d, group):
    per = nblk // d // group
    r = i // per
    n0 = (i % per) * group
    start = r + (d * QBLK) * n0
    pstart = jnp.maximum(start - d * QBLK, r)
    return start, pstart, n0 == 0


def _attn_plan(nblk, d):
    group = min(ATTN_GROUP, nblk // d)
    return group, max(1, min(ATTN_GROUP // group, d))


def _attn_fwd(proj, slopes, to_gather):
    T = proj.shape[0]
    nblk = T // QBLK
    n_g = len(to_gather)

    def body(q_ref, k_ref, v_ref, sl_ref, *rest):
        srcs, (o_ref, m_ref), dsts = rest[:n_g], rest[n_g:n_g + 2], rest[n_g + 2:2 * n_g + 2]
        sems, (l_ref, bias_ref) = rest[2 * n_g + 2:2 * n_g + 5], rest[2 * n_g + 5:]
        h = pl.program_id(0)

        @pl.when(h == 0)
        def _():
            _Gather(list(zip(srcs, dsts)), sems).start()

        @pl.when(h == pl.num_programs(0) - 1)
        def _():
            _Gather(list(zip(srcs, dsts)), sems).forward()

        _attn_bias(sl_ref, bias_ref)
        lo = lax.broadcasted_iota(jnp.int32, (1, LANES), 1) < HEAD_DIM

        order = list(enumerate(DILATIONS))[::-1]
        for di, d in order:
            group, segs = _attn_plan(nblk, d)
            fresh, last = di == order[0][0], di == order[-1][0]

            def step(i, carry, segs=segs, **kw):
                for s in range(segs):
                    segment(i * segs + s, **kw)
                return carry

            def segment(i, d=d, di=di, group=group, fresh=fresh, last=last):
                start, pstart, first = _attn_group_index(i, nblk, d, group)
                prows = _attn_rows(pstart, d)
                k_prev, v_prev = k_ref[prows, :].astype(BF), v_ref[prows, :].astype(BF)
                for b in range(group):
                    out = _attn_rows(start + b * (d * QBLK), d)
                    k_own, v_own = k_ref[out, :].astype(BF), v_ref[out, :].astype(BF)
                    k2, v2 = jnp.concatenate([k_prev, k_own], axis=0), jnp.concatenate([v_prev, v_own], axis=0)
                    k_prev, v_prev = k_own, v_own
                    bias = bias_ref[di * 2 + first.astype(jnp.int32)] if b == 0 else bias_ref[di * 2]
                    s = _dot_nt(_stack_heads(q_ref[out, :] * (HEAD_DIM ** -0.5), lo), k2) + bias
                    m = jnp.max(s, axis=-1, keepdims=True)
                    pr = jnp.exp(s - m)
                    m_b = _unstack_heads(m, lo)
                    l_b = _unstack_heads(jnp.sum(pr, axis=-1, keepdims=True), lo)
                    o_b = _unstack_heads(_dot(pr.astype(BF), v2), lo)
                    if fresh:
                        m_ref[out, :] = m_b
                        l_ref[out, :] = l_b
                        o_ref[out, :] = o_b
                        continue
                    m_o = m_ref[out, :]
                    m_n = jnp.maximum(m_o, m_b)
                    wa, wb = jnp.exp(m_o - m_n), jnp.exp(m_b - m_n)
                    l_n = wa * l_ref[out, :] + wb * l_b
                    o_n = wa * o_ref[out, :] + wb * o_b
                    if last:
                        m_ref[out, :] = m_n + jnp.log(l_n)
                        o_ref[out, :] = o_n / l_n
                    else:
                        m_ref[out, :] = m_n
                        l_ref[out, :] = l_n
                        o_ref[out, :] = o_n

            lax.fori_loop(0, nblk // (group * segs), step, 0)

        @pl.when(h == pl.num_programs(0) - 1)
        def _():
            _Gather(list(zip(srcs, dsts)), sems).finish()

    col = lambda base: pl.BlockSpec((T, LANES), lambda h: (0, base + h))
    tok = pl.BlockSpec((T, LANES), lambda h: (0, h))
    outs = pl.pallas_call(
        body, grid=(4,),
        in_specs=[col(0), col(4), col(8), pl.BlockSpec((1, 8, LANES), lambda h: (h, 0, 0))] + [HBM] * n_g,
        out_specs=[tok, tok] + [HBM] * n_g,
        out_shape=[SDS((T, ATTN_W), F32), SDS((T, ATTN_W), F32)]
        + [SDS((N_DEV,) + g.shape, g.dtype) for g in to_gather],
        scratch_shapes=_scatter_sems(n_g) + [pltpu.VMEM((T, LANES), F32), pltpu.VMEM((6, 2 * QBLK, 2 * QBLK), F32)],
        name="attn_fwd", compiler_params=_params(("arbitrary",), 56))(proj, proj, proj, slopes, *to_gather)
    return outs[0], outs[1], outs[2:]


def _sgu_norm(zv, ln_g, ln_b):
    gz, tz = _gelu(zv)
    mu = jnp.mean(gz, axis=-1, keepdims=True)
    xc = gz - mu
    rs = lax.rsqrt(jnp.mean(xc * xc, axis=-1, keepdims=True) + EPS)
    xhat = xc * rs
    return xhat * ln_g + ln_b, xhat, rs, tz


def _causal(w):
    i = lax.broadcasted_iota(jnp.int32, (CHUNK, CHUNK), 0)
    j = lax.broadcasted_iota(jnp.int32, (CHUNK, CHUNK), 1)
    return jnp.where(i >= j, w, 0.0)


def _sgu_fwd(proj, ln_g, ln_b, w_s, b_st, tm=512):
    T = proj.shape[0]

    def body(u_ref, z_ref, g_ref, b_ref, w_ref, bs_ref, out_ref):
        for g in range(N_GROUPS):
            wm = _causal(w_ref[g]).astype(BF)
            cols = slice(g * GROUP_DIM, (g + 1) * GROUP_DIM)
            for c in range(tm // CHUNK):
                rows = slice(c * CHUNK, (c + 1) * CHUNK)
                zn, _, _, _ = _sgu_norm(z_ref[rows, cols], g_ref[...], b_ref[...])
                mixed = _dot(wm, zn.astype(BF)) + bs_ref[:, g:g + 1]
                gu, _ = _gelu(u_ref[rows, cols])
                out_ref[rows, cols] = gu * mixed

    return pl.pallas_call(
        body, grid=(T // tm,),
        in_specs=[pl.BlockSpec((tm, SGU_W), lambda i: (i, 3)), pl.BlockSpec((tm, SGU_W), lambda i: (i, 4)),
                  pl.BlockSpec((1, GROUP_DIM), _fixed), pl.BlockSpec((1, GROUP_DIM), _fixed),
                  pl.BlockSpec((N_GROUPS, CHUNK, CHUNK), lambda i: (0, 0, 0)), pl.BlockSpec((CHUNK, LANES), _fixed)],
        out_specs=pl.BlockSpec((tm, SGU_W), _row),
        out_shape=SDS((T, SGU_W), F32),
        name="sgu_fwd", compiler_params=_params(("arbitrary",), 32))(proj, proj, ln_g, ln_b, w_s, b_st)


def _out_proj(attn, sgu, x, g_a, g_s, w_out, g_pm, g_pf, tm=512):
    T = x.shape[0]

    def body(a_ref, s_ref, x_ref, ga_ref, gs_ref, w_ref, gpm_ref, gpf_ref, grp_ref, mixed_ref, h1_ref, f_ref):
        av, sv = a_ref[...], s_ref[...]
        an = (av * _rstd(av) * ga_ref[...]).astype(BF)
        sn = (sv * _rstd(sv) * gs_ref[...]).astype(BF)
        grp_ref[:, :ATTN_W] = an
        grp_ref[:, ATTN_W:] = sn
        mixed = _dot(an, w_ref[:ATTN_W, :]) + _dot(sn, w_ref[ATTN_W:, :])
        mixed_ref[...] = mixed
        h1 = x_ref[...] + mixed * _rstd(mixed) * gpm_ref[...]
        h1_ref[...] = h1
        f_ref[...] = (h1 * _rstd(h1) * gpf_ref[...]).astype(BF)

    tok = lambda w: pl.BlockSpec((tm, w), _row)
    vec = lambda w: pl.BlockSpec((1, w), _fixed)
    return pl.pallas_call(
        body, grid=(T // tm,),
        in_specs=[tok(ATTN_W), tok(SGU_W), tok(D_MODEL), vec(ATTN_W), vec(SGU_W),
                  pl.BlockSpec((D_MODEL, D_MODEL), _fixed), vec(D_MODEL), vec(D_MODEL)],
        out_specs=[tok(D_MODEL)] * 4,
        out_shape=[SDS((T, D_MODEL), BF), SDS((T, D_MODEL), F32), SDS((T, D_MODEL), F32), SDS((T, D_MODEL), BF)],
        name="out_proj", compiler_params=_params(("arbitrary",), 48))(attn, sgu, x, g_a, g_s, w_out, g_pm, g_pf)


FF_TILE = 1408
FF_TILES = D_FF // FF_TILE
FF_CHUNK = 256


def _gate_up(f, w_gu_t, tm=512):
    T = f.shape[0]
    tn = FF_TILE

    def body(f_ref, wg_ref, wu_ref, g_ref, u_ref, act_ref):
        fv = f_ref[...]
        g = _dot_nt(fv, wg_ref[...])
        u = _dot_nt(fv, wu_ref[...])
        g_ref[...] = g.astype(BF)
        u_ref[...] = u.astype(BF)
        act_ref[...] = (g * _sigmoid(g) * u).astype(BF)

    ospec = pl.BlockSpec((tm, tn), lambda j, i: (i, j))
    return pl.pallas_call(
        body, grid=(FF_TILES, T // tm),
        in_specs=[pl.BlockSpec((tm, D_MODEL), lambda j, i: (i, 0)), pl.BlockSpec((tn, D_MODEL), lambda j, i: (j, 0)),
                  pl.BlockSpec((tn, D_MODEL), lambda j, i: (j + FF_TILES, 0))],
        out_specs=[ospec] * 3, out_shape=[SDS((T, D_FF), BF)] * 3,
        name="gate_up", compiler_params=_params(("arbitrary", "arbitrary"), 40))(f, w_gu_t, w_gu_t)


def _down_proj(act, w_down, h1, g_pff, tm=512):
    T = act.shape[0]

    def body(a_ref, w_ref, h1_ref, g_ref, y_ref, h2_ref):
        y = _dot(a_ref[...], w_ref[...])
        y_ref[...] = y
        h2_ref[...] = h1_ref[...] + y * _rstd(y) * g_ref[...]

    return pl.pallas_call(
        body, grid=(T // tm,),
        in_specs=[pl.BlockSpec((tm, D_FF), _row), pl.BlockSpec((D_FF, D_MODEL), _fixed),
                  pl.BlockSpec((tm, D_MODEL), _row), pl.BlockSpec((1, D_MODEL), _fixed)],
        out_specs=[pl.BlockSpec((tm, D_MODEL), _row)] * 2,
        out_shape=[SDS((T, D_MODEL), F32)] * 2,
        name="down_proj", compiler_params=_params(("arbitrary",), 48))(act, w_down, h1, g_pff)


def _pe_loss_and_bwd(h2, p, target, y, w_peg, b_peg, w_pep_t, g_pff, tm=512):
    T = h2.shape[0]

    def body(h2_ref, p_ref, t_ref, y_ref, wg_ref, b_ref, wp_ref, g_ref,
             dh2_ref, dy_ref, gpeg_ref, gpep_ref, loss_ref, db_ref, dg_ref):
        _acc_init(pl.program_id(0), gpeg_ref, gpep_ref, loss_ref, db_ref, dg_ref)
        h2v = h2_ref[...]
        h2b = h2v.astype(BF)
        pb = p_ref[...].astype(BF)
        gate = _sigmoid(_dot(h2b, wg_ref[...]) + b_ref[...])
        pp = _dot_nt(pb, wp_ref[...])
        diff = h2v + gate * pp - t_ref[...]
        loss_ref[...] += _colsum(diff * diff)
        dh3 = diff * (1.0 / D_MODEL)
        dpre = dh3 * pp * (gate * (1.0 - gate))
        dpre_b = dpre.astype(BF)
        db_ref[...] += _colsum(dpre)
        gpeg_ref[...] += _dot_tn(h2b, dpre_b)
        gpep_ref[...] += _dot_tn((dh3 * gate).astype(BF), pb)
        dh2 = dh3 + _dot_nt(dpre_b, wg_ref[...])
        dh2_ref[...] = dh2
        dy, dg = _rms_bwd(dh2, y_ref[...], g_ref[...])
        dy_ref[...] = dy.astype(BF)
        dg_ref[...] += _colsum(dg)

    tok = lambda w: pl.BlockSpec((tm, w), _row)
    vec = pl.BlockSpec((1, D_MODEL), _fixed)
    wg = pl.BlockSpec((D_MODEL, D_MODEL), _fixed)
    wp = pl.BlockSpec((D_MODEL, PLE_DIM), _fixed)
    return pl.pallas_call(
        body, grid=(T // tm,),
        in_specs=[tok(D_MODEL), tok(PLE_DIM), tok(D_MODEL), tok(D_MODEL), wg, vec, wp, vec],
        out_specs=[tok(D_MODEL), tok(D_MODEL), wg, wp, vec, vec, vec],
        out_shape=[SDS((T, D_MODEL), F32), SDS((T, D_MODEL), BF), SDS((D_MODEL, D_MODEL), F32),
                   SDS((D_MODEL, PLE_DIM), F32)] + [SDS((1, D_MODEL), F32)] * 3,
        name="pe_loss_and_bwd", compiler_params=_params(("arbitrary",), 56))(
            h2, p, target, y, w_peg, b_peg, w_pep_t, g_pff)


def _weight_grad(a, dy, name, into=None, row_tile=0, rows=None, tk=1024):
    n = dy.shape[1]
    tn = min(n, 1024)
    T, ka = a.shape
    tka = FF_TILE if ka == D_FF else min(ka, 1024)
    rows = ka if rows is None else rows

    def body(a_ref, dy_ref, *rest):
        out_ref = rest[-1]
        _acc_init(pl.program_id(2), out_ref)
        out_ref[...] += _dot_tn(a_ref[...].astype(BF), dy_ref[...].astype(BF))

    carried = [] if into is None else [into]
    return pl.pallas_call(
        body, grid=(ka // tka, n // tn, T // tk),
        in_specs=[pl.BlockSpec((tk, tka), lambda i, j, k: (k, i)), pl.BlockSpec((tk, tn), lambda i, j, k: (k, j))]
        + [HBM] * len(carried),
        out_specs=pl.BlockSpec((tka, tn), lambda i, j, k: (i + row_tile, j)),
        out_shape=SDS((rows, n), F32), input_output_aliases={2: 0} if carried else {},
        name="grad_" + name, compiler_params=_params(("arbitrary",) * 3, 40))(a, dy, *carried)


def _grad_w_in(dparts, a, tk=1024):
    T = a.shape[0]

    def body(*refs):
        d_refs, a_ref, out_ref, acc_ref = refs[:len(dparts)], refs[-3], refs[-2], refs[-1]
        k = pl.program_id(0)
        _acc_init(k, acc_ref)
        cols = [r[part].astype(BF) for r in d_refs for part in range(r.shape[0])]
        acc_ref[...] += _dot_tn(jnp.concatenate(cols, axis=1), a_ref[...])

        @pl.when(k == pl.num_programs(0) - 1)
        def _():
            out_ref[...] = acc_ref[...].astype(BF)

    return pl.pallas_call(
        body, grid=(T // tk,),
        in_specs=[pl.BlockSpec((d.shape[0], tk, d.shape[2]), lambda k: (0, k, 0)) for d in dparts]
        + [pl.BlockSpec((tk, D_MODEL), lambda k: (k, 0))],
        out_specs=pl.BlockSpec((PROJ, D_MODEL), lambda k: (0, 0)),
        out_shape=SDS((PROJ, D_MODEL), BF), scratch_shapes=[pltpu.VMEM((PROJ, D_MODEL), F32)],
        name="grad_w_in", compiler_params=_params(("arbitrary",), 48))(*dparts, a)


def _down_bwd(dy, w_down, g, u, to_send, tm=512):
    T = dy.shape[0]
    n_s = len(to_send)

    def body(dy_ref, w_ref, g_ref, u_ref, *rest):
        srcs, (dg_ref, du_ref), dsts, sems = rest[:n_s], rest[n_s:n_s + 2], rest[n_s + 2:2 * n_s + 2], rest[2 * n_s + 2:]
        i = pl.program_id(0)
        items = list(zip(srcs, dsts, [True] * n_s))

        @pl.when(i == 0)
        def _():
            _Scatter(items, sems).start()

        dyv = dy_ref[...]
        for c in range(D_FF // FF_CHUNK):
            cols = slice(c * FF_CHUNK, (c + 1) * FF_CHUNK)
            dact = _dot_nt(dyv, w_ref[cols, :]).astype(BF)
            gv, uv = g_ref[:, cols], u_ref[:, cols]
            s = _sigmoid(gv)
            ds = dact * s
            dg_ref[:, cols] = ds * uv * (1.0 + gv * (1.0 - s))
            du_ref[:, cols] = ds * gv

        @pl.when(i == pl.num_programs(0) - 1)
        def _():
            _Scatter(items, sems).wait()

    tile = pl.BlockSpec((tm, D_FF), _row)
    outs = pl.pallas_call(
        body, grid=(T // tm,),
        in_specs=[pl.BlockSpec((tm, D_MODEL), _row),
                  pl.BlockSpec((D_FF, D_MODEL), _fixed, pipeline_mode=pl.Buffered(1)), tile, tile] + [HBM] * n_s,
        out_specs=[tile, tile] + [HBM] * n_s,
        out_shape=[SDS((T, D_FF), BF)] * 2 + [SDS(s.shape, s.dtype) for s in to_send],
        scratch_shapes=_scatter_sems(n_s),
        name="down_bwd", compiler_params=_params(("arbitrary",), 48))(dy, w_down, g, u, *to_send)
    return outs[0], outs[1], outs[2:]


def _ffn_in_bwd(dg, du, w_gu_t, h1, dh2, mixed, g_pf, g_pm, to_send, tm=512):
    T = h1.shape[0]
    n_s = len(to_send)

    def body(dg_ref, du_ref, wg_ref, wu_ref, h1_ref, dh2_ref, mx_ref, gpf_ref, gpm_ref, *rest):
        srcs, outs, dsts, sems = rest[:n_s], rest[n_s:n_s + 4], rest[n_s + 4:2 * n_s + 4], rest[2 * n_s + 4:]
        dh1_ref, dmx_ref, dgpf_ref, dgpm_ref = outs
        i = pl.program_id(0)
        items = list(zip(srcs, dsts, [True] * n_s))
        _acc_init(i, dgpf_ref, dgpm_ref)

        @pl.when(i == 0)
        def _():
            _Scatter(items, sems).start()

        df = _dot(dg_ref[...], wg_ref[...]) + _dot(du_ref[...], wu_ref[...])
        dx, dgf = _rms_bwd(df, h1_ref[...], gpf_ref[...])
        dh1 = dh2_ref[...] + dx
        dh1_ref[...] = dh1
        dmx, dgm = _rms_bwd(dh1, mx_ref[...], gpm_ref[...])
        dmx_ref[...] = dmx.astype(BF)
        dgpf_ref[...] += _colsum(dgf)
        dgpm_ref[...] += _colsum(dgm)

        @pl.when(i == pl.num_programs(0) - 1)
        def _():
            _Scatter(items, sems).wait()

    tok = lambda w: pl.BlockSpec((tm, w), _row)
    vec = pl.BlockSpec((1, D_MODEL), _fixed)
    outs = pl.pallas_call(
        body, grid=(T // tm,),
        in_specs=[tok(D_FF), tok(D_FF), pl.BlockSpec((D_FF, D_MODEL), lambda i: (0, 0), pipeline_mode=pl.Buffered(1)),
                  pl.BlockSpec((D_FF, D_MODEL), lambda i: (1, 0), pipeline_mode=pl.Buffered(1)),
                  tok(D_MODEL), tok(D_MODEL), tok(D_MODEL), vec, vec]
        + [HBM] * n_s,
        out_specs=[tok(D_MODEL), tok(D_MODEL), vec, vec] + [HBM] * n_s,
        out_shape=[SDS((T, D_MODEL), F32), SDS((T, D_MODEL), BF), SDS((1, D_MODEL), F32), SDS((1, D_MODEL), F32)]
        + [SDS(s.shape, s.dtype) for s in to_send],
        scratch_shapes=_scatter_sems(n_s),
        name="ffn_in_bwd", compiler_params=_params(("arbitrary",), 56))(
            dg, du, w_gu_t, w_gu_t, h1, dh2, mixed, g_pf, g_pm, *to_send)
    return outs[0], outs[1], outs[2], outs[3], outs[4:]


STAT_LANES = HEAD_DIM // 2


def _ff_grad_spec(half):
    return pl.BlockSpec((D_FF, D_MODEL), lambda i: (half, 0), pipeline_mode=pl.Buffered(1))


def _out_bwd(dmx, w_out, attn, lse, sgu, g_a, g_s, dg, f, tm=512):
    T = attn.shape[0]

    def body(dm_ref, w_ref, a_ref, l_ref, s_ref, ga_ref, gs_ref, dgate_ref, f_ref,
             da_ref, st_ref, ds_ref, dga_ref, dgs_ref, ggu_ref):
        _acc_init(pl.program_id(0), dga_ref, dgs_ref, ggu_ref)
        ggu_ref[...] += _dot_tn(dgate_ref[...], f_ref[...])
        dgr = _dot_nt(dm_ref[...], w_ref[...])
        av = a_ref[...]
        da, dga = _rms_bwd(dgr[:, :ATTN_W], av, ga_ref[...])
        ds, dgs = _rms_bwd(dgr[:, ATTN_W:], s_ref[...], gs_ref[...])
        da_ref[...] = da
        ds_ref[...] = ds
        dga_ref[...] += _colsum(dga)
        dgs_ref[...] += _colsum(dgs)
        lane = lax.broadcasted_iota(jnp.int32, (1, LANES), 1)
        lo = lane < HEAD_DIM
        first = (lane % HEAD_DIM) < STAT_LANES
        prod = da * av
        for c in range(ATTN_W // LANES):
            cols = slice(c * LANES, (c + 1) * LANES)
            pc = prod[:, cols]
            delta = jnp.where(lo, jnp.sum(jnp.where(lo, pc, 0.0), axis=-1, keepdims=True),
                              jnp.sum(jnp.where(lo, 0.0, pc), axis=-1, keepdims=True))
            st_ref[:, cols] = jnp.where(first, l_ref[:, cols], delta)

    tok = lambda w: pl.BlockSpec((tm, w), _row)
    vec = lambda w: pl.BlockSpec((1, w), _fixed)
    return pl.pallas_call(
        body, grid=(T // tm,),
        in_specs=[tok(D_MODEL), pl.BlockSpec((D_MODEL, D_MODEL), _fixed), tok(ATTN_W), tok(ATTN_W), tok(SGU_W),
                  vec(ATTN_W), vec(SGU_W), tok(D_FF), tok(D_MODEL)],
        out_specs=[tok(ATTN_W), tok(ATTN_W), tok(SGU_W), vec(ATTN_W), vec(SGU_W), _ff_grad_spec(0)],
        out_shape=[SDS((T, ATTN_W), F32), SDS((T, ATTN_W), F32), SDS((T, SGU_W), F32), SDS((1, ATTN_W), F32),
                   SDS((1, SGU_W), F32), SDS((2 * D_FF, D_MODEL), F32)],
        name="out_bwd", compiler_params=_params(("arbitrary",), 56))(dmx, w_out, attn, lse, sgu, g_a, g_s, dg, f)


def _sgu_bwd(proj, dsgu, ln_g, ln_b, w_s, b_st, groups, dmx, d_up, f, g_gu_t, tm=512):
    T = proj.shape[0]

    def body(u_ref, z_ref, ds_ref, g_ref, b_ref, w_ref, bs_ref, grp_ref, dmx_ref, dup_ref, f_ref, _,
             duz_ref, dw_ref, dbs_ref, dlg_ref, dlb_ref, gout_ref, ggu_ref, dbacc_ref):
        du_ref, dz_ref = duz_ref.at[0], duz_ref.at[1]
        step = pl.program_id(0)
        _acc_init(step, dw_ref, dbs_ref, dlg_ref, dlb_ref, gout_ref, ggu_ref, dbacc_ref)
        gout_ref[...] += _dot_tn(grp_ref[...], dmx_ref[...])
        ggu_ref[...] += _dot_tn(dup_ref[...], f_ref[...])
        lng, lnb = g_ref[...], b_ref[...]
        for g in range(N_GROUPS):
            wm = _causal(w_ref[g]).astype(BF)
            cols = slice(g * GROUP_DIM, (g + 1) * GROUP_DIM)
            for c in range(tm // CHUNK):
                rows = slice(c * CHUNK, (c + 1) * CHUNK)
                zv, uv, dout = z_ref[rows, cols], u_ref[rows, cols], ds_ref[rows, cols]
                zn, xhat, rs, tz = _sgu_norm(zv, lng, lnb)
                znb = zn.astype(BF)
                mixed = _dot(wm, znb) + bs_ref[:, g:g + 1]
                gu, tu = _gelu(uv)
                du_ref[rows, cols] = (dout * mixed * _gelu_grad(uv, tu)).astype(BF)
                dmix = dout * gu
                dmb = dmix.astype(BF)
                dw_ref[g] += _causal(_dot_nt(dmb, znb))
                dbacc_ref[g] += dmix
                dzn = _dot_tn(wm, dmb)
                dlg_ref[...] += _colsum(dzn * xhat)
                dlb_ref[...] += _colsum(dzn)
                dxh = dzn * lng
                dgz = rs * (dxh - jnp.mean(dxh, axis=-1, keepdims=True)
                            - xhat * jnp.mean(dxh * xhat, axis=-1, keepdims=True))
                dz_ref[rows, cols] = (dgz * _gelu_grad(zv, tz)).astype(BF)

        @pl.when(step == pl.num_programs(0) - 1)
        def _():
            lane = lax.broadcasted_iota(jnp.int32, (CHUNK, LANES), 1)
            acc = jnp.zeros((CHUNK, LANES), F32)
            for g in range(N_GROUPS):
                acc = jnp.where(lane == g, jnp.sum(dbacc_ref[g], axis=-1, keepdims=True), acc)
            dbs_ref[...] = acc

    tok = pl.BlockSpec((tm, SGU_W), _row)
    vec = pl.BlockSpec((1, GROUP_DIM), _fixed)
    wsp = pl.BlockSpec((N_GROUPS, CHUNK, CHUNK), lambda i: (0, 0, 0))
    sq = pl.BlockSpec((CHUNK, LANES), _fixed)
    wide = pl.BlockSpec((tm, D_MODEL), _row)
    return pl.pallas_call(
        body, grid=(T // tm,),
        in_specs=[pl.BlockSpec((tm, SGU_W), lambda i: (i, 3)), pl.BlockSpec((tm, SGU_W), lambda i: (i, 4)), tok,
                  vec, vec, wsp, sq, wide, wide, pl.BlockSpec((tm, D_FF), _row), wide, HBM],
        out_specs=[pl.BlockSpec((2, tm, SGU_W), lambda i: (0, i, 0)), wsp, sq, vec, vec,
                   pl.BlockSpec((D_MODEL, D_MODEL), _fixed), _ff_grad_spec(1)],
        out_shape=[SDS((2, T, SGU_W), BF), SDS((N_GROUPS, CHUNK, CHUNK), F32),
                   SDS((CHUNK, LANES), F32), SDS((1, GROUP_DIM), F32), SDS((1, GROUP_DIM), F32),
                   SDS((D_MODEL, D_MODEL), F32), SDS((2 * D_FF, D_MODEL), F32)],
        input_output_aliases={11: 6},
        scratch_shapes=[pltpu.VMEM((N_GROUPS, CHUNK, LANES), F32)],
        name="sgu_bwd", compiler_params=_params(("arbitrary",), 56))(
            proj, proj, dsgu, ln_g, ln_b, w_s, b_st, groups, dmx, d_up, f, g_gu_t)


def _attn_bwd(proj, do, stats, slopes, to_send, slabbed):
    T = proj.shape[0]
    nblk = T // QBLK
    n_s = len(to_send)

    def body(q_ref, k_ref, v_ref, do_ref, st_ref, sl_ref, *rest):
        srcs, d_ref, dsts = rest[:n_s], rest[n_s], rest[n_s + 1:2 * n_s + 1]
        sems, bias_ref = rest[2 * n_s + 1:2 * n_s + 4], rest[2 * n_s + 4]
        dq_ref, dk_ref, dv_ref = d_ref.at[0], d_ref.at[1], d_ref.at[2]
        h = pl.program_id(0)
        items = list(zip(srcs, dsts, slabbed))

        @pl.when(h == 0)
        def _():
            _Scatter(items, sems).start()

        _attn_bias(sl_ref, bias_ref)
        lo = lax.broadcasted_iota(jnp.int32, (1, LANES), 1) < HEAD_DIM
        scale = HEAD_DIM ** -0.5
        d_ref[...] = jnp.zeros_like(d_ref)

        for di, d in enumerate(DILATIONS):
            group, segs = _attn_plan(nblk, d)

            def step(i, carry, segs=segs, **kw):
                for s in range(segs):
                    segment(i * segs + s, **kw)
                return carry

            def segment(i, d=d, di=di, group=group):
                start, pstart, first = _attn_group_index(i, nblk, d, group)
                rows, prows = _attn_rows(start, d, group), _attn_rows(pstart, d)
                q = q_ref[rows, :] * scale
                k = jnp.concatenate([k_ref[prows, :], k_ref[rows, :]], axis=0).astype(BF)
                v = jnp.concatenate([v_ref[prows, :], v_ref[rows, :]], axis=0).astype(BF)
                dov = do_ref[rows, :]
                stats = st_ref[rows, :]
                masks = [lo, ~lo]
                qm = [jnp.where(masks[j], q, 0.0).astype(BF) for j in range(2)]
                dom = [jnp.where(masks[j], dov, 0.0).astype(BF) for j in range(2)]
                for b in range(group):
                    qb = slice(b * QBLK, (b + 1) * QBLK)
                    kb = slice(b * QBLK, (b + 2) * QBLK)
                    which = di * 2 + first.astype(jnp.int32) if b == 0 else di * 2
                    dq_parts, prs, dss = [], [], []
                    for j in range(2):
                        bias = bias_ref[which, j * QBLK:(j + 1) * QBLK, :]
                        lj = stats[qb, j * HEAD_DIM:j * HEAD_DIM + 1]
                        delta = stats[qb, j * HEAD_DIM + STAT_LANES:j * HEAD_DIM + STAT_LANES + 1]
                        pr = jnp.exp(_dot_nt(qm[j][qb], k[kb]) + bias - lj)
                        ds = (pr * (_dot_nt(dom[j][qb], v[kb]) - delta)).astype(BF)
                        dq_parts.append(_dot(ds, k[kb]))
                        prs.append(pr.astype(BF))
                        dss.append(ds)
                    dk_b = _dot_tn(jnp.concatenate(dss, axis=0), jnp.concatenate([qm[0][qb], qm[1][qb]], axis=0))
                    dv_b = _dot_tn(jnp.concatenate(prs, axis=0), jnp.concatenate([dom[0][qb], dom[1][qb]], axis=0))
                    own = _attn_rows(start + b * (d * QBLK), d)
                    dq_ref[own, :] += jnp.where(lo, dq_parts[0], dq_parts[1]) * scale
                    if b == 0:
                        dk_ref[prows, :] += dk_b[:QBLK]
                        dv_ref[prows, :] += dv_b[:QBLK]
                        dk_ref[own, :] += dk_b[QBLK:]
                        dv_ref[own, :] += dv_b[QBLK:]
                    else:
                        two = _attn_rows(start + (b - 1) * (d * QBLK), d, 2)
                        dk_ref[two, :] += dk_b
                        dv_ref[two, :] += dv_b

            lax.fori_loop(0, nblk // (group * segs), step, 0)

        @pl.when(h == pl.num_programs(0) - 1)
        def _():
            _Scatter(items, sems).wait()

    col = lambda base: pl.BlockSpec((T, LANES), lambda h: (0, base + h))
    outs = pl.pallas_call(
        body, grid=(4,),
        in_specs=[col(0), col(4), col(8), col(0), col(0), pl.BlockSpec((1, 8, LANES), lambda h: (h, 0, 0))]
        + [HBM] * n_s,
        out_specs=[pl.BlockSpec((3, T, LANES), lambda h: (0, 0, h), pipeline_mode=pl.Buffered(1))] + [HBM] * n_s,
        out_shape=[SDS((3, T, ATTN_W), F32)]
        + [SDS(s.shape if sl else (N_DEV,) + s.shape, s.dtype) for s, sl in zip(to_send, slabbed)],
        scratch_shapes=_scatter_sems(n_s) + [pltpu.VMEM((6, 2 * QBLK, 2 * QBLK), F32)],
        name="attn_bwd", compiler_params=_params(("arbitrary",), 60))(proj, proj, proj, do, stats, slopes, *to_send)
    return outs[0], outs[1:]


def _in_bwd(dparts, w_in_t, x, dh1, g1, tm=512):
    T = x.shape[0]
    n = len(dparts)
    w = ATTN_W

    def body(*refs):
        d_refs, (w_ref, x_ref, dh1_ref, g_ref, dx_ref, dg_ref) = refs[:n], refs[n:]
        _acc_init(pl.program_id(0), dg_ref)
        da = None
        col = 0
        for r in d_refs:
            for part in range(r.shape[0]):
                t = _dot(r[part].astype(BF), w_ref[col * w:(col + 1) * w, :])
                da = t if da is None else da + t
                col += 1
        dx, dg = _rms_bwd(da, x_ref[...], g_ref[...])
        dx_ref[...] = dh1_ref[...] + dx
        dg_ref[...] += _colsum(dg)

    tok = lambda c: pl.BlockSpec((tm, c), _row)
    vec = pl.BlockSpec((1, D_MODEL), _fixed)
    return pl.pallas_call(
        body, grid=(T // tm,),
        in_specs=[pl.BlockSpec((d.shape[0], tm, w), lambda i: (0, i, 0)) for d in dparts]
        + [pl.BlockSpec((PROJ, D_MODEL), _fixed), tok(D_MODEL), tok(D_MODEL), vec],
        out_specs=[tok(D_MODEL), vec],
        out_shape=[SDS((T, D_MODEL), F32), SDS((1, D_MODEL), F32)],
        name="in_bwd", compiler_params=_params(("arbitrary",), 52))(*dparts, w_in_t, x, dh1, g1)


def _sum_parts(p_ref):
    g = p_ref[0].astype(F32)
    for s in range(1, N_DEV):
        g = g + p_ref[s].astype(F32)
    return g


def _adamw_math(g, w, m, v):
    nm = ADAM_B1 * m + (1.0 - ADAM_B1) * g
    nv = ADAM_B2 * v + (1.0 - ADAM_B2) * (g * g)
    m_hat = nm / (1.0 - ADAM_B1 ** ADAM_STEP)
    v_hat = nv / (1.0 - ADAM_B2 ** ADAM_STEP)
    return -ADAM_LR * (m_hat / (jnp.sqrt(v_hat) + ADAM_EPS) + ADAM_WD * w), nm, nv


def _row_tile(rows):
    for t in (256, 176, 128, 80):
        if rows % t == 0:
            return t
    raise ValueError(rows)


def _reduce_adamw(parts, w, m, v, name):
    rows, width = w.shape
    tr = _row_tile(rows)

    def body(p_ref, w_ref, m_ref, v_ref, g_ref, d_ref, nm_ref, nv_ref):
        g = _sum_parts(p_ref)
        g_ref[...] = g
        d_ref[...], nm_ref[...], nv_ref[...] = _adamw_math(g, w_ref[...], m_ref[...], v_ref[...])

    blk = pl.BlockSpec((tr, width), _row)
    return pl.pallas_call(
        body, grid=(rows // tr,),
        in_specs=[pl.BlockSpec((N_DEV, tr, width), lambda i: (0, i, 0)), blk, blk, blk],
        out_specs=[blk] * 4, out_shape=[SDS((rows, width), F32)] * 4,
        name="adamw_" + name, compiler_params=_params(("arbitrary",), 32))(parts, w, m, v)


def _reduce(parts, name):
    _, rows, width = parts.shape
    tr = _row_tile(rows)

    def body(p_ref, g_ref):
        g_ref[...] = _sum_parts(p_ref)

    return pl.pallas_call(
        body, grid=(rows // tr,),
        in_specs=[pl.BlockSpec((N_DEV, tr, width), lambda i: (0, i, 0))],
        out_specs=pl.BlockSpec((tr, width), _row), out_shape=SDS((rows, width), F32),
        name="sum_" + name, compiler_params=_params(("arbitrary",), 32))(parts)


def _adamw(g, w, m, v, name):
    rows, width = w.shape
    tr = _row_tile(rows)

    def body(g_ref, w_ref, m_ref, v_ref, d_ref, nm_ref, nv_ref):
        d_ref[...], nm_ref[...], nv_ref[...] = _adamw_math(g_ref[...], w_ref[...], m_ref[...], v_ref[...])

    blk = pl.BlockSpec((tr, width), _row)
    return pl.pallas_call(
        body, grid=(rows // tr,), in_specs=[blk] * 4, out_specs=[blk] * 3, out_shape=[SDS((rows, width), F32)] * 3,
        name="adamw_" + name, compiler_params=_params(("arbitrary",), 32))(g, w, m, v)


SMALL = ("w_spatial", "ln_pre_mix", "ln_post_mix", "ln_pre_ffn", "ln_post_ffn", "b_pe_gate",
         "attn_out_norm", "sgu_out_norm", "b_spatial", "sgu_ln_g", "sgu_ln_b")
SMALL_GROUPS = ((128, ("w_spatial", "b_spatial", "sgu_ln_g", "sgu_ln_b")),
                (512, ("attn_out_norm", "sgu_out_norm")),
                (1024, ("ln_post_mix", "ln_pre_ffn", "ln_post_ffn", "b_pe_gate")))
SMALL_LATE = "ln_pre_mix"
SMALL_SIZE = dict(w_spatial=N_GROUPS * CHUNK * CHUNK, b_spatial=N_GROUPS * CHUNK, sgu_ln_g=GROUP_DIM, sgu_ln_b=GROUP_DIM,
                  attn_out_norm=ATTN_W, sgu_out_norm=SGU_W, ln_pre_mix=D_MODEL, ln_post_mix=D_MODEL, ln_pre_ffn=D_MODEL,
                  ln_post_ffn=D_MODEL, b_pe_gate=D_MODEL)
SUBLANES = 8
ROW_SHARDED = ("w_out", "w_down", "w_pe_gate")
COL_SHARDED = ("w_in", "w_gate_up", "w_pe_proj")
WEIGHTS = ("ln_pre_mix", "w_in", "sgu_ln_g", "sgu_ln_b", "w_spatial", "b_spatial", "attn_out_norm", "sgu_out_norm",
           "w_out", "ln_post_mix", "ln_pre_ffn", "w_gate_up", "w_down", "ln_post_ffn", "w_pe_gate", "b_pe_gate",
           "w_pe_proj")


def _group_rows(width, names, extra=0):
    rows = sum(SMALL_SIZE[n] // width for n in names) + extra
    return -(-rows // SUBLANES) * SUBLANES


def _pack_small_grads(gs, loss_term):
    packed = []
    for width, names in SMALL_GROUPS:
        rows = [gs[n].reshape(-1, width) for n in names]
        extra = int(width == D_MODEL)
        if extra:
            rows.append(jnp.full((1, width), loss_term, F32))
        used = sum(r.shape[0] for r in rows)
        rows.append(jnp.zeros((_group_rows(width, names, extra) - used, width), F32))
        packed.append(jnp.concatenate(rows, axis=0))
    return packed


def _small_adamw(arrived, arrived_late, w, m, v):
    names = [n for _, ns in SMALL_GROUPS for n in ns] + [SMALL_LATE]
    n_groups = len(SMALL_GROUPS)

    def body(*refs):
        group_refs, late_ref = refs[:n_groups], refs[n_groups]
        state = refs[n_groups + 1:n_groups + 1 + 3 * len(names)]
        outs = refs[n_groups + 1 + 3 * len(names):]
        sums = [_sum_parts(r) for r in group_refs]

        def update(name, g):
            i = names.index(name)
            w_ref, m_ref, v_ref = state[3 * i:3 * i + 3]
            delta, nm, nv = _adamw_math(g, w_ref[...].reshape(g.shape), m_ref[...].reshape(g.shape),
                                        v_ref[...].reshape(g.shape))
            for o_ref, val in zip(outs[4 * i:4 * i + 4], (g, delta, nm, nv)):
                o_ref[...] = val.reshape(o_ref.shape)

        for (width, group), total in zip(SMALL_GROUPS, sums):
            row = 0
            for name in group:
                rows = SMALL_SIZE[name] // width
                update(name, total[row:row + rows, :])
                row += rows
            if width == D_MODEL:
                outs[-1][...] = total[row:row + 1, :LANES]
        update(SMALL_LATE, _sum_parts(late_ref)[:1, :])

    state = [t[n] for n in names for t in (w, m, v)]
    plain = jax.ShapeDtypeStruct
    out_shape = [plain(w[n].shape, F32) for n in names for _ in range(4)] + [plain((1, LANES), F32)]
    outs = pl.pallas_call(body, out_shape=out_shape, name="adamw_small",
                          compiler_params=pltpu.CompilerParams(vmem_limit_bytes=32 * MIB))(*arrived, arrived_late, *state)
    return {n: outs[4 * i:4 * i + 4] for i, n in enumerate(names)}, outs[-1]


def _slabs(full):
    return full.reshape(N_DEV, full.shape[0] // N_DEV, full.shape[1])


def kernel(x, p, ln_pre_mix, w_in, sgu_ln_g, sgu_ln_b, w_spatial, b_spatial, attn_out_norm, sgu_out_norm, w_out, ln_post_mix, ln_pre_ffn, w_gate_up, w_down, ln_post_ffn, w_pe_gate, b_pe_gate, w_pe_proj, loss_target, m_ln_pre_mix, m_w_in, m_sgu_ln_g, m_sgu_ln_b, m_w_spatial, m_b_spatial, m_attn_out_norm, m_sgu_out_norm, m_w_out, m_ln_post_mix, m_ln_pre_ffn, m_w_gate_up, m_w_down, m_ln_post_ffn, m_w_pe_gate, m_b_pe_gate, m_w_pe_proj, v_ln_pre_mix, v_w_in, v_sgu_ln_g, v_sgu_ln_b, v_w_spatial, v_b_spatial, v_attn_out_norm, v_sgu_out_norm, v_w_out, v_ln_post_mix, v_ln_pre_ffn, v_w_gate_up, v_w_down, v_ln_post_ffn, v_w_pe_gate, v_b_pe_gate, v_w_pe_proj):
    given = dict(locals())
    w = {n: given[n] for n in WEIGHTS}
    m = {n: given["m_" + n] for n in WEIGHTS}
    v = {n: given["v_" + n] for n in WEIGHTS}
    xs, ps, target = x[0], p[0, 0], loss_target[0]

    shard = {n: w[n][0].astype(BF) for n in ROW_SHARDED}
    shard.update({n: w[n][0].T.astype(BF) for n in COL_SHARDED})
    sm = {n: w[n][0] for n in SMALL}
    sm = {n: (a.reshape(1, -1) if a.ndim == 1 else a) for n, a in sm.items()}
    slopes = jnp.broadcast_to((2.0 ** -(jnp.arange(8, dtype=F32) + 1.0)).reshape(4, 2, 1), (4, 2, LANES))
    slopes = jnp.concatenate([slopes, jnp.zeros((4, 6, LANES), F32)], axis=1)
    b_st = jnp.pad(sm["b_spatial"].T, ((0, 0), (0, LANES - N_GROUPS)))

    def full(gathered):
        return gathered.reshape(-1, gathered.shape[-1])

    proj, a, w_in_all = _in_proj(xs, sm["ln_pre_mix"], shard["w_in"])
    w_in_t = full(w_in_all)
    later = ("w_out", "w_gate_up", "w_down", "w_pe_gate", "w_pe_proj")
    attn, lse, gathered = _attn_fwd(proj, slopes, [shard[n] for n in later])
    w_out_f, w_gu_t, w_down_f, w_peg_f, w_pep_t = [full(g) for g in gathered]
    sgu = _sgu_fwd(proj, sm["sgu_ln_g"], sm["sgu_ln_b"], sm["w_spatial"], b_st)
    groups, mixed, h1, f = _out_proj(attn, sgu, xs, sm["attn_out_norm"], sm["sgu_out_norm"], w_out_f,
                                     sm["ln_post_mix"], sm["ln_pre_ffn"])
    g, u, act = _gate_up(f, w_gu_t)
    y, h2 = _down_proj(act, w_down_f, h1, sm["ln_post_ffn"])
    dh2, dy, g_peg, g_pep_t, loss_cols, db_peg, d_pff = _pe_loss_and_bwd(
        h2, ps, target, y, w_peg_f, sm["b_pe_gate"], w_pep_t, sm["ln_post_ffn"])
    loss_term = 0.5 * jnp.sum(loss_cols) * (1.0 / D_MODEL)

    arrived = {}
    g_down = _weight_grad(act, dy, "w_down")
    dg, du, (arrived["w_pe_proj"], arrived["w_pe_gate"]) = _down_bwd(dy, w_down_f, g, u, [_slabs(g_pep_t), _slabs(g_peg)])
    dh1, dmx, d_pf, d_pm, (arrived["w_down"],) = _ffn_in_bwd(dg, du, w_gu_t, h1, dh2, mixed, sm["ln_pre_ffn"],
                                                            sm["ln_post_mix"], [_slabs(g_down)])
    dattn, stats, dsgu, d_ga, d_gs, g_gu_t = _out_bwd(dmx, w_out_f, attn, lse, sgu, sm["attn_out_norm"],
                                                      sm["sgu_out_norm"], dg, f)
    duz, d_ws, d_bst, d_lg, d_lb, g_out, g_gu_t = _sgu_bwd(proj, dsgu, sm["sgu_ln_g"], sm["sgu_ln_b"],
                                                           sm["w_spatial"], b_st, groups, dmx, du, f, g_gu_t)
    gs = dict(sgu_ln_g=d_lg, sgu_ln_b=d_lb, w_spatial=d_ws, b_spatial=d_bst[:, :N_GROUPS].T, attn_out_norm=d_ga,
              sgu_out_norm=d_gs, ln_post_mix=d_pm, ln_pre_ffn=d_pf, ln_post_ffn=d_pff, b_pe_gate=db_peg)
    small_grads = _pack_small_grads(gs, loss_term)
    dqkv, (arrived["w_gate_up"], arrived["w_out"], *arrived_small) = _attn_bwd(
        proj, dattn, stats, slopes, [_slabs(g_gu_t), _slabs(g_out), *small_grads],
        [True, True] + [False] * len(small_grads))
    send_sems, recv_sems, slabs, landing, token = _scatter_begin(_slabs(_grad_w_in([dqkv, duz], a)), "w_in_grad_send")
    grad_x, d_g1 = _in_bwd([dqkv, duz], w_in_t, xs, dh1, sm["ln_pre_mix"] + token[:1, :1])
    slabs, landing = _scatter_end(send_sems, recv_sems, slabs, landing, d_g1, "w_in_grad_arrive")
    me = 4 * lax.axis_index("x") + 2 * lax.axis_index("y") + lax.axis_index("c")
    own = lax.dynamic_slice_in_dim(slabs, me, 1, axis=0)
    arrived["w_in"] = lax.dynamic_update_slice_in_dim(landing, own, me, axis=0)
    (arrived_late,) = _scatter_call([jnp.pad(d_g1, ((0, SUBLANES - 1), (0, 0)))], [False], "ln_pre_mix_grad_exchange")

    res = {}
    for n in ROW_SHARDED:
        res[n] = _reduce_adamw(arrived[n], w[n][0], m[n][0], v[n][0], n)
    for n in ("w_in", "w_gate_up"):
        res[n] = [t.T for t in _reduce_adamw(arrived[n], w[n][0].T, m[n][0].T, v[n][0].T, n)]
    for n in ("w_pe_proj",):
        grad = _reduce(arrived[n], n).T
        res[n] = (grad, *_adamw(grad, w[n][0], m[n][0], v[n][0], n))
    small, loss_row = _small_adamw(arrived_small, arrived_late, w, m, v)

    out = []
    for k in range(4):
        out += [res[n][k][None] if n in res else small[n][k] for n in WEIGHTS]
    return (loss_row[0, 0], grad_x[None], *out)
```

```python
import math

import jax
import jax.numpy as jnp
from jax import lax
from jax.experimental import pallas as pl
from jax.experimental.pallas import tpu as pltpu

F32 = jnp.float32
BF = jnp.bfloat16


def SDS(shape, dtype):
    return pltpu.HBM(tuple(shape), dtype)

D_MODEL = 1024
ATTN_W = 512
SGU_W = 512
HEAD_DIM = 64
N_GROUPS = 4
GROUP_DIM = 128
CHUNK = 128
D_FF = 2816
PLE_DIM = 256
PROJ = 3 * ATTN_W + 2 * SGU_W
DILATIONS = (1, 4, 16)
QBLK = 128
EPS = 1e-6
NEG = -1e30
N_DEV = 8
LANES = 128

ADAM_LR = 0.001
ADAM_B1 = 0.9
ADAM_B2 = 0.999
ADAM_EPS = 1e-08
ADAM_WD = 0.01
ADAM_STEP = 10

MIB = 2 ** 20
MESH_ID = pl.DeviceIdType.MESH
HBM = pl.BlockSpec(memory_space=pl.ANY)


def _params(sem, vmem_mib):
    return pltpu.CompilerParams(dimension_semantics=sem, vmem_limit_bytes=vmem_mib * MIB)


def _dot(a, b):
    return jnp.dot(a, b, preferred_element_type=F32)


def _dot_nt(a, b):
    return lax.dot_general(a, b, (((1,), (1,)), ((), ())), preferred_element_type=F32)


def _dot_tn(a, b):
    return lax.dot_general(a, b, (((0,), (0,)), ((), ())), preferred_element_type=F32)


def _rstd(x):
    return lax.rsqrt(jnp.mean(x * x, axis=-1, keepdims=True) + EPS)


def _rms_bwd(dy, x, g):
    r = _rstd(x)
    n = x * r
    dn = dy * g
    dx = r * (dn - n * jnp.mean(dn * n, axis=-1, keepdims=True))
    return dx, dy * n


def _colsum(v):
    return jnp.sum(v, axis=0, keepdims=True)


_G0 = math.sqrt(2.0 / math.pi)
_G1 = 0.044715


def _gelu(x):
    t = jnp.tanh(_G0 * (x + _G1 * x * x * x))
    return 0.5 * x * (1.0 + t), t


def _gelu_grad(x, t):
    return 0.5 * (1.0 + t) + 0.5 * x * (1.0 - t * t) * (_G0 * (1.0 + 3.0 * _G1 * x * x))


def _sigmoid(x):
    return 0.5 * jnp.tanh(0.5 * x) + 0.5


def _row(i):
    return (i, 0)


def _fixed(i):
    return (0, 0)


def _acc_init(step, *refs):
    @pl.when(step == 0)
    def _():
        for r in refs:
            r[...] = jnp.zeros_like(r)


FLIPS = [(dx, dy, dc) for dx in (0, 1) for dy in (0, 1) for dc in (0, 1)][1:]
DMA_SEMS = pltpu.SemaphoreType.DMA


def _mesh_pos():
    return lax.axis_index("x"), lax.axis_index("y"), lax.axis_index("c")


def _remote(src, dst, sems, n, to):
    return pltpu.make_async_remote_copy(src_ref=src, dst_ref=dst, send_sem=sems[0].at[n], recv_sem=sems[1].at[n],
                                        device_id=to, device_id_type=MESH_ID)


class _Scatter:
    def __init__(self, items, sems):
        x, y, c = _mesh_pos()
        me = 4 * x + 2 * y + c
        self.local, self.sends, self.arrivals = [], [], []
        for i, (src, dst, slabbed) in enumerate(items):
            self.local.append(pltpu.make_async_copy(src.at[me] if slabbed else src, dst.at[me], sems[2].at[i]))
            for k, (dx, dy, dc) in enumerate(FLIPS):
                to = (1 - x if dx else x, 1 - y if dy else y, 1 - c if dc else c)
                peer = 4 * to[0] + 2 * to[1] + to[2]
                out = src.at[peer] if slabbed else src
                self.sends.append(_remote(out, dst.at[me], sems, 7 * i + k, to))
                self.arrivals.append(_remote(out, dst.at[peer], sems, 7 * i + k, to))

    def start(self):
        for cp in self.local + self.sends:
            cp.start()

    def wait(self):
        for cp in self.arrivals:
            cp.wait_recv()
        for cp in self.sends:
            cp.wait_send()
        for cp in self.local:
            cp.wait()


def _scatter_sems(n):
    return [DMA_SEMS((7 * n,)), DMA_SEMS((7 * n,)), DMA_SEMS((n,))]


class _Gather:
    def __init__(self, items, sems):
        x, y, c = _mesh_pos()
        me, sibling = (x, y, c), (x, y, 1 - c)
        chips = [(1 - x, y), (x, 1 - y), (1 - x, 1 - y)]
        self.first, self.passed, self.from_chips, self.rest, self.local = [], [], [], [], []
        for i, (src, dst) in enumerate(items):
            def slot(p, dst=dst):
                return dst.at[4 * p[0] + 2 * p[1] + p[2]]

            def copy(k, block, to, own=False, i=i, src=src, slot=slot):
                return _remote(src if own else slot(block), slot(block), sems, 7 * i + k, to)

            self.local.append(pltpu.make_async_copy(src, slot(me), sems[2].at[i]))
            self.first.append(copy(0, me, sibling, own=True))
            self.first += [copy(1 + j, me, (*chip, c), own=True) for j, chip in enumerate(chips)]
            self.passed += [copy(4 + j, (*chip, c), sibling) for j, chip in enumerate(chips)]
            self.from_chips += [copy(1 + j, (*chip, c), me) for j, chip in enumerate(chips)]
            self.rest.append(copy(0, sibling, me))
            self.rest += [copy(4 + j, (*chip, 1 - c), me) for j, chip in enumerate(chips)]

    def start(self):
        for cp in self.local + self.first:
            cp.start()

    def forward(self):
        for arrived, onward in zip(self.from_chips, self.passed):
            arrived.wait_recv()
            onward.start()

    def finish(self):
        for cp in self.rest:
            cp.wait_recv()
        for cp in self.first + self.passed:
            cp.wait_send()
        for cp in self.local:
            cp.wait()


def _all_gather(shard, name):
    def body(x_ref, out_ref, *sems):
        g = _Gather([(x_ref, out_ref)], sems)
        g.start()
        g.forward()
        g.finish()

    return pl.pallas_call(
        body, out_shape=SDS((N_DEV,) + shard.shape, shard.dtype), in_specs=[HBM], out_specs=HBM,
        scratch_shapes=_scatter_sems(1), name=name)(shard)


def _scatter_call(srcs, slabbed, name):
    n = len(srcs)

    def body(*refs):
        sc = _Scatter(list(zip(refs[:n], refs[n:2 * n], slabbed)), refs[2 * n:])
        sc.start()
        sc.wait()

    shapes = [SDS(s.shape if sl else (N_DEV,) + s.shape, s.dtype) for s, sl in zip(srcs, slabbed)]
    return pl.pallas_call(body, out_shape=shapes, in_specs=[HBM] * n, out_specs=[HBM] * n,
                          scratch_shapes=_scatter_sems(n), name=name)(*srcs)


SEM = pl.BlockSpec(memory_space=pltpu.SEMAPHORE)
N_PEERS = len(FLIPS)


def _slab_copies(src_ref, land_ref, send_sems, recv_sems):
    x, y, c = _mesh_pos()
    me = 4 * x + 2 * y + c
    copies = []
    for k, (dx, dy, dc) in enumerate(FLIPS):
        to = (1 - x if dx else x, 1 - y if dy else y, 1 - c if dc else c)
        peer = 4 * to[0] + 2 * to[1] + to[2]
        sems = (send_sems, recv_sems)
        copies.append((_remote(src_ref.at[peer], land_ref.at[me], sems, k, to),
                       _remote(src_ref.at[peer], land_ref.at[peer], sems, k, to)))
    return copies


def _scatter_begin(src, name):
    def body(src_ref, land_ref, send_sems, recv_sems, src_thru, land_thru, token):
        for send, _ in _slab_copies(src_ref, land_ref, send_sems, recv_sems):
            send.start()
        token[...] = jnp.zeros_like(token)

    landing = lax.empty(src.shape, src.dtype)
    return pl.pallas_call(
        body, name=name,
        out_shape=(pltpu.SemaphoreType.DMA((N_PEERS,)), pltpu.SemaphoreType.DMA((N_PEERS,)),
                   pltpu.HBM(src.shape, src.dtype), pltpu.HBM(src.shape, src.dtype),
                   jax.ShapeDtypeStruct((SUBLANES, LANES), F32)),
        in_specs=(HBM, HBM), out_specs=(SEM, SEM, HBM, HBM, pl.BlockSpec(memory_space=pltpu.VMEM)),
        input_output_aliases={0: 2, 1: 3},
        compiler_params=pltpu.CompilerParams(has_side_effects=pltpu.SideEffectType.DATAFLOW_SIDE_EFFECTING))(
            pltpu.with_memory_space_constraint(src, pltpu.HBM), pltpu.with_memory_space_constraint(landing, pltpu.HBM))


def _scatter_end(send_sems, recv_sems, src_thru, land_thru, after, name):
    def body(src_ref, land_ref, send_sems, recv_sems, after_ref, src_dead, land_out):
        for send, arrival in _slab_copies(src_ref, land_ref, send_sems, recv_sems):
            send.wait_send()
            arrival.wait_recv()

    return pl.pallas_call(
        body, name=name,
        out_shape=(pltpu.HBM(src_thru.shape, src_thru.dtype), pltpu.HBM(land_thru.shape, land_thru.dtype)),
        in_specs=(HBM, HBM, SEM, SEM, HBM), out_specs=(HBM, HBM), input_output_aliases={0: 0, 1: 1},
        compiler_params=pltpu.CompilerParams(has_side_effects=pltpu.SideEffectType.DATAFLOW_SIDE_EFFECTING))(
            src_thru, land_thru, send_sems, recv_sems, after)


def _in_proj(x, g1, w_in_t, tm=512):
    T = x.shape[0]

    def body(x_ref, g_ref, w_ref, proj_ref, a_ref):
        xv = x_ref[...]
        a = (xv * _rstd(xv) * g_ref[...]).astype(BF)
        a_ref[...] = a
        proj_ref[...] = _dot_nt(a, w_ref[...])

    return pl.pallas_call(
        body, grid=(T // tm,),
        in_specs=[pl.BlockSpec((tm, D_MODEL), _row), pl.BlockSpec((1, D_MODEL), _fixed),
                  pl.BlockSpec((PROJ, D_MODEL), _fixed)],
        out_specs=[pl.BlockSpec((tm, PROJ), _row), pl.BlockSpec((tm, D_MODEL), _row)],
        out_shape=[SDS((T, PROJ), F32), SDS((T, D_MODEL), BF)],
        name="in_proj", compiler_params=_params(("arbitrary",), 48))(x, g1, w_in_t)


ATTN_GROUP = 16


def _attn_bias(sl_ref, bias_ref):
    qi = lax.broadcasted_iota(jnp.int32, (QBLK, QBLK), 0)
    kj = lax.broadcasted_iota(jnp.int32, (QBLK, QBLK), 1)
    step = qi - kj
    for di, d in enumerate(DILATIONS):
        for j in range(2):
            sl = sl_ref[0, j:j + 1, :]
            cur = jnp.where(step >= 0, -sl * (step * d).astype(F32), NEG)
            prev = jnp.where(step <= 0, -sl * ((step + QBLK) * d).astype(F32), NEG)
            rows = slice(j * QBLK, (j + 1) * QBLK)
            bias_ref[di * 2, rows, :QBLK] = prev
            bias_ref[di * 2, rows, QBLK:] = cur
            bias_ref[di * 2 + 1, rows, :QBLK] = jnp.full((QBLK, QBLK), NEG, F32)
            bias_ref[di * 2 + 1, rows, QBLK:] = cur


def _stack_heads(x, lo):
    return jnp.concatenate([jnp.where(lo, x, 0.0), jnp.where(lo, 0.0, x)], axis=0).astype(BF)


def _unstack_heads(x, lo):
    return jnp.where(lo, x[:QBLK], x[QBLK:])


def _attn_rows(start, d, blocks=1):
    if d == 1:
        return pl.ds(pl.multiple_of(start, QBLK), blocks * QBLK)
    return pl.ds(start, blocks * QBLK, stride=d)


def _attn_group_index(i, nblk, d, group):
    per = nblk // d // group
    r = i // per
    n0 = (i % per) * group
    start = r + (d * QBLK) * n0
    pstart = jnp.maximum(start - d * QBLK, r)
    return start, pstart, n0 == 0


def _attn_plan(nblk, d):
    group = min(ATTN_GROUP, nblk // d)
    return group, max(1, min(ATTN_GROUP // group, d))


def _attn_fwd(proj, slopes, to_gather):
    T = proj.shape[0]
    nblk = T // QBLK
    n_g = len(to_gather)

    def body(q_ref, k_ref, v_ref, sl_ref, *rest):
        srcs, (o_ref, m_ref), dsts = rest[:n_g], rest[n_g:n_g + 2], rest[n_g + 2:2 * n_g + 2]
        sems, (l_ref, bias_ref) = rest[2 * n_g + 2:2 * n_g + 5], rest[2 * n_g + 5:]
        h = pl.program_id(0)

        @pl.when(h == 0)
        def _():
            _Gather(list(zip(srcs, dsts)), sems).start()

        @pl.when(h == pl.num_programs(0) - 1)
        def _():
            _Gather(list(zip(srcs, dsts)), sems).forward()

        _attn_bias(sl_ref, bias_ref)
        lo = lax.broadcasted_iota(jnp.int32, (1, LANES), 1) < HEAD_DIM

        order = list(enumerate(DILATIONS))[::-1]
        for di, d in order:
            group, segs = _attn_plan(nblk, d)
            fresh, last = di == order[0][0], di == order[-1][0]

            def step(i, carry, segs=segs, **kw):
                for s in range(segs):
                    segment(i * segs + s, **kw)
                return carry

            def segment(i, d=d, di=di, group=group, fresh=fresh, last=last):
                start, pstart, first = _attn_group_index(i, nblk, d, group)
                prows = _attn_rows(pstart, d)
                k_prev, v_prev = k_ref[prows, :].astype(BF), v_ref[prows, :].astype(BF)
                for b in range(group):
                    out = _attn_rows(start + b * (d * QBLK), d)
                    k_own, v_own = k_ref[out, :].astype(BF), v_ref[out, :].astype(BF)
                    k2, v2 = jnp.concatenate([k_prev, k_own], axis=0), jnp.concatenate([v_prev, v_own], axis=0)
                    k_prev, v_prev = k_own, v_own
                    bias = bias_ref[di * 2 + first.astype(jnp.int32)] if b == 0 else bias_ref[di * 2]
                    s = _dot_nt(_stack_heads(q_ref[out, :] * (HEAD_DIM ** -0.5), lo), k2) + bias
                    m = jnp.max(s, axis=-1, keepdims=True)
                    pr = jnp.exp(s - m)
                    m_b = _unstack_heads(m, lo)
                    l_b = _unstack_heads(jnp.sum(pr, axis=-1, keepdims=True), lo)
                    o_b = _unstack_heads(_dot(pr.astype(BF), v2), lo)
                    if fresh:
                        m_ref[out, :] = m_b
                        l_ref[out, :] = l_b
                        o_ref[out, :] = o_b
                        continue
                    m_o = m_ref[out, :]
                    m_n = jnp.maximum(m_o, m_b)
                    wa, wb = jnp.exp(m_o - m_n), jnp.exp(m_b - m_n)
                    l_n = wa * l_ref[out, :] + wb * l_b
                    o_n = wa * o_ref[out, :] + wb * o_b
                    if last:
                        m_ref[out, :] = m_n + jnp.log(l_n)
                        o_ref[out, :] = o_n / l_n
                    else:
                        m_ref[out, :] = m_n
                        l_ref[out, :] = l_n
                        o_ref[out, :] = o_n

            lax.fori_loop(0, nblk // (group * segs), step, 0)

        @pl.when(h == pl.num_programs(0) - 1)
        def _():
            _Gather(list(zip(srcs, dsts)), sems).finish()

    col = lambda base: pl.BlockSpec((T, LANES), lambda h: (0, base + h))
    tok = pl.BlockSpec((T, LANES), lambda h: (0, h))
    outs = pl.pallas_call(
        body, grid=(4,),
        in_specs=[col(0), col(4), col(8), pl.BlockSpec((1, 8, LANES), lambda h: (h, 0, 0))] + [HBM] * n_g,
        out_specs=[tok, tok] + [HBM] * n_g,
        out_shape=[SDS((T, ATTN_W), F32), SDS((T, ATTN_W), F32)]
        + [SDS((N_DEV,) + g.shape, g.dtype) for g in to_gather],
        scratch_shapes=_scatter_sems(n_g) + [pltpu.VMEM((T, LANES), F32), pltpu.VMEM((6, 2 * QBLK, 2 * QBLK), F32)],
        name="attn_fwd", compiler_params=_params(("arbitrary",), 56))(proj, proj, proj, slopes, *to_gather)
    return outs[0], outs[1], outs[2:]


def _sgu_norm(zv, ln_g, ln_b):
    gz, tz = _gelu(zv)
    mu = jnp.mean(gz, axis=-1, keepdims=True)
    xc = gz - mu
    rs = lax.rsqrt(jnp.mean(xc * xc, axis=-1, keepdims=True) + EPS)
    xhat = xc * rs
    return xhat * ln_g + ln_b, xhat, rs, tz


def _causal(w):
    i = lax.broadcasted_iota(jnp.int32, (CHUNK, CHUNK), 0)
    j = lax.broadcasted_iota(jnp.int32, (CHUNK, CHUNK), 1)
    return jnp.where(i >= j, w, 0.0)


def _sgu_fwd(proj, ln_g, ln_b, w_s, b_st, tm=512):
    T = proj.shape[0]

    def body(u_ref, z_ref, g_ref, b_ref, w_ref, bs_ref, out_ref):
        for g in range(N_GROUPS):
            wm = _causal(w_ref[g]).astype(BF)
            cols = slice(g * GROUP_DIM, (g + 1) * GROUP_DIM)
            for c in range(tm // CHUNK):
                rows = slice(c * CHUNK, (c + 1) * CHUNK)
                zn, _, _, _ = _sgu_norm(z_ref[rows, cols], g_ref[...], b_ref[...])
                mixed = _dot(wm, zn.astype(BF)) + bs_ref[:, g:g + 1]
                gu, _ = _gelu(u_ref[rows, cols])
                out_ref[rows, cols] = gu * mixed

    return pl.pallas_call(
        body, grid=(T // tm,),
        in_specs=[pl.BlockSpec((tm, SGU_W), lambda i: (i, 3)), pl.BlockSpec((tm, SGU_W), lambda i: (i, 4)),
                  pl.BlockSpec((1, GROUP_DIM), _fixed), pl.BlockSpec((1, GROUP_DIM), _fixed),
                  pl.BlockSpec((N_GROUPS, CHUNK, CHUNK), lambda i: (0, 0, 0)), pl.BlockSpec((CHUNK, LANES), _fixed)],
        out_specs=pl.BlockSpec((tm, SGU_W), _row),
        out_shape=SDS((T, SGU_W), F32),
        name="sgu_fwd", compiler_params=_params(("arbitrary",), 32))(proj, proj, ln_g, ln_b, w_s, b_st)


def _out_proj(attn, sgu, x, g_a, g_s, w_out, g_pm, g_pf, tm=512):
    T = x.shape[0]

    def body(a_ref, s_ref, x_ref, ga_ref, gs_ref, w_ref, gpm_ref, gpf_ref, grp_ref, mixed_ref, h1_ref, f_ref):
        av, sv = a_ref[...], s_ref[...]
        an = (av * _rstd(av) * ga_ref[...]).astype(BF)
        sn = (sv * _rstd(sv) * gs_ref[...]).astype(BF)
        grp_ref[:, :ATTN_W] = an
        grp_ref[:, ATTN_W:] = sn
        mixed = _dot(an, w_ref[:ATTN_W, :]) + _dot(sn, w_ref[ATTN_W:, :])
        mixed_ref[...] = mixed
        h1 = x_ref[...] + mixed * _rstd(mixed) * gpm_ref[...]
        h1_ref[...] = h1
        f_ref[...] = (h1 * _rstd(h1) * gpf_ref[...]).astype(BF)

    tok = lambda w: pl.BlockSpec((tm, w), _row)
    vec = lambda w: pl.BlockSpec((1, w), _fixed)
    return pl.pallas_call(
        body, grid=(T // tm,),
        in_specs=[tok(ATTN_W), tok(SGU_W), tok(D_MODEL), vec(ATTN_W), vec(SGU_W),
                  pl.BlockSpec((D_MODEL, D_MODEL), _fixed), vec(D_MODEL), vec(D_MODEL)],
        out_specs=[tok(D_MODEL)] * 4,
        out_shape=[SDS((T, D_MODEL), BF), SDS((T, D_MODEL), F32), SDS((T, D_MODEL), F32), SDS((T, D_MODEL), BF)],
        name="out_proj", compiler_params=_params(("arbitrary",), 48))(attn, sgu, x, g_a, g_s, w_out, g_pm, g_pf)


FF_TILE = 1408
FF_TILES = D_FF // FF_TILE
FF_CHUNK = 256


def _gate_up(f, w_gu_t, tm=512):
    T = f.shape[0]
    tn = FF_TILE

    def body(f_ref, wg_ref, wu_ref, g_ref, u_ref, act_ref):
        fv = f_ref[...]
        g = _dot_nt(fv, wg_ref[...])
        u = _dot_nt(fv, wu_ref[...])
        g_ref[...] = g.astype(BF)
        u_ref[...] = u.astype(BF)
        act_ref[...] = (g * _sigmoid(g) * u).astype(BF)

    ospec = pl.BlockSpec((tm, tn), lambda j, i: (i, j))
    return pl.pallas_call(
        body, grid=(FF_TILES, T // tm),
        in_specs=[pl.BlockSpec((tm, D_MODEL), lambda j, i: (i, 0)), pl.BlockSpec((tn, D_MODEL), lambda j, i: (j, 0)),
                  pl.BlockSpec((tn, D_MODEL), lambda j, i: (j + FF_TILES, 0))],
        out_specs=[ospec] * 3, out_shape=[SDS((T, D_FF), BF)] * 3,
        name="gate_up", compiler_params=_params(("arbitrary", "arbitrary"), 40))(f, w_gu_t, w_gu_t)


def _down_proj(act, w_down, h1, g_pff, tm=512):
    T = act.shape[0]

    def body(a_ref, w_ref, h1_ref, g_ref, y_ref, h2_ref):
        y = _dot(a_ref[...], w_ref[...])
        y_ref[...] = y
        h2_ref[...] = h1_ref[...] + y * _rstd(y) * g_ref[...]

    return pl.pallas_call(
        body, grid=(T // tm,),
        in_specs=[pl.BlockSpec((tm, D_FF), _row), pl.BlockSpec((D_FF, D_MODEL), _fixed),
                  pl.BlockSpec((tm, D_MODEL), _row), pl.BlockSpec((1, D_MODEL), _fixed)],
        out_specs=[pl.BlockSpec((tm, D_MODEL), _row)] * 2,
        out_shape=[SDS((T, D_MODEL), F32)] * 2,
        name="down_proj", compiler_params=_params(("arbitrary",), 48))(act, w_down, h1, g_pff)


def _pe_loss_and_bwd(h2, p, target, y, w_peg, b_peg, w_pep_t, g_pff, tm=512):
    T = h2.shape[0]

    def body(h2_ref, p_ref, t_ref, y_ref, wg_ref, b_ref, wp_ref, g_ref,
             dh2_ref, dy_ref, gpeg_ref, gpep_ref, loss_ref, db_ref, dg_ref):
        _acc_init(pl.program_id(0), gpeg_ref, gpep_ref, loss_ref, db_ref, dg_ref)
        h2v = h2_ref[...]
        h2b = h2v.astype(BF)
        pb = p_ref[...].astype(BF)
        gate = _sigmoid(_dot(h2b, wg_ref[...]) + b_ref[...])
        pp = _dot_nt(pb, wp_ref[...])
        diff = h2v + gate * pp - t_ref[...]
        loss_ref[...] += _colsum(diff * diff)
        dh3 = diff * (1.0 / D_MODEL)
        dpre = dh3 * pp * (gate * (1.0 - gate))
        dpre_b = dpre.astype(BF)
        db_ref[...] += _colsum(dpre)
        gpeg_ref[...] += _dot_tn(h2b, dpre_b)
        gpep_ref[...] += _dot_tn((dh3 * gate).astype(BF), pb)
        dh2 = dh3 + _dot_nt(dpre_b, wg_ref[...])
        dh2_ref[...] = dh2
        dy, dg = _rms_bwd(dh2, y_ref[...], g_ref[...])
        dy_ref[...] = dy.astype(BF)
        dg_ref[...] += _colsum(dg)

    tok = lambda w: pl.BlockSpec((tm, w), _row)
    vec = pl.BlockSpec((1, D_MODEL), _fixed)
    wg = pl.BlockSpec((D_MODEL, D_MODEL), _fixed)
    wp = pl.BlockSpec((D_MODEL, PLE_DIM), _fixed)
    return pl.pallas_call(
        body, grid=(T // tm,),
        in_specs=[tok(D_MODEL), tok(PLE_DIM), tok(D_MODEL), tok(D_MODEL), wg, vec, wp, vec],
        out_specs=[tok(D_MODEL), tok(D_MODEL), wg, wp, vec, vec, vec],
        out_shape=[SDS((T, D_MODEL), F32), SDS((T, D_MODEL), BF), SDS((D_MODEL, D_MODEL), F32),
                   SDS((D_MODEL, PLE_DIM), F32)] + [SDS((1, D_MODEL), F32)] * 3,
        name="pe_loss_and_bwd", compiler_params=_params(("arbitrary",), 56))(
            h2, p, target, y, w_peg, b_peg, w_pep_t, g_pff)


def _weight_grad(a, dy, name, into=None, row_tile=0, rows=None, tk=1024):
    n = dy.shape[1]
    tn = min(n, 1024)
    T, ka = a.shape
    tka = FF_TILE if ka == D_FF else min(ka, 1024)
    rows = ka if rows is None else rows

    def body(a_ref, dy_ref, *rest):
        out_ref = rest[-1]
        _acc_init(pl.program_id(2), out_ref)
        out_ref[...] += _dot_tn(a_ref[...].astype(BF), dy_ref[...].astype(BF))

    carried = [] if into is None else [into]
    return pl.pallas_call(
        body, grid=(ka // tka, n // tn, T // tk),
        in_specs=[pl.BlockSpec((tk, tka), lambda i, j, k: (k, i)), pl.BlockSpec((tk, tn), lambda i, j, k: (k, j))]
        + [HBM] * len(carried),
        out_specs=pl.BlockSpec((tka, tn), lambda i, j, k: (i + row_tile, j)),
        out_shape=SDS((rows, n), F32), input_output_aliases={2: 0} if carried else {},
        name="grad_" + name, compiler_params=_params(("arbitrary",) * 3, 40))(a, dy, *carried)


def _grad_w_in(dparts, a, tk=1024):
    T = a.shape[0]

    def body(*refs):
        d_refs, a_ref, out_ref, acc_ref = refs[:len(dparts)], refs[-3], refs[-2], refs[-1]
        k = pl.program_id(0)
        _acc_init(k, acc_ref)
        cols = [r[part].astype(BF) for r in d_refs for part in range(r.shape[0])]
        acc_ref[...] += _dot_tn(jnp.concatenate(cols, axis=1), a_ref[...])

        @pl.when(k == pl.num_programs(0) - 1)
        def _():
            out_ref[...] = acc_ref[...].astype(BF)

    return pl.pallas_call(
        body, grid=(T // tk,),
        in_specs=[pl.BlockSpec((d.shape[0], tk, d.shape[2]), lambda k: (0, k, 0)) for d in dparts]
        + [pl.BlockSpec((tk, D_MODEL), lambda k: (k, 0))],
        out_specs=pl.BlockSpec((PROJ, D_MODEL), lambda k: (0, 0)),
        out_shape=SDS((PROJ, D_MODEL), BF), scratch_shapes=[pltpu.VMEM((PROJ, D_MODEL), F32)],
        name="grad_w_in", compiler_params=_params(("arbitrary",), 48))(*dparts, a)


def _down_bwd(dy, w_down, g, u, to_send, tm=512):
    T = dy.shape[0]
    n_s = len(to_send)

    def body(dy_ref, w_ref, g_ref, u_ref, *rest):
        srcs, (dg_ref, du_ref), dsts, sems = rest[:n_s], rest[n_s:n_s + 2], rest[n_s + 2:2 * n_s + 2], rest[2 * n_s + 2:]
        i = pl.program_id(0)
        items = list(zip(srcs, dsts, [True] * n_s))

        @pl.when(i == 0)
        def _():
            _Scatter(items, sems).start()

        dyv = dy_ref[...]
        for c in range(D_FF // FF_CHUNK):
            cols = slice(c * FF_CHUNK, (c + 1) * FF_CHUNK)
            dact = _dot_nt(dyv, w_ref[cols, :]).astype(BF)
            gv, uv = g_ref[:, cols], u_ref[:, cols]
            s = _sigmoid(gv)
            ds = dact * s
            dg_ref[:, cols] = ds * uv * (1.0 + gv * (1.0 - s))
            du_ref[:, cols] = ds * gv

        @pl.when(i == pl.num_programs(0) - 1)
        def _():
            _Scatter(items, sems).wait()

    tile = pl.BlockSpec((tm, D_FF), _row)
    outs = pl.pallas_call(
        body, grid=(T // tm,),
        in_specs=[pl.BlockSpec((tm, D_MODEL), _row),
                  pl.BlockSpec((D_FF, D_MODEL), _fixed, pipeline_mode=pl.Buffered(1)), tile, tile] + [HBM] * n_s,
        out_specs=[tile, tile] + [HBM] * n_s,
        out_shape=[SDS((T, D_FF), BF)] * 2 + [SDS(s.shape, s.dtype) for s in to_send],
        scratch_shapes=_scatter_sems(n_s),
        name="down_bwd", compiler_params=_params(("arbitrary",), 48))(dy, w_down, g, u, *to_send)
    return outs[0], outs[1], outs[2:]


def _ffn_in_bwd(dg, du, w_gu_t, h1, dh2, mixed, g_pf, g_pm, to_send, tm=512):
    T = h1.shape[0]
    n_s = len(to_send)

    def body(dg_ref, du_ref, w_ref, h1_ref, dh2_ref, mx_ref, gpf_ref, gpm_ref, *rest):
        srcs, outs, dsts, sems = rest[:n_s], rest[n_s:n_s + 4], rest[n_s + 4:2 * n_s + 4], rest[2 * n_s + 4:]
        dh1_ref, dmx_ref, dgpf_ref, dgpm_ref = outs
        i = pl.program_id(0)
        items = list(zip(srcs, dsts, [True] * n_s))
        _acc_init(i, dgpf_ref, dgpm_ref)

        @pl.when(i == 0)
        def _():
            _Scatter(items, sems).start()

        df = _dot(jnp.concatenate([dg_ref[...], du_ref[...]], axis=1), w_ref[...])
        dx, dgf = _rms_bwd(df, h1_ref[...], gpf_ref[...])
        dh1 = dh2_ref[...] + dx
        dh1_ref[...] = dh1
        dmx, dgm = _rms_bwd(dh1, mx_ref[...], gpm_ref[...])
        dmx_ref[...] = dmx.astype(BF)
        dgpf_ref[...] += _colsum(dgf)
        dgpm_ref[...] += _colsum(dgm)

        @pl.when(i == pl.num_programs(0) - 1)
        def _():
            _Scatter(items, sems).wait()

    tok = lambda w: pl.BlockSpec((tm, w), _row)
    vec = pl.BlockSpec((1, D_MODEL), _fixed)
    outs = pl.pallas_call(
        body, grid=(T // tm,),
        in_specs=[tok(D_FF), tok(D_FF), pl.BlockSpec((2 * D_FF, D_MODEL), _fixed, pipeline_mode=pl.Buffered(1)),
                  tok(D_MODEL), tok(D_MODEL), tok(D_MODEL), vec, vec]
        + [HBM] * n_s,
        out_specs=[tok(D_MODEL), tok(D_MODEL), vec, vec] + [HBM] * n_s,
        out_shape=[SDS((T, D_MODEL), F32), SDS((T, D_MODEL), BF), SDS((1, D_MODEL), F32), SDS((1, D_MODEL), F32)]
        + [SDS(s.shape, s.dtype) for s in to_send],
        scratch_shapes=_scatter_sems(n_s),
        name="ffn_in_bwd", compiler_params=_params(("arbitrary",), 56))(
            dg, du, w_gu_t, h1, dh2, mixed, g_pf, g_pm, *to_send)
    return outs[0], outs[1], outs[2], outs[3], outs[4:]


STAT_LANES = HEAD_DIM // 2


def _ff_grad_spec(half):
    return pl.BlockSpec((D_FF, D_MODEL), lambda i: (half, 0), pipeline_mode=pl.Buffered(1))


def _out_bwd(dmx, w_out, attn, lse, sgu, g_a, g_s, dg, f, tm=512):
    T = attn.shape[0]

    def body(dm_ref, w_ref, a_ref, l_ref, s_ref, ga_ref, gs_ref, dgate_ref, f_ref,
             da_ref, st_ref, ds_ref, dga_ref, dgs_ref, ggu_ref):
        _acc_init(pl.program_id(0), dga_ref, dgs_ref, ggu_ref)
        ggu_ref[...] += _dot_tn(dgate_ref[...], f_ref[...])
        dgr = _dot_nt(dm_ref[...], w_ref[...])
        av = a_ref[...]
        da, dga = _rms_bwd(dgr[:, :ATTN_W], av, ga_ref[...])
        ds, dgs = _rms_bwd(dgr[:, ATTN_W:], s_ref[...], gs_ref[...])
        da_ref[...] = da
        ds_ref[...] = ds
        dga_ref[...] += _colsum(dga)
        dgs_ref[...] += _colsum(dgs)
        lane = lax.broadcasted_iota(jnp.int32, (1, LANES), 1)
        lo = lane < HEAD_DIM
        first = (lane % HEAD_DIM) < STAT_LANES
        prod = da * av
        for c in range(ATTN_W // LANES):
            cols = slice(c * LANES, (c + 1) * LANES)
            pc = prod[:, cols]
            delta = jnp.where(lo, jnp.sum(jnp.where(lo, pc, 0.0), axis=-1, keepdims=True),
                              jnp.sum(jnp.where(lo, 0.0, pc), axis=-1, keepdims=True))
            st_ref[:, cols] = jnp.where(first, l_ref[:, cols], delta)

    tok = lambda w: pl.BlockSpec((tm, w), _row)
    vec = lambda w: pl.BlockSpec((1, w), _fixed)
    return pl.pallas_call(
        body, grid=(T // tm,),
        in_specs=[tok(D_MODEL), pl.BlockSpec((D_MODEL, D_MODEL), _fixed), tok(ATTN_W), tok(ATTN_W), tok(SGU_W),
                  vec(ATTN_W), vec(SGU_W), tok(D_FF), tok(D_MODEL)],
        out_specs=[tok(ATTN_W), tok(ATTN_W), tok(SGU_W), vec(ATTN_W), vec(SGU_W), _ff_grad_spec(0)],
        out_shape=[SDS((T, ATTN_W), F32), SDS((T, ATTN_W), F32), SDS((T, SGU_W), F32), SDS((1, ATTN_W), F32),
                   SDS((1, SGU_W), F32), SDS((2 * D_FF, D_MODEL), F32)],
        name="out_bwd", compiler_params=_params(("arbitrary",), 56))(dmx, w_out, attn, lse, sgu, g_a, g_s, dg, f)


def _sgu_bwd(proj, dsgu, ln_g, ln_b, w_s, b_st, groups, dmx, d_up, f, g_gu_t, tm=512):
    T = proj.shape[0]

    def body(u_ref, z_ref, ds_ref, g_ref, b_ref, w_ref, bs_ref, grp_ref, dmx_ref, dup_ref, f_ref, _,
             duz_ref, dw_ref, dbs_ref, dlg_ref, dlb_ref, gout_ref, ggu_ref, dbacc_ref):
        du_ref, dz_ref = duz_ref.at[0], duz_ref.at[1]
        step = pl.program_id(0)
        _acc_init(step, dw_ref, dbs_ref, dlg_ref, dlb_ref, gout_ref, ggu_ref, dbacc_ref)
        gout_ref[...] += _dot_tn(grp_ref[...], dmx_ref[...])
        ggu_ref[...] += _dot_tn(dup_ref[...], f_ref[...])
        lng, lnb = g_ref[...], b_ref[...]
        for g in range(N_GROUPS):
            wm = _causal(w_ref[g]).astype(BF)
            cols = slice(g * GROUP_DIM, (g + 1) * GROUP_DIM)
            for c in range(tm // CHUNK):
                rows = slice(c * CHUNK, (c + 1) * CHUNK)
                zv, uv, dout = z_ref[rows, cols], u_ref[rows, cols], ds_ref[rows, cols]
                zn, xhat, rs, tz = _sgu_norm(zv, lng, lnb)
                znb = zn.astype(BF)
                mixed = _dot(wm, znb) + bs_ref[:, g:g + 1]
                gu, tu = _gelu(uv)
                du_ref[rows, cols] = (dout * mixed * _gelu_grad(uv, tu)).astype(BF)
                dmix = dout * gu
                dmb = dmix.astype(BF)
                dw_ref[g] += _causal(_dot_nt(dmb, znb))
                dbacc_ref[g] += dmix
                dzn = _dot_tn(wm, dmb)
                dlg_ref[...] += _colsum(dzn * xhat)
                dlb_ref[...] += _colsum(dzn)
                dxh = dzn * lng
                dgz = rs * (dxh - jnp.mean(dxh, axis=-1, keepdims=True)
                            - xhat * jnp.mean(dxh * xhat, axis=-1, keepdims=True))
                dz_ref[rows, cols] = (dgz * _gelu_grad(zv, tz)).astype(BF)

        @pl.when(step == pl.num_programs(0) - 1)
        def _():
            lane = lax.broadcasted_iota(jnp.int32, (CHUNK, LANES), 1)
            acc = jnp.zeros((CHUNK, LANES), F32)
            for g in range(N_GROUPS):
                acc = jnp.where(lane == g, jnp.sum(dbacc_ref[g], axis=-1, keepdims=True), acc)
            dbs_ref[...] = acc

    tok = pl.BlockSpec((tm, SGU_W), _row)
    vec = pl.BlockSpec((1, GROUP_DIM), _fixed)
    wsp = pl.BlockSpec((N_GROUPS, CHUNK, CHUNK), lambda i: (0, 0, 0))
    sq = pl.BlockSpec((CHUNK, LANES), _fixed)
    wide = pl.BlockSpec((tm, D_MODEL), _row)
    return pl.pallas_call(
        body, grid=(T // tm,),
        in_specs=[pl.BlockSpec((tm, SGU_W), lambda i: (i, 3)), pl.BlockSpec((tm, SGU_W), lambda i: (i, 4)), tok,
                  vec, vec, wsp, sq, wide, wide, pl.BlockSpec((tm, D_FF), _row), wide, HBM],
        out_specs=[pl.BlockSpec((2, tm, SGU_W), lambda i: (0, i, 0)), wsp, sq, vec, vec,
                   pl.BlockSpec((D_MODEL, D_MODEL), _fixed), _ff_grad_spec(1)],
        out_shape=[SDS((2, T, SGU_W), BF), SDS((N_GROUPS, CHUNK, CHUNK), F32),
                   SDS((CHUNK, LANES), F32), SDS((1, GROUP_DIM), F32), SDS((1, GROUP_DIM), F32),
                   SDS((D_MODEL, D_MODEL), F32), SDS((2 * D_FF, D_MODEL), F32)],
        input_output_aliases={11: 6},
        scratch_shapes=[pltpu.VMEM((N_GROUPS, CHUNK, LANES), F32)],
        name="sgu_bwd", compiler_params=_params(("arbitrary",), 56))(
            proj, proj, dsgu, ln_g, ln_b, w_s, b_st, groups, dmx, d_up, f, g_gu_t)


def _attn_bwd(proj, do, stats, slopes, to_send, slabbed):
    T = proj.shape[0]
    nblk = T // QBLK
    n_s = len(to_send)

    def body(q_ref, k_ref, v_ref, do_ref, st_ref, sl_ref, *rest):
        srcs, d_ref, dsts = rest[:n_s], rest[n_s], rest[n_s + 1:2 * n_s + 1]
        sems, bias_ref = rest[2 * n_s + 1:2 * n_s + 4], rest[2 * n_s + 4]
        dq_ref, dk_ref, dv_ref = d_ref.at[0], d_ref.at[1], d_ref.at[2]
        h = pl.program_id(0)
        items = list(zip(srcs, dsts, slabbed))

        @pl.when(h == 0)
        def _():
            _Scatter(items, sems).start()

        _attn_bias(sl_ref, bias_ref)
        lo = lax.broadcasted_iota(jnp.int32, (1, LANES), 1) < HEAD_DIM
        scale = HEAD_DIM ** -0.5
        d_ref[...] = jnp.zeros_like(d_ref)

        for di, d in enumerate(DILATIONS):
            group, segs = _attn_plan(nblk, d)

            def step(i, carry, segs=segs, **kw):
                for s in range(segs):
                    segment(i * segs + s, **kw)
                return carry

            def segment(i, d=d, di=di, group=group):
                start, pstart, first = _attn_group_index(i, nblk, d, group)
                rows, prows = _attn_rows(start, d, group), _attn_rows(pstart, d)
                q = q_ref[rows, :] * scale
                k = jnp.concatenate([k_ref[prows, :], k_ref[rows, :]], axis=0).astype(BF)
                v = jnp.concatenate([v_ref[prows, :], v_ref[rows, :]], axis=0).astype(BF)
                dov = do_ref[rows, :]
                stats = st_ref[rows, :]
                masks = [lo, ~lo]
                qm = [jnp.where(masks[j], q, 0.0).astype(BF) for j in range(2)]
                dom = [jnp.where(masks[j], dov, 0.0).astype(BF) for j in range(2)]
                for b in range(group):
                    qb = slice(b * QBLK, (b + 1) * QBLK)
                    kb = slice(b * QBLK, (b + 2) * QBLK)
                    which = di * 2 + first.astype(jnp.int32) if b == 0 else di * 2
                    dq_parts, prs, dss = [], [], []
                    for j in range(2):
                        bias = bias_ref[which, j * QBLK:(j + 1) * QBLK, :]
                        lj = stats[qb, j * HEAD_DIM:j * HEAD_DIM + 1]
                        delta = stats[qb, j * HEAD_DIM + STAT_LANES:j * HEAD_DIM + STAT_LANES + 1]
                        pr = jnp.exp(_dot_nt(qm[j][qb], k[kb]) + bias - lj)
                        ds = (pr * (_dot_nt(dom[j][qb], v[kb]) - delta)).astype(BF)
                        dq_parts.append(_dot(ds, k[kb]))
                        prs.append(pr.astype(BF))
                        dss.append(ds)
                    dk_b = _dot_tn(jnp.concatenate(dss, axis=0), jnp.concatenate([qm[0][qb], qm[1][qb]], axis=0))
                    dv_b = _dot_tn(jnp.concatenate(prs, axis=0), jnp.concatenate([dom[0][qb], dom[1][qb]], axis=0))
                    own = _attn_rows(start + b * (d * QBLK), d)
                    dq_ref[own, :] += jnp.where(lo, dq_parts[0], dq_parts[1]) * scale
                    if b == 0:
                        dk_ref[prows, :] += dk_b[:QBLK]
                        dv_ref[prows, :] += dv_b[:QBLK]
                        dk_ref[own, :] += dk_b[QBLK:]
                        dv_ref[own, :] += dv_b[QBLK:]
                    else:
                        two = _attn_rows(start + (b - 1) * (d * QBLK), d, 2)
                        dk_ref[two, :] += dk_b
                        dv_ref[two, :] += dv_b

            lax.fori_loop(0, nblk // (group * segs), step, 0)

        @pl.when(h == pl.num_programs(0) - 1)
        def _():
            _Scatter(items, sems).wait()

    col = lambda base: pl.BlockSpec((T, LANES), lambda h: (0, base + h))
    outs = pl.pallas_call(
        body, grid=(4,),
        in_specs=[col(0), col(4), col(8), col(0), col(0), pl.BlockSpec((1, 8, LANES), lambda h: (h, 0, 0))]
        + [HBM] * n_s,
        out_specs=[pl.BlockSpec((3, T, LANES), lambda h: (0, 0, h), pipeline_mode=pl.Buffered(1))] + [HBM] * n_s,
        out_shape=[SDS((3, T, ATTN_W), F32)]
        + [SDS(s.shape if sl else (N_DEV,) + s.shape, s.dtype) for s, sl in zip(to_send, slabbed)],
        scratch_shapes=_scatter_sems(n_s) + [pltpu.VMEM((6, 2 * QBLK, 2 * QBLK), F32)],
        name="attn_bwd", compiler_params=_params(("arbitrary",), 60))(proj, proj, proj, do, stats, slopes, *to_send)
    return outs[0], outs[1:]


def _in_bwd(dparts, w_in_t, x, dh1, g1, tm=512):
    T = x.shape[0]
    n = len(dparts)
    w = ATTN_W

    def body(*refs):
        d_refs, (w_ref, x_ref, dh1_ref, g_ref, dx_ref, dg_ref) = refs[:n], refs[n:]
        _acc_init(pl.program_id(0), dg_ref)
        d_proj = jnp.concatenate([r[part].astype(BF) for r in d_refs for part in range(r.shape[0])], axis=1)
        da = _dot(d_proj, w_ref[...])
        dx, dg = _rms_bwd(da, x_ref[...], g_ref[...])
        dx_ref[...] = dh1_ref[...] + dx
        dg_ref[...] += _colsum(dg)

    tok = lambda c: pl.BlockSpec((tm, c), _row)
    vec = pl.BlockSpec((1, D_MODEL), _fixed)
    return pl.pallas_call(
        body, grid=(T // tm,),
        in_specs=[pl.BlockSpec((d.shape[0], tm, w), lambda i: (0, i, 0)) for d in dparts]
        + [pl.BlockSpec((PROJ, D_MODEL), _fixed), tok(D_MODEL), tok(D_MODEL), vec],
        out_specs=[tok(D_MODEL), vec],
        out_shape=[SDS((T, D_MODEL), F32), SDS((1, D_MODEL), F32)],
        name="in_bwd", compiler_params=_params(("arbitrary",), 52))(*dparts, w_in_t, x, dh1, g1)


def _sum_parts(p_ref):
    g = p_ref[0].astype(F32)
    for s in range(1, N_DEV):
        g = g + p_ref[s].astype(F32)
    return g


def _adamw_math(g, w, m, v):
    nm = ADAM_B1 * m + (1.0 - ADAM_B1) * g
    nv = ADAM_B2 * v + (1.0 - ADAM_B2) * (g * g)
    m_hat = nm / (1.0 - ADAM_B1 ** ADAM_STEP)
    v_hat = nv / (1.0 - ADAM_B2 ** ADAM_STEP)
    return -ADAM_LR * (m_hat / (jnp.sqrt(v_hat) + ADAM_EPS) + ADAM_WD * w), nm, nv


def _row_tile(rows):
    for t in (256, 176, 128, 80):
        if rows % t == 0:
            return t
    raise ValueError(rows)


def _reduce_adamw(parts, w, m, v, name):
    rows, width = w.shape
    tr = _row_tile(rows)

    def body(p_ref, w_ref, m_ref, v_ref, g_ref, d_ref, nm_ref, nv_ref):
        g = _sum_parts(p_ref)
        g_ref[...] = g
        d_ref[...], nm_ref[...], nv_ref[...] = _adamw_math(g, w_ref[...], m_ref[...], v_ref[...])

    blk = pl.BlockSpec((tr, width), _row)
    return pl.pallas_call(
        body, grid=(rows // tr,),
        in_specs=[pl.BlockSpec((N_DEV, tr, width), lambda i: (0, i, 0)), blk, blk, blk],
        out_specs=[blk] * 4, out_shape=[SDS((rows, width), F32)] * 4,
        name="adamw_" + name, compiler_params=_params(("arbitrary",), 32))(parts, w, m, v)


def _reduce(parts, name):
    _, rows, width = parts.shape
    tr = _row_tile(rows)

    def body(p_ref, g_ref):
        g_ref[...] = _sum_parts(p_ref)

    return pl.pallas_call(
        body, grid=(rows // tr,),
        in_specs=[pl.BlockSpec((N_DEV, tr, width), lambda i: (0, i, 0))],
        out_specs=pl.BlockSpec((tr, width), _row), out_shape=SDS((rows, width), F32),
        name="sum_" + name, compiler_params=_params(("arbitrary",), 32))(parts)


def _adamw(g, w, m, v, name):
    rows, width = w.shape
    tr = _row_tile(rows)

    def body(g_ref, w_ref, m_ref, v_ref, d_ref, nm_ref, nv_ref):
        d_ref[...], nm_ref[...], nv_ref[...] = _adamw_math(g_ref[...], w_ref[...], m_ref[...], v_ref[...])

    blk = pl.BlockSpec((tr, width), _row)
    return pl.pallas_call(
        body, grid=(rows // tr,), in_specs=[blk] * 4, out_specs=[blk] * 3, out_shape=[SDS((rows, width), F32)] * 3,
        name="adamw_" + name, compiler_params=_params(("arbitrary",), 32))(g, w, m, v)


SMALL = ("w_spatial", "ln_pre_mix", "ln_post_mix", "ln_pre_ffn", "ln_post_ffn", "b_pe_gate",
         "attn_out_norm", "sgu_out_norm", "b_spatial", "sgu_ln_g", "sgu_ln_b")
SMALL_GROUPS = ((128, ("w_spatial", "b_spatial", "sgu_ln_g", "sgu_ln_b")),
                (512, ("attn_out_norm", "sgu_out_norm")),
                (1024, ("ln_post_mix", "ln_pre_ffn", "ln_post_ffn", "b_pe_gate")))
SMALL_LATE = "ln_pre_mix"
SMALL_SIZE = dict(w_spatial=N_GROUPS * CHUNK * CHUNK, b_spatial=N_GROUPS * CHUNK, sgu_ln_g=GROUP_DIM, sgu_ln_b=GROUP_DIM,
                  attn_out_norm=ATTN_W, sgu_out_norm=SGU_W, ln_pre_mix=D_MODEL, ln_post_mix=D_MODEL, ln_pre_ffn=D_MODEL,
                  ln_post_ffn=D_MODEL, b_pe_gate=D_MODEL)
SUBLANES = 8
ROW_SHARDED = ("w_out", "w_down", "w_pe_gate")
COL_SHARDED = ("w_in", "w_gate_up", "w_pe_proj")
WEIGHTS = ("ln_pre_mix", "w_in", "sgu_ln_g", "sgu_ln_b", "w_spatial", "b_spatial", "attn_out_norm", "sgu_out_norm",
           "w_out", "ln_post_mix", "ln_pre_ffn", "w_gate_up", "w_down", "ln_post_ffn", "w_pe_gate", "b_pe_gate",
           "w_pe_proj")


def _group_rows(width, names, extra=0):
    rows = sum(SMALL_SIZE[n] // width for n in names) + extra
    return -(-rows // SUBLANES) * SUBLANES


def _pack_small_grads(gs, loss_term):
    packed = []
    for width, names in SMALL_GROUPS:
        rows = [gs[n].reshape(-1, width) for n in names]
        extra = int(width == D_MODEL)
        if extra:
            rows.append(jnp.full((1, width), loss_term, F32))
        used = sum(r.shape[0] for r in rows)
        rows.append(jnp.zeros((_group_rows(width, names, extra) - used, width), F32))
        packed.append(jnp.concatenate(rows, axis=0))
    return packed


def _small_adamw(arrived, arrived_late, w, m, v):
    names = [n for _, ns in SMALL_GROUPS for n in ns] + [SMALL_LATE]
    n_groups = len(SMALL_GROUPS)

    def body(*refs):
        group_refs, late_ref = refs[:n_groups], refs[n_groups]
        state = refs[n_groups + 1:n_groups + 1 + 3 * len(names)]
        outs = refs[n_groups + 1 + 3 * len(names):]
        sums = [_sum_parts(r) for r in group_refs]

        def update(name, g):
            i = names.index(name)
            w_ref, m_ref, v_ref = state[3 * i:3 * i + 3]
            delta, nm, nv = _adamw_math(g, w_ref[...].reshape(g.shape), m_ref[...].reshape(g.shape),
                                        v_ref[...].reshape(g.shape))
            for o_ref, val in zip(outs[4 * i:4 * i + 4], (g, delta, nm, nv)):
                o_ref[...] = val.reshape(o_ref.shape)

        for (width, group), total in zip(SMALL_GROUPS, sums):
            row = 0
            for name in group:
                rows = SMALL_SIZE[name] // width
                update(name, total[row:row + rows, :])
                row += rows
            if width == D_MODEL:
                outs[-1][...] = total[row:row + 1, :LANES]
        update(SMALL_LATE, _sum_parts(late_ref)[:1, :])

    state = [t[n] for n in names for t in (w, m, v)]
    plain = jax.ShapeDtypeStruct
    out_shape = [plain(w[n].shape, F32) for n in names for _ in range(4)] + [plain((1, LANES), F32)]
    outs = pl.pallas_call(body, out_shape=out_shape, name="adamw_small",
                          compiler_params=pltpu.CompilerParams(vmem_limit_bytes=32 * MIB))(*arrived, arrived_late, *state)
    return {n: outs[4 * i:4 * i + 4] for i, n in enumerate(names)}, outs[-1]


def _slabs(full):
    return full.reshape(N_DEV, full.shape[0] // N_DEV, full.shape[1])


def kernel(x, p, ln_pre_mix, w_in, sgu_ln_g, sgu_ln_b, w_spatial, b_spatial, attn_out_norm, sgu_out_norm, w_out, ln_post_mix, ln_pre_ffn, w_gate_up, w_down, ln_post_ffn, w_pe_gate, b_pe_gate, w_pe_proj, loss_target, m_ln_pre_mix, m_w_in, m_sgu_ln_g, m_sgu_ln_b, m_w_spatial, m_b_spatial, m_attn_out_norm, m_sgu_out_norm, m_w_out, m_ln_post_mix, m_ln_pre_ffn, m_w_gate_up, m_w_down, m_ln_post_ffn, m_w_pe_gate, m_b_pe_gate, m_w_pe_proj, v_ln_pre_mix, v_w_in, v_sgu_ln_g, v_sgu_ln_b, v_w_spatial, v_b_spatial, v_attn_out_norm, v_sgu_out_norm, v_w_out, v_ln_post_mix, v_ln_pre_ffn, v_w_gate_up, v_w_down, v_ln_post_ffn, v_w_pe_gate, v_b_pe_gate, v_w_pe_proj):
    given = dict(locals())
    w = {n: given[n] for n in WEIGHTS}
    m = {n: given["m_" + n] for n in WEIGHTS}
    v = {n: given["v_" + n] for n in WEIGHTS}
    xs, ps, target = x[0], p[0, 0], loss_target[0]

    shard = {n: w[n][0].astype(BF) for n in ROW_SHARDED}
    shard.update({n: w[n][0].T.astype(BF) for n in COL_SHARDED})
    sm = {n: w[n][0] for n in SMALL}
    sm = {n: (a.reshape(1, -1) if a.ndim == 1 else a) for n, a in sm.items()}
    slopes = jnp.broadcast_to((2.0 ** -(jnp.arange(8, dtype=F32) + 1.0)).reshape(4, 2, 1), (4, 2, LANES))
    slopes = jnp.concatenate([slopes, jnp.zeros((4, 6, LANES), F32)], axis=1)
    b_st = jnp.pad(sm["b_spatial"].T, ((0, 0), (0, LANES - N_GROUPS)))

    def full(gathered):
        return gathered.reshape(-1, gathered.shape[-1])

    w_in_t = full(_all_gather(shard["w_in"], "gather_w_in"))
    proj, a = _in_proj(xs, sm["ln_pre_mix"], w_in_t)
    later = ("w_out", "w_gate_up", "w_down", "w_pe_gate", "w_pe_proj")
    attn, lse, gathered = _attn_fwd(proj, slopes, [shard[n] for n in later])
    w_out_f, w_gu_t, w_down_f, w_peg_f, w_pep_t = [full(g) for g in gathered]
    sgu = _sgu_fwd(proj, sm["sgu_ln_g"], sm["sgu_ln_b"], sm["w_spatial"], b_st)
    groups, mixed, h1, f = _out_proj(attn, sgu, xs, sm["attn_out_norm"], sm["sgu_out_norm"], w_out_f,
                                     sm["ln_post_mix"], sm["ln_pre_ffn"])
    g, u, act = _gate_up(f, w_gu_t)
    y, h2 = _down_proj(act, w_down_f, h1, sm["ln_post_ffn"])
    dh2, dy, g_peg, g_pep_t, loss_cols, db_peg, d_pff = _pe_loss_and_bwd(
        h2, ps, target, y, w_peg_f, sm["b_pe_gate"], w_pep_t, sm["ln_post_ffn"])
    loss_term = 0.5 * jnp.sum(loss_cols) * (1.0 / D_MODEL)

    arrived = {}
    g_down = _weight_grad(act, dy, "w_down")
    dg, du, (arrived["w_pe_proj"], arrived["w_pe_gate"]) = _down_bwd(dy, w_down_f, g, u, [_slabs(g_pep_t), _slabs(g_peg)])
    dh1, dmx, d_pf, d_pm, (arrived["w_down"],) = _ffn_in_bwd(dg, du, w_gu_t, h1, dh2, mixed, sm["ln_pre_ffn"],
                                                            sm["ln_post_mix"], [_slabs(g_down)])
    dattn, stats, dsgu, d_ga, d_gs, g_gu_t = _out_bwd(dmx, w_out_f, attn, lse, sgu, sm["attn_out_norm"],
                                                      sm["sgu_out_norm"], dg, f)
    duz, d_ws, d_bst, d_lg, d_lb, g_out, g_gu_t = _sgu_bwd(proj, dsgu, sm["sgu_ln_g"], sm["sgu_ln_b"],
                                                           sm["w_spatial"], b_st, groups, dmx, du, f, g_gu_t)
    gs = dict(sgu_ln_g=d_lg, sgu_ln_b=d_lb, w_spatial=d_ws, b_spatial=d_bst[:, :N_GROUPS].T, attn_out_norm=d_ga,
              sgu_out_norm=d_gs, ln_post_mix=d_pm, ln_pre_ffn=d_pf, ln_post_ffn=d_pff, b_pe_gate=db_peg)
    small_grads = _pack_small_grads(gs, loss_term)
    dqkv, (arrived["w_gate_up"], arrived["w_out"], *arrived_small) = _attn_bwd(
        proj, dattn, stats, slopes, [_slabs(g_gu_t), _slabs(g_out), *small_grads],
        [True, True] + [False] * len(small_grads))
    send_sems, recv_sems, slabs, landing, token = _scatter_begin(_slabs(_grad_w_in([dqkv, duz], a)), "w_in_grad_send")
    grad_x, d_g1 = _in_bwd([dqkv, duz], w_in_t, xs, dh1, sm["ln_pre_mix"] + token[:1, :1])
    slabs, landing = _scatter_end(send_sems, recv_sems, slabs, landing, d_g1, "w_in_grad_arrive")
    me = 4 * lax.axis_index("x") + 2 * lax.axis_index("y") + lax.axis_index("c")
    own = lax.dynamic_slice_in_dim(slabs, me, 1, axis=0)
    arrived["w_in"] = lax.dynamic_update_slice_in_dim(landing, own, me, axis=0)
    (arrived_late,) = _scatter_call([jnp.pad(d_g1, ((0, SUBLANES - 1), (0, 0)))], [False], "ln_pre_mix_grad_exchange")

    res = {}
    for n in ROW_SHARDED:
        res[n] = _reduce_adamw(arrived[n], w[n][0], m[n][0], v[n][0], n)
    for n in ("w_in", "w_gate_up"):
        res[n] = [t.T for t in _reduce_adamw(arrived[n], w[n][0].T, m[n][0].T, v[n][0].T, n)]
    for n in ("w_pe_proj",):
        grad = _reduce(arrived[n], n).T
        res[n] = (grad, *_adamw(grad, w[n][0], m[n][0], v[n][0], n))
    small, loss_row = _small_adamw(arrived_small, arrived_late, w, m, v)

    out = []
    for k in range(4):
        out += [res[n][k][None] if n in res else small[n][k] for n in WEIGHTS]
    return (loss_row[0, 0], grad_x[None], *out)
```

```python
import math

import jax
import jax.numpy as jnp
from jax import lax
from jax.experimental import pallas as pl
from jax.experimental.pallas import tpu as pltpu

F32 = jnp.float32
BF = jnp.bfloat16


def SDS(shape, dtype):
    return pltpu.HBM(tuple(shape), dtype)

D_MODEL = 1024
ATTN_W = 512
SGU_W = 512
HEAD_DIM = 64
N_GROUPS = 4
GROUP_DIM = 128
CHUNK = 128
D_FF = 2816
PLE_DIM = 256
PROJ = 3 * ATTN_W + 2 * SGU_W
DILATIONS = (1, 4, 16)
QBLK = 128
EPS = 1e-6
NEG = -1e30
N_DEV = 8
LANES = 128

ADAM_LR = 0.001
ADAM_B1 = 0.9
ADAM_B2 = 0.999
ADAM_EPS = 1e-08
ADAM_WD = 0.01
ADAM_STEP = 10

MIB = 2 ** 20
MESH_ID = pl.DeviceIdType.MESH
HBM = pl.BlockSpec(memory_space=pl.ANY)


def _params(sem, vmem_mib):
    return pltpu.CompilerParams(dimension_semantics=sem, vmem_limit_bytes=vmem_mib * MIB)


def _dot(a, b):
    return jnp.dot(a, b, preferred_element_type=F32)


def _dot_nt(a, b):
    return lax.dot_general(a, b, (((1,), (1,)), ((), ())), preferred_element_type=F32)


def _dot_tn(a, b):
    return lax.dot_general(a, b, (((0,), (0,)), ((), ())), preferred_element_type=F32)


def _rstd(x):
    return lax.rsqrt(jnp.mean(x * x, axis=-1, keepdims=True) + EPS)


def _rms_bwd(dy, x, g):
    r = _rstd(x)
    n = x * r
    dn = dy * g
    dx = r * (dn - n * jnp.mean(dn * n, axis=-1, keepdims=True))
    return dx, dy * n


def _colsum(v):
    return jnp.sum(v, axis=0, keepdims=True)


_G0 = math.sqrt(2.0 / math.pi)
_G1 = 0.044715


def _gelu(x):
    t = jnp.tanh(_G0 * (x + _G1 * x * x * x))
    return 0.5 * x * (1.0 + t), t


def _gelu_grad(x, t):
    return 0.5 * (1.0 + t) + 0.5 * x * (1.0 - t * t) * (_G0 * (1.0 + 3.0 * _G1 * x * x))


def _sigmoid(x):
    return 0.5 * jnp.tanh(0.5 * x) + 0.5


def _row(i):
    return (i, 0)


def _fixed(i):
    return (0, 0)


def _acc_init(step, *refs):
    @pl.when(step == 0)
    def _():
        for r in refs:
            r[...] = jnp.zeros_like(r)


FLIPS = [(dx, dy, dc) for dx in (0, 1) for dy in (0, 1) for dc in (0, 1)][1:]
DMA_SEMS = pltpu.SemaphoreType.DMA


def _mesh_pos():
    return lax.axis_index("x"), lax.axis_index("y"), lax.axis_index("c")


def _remote(src, dst, sems, n, to):
    return pltpu.make_async_remote_copy(src_ref=src, dst_ref=dst, send_sem=sems[0].at[n], recv_sem=sems[1].at[n],
                                        device_id=to, device_id_type=MESH_ID)


class _Scatter:
    def __init__(self, items, sems):
        x, y, c = _mesh_pos()
        me = 4 * x + 2 * y + c
        self.local, self.sends, self.arrivals = [], [], []
        for i, (src, dst, slabbed) in enumerate(items):
            self.local.append(pltpu.make_async_copy(src.at[me] if slabbed else src, dst.at[me], sems[2].at[i]))
            for k, (dx, dy, dc) in enumerate(FLIPS):
                to = (1 - x if dx else x, 1 - y if dy else y, 1 - c if dc else c)
                peer = 4 * to[0] + 2 * to[1] + to[2]
                out = src.at[peer] if slabbed else src
                self.sends.append(_remote(out, dst.at[me], sems, 7 * i + k, to))
                self.arrivals.append(_remote(out, dst.at[peer], sems, 7 * i + k, to))

    def start(self):
        for cp in self.local + self.sends:
            cp.start()

    def wait(self):
        for cp in self.arrivals:
            cp.wait_recv()
        for cp in self.sends:
            cp.wait_send()
        for cp in self.local:
            cp.wait()


def _scatter_sems(n):
    return [DMA_SEMS((7 * n,)), DMA_SEMS((7 * n,)), DMA_SEMS((n,))]


class _Gather:
    def __init__(self, items, sems):
        x, y, c = _mesh_pos()
        me, sibling = (x, y, c), (x, y, 1 - c)
        chips = [(1 - x, y), (x, 1 - y), (1 - x, 1 - y)]
        self.first, self.passed, self.from_chips, self.rest, self.local = [], [], [], [], []
        for i, (src, dst) in enumerate(items):
            def slot(p, dst=dst):
                return dst.at[4 * p[0] + 2 * p[1] + p[2]]

            def copy(k, block, to, own=False, i=i, src=src, slot=slot):
                return _remote(src if own else slot(block), slot(block), sems, 7 * i + k, to)

            self.local.append(pltpu.make_async_copy(src, slot(me), sems[2].at[i]))
            self.first.append(copy(0, me, sibling, own=True))
            self.first += [copy(1 + j, me, (*chip, c), own=True) for j, chip in enumerate(chips)]
            self.passed += [copy(4 + j, (*chip, c), sibling) for j, chip in enumerate(chips)]
            self.from_chips += [copy(1 + j, (*chip, c), me) for j, chip in enumerate(chips)]
            self.rest.append(copy(0, sibling, me))
            self.rest += [copy(4 + j, (*chip, 1 - c), me) for j, chip in enumerate(chips)]

    def start(self):
        for cp in self.local + self.first:
            cp.start()

    def forward(self):
        for arrived, onward in zip(self.from_chips, self.passed):
            arrived.wait_recv()
            onward.start()

    def finish(self):
        for cp in self.rest:
            cp.wait_recv()
        for cp in self.first + self.passed:
            cp.wait_send()
        for cp in self.local:
            cp.wait()


def _all_gather(shard, name):
    def body(x_ref, out_ref, *sems):
        g = _Gather([(x_ref, out_ref)], sems)
        g.start()
        g.forward()
        g.finish()

    return pl.pallas_call(
        body, out_shape=SDS((N_DEV,) + shard.shape, shard.dtype), in_specs=[HBM], out_specs=HBM,
        scratch_shapes=_scatter_sems(1), name=name)(shard)


def _scatter_call(srcs, slabbed, name):
    n = len(srcs)

    def body(*refs):
        sc = _Scatter(list(zip(refs[:n], refs[n:2 * n], slabbed)), refs[2 * n:])
        sc.start()
        sc.wait()

    shapes = [SDS(s.shape if sl else (N_DEV,) + s.shape, s.dtype) for s, sl in zip(srcs, slabbed)]
    return pl.pallas_call(body, out_shape=shapes, in_specs=[HBM] * n, out_specs=[HBM] * n,
                          scratch_shapes=_scatter_sems(n), name=name)(*srcs)


SEM = pl.BlockSpec(memory_space=pltpu.SEMAPHORE)
N_PEERS = len(FLIPS)


def _slab_copies(src_ref, land_ref, send_sems, recv_sems):
    x, y, c = _mesh_pos()
    me = 4 * x + 2 * y + c
    copies = []
    for k, (dx, dy, dc) in enumerate(FLIPS):
        to = (1 - x if dx else x, 1 - y if dy else y, 1 - c if dc else c)
        peer = 4 * to[0] + 2 * to[1] + to[2]
        sems = (send_sems, recv_sems)
        copies.append((_remote(src_ref.at[peer], land_ref.at[me], sems, k, to),
                       _remote(src_ref.at[peer], land_ref.at[peer], sems, k, to)))
    return copies


def _scatter_begin(src, name):
    def body(src_ref, land_ref, send_sems, recv_sems, src_thru, land_thru, token):
        for send, _ in _slab_copies(src_ref, land_ref, send_sems, recv_sems):
            send.start()
        token[...] = jnp.zeros_like(token)

    landing = lax.empty(src.shape, src.dtype)
    return pl.pallas_call(
        body, name=name,
        out_shape=(pltpu.SemaphoreType.DMA((N_PEERS,)), pltpu.SemaphoreType.DMA((N_PEERS,)),
                   pltpu.HBM(src.shape, src.dtype), pltpu.HBM(src.shape, src.dtype),
                   jax.ShapeDtypeStruct((SUBLANES, LANES), F32)),
        in_specs=(HBM, HBM), out_specs=(SEM, SEM, HBM, HBM, pl.BlockSpec(memory_space=pltpu.VMEM)),
        input_output_aliases={0: 2, 1: 3},
        compiler_params=pltpu.CompilerParams(has_side_effects=pltpu.SideEffectType.DATAFLOW_SIDE_EFFECTING))(
            pltpu.with_memory_space_constraint(src, pltpu.HBM), pltpu.with_memory_space_constraint(landing, pltpu.HBM))


def _scatter_end(send_sems, recv_sems, src_thru, land_thru, after, name):
    def body(src_ref, land_ref, send_sems, recv_sems, after_ref, src_dead, land_out):
        for send, arrival in _slab_copies(src_ref, land_ref, send_sems, recv_sems):
            send.wait_send()
            arrival.wait_recv()

    return pl.pallas_call(
        body, name=name,
        out_shape=(pltpu.HBM(src_thru.shape, src_thru.dtype), pltpu.HBM(land_thru.shape, land_thru.dtype)),
        in_specs=(HBM, HBM, SEM, SEM, HBM), out_specs=(HBM, HBM), input_output_aliases={0: 0, 1: 1},
        compiler_params=pltpu.CompilerParams(has_side_effects=pltpu.SideEffectType.DATAFLOW_SIDE_EFFECTING))(
            src_thru, land_thru, send_sems, recv_sems, after)


def _in_proj(x, g1, w_in_t, tm=512):
    T = x.shape[0]

    def body(x_ref, g_ref, w_ref, proj_ref, a_ref):
        xv = x_ref[...]
        a = (xv * _rstd(xv) * g_ref[...]).astype(BF)
        a_ref[...] = a
        proj_ref[...] = _dot_nt(a, w_ref[...])

    return pl.pallas_call(
        body, grid=(T // tm,),
        in_specs=[pl.BlockSpec((tm, D_MODEL), _row), pl.BlockSpec((1, D_MODEL), _fixed),
                  pl.BlockSpec((PROJ, D_MODEL), _fixed)],
        out_specs=[pl.BlockSpec((tm, PROJ), _row), pl.BlockSpec((tm, D_MODEL), _row)],
        out_shape=[SDS((T, PROJ), F32), SDS((T, D_MODEL), BF)],
        name="in_proj", compiler_params=_params(("arbitrary",), 48))(x, g1, w_in_t)


ATTN_GROUP = 16


def _attn_bias(sl_ref, bias_ref):
    qi = lax.broadcasted_iota(jnp.int32, (QBLK, QBLK), 0)
    kj = lax.broadcasted_iota(jnp.int32, (QBLK, QBLK), 1)
    step = qi - kj
    for di, d in enumerate(DILATIONS):
        for j in range(2):
            sl = sl_ref[0, j:j + 1, :]
            cur = jnp.where(step >= 0, -sl * (step * d).astype(F32), NEG)
            prev = jnp.where(step <= 0, -sl * ((step + QBLK) * d).astype(F32), NEG)
            rows = slice(j * QBLK, (j + 1) * QBLK)
            bias_ref[di * 2, rows, :QBLK] = prev
            bias_ref[di * 2, rows, QBLK:] = cur
            bias_ref[di * 2 + 1, rows, :QBLK] = jnp.full((QBLK, QBLK), NEG, F32)
            bias_ref[di * 2 + 1, rows, QBLK:] = cur


def _stack_heads(x, lo):
    return jnp.concatenate([jnp.where(lo, x, 0.0), jnp.where(lo, 0.0, x)], axis=0).astype(BF)


def _unstack_heads(x, lo):
    return jnp.where(lo, x[:QBLK], x[QBLK:])


def _attn_rows(start, d, blocks=1):
    if d == 1:
        return pl.ds(pl.multiple_of(start, QBLK), blocks * QBLK)
    return pl.ds(start, blocks * QBLK, stride=d)


def _attn_group_index(i, nblk, d, group):
    per = nblk // d // group
    r = i // per
    n0 = (i % per) * group
    start = r + (d * QBLK) * n0
    pstart = jnp.maximum(start - d * QBLK, r)
    return start, pstart, n0 == 0


def _attn_plan(nblk, d):
    group = min(ATTN_GROUP, nblk // d)
    return group, max(1, min(ATTN_GROUP // group, d))


def _attn_fwd(proj, slopes, to_gather):
    T = proj.shape[0]
    nblk = T // QBLK
    n_g = len(to_gather)

    def body(q_ref, k_ref, v_ref, sl_ref, *rest):
        srcs, (o_ref, m_ref), dsts = rest[:n_g], rest[n_g:n_g + 2], rest[n_g + 2:2 * n_g + 2]
        sems, (l_ref, bias_ref) = rest[2 * n_g + 2:2 * n_g + 5], rest[2 * n_g + 5:]
        h = pl.program_id(0)

        @pl.when(h == 0)
        def _():
            _Gather(list(zip(srcs, dsts)), sems).start()

        @pl.when(h == pl.num_programs(0) - 1)
        def _():
            _Gather(list(zip(srcs, dsts)), sems).forward()

        _attn_bias(sl_ref, bias_ref)
        lo = lax.broadcasted_iota(jnp.int32, (1, LANES), 1) < HEAD_DIM

        order = list(enumerate(DILATIONS))[::-1]
        for di, d in order:
            group, segs = _attn_plan(nblk, d)
            fresh, last = di == order[0][0], di == order[-1][0]

            def step(i, carry, segs=segs, **kw):
                for s in range(segs):
                    segment(i * segs + s, **kw)
                return carry

            def segment(i, d=d, di=di, group=group, fresh=fresh, last=last):
                start, pstart, first = _attn_group_index(i, nblk, d, group)
                prows = _attn_rows(pstart, d)
                alone = group == nblk // d
                k_prev = v_prev = None
                if not alone:
                    k_prev, v_prev = k_ref[prows, :].astype(BF), v_ref[prows, :].astype(BF)
                for b in range(group):
                    out = _attn_rows(start + b * (d * QBLK), d)
                    k_own, v_own = k_ref[out, :].astype(BF), v_ref[out, :].astype(BF)
                    if alone and b == 0:
                        k2, v2, bias = k_own, v_own, bias_ref[di * 2, :, QBLK:]
                    else:
                        k2, v2 = jnp.concatenate([k_prev, k_own], axis=0), jnp.concatenate([v_prev, v_own], axis=0)
                        bias = bias_ref[di * 2 + first.astype(jnp.int32)] if b == 0 else bias_ref[di * 2]
                    k_prev, v_prev = k_own, v_own
                    s = _dot_nt(_stack_heads(q_ref[out, :] * (HEAD_DIM ** -0.5), lo), k2) + bias
                    m = jnp.max(s, axis=-1, keepdims=True)
                    pr = jnp.exp(s - m)
                    m_b = _unstack_heads(m, lo)
                    l_b = _unstack_heads(jnp.sum(pr, axis=-1, keepdims=True), lo)
                    o_b = _unstack_heads(_dot(pr.astype(BF), v2), lo)
                    if fresh:
                        m_ref[out, :] = m_b
                        l_ref[out, :] = l_b
                        o_ref[out, :] = o_b
                        continue
                    m_o = m_ref[out, :]
                    m_n = jnp.maximum(m_o, m_b)
                    wa, wb = jnp.exp(m_o - m_n), jnp.exp(m_b - m_n)
                    l_n = wa * l_ref[out, :] + wb * l_b
                    o_n = wa * o_ref[out, :] + wb * o_b
                    if last:
                        m_ref[out, :] = m_n + jnp.log(l_n)
                        o_ref[out, :] = o_n / l_n
                    else:
                        m_ref[out, :] = m_n
                        l_ref[out, :] = l_n
                        o_ref[out, :] = o_n

            lax.fori_loop(0, nblk // (group * segs), step, 0)

        @pl.when(h == pl.num_programs(0) - 1)
        def _():
            _Gather(list(zip(srcs, dsts)), sems).finish()

    col = lambda base: pl.BlockSpec((T, LANES), lambda h: (0, base + h))
    tok = pl.BlockSpec((T, LANES), lambda h: (0, h))
    outs = pl.pallas_call(
        body, grid=(4,),
        in_specs=[col(0), col(4), col(8), pl.BlockSpec((1, 8, LANES), lambda h: (h, 0, 0))] + [HBM] * n_g,
        out_specs=[tok, tok] + [HBM] * n_g,
        out_shape=[SDS((T, ATTN_W), F32), SDS((T, ATTN_W), F32)]
        + [SDS((N_DEV,) + g.shape, g.dtype) for g in to_gather],
        scratch_shapes=_scatter_sems(n_g) + [pltpu.VMEM((T, LANES), F32), pltpu.VMEM((6, 2 * QBLK, 2 * QBLK), F32)],
        name="attn_fwd", compiler_params=_params(("arbitrary",), 56))(proj, proj, proj, slopes, *to_gather)
    return outs[0], outs[1], outs[2:]


def _sgu_norm(zv, ln_g, ln_b):
    gz, tz = _gelu(zv)
    mu = jnp.mean(gz, axis=-1, keepdims=True)
    xc = gz - mu
    rs = lax.rsqrt(jnp.mean(xc * xc, axis=-1, keepdims=True) + EPS)
    xhat = xc * rs
    return xhat * ln_g + ln_b, xhat, rs, tz


def _causal(w):
    i = lax.broadcasted_iota(jnp.int32, (CHUNK, CHUNK), 0)
    j = lax.broadcasted_iota(jnp.int32, (CHUNK, CHUNK), 1)
    return jnp.where(i >= j, w, 0.0)


def _sgu_fwd(proj, ln_g, ln_b, w_s, b_st, tm=512):
    T = proj.shape[0]

    def body(u_ref, z_ref, g_ref, b_ref, w_ref, bs_ref, out_ref):
        for g in range(N_GROUPS):
            wm = _causal(w_ref[g]).astype(BF)
            cols = slice(g * GROUP_DIM, (g + 1) * GROUP_DIM)
            for c in range(tm // CHUNK):
                rows = slice(c * CHUNK, (c + 1) * CHUNK)
                zn, _, _, _ = _sgu_norm(z_ref[rows, cols], g_ref[...], b_ref[...])
                mixed = _dot(wm, zn.astype(BF)) + bs_ref[:, g:g + 1]
                gu, _ = _gelu(u_ref[rows, cols])
                out_ref[rows, cols] = gu * mixed

    return pl.pallas_call(
        body, grid=(T // tm,),
        in_specs=[pl.BlockSpec((tm, SGU_W), lambda i: (i, 3)), pl.BlockSpec((tm, SGU_W), lambda i: (i, 4)),
                  pl.BlockSpec((1, GROUP_DIM), _fixed), pl.BlockSpec((1, GROUP_DIM), _fixed),
                  pl.BlockSpec((N_GROUPS, CHUNK, CHUNK), lambda i: (0, 0, 0)), pl.BlockSpec((CHUNK, LANES), _fixed)],
        out_specs=pl.BlockSpec((tm, SGU_W), _row),
        out_shape=SDS((T, SGU_W), F32),
        name="sgu_fwd", compiler_params=_params(("arbitrary",), 32))(proj, proj, ln_g, ln_b, w_s, b_st)


def _out_proj(attn, sgu, x, g_a, g_s, w_out, g_pm, g_pf, tm=512):
    T = x.shape[0]

    def body(a_ref, s_ref, x_ref, ga_ref, gs_ref, w_ref, gpm_ref, gpf_ref, grp_ref, mixed_ref, h1_ref, f_ref):
        av, sv = a_ref[...], s_ref[...]
        an = (av * _rstd(av) * ga_ref[...]).astype(BF)
        sn = (sv * _rstd(sv) * gs_ref[...]).astype(BF)
        grp_ref[:, :ATTN_W] = an
        grp_ref[:, ATTN_W:] = sn
        mixed = _dot(an, w_ref[:ATTN_W, :]) + _dot(sn, w_ref[ATTN_W:, :])
        mixed_ref[...] = mixed
        h1 = x_ref[...] + mixed * _rstd(mixed) * gpm_ref[...]
        h1_ref[...] = h1
        f_ref[...] = (h1 * _rstd(h1) * gpf_ref[...]).astype(BF)

    tok = lambda w: pl.BlockSpec((tm, w), _row)
    vec = lambda w: pl.BlockSpec((1, w), _fixed)
    return pl.pallas_call(
        body, grid=(T // tm,),
        in_specs=[tok(ATTN_W), tok(SGU_W), tok(D_MODEL), vec(ATTN_W), vec(SGU_W),
                  pl.BlockSpec((D_MODEL, D_MODEL), _fixed), vec(D_MODEL), vec(D_MODEL)],
        out_specs=[tok(D_MODEL)] * 4,
        out_shape=[SDS((T, D_MODEL), BF), SDS((T, D_MODEL), F32), SDS((T, D_MODEL), F32), SDS((T, D_MODEL), BF)],
        name="out_proj", compiler_params=_params(("arbitrary",), 48))(attn, sgu, x, g_a, g_s, w_out, g_pm, g_pf)


FF_TILE = 1408
FF_TILES = D_FF // FF_TILE
FF_CHUNK = 256


def _gate_up(f, w_gu_t, tm=512):
    T = f.shape[0]
    tn = FF_TILE

    def body(f_ref, wg_ref, wu_ref, g_ref, u_ref, act_ref):
        fv = f_ref[...]
        g = _dot_nt(fv, wg_ref[...])
        u = _dot_nt(fv, wu_ref[...])
        g_ref[...] = g.astype(BF)
        u_ref[...] = u.astype(BF)
        act_ref[...] = (g * _sigmoid(g) * u).astype(BF)

    ospec = pl.BlockSpec((tm, tn), lambda j, i: (i, j))
    return pl.pallas_call(
        body, grid=(FF_TILES, T // tm),
        in_specs=[pl.BlockSpec((tm, D_MODEL), lambda j, i: (i, 0)), pl.BlockSpec((tn, D_MODEL), lambda j, i: (j, 0)),
                  pl.BlockSpec((tn, D_MODEL), lambda j, i: (j + FF_TILES, 0))],
        out_specs=[ospec] * 3, out_shape=[SDS((T, D_FF), BF)] * 3,
        name="gate_up", compiler_params=_params(("arbitrary", "arbitrary"), 40))(f, w_gu_t, w_gu_t)


def _down_proj(act, w_down, h1, g_pff, tm=512):
    T = act.shape[0]

    def body(a_ref, w_ref, h1_ref, g_ref, y_ref, h2_ref):
        y = _dot(a_ref[...], w_ref[...])
        y_ref[...] = y
        h2_ref[...] = h1_ref[...] + y * _rstd(y) * g_ref[...]

    return pl.pallas_call(
        body, grid=(T // tm,),
        in_specs=[pl.BlockSpec((tm, D_FF), _row), pl.BlockSpec((D_FF, D_MODEL), _fixed),
                  pl.BlockSpec((tm, D_MODEL), _row), pl.BlockSpec((1, D_MODEL), _fixed)],
        out_specs=[pl.BlockSpec((tm, D_MODEL), _row)] * 2,
        out_shape=[SDS((T, D_MODEL), F32)] * 2,
        name="down_proj", compiler_params=_params(("arbitrary",), 48))(act, w_down, h1, g_pff)


def _pe_loss_and_bwd(h2, p, target, y, w_peg, b_peg, w_pep_t, g_pff, tm=512):
    T = h2.shape[0]

    def body(h2_ref, p_ref, t_ref, y_ref, wg_ref, b_ref, wp_ref, g_ref,
             dh2_ref, dy_ref, gpeg_ref, gpep_ref, loss_ref, db_ref, dg_ref):
        _acc_init(pl.program_id(0), gpeg_ref, gpep_ref, loss_ref, db_ref, dg_ref)
        h2v = h2_ref[...]
        h2b = h2v.astype(BF)
        pb = p_ref[...].astype(BF)
        gate = _sigmoid(_dot(h2b, wg_ref[...]) + b_ref[...])
        pp = _dot_nt(pb, wp_ref[...])
        diff = h2v + gate * pp - t_ref[...]
        loss_ref[...] += _colsum(diff * diff)
        dh3 = diff * (1.0 / D_MODEL)
        dpre = dh3 * pp * (gate * (1.0 - gate))
        dpre_b = dpre.astype(BF)
        db_ref[...] += _colsum(dpre)
        gpeg_ref[...] += _dot_tn(h2b, dpre_b)
        gpep_ref[...] += _dot_tn((dh3 * gate).astype(BF), pb)
        dh2 = dh3 + _dot_nt(dpre_b, wg_ref[...])
        dh2_ref[...] = dh2
        dy, dg = _rms_bwd(dh2, y_ref[...], g_ref[...])
        dy_ref[...] = dy.astype(BF)
        dg_ref[...] += _colsum(dg)

    tok = lambda w: pl.BlockSpec((tm, w), _row)
    vec = pl.BlockSpec((1, D_MODEL), _fixed)
    wg = pl.BlockSpec((D_MODEL, D_MODEL), _fixed)
    wp = pl.BlockSpec((D_MODEL, PLE_DIM), _fixed)
    return pl.pallas_call(
        body, grid=(T // tm,),
        in_specs=[tok(D_MODEL), tok(PLE_DIM), tok(D_MODEL), tok(D_MODEL), wg, vec, wp, vec],
        out_specs=[tok(D_MODEL), tok(D_MODEL), wg, wp, vec, vec, vec],
        out_shape=[SDS((T, D_MODEL), F32), SDS((T, D_MODEL), BF), SDS((D_MODEL, D_MODEL), F32),
                   SDS((D_MODEL, PLE_DIM), F32)] + [SDS((1, D_MODEL), F32)] * 3,
        name="pe_loss_and_bwd", compiler_params=_params(("arbitrary",), 56))(
            h2, p, target, y, w_peg, b_peg, w_pep_t, g_pff)


def _weight_grad(a, dy, name, into=None, row_tile=0, rows=None, tk=1024):
    n = dy.shape[1]
    tn = min(n, 1024)
    T, ka = a.shape
    tka = FF_TILE if ka == D_FF else min(ka, 1024)
    rows = ka if rows is None else rows

    def body(a_ref, dy_ref, *rest):
        out_ref = rest[-1]
        _acc_init(pl.program_id(2), out_ref)
        out_ref[...] += _dot_tn(a_ref[...].astype(BF), dy_ref[...].astype(BF))

    carried = [] if into is None else [into]
    return pl.pallas_call(
        body, grid=(ka // tka, n // tn, T // tk),
        in_specs=[pl.BlockSpec((tk, tka), lambda i, j, k: (k, i)), pl.BlockSpec((tk, tn), lambda i, j, k: (k, j))]
        + [HBM] * len(carried),
        out_specs=pl.BlockSpec((tka, tn), lambda i, j, k: (i + row_tile, j)),
        out_shape=SDS((rows, n), F32), input_output_aliases={2: 0} if carried else {},
        name="grad_" + name, compiler_params=_params(("arbitrary",) * 3, 40))(a, dy, *carried)


def _grad_w_in(dparts, a, tk=1024):
    T = a.shape[0]

    def body(*refs):
        d_refs, a_ref, out_ref, acc_ref = refs[:len(dparts)], refs[-3], refs[-2], refs[-1]
        k = pl.program_id(0)
        _acc_init(k, acc_ref)
        cols = [r[part].astype(BF) for r in d_refs for part in range(r.shape[0])]
        acc_ref[...] += _dot_tn(jnp.concatenate(cols, axis=1), a_ref[...])

        @pl.when(k == pl.num_programs(0) - 1)
        def _():
            out_ref[...] = acc_ref[...].astype(BF)

    return pl.pallas_call(
        body, grid=(T // tk,),
        in_specs=[pl.BlockSpec((d.shape[0], tk, d.shape[2]), lambda k: (0, k, 0)) for d in dparts]
        + [pl.BlockSpec((tk, D_MODEL), lambda k: (k, 0))],
        out_specs=pl.BlockSpec((PROJ, D_MODEL), lambda k: (0, 0)),
        out_shape=SDS((PROJ, D_MODEL), BF), scratch_shapes=[pltpu.VMEM((PROJ, D_MODEL), F32)],
        name="grad_w_in", compiler_params=_params(("arbitrary",), 48))(*dparts, a)


def _down_bwd(dy, w_down, g, u, to_send, tm=512):
    T = dy.shape[0]
    n_s = len(to_send)

    def body(dy_ref, w_ref, g_ref, u_ref, *rest):
        srcs, (dg_ref, du_ref), dsts, sems = rest[:n_s], rest[n_s:n_s + 2], rest[n_s + 2:2 * n_s + 2], rest[2 * n_s + 2:]
        i = pl.program_id(0)
        items = list(zip(srcs, dsts, [True] * n_s))

        @pl.when(i == 0)
        def _():
            _Scatter(items, sems).start()

        dyv = dy_ref[...]
        for c in range(D_FF // FF_CHUNK):
            cols = slice(c * FF_CHUNK, (c + 1) * FF_CHUNK)
            dact = _dot_nt(dyv, w_ref[cols, :]).astype(BF)
            gv, uv = g_ref[:, cols], u_ref[:, cols]
            s = _sigmoid(gv)
            ds = dact * s
            dg_ref[:, cols] = ds * uv * (1.0 + gv * (1.0 - s))
            du_ref[:, cols] = ds * gv

        @pl.when(i == pl.num_programs(0) - 1)
        def _():
            _Scatter(items, sems).wait()

    tile = pl.BlockSpec((tm, D_FF), _row)
    outs = pl.pallas_call(
        body, grid=(T // tm,),
        in_specs=[pl.BlockSpec((tm, D_MODEL), _row),
                  pl.BlockSpec((D_FF, D_MODEL), _fixed, pipeline_mode=pl.Buffered(1)), tile, tile] + [HBM] * n_s,
        out_specs=[tile, tile] + [HBM] * n_s,
        out_shape=[SDS((T, D_FF), BF)] * 2 + [SDS(s.shape, s.dtype) for s in to_send],
        scratch_shapes=_scatter_sems(n_s),
        name="down_bwd", compiler_params=_params(("arbitrary",), 48))(dy, w_down, g, u, *to_send)
    return outs[0], outs[1], outs[2:]


def _ffn_in_bwd(dg, du, w_gu_t, h1, dh2, mixed, g_pf, g_pm, to_send, tm=512):
    T = h1.shape[0]
    n_s = len(to_send)

    def body(dg_ref, du_ref, w_ref, h1_ref, dh2_ref, mx_ref, gpf_ref, gpm_ref, *rest):
        srcs, outs, dsts, sems = rest[:n_s], rest[n_s:n_s + 4], rest[n_s + 4:2 * n_s + 4], rest[2 * n_s + 4:]
        dh1_ref, dmx_ref, dgpf_ref, dgpm_ref = outs
        i = pl.program_id(0)
        items = list(zip(srcs, dsts, [True] * n_s))
        _acc_init(i, dgpf_ref, dgpm_ref)

        @pl.when(i == 0)
        def _():
            _Scatter(items, sems).start()

        df = _dot(jnp.concatenate([dg_ref[...], du_ref[...]], axis=1), w_ref[...])
        dx, dgf = _rms_bwd(df, h1_ref[...], gpf_ref[...])
        dh1 = dh2_ref[...] + dx
        dh1_ref[...] = dh1
        dmx, dgm = _rms_bwd(dh1, mx_ref[...], gpm_ref[...])
        dmx_ref[...] = dmx.astype(BF)
        dgpf_ref[...] += _colsum(dgf)
        dgpm_ref[...] += _colsum(dgm)

        @pl.when(i == pl.num_programs(0) - 1)
        def _():
            _Scatter(items, sems).wait()

    tok = lambda w: pl.BlockSpec((tm, w), _row)
    vec = pl.BlockSpec((1, D_MODEL), _fixed)
    outs = pl.pallas_call(
        body, grid=(T // tm,),
        in_specs=[tok(D_FF), tok(D_FF), pl.BlockSpec((2 * D_FF, D_MODEL), _fixed, pipeline_mode=pl.Buffered(1)),
                  tok(D_MODEL), tok(D_MODEL), tok(D_MODEL), vec, vec]
        + [HBM] * n_s,
        out_specs=[tok(D_MODEL), tok(D_MODEL), vec, vec] + [HBM] * n_s,
        out_shape=[SDS((T, D_MODEL), F32), SDS((T, D_MODEL), BF), SDS((1, D_MODEL), F32), SDS((1, D_MODEL), F32)]
        + [SDS(s.shape, s.dtype) for s in to_send],
        scratch_shapes=_scatter_sems(n_s),
        name="ffn_in_bwd", compiler_params=_params(("arbitrary",), 56))(
            dg, du, w_gu_t, h1, dh2, mixed, g_pf, g_pm, *to_send)
    return outs[0], outs[1], outs[2], outs[3], outs[4:]


STAT_LANES = HEAD_DIM // 2


def _ff_grad_spec(half):
    return pl.BlockSpec((D_FF, D_MODEL), lambda i: (half, 0), pipeline_mode=pl.Buffered(1))


def _out_bwd(dmx, w_out, attn, lse, sgu, g_a, g_s, dg, f, tm=512):
    T = attn.shape[0]

    def body(dm_ref, w_ref, a_ref, l_ref, s_ref, ga_ref, gs_ref, dgate_ref, f_ref,
             da_ref, st_ref, ds_ref, dga_ref, dgs_ref, ggu_ref):
        _acc_init(pl.program_id(0), dga_ref, dgs_ref, ggu_ref)
        ggu_ref[...] += _dot_tn(dgate_ref[...], f_ref[...])
        dgr = _dot_nt(dm_ref[...], w_ref[...])
        av = a_ref[...]
        da, dga = _rms_bwd(dgr[:, :ATTN_W], av, ga_ref[...])
        ds, dgs = _rms_bwd(dgr[:, ATTN_W:], s_ref[...], gs_ref[...])
        da_ref[...] = da
        ds_ref[...] = ds
        dga_ref[...] += _colsum(dga)
        dgs_ref[...] += _colsum(dgs)
        lane = lax.broadcasted_iota(jnp.int32, (1, LANES), 1)
        lo = lane < HEAD_DIM
        first = (lane % HEAD_DIM) < STAT_LANES
        prod = da * av
        for c in range(ATTN_W // LANES):
            cols = slice(c * LANES, (c + 1) * LANES)
            pc = prod[:, cols]
            delta = jnp.where(lo, jnp.sum(jnp.where(lo, pc, 0.0), axis=-1, keepdims=True),
                              jnp.sum(jnp.where(lo, 0.0, pc), axis=-1, keepdims=True))
            st_ref[:, cols] = jnp.where(first, l_ref[:, cols], delta)

    tok = lambda w: pl.BlockSpec((tm, w), _row)
    vec = lambda w: pl.BlockSpec((1, w), _fixed)
    return pl.pallas_call(
        body, grid=(T // tm,),
        in_specs=[tok(D_MODEL), pl.BlockSpec((D_MODEL, D_MODEL), _fixed), tok(ATTN_W), tok(ATTN_W), tok(SGU_W),
                  vec(ATTN_W), vec(SGU_W), tok(D_FF), tok(D_MODEL)],
        out_specs=[tok(ATTN_W), tok(ATTN_W), tok(SGU_W), vec(ATTN_W), vec(SGU_W), _ff_grad_spec(0)],
        out_shape=[SDS((T, ATTN_W), F32), SDS((T, ATTN_W), F32), SDS((T, SGU_W), F32), SDS((1, ATTN_W), F32),
                   SDS((1, SGU_W), F32), SDS((2 * D_FF, D_MODEL), F32)],
        name="out_bwd", compiler_params=_params(("arbitrary",), 56))(dmx, w_out, attn, lse, sgu, g_a, g_s, dg, f)


def _sgu_bwd(proj, dsgu, ln_g, ln_b, w_s, b_st, groups, dmx, d_up, f, g_gu_t, tm=512):
    T = proj.shape[0]

    def body(u_ref, z_ref, ds_ref, g_ref, b_ref, w_ref, bs_ref, grp_ref, dmx_ref, dup_ref, f_ref, _,
             duz_ref, dw_ref, dbs_ref, dlg_ref, dlb_ref, gout_ref, ggu_ref, dbacc_ref):
        du_ref, dz_ref = duz_ref.at[0], duz_ref.at[1]
        step = pl.program_id(0)
        _acc_init(step, dw_ref, dbs_ref, dlg_ref, dlb_ref, gout_ref, ggu_ref, dbacc_ref)
        gout_ref[...] += _dot_tn(grp_ref[...], dmx_ref[...])
        ggu_ref[...] += _dot_tn(dup_ref[...], f_ref[...])
        lng, lnb = g_ref[...], b_ref[...]
        for g in range(N_GROUPS):
            wm = _causal(w_ref[g]).astype(BF)
            cols = slice(g * GROUP_DIM, (g + 1) * GROUP_DIM)
            for c in range(tm // CHUNK):
                rows = slice(c * CHUNK, (c + 1) * CHUNK)
                zv, uv, dout = z_ref[rows, cols], u_ref[rows, cols], ds_ref[rows, cols]
                zn, xhat, rs, tz = _sgu_norm(zv, lng, lnb)
                znb = zn.astype(BF)
                mixed = _dot(wm, znb) + bs_ref[:, g:g + 1]
                gu, tu = _gelu(uv)
                du_ref[rows, cols] = (dout * mixed * _gelu_grad(uv, tu)).astype(BF)
                dmix = dout * gu
                dmb = dmix.astype(BF)
                dw_ref[g] += _causal(_dot_nt(dmb, znb))
                dbacc_ref[g] += dmix
                dzn = _dot_tn(wm, dmb)
                dlg_ref[...] += _colsum(dzn * xhat)
                dlb_ref[...] += _colsum(dzn)
                dxh = dzn * lng
                dgz = rs * (dxh - jnp.mean(dxh, axis=-1, keepdims=True)
                            - xhat * jnp.mean(dxh * xhat, axis=-1, keepdims=True))
                dz_ref[rows, cols] = (dgz * _gelu_grad(zv, tz)).astype(BF)

        @pl.when(step == pl.num_programs(0) - 1)
        def _():
            lane = lax.broadcasted_iota(jnp.int32, (CHUNK, LANES), 1)
            acc = jnp.zeros((CHUNK, LANES), F32)
            for g in range(N_GROUPS):
                acc = jnp.where(lane == g, jnp.sum(dbacc_ref[g], axis=-1, keepdims=True), acc)
            dbs_ref[...] = acc

    tok = pl.BlockSpec((tm, SGU_W), _row)
    vec = pl.BlockSpec((1, GROUP_DIM), _fixed)
    wsp = pl.BlockSpec((N_GROUPS, CHUNK, CHUNK), lambda i: (0, 0, 0))
    sq = pl.BlockSpec((CHUNK, LANES), _fixed)
    wide = pl.BlockSpec((tm, D_MODEL), _row)
    return pl.pallas_call(
        body, grid=(T // tm,),
        in_specs=[pl.BlockSpec((tm, SGU_W), lambda i: (i, 3)), pl.BlockSpec((tm, SGU_W), lambda i: (i, 4)), tok,
                  vec, vec, wsp, sq, wide, wide, pl.BlockSpec((tm, D_FF), _row), wide, HBM],
        out_specs=[pl.BlockSpec((2, tm, SGU_W), lambda i: (0, i, 0)), wsp, sq, vec, vec,
                   pl.BlockSpec((D_MODEL, D_MODEL), _fixed), _ff_grad_spec(1)],
        out_shape=[SDS((2, T, SGU_W), BF), SDS((N_GROUPS, CHUNK, CHUNK), F32),
                   SDS((CHUNK, LANES), F32), SDS((1, GROUP_DIM), F32), SDS((1, GROUP_DIM), F32),
                   SDS((D_MODEL, D_MODEL), F32), SDS((2 * D_FF, D_MODEL), F32)],
        input_output_aliases={11: 6},
        scratch_shapes=[pltpu.VMEM((N_GROUPS, CHUNK, LANES), F32)],
        name="sgu_bwd", compiler_params=_params(("arbitrary",), 56))(
            proj, proj, dsgu, ln_g, ln_b, w_s, b_st, groups, dmx, d_up, f, g_gu_t)


def _attn_bwd(proj, do, stats, slopes, to_send, slabbed):
    T = proj.shape[0]
    nblk = T // QBLK
    n_s = len(to_send)

    def body(q_ref, k_ref, v_ref, do_ref, st_ref, sl_ref, *rest):
        srcs, d_ref, dsts = rest[:n_s], rest[n_s], rest[n_s + 1:2 * n_s + 1]
        sems, bias_ref = rest[2 * n_s + 1:2 * n_s + 4], rest[2 * n_s + 4]
        dq_ref, dk_ref, dv_ref = d_ref.at[0], d_ref.at[1], d_ref.at[2]
        h = pl.program_id(0)
        items = list(zip(srcs, dsts, slabbed))

        @pl.when(h == 0)
        def _():
            _Scatter(items, sems).start()

        _attn_bias(sl_ref, bias_ref)
        lo = lax.broadcasted_iota(jnp.int32, (1, LANES), 1) < HEAD_DIM
        scale = HEAD_DIM ** -0.5
        d_ref[...] = jnp.zeros_like(d_ref)

        for di, d in enumerate(DILATIONS):
            group, segs = _attn_plan(nblk, d)

            def step(i, carry, segs=segs, **kw):
                for s in range(segs):
                    segment(i * segs + s, **kw)
                return carry

            def segment(i, d=d, di=di, group=group):
                start, pstart, first = _attn_group_index(i, nblk, d, group)
                rows, prows = _attn_rows(start, d, group), _attn_rows(pstart, d)
                alone = group == nblk // d
                q = q_ref[rows, :] * scale
                if alone:
                    k, v, own0 = k_ref[rows, :].astype(BF), v_ref[rows, :].astype(BF), 0
                else:
                    k = jnp.concatenate([k_ref[prows, :], k_ref[rows, :]], axis=0).astype(BF)
                    v = jnp.concatenate([v_ref[prows, :], v_ref[rows, :]], axis=0).astype(BF)
                    own0 = QBLK
                dov = do_ref[rows, :]
                stats = st_ref[rows, :]
                masks = [lo, ~lo]
                qm = [jnp.where(masks[j], q, 0.0).astype(BF) for j in range(2)]
                dom = [jnp.where(masks[j], dov, 0.0).astype(BF) for j in range(2)]
                for b in range(group):
                    qb = slice(b * QBLK, (b + 1) * QBLK)
                    own_only = alone and b == 0
                    kb = slice(own0 + (b if own_only else b - 1) * QBLK, own0 + (b + 1) * QBLK)
                    which = di * 2 + first.astype(jnp.int32) if b == 0 and not own_only else di * 2
                    dq_parts, prs, dss = [], [], []
                    for j in range(2):
                        bias = bias_ref[which, j * QBLK:(j + 1) * QBLK, QBLK if own_only else 0:]
                        lj = stats[qb, j * HEAD_DIM:j * HEAD_DIM + 1]
                        delta = stats[qb, j * HEAD_DIM + STAT_LANES:j * HEAD_DIM + STAT_LANES + 1]
                        pr = jnp.exp(_dot_nt(qm[j][qb], k[kb]) + bias - lj)
                        ds = (pr * (_dot_nt(dom[j][qb], v[kb]) - delta)).astype(BF)
                        dq_parts.append(_dot(ds, k[kb]))
                        prs.append(pr.astype(BF))
                        dss.append(ds)
                    dk_b = _dot_tn(jnp.concatenate(dss, axis=0), jnp.concatenate([qm[0][qb], qm[1][qb]], axis=0))
                    dv_b = _dot_tn(jnp.concatenate(prs, axis=0), jnp.concatenate([dom[0][qb], dom[1][qb]], axis=0))
                    own = _attn_rows(start + b * (d * QBLK), d)
                    dq_ref[own, :] += jnp.where(lo, dq_parts[0], dq_parts[1]) * scale
                    if own_only:
                        dk_ref[own, :] += dk_b
                        dv_ref[own, :] += dv_b
                    elif b == 0:
                        dk_ref[prows, :] += dk_b[:QBLK]
                        dv_ref[prows, :] += dv_b[:QBLK]
                        dk_ref[own, :] += dk_b[QBLK:]
                        dv_ref[own, :] += dv_b[QBLK:]
                    else:
                        two = _attn_rows(start + (b - 1) * (d * QBLK), d, 2)
                        dk_ref[two, :] += dk_b
                        dv_ref[two, :] += dv_b

            lax.fori_loop(0, nblk // (group * segs), step, 0)

        @pl.when(h == pl.num_programs(0) - 1)
        def _():
            _Scatter(items, sems).wait()

    col = lambda base: pl.BlockSpec((T, LANES), lambda h: (0, base + h))
    outs = pl.pallas_call(
        body, grid=(4,),
        in_specs=[col(0), col(4), col(8), col(0), col(0), pl.BlockSpec((1, 8, LANES), lambda h: (h, 0, 0))]
        + [HBM] * n_s,
        out_specs=[pl.BlockSpec((3, T, LANES), lambda h: (0, 0, h), pipeline_mode=pl.Buffered(1))] + [HBM] * n_s,
        out_shape=[SDS((3, T, ATTN_W), F32)]
        + [SDS(s.shape if sl else (N_DEV,) + s.shape, s.dtype) for s, sl in zip(to_send, slabbed)],
        scratch_shapes=_scatter_sems(n_s) + [pltpu.VMEM((6, 2 * QBLK, 2 * QBLK), F32)],
        name="attn_bwd", compiler_params=_params(("arbitrary",), 60))(proj, proj, proj, do, stats, slopes, *to_send)
    return outs[0], outs[1:]


def _in_bwd(dparts, w_in_t, x, dh1, g1, tm=512):
    T = x.shape[0]
    n = len(dparts)
    w = ATTN_W

    def body(*refs):
        d_refs, (w_ref, x_ref, dh1_ref, g_ref, dx_ref, dg_ref) = refs[:n], refs[n:]
        _acc_init(pl.program_id(0), dg_ref)
        d_proj = jnp.concatenate([r[part].astype(BF) for r in d_refs for part in range(r.shape[0])], axis=1)
        da = _dot(d_proj, w_ref[...])
        dx, dg = _rms_bwd(da, x_ref[...], g_ref[...])
        dx_ref[...] = dh1_ref[...] + dx
        dg_ref[...] += _colsum(dg)

    tok = lambda c: pl.BlockSpec((tm, c), _row)
    vec = pl.BlockSpec((1, D_MODEL), _fixed)
    return pl.pallas_call(
        body, grid=(T // tm,),
        in_specs=[pl.BlockSpec((d.shape[0], tm, w), lambda i: (0, i, 0)) for d in dparts]
        + [pl.BlockSpec((PROJ, D_MODEL), _fixed), tok(D_MODEL), tok(D_MODEL), vec],
        out_specs=[tok(D_MODEL), vec],
        out_shape=[SDS((T, D_MODEL), F32), SDS((1, D_MODEL), F32)],
        name="in_bwd", compiler_params=_params(("arbitrary",), 52))(*dparts, w_in_t, x, dh1, g1)


def _sum_parts(p_ref):
    g = p_ref[0].astype(F32)
    for s in range(1, N_DEV):
        g = g + p_ref[s].astype(F32)
    return g


def _adamw_math(g, w, m, v):
    nm = ADAM_B1 * m + (1.0 - ADAM_B1) * g
    nv = ADAM_B2 * v + (1.0 - ADAM_B2) * (g * g)
    m_hat = nm / (1.0 - ADAM_B1 ** ADAM_STEP)
    v_hat = nv / (1.0 - ADAM_B2 ** ADAM_STEP)
    return -ADAM_LR * (m_hat / (jnp.sqrt(v_hat) + ADAM_EPS) + ADAM_WD * w), nm, nv


def _row_tile(rows):
    for t in (256, 176, 128, 80):
        if rows % t == 0:
            return t
    raise ValueError(rows)


def _reduce_adamw(parts, w, m, v, name):
    rows, width = w.shape
    tr = _row_tile(rows)

    def body(p_ref, w_ref, m_ref, v_ref, g_ref, d_ref, nm_ref, nv_ref):
        g = _sum_parts(p_ref)
        g_ref[...] = g
        d_ref[...], nm_ref[...], nv_ref[...] = _adamw_math(g, w_ref[...], m_ref[...], v_ref[...])

    blk = pl.BlockSpec((tr, width), _row)
    return pl.pallas_call(
        body, grid=(rows // tr,),
        in_specs=[pl.BlockSpec((N_DEV, tr, width), lambda i: (0, i, 0)), blk, blk, blk],
        out_specs=[blk] * 4, out_shape=[SDS((rows, width), F32)] * 4,
        name="adamw_" + name, compiler_params=_params(("arbitrary",), 32))(parts, w, m, v)


def _reduce(parts, name):
    _, rows, width = parts.shape
    tr = _row_tile(rows)

    def body(p_ref, g_ref):
        g_ref[...] = _sum_parts(p_ref)

    return pl.pallas_call(
        body, grid=(rows // tr,),
        in_specs=[pl.BlockSpec((N_DEV, tr, width), lambda i: (0, i, 0))],
        out_specs=pl.BlockSpec((tr, width), _row), out_shape=SDS((rows, width), F32),
        name="sum_" + name, compiler_params=_params(("arbitrary",), 32))(parts)


def _adamw(g, w, m, v, name):
    rows, width = w.shape
    tr = _row_tile(rows)

    def body(g_ref, w_ref, m_ref, v_ref, d_ref, nm_ref, nv_ref):
        d_ref[...], nm_ref[...], nv_ref[...] = _adamw_math(g_ref[...], w_ref[...], m_ref[...], v_ref[...])

    blk = pl.BlockSpec((tr, width), _row)
    return pl.pallas_call(
        body, grid=(rows // tr,), in_specs=[blk] * 4, out_specs=[blk] * 3, out_shape=[SDS((rows, width), F32)] * 3,
        name="adamw_" + name, compiler_params=_params(("arbitrary",), 32))(g, w, m, v)


SMALL = ("w_spatial", "ln_pre_mix", "ln_post_mix", "ln_pre_ffn", "ln_post_ffn", "b_pe_gate",
         "attn_out_norm", "sgu_out_norm", "b_spatial", "sgu_ln_g", "sgu_ln_b")
SMALL_GROUPS = ((128, ("w_spatial", "b_spatial", "sgu_ln_g", "sgu_ln_b")),
                (512, ("attn_out_norm", "sgu_out_norm")),
                (1024, ("ln_post_mix", "ln_pre_ffn", "ln_post_ffn", "b_pe_gate")))
SMALL_LATE = "ln_pre_mix"
SMALL_SIZE = dict(w_spatial=N_GROUPS * CHUNK * CHUNK, b_spatial=N_GROUPS * CHUNK, sgu_ln_g=GROUP_DIM, sgu_ln_b=GROUP_DIM,
                  attn_out_norm=ATTN_W, sgu_out_norm=SGU_W, ln_pre_mix=D_MODEL, ln_post_mix=D_MODEL, ln_pre_ffn=D_MODEL,
                  ln_post_ffn=D_MODEL, b_pe_gate=D_MODEL)
SUBLANES = 8
ROW_SHARDED = ("w_out", "w_down", "w_pe_gate")
COL_SHARDED = ("w_in", "w_gate_up", "w_pe_proj")
WEIGHTS = ("ln_pre_mix", "w_in", "sgu_ln_g", "sgu_ln_b", "w_spatial", "b_spatial", "attn_out_norm", "sgu_out_norm",
           "w_out", "ln_post_mix", "ln_pre_ffn", "w_gate_up", "w_down", "ln_post_ffn", "w_pe_gate", "b_pe_gate",
           "w_pe_proj")


def _group_rows(width, names, extra=0):
    rows = sum(SMALL_SIZE[n] // width for n in names) + extra
    return -(-rows // SUBLANES) * SUBLANES


def _pack_small_grads(gs, loss_term):
    packed = []
    for width, names in SMALL_GROUPS:
        rows = [gs[n].reshape(-1, width) for n in names]
        extra = int(width == D_MODEL)
        if extra:
            rows.append(jnp.full((1, width), loss_term, F32))
        used = sum(r.shape[0] for r in rows)
        rows.append(jnp.zeros((_group_rows(width, names, extra) - used, width), F32))
        packed.append(jnp.concatenate(rows, axis=0))
    return packed


def _small_adamw(arrived, arrived_late, w, m, v):
    names = [n for _, ns in SMALL_GROUPS for n in ns] + [SMALL_LATE]
    n_groups = len(SMALL_GROUPS)

    def body(*refs):
        group_refs, late_ref = refs[:n_groups], refs[n_groups]
        state = refs[n_groups + 1:n_groups + 1 + 3 * len(names)]
        outs = refs[n_groups + 1 + 3 * len(names):]
        sums = [_sum_parts(r) for r in group_refs]

        def update(name, g):
            i = names.index(name)
            w_ref, m_ref, v_ref = state[3 * i:3 * i + 3]
            delta, nm, nv = _adamw_math(g, w_ref[...].reshape(g.shape), m_ref[...].reshape(g.shape),
                                        v_ref[...].reshape(g.shape))
            for o_ref, val in zip(outs[4 * i:4 * i + 4], (g, delta, nm, nv)):
                o_ref[...] = val.reshape(o_ref.shape)

        for (width, group), total in zip(SMALL_GROUPS, sums):
            row = 0
            for name in group:
                rows = SMALL_SIZE[name] // width
                update(name, total[row:row + rows, :])
                row += rows
            if width == D_MODEL:
                outs[-1][...] = total[row:row + 1, :LANES]
        update(SMALL_LATE, _sum_parts(late_ref)[:1, :])

    state = [t[n] for n in names for t in (w, m, v)]
    plain = jax.ShapeDtypeStruct
    out_shape = [plain(w[n].shape, F32) for n in names for _ in range(4)] + [plain((1, LANES), F32)]
    outs = pl.pallas_call(body, out_shape=out_shape, name="adamw_small",
                          compiler_params=pltpu.CompilerParams(vmem_limit_bytes=32 * MIB))(*arrived, arrived_late, *state)
    return {n: outs[4 * i:4 * i + 4] for i, n in enumerate(names)}, outs[-1]


def _slabs(full):
    return full.reshape(N_DEV, full.shape[0] // N_DEV, full.shape[1])


def kernel(x, p, ln_pre_mix, w_in, sgu_ln_g, sgu_ln_b, w_spatial, b_spatial, attn_out_norm, sgu_out_norm, w_out, ln_post_mix, ln_pre_ffn, w_gate_up, w_down, ln_post_ffn, w_pe_gate, b_pe_gate, w_pe_proj, loss_target, m_ln_pre_mix, m_w_in, m_sgu_ln_g, m_sgu_ln_b, m_w_spatial, m_b_spatial, m_attn_out_norm, m_sgu_out_norm, m_w_out, m_ln_post_mix, m_ln_pre_ffn, m_w_gate_up, m_w_down, m_ln_post_ffn, m_w_pe_gate, m_b_pe_gate, m_w_pe_proj, v_ln_pre_mix, v_w_in, v_sgu_ln_g, v_sgu_ln_b, v_w_spatial, v_b_spatial, v_attn_out_norm, v_sgu_out_norm, v_w_out, v_ln_post_mix, v_ln_pre_ffn, v_w_gate_up, v_w_down, v_ln_post_ffn, v_w_pe_gate, v_b_pe_gate, v_w_pe_proj):
    given = dict(locals())
    w = {n: given[n] for n in WEIGHTS}
    m = {n: given["m_" + n] for n in WEIGHTS}
    v = {n: given["v_" + n] for n in WEIGHTS}
    xs, ps, target = x[0], p[0, 0], loss_target[0]

    shard = {n: w[n][0].astype(BF) for n in ROW_SHARDED}
    shard.update({n: w[n][0].T.astype(BF) for n in COL_SHARDED})
    sm = {n: w[n][0] for n in SMALL}
    sm = {n: (a.reshape(1, -1) if a.ndim == 1 else a) for n, a in sm.items()}
    slopes = jnp.broadcast_to((2.0 ** -(jnp.arange(8, dtype=F32) + 1.0)).reshape(4, 2, 1), (4, 2, LANES))
    slopes = jnp.concatenate([slopes, jnp.zeros((4, 6, LANES), F32)], axis=1)
    b_st = jnp.pad(sm["b_spatial"].T, ((0, 0), (0, LANES - N_GROUPS)))

    def full(gathered):
        return gathered.reshape(-1, gathered.shape[-1])

    w_in_t = full(_all_gather(shard["w_in"], "gather_w_in"))
    proj, a = _in_proj(xs, sm["ln_pre_mix"], w_in_t)
    later = ("w_out", "w_gate_up", "w_down", "w_pe_gate", "w_pe_proj")
    attn, lse, gathered = _attn_fwd(proj, slopes, [shard[n] for n in later])
    w_out_f, w_gu_t, w_down_f, w_peg_f, w_pep_t = [full(g) for g in gathered]
    sgu = _sgu_fwd(proj, sm["sgu_ln_g"], sm["sgu_ln_b"], sm["w_spatial"], b_st)
    groups, mixed, h1, f = _out_proj(attn, sgu, xs, sm["attn_out_norm"], sm["sgu_out_norm"], w_out_f,
                                     sm["ln_post_mix"], sm["ln_pre_ffn"])
    g, u, act = _gate_up(f, w_gu_t)
    y, h2 = _down_proj(act, w_down_f, h1, sm["ln_post_ffn"])
    dh2, dy, g_peg, g_pep_t, loss_cols, db_peg, d_pff = _pe_loss_and_bwd(
        h2, ps, target, y, w_peg_f, sm["b_pe_gate"], w_pep_t, sm["ln_post_ffn"])
    loss_term = 0.5 * jnp.sum(loss_cols) * (1.0 / D_MODEL)

    arrived = {}
    g_down = _weight_grad(act, dy, "w_down")
    dg, du, (arrived["w_pe_proj"], arrived["w_pe_gate"]) = _down_bwd(dy, w_down_f, g, u, [_slabs(g_pep_t), _slabs(g_peg)])
    dh1, dmx, d_pf, d_pm, (arrived["w_down"],) = _ffn_in_bwd(dg, du, w_gu_t, h1, dh2, mixed, sm["ln_pre_ffn"],
                                                            sm["ln_post_mix"], [_slabs(g_down)])
    dattn, stats, dsgu, d_ga, d_gs, g_gu_t = _out_bwd(dmx, w_out_f, attn, lse, sgu, sm["attn_out_norm"],
                                                      sm["sgu_out_norm"], dg, f)
    duz, d_ws, d_bst, d_lg, d_lb, g_out, g_gu_t = _sgu_bwd(proj, dsgu, sm["sgu_ln_g"], sm["sgu_ln_b"],
                                                           sm["w_spatial"], b_st, groups, dmx, du, f, g_gu_t)
    gs = dict(sgu_ln_g=d_lg, sgu_ln_b=d_lb, w_spatial=d_ws, b_spatial=d_bst[:, :N_GROUPS].T, attn_out_norm=d_ga,
              sgu_out_norm=d_gs, ln_post_mix=d_pm, ln_pre_ffn=d_pf, ln_post_ffn=d_pff, b_pe_gate=db_peg)
    small_grads = _pack_small_grads(gs, loss_term)
    dqkv, (arrived["w_gate_up"], arrived["w_out"], *arrived_small) = _attn_bwd(
        proj, dattn, stats, slopes, [_slabs(g_gu_t), _slabs(g_out), *small_grads],
        [True, True] + [False] * len(small_grads))
    send_sems, recv_sems, slabs, landing, token = _scatter_begin(_slabs(_grad_w_in([dqkv, duz], a)), "w_in_grad_send")
    grad_x, d_g1 = _in_bwd([dqkv, duz], w_in_t, xs, dh1, sm["ln_pre_mix"] + token[:1, :1])
    slabs, landing = _scatter_end(send_sems, recv_sems, slabs, landing, d_g1, "w_in_grad_arrive")
    me = 4 * lax.axis_index("x") + 2 * lax.axis_index("y") + lax.axis_index("c")
    own = lax.dynamic_slice_in_dim(slabs, me, 1, axis=0)
    arrived["w_in"] = lax.dynamic_update_slice_in_dim(landing, own, me, axis=0)
    (arrived_late,) = _scatter_call([jnp.pad(d_g1, ((0, SUBLANES - 1), (0, 0)))], [False], "ln_pre_mix_grad_exchange")

    res = {}
    for n in ROW_SHARDED:
        res[n] = _reduce_adamw(arrived[n], w[n][0], m[n][0], v[n][0], n)
    for n in ("w_in", "w_gate_up"):
        res[n] = [t.T for t in _reduce_adamw(arrived[n], w[n][0].T, m[n][0].T, v[n][0].T, n)]
    for n in ("w_pe_proj",):
        grad = _reduce(arrived[n], n).T
        res[n] = (grad, *_adamw(grad, w[n][0], m[n][0], v[n][0], n))
    small, loss_row = _small_adamw(arrived_small, arrived_late, w, m, v)

    out = []
    for k in range(4):
        out += [res[n][k][None] if n in res else small[n][k] for n in WEIGHTS]
    return (loss_row[0, 0], grad_x[None], *out)
```

```python
import math

import jax
import jax.numpy as jnp
from jax import lax
from jax.experimental import pallas as pl
from jax.experimental.pallas import tpu as pltpu

F32 = jnp.float32
BF = jnp.bfloat16


def SDS(shape, dtype):
    return pltpu.HBM(tuple(shape), dtype)

D_MODEL = 1024
ATTN_W = 512
SGU_W = 512
HEAD_DIM = 64
N_GROUPS = 4
GROUP_DIM = 128
CHUNK = 128
D_FF = 2816
PLE_DIM = 256
PROJ = 3 * ATTN_W + 2 * SGU_W
DILATIONS = (1, 4, 16)
QBLK = 128
EPS = 1e-6
NEG = -1e30
N_DEV = 8
LANES = 128

ADAM_LR = 0.001
ADAM_B1 = 0.9
ADAM_B2 = 0.999
ADAM_EPS = 1e-08
ADAM_WD = 0.01
ADAM_STEP = 10

MIB = 2 ** 20
MESH_ID = pl.DeviceIdType.MESH
HBM = pl.BlockSpec(memory_space=pl.ANY)


def _params(sem, vmem_mib):
    return pltpu.CompilerParams(dimension_semantics=sem, vmem_limit_bytes=vmem_mib * MIB)


def _dot(a, b):
    return jnp.dot(a, b, preferred_element_type=F32)


def _dot_nt(a, b):
    return lax.dot_general(a, b, (((1,), (1,)), ((), ())), preferred_element_type=F32)


def _dot_tn(a, b):
    return lax.dot_general(a, b, (((0,), (0,)), ((), ())), preferred_element_type=F32)


def _rstd(x):
    return lax.rsqrt(jnp.mean(x * x, axis=-1, keepdims=True) + EPS)


def _rms_bwd(dy, x, g):
    r = _rstd(x)
    n = x * r
    dn = dy * g
    dx = r * (dn - n * jnp.mean(dn * n, axis=-1, keepdims=True))
    return dx, dy * n


def _colsum(v):
    return jnp.sum(v, axis=0, keepdims=True)


_G0 = math.sqrt(2.0 / math.pi)
_G1 = 0.044715


def _gelu(x):
    t = jnp.tanh(_G0 * (x + _G1 * x * x * x))
    return 0.5 * x * (1.0 + t), t


def _gelu_grad(x, t):
    return 0.5 * (1.0 + t) + 0.5 * x * (1.0 - t * t) * (_G0 * (1.0 + 3.0 * _G1 * x * x))


def _sigmoid(x):
    return 0.5 * jnp.tanh(0.5 * x) + 0.5


def _row(i):
    return (i, 0)


def _fixed(i):
    return (0, 0)


def _acc_init(step, *refs):
    @pl.when(step == 0)
    def _():
        for r in refs:
            r[...] = jnp.zeros_like(r)


FLIPS = [(dx, dy, dc) for dx in (0, 1) for dy in (0, 1) for dc in (0, 1)][1:]
DMA_SEMS = pltpu.SemaphoreType.DMA


def _mesh_pos():
    return lax.axis_index("x"), lax.axis_index("y"), lax.axis_index("c")


def _remote(src, dst, sems, n, to):
    return pltpu.make_async_remote_copy(src_ref=src, dst_ref=dst, send_sem=sems[0].at[n], recv_sem=sems[1].at[n],
                                        device_id=to, device_id_type=MESH_ID)


class _Scatter:
    def __init__(self, items, sems):
        x, y, c = _mesh_pos()
        me = 4 * x + 2 * y + c
        self.local, self.sends, self.arrivals = [], [], []
        for i, (src, dst, slabbed) in enumerate(items):
            self.local.append(pltpu.make_async_copy(src.at[me] if slabbed else src, dst.at[me], sems[2].at[i]))
            for k, (dx, dy, dc) in enumerate(FLIPS):
                to = (1 - x if dx else x, 1 - y if dy else y, 1 - c if dc else c)
                peer = 4 * to[0] + 2 * to[1] + to[2]
                out = src.at[peer] if slabbed else src
                self.sends.append(_remote(out, dst.at[me], sems, 7 * i + k, to))
                self.arrivals.append(_remote(out, dst.at[peer], sems, 7 * i + k, to))

    def start(self):
        for cp in self.local + self.sends:
            cp.start()

    def wait(self):
        for cp in self.arrivals:
            cp.wait_recv()
        for cp in self.sends:
            cp.wait_send()
        for cp in self.local:
            cp.wait()


def _scatter_sems(n):
    return [DMA_SEMS((7 * n,)), DMA_SEMS((7 * n,)), DMA_SEMS((n,))]


class _Gather:
    def __init__(self, items, sems):
        x, y, c = _mesh_pos()
        me, sibling = (x, y, c), (x, y, 1 - c)
        chips = [(1 - x, y), (x, 1 - y), (1 - x, 1 - y)]
        self.first, self.passed, self.from_chips, self.rest, self.local = [], [], [], [], []
        for i, (src, dst) in enumerate(items):
            def slot(p, dst=dst):
                return dst.at[4 * p[0] + 2 * p[1] + p[2]]

            def copy(k, block, to, own=False, i=i, src=src, slot=slot):
                return _remote(src if own else slot(block), slot(block), sems, 7 * i + k, to)

            self.local.append(pltpu.make_async_copy(src, slot(me), sems[2].at[i]))
            self.first.append(copy(0, me, sibling, own=True))
            self.first += [copy(1 + j, me, (*chip, c), own=True) for j, chip in enumerate(chips)]
            self.passed += [copy(4 + j, (*chip, c), sibling) for j, chip in enumerate(chips)]
            self.from_chips += [copy(1 + j, (*chip, c), me) for j, chip in enumerate(chips)]
            self.rest.append(copy(0, sibling, me))
            self.rest += [copy(4 + j, (*chip, 1 - c), me) for j, chip in enumerate(chips)]

    def start(self):
        for cp in self.local + self.first:
            cp.start()

    def forward(self):
        for arrived, onward in zip(self.from_chips, self.passed):
            arrived.wait_recv()
            onward.start()

    def finish(self):
        for cp in self.rest:
            cp.wait_recv()
        for cp in self.first + self.passed:
            cp.wait_send()
        for cp in self.local:
            cp.wait()


def _all_gather(shard, name):
    def body(x_ref, out_ref, *sems):
        g = _Gather([(x_ref, out_ref)], sems)
        g.start()
        g.forward()
        g.finish()

    return pl.pallas_call(
        body, out_shape=SDS((N_DEV,) + shard.shape, shard.dtype), in_specs=[HBM], out_specs=HBM,
        scratch_shapes=_scatter_sems(1), name=name)(shard)


def _scatter_call(srcs, slabbed, name):
    n = len(srcs)

    def body(*refs):
        sc = _Scatter(list(zip(refs[:n], refs[n:2 * n], slabbed)), refs[2 * n:])
        sc.start()
        sc.wait()

    shapes = [SDS(s.shape if sl else (N_DEV,) + s.shape, s.dtype) for s, sl in zip(srcs, slabbed)]
    return pl.pallas_call(body, out_shape=shapes, in_specs=[HBM] * n, out_specs=[HBM] * n,
                          scratch_shapes=_scatter_sems(n), name=name)(*srcs)


SEM = pl.BlockSpec(memory_space=pltpu.SEMAPHORE)
N_PEERS = len(FLIPS)


def _slab_copies(src_ref, land_ref, send_sems, recv_sems):
    x, y, c = _mesh_pos()
    me = 4 * x + 2 * y + c
    copies = []
    for k, (dx, dy, dc) in enumerate(FLIPS):
        to = (1 - x if dx else x, 1 - y if dy else y, 1 - c if dc else c)
        peer = 4 * to[0] + 2 * to[1] + to[2]
        sems = (send_sems, recv_sems)
        copies.append((_remote(src_ref.at[peer], land_ref.at[me], sems, k, to),
                       _remote(src_ref.at[peer], land_ref.at[peer], sems, k, to)))
    return copies


def _scatter_begin(src, name):
    def body(src_ref, land_ref, send_sems, recv_sems, src_thru, land_thru, token):
        for send, _ in _slab_copies(src_ref, land_ref, send_sems, recv_sems):
            send.start()
        token[...] = jnp.zeros_like(token)

    landing = lax.empty(src.shape, src.dtype)
    return pl.pallas_call(
        body, name=name,
        out_shape=(pltpu.SemaphoreType.DMA((N_PEERS,)), pltpu.SemaphoreType.DMA((N_PEERS,)),
                   pltpu.HBM(src.shape, src.dtype), pltpu.HBM(src.shape, src.dtype),
                   jax.ShapeDtypeStruct((SUBLANES, LANES), F32)),
        in_specs=(HBM, HBM), out_specs=(SEM, SEM, HBM, HBM, pl.BlockSpec(memory_space=pltpu.VMEM)),
        input_output_aliases={0: 2, 1: 3},
        compiler_params=pltpu.CompilerParams(has_side_effects=pltpu.SideEffectType.DATAFLOW_SIDE_EFFECTING))(
            pltpu.with_memory_space_constraint(src, pltpu.HBM), pltpu.with_memory_space_constraint(landing, pltpu.HBM))


def _scatter_end(send_sems, recv_sems, src_thru, land_thru, after, name):
    def body(src_ref, land_ref, send_sems, recv_sems, after_ref, src_dead, land_out):
        for send, arrival in _slab_copies(src_ref, land_ref, send_sems, recv_sems):
            send.wait_send()
            arrival.wait_recv()

    return pl.pallas_call(
        body, name=name,
        out_shape=(pltpu.HBM(src_thru.shape, src_thru.dtype), pltpu.HBM(land_thru.shape, land_thru.dtype)),
        in_specs=(HBM, HBM, SEM, SEM, HBM), out_specs=(HBM, HBM), input_output_aliases={0: 0, 1: 1},
        compiler_params=pltpu.CompilerParams(has_side_effects=pltpu.SideEffectType.DATAFLOW_SIDE_EFFECTING))(
            src_thru, land_thru, send_sems, recv_sems, after)


def _in_proj(x, g1, w_in_t, tm=512):
    T = x.shape[0]

    def body(x_ref, g_ref, w_ref, proj_ref, a_ref):
        xv = x_ref[...]
        a = (xv * _rstd(xv) * g_ref[...]).astype(BF)
        a_ref[...] = a
        proj_ref[...] = _dot_nt(a, w_ref[...])

    return pl.pallas_call(
        body, grid=(T // tm,),
        in_specs=[pl.BlockSpec((tm, D_MODEL), _row), pl.BlockSpec((1, D_MODEL), _fixed),
                  pl.BlockSpec((PROJ, D_MODEL), _fixed)],
        out_specs=[pl.BlockSpec((tm, PROJ), _row), pl.BlockSpec((tm, D_MODEL), _row)],
        out_shape=[SDS((T, PROJ), F32), SDS((T, D_MODEL), BF)],
        name="in_proj", compiler_params=_params(("arbitrary",), 48))(x, g1, w_in_t)


ATTN_GROUP = 16


def _attn_bias(sl_ref, bias_ref):
    qi = lax.broadcasted_iota(jnp.int32, (QBLK, QBLK), 0)
    kj = lax.broadcasted_iota(jnp.int32, (QBLK, QBLK), 1)
    step = qi - kj
    for di, d in enumerate(DILATIONS):
        for j in range(2):
            sl = sl_ref[0, j:j + 1, :]
            cur = jnp.where(step >= 0, -sl * (step * d).astype(F32), NEG)
            prev = jnp.where(step <= 0, -sl * ((step + QBLK) * d).astype(F32), NEG)
            rows = slice(j * QBLK, (j + 1) * QBLK)
            bias_ref[di * 2, rows, :QBLK] = prev
            bias_ref[di * 2, rows, QBLK:] = cur
            bias_ref[di * 2 + 1, rows, :QBLK] = jnp.full((QBLK, QBLK), NEG, F32)
            bias_ref[di * 2 + 1, rows, QBLK:] = cur


def _stack_heads(x, lo):
    return jnp.concatenate([jnp.where(lo, x, 0.0), jnp.where(lo, 0.0, x)], axis=0).astype(BF)


def _unstack_heads(x, lo):
    return jnp.where(lo, x[:QBLK], x[QBLK:])


def _attn_rows(start, d, blocks=1):
    if d == 1:
        return pl.ds(pl.multiple_of(start, QBLK), blocks * QBLK)
    return pl.ds(start, blocks * QBLK, stride=d)


def _attn_group_index(i, nblk, d, group):
    per = nblk // d // group
    r = i // per
    n0 = (i % per) * group
    start = r + (d * QBLK) * n0
    pstart = jnp.maximum(start - d * QBLK, r)
    return start, pstart, n0 == 0


def _attn_plan(nblk, d):
    group = min(ATTN_GROUP, nblk // d)
    return group, max(1, min(ATTN_GROUP // group, d))


def _attn_fwd(proj, slopes, to_gather):
    T = proj.shape[0]
    nblk = T // QBLK
    n_g = len(to_gather)

    def body(q_ref, k_ref, v_ref, sl_ref, *rest):
        srcs, (o_ref, m_ref), dsts = rest[:n_g], rest[n_g:n_g + 2], rest[n_g + 2:2 * n_g + 2]
        sems, (l_ref, bias_ref) = rest[2 * n_g + 2:2 * n_g + 5], rest[2 * n_g + 5:]
        h = pl.program_id(0)

        @pl.when(h == 0)
        def _():
            _Gather(list(zip(srcs, dsts)), sems).start()

        @pl.when(h == pl.num_programs(0) - 1)
        def _():
            _Gather(list(zip(srcs, dsts)), sems).forward()

        _attn_bias(sl_ref, bias_ref)
        lo = lax.broadcasted_iota(jnp.int32, (1, LANES), 1) < HEAD_DIM

        order = list(enumerate(DILATIONS))[::-1]
        for di, d in order:
            group, segs = _attn_plan(nblk, d)
            fresh, last = di == order[0][0], di == order[-1][0]

            def step(i, carry, segs=segs, **kw):
                for s in range(segs):
                    segment(i * segs + s, **kw)
                return carry

            def segment(i, d=d, di=di, group=group, fresh=fresh, last=last):
                start, pstart, first = _attn_group_index(i, nblk, d, group)
                prows = _attn_rows(pstart, d)
                alone = group == nblk // d
                k_prev = v_prev = None
                if not alone:
                    k_prev, v_prev = k_ref[prows, :].astype(BF), v_ref[prows, :].astype(BF)
                for b in range(group):
                    out = _attn_rows(start + b * (d * QBLK), d)
                    k_own, v_own = k_ref[out, :].astype(BF), v_ref[out, :].astype(BF)
                    if alone and b == 0:
                        k2, v2, bias = k_own, v_own, bias_ref[di * 2, :, QBLK:]
                    else:
                        k2, v2 = jnp.concatenate([k_prev, k_own], axis=0), jnp.concatenate([v_prev, v_own], axis=0)
                        bias = bias_ref[di * 2 + first.astype(jnp.int32)] if b == 0 else bias_ref[di * 2]
                    k_prev, v_prev = k_own, v_own
                    s = _dot_nt(_stack_heads(q_ref[out, :] * (HEAD_DIM ** -0.5), lo), k2) + bias
                    m = jnp.max(s, axis=-1, keepdims=True)
                    pr = jnp.exp(s - m)
                    m_b = _unstack_heads(m, lo)
                    l_b = _unstack_heads(jnp.sum(pr, axis=-1, keepdims=True), lo)
                    o_b = _unstack_heads(_dot(pr.astype(BF), v2), lo)
                    if fresh:
                        m_ref[out, :] = m_b
                        l_ref[out, :] = l_b
                        o_ref[out, :] = o_b
                        continue
                    m_o = m_ref[out, :]
                    m_n = jnp.maximum(m_o, m_b)
                    wa, wb = jnp.exp(m_o - m_n), jnp.exp(m_b - m_n)
                    l_n = wa * l_ref[out, :] + wb * l_b
                    o_n = wa * o_ref[out, :] + wb * o_b
                    if last:
                        m_ref[out, :] = m_n + jnp.log(l_n)
                        o_ref[out, :] = o_n / l_n
                    else:
                        m_ref[out, :] = m_n
                        l_ref[out, :] = l_n
                        o_ref[out, :] = o_n

            lax.fori_loop(0, nblk // (group * segs), step, 0)

        @pl.when(h == pl.num_programs(0) - 1)
        def _():
            _Gather(list(zip(srcs, dsts)), sems).finish()

    col = lambda base: pl.BlockSpec((T, LANES), lambda h: (0, base + h))
    tok = pl.BlockSpec((T, LANES), lambda h: (0, h))
    outs = pl.pallas_call(
        body, grid=(4,),
        in_specs=[col(0), col(4), col(8), pl.BlockSpec((1, 8, LANES), lambda h: (h, 0, 0))] + [HBM] * n_g,
        out_specs=[tok, tok] + [HBM] * n_g,
        out_shape=[SDS((T, ATTN_W), F32), SDS((T, ATTN_W), F32)]
        + [SDS((N_DEV,) + g.shape, g.dtype) for g in to_gather],
        scratch_shapes=_scatter_sems(n_g) + [pltpu.VMEM((T, LANES), F32), pltpu.VMEM((6, 2 * QBLK, 2 * QBLK), F32)],
        name="attn_fwd", compiler_params=_params(("arbitrary",), 56))(proj, proj, proj, slopes, *to_gather)
    return outs[0], outs[1], outs[2:]


def _sgu_norm(zv, ln_g, ln_b):
    gz, tz = _gelu(zv)
    mu = jnp.mean(gz, axis=-1, keepdims=True)
    xc = gz - mu
    rs = lax.rsqrt(jnp.mean(xc * xc, axis=-1, keepdims=True) + EPS)
    xhat = xc * rs
    return xhat * ln_g + ln_b, xhat, rs, tz


def _causal(w):
    i = lax.broadcasted_iota(jnp.int32, (CHUNK, CHUNK), 0)
    j = lax.broadcasted_iota(jnp.int32, (CHUNK, CHUNK), 1)
    return jnp.where(i >= j, w, 0.0)


def _sgu_fwd_tile(u_ref, z_ref, g_ref, b_ref, w_ref, bs_ref, out_ref):
    for g in range(N_GROUPS):
        wm = _causal(w_ref[g]).astype(BF)
        cols = slice(g * GROUP_DIM, (g + 1) * GROUP_DIM)
        for c in range(u_ref.shape[0] // CHUNK):
            rows = slice(c * CHUNK, (c + 1) * CHUNK)
            zn, _, _, _ = _sgu_norm(z_ref[rows, cols], g_ref[...], b_ref[...])
            mixed = _dot(wm, zn.astype(BF)) + bs_ref[:, g:g + 1]
            gu, _ = _gelu(u_ref[rows, cols])
            out_ref[rows, cols] = gu * mixed


def _out_proj(attn, proj, x, ln_g, ln_b, w_s, b_st, g_a, g_s, w_out, g_pm, g_pf, tm=512):
    T = x.shape[0]

    def body(a_ref, u_ref, z_ref, x_ref, lg_ref, lb_ref, ws_ref, bs_ref, ga_ref, gs_ref, w_ref, gpm_ref, gpf_ref,
             s_ref, grp_ref, mixed_ref, h1_ref, f_ref):
        _sgu_fwd_tile(u_ref, z_ref, lg_ref, lb_ref, ws_ref, bs_ref, s_ref)
        av, sv = a_ref[...], s_ref[...]
        an = (av * _rstd(av) * ga_ref[...]).astype(BF)
        sn = (sv * _rstd(sv) * gs_ref[...]).astype(BF)
        grp_ref[:, :ATTN_W] = an
        grp_ref[:, ATTN_W:] = sn
        mixed = _dot(an, w_ref[:ATTN_W, :]) + _dot(sn, w_ref[ATTN_W:, :])
        mixed_ref[...] = mixed
        h1 = x_ref[...] + mixed * _rstd(mixed) * gpm_ref[...]
        h1_ref[...] = h1
        f_ref[...] = (h1 * _rstd(h1) * gpf_ref[...]).astype(BF)

    tok = lambda w: pl.BlockSpec((tm, w), _row)
    vec = lambda w: pl.BlockSpec((1, w), _fixed)
    return pl.pallas_call(
        body, grid=(T // tm,),
        in_specs=[tok(ATTN_W), pl.BlockSpec((tm, SGU_W), lambda i: (i, 3)), pl.BlockSpec((tm, SGU_W), lambda i: (i, 4)),
                  tok(D_MODEL), vec(GROUP_DIM), vec(GROUP_DIM), pl.BlockSpec((N_GROUPS, CHUNK, CHUNK), lambda i: (0, 0, 0)),
                  pl.BlockSpec((CHUNK, LANES), _fixed), vec(ATTN_W), vec(SGU_W),
                  pl.BlockSpec((D_MODEL, D_MODEL), _fixed), vec(D_MODEL), vec(D_MODEL)],
        out_specs=[tok(SGU_W)] + [tok(D_MODEL)] * 4,
        out_shape=[SDS((T, SGU_W), F32), SDS((T, D_MODEL), BF), SDS((T, D_MODEL), F32), SDS((T, D_MODEL), F32),
                   SDS((T, D_MODEL), BF)],
        name="out_proj", compiler_params=_params(("arbitrary",), 52))(
            attn, proj, proj, x, ln_g, ln_b, w_s, b_st, g_a, g_s, w_out, g_pm, g_pf)


FF_TILE = 1408
FF_TILES = D_FF // FF_TILE
FF_CHUNK = 256


def _gate_up(f, w_gu_t, tm=512):
    T = f.shape[0]
    tn = FF_TILE

    def body(f_ref, wg_ref, wu_ref, g_ref, u_ref, act_ref):
        fv = f_ref[...]
        g = _dot_nt(fv, wg_ref[...])
        u = _dot_nt(fv, wu_ref[...])
        g_ref[...] = g.astype(BF)
        u_ref[...] = u.astype(BF)
        act_ref[...] = (g * _sigmoid(g) * u).astype(BF)

    ospec = pl.BlockSpec((tm, tn), lambda j, i: (i, j))
    return pl.pallas_call(
        body, grid=(FF_TILES, T // tm),
        in_specs=[pl.BlockSpec((tm, D_MODEL), lambda j, i: (i, 0)), pl.BlockSpec((tn, D_MODEL), lambda j, i: (j, 0)),
                  pl.BlockSpec((tn, D_MODEL), lambda j, i: (j + FF_TILES, 0))],
        out_specs=[ospec] * 3, out_shape=[SDS((T, D_FF), BF)] * 3,
        name="gate_up", compiler_params=_params(("arbitrary", "arbitrary"), 40))(f, w_gu_t, w_gu_t)


def _down_proj(act, w_down, h1, g_pff, tm=512):
    T = act.shape[0]

    def body(a_ref, w_ref, h1_ref, g_ref, y_ref, h2_ref):
        y = _dot(a_ref[...], w_ref[...])
        y_ref[...] = y
        h2_ref[...] = h1_ref[...] + y * _rstd(y) * g_ref[...]

    return pl.pallas_call(
        body, grid=(T // tm,),
        in_specs=[pl.BlockSpec((tm, D_FF), _row), pl.BlockSpec((D_FF, D_MODEL), _fixed),
                  pl.BlockSpec((tm, D_MODEL), _row), pl.BlockSpec((1, D_MODEL), _fixed)],
        out_specs=[pl.BlockSpec((tm, D_MODEL), _row)] * 2,
        out_shape=[SDS((T, D_MODEL), F32)] * 2,
        name="down_proj", compiler_params=_params(("arbitrary",), 48))(act, w_down, h1, g_pff)


def _pe_loss_and_bwd(h2, p, target, y, w_peg, b_peg, w_pep_t, g_pff, tm=512):
    T = h2.shape[0]

    def body(h2_ref, p_ref, t_ref, y_ref, wg_ref, b_ref, wp_ref, g_ref,
             dh2_ref, dy_ref, gpeg_ref, gpep_ref, loss_ref, db_ref, dg_ref):
        _acc_init(pl.program_id(0), gpeg_ref, gpep_ref, loss_ref, db_ref, dg_ref)
        h2v = h2_ref[...]
        h2b = h2v.astype(BF)
        pb = p_ref[...].astype(BF)
        gate = _sigmoid(_dot(h2b, wg_ref[...]) + b_ref[...])
        pp = _dot_nt(pb, wp_ref[...])
        diff = h2v + gate * pp - t_ref[...]
        loss_ref[...] += _colsum(diff * diff)
        dh3 = diff * (1.0 / D_MODEL)
        dpre = dh3 * pp * (gate * (1.0 - gate))
        dpre_b = dpre.astype(BF)
        db_ref[...] += _colsum(dpre)
        gpeg_ref[...] += _dot_tn(h2b, dpre_b)
        gpep_ref[...] += _dot_tn((dh3 * gate).astype(BF), pb)
        dh2 = dh3 + _dot_nt(dpre_b, wg_ref[...])
        dh2_ref[...] = dh2
        dy, dg = _rms_bwd(dh2, y_ref[...], g_ref[...])
        dy_ref[...] = dy.astype(BF)
        dg_ref[...] += _colsum(dg)

    tok = lambda w: pl.BlockSpec((tm, w), _row)
    vec = pl.BlockSpec((1, D_MODEL), _fixed)
    wg = pl.BlockSpec((D_MODEL, D_MODEL), _fixed)
    wp = pl.BlockSpec((D_MODEL, PLE_DIM), _fixed)
    return pl.pallas_call(
        body, grid=(T // tm,),
        in_specs=[tok(D_MODEL), tok(PLE_DIM), tok(D_MODEL), tok(D_MODEL), wg, vec, wp, vec],
        out_specs=[tok(D_MODEL), tok(D_MODEL), wg, wp, vec, vec, vec],
        out_shape=[SDS((T, D_MODEL), F32), SDS((T, D_MODEL), BF), SDS((D_MODEL, D_MODEL), F32),
                   SDS((D_MODEL, PLE_DIM), F32)] + [SDS((1, D_MODEL), F32)] * 3,
        name="pe_loss_and_bwd", compiler_params=_params(("arbitrary",), 56))(
            h2, p, target, y, w_peg, b_peg, w_pep_t, g_pff)


def _weight_grad(a, dy, name, into=None, row_tile=0, rows=None, tk=1024):
    n = dy.shape[1]
    tn = min(n, 1024)
    T, ka = a.shape
    tka = FF_TILE if ka == D_FF else min(ka, 1024)
    rows = ka if rows is None else rows

    def body(a_ref, dy_ref, *rest):
        out_ref = rest[-1]
        _acc_init(pl.program_id(2), out_ref)
        out_ref[...] += _dot_tn(a_ref[...].astype(BF), dy_ref[...].astype(BF))

    carried = [] if into is None else [into]
    return pl.pallas_call(
        body, grid=(ka // tka, n // tn, T // tk),
        in_specs=[pl.BlockSpec((tk, tka), lambda i, j, k: (k, i)), pl.BlockSpec((tk, tn), lambda i, j, k: (k, j))]
        + [HBM] * len(carried),
        out_specs=pl.BlockSpec((tka, tn), lambda i, j, k: (i + row_tile, j)),
        out_shape=SDS((rows, n), F32), input_output_aliases={2: 0} if carried else {},
        name="grad_" + name, compiler_params=_params(("arbitrary",) * 3, 40))(a, dy, *carried)


def _grad_w_in(dparts, a, tk=1024):
    T = a.shape[0]

    def body(*refs):
        d_refs, a_ref, out_ref, acc_ref = refs[:len(dparts)], refs[-3], refs[-2], refs[-1]
        k = pl.program_id(0)
        _acc_init(k, acc_ref)
        cols = [r[part].astype(BF) for r in d_refs for part in range(r.shape[0])]
        acc_ref[...] += _dot_tn(jnp.concatenate(cols, axis=1), a_ref[...])

        @pl.when(k == pl.num_programs(0) - 1)
        def _():
            out_ref[...] = acc_ref[...].astype(BF)

    return pl.pallas_call(
        body, grid=(T // tk,),
        in_specs=[pl.BlockSpec((d.shape[0], tk, d.shape[2]), lambda k: (0, k, 0)) for d in dparts]
        + [pl.BlockSpec((tk, D_MODEL), lambda k: (k, 0))],
        out_specs=pl.BlockSpec((PROJ, D_MODEL), lambda k: (0, 0)),
        out_shape=SDS((PROJ, D_MODEL), BF), scratch_shapes=[pltpu.VMEM((PROJ, D_MODEL), F32)],
        name="grad_w_in", compiler_params=_params(("arbitrary",), 48))(*dparts, a)


def _down_bwd(dy, w_down, g, u, to_send, tm=512):
    T = dy.shape[0]
    n_s = len(to_send)

    def body(dy_ref, w_ref, g_ref, u_ref, *rest):
        srcs, (dg_ref, du_ref), dsts, sems = rest[:n_s], rest[n_s:n_s + 2], rest[n_s + 2:2 * n_s + 2], rest[2 * n_s + 2:]
        i = pl.program_id(0)
        items = list(zip(srcs, dsts, [True] * n_s))

        @pl.when(i == 0)
        def _():
            _Scatter(items, sems).start()

        dyv = dy_ref[...]
        for c in range(D_FF // FF_CHUNK):
            cols = slice(c * FF_CHUNK, (c + 1) * FF_CHUNK)
            dact = _dot_nt(dyv, w_ref[cols, :]).astype(BF)
            gv, uv = g_ref[:, cols], u_ref[:, cols]
            s = _sigmoid(gv)
            ds = dact * s
            dg_ref[:, cols] = ds * uv * (1.0 + gv * (1.0 - s))
            du_ref[:, cols] = ds * gv

        @pl.when(i == pl.num_programs(0) - 1)
        def _():
            _Scatter(items, sems).wait()

    tile = pl.BlockSpec((tm, D_FF), _row)
    outs = pl.pallas_call(
        body, grid=(T // tm,),
        in_specs=[pl.BlockSpec((tm, D_MODEL), _row),
                  pl.BlockSpec((D_FF, D_MODEL), _fixed, pipeline_mode=pl.Buffered(1)), tile, tile] + [HBM] * n_s,
        out_specs=[tile, tile] + [HBM] * n_s,
        out_shape=[SDS((T, D_FF), BF)] * 2 + [SDS(s.shape, s.dtype) for s in to_send],
        scratch_shapes=_scatter_sems(n_s),
        name="down_bwd", compiler_params=_params(("arbitrary",), 48))(dy, w_down, g, u, *to_send)
    return outs[0], outs[1], outs[2:]


def _ffn_in_bwd(dg, du, w_gu_t, h1, dh2, mixed, g_pf, g_pm, to_send, tm=512):
    T = h1.shape[0]
    n_s = len(to_send)

    def body(dg_ref, du_ref, w_ref, h1_ref, dh2_ref, mx_ref, gpf_ref, gpm_ref, *rest):
        srcs, outs, dsts, sems = rest[:n_s], rest[n_s:n_s + 4], rest[n_s + 4:2 * n_s + 4], rest[2 * n_s + 4:]
        dh1_ref, dmx_ref, dgpf_ref, dgpm_ref = outs
        i = pl.program_id(0)
        items = list(zip(srcs, dsts, [True] * n_s))
        _acc_init(i, dgpf_ref, dgpm_ref)

        @pl.when(i == 0)
        def _():
            _Scatter(items, sems).start()

        df = _dot(jnp.concatenate([dg_ref[...], du_ref[...]], axis=1), w_ref[...])
        dx, dgf = _rms_bwd(df, h1_ref[...], gpf_ref[...])
        dh1 = dh2_ref[...] + dx
        dh1_ref[...] = dh1
        dmx, dgm = _rms_bwd(dh1, mx_ref[...], gpm_ref[...])
        dmx_ref[...] = dmx.astype(BF)
        dgpf_ref[...] += _colsum(dgf)
        dgpm_ref[...] += _colsum(dgm)

        @pl.when(i == pl.num_programs(0) - 1)
        def _():
            _Scatter(items, sems).wait()

    tok = lambda w: pl.BlockSpec((tm, w), _row)
    vec = pl.BlockSpec((1, D_MODEL), _fixed)
    outs = pl.pallas_call(
        body, grid=(T // tm,),
        in_specs=[tok(D_FF), tok(D_FF), pl.BlockSpec((2 * D_FF, D_MODEL), _fixed, pipeline_mode=pl.Buffered(1)),
                  tok(D_MODEL), tok(D_MODEL), tok(D_MODEL), vec, vec]
        + [HBM] * n_s,
        out_specs=[tok(D_MODEL), tok(D_MODEL), vec, vec] + [HBM] * n_s,
        out_shape=[SDS((T, D_MODEL), F32), SDS((T, D_MODEL), BF), SDS((1, D_MODEL), F32), SDS((1, D_MODEL), F32)]
        + [SDS(s.shape, s.dtype) for s in to_send],
        scratch_shapes=_scatter_sems(n_s),
        name="ffn_in_bwd", compiler_params=_params(("arbitrary",), 56))(
            dg, du, w_gu_t, h1, dh2, mixed, g_pf, g_pm, *to_send)
    return outs[0], outs[1], outs[2], outs[3], outs[4:]


STAT_LANES = HEAD_DIM // 2


def _ff_grad_spec(half):
    return pl.BlockSpec((D_FF, D_MODEL), lambda i: (half, 0), pipeline_mode=pl.Buffered(1))


def _out_bwd(dmx, w_out, attn, lse, sgu, g_a, g_s, dg, f, tm=512):
    T = attn.shape[0]

    def body(dm_ref, w_ref, a_ref, l_ref, s_ref, ga_ref, gs_ref, dgate_ref, f_ref,
             da_ref, st_ref, ds_ref, dga_ref, dgs_ref, ggu_ref):
        _acc_init(pl.program_id(0), dga_ref, dgs_ref, ggu_ref)
        ggu_ref[...] += _dot_tn(dgate_ref[...], f_ref[...])
        dgr = _dot_nt(dm_ref[...], w_ref[...])
        av = a_ref[...]
        da, dga = _rms_bwd(dgr[:, :ATTN_W], av, ga_ref[...])
        ds, dgs = _rms_bwd(dgr[:, ATTN_W:], s_ref[...], gs_ref[...])
        da_ref[...] = da
        ds_ref[...] = ds
        dga_ref[...] += _colsum(dga)
        dgs_ref[...] += _colsum(dgs)
        lane = lax.broadcasted_iota(jnp.int32, (1, LANES), 1)
        lo = lane < HEAD_DIM
        first = (lane % HEAD_DIM) < STAT_LANES
        prod = da * av
        for c in range(ATTN_W // LANES):
            cols = slice(c * LANES, (c + 1) * LANES)
            pc = prod[:, cols]
            delta = jnp.where(lo, jnp.sum(jnp.where(lo, pc, 0.0), axis=-1, keepdims=True),
                              jnp.sum(jnp.where(lo, 0.0, pc), axis=-1, keepdims=True))
            st_ref[:, cols] = jnp.where(first, l_ref[:, cols], delta)

    tok = lambda w: pl.BlockSpec((tm, w), _row)
    vec = lambda w: pl.BlockSpec((1, w), _fixed)
    return pl.pallas_call(
        body, grid=(T // tm,),
        in_specs=[tok(D_MODEL), pl.BlockSpec((D_MODEL, D_MODEL), _fixed), tok(ATTN_W), tok(ATTN_W), tok(SGU_W),
                  vec(ATTN_W), vec(SGU_W), tok(D_FF), tok(D_MODEL)],
        out_specs=[tok(ATTN_W), tok(ATTN_W), tok(SGU_W), vec(ATTN_W), vec(SGU_W), _ff_grad_spec(0)],
        out_shape=[SDS((T, ATTN_W), F32), SDS((T, ATTN_W), F32), SDS((T, SGU_W), F32), SDS((1, ATTN_W), F32),
                   SDS((1, SGU_W), F32), SDS((2 * D_FF, D_MODEL), F32)],
        name="out_bwd", compiler_params=_params(("arbitrary",), 56))(dmx, w_out, attn, lse, sgu, g_a, g_s, dg, f)


def _sgu_bwd(proj, dsgu, ln_g, ln_b, w_s, b_st, groups, dmx, d_up, f, g_gu_t, tm=512):
    T = proj.shape[0]

    def body(u_ref, z_ref, ds_ref, g_ref, b_ref, w_ref, bs_ref, grp_ref, dmx_ref, dup_ref, f_ref, _,
             duz_ref, dw_ref, dbs_ref, dlg_ref, dlb_ref, gout_ref, ggu_ref, dbacc_ref):
        du_ref, dz_ref = duz_ref.at[0], duz_ref.at[1]
        step = pl.program_id(0)
        _acc_init(step, dw_ref, dbs_ref, dlg_ref, dlb_ref, gout_ref, ggu_ref, dbacc_ref)
        gout_ref[...] += _dot_tn(grp_ref[...], dmx_ref[...])
        ggu_ref[...] += _dot_tn(dup_ref[...], f_ref[...])
        lng, lnb = g_ref[...], b_ref[...]
        for g in range(N_GROUPS):
            wm = _causal(w_ref[g]).astype(BF)
            cols = slice(g * GROUP_DIM, (g + 1) * GROUP_DIM)
            for c in range(tm // CHUNK):
                rows = slice(c * CHUNK, (c + 1) * CHUNK)
                zv, uv, dout = z_ref[rows, cols], u_ref[rows, cols], ds_ref[rows, cols]
                zn, xhat, rs, tz = _sgu_norm(zv, lng, lnb)
                znb = zn.astype(BF)
                mixed = _dot(wm, znb) + bs_ref[:, g:g + 1]
                gu, tu = _gelu(uv)
                du_ref[rows, cols] = (dout * mixed * _gelu_grad(uv, tu)).astype(BF)
                dmix = dout * gu
                dmb = dmix.astype(BF)
                dw_ref[g] += _causal(_dot_nt(dmb, znb))
                dbacc_ref[g] += dmix
                dzn = _dot_tn(wm, dmb)
                dlg_ref[...] += _colsum(dzn * xhat)
                dlb_ref[...] += _colsum(dzn)
                dxh = dzn * lng
                dgz = rs * (dxh - jnp.mean(dxh, axis=-1, keepdims=True)
                            - xhat * jnp.mean(dxh * xhat, axis=-1, keepdims=True))
                dz_ref[rows, cols] = (dgz * _gelu_grad(zv, tz)).astype(BF)

        @pl.when(step == pl.num_programs(0) - 1)
        def _():
            lane = lax.broadcasted_iota(jnp.int32, (CHUNK, LANES), 1)
            acc = jnp.zeros((CHUNK, LANES), F32)
            for g in range(N_GROUPS):
                acc = jnp.where(lane == g, jnp.sum(dbacc_ref[g], axis=-1, keepdims=True), acc)
            dbs_ref[...] = acc

    tok = pl.BlockSpec((tm, SGU_W), _row)
    vec = pl.BlockSpec((1, GROUP_DIM), _fixed)
    wsp = pl.BlockSpec((N_GROUPS, CHUNK, CHUNK), lambda i: (0, 0, 0))
    sq = pl.BlockSpec((CHUNK, LANES), _fixed)
    wide = pl.BlockSpec((tm, D_MODEL), _row)
    return pl.pallas_call(
        body, grid=(T // tm,),
        in_specs=[pl.BlockSpec((tm, SGU_W), lambda i: (i, 3)), pl.BlockSpec((tm, SGU_W), lambda i: (i, 4)), tok,
                  vec, vec, wsp, sq, wide, wide, pl.BlockSpec((tm, D_FF), _row), wide, HBM],
        out_specs=[pl.BlockSpec((2, tm, SGU_W), lambda i: (0, i, 0)), wsp, sq, vec, vec,
                   pl.BlockSpec((D_MODEL, D_MODEL), _fixed), _ff_grad_spec(1)],
        out_shape=[SDS((2, T, SGU_W), BF), SDS((N_GROUPS, CHUNK, CHUNK), F32),
                   SDS((CHUNK, LANES), F32), SDS((1, GROUP_DIM), F32), SDS((1, GROUP_DIM), F32),
                   SDS((D_MODEL, D_MODEL), F32), SDS((2 * D_FF, D_MODEL), F32)],
        input_output_aliases={11: 6},
        scratch_shapes=[pltpu.VMEM((N_GROUPS, CHUNK, LANES), F32)],
        name="sgu_bwd", compiler_params=_params(("arbitrary",), 56))(
            proj, proj, dsgu, ln_g, ln_b, w_s, b_st, groups, dmx, d_up, f, g_gu_t)


def _attn_bwd(proj, do, stats, slopes, to_send, slabbed):
    T = proj.shape[0]
    nblk = T // QBLK
    n_s = len(to_send)

    def body(q_ref, k_ref, v_ref, do_ref, st_ref, sl_ref, *rest):
        srcs, d_ref, dsts = rest[:n_s], rest[n_s], rest[n_s + 1:2 * n_s + 1]
        sems, bias_ref = rest[2 * n_s + 1:2 * n_s + 4], rest[2 * n_s + 4]
        dq_ref, dk_ref, dv_ref = d_ref.at[0], d_ref.at[1], d_ref.at[2]
        h = pl.program_id(0)
        items = list(zip(srcs, dsts, slabbed))

        @pl.when(h == 0)
        def _():
            _Scatter(items, sems).start()

        _attn_bias(sl_ref, bias_ref)
        lo = lax.broadcasted_iota(jnp.int32, (1, LANES), 1) < HEAD_DIM
        scale = HEAD_DIM ** -0.5
        d_ref[...] = jnp.zeros_like(d_ref)

        for di, d in enumerate(DILATIONS):
            group, segs = _attn_plan(nblk, d)

            def step(i, carry, segs=segs, **kw):
                for s in range(segs):
                    segment(i * segs + s, **kw)
                return carry

            def segment(i, d=d, di=di, group=group):
                start, pstart, first = _attn_group_index(i, nblk, d, group)
                rows, prows = _attn_rows(start, d, group), _attn_rows(pstart, d)
                alone = group == nblk // d
                q = q_ref[rows, :] * scale
                if alone:
                    k, v, own0 = k_ref[rows, :].astype(BF), v_ref[rows, :].astype(BF), 0
                else:
                    k = jnp.concatenate([k_ref[prows, :], k_ref[rows, :]], axis=0).astype(BF)
                    v = jnp.concatenate([v_ref[prows, :], v_ref[rows, :]], axis=0).astype(BF)
                    own0 = QBLK
                dov = do_ref[rows, :]
                stats = st_ref[rows, :]
                masks = [lo, ~lo]
                qm = [jnp.where(masks[j], q, 0.0).astype(BF) for j in range(2)]
                dom = [jnp.where(masks[j], dov, 0.0).astype(BF) for j in range(2)]
                for b in range(group):
                    qb = slice(b * QBLK, (b + 1) * QBLK)
                    own_only = alone and b == 0
                    kb = slice(own0 + (b if own_only else b - 1) * QBLK, own0 + (b + 1) * QBLK)
                    which = di * 2 + first.astype(jnp.int32) if b == 0 and not own_only else di * 2
                    dq_parts, prs, dss = [], [], []
                    for j in range(2):
                        bias = bias_ref[which, j * QBLK:(j + 1) * QBLK, QBLK if own_only else 0:]
                        lj = stats[qb, j * HEAD_DIM:j * HEAD_DIM + 1]
                        delta = stats[qb, j * HEAD_DIM + STAT_LANES:j * HEAD_DIM + STAT_LANES + 1]
                        pr = jnp.exp(_dot_nt(qm[j][qb], k[kb]) + bias - lj)
                        ds = (pr * (_dot_nt(dom[j][qb], v[kb]) - delta)).astype(BF)
                        dq_parts.append(_dot(ds, k[kb]))
                        prs.append(pr.astype(BF))
                        dss.append(ds)
                    dk_b = _dot_tn(jnp.concatenate(dss, axis=0), jnp.concatenate([qm[0][qb], qm[1][qb]], axis=0))
                    dv_b = _dot_tn(jnp.concatenate(prs, axis=0), jnp.concatenate([dom[0][qb], dom[1][qb]], axis=0))
                    own = _attn_rows(start + b * (d * QBLK), d)
                    dq_ref[own, :] += jnp.where(lo, dq_parts[0], dq_parts[1]) * scale
                    if own_only:
                        dk_ref[own, :] += dk_b
                        dv_ref[own, :] += dv_b
                    elif b == 0:
                        dk_ref[prows, :] += dk_b[:QBLK]
                        dv_ref[prows, :] += dv_b[:QBLK]
                        dk_ref[own, :] += dk_b[QBLK:]
                        dv_ref[own, :] += dv_b[QBLK:]
                    else:
                        two = _attn_rows(start + (b - 1) * (d * QBLK), d, 2)
                        dk_ref[two, :] += dk_b
                        dv_ref[two, :] += dv_b

            lax.fori_loop(0, nblk // (group * segs), step, 0)

        @pl.when(h == pl.num_programs(0) - 1)
        def _():
            _Scatter(items, sems).wait()

    col = lambda base: pl.BlockSpec((T, LANES), lambda h: (0, base + h))
    outs = pl.pallas_call(
        body, grid=(4,),
        in_specs=[col(0), col(4), col(8), col(0), col(0), pl.BlockSpec((1, 8, LANES), lambda h: (h, 0, 0))]
        + [HBM] * n_s,
        out_specs=[pl.BlockSpec((3, T, LANES), lambda h: (0, 0, h), pipeline_mode=pl.Buffered(1))] + [HBM] * n_s,
        out_shape=[SDS((3, T, ATTN_W), F32)]
        + [SDS(s.shape if sl else (N_DEV,) + s.shape, s.dtype) for s, sl in zip(to_send, slabbed)],
        scratch_shapes=_scatter_sems(n_s) + [pltpu.VMEM((6, 2 * QBLK, 2 * QBLK), F32)],
        name="attn_bwd", compiler_params=_params(("arbitrary",), 60))(proj, proj, proj, do, stats, slopes, *to_send)
    return outs[0], outs[1:]


def _in_bwd(dparts, w_in_t, x, dh1, g1, tm=512):
    T = x.shape[0]
    n = len(dparts)
    w = ATTN_W

    def body(*refs):
        d_refs, (w_ref, x_ref, dh1_ref, g_ref, dx_ref, dg_ref) = refs[:n], refs[n:]
        _acc_init(pl.program_id(0), dg_ref)
        d_proj = jnp.concatenate([r[part].astype(BF) for r in d_refs for part in range(r.shape[0])], axis=1)
        da = _dot(d_proj, w_ref[...])
        dx, dg = _rms_bwd(da, x_ref[...], g_ref[...])
        dx_ref[...] = dh1_ref[...] + dx
        dg_ref[...] += _colsum(dg)

    tok = lambda c: pl.BlockSpec((tm, c), _row)
    vec = pl.BlockSpec((1, D_MODEL), _fixed)
    return pl.pallas_call(
        body, grid=(T // tm,),
        in_specs=[pl.BlockSpec((d.shape[0], tm, w), lambda i: (0, i, 0)) for d in dparts]
        + [pl.BlockSpec((PROJ, D_MODEL), _fixed), tok(D_MODEL), tok(D_MODEL), vec],
        out_specs=[tok(D_MODEL), vec],
        out_shape=[SDS((T, D_MODEL), F32), SDS((1, D_MODEL), F32)],
        name="in_bwd", compiler_params=_params(("arbitrary",), 52))(*dparts, w_in_t, x, dh1, g1)


def _sum_parts(p_ref):
    g = p_ref[0].astype(F32)
    for s in range(1, N_DEV):
        g = g + p_ref[s].astype(F32)
    return g


def _adamw_math(g, w, m, v):
    nm = ADAM_B1 * m + (1.0 - ADAM_B1) * g
    nv = ADAM_B2 * v + (1.0 - ADAM_B2) * (g * g)
    m_hat = nm / (1.0 - ADAM_B1 ** ADAM_STEP)
    v_hat = nv / (1.0 - ADAM_B2 ** ADAM_STEP)
    return -ADAM_LR * (m_hat / (jnp.sqrt(v_hat) + ADAM_EPS) + ADAM_WD * w), nm, nv


def _row_tile(rows):
    for t in (256, 176, 128, 80):
        if rows % t == 0:
            return t
    raise ValueError(rows)


def _reduce_adamw(parts, w, m, v, name):
    rows, width = w.shape
    tr = _row_tile(rows)

    def body(p_ref, w_ref, m_ref, v_ref, g_ref, d_ref, nm_ref, nv_ref):
        g = _sum_parts(p_ref)
        g_ref[...] = g
        d_ref[...], nm_ref[...], nv_ref[...] = _adamw_math(g, w_ref[...], m_ref[...], v_ref[...])

    blk = pl.BlockSpec((tr, width), _row)
    return pl.pallas_call(
        body, grid=(rows // tr,),
        in_specs=[pl.BlockSpec((N_DEV, tr, width), lambda i: (0, i, 0)), blk, blk, blk],
        out_specs=[blk] * 4, out_shape=[SDS((rows, width), F32)] * 4,
        name="adamw_" + name, compiler_params=_params(("arbitrary",), 32))(parts, w, m, v)


def _reduce(parts, name):
    _, rows, width = parts.shape
    tr = _row_tile(rows)

    def body(p_ref, g_ref):
        g_ref[...] = _sum_parts(p_ref)

    return pl.pallas_call(
        body, grid=(rows // tr,),
        in_specs=[pl.BlockSpec((N_DEV, tr, width), lambda i: (0, i, 0))],
        out_specs=pl.BlockSpec((tr, width), _row), out_shape=SDS((rows, width), F32),
        name="sum_" + name, compiler_params=_params(("arbitrary",), 32))(parts)


def _adamw(g, w, m, v, name):
    rows, width = w.shape
    tr = _row_tile(rows)

    def body(g_ref, w_ref, m_ref, v_ref, d_ref, nm_ref, nv_ref):
        d_ref[...], nm_ref[...], nv_ref[...] = _adamw_math(g_ref[...], w_ref[...], m_ref[...], v_ref[...])

    blk = pl.BlockSpec((tr, width), _row)
    return pl.pallas_call(
        body, grid=(rows // tr,), in_specs=[blk] * 4, out_specs=[blk] * 3, out_shape=[SDS((rows, width), F32)] * 3,
        name="adamw_" + name, compiler_params=_params(("arbitrary",), 32))(g, w, m, v)


SMALL = ("w_spatial", "ln_pre_mix", "ln_post_mix", "ln_pre_ffn", "ln_post_ffn", "b_pe_gate",
         "attn_out_norm", "sgu_out_norm", "b_spatial", "sgu_ln_g", "sgu_ln_b")
SMALL_GROUPS = ((128, ("w_spatial", "b_spatial", "sgu_ln_g", "sgu_ln_b")),
                (512, ("attn_out_norm", "sgu_out_norm")),
                (1024, ("ln_post_mix", "ln_pre_ffn", "ln_post_ffn", "b_pe_gate")))
SMALL_LATE = "ln_pre_mix"
SMALL_SIZE = dict(w_spatial=N_GROUPS * CHUNK * CHUNK, b_spatial=N_GROUPS * CHUNK, sgu_ln_g=GROUP_DIM, sgu_ln_b=GROUP_DIM,
                  attn_out_norm=ATTN_W, sgu_out_norm=SGU_W, ln_pre_mix=D_MODEL, ln_post_mix=D_MODEL, ln_pre_ffn=D_MODEL,
                  ln_post_ffn=D_MODEL, b_pe_gate=D_MODEL)
SUBLANES = 8
ROW_SHARDED = ("w_out", "w_down", "w_pe_gate")
COL_SHARDED = ("w_in", "w_gate_up", "w_pe_proj")
WEIGHTS = ("ln_pre_mix", "w_in", "sgu_ln_g", "sgu_ln_b", "w_spatial", "b_spatial", "attn_out_norm", "sgu_out_norm",
           "w_out", "ln_post_mix", "ln_pre_ffn", "w_gate_up", "w_down", "ln_post_ffn", "w_pe_gate", "b_pe_gate",
           "w_pe_proj")


def _group_rows(width, names, extra=0):
    rows = sum(SMALL_SIZE[n] // width for n in names) + extra
    return -(-rows // SUBLANES) * SUBLANES


def _pack_small_grads(gs, loss_term):
    packed = []
    for width, names in SMALL_GROUPS:
        rows = [gs[n].reshape(-1, width) for n in names]
        extra = int(width == D_MODEL)
        if extra:
            rows.append(jnp.full((1, width), loss_term, F32))
        used = sum(r.shape[0] for r in rows)
        rows.append(jnp.zeros((_group_rows(width, names, extra) - used, width), F32))
        packed.append(jnp.concatenate(rows, axis=0))
    return packed


def _small_adamw(arrived, arrived_late, w, m, v):
    names = [n for _, ns in SMALL_GROUPS for n in ns] + [SMALL_LATE]
    n_groups = len(SMALL_GROUPS)

    def body(*refs):
        group_refs, late_ref = refs[:n_groups], refs[n_groups]
        state = refs[n_groups + 1:n_groups + 1 + 3 * len(names)]
        outs = refs[n_groups + 1 + 3 * len(names):]
        sums = [_sum_parts(r) for r in group_refs]

        def update(name, g):
            i = names.index(name)
            w_ref, m_ref, v_ref = state[3 * i:3 * i + 3]
            delta, nm, nv = _adamw_math(g, w_ref[...].reshape(g.shape), m_ref[...].reshape(g.shape),
                                        v_ref[...].reshape(g.shape))
            for o_ref, val in zip(outs[4 * i:4 * i + 4], (g, delta, nm, nv)):
                o_ref[...] = val.reshape(o_ref.shape)

        for (width, group), total in zip(SMALL_GROUPS, sums):
            row = 0
            for name in group:
                rows = SMALL_SIZE[name] // width
                update(name, total[row:row + rows, :])
                row += rows
            if width == D_MODEL:
                outs[-1][...] = total[row:row + 1, :LANES]
        update(SMALL_LATE, _sum_parts(late_ref)[:1, :])

    state = [t[n] for n in names for t in (w, m, v)]
    plain = jax.ShapeDtypeStruct
    out_shape = [plain(w[n].shape, F32) for n in names for _ in range(4)] + [plain((1, LANES), F32)]
    outs = pl.pallas_call(body, out_shape=out_shape, name="adamw_small",
                          compiler_params=pltpu.CompilerParams(vmem_limit_bytes=32 * MIB))(*arrived, arrived_late, *state)
    return {n: outs[4 * i:4 * i + 4] for i, n in enumerate(names)}, outs[-1]


def _slabs(full):
    return full.reshape(N_DEV, full.shape[0] // N_DEV, full.shape[1])


def kernel(x, p, ln_pre_mix, w_in, sgu_ln_g, sgu_ln_b, w_spatial, b_spatial, attn_out_norm, sgu_out_norm, w_out, ln_post_mix, ln_pre_ffn, w_gate_up, w_down, ln_post_ffn, w_pe_gate, b_pe_gate, w_pe_proj, loss_target, m_ln_pre_mix, m_w_in, m_sgu_ln_g, m_sgu_ln_b, m_w_spatial, m_b_spatial, m_attn_out_norm, m_sgu_out_norm, m_w_out, m_ln_post_mix, m_ln_pre_ffn, m_w_gate_up, m_w_down, m_ln_post_ffn, m_w_pe_gate, m_b_pe_gate, m_w_pe_proj, v_ln_pre_mix, v_w_in, v_sgu_ln_g, v_sgu_ln_b, v_w_spatial, v_b_spatial, v_attn_out_norm, v_sgu_out_norm, v_w_out, v_ln_post_mix, v_ln_pre_ffn, v_w_gate_up, v_w_down, v_ln_post_ffn, v_w_pe_gate, v_b_pe_gate, v_w_pe_proj):
    given = dict(locals())
    w = {n: given[n] for n in WEIGHTS}
    m = {n: given["m_" + n] for n in WEIGHTS}
    v = {n: given["v_" + n] for n in WEIGHTS}
    xs, ps, target = x[0], p[0, 0], loss_target[0]

    shard = {n: w[n][0].astype(BF) for n in ROW_SHARDED}
    shard.update({n: w[n][0].T.astype(BF) for n in COL_SHARDED})
    sm = {n: w[n][0] for n in SMALL}
    sm = {n: (a.reshape(1, -1) if a.ndim == 1 else a) for n, a in sm.items()}
    slopes = jnp.broadcast_to((2.0 ** -(jnp.arange(8, dtype=F32) + 1.0)).reshape(4, 2, 1), (4, 2, LANES))
    slopes = jnp.concatenate([slopes, jnp.zeros((4, 6, LANES), F32)], axis=1)
    b_st = jnp.pad(sm["b_spatial"].T, ((0, 0), (0, LANES - N_GROUPS)))

    def full(gathered):
        return gathered.reshape(-1, gathered.shape[-1])

    w_in_t = full(_all_gather(shard["w_in"], "gather_w_in"))
    proj, a = _in_proj(xs, sm["ln_pre_mix"], w_in_t)
    later = ("w_out", "w_gate_up", "w_down", "w_pe_gate", "w_pe_proj")
    attn, lse, gathered = _attn_fwd(proj, slopes, [shard[n] for n in later])
    w_out_f, w_gu_t, w_down_f, w_peg_f, w_pep_t = [full(g) for g in gathered]
    sgu, groups, mixed, h1, f = _out_proj(attn, proj, xs, sm["sgu_ln_g"], sm["sgu_ln_b"], sm["w_spatial"], b_st,
                                          sm["attn_out_norm"], sm["sgu_out_norm"], w_out_f,
                                          sm["ln_post_mix"], sm["ln_pre_ffn"])
    g, u, act = _gate_up(f, w_gu_t)
    y, h2 = _down_proj(act, w_down_f, h1, sm["ln_post_ffn"])
    dh2, dy, g_peg, g_pep_t, loss_cols, db_peg, d_pff = _pe_loss_and_bwd(
        h2, ps, target, y, w_peg_f, sm["b_pe_gate"], w_pep_t, sm["ln_post_ffn"])
    loss_term = 0.5 * jnp.sum(loss_cols) * (1.0 / D_MODEL)

    arrived = {}
    g_down = _weight_grad(act, dy, "w_down")
    dg, du, (arrived["w_pe_proj"], arrived["w_pe_gate"]) = _down_bwd(dy, w_down_f, g, u, [_slabs(g_pep_t), _slabs(g_peg)])
    dh1, dmx, d_pf, d_pm, (arrived["w_down"],) = _ffn_in_bwd(dg, du, w_gu_t, h1, dh2, mixed, sm["ln_pre_ffn"],
                                                            sm["ln_post_mix"], [_slabs(g_down)])
    dattn, stats, dsgu, d_ga, d_gs, g_gu_t = _out_bwd(dmx, w_out_f, attn, lse, sgu, sm["attn_out_norm"],
                                                      sm["sgu_out_norm"], dg, f)
    duz, d_ws, d_bst, d_lg, d_lb, g_out, g_gu_t = _sgu_bwd(proj, dsgu, sm["sgu_ln_g"], sm["sgu_ln_b"],
                                                           sm["w_spatial"], b_st, groups, dmx, du, f, g_gu_t)
    gs = dict(sgu_ln_g=d_lg, sgu_ln_b=d_lb, w_spatial=d_ws, b_spatial=d_bst[:, :N_GROUPS].T, attn_out_norm=d_ga,
              sgu_out_norm=d_gs, ln_post_mix=d_pm, ln_pre_ffn=d_pf, ln_post_ffn=d_pff, b_pe_gate=db_peg)
    small_grads = _pack_small_grads(gs, loss_term)
    dqkv, (arrived["w_gate_up"], arrived["w_out"], *arrived_small) = _attn_bwd(
        proj, dattn, stats, slopes, [_slabs(g_gu_t), _slabs(g_out), *small_grads],
        [True, True] + [False] * len(small_grads))
    send_sems, recv_sems, slabs, landing, token = _scatter_begin(_slabs(_grad_w_in([dqkv, duz], a)), "w_in_grad_send")
    grad_x, d_g1 = _in_bwd([dqkv, duz], w_in_t, xs, dh1, sm["ln_pre_mix"] + token[:1, :1])
    slabs, landing = _scatter_end(send_sems, recv_sems, slabs, landing, d_g1, "w_in_grad_arrive")
    me = 4 * lax.axis_index("x") + 2 * lax.axis_index("y") + lax.axis_index("c")
    own = lax.dynamic_slice_in_dim(slabs, me, 1, axis=0)
    arrived["w_in"] = lax.dynamic_update_slice_in_dim(landing, own, me, axis=0)
    (arrived_late,) = _scatter_call([jnp.pad(d_g1, ((0, SUBLANES - 1), (0, 0)))], [False], "ln_pre_mix_grad_exchange")

    res = {}
    for n in ROW_SHARDED:
        res[n] = _reduce_adamw(arrived[n], w[n][0], m[n][0], v[n][0], n)
    for n in ("w_in", "w_gate_up"):
        res[n] = [t.T for t in _reduce_adamw(arrived[n], w[n][0].T, m[n][0].T, v[n][0].T, n)]
    for n in ("w_pe_proj",):
        grad = _reduce(arrived[n], n).T
        res[n] = (grad, *_adamw(grad, w[n][0], m[n][0], v[n][0], n))
    small, loss_row = _small_adamw(arrived_small, arrived_late, w, m, v)

    out = []
    for k in range(4):
        out += [res[n][k][None] if n in res else small[n][k] for n in WEIGHTS]
    return (loss_row[0, 0], grad_x[None], *out)
```

```python
import math

import jax
import jax.numpy as jnp
from jax import lax
from jax.experimental import pallas as pl
from jax.experimental.pallas import tpu as pltpu

F32 = jnp.float32
BF = jnp.bfloat16


def SDS(shape, dtype):
    return pltpu.HBM(tuple(shape), dtype)

D_MODEL = 1024
ATTN_W = 512
SGU_W = 512
HEAD_DIM = 64
N_GROUPS = 4
GROUP_DIM = 128
CHUNK = 128
D_FF = 2816
PLE_DIM = 256
PROJ = 3 * ATTN_W + 2 * SGU_W
DILATIONS = (1, 4, 16)
QBLK = 128
EPS = 1e-6
NEG = -1e30
N_DEV = 8
LANES = 128

ADAM_LR = 0.001
ADAM_B1 = 0.9
ADAM_B2 = 0.999
ADAM_EPS = 1e-08
ADAM_WD = 0.01
ADAM_STEP = 10

MIB = 2 ** 20
MESH_ID = pl.DeviceIdType.MESH
HBM = pl.BlockSpec(memory_space=pl.ANY)


def _params(sem, vmem_mib):
    return pltpu.CompilerParams(dimension_semantics=sem, vmem_limit_bytes=vmem_mib * MIB)


def _dot(a, b):
    return jnp.dot(a, b, preferred_element_type=F32)


def _dot_nt(a, b):
    return lax.dot_general(a, b, (((1,), (1,)), ((), ())), preferred_element_type=F32)


def _dot_tn(a, b):
    return lax.dot_general(a, b, (((0,), (0,)), ((), ())), preferred_element_type=F32)


def _rstd(x):
    return lax.rsqrt(jnp.mean(x * x, axis=-1, keepdims=True) + EPS)


def _rms_bwd(dy, x, g):
    r = _rstd(x)
    n = x * r
    dn = dy * g
    dx = r * (dn - n * jnp.mean(dn * n, axis=-1, keepdims=True))
    return dx, dy * n


def _colsum(v):
    return jnp.sum(v, axis=0, keepdims=True)


_G0 = math.sqrt(2.0 / math.pi)
_G1 = 0.044715


def _gelu(x):
    t = jnp.tanh(_G0 * (x + _G1 * x * x * x))
    return 0.5 * x * (1.0 + t), t


def _gelu_grad(x, t):
    return 0.5 * (1.0 + t) + 0.5 * x * (1.0 - t * t) * (_G0 * (1.0 + 3.0 * _G1 * x * x))


def _sigmoid(x):
    return 0.5 * jnp.tanh(0.5 * x) + 0.5


def _row(i):
    return (i, 0)


def _fixed(i):
    return (0, 0)


def _acc_init(step, *refs):
    @pl.when(step == 0)
    def _():
        for r in refs:
            r[...] = jnp.zeros_like(r)


FLIPS = [(dx, dy, dc) for dx in (0, 1) for dy in (0, 1) for dc in (0, 1)][1:]
DMA_SEMS = pltpu.SemaphoreType.DMA


def _mesh_pos():
    return lax.axis_index("x"), lax.axis_index("y"), lax.axis_index("c")


def _remote(src, dst, sems, n, to):
    return pltpu.make_async_remote_copy(src_ref=src, dst_ref=dst, send_sem=sems[0].at[n], recv_sem=sems[1].at[n],
                                        device_id=to, device_id_type=MESH_ID)


class _Scatter:
    def __init__(self, items, sems):
        x, y, c = _mesh_pos()
        me = 4 * x + 2 * y + c
        self.local, self.sends, self.arrivals = [], [], []
        for i, (src, dst, slabbed) in enumerate(items):
            self.local.append(pltpu.make_async_copy(src.at[me] if slabbed else src, dst.at[me], sems[2].at[i]))
            for k, (dx, dy, dc) in enumerate(FLIPS):
                to = (1 - x if dx else x, 1 - y if dy else y, 1 - c if dc else c)
                peer = 4 * to[0] + 2 * to[1] + to[2]
                out = src.at[peer] if slabbed else src
                self.sends.append(_remote(out, dst.at[me], sems, 7 * i + k, to))
                self.arrivals.append(_remote(out, dst.at[peer], sems, 7 * i + k, to))

    def start(self):
        for cp in self.local + self.sends:
            cp.start()

    def wait(self):
        for cp in self.arrivals:
            cp.wait_recv()
        for cp in self.sends:
            cp.wait_send()
        for cp in self.local:
            cp.wait()


def _scatter_sems(n):
    return [DMA_SEMS((7 * n,)), DMA_SEMS((7 * n,)), DMA_SEMS((n,))]


class _Gather:
    def __init__(self, items, sems):
        x, y, c = _mesh_pos()
        me, sibling = (x, y, c), (x, y, 1 - c)
        chips = [(1 - x, y), (x, 1 - y), (1 - x, 1 - y)]
        self.first, self.passed, self.from_chips, self.rest, self.local = [], [], [], [], []
        for i, (src, dst) in enumerate(items):
            def slot(p, dst=dst):
                return dst.at[4 * p[0] + 2 * p[1] + p[2]]

            def copy(k, block, to, own=False, i=i, src=src, slot=slot):
                return _remote(src if own else slot(block), slot(block), sems, 7 * i + k, to)

            self.local.append(pltpu.make_async_copy(src, slot(me), sems[2].at[i]))
            self.first.append(copy(0, me, sibling, own=True))
            self.first += [copy(1 + j, me, (*chip, c), own=True) for j, chip in enumerate(chips)]
            self.passed += [copy(4 + j, (*chip, c), sibling) for j, chip in enumerate(chips)]
            self.from_chips += [copy(1 + j, (*chip, c), me) for j, chip in enumerate(chips)]
            self.rest.append(copy(0, sibling, me))
            self.rest += [copy(4 + j, (*chip, 1 - c), me) for j, chip in enumerate(chips)]

    def start(self):
        for cp in self.local + self.first:
            cp.start()

    def forward(self):
        for arrived, onward in zip(self.from_chips, self.passed):
            arrived.wait_recv()
            onward.start()

    def finish(self):
        for cp in self.rest:
            cp.wait_recv()
        for cp in self.first + self.passed:
            cp.wait_send()
        for cp in self.local:
            cp.wait()


def _all_gather(shard, name):
    def body(x_ref, out_ref, *sems):
        g = _Gather([(x_ref, out_ref)], sems)
        g.start()
        g.forward()
        g.finish()

    return pl.pallas_call(
        body, out_shape=SDS((N_DEV,) + shard.shape, shard.dtype), in_specs=[HBM], out_specs=HBM,
        scratch_shapes=_scatter_sems(1), name=name)(shard)


def _scatter_call(srcs, slabbed, name):
    n = len(srcs)

    def body(*refs):
        sc = _Scatter(list(zip(refs[:n], refs[n:2 * n], slabbed)), refs[2 * n:])
        sc.start()
        sc.wait()

    shapes = [SDS(s.shape if sl else (N_DEV,) + s.shape, s.dtype) for s, sl in zip(srcs, slabbed)]
    return pl.pallas_call(body, out_shape=shapes, in_specs=[HBM] * n, out_specs=[HBM] * n,
                          scratch_shapes=_scatter_sems(n), name=name)(*srcs)


SEM = pl.BlockSpec(memory_space=pltpu.SEMAPHORE)
N_PEERS = len(FLIPS)


def _slab_copies(src_ref, land_ref, send_sems, recv_sems):
    x, y, c = _mesh_pos()
    me = 4 * x + 2 * y + c
    copies = []
    for k, (dx, dy, dc) in enumerate(FLIPS):
        to = (1 - x if dx else x, 1 - y if dy else y, 1 - c if dc else c)
        peer = 4 * to[0] + 2 * to[1] + to[2]
        sems = (send_sems, recv_sems)
        copies.append((_remote(src_ref.at[peer], land_ref.at[me], sems, k, to),
                       _remote(src_ref.at[peer], land_ref.at[peer], sems, k, to)))
    return copies


def _scatter_begin(src, name):
    def body(src_ref, land_ref, send_sems, recv_sems, src_thru, land_thru, token):
        for send, _ in _slab_copies(src_ref, land_ref, send_sems, recv_sems):
            send.start()
        token[...] = jnp.zeros_like(token)

    landing = lax.empty(src.shape, src.dtype)
    return pl.pallas_call(
        body, name=name,
        out_shape=(pltpu.SemaphoreType.DMA((N_PEERS,)), pltpu.SemaphoreType.DMA((N_PEERS,)),
                   pltpu.HBM(src.shape, src.dtype), pltpu.HBM(src.shape, src.dtype),
                   jax.ShapeDtypeStruct((SUBLANES, LANES), F32)),
        in_specs=(HBM, HBM), out_specs=(SEM, SEM, HBM, HBM, pl.BlockSpec(memory_space=pltpu.VMEM)),
        input_output_aliases={0: 2, 1: 3},
        compiler_params=pltpu.CompilerParams(has_side_effects=pltpu.SideEffectType.DATAFLOW_SIDE_EFFECTING))(
            pltpu.with_memory_space_constraint(src, pltpu.HBM), pltpu.with_memory_space_constraint(landing, pltpu.HBM))


def _scatter_end(send_sems, recv_sems, src_thru, land_thru, after, name):
    def body(src_ref, land_ref, send_sems, recv_sems, after_ref, src_dead, land_out):
        for send, arrival in _slab_copies(src_ref, land_ref, send_sems, recv_sems):
            send.wait_send()
            arrival.wait_recv()

    return pl.pallas_call(
        body, name=name,
        out_shape=(pltpu.HBM(src_thru.shape, src_thru.dtype), pltpu.HBM(land_thru.shape, land_thru.dtype)),
        in_specs=(HBM, HBM, SEM, SEM, HBM), out_specs=(HBM, HBM), input_output_aliases={0: 0, 1: 1},
        compiler_params=pltpu.CompilerParams(has_side_effects=pltpu.SideEffectType.DATAFLOW_SIDE_EFFECTING))(
            src_thru, land_thru, send_sems, recv_sems, after)


def _in_proj(x, g1, w_in_t, tm=512):
    T = x.shape[0]

    def body(x_ref, g_ref, w_ref, proj_ref, a_ref):
        xv = x_ref[...]
        a = (xv * _rstd(xv) * g_ref[...]).astype(BF)
        a_ref[...] = a
        proj_ref[...] = _dot_nt(a, w_ref[...])

    return pl.pallas_call(
        body, grid=(T // tm,),
        in_specs=[pl.BlockSpec((tm, D_MODEL), _row), pl.BlockSpec((1, D_MODEL), _fixed),
                  pl.BlockSpec((PROJ, D_MODEL), _fixed)],
        out_specs=[pl.BlockSpec((tm, PROJ), _row), pl.BlockSpec((tm, D_MODEL), _row)],
        out_shape=[SDS((T, PROJ), F32), SDS((T, D_MODEL), BF)],
        name="in_proj", compiler_params=_params(("arbitrary",), 48))(x, g1, w_in_t)


ATTN_GROUP = 16


def _attn_bias(sl_ref, bias_ref):
    qi = lax.broadcasted_iota(jnp.int32, (QBLK, QBLK), 0)
    kj = lax.broadcasted_iota(jnp.int32, (QBLK, QBLK), 1)
    step = qi - kj
    for di, d in enumerate(DILATIONS):
        for j in range(2):
            sl = sl_ref[0, j:j + 1, :]
            cur = jnp.where(step >= 0, -sl * (step * d).astype(F32), NEG)
            prev = jnp.where(step <= 0, -sl * ((step + QBLK) * d).astype(F32), NEG)
            rows = slice(j * QBLK, (j + 1) * QBLK)
            bias_ref[di * 2, rows, :QBLK] = prev
            bias_ref[di * 2, rows, QBLK:] = cur
            bias_ref[di * 2 + 1, rows, :QBLK] = jnp.full((QBLK, QBLK), NEG, F32)
            bias_ref[di * 2 + 1, rows, QBLK:] = cur


def _stack_heads(x, lo):
    return jnp.concatenate([jnp.where(lo, x, 0.0), jnp.where(lo, 0.0, x)], axis=0).astype(BF)


def _unstack_heads(x, lo):
    return jnp.where(lo, x[:QBLK], x[QBLK:])


def _attn_rows(start, d, blocks=1):
    if d == 1:
        return pl.ds(pl.multiple_of(start, QBLK), blocks * QBLK)
    return pl.ds(start, blocks * QBLK, stride=d)


def _attn_group_index(i, nblk, d, group):
    per = nblk // d // group
    r = i // per
    n0 = (i % per) * group
    start = r + (d * QBLK) * n0
    pstart = jnp.maximum(start - d * QBLK, r)
    return start, pstart, n0 == 0


def _attn_plan(nblk, d):
    group = min(ATTN_GROUP, nblk // d)
    return group, max(1, min(ATTN_GROUP // group, d))


def _attn_fwd(proj, slopes, to_gather):
    T = proj.shape[0]
    nblk = T // QBLK
    n_g = len(to_gather)

    def body(q_ref, k_ref, v_ref, sl_ref, *rest):
        srcs, (o_ref, m_ref), dsts = rest[:n_g], rest[n_g:n_g + 2], rest[n_g + 2:2 * n_g + 2]
        sems, (l_ref, bias_ref) = rest[2 * n_g + 2:2 * n_g + 5], rest[2 * n_g + 5:]
        h = pl.program_id(0)

        @pl.when(h == 0)
        def _():
            _Gather(list(zip(srcs, dsts)), sems).start()

        @pl.when(h == pl.num_programs(0) - 1)
        def _():
            _Gather(list(zip(srcs, dsts)), sems).forward()

        _attn_bias(sl_ref, bias_ref)
        lo = lax.broadcasted_iota(jnp.int32, (1, LANES), 1) < HEAD_DIM

        order = list(enumerate(DILATIONS))[::-1]
        for di, d in order:
            group, segs = _attn_plan(nblk, d)
            fresh, last = di == order[0][0], di == order[-1][0]

            def step(i, carry, segs=segs, **kw):
                for s in range(segs):
                    segment(i * segs + s, **kw)
                return carry

            def segment(i, d=d, di=di, group=group, fresh=fresh, last=last):
                start, pstart, first = _attn_group_index(i, nblk, d, group)
                prows = _attn_rows(pstart, d)
                alone = group == nblk // d
                k_prev = v_prev = None
                if not alone:
                    k_prev, v_prev = k_ref[prows, :].astype(BF), v_ref[prows, :].astype(BF)
                for b in range(group):
                    out = _attn_rows(start + b * (d * QBLK), d)
                    k_own, v_own = k_ref[out, :].astype(BF), v_ref[out, :].astype(BF)
                    if alone and b == 0:
                        k2, v2, bias = k_own, v_own, bias_ref[di * 2, :, QBLK:]
                    else:
                        k2, v2 = jnp.concatenate([k_prev, k_own], axis=0), jnp.concatenate([v_prev, v_own], axis=0)
                        bias = bias_ref[di * 2 + first.astype(jnp.int32)] if b == 0 else bias_ref[di * 2]
                    k_prev, v_prev = k_own, v_own
                    s = _dot_nt(_stack_heads(q_ref[out, :] * (HEAD_DIM ** -0.5), lo), k2) + bias
                    m = jnp.max(s, axis=-1, keepdims=True)
                    pr = jnp.exp(s - m)
                    m_b = _unstack_heads(m, lo)
                    l_b = _unstack_heads(jnp.sum(pr, axis=-1, keepdims=True), lo)
                    o_b = _unstack_heads(_dot(pr.astype(BF), v2), lo)
                    if fresh:
                        m_ref[out, :] = m_b
                        l_ref[out, :] = l_b
                        o_ref[out, :] = o_b
                        continue
                    m_o = m_ref[out, :]
                    m_n = jnp.maximum(m_o, m_b)
                    wa, wb = jnp.exp(m_o - m_n), jnp.exp(m_b - m_n)
                    l_n = wa * l_ref[out, :] + wb * l_b
                    o_n = wa * o_ref[out, :] + wb * o_b
                    if last:
                        m_ref[out, :] = m_n + jnp.log(l_n)
                        o_ref[out, :] = o_n / l_n
                    else:
                        m_ref[out, :] = m_n
                        l_ref[out, :] = l_n
                        o_ref[out, :] = o_n

            lax.fori_loop(0, nblk // (group * segs), step, 0)

        @pl.when(h == pl.num_programs(0) - 1)
        def _():
            _Gather(list(zip(srcs, dsts)), sems).finish()

    col = lambda base: pl.BlockSpec((T, LANES), lambda h: (0, base + h))
    tok = pl.BlockSpec((T, LANES), lambda h: (0, h))
    outs = pl.pallas_call(
        body, grid=(4,),
        in_specs=[col(0), col(4), col(8), pl.BlockSpec((1, 8, LANES), lambda h: (h, 0, 0))] + [HBM] * n_g,
        out_specs=[tok, tok] + [HBM] * n_g,
        out_shape=[SDS((T, ATTN_W), F32), SDS((T, ATTN_W), F32)]
        + [SDS((N_DEV,) + g.shape, g.dtype) for g in to_gather],
        scratch_shapes=_scatter_sems(n_g) + [pltpu.VMEM((T, LANES), F32), pltpu.VMEM((6, 2 * QBLK, 2 * QBLK), F32)],
        name="attn_fwd", compiler_params=_params(("arbitrary",), 56))(proj, proj, proj, slopes, *to_gather)
    return outs[0], outs[1], outs[2:]


def _sgu_norm(zv, ln_g, ln_b):
    gz, tz = _gelu(zv)
    mu = jnp.mean(gz, axis=-1, keepdims=True)
    xc = gz - mu
    rs = lax.rsqrt(jnp.mean(xc * xc, axis=-1, keepdims=True) + EPS)
    xhat = xc * rs
    return xhat * ln_g + ln_b, xhat, rs, tz


def _causal(w):
    i = lax.broadcasted_iota(jnp.int32, (CHUNK, CHUNK), 0)
    j = lax.broadcasted_iota(jnp.int32, (CHUNK, CHUNK), 1)
    return jnp.where(i >= j, w, 0.0)


def _sgu_fwd_tile(u_ref, z_ref, g_ref, b_ref, w_ref, bs_ref, out_ref):
    for g in range(N_GROUPS):
        wm = _causal(w_ref[g]).astype(BF)
        cols = slice(g * GROUP_DIM, (g + 1) * GROUP_DIM)
        for c in range(u_ref.shape[0] // CHUNK):
            rows = slice(c * CHUNK, (c + 1) * CHUNK)
            zn, _, _, _ = _sgu_norm(z_ref[rows, cols], g_ref[...], b_ref[...])
            mixed = _dot(wm, zn.astype(BF)) + bs_ref[:, g:g + 1]
            gu, _ = _gelu(u_ref[rows, cols])
            out_ref[rows, cols] = gu * mixed


def _out_proj(attn, proj, x, ln_g, ln_b, w_s, b_st, g_a, g_s, w_out, g_pm, g_pf, tm=512):
    T = x.shape[0]

    def body(a_ref, u_ref, z_ref, x_ref, lg_ref, lb_ref, ws_ref, bs_ref, ga_ref, gs_ref, w_ref, gpm_ref, gpf_ref,
             s_ref, grp_ref, mixed_ref, h1_ref, f_ref):
        _sgu_fwd_tile(u_ref, z_ref, lg_ref, lb_ref, ws_ref, bs_ref, s_ref)
        av, sv = a_ref[...], s_ref[...]
        an = (av * _rstd(av) * ga_ref[...]).astype(BF)
        sn = (sv * _rstd(sv) * gs_ref[...]).astype(BF)
        grp_ref[:, :ATTN_W] = an
        grp_ref[:, ATTN_W:] = sn
        mixed = _dot(an, w_ref[:ATTN_W, :]) + _dot(sn, w_ref[ATTN_W:, :])
        mixed_ref[...] = mixed
        h1 = x_ref[...] + mixed * _rstd(mixed) * gpm_ref[...]
        h1_ref[...] = h1
        f_ref[...] = (h1 * _rstd(h1) * gpf_ref[...]).astype(BF)

    tok = lambda w: pl.BlockSpec((tm, w), _row)
    vec = lambda w: pl.BlockSpec((1, w), _fixed)
    return pl.pallas_call(
        body, grid=(T // tm,),
        in_specs=[tok(ATTN_W), pl.BlockSpec((tm, SGU_W), lambda i: (i, 3)), pl.BlockSpec((tm, SGU_W), lambda i: (i, 4)),
                  tok(D_MODEL), vec(GROUP_DIM), vec(GROUP_DIM), pl.BlockSpec((N_GROUPS, CHUNK, CHUNK), lambda i: (0, 0, 0)),
                  pl.BlockSpec((CHUNK, LANES), _fixed), vec(ATTN_W), vec(SGU_W),
                  pl.BlockSpec((D_MODEL, D_MODEL), _fixed), vec(D_MODEL), vec(D_MODEL)],
        out_specs=[tok(SGU_W)] + [tok(D_MODEL)] * 4,
        out_shape=[SDS((T, SGU_W), F32), SDS((T, D_MODEL), BF), SDS((T, D_MODEL), F32), SDS((T, D_MODEL), F32),
                   SDS((T, D_MODEL), BF)],
        name="out_proj", compiler_params=_params(("arbitrary",), 52))(
            attn, proj, proj, x, ln_g, ln_b, w_s, b_st, g_a, g_s, w_out, g_pm, g_pf)


FF_TILE = 1408
FF_TILES = D_FF // FF_TILE
FF_CHUNK = 256


def _gate_up(f, w_gu_t, tm=512):
    T = f.shape[0]
    tn = FF_TILE

    def body(f_ref, wg_ref, wu_ref, g_ref, u_ref, act_ref):
        fv = f_ref[...]
        g = _dot_nt(fv, wg_ref[...])
        u = _dot_nt(fv, wu_ref[...])
        g_ref[...] = g.astype(BF)
        u_ref[...] = u.astype(BF)
        act_ref[...] = (g * _sigmoid(g) * u).astype(BF)

    ospec = pl.BlockSpec((tm, tn), lambda j, i: (i, j))
    return pl.pallas_call(
        body, grid=(FF_TILES, T // tm),
        in_specs=[pl.BlockSpec((tm, D_MODEL), lambda j, i: (i, 0)), pl.BlockSpec((tn, D_MODEL), lambda j, i: (j, 0)),
                  pl.BlockSpec((tn, D_MODEL), lambda j, i: (j + FF_TILES, 0))],
        out_specs=[ospec] * 3, out_shape=[SDS((T, D_FF), BF)] * 3,
        name="gate_up", compiler_params=_params(("arbitrary", "arbitrary"), 40))(f, w_gu_t, w_gu_t)


def _pe_loss_and_bwd(act, w_down, h1, p, target, w_peg, b_peg, w_pep_t, g_pff, tm=512):
    T = h1.shape[0]

    def body(a_ref, wd_ref, h1_ref, p_ref, t_ref, wg_ref, b_ref, wp_ref, g_ref,
             dh2_ref, dy_ref, gpeg_ref, gpep_ref, loss_ref, db_ref, dg_ref):
        _acc_init(pl.program_id(0), gpeg_ref, gpep_ref, loss_ref, db_ref, dg_ref)
        y = _dot(a_ref[...], wd_ref[...])
        h2v = h1_ref[...] + y * _rstd(y) * g_ref[...]
        h2b = h2v.astype(BF)
        pb = p_ref[...].astype(BF)
        gate = _sigmoid(_dot(h2b, wg_ref[...]) + b_ref[...])
        pp = _dot_nt(pb, wp_ref[...])
        diff = h2v + gate * pp - t_ref[...]
        loss_ref[...] += _colsum(diff * diff)
        dh3 = diff * (1.0 / D_MODEL)
        dpre = dh3 * pp * (gate * (1.0 - gate))
        dpre_b = dpre.astype(BF)
        db_ref[...] += _colsum(dpre)
        gpeg_ref[...] += _dot_tn(h2b, dpre_b)
        gpep_ref[...] += _dot_tn((dh3 * gate).astype(BF), pb)
        dh2 = dh3 + _dot_nt(dpre_b, wg_ref[...])
        dh2_ref[...] = dh2
        dy, dg = _rms_bwd(dh2, y, g_ref[...])
        dy_ref[...] = dy.astype(BF)
        dg_ref[...] += _colsum(dg)

    tok = lambda w: pl.BlockSpec((tm, w), _row)
    vec = pl.BlockSpec((1, D_MODEL), _fixed)
    wg = pl.BlockSpec((D_MODEL, D_MODEL), _fixed)
    wp = pl.BlockSpec((D_MODEL, PLE_DIM), _fixed)
    return pl.pallas_call(
        body, grid=(T // tm,),
        in_specs=[tok(D_FF), pl.BlockSpec((D_FF, D_MODEL), _fixed, pipeline_mode=pl.Buffered(1)), tok(D_MODEL),
                  tok(PLE_DIM), tok(D_MODEL), wg, vec, wp, vec],
        out_specs=[tok(D_MODEL), tok(D_MODEL), wg, wp, vec, vec, vec],
        out_shape=[SDS((T, D_MODEL), F32), SDS((T, D_MODEL), BF), SDS((D_MODEL, D_MODEL), F32),
                   SDS((D_MODEL, PLE_DIM), F32)] + [SDS((1, D_MODEL), F32)] * 3,
        name="pe_loss_and_bwd", compiler_params=_params(("arbitrary",), 56))(
            act, w_down, h1, p, target, w_peg, b_peg, w_pep_t, g_pff)


def _weight_grad(a, dy, name, into=None, row_tile=0, rows=None, tk=1024):
    n = dy.shape[1]
    tn = min(n, 1024)
    T, ka = a.shape
    tka = FF_TILE if ka == D_FF else min(ka, 1024)
    rows = ka if rows is None else rows

    def body(a_ref, dy_ref, *rest):
        out_ref = rest[-1]
        _acc_init(pl.program_id(2), out_ref)
        out_ref[...] += _dot_tn(a_ref[...].astype(BF), dy_ref[...].astype(BF))

    carried = [] if into is None else [into]
    return pl.pallas_call(
        body, grid=(ka // tka, n // tn, T // tk),
        in_specs=[pl.BlockSpec((tk, tka), lambda i, j, k: (k, i)), pl.BlockSpec((tk, tn), lambda i, j, k: (k, j))]
        + [HBM] * len(carried),
        out_specs=pl.BlockSpec((tka, tn), lambda i, j, k: (i + row_tile, j)),
        out_shape=SDS((rows, n), F32), input_output_aliases={2: 0} if carried else {},
        name="grad_" + name, compiler_params=_params(("arbitrary",) * 3, 40))(a, dy, *carried)


def _grad_w_in(dparts, a, tk=1024):
    T = a.shape[0]

    def body(*refs):
        d_refs, a_ref, out_ref, acc_ref = refs[:len(dparts)], refs[-3], refs[-2], refs[-1]
        k = pl.program_id(0)
        _acc_init(k, acc_ref)
        cols = [r[part].astype(BF) for r in d_refs for part in range(r.shape[0])]
        acc_ref[...] += _dot_tn(jnp.concatenate(cols, axis=1), a_ref[...])

        @pl.when(k == pl.num_programs(0) - 1)
        def _():
            out_ref[...] = acc_ref[...].astype(BF)

    return pl.pallas_call(
        body, grid=(T // tk,),
        in_specs=[pl.BlockSpec((d.shape[0], tk, d.shape[2]), lambda k: (0, k, 0)) for d in dparts]
        + [pl.BlockSpec((tk, D_MODEL), lambda k: (k, 0))],
        out_specs=pl.BlockSpec((PROJ, D_MODEL), lambda k: (0, 0)),
        out_shape=SDS((PROJ, D_MODEL), BF), scratch_shapes=[pltpu.VMEM((PROJ, D_MODEL), F32)],
        name="grad_w_in", compiler_params=_params(("arbitrary",), 48))(*dparts, a)


def _down_bwd(dy, w_down, g, u, to_send, tm=512):
    T = dy.shape[0]
    n_s = len(to_send)

    def body(dy_ref, w_ref, g_ref, u_ref, *rest):
        srcs, (dg_ref, du_ref), dsts, sems = rest[:n_s], rest[n_s:n_s + 2], rest[n_s + 2:2 * n_s + 2], rest[2 * n_s + 2:]
        i = pl.program_id(0)
        items = list(zip(srcs, dsts, [True] * n_s))

        @pl.when(i == 0)
        def _():
            _Scatter(items, sems).start()

        dyv = dy_ref[...]
        for c in range(D_FF // FF_CHUNK):
            cols = slice(c * FF_CHUNK, (c + 1) * FF_CHUNK)
            dact = _dot_nt(dyv, w_ref[cols, :]).astype(BF)
            gv, uv = g_ref[:, cols], u_ref[:, cols]
            s = _sigmoid(gv)
            ds = dact * s
            dg_ref[:, cols] = ds * uv * (1.0 + gv * (1.0 - s))
            du_ref[:, cols] = ds * gv

        @pl.when(i == pl.num_programs(0) - 1)
        def _():
            _Scatter(items, sems).wait()

    tile = pl.BlockSpec((tm, D_FF), _row)
    outs = pl.pallas_call(
        body, grid=(T // tm,),
        in_specs=[pl.BlockSpec((tm, D_MODEL), _row),
                  pl.BlockSpec((D_FF, D_MODEL), _fixed, pipeline_mode=pl.Buffered(1)), tile, tile] + [HBM] * n_s,
        out_specs=[tile, tile] + [HBM] * n_s,
        out_shape=[SDS((T, D_FF), BF)] * 2 + [SDS(s.shape, s.dtype) for s in to_send],
        scratch_shapes=_scatter_sems(n_s),
        name="down_bwd", compiler_params=_params(("arbitrary",), 48))(dy, w_down, g, u, *to_send)
    return outs[0], outs[1], outs[2:]


def _ffn_in_bwd(dg, du, w_gu_t, h1, dh2, mixed, g_pf, g_pm, to_send, tm=512):
    T = h1.shape[0]
    n_s = len(to_send)

    def body(dg_ref, du_ref, w_ref, h1_ref, dh2_ref, mx_ref, gpf_ref, gpm_ref, *rest):
        srcs, outs, dsts, sems = rest[:n_s], rest[n_s:n_s + 4], rest[n_s + 4:2 * n_s + 4], rest[2 * n_s + 4:]
        dh1_ref, dmx_ref, dgpf_ref, dgpm_ref = outs
        i = pl.program_id(0)
        items = list(zip(srcs, dsts, [True] * n_s))
        _acc_init(i, dgpf_ref, dgpm_ref)

        @pl.when(i == 0)
        def _():
            _Scatter(items, sems).start()

        df = _dot(jnp.concatenate([dg_ref[...], du_ref[...]], axis=1), w_ref[...])
        dx, dgf = _rms_bwd(df, h1_ref[...], gpf_ref[...])
        dh1 = dh2_ref[...] + dx
        dh1_ref[...] = dh1
        dmx, dgm = _rms_bwd(dh1, mx_ref[...], gpm_ref[...])
        dmx_ref[...] = dmx.astype(BF)
        dgpf_ref[...] += _colsum(dgf)
        dgpm_ref[...] += _colsum(dgm)

        @pl.when(i == pl.num_programs(0) - 1)
        def _():
            _Scatter(items, sems).wait()

    tok = lambda w: pl.BlockSpec((tm, w), _row)
    vec = pl.BlockSpec((1, D_MODEL), _fixed)
    outs = pl.pallas_call(
        body, grid=(T // tm,),
        in_specs=[tok(D_FF), tok(D_FF), pl.BlockSpec((2 * D_FF, D_MODEL), _fixed, pipeline_mode=pl.Buffered(1)),
                  tok(D_MODEL), tok(D_MODEL), tok(D_MODEL), vec, vec]
        + [HBM] * n_s,
        out_specs=[tok(D_MODEL), tok(D_MODEL), vec, vec] + [HBM] * n_s,
        out_shape=[SDS((T, D_MODEL), F32), SDS((T, D_MODEL), BF), SDS((1, D_MODEL), F32), SDS((1, D_MODEL), F32)]
        + [SDS(s.shape, s.dtype) for s in to_send],
        scratch_shapes=_scatter_sems(n_s),
        name="ffn_in_bwd", compiler_params=_params(("arbitrary",), 56))(
            dg, du, w_gu_t, h1, dh2, mixed, g_pf, g_pm, *to_send)
    return outs[0], outs[1], outs[2], outs[3], outs[4:]


STAT_LANES = HEAD_DIM // 2


def _ff_grad_spec(half):
    return pl.BlockSpec((D_FF, D_MODEL), lambda i: (half, 0), pipeline_mode=pl.Buffered(1))


def _out_bwd(dmx, w_out, attn, lse, sgu, g_a, g_s, dg, f, tm=512):
    T = attn.shape[0]

    def body(dm_ref, w_ref, a_ref, l_ref, s_ref, ga_ref, gs_ref, dgate_ref, f_ref,
             da_ref, st_ref, ds_ref, dga_ref, dgs_ref, ggu_ref):
        _acc_init(pl.program_id(0), dga_ref, dgs_ref, ggu_ref)
        ggu_ref[...] += _dot_tn(dgate_ref[...], f_ref[...])
        dgr = _dot_nt(dm_ref[...], w_ref[...])
        av = a_ref[...]
        da, dga = _rms_bwd(dgr[:, :ATTN_W], av, ga_ref[...])
        ds, dgs = _rms_bwd(dgr[:, ATTN_W:], s_ref[...], gs_ref[...])
        da_ref[...] = da
        ds_ref[...] = ds
        dga_ref[...] += _colsum(dga)
        dgs_ref[...] += _colsum(dgs)
        lane = lax.broadcasted_iota(jnp.int32, (1, LANES), 1)
        lo = lane < HEAD_DIM
        first = (lane % HEAD_DIM) < STAT_LANES
        prod = da * av
        for c in range(ATTN_W // LANES):
            cols = slice(c * LANES, (c + 1) * LANES)
            pc = prod[:, cols]
            delta = jnp.where(lo, jnp.sum(jnp.where(lo, pc, 0.0), axis=-1, keepdims=True),
                              jnp.sum(jnp.where(lo, 0.0, pc), axis=-1, keepdims=True))
            st_ref[:, cols] = jnp.where(first, l_ref[:, cols], delta)

    tok = lambda w: pl.BlockSpec((tm, w), _row)
    vec = lambda w: pl.BlockSpec((1, w), _fixed)
    return pl.pallas_call(
        body, grid=(T // tm,),
        in_specs=[tok(D_MODEL), pl.BlockSpec((D_MODEL, D_MODEL), _fixed), tok(ATTN_W), tok(ATTN_W), tok(SGU_W),
                  vec(ATTN_W), vec(SGU_W), tok(D_FF), tok(D_MODEL)],
        out_specs=[tok(ATTN_W), tok(ATTN_W), tok(SGU_W), vec(ATTN_W), vec(SGU_W), _ff_grad_spec(0)],
        out_shape=[SDS((T, ATTN_W), F32), SDS((T, ATTN_W), F32), SDS((T, SGU_W), F32), SDS((1, ATTN_W), F32),
                   SDS((1, SGU_W), F32), SDS((2 * D_FF, D_MODEL), F32)],
        name="out_bwd", compiler_params=_params(("arbitrary",), 56))(dmx, w_out, attn, lse, sgu, g_a, g_s, dg, f)


def _sgu_bwd(proj, dsgu, ln_g, ln_b, w_s, b_st, groups, dmx, d_up, f, g_gu_t, tm=512):
    T = proj.shape[0]

    def body(u_ref, z_ref, ds_ref, g_ref, b_ref, w_ref, bs_ref, grp_ref, dmx_ref, dup_ref, f_ref, _,
             duz_ref, dw_ref, dbs_ref, dlg_ref, dlb_ref, gout_ref, ggu_ref, dbacc_ref):
        du_ref, dz_ref = duz_ref.at[0], duz_ref.at[1]
        step = pl.program_id(0)
        _acc_init(step, dw_ref, dbs_ref, dlg_ref, dlb_ref, gout_ref, ggu_ref, dbacc_ref)
        gout_ref[...] += _dot_tn(grp_ref[...], dmx_ref[...])
        ggu_ref[...] += _dot_tn(dup_ref[...], f_ref[...])
        lng, lnb = g_ref[...], b_ref[...]
        for g in range(N_GROUPS):
            wm = _causal(w_ref[g]).astype(BF)
            cols = slice(g * GROUP_DIM, (g + 1) * GROUP_DIM)
            for c in range(tm // CHUNK):
                rows = slice(c * CHUNK, (c + 1) * CHUNK)
                zv, uv, dout = z_ref[rows, cols], u_ref[rows, cols], ds_ref[rows, cols]
                zn, xhat, rs, tz = _sgu_norm(zv, lng, lnb)
                znb = zn.astype(BF)
                mixed = _dot(wm, znb) + bs_ref[:, g:g + 1]
                gu, tu = _gelu(uv)
                du_ref[rows, cols] = (dout * mixed * _gelu_grad(uv, tu)).astype(BF)
                dmix = dout * gu
                dmb = dmix.astype(BF)
                dw_ref[g] += _causal(_dot_nt(dmb, znb))
                dbacc_ref[g] += dmix
                dzn = _dot_tn(wm, dmb)
                dlg_ref[...] += _colsum(dzn * xhat)
                dlb_ref[...] += _colsum(dzn)
                dxh = dzn * lng
                dgz = rs * (dxh - jnp.mean(dxh, axis=-1, keepdims=True)
                            - xhat * jnp.mean(dxh * xhat, axis=-1, keepdims=True))
                dz_ref[rows, cols] = (dgz * _gelu_grad(zv, tz)).astype(BF)

        @pl.when(step == pl.num_programs(0) - 1)
        def _():
            lane = lax.broadcasted_iota(jnp.int32, (CHUNK, LANES), 1)
            acc = jnp.zeros((CHUNK, LANES), F32)
            for g in range(N_GROUPS):
                acc = jnp.where(lane == g, jnp.sum(dbacc_ref[g], axis=-1, keepdims=True), acc)
            dbs_ref[...] = acc

    tok = pl.BlockSpec((tm, SGU_W), _row)
    vec = pl.BlockSpec((1, GROUP_DIM), _fixed)
    wsp = pl.BlockSpec((N_GROUPS, CHUNK, CHUNK), lambda i: (0, 0, 0))
    sq = pl.BlockSpec((CHUNK, LANES), _fixed)
    wide = pl.BlockSpec((tm, D_MODEL), _row)
    return pl.pallas_call(
        body, grid=(T // tm,),
        in_specs=[pl.BlockSpec((tm, SGU_W), lambda i: (i, 3)), pl.BlockSpec((tm, SGU_W), lambda i: (i, 4)), tok,
                  vec, vec, wsp, sq, wide, wide, pl.BlockSpec((tm, D_FF), _row), wide, HBM],
        out_specs=[pl.BlockSpec((2, tm, SGU_W), lambda i: (0, i, 0)), wsp, sq, vec, vec,
                   pl.BlockSpec((D_MODEL, D_MODEL), _fixed), _ff_grad_spec(1)],
        out_shape=[SDS((2, T, SGU_W), BF), SDS((N_GROUPS, CHUNK, CHUNK), F32),
                   SDS((CHUNK, LANES), F32), SDS((1, GROUP_DIM), F32), SDS((1, GROUP_DIM), F32),
                   SDS((D_MODEL, D_MODEL), F32), SDS((2 * D_FF, D_MODEL), F32)],
        input_output_aliases={11: 6},
        scratch_shapes=[pltpu.VMEM((N_GROUPS, CHUNK, LANES), F32)],
        name="sgu_bwd", compiler_params=_params(("arbitrary",), 56))(
            proj, proj, dsgu, ln_g, ln_b, w_s, b_st, groups, dmx, d_up, f, g_gu_t)


def _attn_bwd(proj, do, stats, slopes, to_send, slabbed):
    T = proj.shape[0]
    nblk = T // QBLK
    n_s = len(to_send)

    def body(q_ref, k_ref, v_ref, do_ref, st_ref, sl_ref, *rest):
        srcs, d_ref, dsts = rest[:n_s], rest[n_s], rest[n_s + 1:2 * n_s + 1]
        sems, bias_ref = rest[2 * n_s + 1:2 * n_s + 4], rest[2 * n_s + 4]
        dq_ref, dk_ref, dv_ref = d_ref.at[0], d_ref.at[1], d_ref.at[2]
        h = pl.program_id(0)
        items = list(zip(srcs, dsts, slabbed))

        @pl.when(h == 0)
        def _():
            _Scatter(items, sems).start()

        _attn_bias(sl_ref, bias_ref)
        lo = lax.broadcasted_iota(jnp.int32, (1, LANES), 1) < HEAD_DIM
        scale = HEAD_DIM ** -0.5
        d_ref[...] = jnp.zeros_like(d_ref)

        for di, d in enumerate(DILATIONS):
            group, segs = _attn_plan(nblk, d)

            def step(i, carry, segs=segs, **kw):
                for s in range(segs):
                    segment(i * segs + s, **kw)
                return carry

            def segment(i, d=d, di=di, group=group):
                start, pstart, first = _attn_group_index(i, nblk, d, group)
                rows, prows = _attn_rows(start, d, group), _attn_rows(pstart, d)
                alone = group == nblk // d
                q = q_ref[rows, :] * scale
                if alone:
                    k, v, own0 = k_ref[rows, :].astype(BF), v_ref[rows, :].astype(BF), 0
                else:
                    k = jnp.concatenate([k_ref[prows, :], k_ref[rows, :]], axis=0).astype(BF)
                    v = jnp.concatenate([v_ref[prows, :], v_ref[rows, :]], axis=0).astype(BF)
                    own0 = QBLK
                dov = do_ref[rows, :]
                stats = st_ref[rows, :]
                masks = [lo, ~lo]
                qm = [jnp.where(masks[j], q, 0.0).astype(BF) for j in range(2)]
                dom = [jnp.where(masks[j], dov, 0.0).astype(BF) for j in range(2)]
                for b in range(group):
                    qb = slice(b * QBLK, (b + 1) * QBLK)
                    own_only = alone and b == 0
                    kb = slice(own0 + (b if own_only else b - 1) * QBLK, own0 + (b + 1) * QBLK)
                    which = di * 2 + first.astype(jnp.int32) if b == 0 and not own_only else di * 2
                    dq_parts, prs, dss = [], [], []
                    for j in range(2):
                        bias = bias_ref[which, j * QBLK:(j + 1) * QBLK, QBLK if own_only else 0:]
                        lj = stats[qb, j * HEAD_DIM:j * HEAD_DIM + 1]
                        delta = stats[qb, j * HEAD_DIM + STAT_LANES:j * HEAD_DIM + STAT_LANES + 1]
                        pr = jnp.exp(_dot_nt(qm[j][qb], k[kb]) + bias - lj)
                        ds = (pr * (_dot_nt(dom[j][qb], v[kb]) - delta)).astype(BF)
                        dq_parts.append(_dot(ds, k[kb]))
                        prs.append(pr.astype(BF))
                        dss.append(ds)
                    dk_b = _dot_tn(jnp.concatenate(dss, axis=0), jnp.concatenate([qm[0][qb], qm[1][qb]], axis=0))
                    dv_b = _dot_tn(jnp.concatenate(prs, axis=0), jnp.concatenate([dom[0][qb], dom[1][qb]], axis=0))
                    own = _attn_rows(start + b * (d * QBLK), d)
                    dq_ref[own, :] += jnp.where(lo, dq_parts[0], dq_parts[1]) * scale
                    if own_only:
                        dk_ref[own, :] += dk_b
                        dv_ref[own, :] += dv_b
                    elif b == 0:
                        dk_ref[prows, :] += dk_b[:QBLK]
                        dv_ref[prows, :] += dv_b[:QBLK]
                        dk_ref[own, :] += dk_b[QBLK:]
                        dv_ref[own, :] += dv_b[QBLK:]
                    else:
                        two = _attn_rows(start + (b - 1) * (d * QBLK), d, 2)
                        dk_ref[two, :] += dk_b
                        dv_ref[two, :] += dv_b

            lax.fori_loop(0, nblk // (group * segs), step, 0)

        @pl.when(h == pl.num_programs(0) - 1)
        def _():
            _Scatter(items, sems).wait()

    col = lambda base: pl.BlockSpec((T, LANES), lambda h: (0, base + h))
    outs = pl.pallas_call(
        body, grid=(4,),
        in_specs=[col(0), col(4), col(8), col(0), col(0), pl.BlockSpec((1, 8, LANES), lambda h: (h, 0, 0))]
        + [HBM] * n_s,
        out_specs=[pl.BlockSpec((3, T, LANES), lambda h: (0, 0, h), pipeline_mode=pl.Buffered(1))] + [HBM] * n_s,
        out_shape=[SDS((3, T, ATTN_W), F32)]
        + [SDS(s.shape if sl else (N_DEV,) + s.shape, s.dtype) for s, sl in zip(to_send, slabbed)],
        scratch_shapes=_scatter_sems(n_s) + [pltpu.VMEM((6, 2 * QBLK, 2 * QBLK), F32)],
        name="attn_bwd", compiler_params=_params(("arbitrary",), 60))(proj, proj, proj, do, stats, slopes, *to_send)
    return outs[0], outs[1:]


def _in_bwd(dparts, w_in_t, x, dh1, g1, tm=512):
    T = x.shape[0]
    n = len(dparts)
    w = ATTN_W

    def body(*refs):
        d_refs, (w_ref, x_ref, dh1_ref, g_ref, dx_ref, dg_ref) = refs[:n], refs[n:]
        _acc_init(pl.program_id(0), dg_ref)
        d_proj = jnp.concatenate([r[part].astype(BF) for r in d_refs for part in range(r.shape[0])], axis=1)
        da = _dot(d_proj, w_ref[...])
        dx, dg = _rms_bwd(da, x_ref[...], g_ref[...])
        dx_ref[...] = dh1_ref[...] + dx
        dg_ref[...] += _colsum(dg)

    tok = lambda c: pl.BlockSpec((tm, c), _row)
    vec = pl.BlockSpec((1, D_MODEL), _fixed)
    return pl.pallas_call(
        body, grid=(T // tm,),
        in_specs=[pl.BlockSpec((d.shape[0], tm, w), lambda i: (0, i, 0)) for d in dparts]
        + [pl.BlockSpec((PROJ, D_MODEL), _fixed), tok(D_MODEL), tok(D_MODEL), vec],
        out_specs=[tok(D_MODEL), vec],
        out_shape=[SDS((T, D_MODEL), F32), SDS((1, D_MODEL), F32)],
        name="in_bwd", compiler_params=_params(("arbitrary",), 52))(*dparts, w_in_t, x, dh1, g1)


def _sum_parts(p_ref):
    g = p_ref[0].astype(F32)
    for s in range(1, N_DEV):
        g = g + p_ref[s].astype(F32)
    return g


def _adamw_math(g, w, m, v):
    nm = ADAM_B1 * m + (1.0 - ADAM_B1) * g
    nv = ADAM_B2 * v + (1.0 - ADAM_B2) * (g * g)
    m_hat = nm / (1.0 - ADAM_B1 ** ADAM_STEP)
    v_hat = nv / (1.0 - ADAM_B2 ** ADAM_STEP)
    return -ADAM_LR * (m_hat / (jnp.sqrt(v_hat) + ADAM_EPS) + ADAM_WD * w), nm, nv


def _row_tile(rows):
    for t in (256, 176, 128, 80):
        if rows % t == 0:
            return t
    raise ValueError(rows)


def _reduce_adamw(parts, w, m, v, name):
    rows, width = w.shape
    tr = _row_tile(rows)

    def body(p_ref, w_ref, m_ref, v_ref, g_ref, d_ref, nm_ref, nv_ref):
        g = _sum_parts(p_ref)
        g_ref[...] = g
        d_ref[...], nm_ref[...], nv_ref[...] = _adamw_math(g, w_ref[...], m_ref[...], v_ref[...])

    blk = pl.BlockSpec((tr, width), _row)
    return pl.pallas_call(
        body, grid=(rows // tr,),
        in_specs=[pl.BlockSpec((N_DEV, tr, width), lambda i: (0, i, 0)), blk, blk, blk],
        out_specs=[blk] * 4, out_shape=[SDS((rows, width), F32)] * 4,
        name="adamw_" + name, compiler_params=_params(("arbitrary",), 32))(parts, w, m, v)


def _reduce(parts, name):
    _, rows, width = parts.shape
    tr = _row_tile(rows)

    def body(p_ref, g_ref):
        g_ref[...] = _sum_parts(p_ref)

    return pl.pallas_call(
        body, grid=(rows // tr,),
        in_specs=[pl.BlockSpec((N_DEV, tr, width), lambda i: (0, i, 0))],
        out_specs=pl.BlockSpec((tr, width), _row), out_shape=SDS((rows, width), F32),
        name="sum_" + name, compiler_params=_params(("arbitrary",), 32))(parts)


def _adamw(g, w, m, v, name):
    rows, width = w.shape
    tr = _row_tile(rows)

    def body(g_ref, w_ref, m_ref, v_ref, d_ref, nm_ref, nv_ref):
        d_ref[...], nm_ref[...], nv_ref[...] = _adamw_math(g_ref[...], w_ref[...], m_ref[...], v_ref[...])

    blk = pl.BlockSpec((tr, width), _row)
    return pl.pallas_call(
        body, grid=(rows // tr,), in_specs=[blk] * 4, out_specs=[blk] * 3, out_shape=[SDS((rows, width), F32)] * 3,
        name="adamw_" + name, compiler_params=_params(("arbitrary",), 32))(g, w, m, v)


SMALL = ("w_spatial", "ln_pre_mix", "ln_post_mix", "ln_pre_ffn", "ln_post_ffn", "b_pe_gate",
         "attn_out_norm", "sgu_out_norm", "b_spatial", "sgu_ln_g", "sgu_ln_b")
SMALL_GROUPS = ((128, ("w_spatial", "b_spatial", "sgu_ln_g", "sgu_ln_b")),
                (512, ("attn_out_norm", "sgu_out_norm")),
                (1024, ("ln_post_mix", "ln_pre_ffn", "ln_post_ffn", "b_pe_gate")))
SMALL_LATE = "ln_pre_mix"
SMALL_SIZE = dict(w_spatial=N_GROUPS * CHUNK * CHUNK, b_spatial=N_GROUPS * CHUNK, sgu_ln_g=GROUP_DIM, sgu_ln_b=GROUP_DIM,
                  attn_out_norm=ATTN_W, sgu_out_norm=SGU_W, ln_pre_mix=D_MODEL, ln_post_mix=D_MODEL, ln_pre_ffn=D_MODEL,
                  ln_post_ffn=D_MODEL, b_pe_gate=D_MODEL)
SUBLANES = 8
ROW_SHARDED = ("w_out", "w_down", "w_pe_gate")
COL_SHARDED = ("w_in", "w_gate_up", "w_pe_proj")
WEIGHTS = ("ln_pre_mix", "w_in", "sgu_ln_g", "sgu_ln_b", "w_spatial", "b_spatial", "attn_out_norm", "sgu_out_norm",
           "w_out", "ln_post_mix", "ln_pre_ffn", "w_gate_up", "w_down", "ln_post_ffn", "w_pe_gate", "b_pe_gate",
           "w_pe_proj")


def _group_rows(width, names, extra=0):
    rows = sum(SMALL_SIZE[n] // width for n in names) + extra
    return -(-rows // SUBLANES) * SUBLANES


def _pack_small_grads(gs, loss_term):
    packed = []
    for width, names in SMALL_GROUPS:
        rows = [gs[n].reshape(-1, width) for n in names]
        extra = int(width == D_MODEL)
        if extra:
            rows.append(jnp.full((1, width), loss_term, F32))
        used = sum(r.shape[0] for r in rows)
        rows.append(jnp.zeros((_group_rows(width, names, extra) - used, width), F32))
        packed.append(jnp.concatenate(rows, axis=0))
    return packed


def _small_adamw(arrived, arrived_late, w, m, v):
    names = [n for _, ns in SMALL_GROUPS for n in ns] + [SMALL_LATE]
    n_groups = len(SMALL_GROUPS)

    def body(*refs):
        group_refs, late_ref = refs[:n_groups], refs[n_groups]
        state = refs[n_groups + 1:n_groups + 1 + 3 * len(names)]
        outs = refs[n_groups + 1 + 3 * len(names):]
        sums = [_sum_parts(r) for r in group_refs]

        def update(name, g):
            i = names.index(name)
            w_ref, m_ref, v_ref = state[3 * i:3 * i + 3]
            delta, nm, nv = _adamw_math(g, w_ref[...].reshape(g.shape), m_ref[...].reshape(g.shape),
                                        v_ref[...].reshape(g.shape))
            for o_ref, val in zip(outs[4 * i:4 * i + 4], (g, delta, nm, nv)):
                o_ref[...] = val.reshape(o_ref.shape)

        for (width, group), total in zip(SMALL_GROUPS, sums):
            row = 0
            for name in group:
                rows = SMALL_SIZE[name] // width
                update(name, total[row:row + rows, :])
                row += rows
            if width == D_MODEL:
                outs[-1][...] = total[row:row + 1, :LANES]
        update(SMALL_LATE, _sum_parts(late_ref)[:1, :])

    state = [t[n] for n in names for t in (w, m, v)]
    plain = jax.ShapeDtypeStruct
    out_shape = [plain(w[n].shape, F32) for n in names for _ in range(4)] + [plain((1, LANES), F32)]
    outs = pl.pallas_call(body, out_shape=out_shape, name="adamw_small",
                          compiler_params=pltpu.CompilerParams(vmem_limit_bytes=32 * MIB))(*arrived, arrived_late, *state)
    return {n: outs[4 * i:4 * i + 4] for i, n in enumerate(names)}, outs[-1]


def _slabs(full):
    return full.reshape(N_DEV, full.shape[0] // N_DEV, full.shape[1])


def kernel(x, p, ln_pre_mix, w_in, sgu_ln_g, sgu_ln_b, w_spatial, b_spatial, attn_out_norm, sgu_out_norm, w_out, ln_post_mix, ln_pre_ffn, w_gate_up, w_down, ln_post_ffn, w_pe_gate, b_pe_gate, w_pe_proj, loss_target, m_ln_pre_mix, m_w_in, m_sgu_ln_g, m_sgu_ln_b, m_w_spatial, m_b_spatial, m_attn_out_norm, m_sgu_out_norm, m_w_out, m_ln_post_mix, m_ln_pre_ffn, m_w_gate_up, m_w_down, m_ln_post_ffn, m_w_pe_gate, m_b_pe_gate, m_w_pe_proj, v_ln_pre_mix, v_w_in, v_sgu_ln_g, v_sgu_ln_b, v_w_spatial, v_b_spatial, v_attn_out_norm, v_sgu_out_norm, v_w_out, v_ln_post_mix, v_ln_pre_ffn, v_w_gate_up, v_w_down, v_ln_post_ffn, v_w_pe_gate, v_b_pe_gate, v_w_pe_proj):
    given = dict(locals())
    w = {n: given[n] for n in WEIGHTS}
    m = {n: given["m_" + n] for n in WEIGHTS}
    v = {n: given["v_" + n] for n in WEIGHTS}
    xs, ps, target = x[0], p[0, 0], loss_target[0]

    shard = {n: w[n][0].astype(BF) for n in ROW_SHARDED}
    shard.update({n: w[n][0].T.astype(BF) for n in COL_SHARDED})
    sm = {n: w[n][0] for n in SMALL}
    sm = {n: (a.reshape(1, -1) if a.ndim == 1 else a) for n, a in sm.items()}
    slopes = jnp.broadcast_to((2.0 ** -(jnp.arange(8, dtype=F32) + 1.0)).reshape(4, 2, 1), (4, 2, LANES))
    slopes = jnp.concatenate([slopes, jnp.zeros((4, 6, LANES), F32)], axis=1)
    b_st = jnp.pad(sm["b_spatial"].T, ((0, 0), (0, LANES - N_GROUPS)))

    def full(gathered):
        return gathered.reshape(-1, gathered.shape[-1])

    w_in_t = full(_all_gather(shard["w_in"], "gather_w_in"))
    proj, a = _in_proj(xs, sm["ln_pre_mix"], w_in_t)
    later = ("w_out", "w_gate_up", "w_down", "w_pe_gate", "w_pe_proj")
    attn, lse, gathered = _attn_fwd(proj, slopes, [shard[n] for n in later])
    w_out_f, w_gu_t, w_down_f, w_peg_f, w_pep_t = [full(g) for g in gathered]
    sgu, groups, mixed, h1, f = _out_proj(attn, proj, xs, sm["sgu_ln_g"], sm["sgu_ln_b"], sm["w_spatial"], b_st,
                                          sm["attn_out_norm"], sm["sgu_out_norm"], w_out_f,
                                          sm["ln_post_mix"], sm["ln_pre_ffn"])
    g, u, act = _gate_up(f, w_gu_t)
    dh2, dy, g_peg, g_pep_t, loss_cols, db_peg, d_pff = _pe_loss_and_bwd(
        act, w_down_f, h1, ps, target, w_peg_f, sm["b_pe_gate"], w_pep_t, sm["ln_post_ffn"])
    loss_term = 0.5 * jnp.sum(loss_cols) * (1.0 / D_MODEL)

    arrived = {}
    g_down = _weight_grad(act, dy, "w_down")
    dg, du, (arrived["w_pe_proj"], arrived["w_pe_gate"]) = _down_bwd(dy, w_down_f, g, u, [_slabs(g_pep_t), _slabs(g_peg)])
    dh1, dmx, d_pf, d_pm, (arrived["w_down"],) = _ffn_in_bwd(dg, du, w_gu_t, h1, dh2, mixed, sm["ln_pre_ffn"],
                                                            sm["ln_post_mix"], [_slabs(g_down)])
    dattn, stats, dsgu, d_ga, d_gs, g_gu_t = _out_bwd(dmx, w_out_f, attn, lse, sgu, sm["attn_out_norm"],
                                                      sm["sgu_out_norm"], dg, f)
    duz, d_ws, d_bst, d_lg, d_lb, g_out, g_gu_t = _sgu_bwd(proj, dsgu, sm["sgu_ln_g"], sm["sgu_ln_b"],
                                                           sm["w_spatial"], b_st, groups, dmx, du, f, g_gu_t)
    gs = dict(sgu_ln_g=d_lg, sgu_ln_b=d_lb, w_spatial=d_ws, b_spatial=d_bst[:, :N_GROUPS].T, attn_out_norm=d_ga,
              sgu_out_norm=d_gs, ln_post_mix=d_pm, ln_pre_ffn=d_pf, ln_post_ffn=d_pff, b_pe_gate=db_peg)
    small_grads = _pack_small_grads(gs, loss_term)
    dqkv, (arrived["w_gate_up"], arrived["w_out"], *arrived_small) = _attn_bwd(
        proj, dattn, stats, slopes, [_slabs(g_gu_t), _slabs(g_out), *small_grads],
        [True, True] + [False] * len(small_grads))
    send_sems, recv_sems, slabs, landing, token = _scatter_begin(_slabs(_grad_w_in([dqkv, duz], a)), "w_in_grad_send")
    grad_x, d_g1 = _in_bwd([dqkv, duz], w_in_t, xs, dh1, sm["ln_pre_mix"] + token[:1, :1])
    slabs, landing = _scatter_end(send_sems, recv_sems, slabs, landing, d_g1, "w_in_grad_arrive")
    me = 4 * lax.axis_index("x") + 2 * lax.axis_index("y") + lax.axis_index("c")
    own = lax.dynamic_slice_in_dim(slabs, me, 1, axis=0)
    arrived["w_in"] = lax.dynamic_update_slice_in_dim(landing, own, me, axis=0)
    (arrived_late,) = _scatter_call([jnp.pad(d_g1, ((0, SUBLANES - 1), (0, 0)))], [False], "ln_pre_mix_grad_exchange")

    res = {}
    for n in ROW_SHARDED:
        res[n] = _reduce_adamw(arrived[n], w[n][0], m[n][0], v[n][0], n)
    for n in ("w_in", "w_gate_up"):
        res[n] = [t.T for t in _reduce_adamw(arrived[n], w[n][0].T, m[n][0].T, v[n][0].T, n)]
    for n in ("w_pe_proj",):
        grad = _reduce(arrived[n], n).T
        res[n] = (grad, *_adamw(grad, w[n][0], m[n][0], v[n][0], n))
    small, loss_row = _small_adamw(arrived_small, arrived_late, w, m, v)

    out = []
    for k in range(4):
        out += [res[n][k][None] if n in res else small[n][k] for n in WEIGHTS]
    return (loss_row[0, 0], grad_x[None], *out)
```

```python
import math

import jax
import jax.numpy as jnp
from jax import lax
from jax.experimental import pallas as pl
from jax.experimental.pallas import tpu as pltpu

F32 = jnp.float32
BF = jnp.bfloat16


def SDS(shape, dtype):
    return pltpu.HBM(tuple(shape), dtype)

D_MODEL = 1024
ATTN_W = 512
SGU_W = 512
HEAD_DIM = 64
N_GROUPS = 4
GROUP_DIM = 128
CHUNK = 128
D_FF = 2816
PLE_DIM = 256
PROJ = 3 * ATTN_W + 2 * SGU_W
DILATIONS = (1, 4, 16)
QBLK = 128
EPS = 1e-6
NEG = -1e30
N_DEV = 8
LANES = 128

ADAM_LR = 0.001
ADAM_B1 = 0.9
ADAM_B2 = 0.999
ADAM_EPS = 1e-08
ADAM_WD = 0.01
ADAM_STEP = 10

MIB = 2 ** 20
MESH_ID = pl.DeviceIdType.MESH
HBM = pl.BlockSpec(memory_space=pl.ANY)


def _params(sem, vmem_mib):
    return pltpu.CompilerParams(dimension_semantics=sem, vmem_limit_bytes=vmem_mib * MIB)


def _dot(a, b):
    return jnp.dot(a, b, preferred_element_type=F32)


def _dot_nt(a, b):
    return lax.dot_general(a, b, (((1,), (1,)), ((), ())), preferred_element_type=F32)


def _dot_tn(a, b):
    return lax.dot_general(a, b, (((0,), (0,)), ((), ())), preferred_element_type=F32)


def _rstd(x):
    return lax.rsqrt(jnp.mean(x * x, axis=-1, keepdims=True) + EPS)


def _rms_bwd(dy, x, g):
    r = _rstd(x)
    n = x * r
    dn = dy * g
    dx = r * (dn - n * jnp.mean(dn * n, axis=-1, keepdims=True))
    return dx, dy * n


def _colsum(v):
    return jnp.sum(v, axis=0, keepdims=True)


_G0 = math.sqrt(2.0 / math.pi)
_G1 = 0.044715


def _gelu(x):
    t = jnp.tanh(_G0 * (x + _G1 * x * x * x))
    return 0.5 * x * (1.0 + t), t


def _gelu_grad(x, t):
    return 0.5 * (1.0 + t) + 0.5 * x * (1.0 - t * t) * (_G0 * (1.0 + 3.0 * _G1 * x * x))


def _sigmoid(x):
    return 0.5 * jnp.tanh(0.5 * x) + 0.5


def _row(i):
    return (i, 0)


def _fixed(i):
    return (0, 0)


def _acc_init(step, *refs):
    @pl.when(step == 0)
    def _():
        for r in refs:
            r[...] = jnp.zeros_like(r)


FLIPS = [(dx, dy, dc) for dx in (0, 1) for dy in (0, 1) for dc in (0, 1)][1:]
DMA_SEMS = pltpu.SemaphoreType.DMA


def _mesh_pos():
    return lax.axis_index("x"), lax.axis_index("y"), lax.axis_index("c")


def _remote(src, dst, sems, n, to):
    return pltpu.make_async_remote_copy(src_ref=src, dst_ref=dst, send_sem=sems[0].at[n], recv_sem=sems[1].at[n],
                                        device_id=to, device_id_type=MESH_ID)


class _Scatter:
    def __init__(self, items, sems):
        x, y, c = _mesh_pos()
        me = 4 * x + 2 * y + c
        self.local, self.sends, self.arrivals = [], [], []
        for i, (src, dst, slabbed) in enumerate(items):
            self.local.append(pltpu.make_async_copy(src.at[me] if slabbed else src, dst.at[me], sems[2].at[i]))
            for k, (dx, dy, dc) in enumerate(FLIPS):
                to = (1 - x if dx else x, 1 - y if dy else y, 1 - c if dc else c)
                peer = 4 * to[0] + 2 * to[1] + to[2]
                out = src.at[peer] if slabbed else src
                self.sends.append(_remote(out, dst.at[me], sems, 7 * i + k, to))
                self.arrivals.append(_remote(out, dst.at[peer], sems, 7 * i + k, to))

    def start(self):
        for cp in self.local + self.sends:
            cp.start()

    def wait(self):
        for cp in self.arrivals:
            cp.wait_recv()
        for cp in self.sends:
            cp.wait_send()
        for cp in self.local:
            cp.wait()


def _scatter_sems(n):
    return [DMA_SEMS((7 * n,)), DMA_SEMS((7 * n,)), DMA_SEMS((n,))]


class _Gather:
    def __init__(self, items, sems):
        x, y, c = _mesh_pos()
        me, sibling = (x, y, c), (x, y, 1 - c)
        chips = [(1 - x, y), (x, 1 - y), (1 - x, 1 - y)]
        self.first, self.passed, self.from_chips, self.rest, self.local = [], [], [], [], []
        for i, (src, dst) in enumerate(items):
            def slot(p, dst=dst):
                return dst.at[4 * p[0] + 2 * p[1] + p[2]]

            def copy(k, block, to, own=False, i=i, src=src, slot=slot):
                return _remote(src if own else slot(block), slot(block), sems, 7 * i + k, to)

            self.local.append(pltpu.make_async_copy(src, slot(me), sems[2].at[i]))
            self.first.append(copy(0, me, sibling, own=True))
            self.first += [copy(1 + j, me, (*chip, c), own=True) for j, chip in enumerate(chips)]
            self.passed += [copy(4 + j, (*chip, c), sibling) for j, chip in enumerate(chips)]
            self.from_chips += [copy(1 + j, (*chip, c), me) for j, chip in enumerate(chips)]
            self.rest.append(copy(0, sibling, me))
            self.rest += [copy(4 + j, (*chip, 1 - c), me) for j, chip in enumerate(chips)]

    def start(self):
        for cp in self.local + self.first:
            cp.start()

    def forward(self):
        for arrived, onward in zip(self.from_chips, self.passed):
            arrived.wait_recv()
            onward.start()

    def finish(self):
        for cp in self.rest:
            cp.wait_recv()
        for cp in self.first + self.passed:
            cp.wait_send()
        for cp in self.local:
            cp.wait()


def _all_gather(shard, name):
    def body(x_ref, out_ref, *sems):
        g = _Gather([(x_ref, out_ref)], sems)
        g.start()
        g.forward()
        g.finish()

    return pl.pallas_call(
        body, out_shape=SDS((N_DEV,) + shard.shape, shard.dtype), in_specs=[HBM], out_specs=HBM,
        scratch_shapes=_scatter_sems(1), name=name)(shard)


def _scatter_call(srcs, slabbed, name):
    n = len(srcs)

    def body(*refs):
        sc = _Scatter(list(zip(refs[:n], refs[n:2 * n], slabbed)), refs[2 * n:])
        sc.start()
        sc.wait()

    shapes = [SDS(s.shape if sl else (N_DEV,) + s.shape, s.dtype) for s, sl in zip(srcs, slabbed)]
    return pl.pallas_call(body, out_shape=shapes, in_specs=[HBM] * n, out_specs=[HBM] * n,
                          scratch_shapes=_scatter_sems(n), name=name)(*srcs)


SEM = pl.BlockSpec(memory_space=pltpu.SEMAPHORE)
N_PEERS = len(FLIPS)


def _slab_copies(src_ref, land_ref, send_sems, recv_sems):
    x, y, c = _mesh_pos()
    me = 4 * x + 2 * y + c
    copies = []
    for k, (dx, dy, dc) in enumerate(FLIPS):
        to = (1 - x if dx else x, 1 - y if dy else y, 1 - c if dc else c)
        peer = 4 * to[0] + 2 * to[1] + to[2]
        sems = (send_sems, recv_sems)
        copies.append((_remote(src_ref.at[peer], land_ref.at[me], sems, k, to),
                       _remote(src_ref.at[peer], land_ref.at[peer], sems, k, to)))
    return copies


def _scatter_begin(src, name):
    def body(src_ref, land_ref, send_sems, recv_sems, src_thru, land_thru, token):
        for send, _ in _slab_copies(src_ref, land_ref, send_sems, recv_sems):
            send.start()
        token[...] = jnp.zeros_like(token)

    landing = lax.empty(src.shape, src.dtype)
    return pl.pallas_call(
        body, name=name,
        out_shape=(pltpu.SemaphoreType.DMA((N_PEERS,)), pltpu.SemaphoreType.DMA((N_PEERS,)),
                   pltpu.HBM(src.shape, src.dtype), pltpu.HBM(src.shape, src.dtype),
                   jax.ShapeDtypeStruct((SUBLANES, LANES), F32)),
        in_specs=(HBM, HBM), out_specs=(SEM, SEM, HBM, HBM, pl.BlockSpec(memory_space=pltpu.VMEM)),
        input_output_aliases={0: 2, 1: 3},
        compiler_params=pltpu.CompilerParams(has_side_effects=pltpu.SideEffectType.DATAFLOW_SIDE_EFFECTING))(
            pltpu.with_memory_space_constraint(src, pltpu.HBM), pltpu.with_memory_space_constraint(landing, pltpu.HBM))


def _scatter_end(send_sems, recv_sems, src_thru, land_thru, after, name):
    def body(src_ref, land_ref, send_sems, recv_sems, after_ref, src_dead, land_out):
        for send, arrival in _slab_copies(src_ref, land_ref, send_sems, recv_sems):
            send.wait_send()
            arrival.wait_recv()

    return pl.pallas_call(
        body, name=name,
        out_shape=(pltpu.HBM(src_thru.shape, src_thru.dtype), pltpu.HBM(land_thru.shape, land_thru.dtype)),
        in_specs=(HBM, HBM, SEM, SEM, HBM), out_specs=(HBM, HBM), input_output_aliases={0: 0, 1: 1},
        compiler_params=pltpu.CompilerParams(has_side_effects=pltpu.SideEffectType.DATAFLOW_SIDE_EFFECTING))(
            src_thru, land_thru, send_sems, recv_sems, after)


def _in_proj(x, g1, w_in_t, tm=512):
    T = x.shape[0]

    def body(x_ref, g_ref, w_ref, proj_ref, a_ref):
        xv = x_ref[...]
        a = (xv * _rstd(xv) * g_ref[...]).astype(BF)
        a_ref[...] = a
        proj_ref[...] = _dot_nt(a, w_ref[...])

    return pl.pallas_call(
        body, grid=(T // tm,),
        in_specs=[pl.BlockSpec((tm, D_MODEL), _row), pl.BlockSpec((1, D_MODEL), _fixed),
                  pl.BlockSpec((PROJ, D_MODEL), _fixed)],
        out_specs=[pl.BlockSpec((tm, PROJ), _row), pl.BlockSpec((tm, D_MODEL), _row)],
        out_shape=[SDS((T, PROJ), F32), SDS((T, D_MODEL), BF)],
        name="in_proj", compiler_params=_params(("arbitrary",), 48))(x, g1, w_in_t)


ATTN_GROUP = 16


def _attn_bias(sl_ref, bias_ref):
    qi = lax.broadcasted_iota(jnp.int32, (QBLK, QBLK), 0)
    kj = lax.broadcasted_iota(jnp.int32, (QBLK, QBLK), 1)
    step = qi - kj
    for di, d in enumerate(DILATIONS):
        for j in range(2):
            sl = sl_ref[0, j:j + 1, :]
            cur = jnp.where(step >= 0, -sl * (step * d).astype(F32), NEG)
            prev = jnp.where(step <= 0, -sl * ((step + QBLK) * d).astype(F32), NEG)
            rows = slice(j * QBLK, (j + 1) * QBLK)
            bias_ref[di * 2, rows, :QBLK] = prev
            bias_ref[di * 2, rows, QBLK:] = cur
            bias_ref[di * 2 + 1, rows, :QBLK] = jnp.full((QBLK, QBLK), NEG, F32)
            bias_ref[di * 2 + 1, rows, QBLK:] = cur


def _stack_heads(x, lo):
    return jnp.concatenate([jnp.where(lo, x, 0.0), jnp.where(lo, 0.0, x)], axis=0).astype(BF)


def _unstack_heads(x, lo):
    return jnp.where(lo, x[:QBLK], x[QBLK:])


def _attn_rows(start, d, blocks=1):
    if d == 1:
        return pl.ds(pl.multiple_of(start, QBLK), blocks * QBLK)
    return pl.ds(start, blocks * QBLK, stride=d)


def _attn_group_index(i, nblk, d, group):
    per = nblk // d // group
    r = i // per
    n0 = (i % per) * group
    start = r + (d * QBLK) * n0
    pstart = jnp.maximum(start - d * QBLK, r)
    return start, pstart, n0 == 0


def _attn_plan(nblk, d):
    group = min(ATTN_GROUP, nblk // d)
    return group, max(1, min(ATTN_GROUP // group, d))


def _attn_fwd(proj, slopes, to_gather):
    T = proj.shape[0]
    nblk = T // QBLK
    n_g = len(to_gather)

    def body(q_ref, k_ref, v_ref, sl_ref, *rest):
        srcs, (o_ref, m_ref), dsts = rest[:n_g], rest[n_g:n_g + 2], rest[n_g + 2:2 * n_g + 2]
        sems, (l_ref, bias_ref) = rest[2 * n_g + 2:2 * n_g + 5], rest[2 * n_g + 5:]
        h = pl.program_id(0)

        @pl.when(h == 0)
        def _():
            _Gather(list(zip(srcs, dsts)), sems).start()

        @pl.when(h == pl.num_programs(0) - 1)
        def _():
            _Gather(list(zip(srcs, dsts)), sems).forward()

        _attn_bias(sl_ref, bias_ref)
        lo = lax.broadcasted_iota(jnp.int32, (1, LANES), 1) < HEAD_DIM

        order = list(enumerate(DILATIONS))[::-1]
        for di, d in order:
            group, segs = _attn_plan(nblk, d)
            fresh, last = di == order[0][0], di == order[-1][0]

            def step(i, carry, segs=segs, **kw):
                for s in range(segs):
                    segment(i * segs + s, **kw)
                return carry

            def segment(i, d=d, di=di, group=group, fresh=fresh, last=last):
                start, pstart, first = _attn_group_index(i, nblk, d, group)
                prows = _attn_rows(pstart, d)
                alone = group == nblk // d
                k_prev = v_prev = None
                if not alone:
                    k_prev, v_prev = k_ref[prows, :].astype(BF), v_ref[prows, :].astype(BF)
                for b in range(group):
                    out = _attn_rows(start + b * (d * QBLK), d)
                    k_own, v_own = k_ref[out, :].astype(BF), v_ref[out, :].astype(BF)
                    if alone and b == 0:
                        k2, v2, bias = k_own, v_own, bias_ref[di * 2, :, QBLK:]
                    else:
                        k2, v2 = jnp.concatenate([k_prev, k_own], axis=0), jnp.concatenate([v_prev, v_own], axis=0)
                        bias = bias_ref[di * 2 + first.astype(jnp.int32)] if b == 0 else bias_ref[di * 2]
                    k_prev, v_prev = k_own, v_own
                    s = _dot_nt(_stack_heads(q_ref[out, :] * (HEAD_DIM ** -0.5), lo), k2) + bias
                    m = jnp.max(s, axis=-1, keepdims=True)
                    pr = jnp.exp(s - m)
                    m_b = _unstack_heads(m, lo)
                    l_b = _unstack_heads(jnp.sum(pr, axis=-1, keepdims=True), lo)
                    o_b = _unstack_heads(_dot(pr.astype(BF), v2), lo)
                    if fresh:
                        m_ref[out, :] = m_b
                        l_ref[out, :] = l_b
                        o_ref[out, :] = o_b
                        continue
                    m_o = m_ref[out, :]
                    m_n = jnp.maximum(m_o, m_b)
                    wa, wb = jnp.exp(m_o - m_n), jnp.exp(m_b - m_n)
                    l_n = wa * l_ref[out, :] + wb * l_b
                    o_n = wa * o_ref[out, :] + wb * o_b
                    if last:
                        m_ref[out, :] = m_n + jnp.log(l_n)
                        o_ref[out, :] = o_n / l_n
                    else:
                        m_ref[out, :] = m_n
                        l_ref[out, :] = l_n
                        o_ref[out, :] = o_n

            lax.fori_loop(0, nblk // (group * segs), step, 0)

        @pl.when(h == pl.num_programs(0) - 1)
        def _():
            _Gather(list(zip(srcs, dsts)), sems).finish()

    col = lambda base: pl.BlockSpec((T, LANES), lambda h: (0, base + h))
    tok = pl.BlockSpec((T, LANES), lambda h: (0, h))
    outs = pl.pallas_call(
        body, grid=(4,),
        in_specs=[col(0), col(4), col(8), pl.BlockSpec((1, 8, LANES), lambda h: (h, 0, 0))] + [HBM] * n_g,
        out_specs=[tok, tok] + [HBM] * n_g,
        out_shape=[SDS((T, ATTN_W), F32), SDS((T, ATTN_W), F32)]
        + [SDS((N_DEV,) + g.shape, g.dtype) for g in to_gather],
        scratch_shapes=_scatter_sems(n_g) + [pltpu.VMEM((T, LANES), F32), pltpu.VMEM((6, 2 * QBLK, 2 * QBLK), F32)],
        name="attn_fwd", compiler_params=_params(("arbitrary",), 56))(proj, proj, proj, slopes, *to_gather)
    return outs[0], outs[1], outs[2:]


def _sgu_norm(zv, ln_g, ln_b):
    gz, tz = _gelu(zv)
    mu = jnp.mean(gz, axis=-1, keepdims=True)
    xc = gz - mu
    rs = lax.rsqrt(jnp.mean(xc * xc, axis=-1, keepdims=True) + EPS)
    xhat = xc * rs
    return xhat * ln_g + ln_b, xhat, rs, tz


def _causal(w):
    i = lax.broadcasted_iota(jnp.int32, (CHUNK, CHUNK), 0)
    j = lax.broadcasted_iota(jnp.int32, (CHUNK, CHUNK), 1)
    return jnp.where(i >= j, w, 0.0)


def _sgu_fwd_tile(u_ref, z_ref, g_ref, b_ref, w_ref, bs_ref, out_ref):
    for g in range(N_GROUPS):
        wm = _causal(w_ref[g]).astype(BF)
        cols = slice(g * GROUP_DIM, (g + 1) * GROUP_DIM)
        for c in range(u_ref.shape[0] // CHUNK):
            rows = slice(c * CHUNK, (c + 1) * CHUNK)
            zn, _, _, _ = _sgu_norm(z_ref[rows, cols], g_ref[...], b_ref[...])
            mixed = _dot(wm, zn.astype(BF)) + bs_ref[:, g:g + 1]
            gu, _ = _gelu(u_ref[rows, cols])
            out_ref[rows, cols] = gu * mixed


def _out_proj(attn, proj, x, ln_g, ln_b, w_s, b_st, g_a, g_s, w_out, g_pm, g_pf, tm=512):
    T = x.shape[0]

    def body(a_ref, u_ref, z_ref, x_ref, lg_ref, lb_ref, ws_ref, bs_ref, ga_ref, gs_ref, w_ref, gpm_ref, gpf_ref,
             s_ref, grp_ref, mixed_ref, h1_ref, f_ref):
        _sgu_fwd_tile(u_ref, z_ref, lg_ref, lb_ref, ws_ref, bs_ref, s_ref)
        av, sv = a_ref[...], s_ref[...]
        an = (av * _rstd(av) * ga_ref[...]).astype(BF)
        sn = (sv * _rstd(sv) * gs_ref[...]).astype(BF)
        grp_ref[:, :ATTN_W] = an
        grp_ref[:, ATTN_W:] = sn
        mixed = _dot(an, w_ref[:ATTN_W, :]) + _dot(sn, w_ref[ATTN_W:, :])
        mixed_ref[...] = mixed
        h1 = x_ref[...] + mixed * _rstd(mixed) * gpm_ref[...]
        h1_ref[...] = h1
        f_ref[...] = (h1 * _rstd(h1) * gpf_ref[...]).astype(BF)

    tok = lambda w: pl.BlockSpec((tm, w), _row)
    vec = lambda w: pl.BlockSpec((1, w), _fixed)
    return pl.pallas_call(
        body, grid=(T // tm,),
        in_specs=[tok(ATTN_W), pl.BlockSpec((tm, SGU_W), lambda i: (i, 3)), pl.BlockSpec((tm, SGU_W), lambda i: (i, 4)),
                  tok(D_MODEL), vec(GROUP_DIM), vec(GROUP_DIM), pl.BlockSpec((N_GROUPS, CHUNK, CHUNK), lambda i: (0, 0, 0)),
                  pl.BlockSpec((CHUNK, LANES), _fixed), vec(ATTN_W), vec(SGU_W),
                  pl.BlockSpec((D_MODEL, D_MODEL), _fixed), vec(D_MODEL), vec(D_MODEL)],
        out_specs=[tok(SGU_W)] + [tok(D_MODEL)] * 4,
        out_shape=[SDS((T, SGU_W), F32), SDS((T, D_MODEL), BF), SDS((T, D_MODEL), F32), SDS((T, D_MODEL), F32),
                   SDS((T, D_MODEL), BF)],
        name="out_proj", compiler_params=_params(("arbitrary",), 52))(
            attn, proj, proj, x, ln_g, ln_b, w_s, b_st, g_a, g_s, w_out, g_pm, g_pf)


FF_TILE = 1408
FF_TILES = D_FF // FF_TILE
FF_CHUNK = 256


def _gate_up(f, w_gu_t, tm=512):
    T = f.shape[0]
    tn = FF_TILE

    def body(f_ref, wg_ref, wu_ref, g_ref, u_ref, act_ref):
        fv = f_ref[...]
        g = _dot_nt(fv, wg_ref[...])
        u = _dot_nt(fv, wu_ref[...])
        g_ref[...] = g.astype(BF)
        u_ref[...] = u.astype(BF)
        act_ref[...] = (g * _sigmoid(g) * u).astype(BF)

    ospec = pl.BlockSpec((tm, tn), lambda j, i: (i, j))
    return pl.pallas_call(
        body, grid=(FF_TILES, T // tm),
        in_specs=[pl.BlockSpec((tm, D_MODEL), lambda j, i: (i, 0)), pl.BlockSpec((tn, D_MODEL), lambda j, i: (j, 0)),
                  pl.BlockSpec((tn, D_MODEL), lambda j, i: (j + FF_TILES, 0))],
        out_specs=[ospec] * 3, out_shape=[SDS((T, D_FF), BF)] * 3,
        name="gate_up", compiler_params=_params(("arbitrary", "arbitrary"), 40))(f, w_gu_t, w_gu_t)


def _down_proj(act, w_down, h1, g_pff, tm=512):
    T = act.shape[0]

    def body(a_ref, w_ref, h1_ref, g_ref, y_ref, h2_ref):
        y = _dot(a_ref[...], w_ref[...])
        y_ref[...] = y
        h2_ref[...] = h1_ref[...] + y * _rstd(y) * g_ref[...]

    return pl.pallas_call(
        body, grid=(T // tm,),
        in_specs=[pl.BlockSpec((tm, D_FF), _row), pl.BlockSpec((D_FF, D_MODEL), _fixed),
                  pl.BlockSpec((tm, D_MODEL), _row), pl.BlockSpec((1, D_MODEL), _fixed)],
        out_specs=[pl.BlockSpec((tm, D_MODEL), _row)] * 2,
        out_shape=[SDS((T, D_MODEL), F32)] * 2,
        name="down_proj", compiler_params=_params(("arbitrary",), 48))(act, w_down, h1, g_pff)


def _pe_loss_and_bwd(h2, p, target, y, w_peg, b_peg, w_pep_t, g_pff, tm=512):
    T = h2.shape[0]

    def body(h2_ref, p_ref, t_ref, y_ref, wg_ref, b_ref, wp_ref, g_ref,
             dh2_ref, dy_ref, gpeg_ref, gpep_ref, loss_ref, db_ref, dg_ref):
        _acc_init(pl.program_id(0), gpeg_ref, gpep_ref, loss_ref, db_ref, dg_ref)
        h2v = h2_ref[...]
        h2b = h2v.astype(BF)
        pb = p_ref[...].astype(BF)
        gate = _sigmoid(_dot(h2b, wg_ref[...]) + b_ref[...])
        pp = _dot_nt(pb, wp_ref[...])
        diff = h2v + gate * pp - t_ref[...]
        loss_ref[...] += _colsum(diff * diff)
        dh3 = diff * (1.0 / D_MODEL)
        dpre = dh3 * pp * (gate * (1.0 - gate))
        dpre_b = dpre.astype(BF)
        db_ref[...] += _colsum(dpre)
        gpeg_ref[...] += _dot_tn(h2b, dpre_b)
        gpep_ref[...] += _dot_tn((dh3 * gate).astype(BF), pb)
        dh2 = dh3 + _dot_nt(dpre_b, wg_ref[...])
        dh2_ref[...] = dh2
        dy, dg = _rms_bwd(dh2, y_ref[...], g_ref[...])
        dy_ref[...] = dy.astype(BF)
        dg_ref[...] += _colsum(dg)

    tok = lambda w: pl.BlockSpec((tm, w), _row)
    vec = pl.BlockSpec((1, D_MODEL), _fixed)
    wg = pl.BlockSpec((D_MODEL, D_MODEL), _fixed)
    wp = pl.BlockSpec((D_MODEL, PLE_DIM), _fixed)
    return pl.pallas_call(
        body, grid=(T // tm,),
        in_specs=[tok(D_MODEL), tok(PLE_DIM), tok(D_MODEL), tok(D_MODEL), wg, vec, wp, vec],
        out_specs=[tok(D_MODEL), tok(D_MODEL), wg, wp, vec, vec, vec],
        out_shape=[SDS((T, D_MODEL), F32), SDS((T, D_MODEL), BF), SDS((D_MODEL, D_MODEL), F32),
                   SDS((D_MODEL, PLE_DIM), F32)] + [SDS((1, D_MODEL), F32)] * 3,
        name="pe_loss_and_bwd", compiler_params=_params(("arbitrary",), 56))(
            h2, p, target, y, w_peg, b_peg, w_pep_t, g_pff)


def _weight_grad(a, dy, name, into=None, row_tile=0, rows=None, tk=1024):
    n = dy.shape[1]
    tn = min(n, 1024)
    T, ka = a.shape
    tka = FF_TILE if ka == D_FF else min(ka, 1024)
    rows = ka if rows is None else rows

    def body(a_ref, dy_ref, *rest):
        out_ref = rest[-1]
        _acc_init(pl.program_id(2), out_ref)
        out_ref[...] += _dot_tn(a_ref[...].astype(BF), dy_ref[...].astype(BF))

    carried = [] if into is None else [into]
    return pl.pallas_call(
        body, grid=(ka // tka, n // tn, T // tk),
        in_specs=[pl.BlockSpec((tk, tka), lambda i, j, k: (k, i)), pl.BlockSpec((tk, tn), lambda i, j, k: (k, j))]
        + [HBM] * len(carried),
        out_specs=pl.BlockSpec((tka, tn), lambda i, j, k: (i + row_tile, j)),
        out_shape=SDS((rows, n), F32), input_output_aliases={2: 0} if carried else {},
        name="grad_" + name, compiler_params=_params(("arbitrary",) * 3, 40))(a, dy, *carried)


def _grad_w_in(dparts, a, tk=1024):
    T = a.shape[0]

    def body(*refs):
        d_refs, a_ref, out_ref, acc_ref = refs[:len(dparts)], refs[-3], refs[-2], refs[-1]
        k = pl.program_id(0)
        _acc_init(k, acc_ref)
        cols = [r[part].astype(BF) for r in d_refs for part in range(r.shape[0])]
        acc_ref[...] += _dot_tn(jnp.concatenate(cols, axis=1), a_ref[...])

        @pl.when(k == pl.num_programs(0) - 1)
        def _():
            out_ref[...] = acc_ref[...].astype(BF)

    return pl.pallas_call(
        body, grid=(T // tk,),
        in_specs=[pl.BlockSpec((d.shape[0], tk, d.shape[2]), lambda k: (0, k, 0)) for d in dparts]
        + [pl.BlockSpec((tk, D_MODEL), lambda k: (k, 0))],
        out_specs=pl.BlockSpec((PROJ, D_MODEL), lambda k: (0, 0)),
        out_shape=SDS((PROJ, D_MODEL), BF), scratch_shapes=[pltpu.VMEM((PROJ, D_MODEL), F32)],
        name="grad_w_in", compiler_params=_params(("arbitrary",), 48))(*dparts, a)


def _down_bwd(dy, w_down, g, u, to_send, tm=512):
    T = dy.shape[0]
    n_s = len(to_send)

    def body(dy_ref, w_ref, g_ref, u_ref, *rest):
        srcs, (dg_ref, du_ref), dsts, sems = rest[:n_s], rest[n_s:n_s + 2], rest[n_s + 2:2 * n_s + 2], rest[2 * n_s + 2:]
        i = pl.program_id(0)
        items = list(zip(srcs, dsts, [True] * n_s))

        @pl.when(i == 0)
        def _():
            _Scatter(items, sems).start()

        dyv = dy_ref[...]
        for c in range(D_FF // FF_CHUNK):
            cols = slice(c * FF_CHUNK, (c + 1) * FF_CHUNK)
            dact = _dot_nt(dyv, w_ref[cols, :]).astype(BF)
            gv, uv = g_ref[:, cols], u_ref[:, cols]
            s = _sigmoid(gv)
            ds = dact * s
            dg_ref[:, cols] = ds * uv * (1.0 + gv * (1.0 - s))
            du_ref[:, cols] = ds * gv

        @pl.when(i == pl.num_programs(0) - 1)
        def _():
            _Scatter(items, sems).wait()

    tile = pl.BlockSpec((tm, D_FF), _row)
    outs = pl.pallas_call(
        body, grid=(T // tm,),
        in_specs=[pl.BlockSpec((tm, D_MODEL), _row),
                  pl.BlockSpec((D_FF, D_MODEL), _fixed, pipeline_mode=pl.Buffered(1)), tile, tile] + [HBM] * n_s,
        out_specs=[tile, tile] + [HBM] * n_s,
        out_shape=[SDS((T, D_FF), BF)] * 2 + [SDS(s.shape, s.dtype) for s in to_send],
        scratch_shapes=_scatter_sems(n_s),
        name="down_bwd", compiler_params=_params(("arbitrary",), 48))(dy, w_down, g, u, *to_send)
    return outs[0], outs[1], outs[2:]


def _ffn_in_bwd(dg, du, w_gu_t, h1, dh2, mixed, g_pf, g_pm, to_send, tm=512):
    T = h1.shape[0]
    n_s = len(to_send)

    def body(dg_ref, du_ref, w_ref, h1_ref, dh2_ref, mx_ref, gpf_ref, gpm_ref, *rest):
        srcs, outs, dsts, sems = rest[:n_s], rest[n_s:n_s + 4], rest[n_s + 4:2 * n_s + 4], rest[2 * n_s + 4:]
        dh1_ref, dmx_ref, dgpf_ref, dgpm_ref = outs
        i = pl.program_id(0)
        items = list(zip(srcs, dsts, [True] * n_s))
        _acc_init(i, dgpf_ref, dgpm_ref)

        @pl.when(i == 0)
        def _():
            _Scatter(items, sems).start()

        df = _dot(jnp.concatenate([dg_ref[...], du_ref[...]], axis=1), w_ref[...])
        dx, dgf = _rms_bwd(df, h1_ref[...], gpf_ref[...])
        dh1 = dh2_ref[...] + dx
        dh1_ref[...] = dh1
        dmx, dgm = _rms_bwd(dh1, mx_ref[...], gpm_ref[...])
        dmx_ref[...] = dmx.astype(BF)
        dgpf_ref[...] += _colsum(dgf)
        dgpm_ref[...] += _colsum(dgm)

        @pl.when(i == pl.num_programs(0) - 1)
        def _():
            _Scatter(items, sems).wait()

    tok = lambda w: pl.BlockSpec((tm, w), _row)
    vec = pl.BlockSpec((1, D_MODEL), _fixed)
    outs = pl.pallas_call(
        body, grid=(T // tm,),
        in_specs=[tok(D_FF), tok(D_FF), pl.BlockSpec((2 * D_FF, D_MODEL), _fixed, pipeline_mode=pl.Buffered(1)),
                  tok(D_MODEL), tok(D_MODEL), tok(D_MODEL), vec, vec]
        + [HBM] * n_s,
        out_specs=[tok(D_MODEL), tok(D_MODEL), vec, vec] + [HBM] * n_s,
        out_shape=[SDS((T, D_MODEL), F32), SDS((T, D_MODEL), BF), SDS((1, D_MODEL), F32), SDS((1, D_MODEL), F32)]
        + [SDS(s.shape, s.dtype) for s in to_send],
        scratch_shapes=_scatter_sems(n_s),
        name="ffn_in_bwd", compiler_params=_params(("arbitrary",), 56))(
            dg, du, w_gu_t, h1, dh2, mixed, g_pf, g_pm, *to_send)
    return outs[0], outs[1], outs[2], outs[3], outs[4:]


STAT_LANES = HEAD_DIM // 2


def _ff_grad_spec(half):
    return pl.BlockSpec((D_FF, D_MODEL), lambda i: (half, 0), pipeline_mode=pl.Buffered(1))


def _out_bwd(dmx, w_out, attn, lse, sgu, g_a, g_s, dg, f, tm=512):
    T = attn.shape[0]

    def body(dm_ref, w_ref, a_ref, l_ref, s_ref, ga_ref, gs_ref, dgate_ref, f_ref,
             da_ref, st_ref, ds_ref, dga_ref, dgs_ref, ggu_ref):
        _acc_init(pl.program_id(0), dga_ref, dgs_ref, ggu_ref)
        ggu_ref[...] += _dot_tn(dgate_ref[...], f_ref[...])
        dgr = _dot_nt(dm_ref[...], w_ref[...])
        av = a_ref[...]
        da, dga = _rms_bwd(dgr[:, :ATTN_W], av, ga_ref[...])
        ds, dgs = _rms_bwd(dgr[:, ATTN_W:], s_ref[...], gs_ref[...])
        da_ref[...] = da
        ds_ref[...] = ds
        dga_ref[...] += _colsum(dga)
        dgs_ref[...] += _colsum(dgs)
        lane = lax.broadcasted_iota(jnp.int32, (1, LANES), 1)
        lo = lane < HEAD_DIM
        first = (lane % HEAD_DIM) < STAT_LANES
        prod = da * av
        for c in range(ATTN_W // LANES):
            cols = slice(c * LANES, (c + 1) * LANES)
            pc = prod[:, cols]
            delta = jnp.where(lo, jnp.sum(jnp.where(lo, pc, 0.0), axis=-1, keepdims=True),
                              jnp.sum(jnp.where(lo, 0.0, pc), axis=-1, keepdims=True))
            st_ref[:, cols] = jnp.where(first, l_ref[:, cols], delta)

    tok = lambda w: pl.BlockSpec((tm, w), _row)
    vec = lambda w: pl.BlockSpec((1, w), _fixed)
    return pl.pallas_call(
        body, grid=(T // tm,),
        in_specs=[tok(D_MODEL), pl.BlockSpec((D_MODEL, D_MODEL), _fixed), tok(ATTN_W), tok(ATTN_W), tok(SGU_W),
                  vec(ATTN_W), vec(SGU_W), tok(D_FF), tok(D_MODEL)],
        out_specs=[tok(ATTN_W), tok(ATTN_W), tok(SGU_W), vec(ATTN_W), vec(SGU_W), _ff_grad_spec(0)],
        out_shape=[SDS((T, ATTN_W), F32), SDS((T, ATTN_W), F32), SDS((T, SGU_W), F32), SDS((1, ATTN_W), F32),
                   SDS((1, SGU_W), F32), SDS((2 * D_FF, D_MODEL), F32)],
        name="out_bwd", compiler_params=_params(("arbitrary",), 56))(dmx, w_out, attn, lse, sgu, g_a, g_s, dg, f)


def _sgu_bwd(proj, dsgu, ln_g, ln_b, w_s, b_st, groups, dmx, d_up, f, g_gu_t, tm=512):
    T = proj.shape[0]

    def body(u_ref, z_ref, ds_ref, g_ref, b_ref, w_ref, bs_ref, grp_ref, dmx_ref, dup_ref, f_ref, _,
             duz_ref, dw_ref, dbs_ref, dlg_ref, dlb_ref, gout_ref, ggu_ref, dbacc_ref):
        du_ref, dz_ref = duz_ref.at[0], duz_ref.at[1]
        step = pl.program_id(0)
        _acc_init(step, dw_ref, dbs_ref, dlg_ref, dlb_ref, gout_ref, ggu_ref, dbacc_ref)
        gout_ref[...] += _dot_tn(grp_ref[...], dmx_ref[...])
        ggu_ref[...] += _dot_tn(dup_ref[...], f_ref[...])
        lng, lnb = g_ref[...], b_ref[...]
        for g in range(N_GROUPS):
            wm = _causal(w_ref[g]).astype(BF)
            cols = slice(g * GROUP_DIM, (g + 1) * GROUP_DIM)
            for c in range(tm // CHUNK):
                rows = slice(c * CHUNK, (c + 1) * CHUNK)
                zv, uv, dout = z_ref[rows, cols], u_ref[rows, cols], ds_ref[rows, cols]
                zn, xhat, rs, tz = _sgu_norm(zv, lng, lnb)
                znb = zn.astype(BF)
                mixed = _dot(wm, znb) + bs_ref[:, g:g + 1]
                gu, tu = _gelu(uv)
                du_ref[rows, cols] = (dout * mixed * _gelu_grad(uv, tu)).astype(BF)
                dmix = dout * gu
                dmb = dmix.astype(BF)
                dw_ref[g] += _causal(_dot_nt(dmb, znb))
                dbacc_ref[g] += dmix
                dzn = _dot_tn(wm, dmb)
                dlg_ref[...] += _colsum(dzn * xhat)
                dlb_ref[...] += _colsum(dzn)
                dxh = dzn * lng
                dgz = rs * (dxh - jnp.mean(dxh, axis=-1, keepdims=True)
                            - xhat * jnp.mean(dxh * xhat, axis=-1, keepdims=True))
                dz_ref[rows, cols] = (dgz * _gelu_grad(zv, tz)).astype(BF)

        @pl.when(step == pl.num_programs(0) - 1)
        def _():
            lane = lax.broadcasted_iota(jnp.int32, (CHUNK, LANES), 1)
            acc = jnp.zeros((CHUNK, LANES), F32)
            for g in range(N_GROUPS):
                acc = jnp.where(lane == g, jnp.sum(dbacc_ref[g], axis=-1, keepdims=True), acc)
            dbs_ref[...] = acc

    tok = pl.BlockSpec((tm, SGU_W), _row)
    vec = pl.BlockSpec((1, GROUP_DIM), _fixed)
    wsp = pl.BlockSpec((N_GROUPS, CHUNK, CHUNK), lambda i: (0, 0, 0))
    sq = pl.BlockSpec((CHUNK, LANES), _fixed)
    wide = pl.BlockSpec((tm, D_MODEL), _row)
    return pl.pallas_call(
        body, grid=(T // tm,),
        in_specs=[pl.BlockSpec((tm, SGU_W), lambda i: (i, 3)), pl.BlockSpec((tm, SGU_W), lambda i: (i, 4)), tok,
                  vec, vec, wsp, sq, wide, wide, pl.BlockSpec((tm, D_FF), _row), wide, HBM],
        out_specs=[pl.BlockSpec((2, tm, SGU_W), lambda i: (0, i, 0)), wsp, sq, vec, vec,
                   pl.BlockSpec((D_MODEL, D_MODEL), _fixed), _ff_grad_spec(1)],
        out_shape=[SDS((2, T, SGU_W), BF), SDS((N_GROUPS, CHUNK, CHUNK), F32),
                   SDS((CHUNK, LANES), F32), SDS((1, GROUP_DIM), F32), SDS((1, GROUP_DIM), F32),
                   SDS((D_MODEL, D_MODEL), F32), SDS((2 * D_FF, D_MODEL), F32)],
        input_output_aliases={11: 6},
        scratch_shapes=[pltpu.VMEM((N_GROUPS, CHUNK, LANES), F32)],
        name="sgu_bwd", compiler_params=_params(("arbitrary",), 56))(
            proj, proj, dsgu, ln_g, ln_b, w_s, b_st, groups, dmx, d_up, f, g_gu_t)


def _attn_bwd(proj, do, stats, slopes, to_send, slabbed):
    T = proj.shape[0]
    nblk = T // QBLK
    n_s = len(to_send)

    def body(q_ref, k_ref, v_ref, do_ref, st_ref, sl_ref, *rest):
        srcs, d_ref, dsts = rest[:n_s], rest[n_s], rest[n_s + 1:2 * n_s + 1]
        sems, bias_ref = rest[2 * n_s + 1:2 * n_s + 4], rest[2 * n_s + 4]
        dq_ref, dk_ref, dv_ref = d_ref.at[0], d_ref.at[1], d_ref.at[2]
        h = pl.program_id(0)
        items = list(zip(srcs, dsts, slabbed))

        @pl.when(h == 0)
        def _():
            _Scatter(items, sems).start()

        _attn_bias(sl_ref, bias_ref)
        lo = lax.broadcasted_iota(jnp.int32, (1, LANES), 1) < HEAD_DIM
        scale = HEAD_DIM ** -0.5
        d_ref[...] = jnp.zeros_like(d_ref)

        for di, d in enumerate(DILATIONS):
            group, segs = _attn_plan(nblk, d)

            def step(i, carry, segs=segs, **kw):
                for s in range(segs):
                    segment(i * segs + s, **kw)
                return carry

            def segment(i, d=d, di=di, group=group):
                start, pstart, first = _attn_group_index(i, nblk, d, group)
                rows, prows = _attn_rows(start, d, group), _attn_rows(pstart, d)
                alone = group == nblk // d
                q = q_ref[rows, :] * scale
                if alone:
                    k, v, own0 = k_ref[rows, :].astype(BF), v_ref[rows, :].astype(BF), 0
                else:
                    k = jnp.concatenate([k_ref[prows, :], k_ref[rows, :]], axis=0).astype(BF)
                    v = jnp.concatenate([v_ref[prows, :], v_ref[rows, :]], axis=0).astype(BF)
                    own0 = QBLK
                dov = do_ref[rows, :]
                stats = st_ref[rows, :]
                masks = [lo, ~lo]
                qm = [jnp.where(masks[j], q, 0.0).astype(BF) for j in range(2)]
                dom = [jnp.where(masks[j], dov, 0.0).astype(BF) for j in range(2)]
                for b in range(group):
                    qb = slice(b * QBLK, (b + 1) * QBLK)
                    own_only = alone and b == 0
                    kb = slice(own0 + (b if own_only else b - 1) * QBLK, own0 + (b + 1) * QBLK)
                    which = di * 2 + first.astype(jnp.int32) if b == 0 and not own_only else di * 2
                    dq_parts, prs, dss = [], [], []
                    for j in range(2):
                        bias = bias_ref[which, j * QBLK:(j + 1) * QBLK, QBLK if own_only else 0:]
                        lj = stats[qb, j * HEAD_DIM:j * HEAD_DIM + 1]
                        delta = stats[qb, j * HEAD_DIM + STAT_LANES:j * HEAD_DIM + STAT_LANES + 1]
                        pr = jnp.exp(_dot_nt(qm[j][qb], k[kb]) + bias - lj)
                        ds = (pr * (_dot_nt(dom[j][qb], v[kb]) - delta)).astype(BF)
                        dq_parts.append(_dot(ds, k[kb]))
                        prs.append(pr.astype(BF))
                        dss.append(ds)
                    dk_b = _dot_tn(jnp.concatenate(dss, axis=0), jnp.concatenate([qm[0][qb], qm[1][qb]], axis=0))
                    dv_b = _dot_tn(jnp.concatenate(prs, axis=0), jnp.concatenate([dom[0][qb], dom[1][qb]], axis=0))
                    own = _attn_rows(start + b * (d * QBLK), d)
                    dq_ref[own, :] += jnp.where(lo, dq_parts[0], dq_parts[1]) * scale
                    if own_only:
                        dk_ref[own, :] += dk_b
                        dv_ref[own, :] += dv_b
                    elif b == 0:
                        dk_ref[prows, :] += dk_b[:QBLK]
                        dv_ref[prows, :] += dv_b[:QBLK]
                        dk_ref[own, :] += dk_b[QBLK:]
                        dv_ref[own, :] += dv_b[QBLK:]
                    else:
                        two = _attn_rows(start + (b - 1) * (d * QBLK), d, 2)
                        dk_ref[two, :] += dk_b
                        dv_ref[two, :] += dv_b

            lax.fori_loop(0, nblk // (group * segs), step, 0)

        @pl.when(h == pl.num_programs(0) - 1)
        def _():
            _Scatter(items, sems).wait()

    col = lambda base: pl.BlockSpec((T, LANES), lambda h: (0, base + h))
    outs = pl.pallas_call(
        body, grid=(4,),
        in_specs=[col(0), col(4), col(8), col(0), col(0), pl.BlockSpec((1, 8, LANES), lambda h: (h, 0, 0))]
        + [HBM] * n_s,
        out_specs=[pl.BlockSpec((3, T, LANES), lambda h: (0, 0, h), pipeline_mode=pl.Buffered(1))] + [HBM] * n_s,
        out_shape=[SDS((3, T, ATTN_W), F32)]
        + [SDS(s.shape if sl else (N_DEV,) + s.shape, s.dtype) for s, sl in zip(to_send, slabbed)],
        scratch_shapes=_scatter_sems(n_s) + [pltpu.VMEM((6, 2 * QBLK, 2 * QBLK), F32)],
        name="attn_bwd", compiler_params=_params(("arbitrary",), 60))(proj, proj, proj, do, stats, slopes, *to_send)
    return outs[0], outs[1:]


def _in_bwd(dparts, w_in_t, x, dh1, g1, tm=512):
    T = x.shape[0]
    n = len(dparts)
    w = ATTN_W

    def body(*refs):
        d_refs, (w_ref, x_ref, dh1_ref, g_ref, dx_ref, dg_ref) = refs[:n], refs[n:]
        _acc_init(pl.program_id(0), dg_ref)
        d_proj = jnp.concatenate([r[part].astype(BF) for r in d_refs for part in range(r.shape[0])], axis=1)
        da = _dot(d_proj, w_ref[...])
        dx, dg = _rms_bwd(da, x_ref[...], g_ref[...])
        dx_ref[...] = dh1_ref[...] + dx
        dg_ref[...] += _colsum(dg)

    tok = lambda c: pl.BlockSpec((tm, c), _row)
    vec = pl.BlockSpec((1, D_MODEL), _fixed)
    return pl.pallas_call(
        body, grid=(T // tm,),
        in_specs=[pl.BlockSpec((d.shape[0], tm, w), lambda i: (0, i, 0)) for d in dparts]
        + [pl.BlockSpec((PROJ, D_MODEL), _fixed), tok(D_MODEL), tok(D_MODEL), vec],
        out_specs=[tok(D_MODEL), vec],
        out_shape=[SDS((T, D_MODEL), F32), SDS((1, D_MODEL), F32)],
        name="in_bwd", compiler_params=_params(("arbitrary",), 52))(*dparts, w_in_t, x, dh1, g1)


def _in_hbm(*arrays):
    return [pltpu.with_memory_space_constraint(a, pltpu.HBM) for a in arrays]


def _sum_parts(p_ref):
    g = p_ref[0].astype(F32)
    for s in range(1, N_DEV):
        g = g + p_ref[s].astype(F32)
    return g


def _adamw_math(g, w, m, v):
    nm = ADAM_B1 * m + (1.0 - ADAM_B1) * g
    nv = ADAM_B2 * v + (1.0 - ADAM_B2) * (g * g)
    m_hat = nm / (1.0 - ADAM_B1 ** ADAM_STEP)
    v_hat = nv / (1.0 - ADAM_B2 ** ADAM_STEP)
    return -ADAM_LR * (m_hat / (jnp.sqrt(v_hat) + ADAM_EPS) + ADAM_WD * w), nm, nv


def _row_tile(rows):
    for t in (256, 176, 128, 80):
        if rows % t == 0:
            return t
    raise ValueError(rows)


def _reduce_adamw(parts, w, m, v, name):
    rows, width = w.shape
    tr = _row_tile(rows)

    def body(p_ref, w_ref, m_ref, v_ref, g_ref, d_ref, nm_ref, nv_ref):
        g = _sum_parts(p_ref)
        g_ref[...] = g
        d_ref[...], nm_ref[...], nv_ref[...] = _adamw_math(g, w_ref[...], m_ref[...], v_ref[...])

    blk = pl.BlockSpec((tr, width), _row)
    return pl.pallas_call(
        body, grid=(rows // tr,),
        in_specs=[pl.BlockSpec((N_DEV, tr, width), lambda i: (0, i, 0)), blk, blk, blk],
        out_specs=[blk] * 4, out_shape=[SDS((rows, width), F32)] * 4,
        name="adamw_" + name, compiler_params=_params(("arbitrary",), 32))(parts, *_in_hbm(w, m, v))


def _reduce(parts, name):
    _, rows, width = parts.shape
    tr = _row_tile(rows)

    def body(p_ref, g_ref):
        g_ref[...] = _sum_parts(p_ref)

    return pl.pallas_call(
        body, grid=(rows // tr,),
        in_specs=[pl.BlockSpec((N_DEV, tr, width), lambda i: (0, i, 0))],
        out_specs=pl.BlockSpec((tr, width), _row), out_shape=SDS((rows, width), F32),
        name="sum_" + name, compiler_params=_params(("arbitrary",), 32))(parts)


def _adamw(g, w, m, v, name):
    rows, width = w.shape
    tr = _row_tile(rows)

    def body(g_ref, w_ref, m_ref, v_ref, d_ref, nm_ref, nv_ref):
        d_ref[...], nm_ref[...], nv_ref[...] = _adamw_math(g_ref[...], w_ref[...], m_ref[...], v_ref[...])

    blk = pl.BlockSpec((tr, width), _row)
    return pl.pallas_call(
        body, grid=(rows // tr,), in_specs=[blk] * 4, out_specs=[blk] * 3, out_shape=[SDS((rows, width), F32)] * 3,
        name="adamw_" + name, compiler_params=_params(("arbitrary",), 32))(*_in_hbm(g, w, m, v))


SMALL = ("w_spatial", "ln_pre_mix", "ln_post_mix", "ln_pre_ffn", "ln_post_ffn", "b_pe_gate",
         "attn_out_norm", "sgu_out_norm", "b_spatial", "sgu_ln_g", "sgu_ln_b")
SMALL_GROUPS = ((128, ("w_spatial", "b_spatial", "sgu_ln_g", "sgu_ln_b")),
                (512, ("attn_out_norm", "sgu_out_norm")),
                (1024, ("ln_post_mix", "ln_pre_ffn", "ln_post_ffn", "b_pe_gate")))
SMALL_LATE = "ln_pre_mix"
SMALL_SIZE = dict(w_spatial=N_GROUPS * CHUNK * CHUNK, b_spatial=N_GROUPS * CHUNK, sgu_ln_g=GROUP_DIM, sgu_ln_b=GROUP_DIM,
                  attn_out_norm=ATTN_W, sgu_out_norm=SGU_W, ln_pre_mix=D_MODEL, ln_post_mix=D_MODEL, ln_pre_ffn=D_MODEL,
                  ln_post_ffn=D_MODEL, b_pe_gate=D_MODEL)
SUBLANES = 8
ROW_SHARDED = ("w_out", "w_down", "w_pe_gate")
COL_SHARDED = ("w_in", "w_gate_up", "w_pe_proj")
WEIGHTS = ("ln_pre_mix", "w_in", "sgu_ln_g", "sgu_ln_b", "w_spatial", "b_spatial", "attn_out_norm", "sgu_out_norm",
           "w_out", "ln_post_mix", "ln_pre_ffn", "w_gate_up", "w_down", "ln_post_ffn", "w_pe_gate", "b_pe_gate",
           "w_pe_proj")


def _group_rows(width, names, extra=0):
    rows = sum(SMALL_SIZE[n] // width for n in names) + extra
    return -(-rows // SUBLANES) * SUBLANES


def _pack_small_grads(gs, loss_term):
    packed = []
    for width, names in SMALL_GROUPS:
        rows = [gs[n].reshape(-1, width) for n in names]
        extra = int(width == D_MODEL)
        if extra:
            rows.append(jnp.full((1, width), loss_term, F32))
        used = sum(r.shape[0] for r in rows)
        rows.append(jnp.zeros((_group_rows(width, names, extra) - used, width), F32))
        packed.append(jnp.concatenate(rows, axis=0))
    return packed


def _small_adamw(arrived, arrived_late, w, m, v):
    names = [n for _, ns in SMALL_GROUPS for n in ns] + [SMALL_LATE]
    n_groups = len(SMALL_GROUPS)

    def body(*refs):
        group_refs, late_ref = refs[:n_groups], refs[n_groups]
        state = refs[n_groups + 1:n_groups + 1 + 3 * len(names)]
        outs = refs[n_groups + 1 + 3 * len(names):]
        sums = [_sum_parts(r) for r in group_refs]

        def update(name, g):
            i = names.index(name)
            w_ref, m_ref, v_ref = state[3 * i:3 * i + 3]
            delta, nm, nv = _adamw_math(g, w_ref[...].reshape(g.shape), m_ref[...].reshape(g.shape),
                                        v_ref[...].reshape(g.shape))
            for o_ref, val in zip(outs[4 * i:4 * i + 4], (g, delta, nm, nv)):
                o_ref[...] = val.reshape(o_ref.shape)

        for (width, group), total in zip(SMALL_GROUPS, sums):
            row = 0
            for name in group:
                rows = SMALL_SIZE[name] // width
                update(name, total[row:row + rows, :])
                row += rows
            if width == D_MODEL:
                outs[-1][...] = total[row:row + 1, :LANES]
        update(SMALL_LATE, _sum_parts(late_ref)[:1, :])

    state = [t[n] for n in names for t in (w, m, v)]
    plain = jax.ShapeDtypeStruct
    out_shape = [plain(w[n].shape, F32) for n in names for _ in range(4)] + [plain((1, LANES), F32)]
    outs = pl.pallas_call(body, out_shape=out_shape, name="adamw_small",
                          compiler_params=pltpu.CompilerParams(vmem_limit_bytes=32 * MIB))(
                              *arrived, arrived_late, *_in_hbm(*state))
    return {n: outs[4 * i:4 * i + 4] for i, n in enumerate(names)}, outs[-1]


def _slabs(full):
    return full.reshape(N_DEV, full.shape[0] // N_DEV, full.shape[1])


def kernel(x, p, ln_pre_mix, w_in, sgu_ln_g, sgu_ln_b, w_spatial, b_spatial, attn_out_norm, sgu_out_norm, w_out, ln_post_mix, ln_pre_ffn, w_gate_up, w_down, ln_post_ffn, w_pe_gate, b_pe_gate, w_pe_proj, loss_target, m_ln_pre_mix, m_w_in, m_sgu_ln_g, m_sgu_ln_b, m_w_spatial, m_b_spatial, m_attn_out_norm, m_sgu_out_norm, m_w_out, m_ln_post_mix, m_ln_pre_ffn, m_w_gate_up, m_w_down, m_ln_post_ffn, m_w_pe_gate, m_b_pe_gate, m_w_pe_proj, v_ln_pre_mix, v_w_in, v_sgu_ln_g, v_sgu_ln_b, v_w_spatial, v_b_spatial, v_attn_out_norm, v_sgu_out_norm, v_w_out, v_ln_post_mix, v_ln_pre_ffn, v_w_gate_up, v_w_down, v_ln_post_ffn, v_w_pe_gate, v_b_pe_gate, v_w_pe_proj):
    given = dict(locals())
    w = {n: given[n] for n in WEIGHTS}
    m = {n: given["m_" + n] for n in WEIGHTS}
    v = {n: given["v_" + n] for n in WEIGHTS}
    xs, ps, target = x[0], p[0, 0], loss_target[0]

    shard = {n: w[n][0].astype(BF) for n in ROW_SHARDED}
    shard.update({n: w[n][0].T.astype(BF) for n in COL_SHARDED})
    sm = {n: w[n][0] for n in SMALL}
    sm = {n: (a.reshape(1, -1) if a.ndim == 1 else a) for n, a in sm.items()}
    slopes = jnp.broadcast_to((2.0 ** -(jnp.arange(8, dtype=F32) + 1.0)).reshape(4, 2, 1), (4, 2, LANES))
    slopes = jnp.concatenate([slopes, jnp.zeros((4, 6, LANES), F32)], axis=1)
    b_st = jnp.pad(sm["b_spatial"].T, ((0, 0), (0, LANES - N_GROUPS)))

    def full(gathered):
        return gathered.reshape(-1, gathered.shape[-1])

    w_in_t = full(_all_gather(shard["w_in"], "gather_w_in"))
    proj, a = _in_proj(xs, sm["ln_pre_mix"], w_in_t)
    later = ("w_out", "w_gate_up", "w_down", "w_pe_gate", "w_pe_proj")
    attn, lse, gathered = _attn_fwd(proj, slopes, [shard[n] for n in later])
    w_out_f, w_gu_t, w_down_f, w_peg_f, w_pep_t = [full(g) for g in gathered]
    sgu, groups, mixed, h1, f = _out_proj(attn, proj, xs, sm["sgu_ln_g"], sm["sgu_ln_b"], sm["w_spatial"], b_st,
                                          sm["attn_out_norm"], sm["sgu_out_norm"], w_out_f,
                                          sm["ln_post_mix"], sm["ln_pre_ffn"])
    g, u, act = _gate_up(f, w_gu_t)
    y, h2 = _down_proj(act, w_down_f, h1, sm["ln_post_ffn"])
    dh2, dy, g_peg, g_pep_t, loss_cols, db_peg, d_pff = _pe_loss_and_bwd(
        h2, ps, target, y, w_peg_f, sm["b_pe_gate"], w_pep_t, sm["ln_post_ffn"])
    loss_term = 0.5 * jnp.sum(loss_cols) * (1.0 / D_MODEL)

    arrived = {}
    g_down = _weight_grad(act, dy, "w_down")
    dg, du, (arrived["w_pe_proj"], arrived["w_pe_gate"]) = _down_bwd(dy, w_down_f, g, u, [_slabs(g_pep_t), _slabs(g_peg)])
    dh1, dmx, d_pf, d_pm, (arrived["w_down"],) = _ffn_in_bwd(dg, du, w_gu_t, h1, dh2, mixed, sm["ln_pre_ffn"],
                                                            sm["ln_post_mix"], [_slabs(g_down)])
    dattn, stats, dsgu, d_ga, d_gs, g_gu_t = _out_bwd(dmx, w_out_f, attn, lse, sgu, sm["attn_out_norm"],
                                                      sm["sgu_out_norm"], dg, f)
    duz, d_ws, d_bst, d_lg, d_lb, g_out, g_gu_t = _sgu_bwd(proj, dsgu, sm["sgu_ln_g"], sm["sgu_ln_b"],
                                                           sm["w_spatial"], b_st, groups, dmx, du, f, g_gu_t)
    gs = dict(sgu_ln_g=d_lg, sgu_ln_b=d_lb, w_spatial=d_ws, b_spatial=d_bst[:, :N_GROUPS].T, attn_out_norm=d_ga,
              sgu_out_norm=d_gs, ln_post_mix=d_pm, ln_pre_ffn=d_pf, ln_post_ffn=d_pff, b_pe_gate=db_peg)
    small_grads = _pack_small_grads(gs, loss_term)
    dqkv, (arrived["w_gate_up"], arrived["w_out"], *arrived_small) = _attn_bwd(
        proj, dattn, stats, slopes, [_slabs(g_gu_t), _slabs(g_out), *small_grads],
        [True, True] + [False] * len(small_grads))
    send_sems, recv_sems, slabs, landing, token = _scatter_begin(_slabs(_grad_w_in([dqkv, duz], a)), "w_in_grad_send")
    grad_x, d_g1 = _in_bwd([dqkv, duz], w_in_t, xs, dh1, sm["ln_pre_mix"] + token[:1, :1])
    slabs, landing = _scatter_end(send_sems, recv_sems, slabs, landing, d_g1, "w_in_grad_arrive")
    me = 4 * lax.axis_index("x") + 2 * lax.axis_index("y") + lax.axis_index("c")
    own = lax.dynamic_slice_in_dim(slabs, me, 1, axis=0)
    arrived["w_in"] = lax.dynamic_update_slice_in_dim(landing, own, me, axis=0)
    (arrived_late,) = _scatter_call([jnp.pad(d_g1, ((0, SUBLANES - 1), (0, 0)))], [False], "ln_pre_mix_grad_exchange")

    res = {}
    for n in ROW_SHARDED:
        res[n] = _reduce_adamw(arrived[n], w[n][0], m[n][0], v[n][0], n)
    for n in ("w_in", "w_gate_up"):
        res[n] = [t.T for t in _reduce_adamw(arrived[n], w[n][0].T, m[n][0].T, v[n][0].T, n)]
    for n in ("w_pe_proj",):
        grad = _reduce(arrived[n], n).T
        res[n] = (grad, *_adamw(grad, w[n][0], m[n][0], v[n][0], n))
    small, loss_row = _small_adamw(arrived_small, arrived_late, w, m, v)

    out = []
    for k in range(4):
        out += [res[n][k][None] if n in res else small[n][k] for n in WEIGHTS]
    return (loss_row[0, 0], grad_x[None], *out)
```

```python
import math

import jax
import jax.numpy as jnp
from jax import lax
from jax.experimental import pallas as pl
from jax.experimental.pallas import tpu as pltpu

F32 = jnp.float32
BF = jnp.bfloat16


def SDS(shape, dtype):
    return pltpu.HBM(tuple(shape), dtype)

D_MODEL = 1024
ATTN_W = 512
SGU_W = 512
HEAD_DIM = 64
N_GROUPS = 4
GROUP_DIM = 128
CHUNK = 128
D_FF = 2816
PLE_DIM = 256
PROJ = 3 * ATTN_W + 2 * SGU_W
DILATIONS = (1, 4, 16)
QBLK = 128
EPS = 1e-6
NEG = -1e30
N_DEV = 8
LANES = 128

ADAM_LR = 0.001
ADAM_B1 = 0.9
ADAM_B2 = 0.999
ADAM_EPS = 1e-08
ADAM_WD = 0.01
ADAM_STEP = 10

MIB = 2 ** 20
MESH_ID = pl.DeviceIdType.MESH
HBM = pl.BlockSpec(memory_space=pl.ANY)


def _params(sem, vmem_mib):
    return pltpu.CompilerParams(dimension_semantics=sem, vmem_limit_bytes=vmem_mib * MIB)


def _dot(a, b):
    return jnp.dot(a, b, preferred_element_type=F32)


def _dot_nt(a, b):
    return lax.dot_general(a, b, (((1,), (1,)), ((), ())), preferred_element_type=F32)


def _dot_tn(a, b):
    return lax.dot_general(a, b, (((0,), (0,)), ((), ())), preferred_element_type=F32)


def _rstd(x):
    return lax.rsqrt(jnp.mean(x * x, axis=-1, keepdims=True) + EPS)


def _rms_bwd(dy, x, g):
    r = _rstd(x)
    n = x * r
    dn = dy * g
    dx = r * (dn - n * jnp.mean(dn * n, axis=-1, keepdims=True))
    return dx, dy * n


def _colsum(v):
    return jnp.sum(v, axis=0, keepdims=True)


_G0 = math.sqrt(2.0 / math.pi)
_G1 = 0.044715


def _gelu(x):
    t = jnp.tanh(_G0 * (x + _G1 * x * x * x))
    return 0.5 * x * (1.0 + t), t


def _gelu_grad(x, t):
    return 0.5 * (1.0 + t) + 0.5 * x * (1.0 - t * t) * (_G0 * (1.0 + 3.0 * _G1 * x * x))


def _sigmoid(x):
    return 0.5 * jnp.tanh(0.5 * x) + 0.5


def _row(i):
    return (i, 0)


def _fixed(i):
    return (0, 0)


def _acc_init(step, *refs):
    @pl.when(step == 0)
    def _():
        for r in refs:
            r[...] = jnp.zeros_like(r)


FLIPS = [(dx, dy, dc) for dx in (0, 1) for dy in (0, 1) for dc in (0, 1)][1:]
DMA_SEMS = pltpu.SemaphoreType.DMA


def _mesh_pos():
    return lax.axis_index("x"), lax.axis_index("y"), lax.axis_index("c")


def _remote(src, dst, sems, n, to):
    return pltpu.make_async_remote_copy(src_ref=src, dst_ref=dst, send_sem=sems[0].at[n], recv_sem=sems[1].at[n],
                                        device_id=to, device_id_type=MESH_ID)


class _Scatter:
    def __init__(self, items, sems):
        x, y, c = _mesh_pos()
        me = 4 * x + 2 * y + c
        self.local, self.sends, self.arrivals = [], [], []
        for i, (src, dst, slabbed) in enumerate(items):
            self.local.append(pltpu.make_async_copy(src.at[me] if slabbed else src, dst.at[me], sems[2].at[i]))
            for k, (dx, dy, dc) in enumerate(FLIPS):
                to = (1 - x if dx else x, 1 - y if dy else y, 1 - c if dc else c)
                peer = 4 * to[0] + 2 * to[1] + to[2]
                out = src.at[peer] if slabbed else src
                self.sends.append(_remote(out, dst.at[me], sems, 7 * i + k, to))
                self.arrivals.append(_remote(out, dst.at[peer], sems, 7 * i + k, to))

    def start(self):
        for cp in self.local + self.sends:
            cp.start()

    def wait(self):
        for cp in self.arrivals:
            cp.wait_recv()
        for cp in self.sends:
            cp.wait_send()
        for cp in self.local:
            cp.wait()


def _scatter_sems(n):
    return [DMA_SEMS((7 * n,)), DMA_SEMS((7 * n,)), DMA_SEMS((n,))]


class _Gather:
    def __init__(self, items, sems):
        x, y, c = _mesh_pos()
        me, sibling = (x, y, c), (x, y, 1 - c)
        chips = [(1 - x, y), (x, 1 - y), (1 - x, 1 - y)]
        self.first, self.passed, self.from_chips, self.rest, self.local = [], [], [], [], []
        for i, (src, dst) in enumerate(items):
            def slot(p, dst=dst):
                return dst.at[4 * p[0] + 2 * p[1] + p[2]]

            def copy(k, block, to, own=False, i=i, src=src, slot=slot):
                return _remote(src if own else slot(block), slot(block), sems, 7 * i + k, to)

            self.local.append(pltpu.make_async_copy(src, slot(me), sems[2].at[i]))
            self.first.append(copy(0, me, sibling, own=True))
            self.first += [copy(1 + j, me, (*chip, c), own=True) for j, chip in enumerate(chips)]
            self.passed += [copy(4 + j, (*chip, c), sibling) for j, chip in enumerate(chips)]
            self.from_chips += [copy(1 + j, (*chip, c), me) for j, chip in enumerate(chips)]
            self.rest.append(copy(0, sibling, me))
            self.rest += [copy(4 + j, (*chip, 1 - c), me) for j, chip in enumerate(chips)]

    def start(self):
        for cp in self.local + self.first:
            cp.start()

    def forward(self):
        for arrived, onward in zip(self.from_chips, self.passed):
            arrived.wait_recv()
            onward.start()

    def finish(self):
        for cp in self.rest:
            cp.wait_recv()
        for cp in self.first + self.passed:
            cp.wait_send()
        for cp in self.local:
            cp.wait()


def _all_gather(shard, name):
    def body(x_ref, out_ref, *sems):
        g = _Gather([(x_ref, out_ref)], sems)
        g.start()
        g.forward()
        g.finish()

    return pl.pallas_call(
        body, out_shape=SDS((N_DEV,) + shard.shape, shard.dtype), in_specs=[HBM], out_specs=HBM,
        scratch_shapes=_scatter_sems(1), name=name)(shard)


def _scatter_call(srcs, slabbed, name):
    n = len(srcs)

    def body(*refs):
        sc = _Scatter(list(zip(refs[:n], refs[n:2 * n], slabbed)), refs[2 * n:])
        sc.start()
        sc.wait()

    shapes = [SDS(s.shape if sl else (N_DEV,) + s.shape, s.dtype) for s, sl in zip(srcs, slabbed)]
    return pl.pallas_call(body, out_shape=shapes, in_specs=[HBM] * n, out_specs=[HBM] * n,
                          scratch_shapes=_scatter_sems(n), name=name)(*srcs)


SEM = pl.BlockSpec(memory_space=pltpu.SEMAPHORE)
N_PEERS = len(FLIPS)


def _slab_copies(src_ref, land_ref, send_sems, recv_sems):
    x, y, c = _mesh_pos()
    me = 4 * x + 2 * y + c
    copies = []
    for k, (dx, dy, dc) in enumerate(FLIPS):
        to = (1 - x if dx else x, 1 - y if dy else y, 1 - c if dc else c)
        peer = 4 * to[0] + 2 * to[1] + to[2]
        sems = (send_sems, recv_sems)
        copies.append((_remote(src_ref.at[peer], land_ref.at[me], sems, k, to),
                       _remote(src_ref.at[peer], land_ref.at[peer], sems, k, to)))
    return copies


def _scatter_begin(src, name):
    def body(src_ref, land_ref, send_sems, recv_sems, src_thru, land_thru, token):
        for send, _ in _slab_copies(src_ref, land_ref, send_sems, recv_sems):
            send.start()
        token[...] = jnp.zeros_like(token)

    landing = lax.empty(src.shape, src.dtype)
    return pl.pallas_call(
        body, name=name,
        out_shape=(pltpu.SemaphoreType.DMA((N_PEERS,)), pltpu.SemaphoreType.DMA((N_PEERS,)),
                   pltpu.HBM(src.shape, src.dtype), pltpu.HBM(src.shape, src.dtype),
                   jax.ShapeDtypeStruct((SUBLANES, LANES), F32)),
        in_specs=(HBM, HBM), out_specs=(SEM, SEM, HBM, HBM, pl.BlockSpec(memory_space=pltpu.VMEM)),
        input_output_aliases={0: 2, 1: 3},
        compiler_params=pltpu.CompilerParams(has_side_effects=pltpu.SideEffectType.DATAFLOW_SIDE_EFFECTING))(
            pltpu.with_memory_space_constraint(src, pltpu.HBM), pltpu.with_memory_space_constraint(landing, pltpu.HBM))


def _scatter_end(send_sems, recv_sems, src_thru, land_thru, after, name):
    def body(src_ref, land_ref, send_sems, recv_sems, after_ref, src_dead, land_out):
        for send, arrival in _slab_copies(src_ref, land_ref, send_sems, recv_sems):
            send.wait_send()
            arrival.wait_recv()

    return pl.pallas_call(
        body, name=name,
        out_shape=(pltpu.HBM(src_thru.shape, src_thru.dtype), pltpu.HBM(land_thru.shape, land_thru.dtype)),
        in_specs=(HBM, HBM, SEM, SEM, HBM), out_specs=(HBM, HBM), input_output_aliases={0: 0, 1: 1},
        compiler_params=pltpu.CompilerParams(has_side_effects=pltpu.SideEffectType.DATAFLOW_SIDE_EFFECTING))(
            src_thru, land_thru, send_sems, recv_sems, after)


def _in_proj(x, g1, w_in_t, tm=512):
    T = x.shape[0]

    def body(x_ref, g_ref, w_ref, proj_ref, a_ref):
        xv = x_ref[...]
        a = (xv * _rstd(xv) * g_ref[...]).astype(BF)
        a_ref[...] = a
        proj_ref[...] = _dot_nt(a, w_ref[...])

    return pl.pallas_call(
        body, grid=(T // tm,),
        in_specs=[pl.BlockSpec((tm, D_MODEL), _row), pl.BlockSpec((1, D_MODEL), _fixed),
                  pl.BlockSpec((PROJ, D_MODEL), _fixed)],
        out_specs=[pl.BlockSpec((tm, PROJ), _row), pl.BlockSpec((tm, D_MODEL), _row)],
        out_shape=[SDS((T, PROJ), F32), SDS((T, D_MODEL), BF)],
        name="in_proj", compiler_params=_params(("arbitrary",), 48))(x, g1, w_in_t)


ATTN_GROUP = 16


def _attn_bias(sl_ref, bias_ref):
    qi = lax.broadcasted_iota(jnp.int32, (QBLK, QBLK), 0)
    kj = lax.broadcasted_iota(jnp.int32, (QBLK, QBLK), 1)
    step = qi - kj
    for di, d in enumerate(DILATIONS):
        for j in range(2):
            sl = sl_ref[0, j:j + 1, :]
            cur = jnp.where(step >= 0, -sl * (step * d).astype(F32), NEG)
            prev = jnp.where(step <= 0, -sl * ((step + QBLK) * d).astype(F32), NEG)
            rows = slice(j * QBLK, (j + 1) * QBLK)
            bias_ref[di * 2, rows, :QBLK] = prev
            bias_ref[di * 2, rows, QBLK:] = cur
            bias_ref[di * 2 + 1, rows, :QBLK] = jnp.full((QBLK, QBLK), NEG, F32)
            bias_ref[di * 2 + 1, rows, QBLK:] = cur


def _stack_heads(x, lo):
    return jnp.concatenate([jnp.where(lo, x, 0.0), jnp.where(lo, 0.0, x)], axis=0).astype(BF)


def _unstack_heads(x, lo):
    return jnp.where(lo, x[:QBLK], x[QBLK:])


def _attn_rows(start, d, blocks=1):
    if d == 1:
        return pl.ds(pl.multiple_of(start, QBLK), blocks * QBLK)
    return pl.ds(start, blocks * QBLK, stride=d)


def _attn_group_index(i, nblk, d, group):
    per = nblk // d // group
    r = i // per
    n0 = (i % per) * group
    start = r + (d * QBLK) * n0
    pstart = jnp.maximum(start - d * QBLK, r)
    return start, pstart, n0 == 0


def _attn_plan(nblk, d):
    group = min(ATTN_GROUP, nblk // d)
    return group, max(1, min(ATTN_GROUP // group, d))


def _attn_fwd(proj, slopes, to_gather):
    T = proj.shape[0]
    nblk = T // QBLK
    n_g = len(to_gather)

    def body(q_ref, k_ref, v_ref, sl_ref, *rest):
        srcs, (o_ref, m_ref), dsts = rest[:n_g], rest[n_g:n_g + 2], rest[n_g + 2:2 * n_g + 2]
        sems, (l_ref, bias_ref) = rest[2 * n_g + 2:2 * n_g + 5], rest[2 * n_g + 5:]
        h = pl.program_id(0)

        @pl.when(h == 0)
        def _():
            _Gather(list(zip(srcs, dsts)), sems).start()

        @pl.when(h == pl.num_programs(0) - 1)
        def _():
            _Gather(list(zip(srcs, dsts)), sems).forward()

        _attn_bias(sl_ref, bias_ref)
        lo = lax.broadcasted_iota(jnp.int32, (1, LANES), 1) < HEAD_DIM

        order = list(enumerate(DILATIONS))[::-1]
        for di, d in order:
            group, segs = _attn_plan(nblk, d)
            fresh, last = di == order[0][0], di == order[-1][0]

            def step(i, carry, segs=segs, **kw):
                for s in range(segs):
                    segment(i * segs + s, **kw)
                return carry

            def segment(i, d=d, di=di, group=group, fresh=fresh, last=last):
                start, pstart, first = _attn_group_index(i, nblk, d, group)
                prows = _attn_rows(pstart, d)
                alone = group == nblk // d
                k_prev = v_prev = None
                if not alone:
                    k_prev, v_prev = k_ref[prows, :].astype(BF), v_ref[prows, :].astype(BF)
                for b in range(group):
                    out = _attn_rows(start + b * (d * QBLK), d)
                    k_own, v_own = k_ref[out, :].astype(BF), v_ref[out, :].astype(BF)
                    if alone and b == 0:
                        k2, v2, bias = k_own, v_own, bias_ref[di * 2, :, QBLK:]
                    else:
                        k2, v2 = jnp.concatenate([k_prev, k_own], axis=0), jnp.concatenate([v_prev, v_own], axis=0)
                        bias = bias_ref[di * 2 + first.astype(jnp.int32)] if b == 0 else bias_ref[di * 2]
                    k_prev, v_prev = k_own, v_own
                    s = _dot_nt(_stack_heads(q_ref[out, :] * (HEAD_DIM ** -0.5), lo), k2) + bias
                    m = jnp.max(s, axis=-1, keepdims=True)
                    pr = jnp.exp(s - m)
                    m_b = _unstack_heads(m, lo)
                    l_b = _unstack_heads(jnp.sum(pr, axis=-1, keepdims=True), lo)
                    o_b = _unstack_heads(_dot(pr.astype(BF), v2), lo)
                    if fresh:
                        m_ref[out, :] = m_b
                        l_ref[out, :] = l_b
                        o_ref[out, :] = o_b
                        continue
                    m_o = m_ref[out, :]
                    m_n = jnp.maximum(m_o, m_b)
                    wa, wb = jnp.exp(m_o - m_n), jnp.exp(m_b - m_n)
                    l_n = wa * l_ref[out, :] + wb * l_b
                    o_n = wa * o_ref[out, :] + wb * o_b
                    if last:
                        m_ref[out, :] = m_n + jnp.log(l_n)
                        o_ref[out, :] = o_n / l_n
                    else:
                        m_ref[out, :] = m_n
                        l_ref[out, :] = l_n
                        o_ref[out, :] = o_n

            lax.fori_loop(0, nblk // (group * segs), step, 0)

        @pl.when(h == pl.num_programs(0) - 1)
        def _():
            _Gather(list(zip(srcs, dsts)), sems).finish()

    col = lambda base: pl.BlockSpec((T, LANES), lambda h: (0, base + h))
    tok = pl.BlockSpec((T, LANES), lambda h: (0, h))
    outs = pl.pallas_call(
        body, grid=(4,),
        in_specs=[col(0), col(4), col(8), pl.BlockSpec((1, 8, LANES), lambda h: (h, 0, 0))] + [HBM] * n_g,
        out_specs=[tok, tok] + [HBM] * n_g,
        out_shape=[SDS((T, ATTN_W), F32), SDS((T, ATTN_W), F32)]
        + [SDS((N_DEV,) + g.shape, g.dtype) for g in to_gather],
        scratch_shapes=_scatter_sems(n_g) + [pltpu.VMEM((T, LANES), F32), pltpu.VMEM((6, 2 * QBLK, 2 * QBLK), F32)],
        name="attn_fwd", compiler_params=_params(("arbitrary",), 56))(proj, proj, proj, slopes, *to_gather)
    return outs[0], outs[1], outs[2:]


def _sgu_norm(zv, ln_g, ln_b):
    gz, tz = _gelu(zv)
    mu = jnp.mean(gz, axis=-1, keepdims=True)
    xc = gz - mu
    rs = lax.rsqrt(jnp.mean(xc * xc, axis=-1, keepdims=True) + EPS)
    xhat = xc * rs
    return xhat * ln_g + ln_b, xhat, rs, tz


def _causal(w):
    i = lax.broadcasted_iota(jnp.int32, (CHUNK, CHUNK), 0)
    j = lax.broadcasted_iota(jnp.int32, (CHUNK, CHUNK), 1)
    return jnp.where(i >= j, w, 0.0)


def _sgu_fwd_tile(u_ref, z_ref, g_ref, b_ref, w_ref, bs_ref, out_ref):
    for g in range(N_GROUPS):
        wm = _causal(w_ref[g]).astype(BF)
        cols = slice(g * GROUP_DIM, (g + 1) * GROUP_DIM)
        for c in range(u_ref.shape[0] // CHUNK):
            rows = slice(c * CHUNK, (c + 1) * CHUNK)
            zn, _, _, _ = _sgu_norm(z_ref[rows, cols], g_ref[...], b_ref[...])
            mixed = _dot(wm, zn.astype(BF)) + bs_ref[:, g:g + 1]
            gu, _ = _gelu(u_ref[rows, cols])
            out_ref[rows, cols] = gu * mixed


def _out_proj(attn, proj, x, ln_g, ln_b, w_s, b_st, g_a, g_s, w_out, g_pm, g_pf, tm=512):
    T = x.shape[0]

    def body(a_ref, u_ref, z_ref, x_ref, lg_ref, lb_ref, ws_ref, bs_ref, ga_ref, gs_ref, w_ref, gpm_ref, gpf_ref,
             s_ref, grp_ref, mixed_ref, h1_ref, f_ref):
        _sgu_fwd_tile(u_ref, z_ref, lg_ref, lb_ref, ws_ref, bs_ref, s_ref)
        av, sv = a_ref[...], s_ref[...]
        an = (av * _rstd(av) * ga_ref[...]).astype(BF)
        sn = (sv * _rstd(sv) * gs_ref[...]).astype(BF)
        grp_ref[:, :ATTN_W] = an
        grp_ref[:, ATTN_W:] = sn
        mixed = _dot(an, w_ref[:ATTN_W, :]) + _dot(sn, w_ref[ATTN_W:, :])
        mixed_ref[...] = mixed
        h1 = x_ref[...] + mixed * _rstd(mixed) * gpm_ref[...]
        h1_ref[...] = h1
        f_ref[...] = (h1 * _rstd(h1) * gpf_ref[...]).astype(BF)

    tok = lambda w: pl.BlockSpec((tm, w), _row)
    vec = lambda w: pl.BlockSpec((1, w), _fixed)
    return pl.pallas_call(
        body, grid=(T // tm,),
        in_specs=[tok(ATTN_W), pl.BlockSpec((tm, SGU_W), lambda i: (i, 3)), pl.BlockSpec((tm, SGU_W), lambda i: (i, 4)),
                  tok(D_MODEL), vec(GROUP_DIM), vec(GROUP_DIM), pl.BlockSpec((N_GROUPS, CHUNK, CHUNK), lambda i: (0, 0, 0)),
                  pl.BlockSpec((CHUNK, LANES), _fixed), vec(ATTN_W), vec(SGU_W),
                  pl.BlockSpec((D_MODEL, D_MODEL), _fixed), vec(D_MODEL), vec(D_MODEL)],
        out_specs=[tok(SGU_W)] + [tok(D_MODEL)] * 4,
        out_shape=[SDS((T, SGU_W), F32), SDS((T, D_MODEL), BF), SDS((T, D_MODEL), F32), SDS((T, D_MODEL), F32),
                   SDS((T, D_MODEL), BF)],
        name="out_proj", compiler_params=_params(("arbitrary",), 52))(
            attn, proj, proj, x, ln_g, ln_b, w_s, b_st, g_a, g_s, w_out, g_pm, g_pf)


FF_TILE = 1408
FF_TILES = D_FF // FF_TILE
FF_CHUNK = 256


def _gate_up(f, w_gu_t, tm=512):
    T = f.shape[0]
    tn = FF_TILE

    def body(f_ref, wg_ref, wu_ref, g_ref, u_ref, act_ref):
        fv = f_ref[...]
        g = _dot_nt(fv, wg_ref[...])
        u = _dot_nt(fv, wu_ref[...])
        g_ref[...] = g.astype(BF)
        u_ref[...] = u.astype(BF)
        act_ref[...] = (g * _sigmoid(g) * u).astype(BF)

    ospec = pl.BlockSpec((tm, tn), lambda j, i: (i, j))
    return pl.pallas_call(
        body, grid=(FF_TILES, T // tm),
        in_specs=[pl.BlockSpec((tm, D_MODEL), lambda j, i: (i, 0)), pl.BlockSpec((tn, D_MODEL), lambda j, i: (j, 0)),
                  pl.BlockSpec((tn, D_MODEL), lambda j, i: (j + FF_TILES, 0))],
        out_specs=[ospec] * 3, out_shape=[SDS((T, D_FF), BF)] * 3,
        name="gate_up", compiler_params=_params(("arbitrary", "arbitrary"), 40))(f, w_gu_t, w_gu_t)


def _down_proj(act, w_down, h1, g_pff, tm=512):
    T = act.shape[0]

    def body(a_ref, w_ref, h1_ref, g_ref, y_ref, h2_ref):
        y = _dot(a_ref[...], w_ref[...])
        y_ref[...] = y
        h2_ref[...] = h1_ref[...] + y * _rstd(y) * g_ref[...]

    return pl.pallas_call(
        body, grid=(T // tm,),
        in_specs=[pl.BlockSpec((tm, D_FF), _row), pl.BlockSpec((D_FF, D_MODEL), _fixed),
                  pl.BlockSpec((tm, D_MODEL), _row), pl.BlockSpec((1, D_MODEL), _fixed)],
        out_specs=[pl.BlockSpec((tm, D_MODEL), _row)] * 2,
        out_shape=[SDS((T, D_MODEL), F32)] * 2,
        name="down_proj", compiler_params=_params(("arbitrary",), 48))(act, w_down, h1, g_pff)


def _pe_loss_and_bwd(h2, p, target, y, w_peg, b_peg, w_pep_t, g_pff, tm=512):
    T = h2.shape[0]

    def body(h2_ref, p_ref, t_ref, y_ref, wg_ref, b_ref, wp_ref, g_ref,
             dh2_ref, dy_ref, gpeg_ref, gpep_ref, loss_ref, db_ref, dg_ref):
        _acc_init(pl.program_id(0), gpeg_ref, gpep_ref, loss_ref, db_ref, dg_ref)
        h2v = h2_ref[...]
        h2b = h2v.astype(BF)
        pb = p_ref[...].astype(BF)
        gate = _sigmoid(_dot(h2b, wg_ref[...]) + b_ref[...])
        pp = _dot_nt(pb, wp_ref[...])
        diff = h2v + gate * pp - t_ref[...]
        loss_ref[...] += _colsum(diff * diff)
        dh3 = diff * (1.0 / D_MODEL)
        dpre = dh3 * pp * (gate * (1.0 - gate))
        dpre_b = dpre.astype(BF)
        db_ref[...] += _colsum(dpre)
        gpeg_ref[...] += _dot_tn(h2b, dpre_b)
        gpep_ref[...] += _dot_tn((dh3 * gate).astype(BF), pb)
        dh2 = dh3 + _dot_nt(dpre_b, wg_ref[...])
        dh2_ref[...] = dh2
        dy, dg = _rms_bwd(dh2, y_ref[...], g_ref[...])
        dy_ref[...] = dy.astype(BF)
        dg_ref[...] += _colsum(dg)

    tok = lambda w: pl.BlockSpec((tm, w), _row)
    vec = pl.BlockSpec((1, D_MODEL), _fixed)
    wg = pl.BlockSpec((D_MODEL, D_MODEL), _fixed)
    wp = pl.BlockSpec((D_MODEL, PLE_DIM), _fixed)
    return pl.pallas_call(
        body, grid=(T // tm,),
        in_specs=[tok(D_MODEL), tok(PLE_DIM), tok(D_MODEL), tok(D_MODEL), wg, vec, wp, vec],
        out_specs=[tok(D_MODEL), tok(D_MODEL), wg, wp, vec, vec, vec],
        out_shape=[SDS((T, D_MODEL), F32), SDS((T, D_MODEL), BF), SDS((D_MODEL, D_MODEL), F32),
                   SDS((D_MODEL, PLE_DIM), F32)] + [SDS((1, D_MODEL), F32)] * 3,
        name="pe_loss_and_bwd", compiler_params=_params(("arbitrary",), 56))(
            h2, p, target, y, w_peg, b_peg, w_pep_t, g_pff)


def _weight_grad(a, dy, name, into=None, row_tile=0, rows=None, tk=1024):
    n = dy.shape[1]
    tn = min(n, 1024)
    T, ka = a.shape
    tka = FF_TILE if ka == D_FF else min(ka, 1024)
    rows = ka if rows is None else rows

    def body(a_ref, dy_ref, *rest):
        out_ref = rest[-1]
        _acc_init(pl.program_id(2), out_ref)
        out_ref[...] += _dot_tn(a_ref[...].astype(BF), dy_ref[...].astype(BF))

    carried = [] if into is None else [into]
    return pl.pallas_call(
        body, grid=(ka // tka, n // tn, T // tk),
        in_specs=[pl.BlockSpec((tk, tka), lambda i, j, k: (k, i)), pl.BlockSpec((tk, tn), lambda i, j, k: (k, j))]
        + [HBM] * len(carried),
        out_specs=pl.BlockSpec((tka, tn), lambda i, j, k: (i + row_tile, j)),
        out_shape=SDS((rows, n), F32), input_output_aliases={2: 0} if carried else {},
        name="grad_" + name, compiler_params=_params(("arbitrary",) * 3, 40))(a, dy, *carried)


def _grad_w_in(dparts, a, tk=1024):
    T = a.shape[0]

    def body(*refs):
        d_refs, a_ref, out_ref, acc_ref = refs[:len(dparts)], refs[-3], refs[-2], refs[-1]
        k = pl.program_id(0)
        _acc_init(k, acc_ref)
        cols = [r[part].astype(BF) for r in d_refs for part in range(r.shape[0])]
        acc_ref[...] += _dot_tn(jnp.concatenate(cols, axis=1), a_ref[...])

        @pl.when(k == pl.num_programs(0) - 1)
        def _():
            out_ref[...] = acc_ref[...].astype(BF)

    return pl.pallas_call(
        body, grid=(T // tk,),
        in_specs=[pl.BlockSpec((d.shape[0], tk, d.shape[2]), lambda k: (0, k, 0)) for d in dparts]
        + [pl.BlockSpec((tk, D_MODEL), lambda k: (k, 0))],
        out_specs=pl.BlockSpec((PROJ, D_MODEL), lambda k: (0, 0)),
        out_shape=SDS((PROJ, D_MODEL), BF), scratch_shapes=[pltpu.VMEM((PROJ, D_MODEL), F32)],
        name="grad_w_in", compiler_params=_params(("arbitrary",), 48))(*dparts, a)


def _down_bwd(dy, w_down, g, u, to_send, tm=512):
    T = dy.shape[0]
    n_s = len(to_send)

    def body(dy_ref, w_ref, g_ref, u_ref, *rest):
        srcs, (dg_ref, du_ref), dsts, sems = rest[:n_s], rest[n_s:n_s + 2], rest[n_s + 2:2 * n_s + 2], rest[2 * n_s + 2:]
        i = pl.program_id(0)
        items = list(zip(srcs, dsts, [True] * n_s))

        @pl.when(i == 0)
        def _():
            _Scatter(items, sems).start()

        dyv = dy_ref[...]
        for c in range(D_FF // FF_CHUNK):
            cols = slice(c * FF_CHUNK, (c + 1) * FF_CHUNK)
            dact = _dot_nt(dyv, w_ref[cols, :]).astype(BF)
            gv, uv = g_ref[:, cols], u_ref[:, cols]
            s = _sigmoid(gv)
            ds = dact * s
            dg_ref[:, cols] = ds * uv * (1.0 + gv * (1.0 - s))
            du_ref[:, cols] = ds * gv

        @pl.when(i == pl.num_programs(0) - 1)
        def _():
            _Scatter(items, sems).wait()

    tile = pl.BlockSpec((tm, D_FF), _row)
    outs = pl.pallas_call(
        body, grid=(T // tm,),
        in_specs=[pl.BlockSpec((tm, D_MODEL), _row),
                  pl.BlockSpec((D_FF, D_MODEL), _fixed, pipeline_mode=pl.Buffered(1)), tile, tile] + [HBM] * n_s,
        out_specs=[tile, tile] + [HBM] * n_s,
        out_shape=[SDS((T, D_FF), BF)] * 2 + [SDS(s.shape, s.dtype) for s in to_send],
        scratch_shapes=_scatter_sems(n_s),
        name="down_bwd", compiler_params=_params(("arbitrary",), 48))(dy, w_down, g, u, *to_send)
    return outs[0], outs[1], outs[2:]


def _ffn_in_bwd(dg, du, w_gu_t, h1, dh2, mixed, g_pf, g_pm, to_send, tm=512):
    T = h1.shape[0]
    n_s = len(to_send)

    def body(dg_ref, du_ref, w_ref, h1_ref, dh2_ref, mx_ref, gpf_ref, gpm_ref, *rest):
        srcs, outs, dsts, sems = rest[:n_s], rest[n_s:n_s + 4], rest[n_s + 4:2 * n_s + 4], rest[2 * n_s + 4:]
        dh1_ref, dmx_ref, dgpf_ref, dgpm_ref = outs
        i = pl.program_id(0)
        items = list(zip(srcs, dsts, [True] * n_s))
        _acc_init(i, dgpf_ref, dgpm_ref)

        @pl.when(i == 0)
        def _():
            _Scatter(items, sems).start()

        df = _dot(jnp.concatenate([dg_ref[...], du_ref[...]], axis=1), w_ref[...])
        dx, dgf = _rms_bwd(df, h1_ref[...], gpf_ref[...])
        dh1 = dh2_ref[...] + dx
        dh1_ref[...] = dh1
        dmx, dgm = _rms_bwd(dh1, mx_ref[...], gpm_ref[...])
        dmx_ref[...] = dmx.astype(BF)
        dgpf_ref[...] += _colsum(dgf)
        dgpm_ref[...] += _colsum(dgm)

        @pl.when(i == pl.num_programs(0) - 1)
        def _():
            _Scatter(items, sems).wait()

    tok = lambda w: pl.BlockSpec((tm, w), _row)
    vec = pl.BlockSpec((1, D_MODEL), _fixed)
    outs = pl.pallas_call(
        body, grid=(T // tm,),
        in_specs=[tok(D_FF), tok(D_FF), pl.BlockSpec((2 * D_FF, D_MODEL), _fixed, pipeline_mode=pl.Buffered(1)),
                  tok(D_MODEL), tok(D_MODEL), tok(D_MODEL), vec, vec]
        + [HBM] * n_s,
        out_specs=[tok(D_MODEL), tok(D_MODEL), vec, vec] + [HBM] * n_s,
        out_shape=[SDS((T, D_MODEL), F32), SDS((T, D_MODEL), BF), SDS((1, D_MODEL), F32), SDS((1, D_MODEL), F32)]
        + [SDS(s.shape, s.dtype) for s in to_send],
        scratch_shapes=_scatter_sems(n_s),
        name="ffn_in_bwd", compiler_params=_params(("arbitrary",), 56))(
            dg, du, w_gu_t, h1, dh2, mixed, g_pf, g_pm, *to_send)
    return outs[0], outs[1], outs[2], outs[3], outs[4:]


STAT_LANES = HEAD_DIM // 2


def _ff_grad_spec(half):
    return pl.BlockSpec((D_FF, D_MODEL), lambda i: (half, 0), pipeline_mode=pl.Buffered(1))


def _out_bwd(dmx, w_out, attn, lse, sgu, g_a, g_s, dg, f, tm=512):
    T = attn.shape[0]

    def body(dm_ref, w_ref, a_ref, l_ref, s_ref, ga_ref, gs_ref, dgate_ref, f_ref,
             da_ref, st_ref, ds_ref, dga_ref, dgs_ref, ggu_ref, gacc_ref):
        _acc_init(pl.program_id(0), dga_ref, dgs_ref, gacc_ref)
        gacc_ref[...] += _dot_tn(dgate_ref[...], f_ref[...])

        @pl.when(pl.program_id(0) == pl.num_programs(0) - 1)
        def _():
            ggu_ref[...] = gacc_ref[...].astype(BF)

        dgr = _dot_nt(dm_ref[...], w_ref[...])
        av = a_ref[...]
        da, dga = _rms_bwd(dgr[:, :ATTN_W], av, ga_ref[...])
        ds, dgs = _rms_bwd(dgr[:, ATTN_W:], s_ref[...], gs_ref[...])
        da_ref[...] = da
        ds_ref[...] = ds
        dga_ref[...] += _colsum(dga)
        dgs_ref[...] += _colsum(dgs)
        lane = lax.broadcasted_iota(jnp.int32, (1, LANES), 1)
        lo = lane < HEAD_DIM
        first = (lane % HEAD_DIM) < STAT_LANES
        prod = da * av
        for c in range(ATTN_W // LANES):
            cols = slice(c * LANES, (c + 1) * LANES)
            pc = prod[:, cols]
            delta = jnp.where(lo, jnp.sum(jnp.where(lo, pc, 0.0), axis=-1, keepdims=True),
                              jnp.sum(jnp.where(lo, 0.0, pc), axis=-1, keepdims=True))
            st_ref[:, cols] = jnp.where(first, l_ref[:, cols], delta)

    tok = lambda w: pl.BlockSpec((tm, w), _row)
    vec = lambda w: pl.BlockSpec((1, w), _fixed)
    return pl.pallas_call(
        body, grid=(T // tm,),
        in_specs=[tok(D_MODEL), pl.BlockSpec((D_MODEL, D_MODEL), _fixed), tok(ATTN_W), tok(ATTN_W), tok(SGU_W),
                  vec(ATTN_W), vec(SGU_W), tok(D_FF), tok(D_MODEL)],
        out_specs=[tok(ATTN_W), tok(ATTN_W), tok(SGU_W), vec(ATTN_W), vec(SGU_W), _ff_grad_spec(0)],
        out_shape=[SDS((T, ATTN_W), F32), SDS((T, ATTN_W), F32), SDS((T, SGU_W), F32), SDS((1, ATTN_W), F32),
                   SDS((1, SGU_W), F32), SDS((2 * D_FF, D_MODEL), BF)],
        scratch_shapes=[pltpu.VMEM((D_FF, D_MODEL), F32)],
        name="out_bwd", compiler_params=_params(("arbitrary",), 56))(dmx, w_out, attn, lse, sgu, g_a, g_s, dg, f)


def _sgu_bwd(proj, dsgu, ln_g, ln_b, w_s, b_st, groups, dmx, d_up, f, g_gu_t, tm=512):
    T = proj.shape[0]

    def body(u_ref, z_ref, ds_ref, g_ref, b_ref, w_ref, bs_ref, grp_ref, dmx_ref, dup_ref, f_ref, _,
             duz_ref, dw_ref, dbs_ref, dlg_ref, dlb_ref, gout_ref, ggu_ref, dbacc_ref, gacc_ref):
        du_ref, dz_ref = duz_ref.at[0], duz_ref.at[1]
        step = pl.program_id(0)
        _acc_init(step, dw_ref, dbs_ref, dlg_ref, dlb_ref, gout_ref, dbacc_ref, gacc_ref)
        gout_ref[...] += _dot_tn(grp_ref[...], dmx_ref[...])
        gacc_ref[...] += _dot_tn(dup_ref[...], f_ref[...])
        lng, lnb = g_ref[...], b_ref[...]
        for g in range(N_GROUPS):
            wm = _causal(w_ref[g]).astype(BF)
            cols = slice(g * GROUP_DIM, (g + 1) * GROUP_DIM)
            for c in range(tm // CHUNK):
                rows = slice(c * CHUNK, (c + 1) * CHUNK)
                zv, uv, dout = z_ref[rows, cols], u_ref[rows, cols], ds_ref[rows, cols]
                zn, xhat, rs, tz = _sgu_norm(zv, lng, lnb)
                znb = zn.astype(BF)
                mixed = _dot(wm, znb) + bs_ref[:, g:g + 1]
                gu, tu = _gelu(uv)
                du_ref[rows, cols] = (dout * mixed * _gelu_grad(uv, tu)).astype(BF)
                dmix = dout * gu
                dmb = dmix.astype(BF)
                dw_ref[g] += _causal(_dot_nt(dmb, znb))
                dbacc_ref[g] += dmix
                dzn = _dot_tn(wm, dmb)
                dlg_ref[...] += _colsum(dzn * xhat)
                dlb_ref[...] += _colsum(dzn)
                dxh = dzn * lng
                dgz = rs * (dxh - jnp.mean(dxh, axis=-1, keepdims=True)
                            - xhat * jnp.mean(dxh * xhat, axis=-1, keepdims=True))
                dz_ref[rows, cols] = (dgz * _gelu_grad(zv, tz)).astype(BF)

        @pl.when(step == pl.num_programs(0) - 1)
        def _():
            lane = lax.broadcasted_iota(jnp.int32, (CHUNK, LANES), 1)
            acc = jnp.zeros((CHUNK, LANES), F32)
            for g in range(N_GROUPS):
                acc = jnp.where(lane == g, jnp.sum(dbacc_ref[g], axis=-1, keepdims=True), acc)
            dbs_ref[...] = acc
            ggu_ref[...] = gacc_ref[...].astype(BF)

    tok = pl.BlockSpec((tm, SGU_W), _row)
    vec = pl.BlockSpec((1, GROUP_DIM), _fixed)
    wsp = pl.BlockSpec((N_GROUPS, CHUNK, CHUNK), lambda i: (0, 0, 0))
    sq = pl.BlockSpec((CHUNK, LANES), _fixed)
    wide = pl.BlockSpec((tm, D_MODEL), _row)
    return pl.pallas_call(
        body, grid=(T // tm,),
        in_specs=[pl.BlockSpec((tm, SGU_W), lambda i: (i, 3)), pl.BlockSpec((tm, SGU_W), lambda i: (i, 4)), tok,
                  vec, vec, wsp, sq, wide, wide, pl.BlockSpec((tm, D_FF), _row), wide, HBM],
        out_specs=[pl.BlockSpec((2, tm, SGU_W), lambda i: (0, i, 0)), wsp, sq, vec, vec,
                   pl.BlockSpec((D_MODEL, D_MODEL), _fixed), _ff_grad_spec(1)],
        out_shape=[SDS((2, T, SGU_W), BF), SDS((N_GROUPS, CHUNK, CHUNK), F32),
                   SDS((CHUNK, LANES), F32), SDS((1, GROUP_DIM), F32), SDS((1, GROUP_DIM), F32),
                   SDS((D_MODEL, D_MODEL), F32), SDS((2 * D_FF, D_MODEL), BF)],
        input_output_aliases={11: 6},
        scratch_shapes=[pltpu.VMEM((N_GROUPS, CHUNK, LANES), F32), pltpu.VMEM((D_FF, D_MODEL), F32)],
        name="sgu_bwd", compiler_params=_params(("arbitrary",), 56))(
            proj, proj, dsgu, ln_g, ln_b, w_s, b_st, groups, dmx, d_up, f, g_gu_t)


def _attn_bwd(proj, do, stats, slopes, to_send, slabbed):
    T = proj.shape[0]
    nblk = T // QBLK
    n_s = len(to_send)

    def body(q_ref, k_ref, v_ref, do_ref, st_ref, sl_ref, *rest):
        srcs, d_ref, dsts = rest[:n_s], rest[n_s], rest[n_s + 1:2 * n_s + 1]
        sems, bias_ref = rest[2 * n_s + 1:2 * n_s + 4], rest[2 * n_s + 4]
        dq_ref, dk_ref, dv_ref = d_ref.at[0], d_ref.at[1], d_ref.at[2]
        h = pl.program_id(0)
        items = list(zip(srcs, dsts, slabbed))

        @pl.when(h == 0)
        def _():
            _Scatter(items, sems).start()

        _attn_bias(sl_ref, bias_ref)
        lo = lax.broadcasted_iota(jnp.int32, (1, LANES), 1) < HEAD_DIM
        scale = HEAD_DIM ** -0.5
        d_ref[...] = jnp.zeros_like(d_ref)

        for di, d in enumerate(DILATIONS):
            group, segs = _attn_plan(nblk, d)

            def step(i, carry, segs=segs, **kw):
                for s in range(segs):
                    segment(i * segs + s, **kw)
                return carry

            def segment(i, d=d, di=di, group=group):
                start, pstart, first = _attn_group_index(i, nblk, d, group)
                rows, prows = _attn_rows(start, d, group), _attn_rows(pstart, d)
                alone = group == nblk // d
                q = q_ref[rows, :] * scale
                if alone:
                    k, v, own0 = k_ref[rows, :].astype(BF), v_ref[rows, :].astype(BF), 0
                else:
                    k = jnp.concatenate([k_ref[prows, :], k_ref[rows, :]], axis=0).astype(BF)
                    v = jnp.concatenate([v_ref[prows, :], v_ref[rows, :]], axis=0).astype(BF)
                    own0 = QBLK
                dov = do_ref[rows, :]
                stats = st_ref[rows, :]
                masks = [lo, ~lo]
                qm = [jnp.where(masks[j], q, 0.0).astype(BF) for j in range(2)]
                dom = [jnp.where(masks[j], dov, 0.0).astype(BF) for j in range(2)]
                for b in range(group):
                    qb = slice(b * QBLK, (b + 1) * QBLK)
                    own_only = alone and b == 0
                    kb = slice(own0 + (b if own_only else b - 1) * QBLK, own0 + (b + 1) * QBLK)
                    which = di * 2 + first.astype(jnp.int32) if b == 0 and not own_only else di * 2
                    dq_parts, prs, dss = [], [], []
                    for j in range(2):
                        bias = bias_ref[which, j * QBLK:(j + 1) * QBLK, QBLK if own_only else 0:]
                        lj = stats[qb, j * HEAD_DIM:j * HEAD_DIM + 1]
                        delta = stats[qb, j * HEAD_DIM + STAT_LANES:j * HEAD_DIM + STAT_LANES + 1]
                        pr = jnp.exp(_dot_nt(qm[j][qb], k[kb]) + bias - lj)
                        ds = (pr * (_dot_nt(dom[j][qb], v[kb]) - delta)).astype(BF)
                        dq_parts.append(_dot(ds, k[kb]))
                        prs.append(pr.astype(BF))
                        dss.append(ds)
                    dk_b = _dot_tn(jnp.concatenate(dss, axis=0), jnp.concatenate([qm[0][qb], qm[1][qb]], axis=0))
                    dv_b = _dot_tn(jnp.concatenate(prs, axis=0), jnp.concatenate([dom[0][qb], dom[1][qb]], axis=0))
                    own = _attn_rows(start + b * (d * QBLK), d)
                    dq_ref[own, :] += jnp.where(lo, dq_parts[0], dq_parts[1]) * scale
                    if own_only:
                        dk_ref[own, :] += dk_b
                        dv_ref[own, :] += dv_b
                    elif b == 0:
                        dk_ref[prows, :] += dk_b[:QBLK]
                        dv_ref[prows, :] += dv_b[:QBLK]
                        dk_ref[own, :] += dk_b[QBLK:]
                        dv_ref[own, :] += dv_b[QBLK:]
                    else:
                        two = _attn_rows(start + (b - 1) * (d * QBLK), d, 2)
                        dk_ref[two, :] += dk_b
                        dv_ref[two, :] += dv_b

            lax.fori_loop(0, nblk // (group * segs), step, 0)

        @pl.when(h == pl.num_programs(0) - 1)
        def _():
            _Scatter(items, sems).wait()

    col = lambda base: pl.BlockSpec((T, LANES), lambda h: (0, base + h))
    outs = pl.pallas_call(
        body, grid=(4,),
        in_specs=[col(0), col(4), col(8), col(0), col(0), pl.BlockSpec((1, 8, LANES), lambda h: (h, 0, 0))]
        + [HBM] * n_s,
        out_specs=[pl.BlockSpec((3, T, LANES), lambda h: (0, 0, h), pipeline_mode=pl.Buffered(1))] + [HBM] * n_s,
        out_shape=[SDS((3, T, ATTN_W), F32)]
        + [SDS(s.shape if sl else (N_DEV,) + s.shape, s.dtype) for s, sl in zip(to_send, slabbed)],
        scratch_shapes=_scatter_sems(n_s) + [pltpu.VMEM((6, 2 * QBLK, 2 * QBLK), F32)],
        name="attn_bwd", compiler_params=_params(("arbitrary",), 60))(proj, proj, proj, do, stats, slopes, *to_send)
    return outs[0], outs[1:]


def _in_bwd(dparts, w_in_t, x, dh1, g1, tm=512):
    T = x.shape[0]
    n = len(dparts)
    w = ATTN_W

    def body(*refs):
        d_refs, (w_ref, x_ref, dh1_ref, g_ref, dx_ref, dg_ref) = refs[:n], refs[n:]
        _acc_init(pl.program_id(0), dg_ref)
        d_proj = jnp.concatenate([r[part].astype(BF) for r in d_refs for part in range(r.shape[0])], axis=1)
        da = _dot(d_proj, w_ref[...])
        dx, dg = _rms_bwd(da, x_ref[...], g_ref[...])
        dx_ref[...] = dh1_ref[...] + dx
        dg_ref[...] += _colsum(dg)

    tok = lambda c: pl.BlockSpec((tm, c), _row)
    vec = pl.BlockSpec((1, D_MODEL), _fixed)
    return pl.pallas_call(
        body, grid=(T // tm,),
        in_specs=[pl.BlockSpec((d.shape[0], tm, w), lambda i: (0, i, 0)) for d in dparts]
        + [pl.BlockSpec((PROJ, D_MODEL), _fixed), tok(D_MODEL), tok(D_MODEL), vec],
        out_specs=[tok(D_MODEL), vec],
        out_shape=[SDS((T, D_MODEL), F32), SDS((1, D_MODEL), F32)],
        name="in_bwd", compiler_params=_params(("arbitrary",), 52))(*dparts, w_in_t, x, dh1, g1)


def _sum_parts(p_ref):
    g = p_ref[0].astype(F32)
    for s in range(1, N_DEV):
        g = g + p_ref[s].astype(F32)
    return g


def _adamw_math(g, w, m, v):
    nm = ADAM_B1 * m + (1.0 - ADAM_B1) * g
    nv = ADAM_B2 * v + (1.0 - ADAM_B2) * (g * g)
    m_hat = nm / (1.0 - ADAM_B1 ** ADAM_STEP)
    v_hat = nv / (1.0 - ADAM_B2 ** ADAM_STEP)
    return -ADAM_LR * (m_hat / (jnp.sqrt(v_hat) + ADAM_EPS) + ADAM_WD * w), nm, nv


def _row_tile(rows):
    for t in (256, 176, 128, 80):
        if rows % t == 0:
            return t
    raise ValueError(rows)


def _reduce_adamw(parts, w, m, v, name):
    rows, width = w.shape
    tr = _row_tile(rows)

    def body(p_ref, w_ref, m_ref, v_ref, g_ref, d_ref, nm_ref, nv_ref):
        g = _sum_parts(p_ref)
        g_ref[...] = g
        d_ref[...], nm_ref[...], nv_ref[...] = _adamw_math(g, w_ref[...], m_ref[...], v_ref[...])

    blk = pl.BlockSpec((tr, width), _row)
    return pl.pallas_call(
        body, grid=(rows // tr,),
        in_specs=[pl.BlockSpec((N_DEV, tr, width), lambda i: (0, i, 0)), blk, blk, blk],
        out_specs=[blk] * 4, out_shape=[SDS((rows, width), F32)] * 4,
        name="adamw_" + name, compiler_params=_params(("arbitrary",), 32))(parts, w, m, v)


def _reduce(parts, name):
    _, rows, width = parts.shape
    tr = _row_tile(rows)

    def body(p_ref, g_ref):
        g_ref[...] = _sum_parts(p_ref)

    return pl.pallas_call(
        body, grid=(rows // tr,),
        in_specs=[pl.BlockSpec((N_DEV, tr, width), lambda i: (0, i, 0))],
        out_specs=pl.BlockSpec((tr, width), _row), out_shape=SDS((rows, width), F32),
        name="sum_" + name, compiler_params=_params(("arbitrary",), 32))(parts)


def _adamw(g, w, m, v, name):
    rows, width = w.shape
    tr = _row_tile(rows)

    def body(g_ref, w_ref, m_ref, v_ref, d_ref, nm_ref, nv_ref):
        d_ref[...], nm_ref[...], nv_ref[...] = _adamw_math(g_ref[...], w_ref[...], m_ref[...], v_ref[...])

    blk = pl.BlockSpec((tr, width), _row)
    return pl.pallas_call(
        body, grid=(rows // tr,), in_specs=[blk] * 4, out_specs=[blk] * 3, out_shape=[SDS((rows, width), F32)] * 3,
        name="adamw_" + name, compiler_params=_params(("arbitrary",), 32))(g, w, m, v)


SMALL = ("w_spatial", "ln_pre_mix", "ln_post_mix", "ln_pre_ffn", "ln_post_ffn", "b_pe_gate",
         "attn_out_norm", "sgu_out_norm", "b_spatial", "sgu_ln_g", "sgu_ln_b")
SMALL_GROUPS = ((128, ("w_spatial", "b_spatial", "sgu_ln_g", "sgu_ln_b")),
                (512, ("attn_out_norm", "sgu_out_norm")),
                (1024, ("ln_post_mix", "ln_pre_ffn", "ln_post_ffn", "b_pe_gate")))
SMALL_LATE = "ln_pre_mix"
SMALL_SIZE = dict(w_spatial=N_GROUPS * CHUNK * CHUNK, b_spatial=N_GROUPS * CHUNK, sgu_ln_g=GROUP_DIM, sgu_ln_b=GROUP_DIM,
                  attn_out_norm=ATTN_W, sgu_out_norm=SGU_W, ln_pre_mix=D_MODEL, ln_post_mix=D_MODEL, ln_pre_ffn=D_MODEL,
                  ln_post_ffn=D_MODEL, b_pe_gate=D_MODEL)
SUBLANES = 8
ROW_SHARDED = ("w_out", "w_down", "w_pe_gate")
COL_SHARDED = ("w_in", "w_gate_up", "w_pe_proj")
WEIGHTS = ("ln_pre_mix", "w_in", "sgu_ln_g", "sgu_ln_b", "w_spatial", "b_spatial", "attn_out_norm", "sgu_out_norm",
           "w_out", "ln_post_mix", "ln_pre_ffn", "w_gate_up", "w_down", "ln_post_ffn", "w_pe_gate", "b_pe_gate",
           "w_pe_proj")


def _group_rows(width, names, extra=0):
    rows = sum(SMALL_SIZE[n] // width for n in names) + extra
    return -(-rows // SUBLANES) * SUBLANES


def _pack_small_grads(gs, loss_term):
    packed = []
    for width, names in SMALL_GROUPS:
        rows = [gs[n].reshape(-1, width) for n in names]
        extra = int(width == D_MODEL)
        if extra:
            rows.append(jnp.full((1, width), loss_term, F32))
        used = sum(r.shape[0] for r in rows)
        rows.append(jnp.zeros((_group_rows(width, names, extra) - used, width), F32))
        packed.append(jnp.concatenate(rows, axis=0))
    return packed


def _small_adamw(arrived, arrived_late, w, m, v):
    names = [n for _, ns in SMALL_GROUPS for n in ns] + [SMALL_LATE]
    n_groups = len(SMALL_GROUPS)

    def body(*refs):
        group_refs, late_ref = refs[:n_groups], refs[n_groups]
        state = refs[n_groups + 1:n_groups + 1 + 3 * len(names)]
        outs = refs[n_groups + 1 + 3 * len(names):]
        sums = [_sum_parts(r) for r in group_refs]

        def update(name, g):
            i = names.index(name)
            w_ref, m_ref, v_ref = state[3 * i:3 * i + 3]
            delta, nm, nv = _adamw_math(g, w_ref[...].reshape(g.shape), m_ref[...].reshape(g.shape),
                                        v_ref[...].reshape(g.shape))
            for o_ref, val in zip(outs[4 * i:4 * i + 4], (g, delta, nm, nv)):
                o_ref[...] = val.reshape(o_ref.shape)

        for (width, group), total in zip(SMALL_GROUPS, sums):
            row = 0
            for name in group:
                rows = SMALL_SIZE[name] // width
                update(name, total[row:row + rows, :])
                row += rows
            if width == D_MODEL:
                outs[-1][...] = total[row:row + 1, :LANES]
        update(SMALL_LATE, _sum_parts(late_ref)[:1, :])

    state = [t[n] for n in names for t in (w, m, v)]
    plain = jax.ShapeDtypeStruct
    out_shape = [plain(w[n].shape, F32) for n in names for _ in range(4)] + [plain((1, LANES), F32)]
    outs = pl.pallas_call(body, out_shape=out_shape, name="adamw_small",
                          compiler_params=pltpu.CompilerParams(vmem_limit_bytes=32 * MIB))(*arrived, arrived_late, *state)
    return {n: outs[4 * i:4 * i + 4] for i, n in enumerate(names)}, outs[-1]


def _slabs(full):
    return full.reshape(N_DEV, full.shape[0] // N_DEV, full.shape[1])


def kernel(x, p, ln_pre_mix, w_in, sgu_ln_g, sgu_ln_b, w_spatial, b_spatial, attn_out_norm, sgu_out_norm, w_out, ln_post_mix, ln_pre_ffn, w_gate_up, w_down, ln_post_ffn, w_pe_gate, b_pe_gate, w_pe_proj, loss_target, m_ln_pre_mix, m_w_in, m_sgu_ln_g, m_sgu_ln_b, m_w_spatial, m_b_spatial, m_attn_out_norm, m_sgu_out_norm, m_w_out, m_ln_post_mix, m_ln_pre_ffn, m_w_gate_up, m_w_down, m_ln_post_ffn, m_w_pe_gate, m_b_pe_gate, m_w_pe_proj, v_ln_pre_mix, v_w_in, v_sgu_ln_g, v_sgu_ln_b, v_w_spatial, v_b_spatial, v_attn_out_norm, v_sgu_out_norm, v_w_out, v_ln_post_mix, v_ln_pre_ffn, v_w_gate_up, v_w_down, v_ln_post_ffn, v_w_pe_gate, v_b_pe_gate, v_w_pe_proj):
    given = dict(locals())
    w = {n: given[n] for n in WEIGHTS}
    m = {n: given["m_" + n] for n in WEIGHTS}
    v = {n: given["v_" + n] for n in WEIGHTS}
    xs, ps, target = x[0], p[0, 0], loss_target[0]

    shard = {n: w[n][0].astype(BF) for n in ROW_SHARDED}
    shard.update({n: w[n][0].T.astype(BF) for n in COL_SHARDED})
    sm = {n: w[n][0] for n in SMALL}
    sm = {n: (a.reshape(1, -1) if a.ndim == 1 else a) for n, a in sm.items()}
    slopes = jnp.broadcast_to((2.0 ** -(jnp.arange(8, dtype=F32) + 1.0)).reshape(4, 2, 1), (4, 2, LANES))
    slopes = jnp.concatenate([slopes, jnp.zeros((4, 6, LANES), F32)], axis=1)
    b_st = jnp.pad(sm["b_spatial"].T, ((0, 0), (0, LANES - N_GROUPS)))

    def full(gathered):
        return gathered.reshape(-1, gathered.shape[-1])

    w_in_t = full(_all_gather(shard["w_in"], "gather_w_in"))
    proj, a = _in_proj(xs, sm["ln_pre_mix"], w_in_t)
    later = ("w_out", "w_gate_up", "w_down", "w_pe_gate", "w_pe_proj")
    attn, lse, gathered = _attn_fwd(proj, slopes, [shard[n] for n in later])
    w_out_f, w_gu_t, w_down_f, w_peg_f, w_pep_t = [full(g) for g in gathered]
    sgu, groups, mixed, h1, f = _out_proj(attn, proj, xs, sm["sgu_ln_g"], sm["sgu_ln_b"], sm["w_spatial"], b_st,
                                          sm["attn_out_norm"], sm["sgu_out_norm"], w_out_f,
                                          sm["ln_post_mix"], sm["ln_pre_ffn"])
    g, u, act = _gate_up(f, w_gu_t)
    y, h2 = _down_proj(act, w_down_f, h1, sm["ln_post_ffn"])
    dh2, dy, g_peg, g_pep_t, loss_cols, db_peg, d_pff = _pe_loss_and_bwd(
        h2, ps, target, y, w_peg_f, sm["b_pe_gate"], w_pep_t, sm["ln_post_ffn"])
    loss_term = 0.5 * jnp.sum(loss_cols) * (1.0 / D_MODEL)

    arrived = {}
    g_down = _weight_grad(act, dy, "w_down")
    dg, du, (arrived["w_pe_proj"], arrived["w_pe_gate"]) = _down_bwd(dy, w_down_f, g, u, [_slabs(g_pep_t), _slabs(g_peg)])
    dh1, dmx, d_pf, d_pm, (arrived["w_down"],) = _ffn_in_bwd(dg, du, w_gu_t, h1, dh2, mixed, sm["ln_pre_ffn"],
                                                            sm["ln_post_mix"], [_slabs(g_down)])
    dattn, stats, dsgu, d_ga, d_gs, g_gu_t = _out_bwd(dmx, w_out_f, attn, lse, sgu, sm["attn_out_norm"],
                                                      sm["sgu_out_norm"], dg, f)
    duz, d_ws, d_bst, d_lg, d_lb, g_out, g_gu_t = _sgu_bwd(proj, dsgu, sm["sgu_ln_g"], sm["sgu_ln_b"],
                                                           sm["w_spatial"], b_st, groups, dmx, du, f, g_gu_t)
    gs = dict(sgu_ln_g=d_lg, sgu_ln_b=d_lb, w_spatial=d_ws, b_spatial=d_bst[:, :N_GROUPS].T, attn_out_norm=d_ga,
              sgu_out_norm=d_gs, ln_post_mix=d_pm, ln_pre_ffn=d_pf, ln_post_ffn=d_pff, b_pe_gate=db_peg)
    small_grads = _pack_small_grads(gs, loss_term)
    dqkv, (arrived["w_gate_up"], arrived["w_out"], *arrived_small) = _attn_bwd(
        proj, dattn, stats, slopes, [_slabs(g_gu_t), _slabs(g_out), *small_grads],
        [True, True] + [False] * len(small_grads))
    send_sems, recv_sems, slabs, landing, token = _scatter_begin(_slabs(_grad_w_in([dqkv, duz], a)), "w_in_grad_send")
    grad_x, d_g1 = _in_bwd([dqkv, duz], w_in_t, xs, dh1, sm["ln_pre_mix"] + token[:1, :1])
    slabs, landing = _scatter_end(send_sems, recv_sems, slabs, landing, d_g1, "w_in_grad_arrive")
    me = 4 * lax.axis_index("x") + 2 * lax.axis_index("y") + lax.axis_index("c")
    own = lax.dynamic_slice_in_dim(slabs, me, 1, axis=0)
    arrived["w_in"] = lax.dynamic_update_slice_in_dim(landing, own, me, axis=0)
    (arrived_late,) = _scatter_call([jnp.pad(d_g1, ((0, SUBLANES - 1), (0, 0)))], [False], "ln_pre_mix_grad_exchange")

    res = {}
    for n in ROW_SHARDED:
        res[n] = _reduce_adamw(arrived[n], w[n][0], m[n][0], v[n][0], n)
    for n in ("w_in", "w_gate_up"):
        res[n] = [t.T for t in _reduce_adamw(arrived[n], w[n][0].T, m[n][0].T, v[n][0].T, n)]
    for n in ("w_pe_proj",):
        grad = _reduce(arrived[n], n).T
        res[n] = (grad, *_adamw(grad, w[n][0], m[n][0], v[n][0], n))
    small, loss_row = _small_adamw(arrived_small, arrived_late, w, m, v)

    out = []
    for k in range(4):
        out += [res[n][k][None] if n in res else small[n][k] for n in WEIGHTS]
    return (loss_row[0, 0], grad_x[None], *out)
```

```python
import math

import jax
import jax.numpy as jnp
from jax import lax
from jax.experimental import pallas as pl
from jax.experimental.pallas import tpu as pltpu

F32 = jnp.float32
BF = jnp.bfloat16


def SDS(shape, dtype):
    return pltpu.HBM(tuple(shape), dtype)

D_MODEL = 1024
ATTN_W = 512
SGU_W = 512
HEAD_DIM = 64
N_GROUPS = 4
GROUP_DIM = 128
CHUNK = 128
D_FF = 2816
PLE_DIM = 256
PROJ = 3 * ATTN_W + 2 * SGU_W
DILATIONS = (1, 4, 16)
QBLK = 128
EPS = 1e-6
NEG = -1e30
N_DEV = 8
LANES = 128

ADAM_LR = 0.001
ADAM_B1 = 0.9
ADAM_B2 = 0.999
ADAM_EPS = 1e-08
ADAM_WD = 0.01
ADAM_STEP = 10

MIB = 2 ** 20
MESH_ID = pl.DeviceIdType.MESH
HBM = pl.BlockSpec(memory_space=pl.ANY)


def _params(sem, vmem_mib):
    return pltpu.CompilerParams(dimension_semantics=sem, vmem_limit_bytes=vmem_mib * MIB)


def _dot(a, b):
    return jnp.dot(a, b, preferred_element_type=F32)


def _dot_nt(a, b):
    return lax.dot_general(a, b, (((1,), (1,)), ((), ())), preferred_element_type=F32)


def _dot_tn(a, b):
    return lax.dot_general(a, b, (((0,), (0,)), ((), ())), preferred_element_type=F32)


def _rstd(x):
    return lax.rsqrt(jnp.mean(x * x, axis=-1, keepdims=True) + EPS)


def _rms_bwd(dy, x, g):
    r = _rstd(x)
    n = x * r
    dn = dy * g
    dx = r * (dn - n * jnp.mean(dn * n, axis=-1, keepdims=True))
    return dx, dy * n


def _colsum(v):
    return jnp.sum(v, axis=0, keepdims=True)


_G0 = math.sqrt(2.0 / math.pi)
_G1 = 0.044715


def _gelu(x):
    t = jnp.tanh(_G0 * (x + _G1 * x * x * x))
    return 0.5 * x * (1.0 + t), t


def _gelu_grad(x, t):
    return 0.5 * (1.0 + t) + 0.5 * x * (1.0 - t * t) * (_G0 * (1.0 + 3.0 * _G1 * x * x))


def _sigmoid(x):
    return 0.5 * jnp.tanh(0.5 * x) + 0.5


def _row(i):
    return (i, 0)


def _fixed(i):
    return (0, 0)


def _acc_init(step, *refs):
    @pl.when(step == 0)
    def _():
        for r in refs:
            r[...] = jnp.zeros_like(r)


FLIPS = [(dx, dy, dc) for dx in (0, 1) for dy in (0, 1) for dc in (0, 1)][1:]
DMA_SEMS = pltpu.SemaphoreType.DMA


def _mesh_pos():
    return lax.axis_index("x"), lax.axis_index("y"), lax.axis_index("c")


def _remote(src, dst, sems, n, to):
    return pltpu.make_async_remote_copy(src_ref=src, dst_ref=dst, send_sem=sems[0].at[n], recv_sem=sems[1].at[n],
                                        device_id=to, device_id_type=MESH_ID)


class _Scatter:
    def __init__(self, items, sems):
        x, y, c = _mesh_pos()
        me = 4 * x + 2 * y + c
        self.local, self.sends, self.arrivals = [], [], []
        for i, (src, dst, slabbed) in enumerate(items):
            self.local.append(pltpu.make_async_copy(src.at[me] if slabbed else src, dst.at[me], sems[2].at[i]))
            for k, (dx, dy, dc) in enumerate(FLIPS):
                to = (1 - x if dx else x, 1 - y if dy else y, 1 - c if dc else c)
                peer = 4 * to[0] + 2 * to[1] + to[2]
                out = src.at[peer] if slabbed else src
                self.sends.append(_remote(out, dst.at[me], sems, 7 * i + k, to))
                self.arrivals.append(_remote(out, dst.at[peer], sems, 7 * i + k, to))

    def start(self):
        for cp in self.local + self.sends:
            cp.start()

    def wait(self):
        for cp in self.arrivals:
            cp.wait_recv()
        for cp in self.sends:
            cp.wait_send()
        for cp in self.local:
            cp.wait()


def _scatter_sems(n):
    return [DMA_SEMS((7 * n,)), DMA_SEMS((7 * n,)), DMA_SEMS((n,))]


class _Gather:
    def __init__(self, items, sems):
        x, y, c = _mesh_pos()
        me, sibling = (x, y, c), (x, y, 1 - c)
        chips = [(1 - x, y), (x, 1 - y), (1 - x, 1 - y)]
        self.first, self.passed, self.from_chips, self.rest, self.local = [], [], [], [], []
        for i, (src, dst) in enumerate(items):
            def slot(p, dst=dst):
                return dst.at[4 * p[0] + 2 * p[1] + p[2]]

            def copy(k, block, to, own=False, i=i, src=src, slot=slot):
                return _remote(src if own else slot(block), slot(block), sems, 7 * i + k, to)

            self.local.append(pltpu.make_async_copy(src, slot(me), sems[2].at[i]))
            self.first.append(copy(0, me, sibling, own=True))
            self.first += [copy(1 + j, me, (*chip, c), own=True) for j, chip in enumerate(chips)]
            self.passed += [copy(4 + j, (*chip, c), sibling) for j, chip in enumerate(chips)]
            self.from_chips += [copy(1 + j, (*chip, c), me) for j, chip in enumerate(chips)]
            self.rest.append(copy(0, sibling, me))
            self.rest += [copy(4 + j, (*chip, 1 - c), me) for j, chip in enumerate(chips)]

    def start(self):
        for cp in self.local + self.first:
            cp.start()

    def forward(self):
        for arrived, onward in zip(self.from_chips, self.passed):
            arrived.wait_recv()
            onward.start()

    def finish(self):
        for cp in self.rest:
            cp.wait_recv()
        for cp in self.first + self.passed:
            cp.wait_send()
        for cp in self.local:
            cp.wait()


def _all_gather(shard, name):
    def body(x_ref, out_ref, *sems):
        g = _Gather([(x_ref, out_ref)], sems)
        g.start()
        g.forward()
        g.finish()

    return pl.pallas_call(
        body, out_shape=SDS((N_DEV,) + shard.shape, shard.dtype), in_specs=[HBM], out_specs=HBM,
        scratch_shapes=_scatter_sems(1), name=name)(shard)


def _scatter_call(srcs, slabbed, name):
    n = len(srcs)

    def body(*refs):
        sc = _Scatter(list(zip(refs[:n], refs[n:2 * n], slabbed)), refs[2 * n:])
        sc.start()
        sc.wait()

    shapes = [SDS(s.shape if sl else (N_DEV,) + s.shape, s.dtype) for s, sl in zip(srcs, slabbed)]
    return pl.pallas_call(body, out_shape=shapes, in_specs=[HBM] * n, out_specs=[HBM] * n,
                          scratch_shapes=_scatter_sems(n), name=name)(*srcs)


SEM = pl.BlockSpec(memory_space=pltpu.SEMAPHORE)
N_PEERS = len(FLIPS)


def _slab_copies(src_ref, land_ref, send_sems, recv_sems):
    x, y, c = _mesh_pos()
    me = 4 * x + 2 * y + c
    copies = []
    for k, (dx, dy, dc) in enumerate(FLIPS):
        to = (1 - x if dx else x, 1 - y if dy else y, 1 - c if dc else c)
        peer = 4 * to[0] + 2 * to[1] + to[2]
        sems = (send_sems, recv_sems)
        copies.append((_remote(src_ref.at[peer], land_ref.at[me], sems, k, to),
                       _remote(src_ref.at[peer], land_ref.at[peer], sems, k, to)))
    return copies


def _scatter_begin(src, name):
    def body(src_ref, land_ref, send_sems, recv_sems, src_thru, land_thru, token):
        for send, _ in _slab_copies(src_ref, land_ref, send_sems, recv_sems):
            send.start()
        token[...] = jnp.zeros_like(token)

    landing = lax.empty(src.shape, src.dtype)
    return pl.pallas_call(
        body, name=name,
        out_shape=(pltpu.SemaphoreType.DMA((N_PEERS,)), pltpu.SemaphoreType.DMA((N_PEERS,)),
                   pltpu.HBM(src.shape, src.dtype), pltpu.HBM(src.shape, src.dtype),
                   jax.ShapeDtypeStruct((SUBLANES, LANES), F32)),
        in_specs=(HBM, HBM), out_specs=(SEM, SEM, HBM, HBM, pl.BlockSpec(memory_space=pltpu.VMEM)),
        input_output_aliases={0: 2, 1: 3},
        compiler_params=pltpu.CompilerParams(has_side_effects=pltpu.SideEffectType.DATAFLOW_SIDE_EFFECTING))(
            pltpu.with_memory_space_constraint(src, pltpu.HBM), pltpu.with_memory_space_constraint(landing, pltpu.HBM))


def _scatter_end(send_sems, recv_sems, src_thru, land_thru, after, name):
    def body(src_ref, land_ref, send_sems, recv_sems, after_ref, src_dead, land_out):
        for send, arrival in _slab_copies(src_ref, land_ref, send_sems, recv_sems):
            send.wait_send()
            arrival.wait_recv()

    return pl.pallas_call(
        body, name=name,
        out_shape=(pltpu.HBM(src_thru.shape, src_thru.dtype), pltpu.HBM(land_thru.shape, land_thru.dtype)),
        in_specs=(HBM, HBM, SEM, SEM, HBM), out_specs=(HBM, HBM), input_output_aliases={0: 0, 1: 1},
        compiler_params=pltpu.CompilerParams(has_side_effects=pltpu.SideEffectType.DATAFLOW_SIDE_EFFECTING))(
            src_thru, land_thru, send_sems, recv_sems, after)


def _in_proj(x, g1, w_in_t, tm=512):
    T = x.shape[0]

    def body(x_ref, g_ref, w_ref, proj_ref, a_ref):
        xv = x_ref[...]
        a = (xv * _rstd(xv) * g_ref[...]).astype(BF)
        a_ref[...] = a
        proj_ref[...] = _dot_nt(a, w_ref[...])

    return pl.pallas_call(
        body, grid=(T // tm,),
        in_specs=[pl.BlockSpec((tm, D_MODEL), _row), pl.BlockSpec((1, D_MODEL), _fixed),
                  pl.BlockSpec((PROJ, D_MODEL), _fixed)],
        out_specs=[pl.BlockSpec((tm, PROJ), _row), pl.BlockSpec((tm, D_MODEL), _row)],
        out_shape=[SDS((T, PROJ), F32), SDS((T, D_MODEL), BF)],
        name="in_proj", compiler_params=_params(("arbitrary",), 48))(x, g1, w_in_t)


ATTN_GROUP = 16


def _attn_bias(sl_ref, bias_ref):
    qi = lax.broadcasted_iota(jnp.int32, (QBLK, QBLK), 0)
    kj = lax.broadcasted_iota(jnp.int32, (QBLK, QBLK), 1)
    step = qi - kj
    for di, d in enumerate(DILATIONS):
        for j in range(2):
            sl = sl_ref[0, j:j + 1, :]
            cur = jnp.where(step >= 0, -sl * (step * d).astype(F32), NEG)
            prev = jnp.where(step <= 0, -sl * ((step + QBLK) * d).astype(F32), NEG)
            rows = slice(j * QBLK, (j + 1) * QBLK)
            bias_ref[di * 2, rows, :QBLK] = prev
            bias_ref[di * 2, rows, QBLK:] = cur
            bias_ref[di * 2 + 1, rows, :QBLK] = jnp.full((QBLK, QBLK), NEG, F32)
            bias_ref[di * 2 + 1, rows, QBLK:] = cur


def _stack_heads(x, lo):
    return jnp.concatenate([jnp.where(lo, x, 0.0), jnp.where(lo, 0.0, x)], axis=0).astype(BF)


def _unstack_heads(x, lo):
    return jnp.where(lo, x[:QBLK], x[QBLK:])


def _attn_rows(start, d, blocks=1):
    if d == 1:
        return pl.ds(pl.multiple_of(start, QBLK), blocks * QBLK)
    return pl.ds(start, blocks * QBLK, stride=d)


def _attn_group_index(i, nblk, d, group):
    per = nblk // d // group
    r = i // per
    n0 = (i % per) * group
    start = r + (d * QBLK) * n0
    pstart = jnp.maximum(start - d * QBLK, r)
    return start, pstart, n0 == 0


def _attn_plan(nblk, d):
    group = min(ATTN_GROUP, nblk // d)
    return group, max(1, min(ATTN_GROUP // group, d))


def _attn_fwd(proj, slopes, to_gather):
    T = proj.shape[0]
    nblk = T // QBLK
    n_g = len(to_gather)

    def body(q_ref, k_ref, v_ref, sl_ref, *rest):
        srcs, (o_ref, m_ref), dsts = rest[:n_g], rest[n_g:n_g + 2], rest[n_g + 2:2 * n_g + 2]
        sems, (l_ref, bias_ref) = rest[2 * n_g + 2:2 * n_g + 5], rest[2 * n_g + 5:]
        h = pl.program_id(0)

        @pl.when(h == 0)
        def _():
            _Gather(list(zip(srcs, dsts)), sems).start()

        @pl.when(h == pl.num_programs(0) - 1)
        def _():
            _Gather(list(zip(srcs, dsts)), sems).forward()

        _attn_bias(sl_ref, bias_ref)
        lo = lax.broadcasted_iota(jnp.int32, (1, LANES), 1) < HEAD_DIM

        order = list(enumerate(DILATIONS))[::-1]
        for di, d in order:
            group, segs = _attn_plan(nblk, d)
            fresh, last = di == order[0][0], di == order[-1][0]

            def step(i, carry, segs=segs, **kw):
                for s in range(segs):
                    segment(i * segs + s, **kw)
                return carry

            def segment(i, d=d, di=di, group=group, fresh=fresh, last=last):
                start, pstart, first = _attn_group_index(i, nblk, d, group)
                prows = _attn_rows(pstart, d)
                alone = group == nblk // d
                k_prev = v_prev = None
                if not alone:
                    k_prev, v_prev = k_ref[prows, :].astype(BF), v_ref[prows, :].astype(BF)
                for b in range(group):
                    out = _attn_rows(start + b * (d * QBLK), d)
                    k_own, v_own = k_ref[out, :].astype(BF), v_ref[out, :].astype(BF)
                    if alone and b == 0:
                        k2, v2, bias = k_own, v_own, bias_ref[di * 2, :, QBLK:]
                    else:
                        k2, v2 = jnp.concatenate([k_prev, k_own], axis=0), jnp.concatenate([v_prev, v_own], axis=0)
                        bias = bias_ref[di * 2 + first.astype(jnp.int32)] if b == 0 else bias_ref[di * 2]
                    k_prev, v_prev = k_own, v_own
                    s = _dot_nt(_stack_heads(q_ref[out, :] * (HEAD_DIM ** -0.5), lo), k2) + bias
                    m = jnp.max(s, axis=-1, keepdims=True)
                    pr = jnp.exp(s - m)
                    m_b = _unstack_heads(m, lo)
                    l_b = _unstack_heads(jnp.sum(pr, axis=-1, keepdims=True), lo)
                    o_b = _unstack_heads(_dot(pr.astype(BF), v2), lo)
                    if fresh:
                        m_ref[out, :] = m_b
                        l_ref[out, :] = l_b
                        o_ref[out, :] = o_b
                        continue
                    m_o = m_ref[out, :]
                    m_n = jnp.maximum(m_o, m_b)
                    wa, wb = jnp.exp(m_o - m_n), jnp.exp(m_b - m_n)
                    l_n = wa * l_ref[out, :] + wb * l_b
                    o_n = wa * o_ref[out, :] + wb * o_b
                    if last:
                        m_ref[out, :] = m_n + jnp.log(l_n)
                        o_ref[out, :] = o_n / l_n
                    else:
                        m_ref[out, :] = m_n
                        l_ref[out, :] = l_n
                        o_ref[out, :] = o_n

            lax.fori_loop(0, nblk // (group * segs), step, 0)

        @pl.when(h == pl.num_programs(0) - 1)
        def _():
            _Gather(list(zip(srcs, dsts)), sems).finish()

    col = lambda base: pl.BlockSpec((T, LANES), lambda h: (0, base + h))
    tok = pl.BlockSpec((T, LANES), lambda h: (0, h))
    outs = pl.pallas_call(
        body, grid=(4,),
        in_specs=[col(0), col(4), col(8), pl.BlockSpec((1, 8, LANES), lambda h: (h, 0, 0))] + [HBM] * n_g,
        out_specs=[tok, tok] + [HBM] * n_g,
        out_shape=[SDS((T, ATTN_W), F32), SDS((T, ATTN_W), F32)]
        + [SDS((N_DEV,) + g.shape, g.dtype) for g in to_gather],
        scratch_shapes=_scatter_sems(n_g) + [pltpu.VMEM((T, LANES), F32), pltpu.VMEM((6, 2 * QBLK, 2 * QBLK), F32)],
        name="attn_fwd", compiler_params=_params(("arbitrary",), 56))(proj, proj, proj, slopes, *to_gather)
    return outs[0], outs[1], outs[2:]


def _sgu_norm(zv, ln_g, ln_b):
    gz, tz = _gelu(zv)
    mu = jnp.mean(gz, axis=-1, keepdims=True)
    xc = gz - mu
    rs = lax.rsqrt(jnp.mean(xc * xc, axis=-1, keepdims=True) + EPS)
    xhat = xc * rs
    return xhat * ln_g + ln_b, xhat, rs, tz


def _causal(w):
    i = lax.broadcasted_iota(jnp.int32, (CHUNK, CHUNK), 0)
    j = lax.broadcasted_iota(jnp.int32, (CHUNK, CHUNK), 1)
    return jnp.where(i >= j, w, 0.0)


def _sgu_fwd_tile(u_ref, z_ref, g_ref, b_ref, w_ref, bs_ref, out_ref):
    for g in range(N_GROUPS):
        wm = _causal(w_ref[g]).astype(BF)
        cols = slice(g * GROUP_DIM, (g + 1) * GROUP_DIM)
        for c in range(u_ref.shape[0] // CHUNK):
            rows = slice(c * CHUNK, (c + 1) * CHUNK)
            zn, _, _, _ = _sgu_norm(z_ref[rows, cols], g_ref[...], b_ref[...])
            mixed = _dot(wm, zn.astype(BF)) + bs_ref[:, g:g + 1]
            gu, _ = _gelu(u_ref[rows, cols])
            out_ref[rows, cols] = gu * mixed


def _out_proj(attn, proj, x, ln_g, ln_b, w_s, b_st, g_a, g_s, w_out, g_pm, g_pf, tm=512):
    T = x.shape[0]

    def body(a_ref, u_ref, z_ref, x_ref, lg_ref, lb_ref, ws_ref, bs_ref, ga_ref, gs_ref, w_ref, gpm_ref, gpf_ref,
             s_ref, grp_ref, mixed_ref, h1_ref, f_ref):
        _sgu_fwd_tile(u_ref, z_ref, lg_ref, lb_ref, ws_ref, bs_ref, s_ref)
        av, sv = a_ref[...], s_ref[...]
        an = (av * _rstd(av) * ga_ref[...]).astype(BF)
        sn = (sv * _rstd(sv) * gs_ref[...]).astype(BF)
        grp_ref[:, :ATTN_W] = an
        grp_ref[:, ATTN_W:] = sn
        mixed = _dot(an, w_ref[:ATTN_W, :]) + _dot(sn, w_ref[ATTN_W:, :])
        mixed_ref[...] = mixed
        h1 = x_ref[...] + mixed * _rstd(mixed) * gpm_ref[...]
        h1_ref[...] = h1
        f_ref[...] = (h1 * _rstd(h1) * gpf_ref[...]).astype(BF)

    tok = lambda w: pl.BlockSpec((tm, w), _row)
    vec = lambda w: pl.BlockSpec((1, w), _fixed)
    return pl.pallas_call(
        body, grid=(T // tm,),
        in_specs=[tok(ATTN_W), pl.BlockSpec((tm, SGU_W), lambda i: (i, 3)), pl.BlockSpec((tm, SGU_W), lambda i: (i, 4)),
                  tok(D_MODEL), vec(GROUP_DIM), vec(GROUP_DIM), pl.BlockSpec((N_GROUPS, CHUNK, CHUNK), lambda i: (0, 0, 0)),
                  pl.BlockSpec((CHUNK, LANES), _fixed), vec(ATTN_W), vec(SGU_W),
                  pl.BlockSpec((D_MODEL, D_MODEL), _fixed), vec(D_MODEL), vec(D_MODEL)],
        out_specs=[tok(SGU_W)] + [tok(D_MODEL)] * 4,
        out_shape=[SDS((T, SGU_W), F32), SDS((T, D_MODEL), BF), SDS((T, D_MODEL), F32), SDS((T, D_MODEL), F32),
                   SDS((T, D_MODEL), BF)],
        name="out_proj", compiler_params=_params(("arbitrary",), 52))(
            attn, proj, proj, x, ln_g, ln_b, w_s, b_st, g_a, g_s, w_out, g_pm, g_pf)


FF_TILE = 1408
FF_TILES = D_FF // FF_TILE
FF_CHUNK = 256
FETCH_DEPTH = 3


def _gate_up(f, w_gu_t, tm=512):
    T = f.shape[0]
    tn = FF_TILE

    def body(f_ref, wg_ref, wu_ref, g_ref, u_ref, act_ref):
        fv = f_ref[...]
        g = _dot_nt(fv, wg_ref[...])
        u = _dot_nt(fv, wu_ref[...])
        g_ref[...] = g.astype(BF)
        u_ref[...] = u.astype(BF)
        act_ref[...] = (g * _sigmoid(g) * u).astype(BF)

    ospec = pl.BlockSpec((tm, tn), lambda j, i: (i, j))
    return pl.pallas_call(
        body, grid=(FF_TILES, T // tm),
        in_specs=[pl.BlockSpec((tm, D_MODEL), lambda j, i: (i, 0)), pl.BlockSpec((tn, D_MODEL), lambda j, i: (j, 0)),
                  pl.BlockSpec((tn, D_MODEL), lambda j, i: (j + FF_TILES, 0))],
        out_specs=[ospec] * 3, out_shape=[SDS((T, D_FF), BF)] * 3,
        name="gate_up", compiler_params=_params(("arbitrary", "arbitrary"), 40))(f, w_gu_t, w_gu_t)


def _down_proj(act, w_down, h1, g_pff, tm=512):
    T = act.shape[0]

    def body(a_ref, w_ref, h1_ref, g_ref, y_ref, h2_ref):
        y = _dot(a_ref[...], w_ref[...])
        y_ref[...] = y
        h2_ref[...] = h1_ref[...] + y * _rstd(y) * g_ref[...]

    return pl.pallas_call(
        body, grid=(T // tm,),
        in_specs=[pl.BlockSpec((tm, D_FF), _row), pl.BlockSpec((D_FF, D_MODEL), _fixed),
                  pl.BlockSpec((tm, D_MODEL), _row), pl.BlockSpec((1, D_MODEL), _fixed)],
        out_specs=[pl.BlockSpec((tm, D_MODEL), _row)] * 2,
        out_shape=[SDS((T, D_MODEL), F32)] * 2,
        name="down_proj", compiler_params=_params(("arbitrary",), 48))(act, w_down, h1, g_pff)


def _pe_loss_and_bwd(h2, p, target, y, w_peg, b_peg, w_pep_t, g_pff, tm=512):
    T = h2.shape[0]

    def body(h2_ref, p_ref, t_ref, y_ref, wg_ref, b_ref, wp_ref, g_ref,
             dh2_ref, dy_ref, gpeg_ref, gpep_ref, loss_ref, db_ref, dg_ref):
        _acc_init(pl.program_id(0), gpeg_ref, gpep_ref, loss_ref, db_ref, dg_ref)
        h2v = h2_ref[...]
        h2b = h2v.astype(BF)
        pb = p_ref[...].astype(BF)
        gate = _sigmoid(_dot(h2b, wg_ref[...]) + b_ref[...])
        pp = _dot_nt(pb, wp_ref[...])
        diff = h2v + gate * pp - t_ref[...]
        loss_ref[...] += _colsum(diff * diff)
        dh3 = diff * (1.0 / D_MODEL)
        dpre = dh3 * pp * (gate * (1.0 - gate))
        dpre_b = dpre.astype(BF)
        db_ref[...] += _colsum(dpre)
        gpeg_ref[...] += _dot_tn(h2b, dpre_b)
        gpep_ref[...] += _dot_tn((dh3 * gate).astype(BF), pb)
        dh2 = dh3 + _dot_nt(dpre_b, wg_ref[...])
        dh2_ref[...] = dh2
        dy, dg = _rms_bwd(dh2, y_ref[...], g_ref[...])
        dy_ref[...] = dy.astype(BF)
        dg_ref[...] += _colsum(dg)

    tok = lambda w: pl.BlockSpec((tm, w), _row)
    vec = pl.BlockSpec((1, D_MODEL), _fixed)
    wg = pl.BlockSpec((D_MODEL, D_MODEL), _fixed)
    wp = pl.BlockSpec((D_MODEL, PLE_DIM), _fixed)
    return pl.pallas_call(
        body, grid=(T // tm,),
        in_specs=[tok(D_MODEL), tok(PLE_DIM), tok(D_MODEL), tok(D_MODEL), wg, vec, wp, vec],
        out_specs=[tok(D_MODEL), tok(D_MODEL), wg, wp, vec, vec, vec],
        out_shape=[SDS((T, D_MODEL), F32), SDS((T, D_MODEL), BF), SDS((D_MODEL, D_MODEL), F32),
                   SDS((D_MODEL, PLE_DIM), F32)] + [SDS((1, D_MODEL), F32)] * 3,
        name="pe_loss_and_bwd", compiler_params=_params(("arbitrary",), 56))(
            h2, p, target, y, w_peg, b_peg, w_pep_t, g_pff)


def _weight_grad(a, dy, name, into=None, row_tile=0, rows=None, tk=1024):
    n = dy.shape[1]
    tn = min(n, 1024)
    T, ka = a.shape
    tka = FF_TILE if ka == D_FF else min(ka, 1024)
    rows = ka if rows is None else rows

    def body(a_ref, dy_ref, *rest):
        out_ref = rest[-1]
        _acc_init(pl.program_id(2), out_ref)
        out_ref[...] += _dot_tn(a_ref[...].astype(BF), dy_ref[...].astype(BF))

    carried = [] if into is None else [into]
    return pl.pallas_call(
        body, grid=(ka // tka, n // tn, T // tk),
        in_specs=[pl.BlockSpec((tk, tka), lambda i, j, k: (k, i)), pl.BlockSpec((tk, tn), lambda i, j, k: (k, j))]
        + [HBM] * len(carried),
        out_specs=pl.BlockSpec((tka, tn), lambda i, j, k: (i + row_tile, j)),
        out_shape=SDS((rows, n), F32), input_output_aliases={2: 0} if carried else {},
        name="grad_" + name, compiler_params=_params(("arbitrary",) * 3, 40))(a, dy, *carried)


def _grad_w_in(dparts, a, tk=1024):
    T = a.shape[0]

    def body(*refs):
        d_refs, a_ref, out_ref, acc_ref = refs[:len(dparts)], refs[-3], refs[-2], refs[-1]
        k = pl.program_id(0)
        _acc_init(k, acc_ref)
        cols = [r[part].astype(BF) for r in d_refs for part in range(r.shape[0])]
        acc_ref[...] += _dot_tn(jnp.concatenate(cols, axis=1), a_ref[...])

        @pl.when(k == pl.num_programs(0) - 1)
        def _():
            out_ref[...] = acc_ref[...].astype(BF)

    return pl.pallas_call(
        body, grid=(T // tk,),
        in_specs=[pl.BlockSpec((d.shape[0], tk, d.shape[2]), lambda k: (0, k, 0)) for d in dparts]
        + [pl.BlockSpec((tk, D_MODEL), lambda k: (k, 0))],
        out_specs=pl.BlockSpec((PROJ, D_MODEL), lambda k: (0, 0)),
        out_shape=SDS((PROJ, D_MODEL), BF), scratch_shapes=[pltpu.VMEM((PROJ, D_MODEL), F32)],
        name="grad_w_in", compiler_params=_params(("arbitrary",), 48))(*dparts, a)


def _down_bwd(dy, w_down, g, u, to_send, tm=512):
    T = dy.shape[0]
    n_s = len(to_send)

    def body(dy_ref, w_ref, g_hbm, u_hbm, *rest):
        srcs, (dg_ref, du_ref), dsts = rest[:n_s], rest[n_s:n_s + 2], rest[n_s + 2:2 * n_s + 2]
        sems, (g_buf, u_buf, fetch_sem) = rest[2 * n_s + 2:-3], rest[-3:]
        i = pl.program_id(0)
        items = list(zip(srcs, dsts, [True] * n_s))

        def fetch(t):
            slot = t % FETCH_DEPTH
            rows = pl.ds(t * tm if isinstance(t, int) else pl.multiple_of(t * tm, tm), tm)
            return [pltpu.make_async_copy(src.at[rows], buf.at[slot], fetch_sem.at[j, slot])
                    for j, (src, buf) in enumerate(((g_hbm, g_buf), (u_hbm, u_buf)))]

        @pl.when(i == 0)
        def _():
            _Scatter(items, sems).start()
            for t in range(FETCH_DEPTH - 1):
                for copy in fetch(t):
                    copy.start()

        @pl.when(i + FETCH_DEPTH - 1 < pl.num_programs(0))
        def _():
            for copy in fetch(i + FETCH_DEPTH - 1):
                copy.start()

        for copy in fetch(i):
            copy.wait()
        g_ref, u_ref = g_buf.at[i % FETCH_DEPTH], u_buf.at[i % FETCH_DEPTH]
        dyv = dy_ref[...]
        for c in range(D_FF // FF_CHUNK):
            cols = slice(c * FF_CHUNK, (c + 1) * FF_CHUNK)
            dact = _dot_nt(dyv, w_ref[cols, :]).astype(BF)
            gv, uv = g_ref[:, cols], u_ref[:, cols]
            s = _sigmoid(gv)
            ds = dact * s
            dg_ref[:, cols] = ds * uv * (1.0 + gv * (1.0 - s))
            du_ref[:, cols] = ds * gv

        @pl.when(i == pl.num_programs(0) - 1)
        def _():
            _Scatter(items, sems).wait()

    tile = pl.BlockSpec((tm, D_FF), _row)
    ring = pltpu.VMEM((FETCH_DEPTH, tm, D_FF), BF)
    outs = pl.pallas_call(
        body, grid=(T // tm,),
        in_specs=[pl.BlockSpec((tm, D_MODEL), _row),
                  pl.BlockSpec((D_FF, D_MODEL), _fixed, pipeline_mode=pl.Buffered(1)), HBM, HBM] + [HBM] * n_s,
        out_specs=[tile, tile] + [HBM] * n_s,
        out_shape=[SDS((T, D_FF), BF)] * 2 + [SDS(s.shape, s.dtype) for s in to_send],
        scratch_shapes=list(_scatter_sems(n_s)) + [ring, ring, DMA_SEMS((2, FETCH_DEPTH))],
        name="down_bwd", compiler_params=_params(("arbitrary",), 56))(dy, w_down, g, u, *to_send)
    return outs[0], outs[1], outs[2:]


def _ffn_in_bwd(dg, du, w_gu_t, h1, dh2, mixed, g_pf, g_pm, to_send, tm=512):
    T = h1.shape[0]
    n_s = len(to_send)

    def body(dg_ref, du_ref, w_ref, h1_ref, dh2_ref, mx_ref, gpf_ref, gpm_ref, *rest):
        srcs, outs, dsts, sems = rest[:n_s], rest[n_s:n_s + 4], rest[n_s + 4:2 * n_s + 4], rest[2 * n_s + 4:]
        dh1_ref, dmx_ref, dgpf_ref, dgpm_ref = outs
        i = pl.program_id(0)
        items = list(zip(srcs, dsts, [True] * n_s))
        _acc_init(i, dgpf_ref, dgpm_ref)

        @pl.when(i == 0)
        def _():
            _Scatter(items, sems).start()

        df = _dot(jnp.concatenate([dg_ref[...], du_ref[...]], axis=1), w_ref[...])
        dx, dgf = _rms_bwd(df, h1_ref[...], gpf_ref[...])
        dh1 = dh2_ref[...] + dx
        dh1_ref[...] = dh1
        dmx, dgm = _rms_bwd(dh1, mx_ref[...], gpm_ref[...])
        dmx_ref[...] = dmx.astype(BF)
        dgpf_ref[...] += _colsum(dgf)
        dgpm_ref[...] += _colsum(dgm)

        @pl.when(i == pl.num_programs(0) - 1)
        def _():
            _Scatter(items, sems).wait()

    tok = lambda w: pl.BlockSpec((tm, w), _row)
    vec = pl.BlockSpec((1, D_MODEL), _fixed)
    outs = pl.pallas_call(
        body, grid=(T // tm,),
        in_specs=[tok(D_FF), tok(D_FF), pl.BlockSpec((2 * D_FF, D_MODEL), _fixed, pipeline_mode=pl.Buffered(1)),
                  tok(D_MODEL), tok(D_MODEL), tok(D_MODEL), vec, vec]
        + [HBM] * n_s,
        out_specs=[tok(D_MODEL), tok(D_MODEL), vec, vec] + [HBM] * n_s,
        out_shape=[SDS((T, D_MODEL), F32), SDS((T, D_MODEL), BF), SDS((1, D_MODEL), F32), SDS((1, D_MODEL), F32)]
        + [SDS(s.shape, s.dtype) for s in to_send],
        scratch_shapes=_scatter_sems(n_s),
        name="ffn_in_bwd", compiler_params=_params(("arbitrary",), 56))(
            dg, du, w_gu_t, h1, dh2, mixed, g_pf, g_pm, *to_send)
    return outs[0], outs[1], outs[2], outs[3], outs[4:]


STAT_LANES = HEAD_DIM // 2


def _ff_grad_spec(half):
    return pl.BlockSpec((D_FF, D_MODEL), lambda i: (half, 0), pipeline_mode=pl.Buffered(1))


def _out_bwd(dmx, w_out, attn, lse, sgu, g_a, g_s, dg, f, tm=512):
    T = attn.shape[0]

    def body(dm_ref, w_ref, a_ref, l_ref, s_ref, ga_ref, gs_ref, dgate_ref, f_ref,
             da_ref, st_ref, ds_ref, dga_ref, dgs_ref, ggu_ref):
        _acc_init(pl.program_id(0), dga_ref, dgs_ref, ggu_ref)
        ggu_ref[...] += _dot_tn(dgate_ref[...], f_ref[...])
        dgr = _dot_nt(dm_ref[...], w_ref[...])
        av = a_ref[...]
        da, dga = _rms_bwd(dgr[:, :ATTN_W], av, ga_ref[...])
        ds, dgs = _rms_bwd(dgr[:, ATTN_W:], s_ref[...], gs_ref[...])
        da_ref[...] = da
        ds_ref[...] = ds
        dga_ref[...] += _colsum(dga)
        dgs_ref[...] += _colsum(dgs)
        lane = lax.broadcasted_iota(jnp.int32, (1, LANES), 1)
        lo = lane < HEAD_DIM
        first = (lane % HEAD_DIM) < STAT_LANES
        prod = da * av
        for c in range(ATTN_W // LANES):
            cols = slice(c * LANES, (c + 1) * LANES)
            pc = prod[:, cols]
            delta = jnp.where(lo, jnp.sum(jnp.where(lo, pc, 0.0), axis=-1, keepdims=True),
                              jnp.sum(jnp.where(lo, 0.0, pc), axis=-1, keepdims=True))
            st_ref[:, cols] = jnp.where(first, l_ref[:, cols], delta)

    tok = lambda w: pl.BlockSpec((tm, w), _row)
    vec = lambda w: pl.BlockSpec((1, w), _fixed)
    return pl.pallas_call(
        body, grid=(T // tm,),
        in_specs=[tok(D_MODEL), pl.BlockSpec((D_MODEL, D_MODEL), _fixed), tok(ATTN_W), tok(ATTN_W), tok(SGU_W),
                  vec(ATTN_W), vec(SGU_W), tok(D_FF), tok(D_MODEL)],
        out_specs=[tok(ATTN_W), tok(ATTN_W), tok(SGU_W), vec(ATTN_W), vec(SGU_W), _ff_grad_spec(0)],
        out_shape=[SDS((T, ATTN_W), F32), SDS((T, ATTN_W), F32), SDS((T, SGU_W), F32), SDS((1, ATTN_W), F32),
                   SDS((1, SGU_W), F32), SDS((2 * D_FF, D_MODEL), F32)],
        name="out_bwd", compiler_params=_params(("arbitrary",), 56))(dmx, w_out, attn, lse, sgu, g_a, g_s, dg, f)


def _sgu_bwd(proj, dsgu, ln_g, ln_b, w_s, b_st, groups, dmx, d_up, f, g_gu_t, tm=512):
    T = proj.shape[0]

    def body(u_ref, z_ref, ds_ref, g_ref, b_ref, w_ref, bs_ref, grp_ref, dmx_ref, dup_ref, f_ref, _,
             duz_ref, dw_ref, dbs_ref, dlg_ref, dlb_ref, gout_ref, ggu_ref, dbacc_ref):
        du_ref, dz_ref = duz_ref.at[0], duz_ref.at[1]
        step = pl.program_id(0)
        _acc_init(step, dw_ref, dbs_ref, dlg_ref, dlb_ref, gout_ref, ggu_ref, dbacc_ref)
        gout_ref[...] += _dot_tn(grp_ref[...], dmx_ref[...])
        ggu_ref[...] += _dot_tn(dup_ref[...], f_ref[...])
        lng, lnb = g_ref[...], b_ref[...]
        for g in range(N_GROUPS):
            wm = _causal(w_ref[g]).astype(BF)
            cols = slice(g * GROUP_DIM, (g + 1) * GROUP_DIM)
            for c in range(tm // CHUNK):
                rows = slice(c * CHUNK, (c + 1) * CHUNK)
                zv, uv, dout = z_ref[rows, cols], u_ref[rows, cols], ds_ref[rows, cols]
                zn, xhat, rs, tz = _sgu_norm(zv, lng, lnb)
                znb = zn.astype(BF)
                mixed = _dot(wm, znb) + bs_ref[:, g:g + 1]
                gu, tu = _gelu(uv)
                du_ref[rows, cols] = (dout * mixed * _gelu_grad(uv, tu)).astype(BF)
                dmix = dout * gu
                dmb = dmix.astype(BF)
                dw_ref[g] += _causal(_dot_nt(dmb, znb))
                dbacc_ref[g] += dmix
                dzn = _dot_tn(wm, dmb)
                dlg_ref[...] += _colsum(dzn * xhat)
                dlb_ref[...] += _colsum(dzn)
                dxh = dzn * lng
                dgz = rs * (dxh - jnp.mean(dxh, axis=-1, keepdims=True)
                            - xhat * jnp.mean(dxh * xhat, axis=-1, keepdims=True))
                dz_ref[rows, cols] = (dgz * _gelu_grad(zv, tz)).astype(BF)

        @pl.when(step == pl.num_programs(0) - 1)
        def _():
            lane = lax.broadcasted_iota(jnp.int32, (CHUNK, LANES), 1)
            acc = jnp.zeros((CHUNK, LANES), F32)
            for g in range(N_GROUPS):
                acc = jnp.where(lane == g, jnp.sum(dbacc_ref[g], axis=-1, keepdims=True), acc)
            dbs_ref[...] = acc

    tok = pl.BlockSpec((tm, SGU_W), _row)
    vec = pl.BlockSpec((1, GROUP_DIM), _fixed)
    wsp = pl.BlockSpec((N_GROUPS, CHUNK, CHUNK), lambda i: (0, 0, 0))
    sq = pl.BlockSpec((CHUNK, LANES), _fixed)
    wide = pl.BlockSpec((tm, D_MODEL), _row)
    return pl.pallas_call(
        body, grid=(T // tm,),
        in_specs=[pl.BlockSpec((tm, SGU_W), lambda i: (i, 3)), pl.BlockSpec((tm, SGU_W), lambda i: (i, 4)), tok,
                  vec, vec, wsp, sq, wide, wide, pl.BlockSpec((tm, D_FF), _row), wide, HBM],
        out_specs=[pl.BlockSpec((2, tm, SGU_W), lambda i: (0, i, 0)), wsp, sq, vec, vec,
                   pl.BlockSpec((D_MODEL, D_MODEL), _fixed), _ff_grad_spec(1)],
        out_shape=[SDS((2, T, SGU_W), BF), SDS((N_GROUPS, CHUNK, CHUNK), F32),
                   SDS((CHUNK, LANES), F32), SDS((1, GROUP_DIM), F32), SDS((1, GROUP_DIM), F32),
                   SDS((D_MODEL, D_MODEL), F32), SDS((2 * D_FF, D_MODEL), F32)],
        input_output_aliases={11: 6},
        scratch_shapes=[pltpu.VMEM((N_GROUPS, CHUNK, LANES), F32)],
        name="sgu_bwd", compiler_params=_params(("arbitrary",), 56))(
            proj, proj, dsgu, ln_g, ln_b, w_s, b_st, groups, dmx, d_up, f, g_gu_t)


def _attn_bwd(proj, do, stats, slopes, to_send, slabbed):
    T = proj.shape[0]
    nblk = T // QBLK
    n_s = len(to_send)

    def body(q_ref, k_ref, v_ref, do_ref, st_ref, sl_ref, *rest):
        srcs, d_ref, dsts = rest[:n_s], rest[n_s], rest[n_s + 1:2 * n_s + 1]
        sems, bias_ref = rest[2 * n_s + 1:2 * n_s + 4], rest[2 * n_s + 4]
        dq_ref, dk_ref, dv_ref = d_ref.at[0], d_ref.at[1], d_ref.at[2]
        h = pl.program_id(0)
        items = list(zip(srcs, dsts, slabbed))

        @pl.when(h == 0)
        def _():
            _Scatter(items, sems).start()

        _attn_bias(sl_ref, bias_ref)
        lo = lax.broadcasted_iota(jnp.int32, (1, LANES), 1) < HEAD_DIM
        scale = HEAD_DIM ** -0.5
        d_ref[...] = jnp.zeros_like(d_ref)

        for di, d in enumerate(DILATIONS):
            group, segs = _attn_plan(nblk, d)

            def step(i, carry, segs=segs, **kw):
                for s in range(segs):
                    segment(i * segs + s, **kw)
                return carry

            def segment(i, d=d, di=di, group=group):
                start, pstart, first = _attn_group_index(i, nblk, d, group)
                rows, prows = _attn_rows(start, d, group), _attn_rows(pstart, d)
                alone = group == nblk // d
                q = q_ref[rows, :] * scale
                if alone:
                    k, v, own0 = k_ref[rows, :].astype(BF), v_ref[rows, :].astype(BF), 0
                else:
                    k = jnp.concatenate([k_ref[prows, :], k_ref[rows, :]], axis=0).astype(BF)
                    v = jnp.concatenate([v_ref[prows, :], v_ref[rows, :]], axis=0).astype(BF)
                    own0 = QBLK
                dov = do_ref[rows, :]
                stats = st_ref[rows, :]
                masks = [lo, ~lo]
                qm = [jnp.where(masks[j], q, 0.0).astype(BF) for j in range(2)]
                dom = [jnp.where(masks[j], dov, 0.0).astype(BF) for j in range(2)]
                for b in range(group):
                    qb = slice(b * QBLK, (b + 1) * QBLK)
                    own_only = alone and b == 0
                    kb = slice(own0 + (b if own_only else b - 1) * QBLK, own0 + (b + 1) * QBLK)
                    which = di * 2 + first.astype(jnp.int32) if b == 0 and not own_only else di * 2
                    dq_parts, prs, dss = [], [], []
                    for j in range(2):
                        bias = bias_ref[which, j * QBLK:(j + 1) * QBLK, QBLK if own_only else 0:]
                        lj = stats[qb, j * HEAD_DIM:j * HEAD_DIM + 1]
                        delta = stats[qb, j * HEAD_DIM + STAT_LANES:j * HEAD_DIM + STAT_LANES + 1]
                        pr = jnp.exp(_dot_nt(qm[j][qb], k[kb]) + bias - lj)
                        ds = (pr * (_dot_nt(dom[j][qb], v[kb]) - delta)).astype(BF)
                        dq_parts.append(_dot(ds, k[kb]))
                        prs.append(pr.astype(BF))
                        dss.append(ds)
                    dk_b = _dot_tn(jnp.concatenate(dss, axis=0), jnp.concatenate([qm[0][qb], qm[1][qb]], axis=0))
                    dv_b = _dot_tn(jnp.concatenate(prs, axis=0), jnp.concatenate([dom[0][qb], dom[1][qb]], axis=0))
                    own = _attn_rows(start + b * (d * QBLK), d)
                    dq_ref[own, :] += jnp.where(lo, dq_parts[0], dq_parts[1]) * scale
                    if own_only:
                        dk_ref[own, :] += dk_b
                        dv_ref[own, :] += dv_b
                    elif b == 0:
                        dk_ref[prows, :] += dk_b[:QBLK]
                        dv_ref[prows, :] += dv_b[:QBLK]
                        dk_ref[own, :] += dk_b[QBLK:]
                        dv_ref[own, :] += dv_b[QBLK:]
                    else:
                        two = _attn_rows(start + (b - 1) * (d * QBLK), d, 2)
                        dk_ref[two, :] += dk_b
                        dv_ref[two, :] += dv_b

            lax.fori_loop(0, nblk // (group * segs), step, 0)

        @pl.when(h == pl.num_programs(0) - 1)
        def _():
            _Scatter(items, sems).wait()

    col = lambda base: pl.BlockSpec((T, LANES), lambda h: (0, base + h))
    outs = pl.pallas_call(
        body, grid=(4,),
        in_specs=[col(0), col(4), col(8), col(0), col(0), pl.BlockSpec((1, 8, LANES), lambda h: (h, 0, 0))]
        + [HBM] * n_s,
        out_specs=[pl.BlockSpec((3, T, LANES), lambda h: (0, 0, h), pipeline_mode=pl.Buffered(1))] + [HBM] * n_s,
        out_shape=[SDS((3, T, ATTN_W), F32)]
        + [SDS(s.shape if sl else (N_DEV,) + s.shape, s.dtype) for s, sl in zip(to_send, slabbed)],
        scratch_shapes=_scatter_sems(n_s) + [pltpu.VMEM((6, 2 * QBLK, 2 * QBLK), F32)],
        name="attn_bwd", compiler_params=_params(("arbitrary",), 60))(proj, proj, proj, do, stats, slopes, *to_send)
    return outs[0], outs[1:]


def _in_bwd(dparts, w_in_t, x, dh1, g1, tm=512):
    T = x.shape[0]
    n = len(dparts)
    w = ATTN_W

    def body(*refs):
        d_refs, (w_ref, x_ref, dh1_ref, g_ref, dx_ref, dg_ref) = refs[:n], refs[n:]
        _acc_init(pl.program_id(0), dg_ref)
        d_proj = jnp.concatenate([r[part].astype(BF) for r in d_refs for part in range(r.shape[0])], axis=1)
        da = _dot(d_proj, w_ref[...])
        dx, dg = _rms_bwd(da, x_ref[...], g_ref[...])
        dx_ref[...] = dh1_ref[...] + dx
        dg_ref[...] += _colsum(dg)

    tok = lambda c: pl.BlockSpec((tm, c), _row)
    vec = pl.BlockSpec((1, D_MODEL), _fixed)
    return pl.pallas_call(
        body, grid=(T // tm,),
        in_specs=[pl.BlockSpec((d.shape[0], tm, w), lambda i: (0, i, 0)) for d in dparts]
        + [pl.BlockSpec((PROJ, D_MODEL), _fixed), tok(D_MODEL), tok(D_MODEL), vec],
        out_specs=[tok(D_MODEL), vec],
        out_shape=[SDS((T, D_MODEL), F32), SDS((1, D_MODEL), F32)],
        name="in_bwd", compiler_params=_params(("arbitrary",), 52))(*dparts, w_in_t, x, dh1, g1)


def _sum_parts(p_ref):
    g = p_ref[0].astype(F32)
    for s in range(1, N_DEV):
        g = g + p_ref[s].astype(F32)
    return g


def _adamw_math(g, w, m, v):
    nm = ADAM_B1 * m + (1.0 - ADAM_B1) * g
    nv = ADAM_B2 * v + (1.0 - ADAM_B2) * (g * g)
    m_hat = nm / (1.0 - ADAM_B1 ** ADAM_STEP)
    v_hat = nv / (1.0 - ADAM_B2 ** ADAM_STEP)
    return -ADAM_LR * (m_hat / (jnp.sqrt(v_hat) + ADAM_EPS) + ADAM_WD * w), nm, nv


def _row_tile(rows):
    for t in (256, 176, 128, 80):
        if rows % t == 0:
            return t
    raise ValueError(rows)


def _reduce_adamw(parts, w, m, v, name):
    rows, width = w.shape
    tr = _row_tile(rows)

    def body(p_ref, w_ref, m_ref, v_ref, g_ref, d_ref, nm_ref, nv_ref):
        g = _sum_parts(p_ref)
        g_ref[...] = g
        d_ref[...], nm_ref[...], nv_ref[...] = _adamw_math(g, w_ref[...], m_ref[...], v_ref[...])

    blk = pl.BlockSpec((tr, width), _row)
    return pl.pallas_call(
        body, grid=(rows // tr,),
        in_specs=[pl.BlockSpec((N_DEV, tr, width), lambda i: (0, i, 0)), blk, blk, blk],
        out_specs=[blk] * 4, out_shape=[SDS((rows, width), F32)] * 4,
        name="adamw_" + name, compiler_params=_params(("arbitrary",), 32))(parts, w, m, v)


def _reduce(parts, name):
    _, rows, width = parts.shape
    tr = _row_tile(rows)

    def body(p_ref, g_ref):
        g_ref[...] = _sum_parts(p_ref)

    return pl.pallas_call(
        body, grid=(rows // tr,),
        in_specs=[pl.BlockSpec((N_DEV, tr, width), lambda i: (0, i, 0))],
        out_specs=pl.BlockSpec((tr, width), _row), out_shape=SDS((rows, width), F32),
        name="sum_" + name, compiler_params=_params(("arbitrary",), 32))(parts)


def _adamw(g, w, m, v, name):
    rows, width = w.shape
    tr = _row_tile(rows)

    def body(g_ref, w_ref, m_ref, v_ref, d_ref, nm_ref, nv_ref):
        d_ref[...], nm_ref[...], nv_ref[...] = _adamw_math(g_ref[...], w_ref[...], m_ref[...], v_ref[...])

    blk = pl.BlockSpec((tr, width), _row)
    return pl.pallas_call(
        body, grid=(rows // tr,), in_specs=[blk] * 4, out_specs=[blk] * 3, out_shape=[SDS((rows, width), F32)] * 3,
        name="adamw_" + name, compiler_params=_params(("arbitrary",), 32))(g, w, m, v)


SMALL = ("w_spatial", "ln_pre_mix", "ln_post_mix", "ln_pre_ffn", "ln_post_ffn", "b_pe_gate",
         "attn_out_norm", "sgu_out_norm", "b_spatial", "sgu_ln_g", "sgu_ln_b")
SMALL_GROUPS = ((128, ("w_spatial", "b_spatial", "sgu_ln_g", "sgu_ln_b")),
                (512, ("attn_out_norm", "sgu_out_norm")),
                (1024, ("ln_post_mix", "ln_pre_ffn", "ln_post_ffn", "b_pe_gate")))
SMALL_LATE = "ln_pre_mix"
SMALL_SIZE = dict(w_spatial=N_GROUPS * CHUNK * CHUNK, b_spatial=N_GROUPS * CHUNK, sgu_ln_g=GROUP_DIM, sgu_ln_b=GROUP_DIM,
                  attn_out_norm=ATTN_W, sgu_out_norm=SGU_W, ln_pre_mix=D_MODEL, ln_post_mix=D_MODEL, ln_pre_ffn=D_MODEL,
                  ln_post_ffn=D_MODEL, b_pe_gate=D_MODEL)
SUBLANES = 8
ROW_SHARDED = ("w_out", "w_down", "w_pe_gate")
COL_SHARDED = ("w_in", "w_gate_up", "w_pe_proj")
WEIGHTS = ("ln_pre_mix", "w_in", "sgu_ln_g", "sgu_ln_b", "w_spatial", "b_spatial", "attn_out_norm", "sgu_out_norm",
           "w_out", "ln_post_mix", "ln_pre_ffn", "w_gate_up", "w_down", "ln_post_ffn", "w_pe_gate", "b_pe_gate",
           "w_pe_proj")


def _group_rows(width, names, extra=0):
    rows = sum(SMALL_SIZE[n] // width for n in names) + extra
    return -(-rows // SUBLANES) * SUBLANES


def _pack_small_grads(gs, loss_term):
    packed = []
    for width, names in SMALL_GROUPS:
        rows = [gs[n].reshape(-1, width) for n in names]
        extra = int(width == D_MODEL)
        if extra:
            rows.append(jnp.full((1, width), loss_term, F32))
        used = sum(r.shape[0] for r in rows)
        rows.append(jnp.zeros((_group_rows(width, names, extra) - used, width), F32))
        packed.append(jnp.concatenate(rows, axis=0))
    return packed


def _small_adamw(arrived, arrived_late, w, m, v):
    names = [n for _, ns in SMALL_GROUPS for n in ns] + [SMALL_LATE]
    n_groups = len(SMALL_GROUPS)

    def body(*refs):
        group_refs, late_ref = refs[:n_groups], refs[n_groups]
        state = refs[n_groups + 1:n_groups + 1 + 3 * len(names)]
        outs = refs[n_groups + 1 + 3 * len(names):]
        sums = [_sum_parts(r) for r in group_refs]

        def update(name, g):
            i = names.index(name)
            w_ref, m_ref, v_ref = state[3 * i:3 * i + 3]
            delta, nm, nv = _adamw_math(g, w_ref[...].reshape(g.shape), m_ref[...].reshape(g.shape),
                                        v_ref[...].reshape(g.shape))
            for o_ref, val in zip(outs[4 * i:4 * i + 4], (g, delta, nm, nv)):
                o_ref[...] = val.reshape(o_ref.shape)

        for (width, group), total in zip(SMALL_GROUPS, sums):
            row = 0
            for name in group:
                rows = SMALL_SIZE[name] // width
                update(name, total[row:row + rows, :])
                row += rows
            if width == D_MODEL:
                outs[-1][...] = total[row:row + 1, :LANES]
        update(SMALL_LATE, _sum_parts(late_ref)[:1, :])

    state = [t[n] for n in names for t in (w, m, v)]
    plain = jax.ShapeDtypeStruct
    out_shape = [plain(w[n].shape, F32) for n in names for _ in range(4)] + [plain((1, LANES), F32)]
    outs = pl.pallas_call(body, out_shape=out_shape, name="adamw_small",
                          compiler_params=pltpu.CompilerParams(vmem_limit_bytes=32 * MIB))(*arrived, arrived_late, *state)
    return {n: outs[4 * i:4 * i + 4] for i, n in enumerate(names)}, outs[-1]


def _slabs(full):
    return full.reshape(N_DEV, full.shape[0] // N_DEV, full.shape[1])


def kernel(x, p, ln_pre_mix, w_in, sgu_ln_g, sgu_ln_b, w_spatial, b_spatial, attn_out_norm, sgu_out_norm, w_out, ln_post_mix, ln_pre_ffn, w_gate_up, w_down, ln_post_ffn, w_pe_gate, b_pe_gate, w_pe_proj, loss_target, m_ln_pre_mix, m_w_in, m_sgu_ln_g, m_sgu_ln_b, m_w_spatial, m_b_spatial, m_attn_out_norm, m_sgu_out_norm, m_w_out, m_ln_post_mix, m_ln_pre_ffn, m_w_gate_up, m_w_down, m_ln_post_ffn, m_w_pe_gate, m_b_pe_gate, m_w_pe_proj, v_ln_pre_mix, v_w_in, v_sgu_ln_g, v_sgu_ln_b, v_w_spatial, v_b_spatial, v_attn_out_norm, v_sgu_out_norm, v_w_out, v_ln_post_mix, v_ln_pre_ffn, v_w_gate_up, v_w_down, v_ln_post_ffn, v_w_pe_gate, v_b_pe_gate, v_w_pe_proj):
    given = dict(locals())
    w = {n: given[n] for n in WEIGHTS}
    m = {n: given["m_" + n] for n in WEIGHTS}
    v = {n: given["v_" + n] for n in WEIGHTS}
    xs, ps, target = x[0], p[0, 0], loss_target[0]

    shard = {n: w[n][0].astype(BF) for n in ROW_SHARDED}
    shard.update({n: w[n][0].T.astype(BF) for n in COL_SHARDED})
    sm = {n: w[n][0] for n in SMALL}
    sm = {n: (a.reshape(1, -1) if a.ndim == 1 else a) for n, a in sm.items()}
    slopes = jnp.broadcast_to((2.0 ** -(jnp.arange(8, dtype=F32) + 1.0)).reshape(4, 2, 1), (4, 2, LANES))
    slopes = jnp.concatenate([slopes, jnp.zeros((4, 6, LANES), F32)], axis=1)
    b_st = jnp.pad(sm["b_spatial"].T, ((0, 0), (0, LANES - N_GROUPS)))

    def full(gathered):
        return gathered.reshape(-1, gathered.shape[-1])

    w_in_t = full(_all_gather(shard["w_in"], "gather_w_in"))
    proj, a = _in_proj(xs, sm["ln_pre_mix"], w_in_t)
    later = ("w_out", "w_gate_up", "w_down", "w_pe_gate", "w_pe_proj")
    attn, lse, gathered = _attn_fwd(proj, slopes, [shard[n] for n in later])
    w_out_f, w_gu_t, w_down_f, w_peg_f, w_pep_t = [full(g) for g in gathered]
    sgu, groups, mixed, h1, f = _out_proj(attn, proj, xs, sm["sgu_ln_g"], sm["sgu_ln_b"], sm["w_spatial"], b_st,
                                          sm["attn_out_norm"], sm["sgu_out_norm"], w_out_f,
                                          sm["ln_post_mix"], sm["ln_pre_ffn"])
    g, u, act = _gate_up(f, w_gu_t)
    y, h2 = _down_proj(act, w_down_f, h1, sm["ln_post_ffn"])
    dh2, dy, g_peg, g_pep_t, loss_cols, db_peg, d_pff = _pe_loss_and_bwd(
        h2, ps, target, y, w_peg_f, sm["b_pe_gate"], w_pep_t, sm["ln_post_ffn"])
    loss_term = 0.5 * jnp.sum(loss_cols) * (1.0 / D_MODEL)

    arrived = {}
    g_down = _weight_grad(act, dy, "w_down")
    dg, du, (arrived["w_pe_proj"], arrived["w_pe_gate"]) = _down_bwd(dy, w_down_f, g, u, [_slabs(g_pep_t), _slabs(g_peg)])
    dh1, dmx, d_pf, d_pm, (arrived["w_down"],) = _ffn_in_bwd(dg, du, w_gu_t, h1, dh2, mixed, sm["ln_pre_ffn"],
                                                            sm["ln_post_mix"], [_slabs(g_down)])
    dattn, stats, dsgu, d_ga, d_gs, g_gu_t = _out_bwd(dmx, w_out_f, attn, lse, sgu, sm["attn_out_norm"],
                                                      sm["sgu_out_norm"], dg, f)
    duz, d_ws, d_bst, d_lg, d_lb, g_out, g_gu_t = _sgu_bwd(proj, dsgu, sm["sgu_ln_g"], sm["sgu_ln_b"],
                                                           sm["w_spatial"], b_st, groups, dmx, du, f, g_gu_t)
    gs = dict(sgu_ln_g=d_lg, sgu_ln_b=d_lb, w_spatial=d_ws, b_spatial=d_bst[:, :N_GROUPS].T, attn_out_norm=d_ga,
              sgu_out_norm=d_gs, ln_post_mix=d_pm, ln_pre_ffn=d_pf, ln_post_ffn=d_pff, b_pe_gate=db_peg)
    small_grads = _pack_small_grads(gs, loss_term)
    dqkv, (arrived["w_gate_up"], arrived["w_out"], *arrived_small) = _attn_bwd(
        proj, dattn, stats, slopes, [_slabs(g_gu_t), _slabs(g_out), *small_grads],
        [True, True] + [False] * len(small_grads))
    send_sems, recv_sems, slabs, landing, token = _scatter_begin(_slabs(_grad_w_in([dqkv, duz], a)), "w_in_grad_send")
    grad_x, d_g1 = _in_bwd([dqkv, duz], w_in_t, xs, dh1, sm["ln_pre_mix"] + token[:1, :1])
    slabs, landing = _scatter_end(send_sems, recv_sems, slabs, landing, d_g1, "w_in_grad_arrive")
    me = 4 * lax.axis_index("x") + 2 * lax.axis_index("y") + lax.axis_index("c")
    own = lax.dynamic_slice_in_dim(slabs, me, 1, axis=0)
    arrived["w_in"] = lax.dynamic_update_slice_in_dim(landing, own, me, axis=0)
    (arrived_late,) = _scatter_call([jnp.pad(d_g1, ((0, SUBLANES - 1), (0, 0)))], [False], "ln_pre_mix_grad_exchange")

    res = {}
    for n in ROW_SHARDED:
        res[n] = _reduce_adamw(arrived[n], w[n][0], m[n][0], v[n][0], n)
    for n in ("w_in", "w_gate_up"):
        res[n] = [t.T for t in _reduce_adamw(arrived[n], w[n][0].T, m[n][0].T, v[n][0].T, n)]
    for n in ("w_pe_proj",):
        grad = _reduce(arrived[n], n).T
        res[n] = (grad, *_adamw(grad, w[n][0], m[n][0], v[n][0], n))
    small, loss_row = _small_adamw(arrived_small, arrived_late, w, m, v)

    out = []
    for k in range(4):
        out += [res[n][k][None] if n in res else small[n][k] for n in WEIGHTS]
    return (loss_row[0, 0], grad_x[None], *out)
```

```python
import math

import jax
import jax.numpy as jnp
from jax import lax
from jax.experimental import pallas as pl
from jax.experimental.pallas import tpu as pltpu

F32 = jnp.float32
BF = jnp.bfloat16


def SDS(shape, dtype):
    return pltpu.HBM(tuple(shape), dtype)

D_MODEL = 1024
ATTN_W = 512
SGU_W = 512
HEAD_DIM = 64
N_GROUPS = 4
GROUP_DIM = 128
CHUNK = 128
D_FF = 2816
PLE_DIM = 256
PROJ = 3 * ATTN_W + 2 * SGU_W
DILATIONS = (1, 4, 16)
QBLK = 128
EPS = 1e-6
NEG = -1e30
N_DEV = 8
LANES = 128

ADAM_LR = 0.001
ADAM_B1 = 0.9
ADAM_B2 = 0.999
ADAM_EPS = 1e-08
ADAM_WD = 0.01
ADAM_STEP = 10

MIB = 2 ** 20
MESH_ID = pl.DeviceIdType.MESH
HBM = pl.BlockSpec(memory_space=pl.ANY)


def _params(sem, vmem_mib):
    return pltpu.CompilerParams(dimension_semantics=sem, vmem_limit_bytes=vmem_mib * MIB)


def _dot(a, b):
    return jnp.dot(a, b, preferred_element_type=F32)


def _dot_nt(a, b):
    return lax.dot_general(a, b, (((1,), (1,)), ((), ())), preferred_element_type=F32)


def _dot_tn(a, b):
    return lax.dot_general(a, b, (((0,), (0,)), ((), ())), preferred_element_type=F32)


def _rstd(x):
    return lax.rsqrt(jnp.mean(x * x, axis=-1, keepdims=True) + EPS)


def _rms_bwd(dy, x, g):
    r = _rstd(x)
    n = x * r
    dn = dy * g
    dx = r * (dn - n * jnp.mean(dn * n, axis=-1, keepdims=True))
    return dx, dy * n


def _colsum(v):
    return jnp.sum(v, axis=0, keepdims=True)


_G0 = math.sqrt(2.0 / math.pi)
_G1 = 0.044715


def _gelu(x):
    t = jnp.tanh(_G0 * (x + _G1 * x * x * x))
    return 0.5 * x * (1.0 + t), t


def _gelu_grad(x, t):
    return 0.5 * (1.0 + t) + 0.5 * x * (1.0 - t * t) * (_G0 * (1.0 + 3.0 * _G1 * x * x))


def _sigmoid(x):
    return 0.5 * jnp.tanh(0.5 * x) + 0.5


def _row(i):
    return (i, 0)


def _fixed(i):
    return (0, 0)


def _acc_init(step, *refs):
    @pl.when(step == 0)
    def _():
        for r in refs:
            r[...] = jnp.zeros_like(r)


FLIPS = [(dx, dy, dc) for dx in (0, 1) for dy in (0, 1) for dc in (0, 1)][1:]
DMA_SEMS = pltpu.SemaphoreType.DMA


def _mesh_pos():
    return lax.axis_index("x"), lax.axis_index("y"), lax.axis_index("c")


def _remote(src, dst, sems, n, to):
    return pltpu.make_async_remote_copy(src_ref=src, dst_ref=dst, send_sem=sems[0].at[n], recv_sem=sems[1].at[n],
                                        device_id=to, device_id_type=MESH_ID)


class _Scatter:
    def __init__(self, items, sems):
        x, y, c = _mesh_pos()
        me = 4 * x + 2 * y + c
        self.local, self.sends, self.arrivals = [], [], []
        for i, (src, dst, slabbed) in enumerate(items):
            self.local.append(pltpu.make_async_copy(src.at[me] if slabbed else src, dst.at[me], sems[2].at[i]))
            for k, (dx, dy, dc) in enumerate(FLIPS):
                to = (1 - x if dx else x, 1 - y if dy else y, 1 - c if dc else c)
                peer = 4 * to[0] + 2 * to[1] + to[2]
                out = src.at[peer] if slabbed else src
                self.sends.append(_remote(out, dst.at[me], sems, 7 * i + k, to))
                self.arrivals.append(_remote(out, dst.at[peer], sems, 7 * i + k, to))

    def start(self):
        for cp in self.local + self.sends:
            cp.start()

    def wait(self):
        for cp in self.arrivals:
            cp.wait_recv()
        for cp in self.sends:
            cp.wait_send()
        for cp in self.local:
            cp.wait()


def _scatter_sems(n):
    return [DMA_SEMS((7 * n,)), DMA_SEMS((7 * n,)), DMA_SEMS((n,))]


class _Gather:
    def __init__(self, items, sems):
        x, y, c = _mesh_pos()
        me, sibling = (x, y, c), (x, y, 1 - c)
        chips = [(1 - x, y), (x, 1 - y), (1 - x, 1 - y)]
        self.first, self.passed, self.from_chips, self.rest, self.local = [], [], [], [], []
        for i, (src, dst) in enumerate(items):
            def slot(p, dst=dst):
                return dst.at[4 * p[0] + 2 * p[1] + p[2]]

            def copy(k, block, to, own=False, i=i, src=src, slot=slot):
                return _remote(src if own else slot(block), slot(block), sems, 7 * i + k, to)

            self.local.append(pltpu.make_async_copy(src, slot(me), sems[2].at[i]))
            self.first.append(copy(0, me, sibling, own=True))
            self.first += [copy(1 + j, me, (*chip, c), own=True) for j, chip in enumerate(chips)]
            self.passed += [copy(4 + j, (*chip, c), sibling) for j, chip in enumerate(chips)]
            self.from_chips += [copy(1 + j, (*chip, c), me) for j, chip in enumerate(chips)]
            self.rest.append(copy(0, sibling, me))
            self.rest += [copy(4 + j, (*chip, 1 - c), me) for j, chip in enumerate(chips)]

    def start(self):
        for cp in self.local + self.first:
            cp.start()

    def forward(self):
        for arrived, onward in zip(self.from_chips, self.passed):
            arrived.wait_recv()
            onward.start()

    def finish(self):
        for cp in self.rest:
            cp.wait_recv()
        for cp in self.first + self.passed:
            cp.wait_send()
        for cp in self.local:
            cp.wait()


def _all_gather(shard, name):
    def body(x_ref, out_ref, *sems):
        g = _Gather([(x_ref, out_ref)], sems)
        g.start()
        g.forward()
        g.finish()

    return pl.pallas_call(
        body, out_shape=SDS((N_DEV,) + shard.shape, shard.dtype), in_specs=[HBM], out_specs=HBM,
        scratch_shapes=_scatter_sems(1), name=name)(shard)


def _scatter_call(srcs, slabbed, name):
    n = len(srcs)

    def body(*refs):
        sc = _Scatter(list(zip(refs[:n], refs[n:2 * n], slabbed)), refs[2 * n:])
        sc.start()
        sc.wait()

    shapes = [SDS(s.shape if sl else (N_DEV,) + s.shape, s.dtype) for s, sl in zip(srcs, slabbed)]
    return pl.pallas_call(body, out_shape=shapes, in_specs=[HBM] * n, out_specs=[HBM] * n,
                          scratch_shapes=_scatter_sems(n), name=name)(*srcs)


SEM = pl.BlockSpec(memory_space=pltpu.SEMAPHORE)
N_PEERS = len(FLIPS)


def _slab_copies(src_ref, land_ref, send_sems, recv_sems):
    x, y, c = _mesh_pos()
    me = 4 * x + 2 * y + c
    copies = []
    for k, (dx, dy, dc) in enumerate(FLIPS):
        to = (1 - x if dx else x, 1 - y if dy else y, 1 - c if dc else c)
        peer = 4 * to[0] + 2 * to[1] + to[2]
        sems = (send_sems, recv_sems)
        copies.append((_remote(src_ref.at[peer], land_ref.at[me], sems, k, to),
                       _remote(src_ref.at[peer], land_ref.at[peer], sems, k, to)))
    return copies


def _scatter_begin(src, name):
    def body(src_ref, land_ref, send_sems, recv_sems, src_thru, land_thru, token):
        for send, _ in _slab_copies(src_ref, land_ref, send_sems, recv_sems):
            send.start()
        token[...] = jnp.zeros_like(token)

    landing = lax.empty(src.shape, src.dtype)
    return pl.pallas_call(
        body, name=name,
        out_shape=(pltpu.SemaphoreType.DMA((N_PEERS,)), pltpu.SemaphoreType.DMA((N_PEERS,)),
                   pltpu.HBM(src.shape, src.dtype), pltpu.HBM(src.shape, src.dtype),
                   jax.ShapeDtypeStruct((SUBLANES, LANES), F32)),
        in_specs=(HBM, HBM), out_specs=(SEM, SEM, HBM, HBM, pl.BlockSpec(memory_space=pltpu.VMEM)),
        input_output_aliases={0: 2, 1: 3},
        compiler_params=pltpu.CompilerParams(has_side_effects=pltpu.SideEffectType.DATAFLOW_SIDE_EFFECTING))(
            pltpu.with_memory_space_constraint(src, pltpu.HBM), pltpu.with_memory_space_constraint(landing, pltpu.HBM))


def _scatter_end(send_sems, recv_sems, src_thru, land_thru, after, name):
    def body(src_ref, land_ref, send_sems, recv_sems, after_ref, src_dead, land_out):
        for send, arrival in _slab_copies(src_ref, land_ref, send_sems, recv_sems):
            send.wait_send()
            arrival.wait_recv()

    return pl.pallas_call(
        body, name=name,
        out_shape=(pltpu.HBM(src_thru.shape, src_thru.dtype), pltpu.HBM(land_thru.shape, land_thru.dtype)),
        in_specs=(HBM, HBM, SEM, SEM, HBM), out_specs=(HBM, HBM), input_output_aliases={0: 0, 1: 1},
        compiler_params=pltpu.CompilerParams(has_side_effects=pltpu.SideEffectType.DATAFLOW_SIDE_EFFECTING))(
            src_thru, land_thru, send_sems, recv_sems, after)


def _in_proj(x, g1, w_in_t, tm=512):
    T = x.shape[0]

    def body(x_ref, g_ref, w_ref, proj_ref, a_ref):
        xv = x_ref[...]
        a = (xv * _rstd(xv) * g_ref[...]).astype(BF)
        a_ref[...] = a
        proj_ref[...] = _dot_nt(a, w_ref[...])

    return pl.pallas_call(
        body, grid=(T // tm,),
        in_specs=[pl.BlockSpec((tm, D_MODEL), _row), pl.BlockSpec((1, D_MODEL), _fixed),
                  pl.BlockSpec((PROJ, D_MODEL), _fixed)],
        out_specs=[pl.BlockSpec((tm, PROJ), _row), pl.BlockSpec((tm, D_MODEL), _row)],
        out_shape=[SDS((T, PROJ), F32), SDS((T, D_MODEL), BF)],
        name="in_proj", compiler_params=_params(("arbitrary",), 48))(x, g1, w_in_t)


ATTN_GROUP = 16


def _attn_bias(sl_ref, bias_ref):
    qi = lax.broadcasted_iota(jnp.int32, (QBLK, QBLK), 0)
    kj = lax.broadcasted_iota(jnp.int32, (QBLK, QBLK), 1)
    step = qi - kj
    for di, d in enumerate(DILATIONS):
        for j in range(2):
            sl = sl_ref[0, j:j + 1, :]
            cur = jnp.where(step >= 0, -sl * (step * d).astype(F32), NEG)
            prev = jnp.where(step <= 0, -sl * ((step + QBLK) * d).astype(F32), NEG)
            rows = slice(j * QBLK, (j + 1) * QBLK)
            bias_ref[di * 2, rows, :QBLK] = prev
            bias_ref[di * 2, rows, QBLK:] = cur
            bias_ref[di * 2 + 1, rows, :QBLK] = jnp.full((QBLK, QBLK), NEG, F32)
            bias_ref[di * 2 + 1, rows, QBLK:] = cur


def _stack_heads(x, lo):
    return jnp.concatenate([jnp.where(lo, x, 0.0), jnp.where(lo, 0.0, x)], axis=0).astype(BF)


def _unstack_heads(x, lo):
    return jnp.where(lo, x[:QBLK], x[QBLK:])


def _attn_rows(start, d, blocks=1):
    if d == 1:
        return pl.ds(pl.multiple_of(start, QBLK), blocks * QBLK)
    return pl.ds(start, blocks * QBLK, stride=d)


def _attn_group_index(i, nblk, d, group):
    per = nblk // d // group
    r = i // per
    n0 = (i % per) * group
    start = r + (d * QBLK) * n0
    pstart = jnp.maximum(start - d * QBLK, r)
    return start, pstart, n0 == 0


def _attn_plan(nblk, d):
    group = min(ATTN_GROUP, nblk // d)
    return group, max(1, min(ATTN_GROUP // group, d))


def _attn_fwd(proj, slopes, to_gather):
    T = proj.shape[0]
    nblk = T // QBLK
    n_g = len(to_gather)

    def body(q_ref, k_ref, v_ref, sl_ref, *rest):
        srcs, (o_ref, m_ref), dsts = rest[:n_g], rest[n_g:n_g + 2], rest[n_g + 2:2 * n_g + 2]
        sems, (l_ref, bias_ref) = rest[2 * n_g + 2:2 * n_g + 5], rest[2 * n_g + 5:]
        h = pl.program_id(0)

        @pl.when(h == 0)
        def _():
            _Gather(list(zip(srcs, dsts)), sems).start()

        @pl.when(h == pl.num_programs(0) - 1)
        def _():
            _Gather(list(zip(srcs, dsts)), sems).forward()

        _attn_bias(sl_ref, bias_ref)
        lo = lax.broadcasted_iota(jnp.int32, (1, LANES), 1) < HEAD_DIM

        order = list(enumerate(DILATIONS))[::-1]
        for di, d in order:
            group, segs = _attn_plan(nblk, d)
            fresh, last = di == order[0][0], di == order[-1][0]

            def step(i, carry, segs=segs, **kw):
                for s in range(segs):
                    segment(i * segs + s, **kw)
                return carry

            def segment(i, d=d, di=di, group=group, fresh=fresh, last=last):
                start, pstart, first = _attn_group_index(i, nblk, d, group)
                prows = _attn_rows(pstart, d)
                alone = group == nblk // d
                k_prev = v_prev = None
                if not alone:
                    k_prev, v_prev = k_ref[prows, :].astype(BF), v_ref[prows, :].astype(BF)
                for b in range(group):
                    out = _attn_rows(start + b * (d * QBLK), d)
                    k_own, v_own = k_ref[out, :].astype(BF), v_ref[out, :].astype(BF)
                    if alone and b == 0:
                        k2, v2, bias = k_own, v_own, bias_ref[di * 2, :, QBLK:]
                    else:
                        k2, v2 = jnp.concatenate([k_prev, k_own], axis=0), jnp.concatenate([v_prev, v_own], axis=0)
                        bias = bias_ref[di * 2 + first.astype(jnp.int32)] if b == 0 else bias_ref[di * 2]
                    k_prev, v_prev = k_own, v_own
                    s = _dot_nt(_stack_heads(q_ref[out, :] * (HEAD_DIM ** -0.5), lo), k2) + bias
                    m = jnp.max(s, axis=-1, keepdims=True)
                    pr = jnp.exp(s - m)
                    m_b = _unstack_heads(m, lo)
                    l_b = _unstack_heads(jnp.sum(pr, axis=-1, keepdims=True), lo)
                    o_b = _unstack_heads(_dot(pr.astype(BF), v2), lo)
                    if fresh:
                        m_ref[out, :] = m_b
                        l_ref[out, :] = l_b
                        o_ref[out, :] = o_b
                        continue
                    m_o = m_ref[out, :]
                    m_n = jnp.maximum(m_o, m_b)
                    wa, wb = jnp.exp(m_o - m_n), jnp.exp(m_b - m_n)
                    l_n = wa * l_ref[out, :] + wb * l_b
                    o_n = wa * o_ref[out, :] + wb * o_b
                    if last:
                        m_ref[out, :] = m_n + jnp.log(l_n)
                        o_ref[out, :] = o_n / l_n
                    else:
                        m_ref[out, :] = m_n
                        l_ref[out, :] = l_n
                        o_ref[out, :] = o_n

            lax.fori_loop(0, nblk // (group * segs), step, 0)

        @pl.when(h == pl.num_programs(0) - 1)
        def _():
            _Gather(list(zip(srcs, dsts)), sems).finish()

    col = lambda base: pl.BlockSpec((T, LANES), lambda h: (0, base + h))
    tok = pl.BlockSpec((T, LANES), lambda h: (0, h))
    outs = pl.pallas_call(
        body, grid=(4,),
        in_specs=[col(0), col(4), col(8), pl.BlockSpec((1, 8, LANES), lambda h: (h, 0, 0))] + [HBM] * n_g,
        out_specs=[tok, tok] + [HBM] * n_g,
        out_shape=[SDS((T, ATTN_W), F32), SDS((T, ATTN_W), F32)]
        + [SDS((N_DEV,) + g.shape, g.dtype) for g in to_gather],
        scratch_shapes=_scatter_sems(n_g) + [pltpu.VMEM((T, LANES), F32), pltpu.VMEM((6, 2 * QBLK, 2 * QBLK), F32)],
        name="attn_fwd", compiler_params=_params(("arbitrary",), 56))(proj, proj, proj, slopes, *to_gather)
    return outs[0], outs[1], outs[2:]


def _sgu_norm(zv, ln_g, ln_b):
    gz, tz = _gelu(zv)
    mu = jnp.mean(gz, axis=-1, keepdims=True)
    xc = gz - mu
    rs = lax.rsqrt(jnp.mean(xc * xc, axis=-1, keepdims=True) + EPS)
    xhat = xc * rs
    return xhat * ln_g + ln_b, xhat, rs, tz


def _causal(w):
    i = lax.broadcasted_iota(jnp.int32, (CHUNK, CHUNK), 0)
    j = lax.broadcasted_iota(jnp.int32, (CHUNK, CHUNK), 1)
    return jnp.where(i >= j, w, 0.0)


def _sgu_fwd_tile(u_ref, z_ref, g_ref, b_ref, w_ref, bs_ref, out_ref):
    for g in range(N_GROUPS):
        wm = _causal(w_ref[g]).astype(BF)
        cols = slice(g * GROUP_DIM, (g + 1) * GROUP_DIM)
        for c in range(u_ref.shape[0] // CHUNK):
            rows = slice(c * CHUNK, (c + 1) * CHUNK)
            zn, _, _, _ = _sgu_norm(z_ref[rows, cols], g_ref[...], b_ref[...])
            mixed = _dot(wm, zn.astype(BF)) + bs_ref[:, g:g + 1]
            gu, _ = _gelu(u_ref[rows, cols])
            out_ref[rows, cols] = gu * mixed


def _out_proj(attn, proj, x, ln_g, ln_b, w_s, b_st, g_a, g_s, w_out, g_pm, g_pf, tm=512):
    T = x.shape[0]

    def body(a_ref, u_ref, z_ref, x_ref, lg_ref, lb_ref, ws_ref, bs_ref, ga_ref, gs_ref, w_ref, gpm_ref, gpf_ref,
             s_ref, grp_ref, mixed_ref, h1_ref, f_ref):
        _sgu_fwd_tile(u_ref, z_ref, lg_ref, lb_ref, ws_ref, bs_ref, s_ref)
        av, sv = a_ref[...], s_ref[...]
        an = (av * _rstd(av) * ga_ref[...]).astype(BF)
        sn = (sv * _rstd(sv) * gs_ref[...]).astype(BF)
        grp_ref[:, :ATTN_W] = an
        grp_ref[:, ATTN_W:] = sn
        mixed = _dot(an, w_ref[:ATTN_W, :]) + _dot(sn, w_ref[ATTN_W:, :])
        mixed_ref[...] = mixed
        h1 = x_ref[...] + mixed * _rstd(mixed) * gpm_ref[...]
        h1_ref[...] = h1
        f_ref[...] = (h1 * _rstd(h1) * gpf_ref[...]).astype(BF)

    tok = lambda w: pl.BlockSpec((tm, w), _row)
    vec = lambda w: pl.BlockSpec((1, w), _fixed)
    return pl.pallas_call(
        body, grid=(T // tm,),
        in_specs=[tok(ATTN_W), pl.BlockSpec((tm, SGU_W), lambda i: (i, 3)), pl.BlockSpec((tm, SGU_W), lambda i: (i, 4)),
                  tok(D_MODEL), vec(GROUP_DIM), vec(GROUP_DIM), pl.BlockSpec((N_GROUPS, CHUNK, CHUNK), lambda i: (0, 0, 0)),
                  pl.BlockSpec((CHUNK, LANES), _fixed), vec(ATTN_W), vec(SGU_W),
                  pl.BlockSpec((D_MODEL, D_MODEL), _fixed), vec(D_MODEL), vec(D_MODEL)],
        out_specs=[tok(SGU_W)] + [tok(D_MODEL)] * 4,
        out_shape=[SDS((T, SGU_W), F32), SDS((T, D_MODEL), BF), SDS((T, D_MODEL), F32), SDS((T, D_MODEL), F32),
                   SDS((T, D_MODEL), BF)],
        name="out_proj", compiler_params=_params(("arbitrary",), 52))(
            attn, proj, proj, x, ln_g, ln_b, w_s, b_st, g_a, g_s, w_out, g_pm, g_pf)


FF_TILE = 1408
FF_TILES = D_FF // FF_TILE
FF_CHUNK = 256
FETCH_DEPTH = 3


def _gate_up(f, w_gu_t, tm=512):
    T = f.shape[0]
    tn = FF_TILE

    def body(f_ref, wg_ref, wu_ref, g_ref, u_ref, act_ref):
        fv = f_ref[...]
        g = _dot_nt(fv, wg_ref[...])
        u = _dot_nt(fv, wu_ref[...])
        g_ref[...] = g.astype(BF)
        u_ref[...] = u.astype(BF)
        act_ref[...] = (g * _sigmoid(g) * u).astype(BF)

    ospec = pl.BlockSpec((tm, tn), lambda j, i: (i, j))
    return pl.pallas_call(
        body, grid=(FF_TILES, T // tm),
        in_specs=[pl.BlockSpec((tm, D_MODEL), lambda j, i: (i, 0)), pl.BlockSpec((tn, D_MODEL), lambda j, i: (j, 0)),
                  pl.BlockSpec((tn, D_MODEL), lambda j, i: (j + FF_TILES, 0))],
        out_specs=[ospec] * 3, out_shape=[SDS((T, D_FF), BF)] * 3,
        name="gate_up", compiler_params=_params(("arbitrary", "arbitrary"), 40))(f, w_gu_t, w_gu_t)


def _down_proj(act, w_down, h1, g_pff, tm=512):
    T = act.shape[0]

    def body(a_ref, w_ref, h1_ref, g_ref, y_ref, h2_ref):
        y = _dot(a_ref[...], w_ref[...])
        y_ref[...] = y
        h2_ref[...] = h1_ref[...] + y * _rstd(y) * g_ref[...]

    return pl.pallas_call(
        body, grid=(T // tm,),
        in_specs=[pl.BlockSpec((tm, D_FF), _row), pl.BlockSpec((D_FF, D_MODEL), _fixed),
                  pl.BlockSpec((tm, D_MODEL), _row), pl.BlockSpec((1, D_MODEL), _fixed)],
        out_specs=[pl.BlockSpec((tm, D_MODEL), _row)] * 2,
        out_shape=[SDS((T, D_MODEL), F32)] * 2,
        name="down_proj", compiler_params=_params(("arbitrary",), 48))(act, w_down, h1, g_pff)


def _pe_loss_and_bwd(h2, p, target, y, w_peg, b_peg, w_pep_t, g_pff, tm=512):
    T = h2.shape[0]

    def body(h2_ref, p_ref, t_ref, y_ref, wg_ref, b_ref, wp_ref, g_ref,
             dh2_ref, dy_ref, gpeg_ref, gpep_ref, loss_ref, db_ref, dg_ref):
        _acc_init(pl.program_id(0), gpeg_ref, gpep_ref, loss_ref, db_ref, dg_ref)
        h2v = h2_ref[...]
        h2b = h2v.astype(BF)
        pb = p_ref[...].astype(BF)
        gate = _sigmoid(_dot(h2b, wg_ref[...]) + b_ref[...])
        pp = _dot_nt(pb, wp_ref[...])
        diff = h2v + gate * pp - t_ref[...]
        loss_ref[...] += _colsum(diff * diff)
        dh3 = diff * (1.0 / D_MODEL)
        dpre = dh3 * pp * (gate * (1.0 - gate))
        dpre_b = dpre.astype(BF)
        db_ref[...] += _colsum(dpre)
        gpeg_ref[...] += _dot_tn(h2b, dpre_b)
        gpep_ref[...] += _dot_tn((dh3 * gate).astype(BF), pb)
        dh2 = dh3 + _dot_nt(dpre_b, wg_ref[...])
        dh2_ref[...] = dh2
        dy, dg = _rms_bwd(dh2, y_ref[...], g_ref[...])
        dy_ref[...] = dy.astype(BF)
        dg_ref[...] += _colsum(dg)

    tok = lambda w: pl.BlockSpec((tm, w), _row)
    vec = pl.BlockSpec((1, D_MODEL), _fixed)
    wg = pl.BlockSpec((D_MODEL, D_MODEL), _fixed)
    wp = pl.BlockSpec((D_MODEL, PLE_DIM), _fixed)
    return pl.pallas_call(
        body, grid=(T // tm,),
        in_specs=[tok(D_MODEL), tok(PLE_DIM), tok(D_MODEL), tok(D_MODEL), wg, vec, wp, vec],
        out_specs=[tok(D_MODEL), tok(D_MODEL), wg, wp, vec, vec, vec],
        out_shape=[SDS((T, D_MODEL), F32), SDS((T, D_MODEL), BF), SDS((D_MODEL, D_MODEL), F32),
                   SDS((D_MODEL, PLE_DIM), F32)] + [SDS((1, D_MODEL), F32)] * 3,
        name="pe_loss_and_bwd", compiler_params=_params(("arbitrary",), 56))(
            h2, p, target, y, w_peg, b_peg, w_pep_t, g_pff)


def _weight_grad(a, dy, name, into=None, row_tile=0, rows=None, tk=1024):
    n = dy.shape[1]
    tn = min(n, 1024)
    T, ka = a.shape
    tka = FF_TILE if ka == D_FF else min(ka, 1024)
    rows = ka if rows is None else rows

    def body(a_ref, dy_ref, *rest):
        out_ref = rest[-1]
        _acc_init(pl.program_id(2), out_ref)
        out_ref[...] += _dot_tn(a_ref[...].astype(BF), dy_ref[...].astype(BF))

    carried = [] if into is None else [into]
    return pl.pallas_call(
        body, grid=(ka // tka, n // tn, T // tk),
        in_specs=[pl.BlockSpec((tk, tka), lambda i, j, k: (k, i)), pl.BlockSpec((tk, tn), lambda i, j, k: (k, j))]
        + [HBM] * len(carried),
        out_specs=pl.BlockSpec((tka, tn), lambda i, j, k: (i + row_tile, j)),
        out_shape=SDS((rows, n), F32), input_output_aliases={2: 0} if carried else {},
        name="grad_" + name, compiler_params=_params(("arbitrary",) * 3, 40))(a, dy, *carried)


def _grad_w_in(dparts, a, tk=1024):
    T = a.shape[0]

    def body(*refs):
        d_refs, a_ref, out_ref, acc_ref = refs[:len(dparts)], refs[-3], refs[-2], refs[-1]
        k = pl.program_id(0)
        _acc_init(k, acc_ref)
        cols = [r[part].astype(BF) for r in d_refs for part in range(r.shape[0])]
        acc_ref[...] += _dot_tn(jnp.concatenate(cols, axis=1), a_ref[...])

        @pl.when(k == pl.num_programs(0) - 1)
        def _():
            out_ref[...] = acc_ref[...].astype(BF)

    return pl.pallas_call(
        body, grid=(T // tk,),
        in_specs=[pl.BlockSpec((d.shape[0], tk, d.shape[2]), lambda k: (0, k, 0)) for d in dparts]
        + [pl.BlockSpec((tk, D_MODEL), lambda k: (k, 0))],
        out_specs=pl.BlockSpec((PROJ, D_MODEL), lambda k: (0, 0)),
        out_shape=SDS((PROJ, D_MODEL), BF), scratch_shapes=[pltpu.VMEM((PROJ, D_MODEL), F32)],
        name="grad_w_in", compiler_params=_params(("arbitrary",), 48))(*dparts, a)


def _down_bwd(dy, w_down, g, u, to_send, tm=512):
    T = dy.shape[0]
    n_s = len(to_send)

    def body(dy_ref, w_ref, g_hbm, u_hbm, *rest):
        srcs, (dg_ref, du_ref), dsts = rest[:n_s], rest[n_s:n_s + 2], rest[n_s + 2:2 * n_s + 2]
        sems, (g_buf, u_buf, fetch_sem) = rest[2 * n_s + 2:-3], rest[-3:]
        i = pl.program_id(0)
        items = list(zip(srcs, dsts, [True] * n_s))

        def fetch(t):
            slot = t % FETCH_DEPTH
            rows = pl.ds(t * tm if isinstance(t, int) else pl.multiple_of(t * tm, tm), tm)
            return [pltpu.make_async_copy(src.at[rows], buf.at[slot], fetch_sem.at[j, slot])
                    for j, (src, buf) in enumerate(((g_hbm, g_buf), (u_hbm, u_buf)))]

        @pl.when(i == 0)
        def _():
            _Scatter(items, sems).start()
            for t in range(FETCH_DEPTH - 1):
                for copy in fetch(t):
                    copy.start()

        @pl.when(i + FETCH_DEPTH - 1 < pl.num_programs(0))
        def _():
            for copy in fetch(i + FETCH_DEPTH - 1):
                copy.start()

        for copy in fetch(i):
            copy.wait()
        g_ref, u_ref = g_buf.at[i % FETCH_DEPTH], u_buf.at[i % FETCH_DEPTH]
        dyv = dy_ref[...]
        for c in range(D_FF // FF_CHUNK):
            cols = slice(c * FF_CHUNK, (c + 1) * FF_CHUNK)
            dact = _dot_nt(dyv, w_ref[cols, :]).astype(BF)
            gv, uv = g_ref[:, cols], u_ref[:, cols]
            s = _sigmoid(gv)
            ds = dact * s
            dg_ref[:, cols] = ds * uv * (1.0 + gv * (1.0 - s))
            du_ref[:, cols] = ds * gv

        @pl.when(i == pl.num_programs(0) - 1)
        def _():
            _Scatter(items, sems).wait()

    tile = pl.BlockSpec((tm, D_FF), _row)
    ring = pltpu.VMEM((FETCH_DEPTH, tm, D_FF), BF)
    outs = pl.pallas_call(
        body, grid=(T // tm,),
        in_specs=[pl.BlockSpec((tm, D_MODEL), _row),
                  pl.BlockSpec((D_FF, D_MODEL), _fixed, pipeline_mode=pl.Buffered(1)), HBM, HBM] + [HBM] * n_s,
        out_specs=[tile, tile] + [HBM] * n_s,
        out_shape=[SDS((T, D_FF), BF)] * 2 + [SDS(s.shape, s.dtype) for s in to_send],
        scratch_shapes=list(_scatter_sems(n_s)) + [ring, ring, DMA_SEMS((2, FETCH_DEPTH))],
        name="down_bwd", compiler_params=_params(("arbitrary",), 56))(dy, w_down, g, u, *to_send)
    return outs[0], outs[1], outs[2:]


def _ffn_in_bwd(dg, du, w_gu_t, h1, dh2, mixed, g_pf, g_pm, to_send, tm=512):
    T = h1.shape[0]
    n_s = len(to_send)

    def body(dg_hbm, du_hbm, w_ref, h1_ref, dh2_ref, mx_ref, gpf_ref, gpm_ref, *rest):
        srcs, outs, dsts = rest[:n_s], rest[n_s:n_s + 4], rest[n_s + 4:2 * n_s + 4]
        sems, (d_buf, fetch_sem) = rest[2 * n_s + 4:-2], rest[-2:]
        dh1_ref, dmx_ref, dgpf_ref, dgpm_ref = outs
        i = pl.program_id(0)
        items = list(zip(srcs, dsts, [True] * n_s))
        _acc_init(i, dgpf_ref, dgpm_ref)

        def fetch(t):
            slot = t % FETCH_DEPTH
            rows = pl.ds(t * tm if isinstance(t, int) else pl.multiple_of(t * tm, tm), tm)
            return [pltpu.make_async_copy(src.at[rows], d_buf.at[slot, :, pl.ds(j * D_FF, D_FF)], fetch_sem.at[j, slot])
                    for j, src in enumerate((dg_hbm, du_hbm))]

        @pl.when(i == 0)
        def _():
            _Scatter(items, sems).start()
            for t in range(FETCH_DEPTH - 1):
                for copy in fetch(t):
                    copy.start()

        @pl.when(i + FETCH_DEPTH - 1 < pl.num_programs(0))
        def _():
            for copy in fetch(i + FETCH_DEPTH - 1):
                copy.start()

        for copy in fetch(i):
            copy.wait()
        df = _dot(d_buf[i % FETCH_DEPTH], w_ref[...])
        dx, dgf = _rms_bwd(df, h1_ref[...], gpf_ref[...])
        dh1 = dh2_ref[...] + dx
        dh1_ref[...] = dh1
        dmx, dgm = _rms_bwd(dh1, mx_ref[...], gpm_ref[...])
        dmx_ref[...] = dmx.astype(BF)
        dgpf_ref[...] += _colsum(dgf)
        dgpm_ref[...] += _colsum(dgm)

        @pl.when(i == pl.num_programs(0) - 1)
        def _():
            _Scatter(items, sems).wait()

    tok = lambda w: pl.BlockSpec((tm, w), _row)
    vec = pl.BlockSpec((1, D_MODEL), _fixed)
    outs = pl.pallas_call(
        body, grid=(T // tm,),
        in_specs=[HBM, HBM, pl.BlockSpec((2 * D_FF, D_MODEL), _fixed, pipeline_mode=pl.Buffered(1)),
                  tok(D_MODEL), tok(D_MODEL), tok(D_MODEL), vec, vec]
        + [HBM] * n_s,
        out_specs=[tok(D_MODEL), tok(D_MODEL), vec, vec] + [HBM] * n_s,
        out_shape=[SDS((T, D_MODEL), F32), SDS((T, D_MODEL), BF), SDS((1, D_MODEL), F32), SDS((1, D_MODEL), F32)]
        + [SDS(s.shape, s.dtype) for s in to_send],
        scratch_shapes=list(_scatter_sems(n_s)) + [pltpu.VMEM((FETCH_DEPTH, tm, 2 * D_FF), BF),
                                                   DMA_SEMS((2, FETCH_DEPTH))],
        name="ffn_in_bwd", compiler_params=_params(("arbitrary",), 56))(
            dg, du, w_gu_t, h1, dh2, mixed, g_pf, g_pm, *to_send)
    return outs[0], outs[1], outs[2], outs[3], outs[4:]


STAT_LANES = HEAD_DIM // 2


def _ff_grad_spec(half):
    return pl.BlockSpec((D_FF, D_MODEL), lambda i: (half, 0), pipeline_mode=pl.Buffered(1))


def _out_bwd(dmx, w_out, attn, lse, sgu, g_a, g_s, dg, f, tm=512):
    T = attn.shape[0]

    def body(dm_ref, w_ref, a_ref, l_ref, s_ref, ga_ref, gs_ref, dgate_ref, f_ref,
             da_ref, st_ref, ds_ref, dga_ref, dgs_ref, ggu_ref):
        _acc_init(pl.program_id(0), dga_ref, dgs_ref, ggu_ref)
        ggu_ref[...] += _dot_tn(dgate_ref[...], f_ref[...])
        dgr = _dot_nt(dm_ref[...], w_ref[...])
        av = a_ref[...]
        da, dga = _rms_bwd(dgr[:, :ATTN_W], av, ga_ref[...])
        ds, dgs = _rms_bwd(dgr[:, ATTN_W:], s_ref[...], gs_ref[...])
        da_ref[...] = da
        ds_ref[...] = ds
        dga_ref[...] += _colsum(dga)
        dgs_ref[...] += _colsum(dgs)
        lane = lax.broadcasted_iota(jnp.int32, (1, LANES), 1)
        lo = lane < HEAD_DIM
        first = (lane % HEAD_DIM) < STAT_LANES
        prod = da * av
        for c in range(ATTN_W // LANES):
            cols = slice(c * LANES, (c + 1) * LANES)
            pc = prod[:, cols]
            delta = jnp.where(lo, jnp.sum(jnp.where(lo, pc, 0.0), axis=-1, keepdims=True),
                              jnp.sum(jnp.where(lo, 0.0, pc), axis=-1, keepdims=True))
            st_ref[:, cols] = jnp.where(first, l_ref[:, cols], delta)

    tok = lambda w: pl.BlockSpec((tm, w), _row)
    vec = lambda w: pl.BlockSpec((1, w), _fixed)
    return pl.pallas_call(
        body, grid=(T // tm,),
        in_specs=[tok(D_MODEL), pl.BlockSpec((D_MODEL, D_MODEL), _fixed), tok(ATTN_W), tok(ATTN_W), tok(SGU_W),
                  vec(ATTN_W), vec(SGU_W), tok(D_FF), tok(D_MODEL)],
        out_specs=[tok(ATTN_W), tok(ATTN_W), tok(SGU_W), vec(ATTN_W), vec(SGU_W), _ff_grad_spec(0)],
        out_shape=[SDS((T, ATTN_W), F32), SDS((T, ATTN_W), F32), SDS((T, SGU_W), F32), SDS((1, ATTN_W), F32),
                   SDS((1, SGU_W), F32), SDS((2 * D_FF, D_MODEL), F32)],
        name="out_bwd", compiler_params=_params(("arbitrary",), 56))(dmx, w_out, attn, lse, sgu, g_a, g_s, dg, f)


def _sgu_bwd(proj, dsgu, ln_g, ln_b, w_s, b_st, groups, dmx, d_up, f, g_gu_t, tm=512):
    T = proj.shape[0]

    def body(u_ref, z_ref, ds_ref, g_ref, b_ref, w_ref, bs_ref, grp_ref, dmx_ref, dup_ref, f_ref, _,
             duz_ref, dw_ref, dbs_ref, dlg_ref, dlb_ref, gout_ref, ggu_ref, dbacc_ref):
        du_ref, dz_ref = duz_ref.at[0], duz_ref.at[1]
        step = pl.program_id(0)
        _acc_init(step, dw_ref, dbs_ref, dlg_ref, dlb_ref, gout_ref, ggu_ref, dbacc_ref)
        gout_ref[...] += _dot_tn(grp_ref[...], dmx_ref[...])
        ggu_ref[...] += _dot_tn(dup_ref[...], f_ref[...])
        lng, lnb = g_ref[...], b_ref[...]
        for g in range(N_GROUPS):
            wm = _causal(w_ref[g]).astype(BF)
            cols = slice(g * GROUP_DIM, (g + 1) * GROUP_DIM)
            for c in range(tm // CHUNK):
                rows = slice(c * CHUNK, (c + 1) * CHUNK)
                zv, uv, dout = z_ref[rows, cols], u_ref[rows, cols], ds_ref[rows, cols]
                zn, xhat, rs, tz = _sgu_norm(zv, lng, lnb)
                znb = zn.astype(BF)
                mixed = _dot(wm, znb) + bs_ref[:, g:g + 1]
                gu, tu = _gelu(uv)
                du_ref[rows, cols] = (dout * mixed * _gelu_grad(uv, tu)).astype(BF)
                dmix = dout * gu
                dmb = dmix.astype(BF)
                dw_ref[g] += _causal(_dot_nt(dmb, znb))
                dbacc_ref[g] += dmix
                dzn = _dot_tn(wm, dmb)
                dlg_ref[...] += _colsum(dzn * xhat)
                dlb_ref[...] += _colsum(dzn)
                dxh = dzn * lng
                dgz = rs * (dxh - jnp.mean(dxh, axis=-1, keepdims=True)
                            - xhat * jnp.mean(dxh * xhat, axis=-1, keepdims=True))
                dz_ref[rows, cols] = (dgz * _gelu_grad(zv, tz)).astype(BF)

        @pl.when(step == pl.num_programs(0) - 1)
        def _():
            lane = lax.broadcasted_iota(jnp.int32, (CHUNK, LANES), 1)
            acc = jnp.zeros((CHUNK, LANES), F32)
            for g in range(N_GROUPS):
                acc = jnp.where(lane == g, jnp.sum(dbacc_ref[g], axis=-1, keepdims=True), acc)
            dbs_ref[...] = acc

    tok = pl.BlockSpec((tm, SGU_W), _row)
    vec = pl.BlockSpec((1, GROUP_DIM), _fixed)
    wsp = pl.BlockSpec((N_GROUPS, CHUNK, CHUNK), lambda i: (0, 0, 0))
    sq = pl.BlockSpec((CHUNK, LANES), _fixed)
    wide = pl.BlockSpec((tm, D_MODEL), _row)
    return pl.pallas_call(
        body, grid=(T // tm,),
        in_specs=[pl.BlockSpec((tm, SGU_W), lambda i: (i, 3)), pl.BlockSpec((tm, SGU_W), lambda i: (i, 4)), tok,
                  vec, vec, wsp, sq, wide, wide, pl.BlockSpec((tm, D_FF), _row), wide, HBM],
        out_specs=[pl.BlockSpec((2, tm, SGU_W), lambda i: (0, i, 0)), wsp, sq, vec, vec,
                   pl.BlockSpec((D_MODEL, D_MODEL), _fixed), _ff_grad_spec(1)],
        out_shape=[SDS((2, T, SGU_W), BF), SDS((N_GROUPS, CHUNK, CHUNK), F32),
                   SDS((CHUNK, LANES), F32), SDS((1, GROUP_DIM), F32), SDS((1, GROUP_DIM), F32),
                   SDS((D_MODEL, D_MODEL), F32), SDS((2 * D_FF, D_MODEL), F32)],
        input_output_aliases={11: 6},
        scratch_shapes=[pltpu.VMEM((N_GROUPS, CHUNK, LANES), F32)],
        name="sgu_bwd", compiler_params=_params(("arbitrary",), 56))(
            proj, proj, dsgu, ln_g, ln_b, w_s, b_st, groups, dmx, d_up, f, g_gu_t)


def _attn_bwd(proj, do, stats, slopes, to_send, slabbed):
    T = proj.shape[0]
    nblk = T // QBLK
    n_s = len(to_send)

    def body(q_ref, k_ref, v_ref, do_ref, st_ref, sl_ref, *rest):
        srcs, d_ref, dsts = rest[:n_s], rest[n_s], rest[n_s + 1:2 * n_s + 1]
        sems, bias_ref = rest[2 * n_s + 1:2 * n_s + 4], rest[2 * n_s + 4]
        dq_ref, dk_ref, dv_ref = d_ref.at[0], d_ref.at[1], d_ref.at[2]
        h = pl.program_id(0)
        items = list(zip(srcs, dsts, slabbed))

        @pl.when(h == 0)
        def _():
            _Scatter(items, sems).start()

        _attn_bias(sl_ref, bias_ref)
        lo = lax.broadcasted_iota(jnp.int32, (1, LANES), 1) < HEAD_DIM
        scale = HEAD_DIM ** -0.5
        d_ref[...] = jnp.zeros_like(d_ref)

        for di, d in enumerate(DILATIONS):
            group, segs = _attn_plan(nblk, d)

            def step(i, carry, segs=segs, **kw):
                for s in range(segs):
                    segment(i * segs + s, **kw)
                return carry

            def segment(i, d=d, di=di, group=group):
                start, pstart, first = _attn_group_index(i, nblk, d, group)
                rows, prows = _attn_rows(start, d, group), _attn_rows(pstart, d)
                alone = group == nblk // d
                q = q_ref[rows, :] * scale
                if alone:
                    k, v, own0 = k_ref[rows, :].astype(BF), v_ref[rows, :].astype(BF), 0
                else:
                    k = jnp.concatenate([k_ref[prows, :], k_ref[rows, :]], axis=0).astype(BF)
                    v = jnp.concatenate([v_ref[prows, :], v_ref[rows, :]], axis=0).astype(BF)
                    own0 = QBLK
                dov = do_ref[rows, :]
                stats = st_ref[rows, :]
                masks = [lo, ~lo]
                qm = [jnp.where(masks[j], q, 0.0).astype(BF) for j in range(2)]
                dom = [jnp.where(masks[j], dov, 0.0).astype(BF) for j in range(2)]
                for b in range(group):
                    qb = slice(b * QBLK, (b + 1) * QBLK)
                    own_only = alone and b == 0
                    kb = slice(own0 + (b if own_only else b - 1) * QBLK, own0 + (b + 1) * QBLK)
                    which = di * 2 + first.astype(jnp.int32) if b == 0 and not own_only else di * 2
                    dq_parts, prs, dss = [], [], []
                    for j in range(2):
                        bias = bias_ref[which, j * QBLK:(j + 1) * QBLK, QBLK if own_only else 0:]
                        lj = stats[qb, j * HEAD_DIM:j * HEAD_DIM + 1]
                        delta = stats[qb, j * HEAD_DIM + STAT_LANES:j * HEAD_DIM + STAT_LANES + 1]
                        pr = jnp.exp(_dot_nt(qm[j][qb], k[kb]) + bias - lj)
                        ds = (pr * (_dot_nt(dom[j][qb], v[kb]) - delta)).astype(BF)
                        dq_parts.append(_dot(ds, k[kb]))
                        prs.append(pr.astype(BF))
                        dss.append(ds)
                    dk_b = _dot_tn(jnp.concatenate(dss, axis=0), jnp.concatenate([qm[0][qb], qm[1][qb]], axis=0))
                    dv_b = _dot_tn(jnp.concatenate(prs, axis=0), jnp.concatenate([dom[0][qb], dom[1][qb]], axis=0))
                    own = _attn_rows(start + b * (d * QBLK), d)
                    dq_ref[own, :] += jnp.where(lo, dq_parts[0], dq_parts[1]) * scale
                    if own_only:
                        dk_ref[own, :] += dk_b
                        dv_ref[own, :] += dv_b
                    elif b == 0:
                        dk_ref[prows, :] += dk_b[:QBLK]
                        dv_ref[prows, :] += dv_b[:QBLK]
                        dk_ref[own, :] += dk_b[QBLK:]
                        dv_ref[own, :] += dv_b[QBLK:]
                    else:
                        two = _attn_rows(start + (b - 1) * (d * QBLK), d, 2)
                        dk_ref[two, :] += dk_b
                        dv_ref[two, :] += dv_b

            lax.fori_loop(0, nblk // (group * segs), step, 0)

        @pl.when(h == pl.num_programs(0) - 1)
        def _():
            _Scatter(items, sems).wait()

    col = lambda base: pl.BlockSpec((T, LANES), lambda h: (0, base + h))
    outs = pl.pallas_call(
        body, grid=(4,),
        in_specs=[col(0), col(4), col(8), col(0), col(0), pl.BlockSpec((1, 8, LANES), lambda h: (h, 0, 0))]
        + [HBM] * n_s,
        out_specs=[pl.BlockSpec((3, T, LANES), lambda h: (0, 0, h), pipeline_mode=pl.Buffered(1))] + [HBM] * n_s,
        out_shape=[SDS((3, T, ATTN_W), F32)]
        + [SDS(s.shape if sl else (N_DEV,) + s.shape, s.dtype) for s, sl in zip(to_send, slabbed)],
        scratch_shapes=_scatter_sems(n_s) + [pltpu.VMEM((6, 2 * QBLK, 2 * QBLK), F32)],
        name="attn_bwd", compiler_params=_params(("arbitrary",), 60))(proj, proj, proj, do, stats, slopes, *to_send)
    return outs[0], outs[1:]


def _in_bwd(dparts, w_in_t, x, dh1, g1, tm=512):
    T = x.shape[0]
    n = len(dparts)
    w = ATTN_W

    def body(*refs):
        d_refs, (w_ref, x_ref, dh1_ref, g_ref, dx_ref, dg_ref) = refs[:n], refs[n:]
        _acc_init(pl.program_id(0), dg_ref)
        d_proj = jnp.concatenate([r[part].astype(BF) for r in d_refs for part in range(r.shape[0])], axis=1)
        da = _dot(d_proj, w_ref[...])
        dx, dg = _rms_bwd(da, x_ref[...], g_ref[...])
        dx_ref[...] = dh1_ref[...] + dx
        dg_ref[...] += _colsum(dg)

    tok = lambda c: pl.BlockSpec((tm, c), _row)
    vec = pl.BlockSpec((1, D_MODEL), _fixed)
    return pl.pallas_call(
        body, grid=(T // tm,),
        in_specs=[pl.BlockSpec((d.shape[0], tm, w), lambda i: (0, i, 0)) for d in dparts]
        + [pl.BlockSpec((PROJ, D_MODEL), _fixed), tok(D_MODEL), tok(D_MODEL), vec],
        out_specs=[tok(D_MODEL), vec],
        out_shape=[SDS((T, D_MODEL), F32), SDS((1, D_MODEL), F32)],
        name="in_bwd", compiler_params=_params(("arbitrary",), 52))(*dparts, w_in_t, x, dh1, g1)


def _sum_parts(p_ref):
    g = p_ref[0].astype(F32)
    for s in range(1, N_DEV):
        g = g + p_ref[s].astype(F32)
    return g


def _adamw_math(g, w, m, v):
    nm = ADAM_B1 * m + (1.0 - ADAM_B1) * g
    nv = ADAM_B2 * v + (1.0 - ADAM_B2) * (g * g)
    m_hat = nm / (1.0 - ADAM_B1 ** ADAM_STEP)
    v_hat = nv / (1.0 - ADAM_B2 ** ADAM_STEP)
    return -ADAM_LR * (m_hat / (jnp.sqrt(v_hat) + ADAM_EPS) + ADAM_WD * w), nm, nv


def _row_tile(rows):
    for t in (256, 176, 128, 80):
        if rows % t == 0:
            return t
    raise ValueError(rows)


def _reduce_adamw(parts, w, m, v, name):
    rows, width = w.shape
    tr = _row_tile(rows)

    def body(p_ref, w_ref, m_ref, v_ref, g_ref, d_ref, nm_ref, nv_ref):
        g = _sum_parts(p_ref)
        g_ref[...] = g
        d_ref[...], nm_ref[...], nv_ref[...] = _adamw_math(g, w_ref[...], m_ref[...], v_ref[...])

    blk = pl.BlockSpec((tr, width), _row)
    return pl.pallas_call(
        body, grid=(rows // tr,),
        in_specs=[pl.BlockSpec((N_DEV, tr, width), lambda i: (0, i, 0)), blk, blk, blk],
        out_specs=[blk] * 4, out_shape=[SDS((rows, width), F32)] * 4,
        name="adamw_" + name, compiler_params=_params(("arbitrary",), 32))(parts, w, m, v)


def _reduce(parts, name):
    _, rows, width = parts.shape
    tr = _row_tile(rows)

    def body(p_ref, g_ref):
        g_ref[...] = _sum_parts(p_ref)

    return pl.pallas_call(
        body, grid=(rows // tr,),
        in_specs=[pl.BlockSpec((N_DEV, tr, width), lambda i: (0, i, 0))],
        out_specs=pl.BlockSpec((tr, width), _row), out_shape=SDS((rows, width), F32),
        name="sum_" + name, compiler_params=_params(("arbitrary",), 32))(parts)


def _adamw(g, w, m, v, name):
    rows, width = w.shape
    tr = _row_tile(rows)

    def body(g_ref, w_ref, m_ref, v_ref, d_ref, nm_ref, nv_ref):
        d_ref[...], nm_ref[...], nv_ref[...] = _adamw_math(g_ref[...], w_ref[...], m_ref[...], v_ref[...])

    blk = pl.BlockSpec((tr, width), _row)
    return pl.pallas_call(
        body, grid=(rows // tr,), in_specs=[blk] * 4, out_specs=[blk] * 3, out_shape=[SDS((rows, width), F32)] * 3,
        name="adamw_" + name, compiler_params=_params(("arbitrary",), 32))(g, w, m, v)


SMALL = ("w_spatial", "ln_pre_mix", "ln_post_mix", "ln_pre_ffn", "ln_post_ffn", "b_pe_gate",
         "attn_out_norm", "sgu_out_norm", "b_spatial", "sgu_ln_g", "sgu_ln_b")
SMALL_GROUPS = ((128, ("w_spatial", "b_spatial", "sgu_ln_g", "sgu_ln_b")),
                (512, ("attn_out_norm", "sgu_out_norm")),
                (1024, ("ln_post_mix", "ln_pre_ffn", "ln_post_ffn", "b_pe_gate")))
SMALL_LATE = "ln_pre_mix"
SMALL_SIZE = dict(w_spatial=N_GROUPS * CHUNK * CHUNK, b_spatial=N_GROUPS * CHUNK, sgu_ln_g=GROUP_DIM, sgu_ln_b=GROUP_DIM,
                  attn_out_norm=ATTN_W, sgu_out_norm=SGU_W, ln_pre_mix=D_MODEL, ln_post_mix=D_MODEL, ln_pre_ffn=D_MODEL,
                  ln_post_ffn=D_MODEL, b_pe_gate=D_MODEL)
SUBLANES = 8
ROW_SHARDED = ("w_out", "w_down", "w_pe_gate")
COL_SHARDED = ("w_in", "w_gate_up", "w_pe_proj")
WEIGHTS = ("ln_pre_mix", "w_in", "sgu_ln_g", "sgu_ln_b", "w_spatial", "b_spatial", "attn_out_norm", "sgu_out_norm",
           "w_out", "ln_post_mix", "ln_pre_ffn", "w_gate_up", "w_down", "ln_post_ffn", "w_pe_gate", "b_pe_gate",
           "w_pe_proj")


def _group_rows(width, names, extra=0):
    rows = sum(SMALL_SIZE[n] // width for n in names) + extra
    return -(-rows // SUBLANES) * SUBLANES


def _pack_small_grads(gs, loss_term):
    packed = []
    for width, names in SMALL_GROUPS:
        rows = [gs[n].reshape(-1, width) for n in names]
        extra = int(width == D_MODEL)
        if extra:
            rows.append(jnp.full((1, width), loss_term, F32))
        used = sum(r.shape[0] for r in rows)
        rows.append(jnp.zeros((_group_rows(width, names, extra) - used, width), F32))
        packed.append(jnp.concatenate(rows, axis=0))
    return packed


def _small_adamw(arrived, arrived_late, w, m, v):
    names = [n for _, ns in SMALL_GROUPS for n in ns] + [SMALL_LATE]
    n_groups = len(SMALL_GROUPS)

    def body(*refs):
        group_refs, late_ref = refs[:n_groups], refs[n_groups]
        state = refs[n_groups + 1:n_groups + 1 + 3 * len(names)]
        outs = refs[n_groups + 1 + 3 * len(names):]
        sums = [_sum_parts(r) for r in group_refs]

        def update(name, g):
            i = names.index(name)
            w_ref, m_ref, v_ref = state[3 * i:3 * i + 3]
            delta, nm, nv = _adamw_math(g, w_ref[...].reshape(g.shape), m_ref[...].reshape(g.shape),
                                        v_ref[...].reshape(g.shape))
            for o_ref, val in zip(outs[4 * i:4 * i + 4], (g, delta, nm, nv)):
                o_ref[...] = val.reshape(o_ref.shape)

        for (width, group), total in zip(SMALL_GROUPS, sums):
            row = 0
            for name in group:
                rows = SMALL_SIZE[name] // width
                update(name, total[row:row + rows, :])
                row += rows
            if width == D_MODEL:
                outs[-1][...] = total[row:row + 1, :LANES]
        update(SMALL_LATE, _sum_parts(late_ref)[:1, :])

    state = [t[n] for n in names for t in (w, m, v)]
    plain = jax.ShapeDtypeStruct
    out_shape = [plain(w[n].shape, F32) for n in names for _ in range(4)] + [plain((1, LANES), F32)]
    outs = pl.pallas_call(body, out_shape=out_shape, name="adamw_small",
                          compiler_params=pltpu.CompilerParams(vmem_limit_bytes=32 * MIB))(*arrived, arrived_late, *state)
    return {n: outs[4 * i:4 * i + 4] for i, n in enumerate(names)}, outs[-1]


def _slabs(full):
    return full.reshape(N_DEV, full.shape[0] // N_DEV, full.shape[1])


def kernel(x, p, ln_pre_mix, w_in, sgu_ln_g, sgu_ln_b, w_spatial, b_spatial, attn_out_norm, sgu_out_norm, w_out, ln_post_mix, ln_pre_ffn, w_gate_up, w_down, ln_post_ffn, w_pe_gate, b_pe_gate, w_pe_proj, loss_target, m_ln_pre_mix, m_w_in, m_sgu_ln_g, m_sgu_ln_b, m_w_spatial, m_b_spatial, m_attn_out_norm, m_sgu_out_norm, m_w_out, m_ln_post_mix, m_ln_pre_ffn, m_w_gate_up, m_w_down, m_ln_post_ffn, m_w_pe_gate, m_b_pe_gate, m_w_pe_proj, v_ln_pre_mix, v_w_in, v_sgu_ln_g, v_sgu_ln_b, v_w_spatial, v_b_spatial, v_attn_out_norm, v_sgu_out_norm, v_w_out, v_ln_post_mix, v_ln_pre_ffn, v_w_gate_up, v_w_down, v_ln_post_ffn, v_w_pe_gate, v_b_pe_gate, v_w_pe_proj):
    given = dict(locals())
    w = {n: given[n] for n in WEIGHTS}
    m = {n: given["m_" + n] for n in WEIGHTS}
    v = {n: given["v_" + n] for n in WEIGHTS}
    xs, ps, target = x[0], p[0, 0], loss_target[0]

    shard = {n: w[n][0].astype(BF) for n in ROW_SHARDED}
    shard.update({n: w[n][0].T.astype(BF) for n in COL_SHARDED})
    sm = {n: w[n][0] for n in SMALL}
    sm = {n: (a.reshape(1, -1) if a.ndim == 1 else a) for n, a in sm.items()}
    slopes = jnp.broadcast_to((2.0 ** -(jnp.arange(8, dtype=F32) + 1.0)).reshape(4, 2, 1), (4, 2, LANES))
    slopes = jnp.concatenate([slopes, jnp.zeros((4, 6, LANES), F32)], axis=1)
    b_st = jnp.pad(sm["b_spatial"].T, ((0, 0), (0, LANES - N_GROUPS)))

    def full(gathered):
        return gathered.reshape(-1, gathered.shape[-1])

    w_in_t = full(_all_gather(shard["w_in"], "gather_w_in"))
    proj, a = _in_proj(xs, sm["ln_pre_mix"], w_in_t)
    later = ("w_out", "w_gate_up", "w_down", "w_pe_gate", "w_pe_proj")
    attn, lse, gathered = _attn_fwd(proj, slopes, [shard[n] for n in later])
    w_out_f, w_gu_t, w_down_f, w_peg_f, w_pep_t = [full(g) for g in gathered]
    sgu, groups, mixed, h1, f = _out_proj(attn, proj, xs, sm["sgu_ln_g"], sm["sgu_ln_b"], sm["w_spatial"], b_st,
                                          sm["attn_out_norm"], sm["sgu_out_norm"], w_out_f,
                                          sm["ln_post_mix"], sm["ln_pre_ffn"])
    g, u, act = _gate_up(f, w_gu_t)
    y, h2 = _down_proj(act, w_down_f, h1, sm["ln_post_ffn"])
    dh2, dy, g_peg, g_pep_t, loss_cols, db_peg, d_pff = _pe_loss_and_bwd(
        h2, ps, target, y, w_peg_f, sm["b_pe_gate"], w_pep_t, sm["ln_post_ffn"])
    loss_term = 0.5 * jnp.sum(loss_cols) * (1.0 / D_MODEL)

    arrived = {}
    g_down = _weight_grad(act, dy, "w_down")
    dg, du, (arrived["w_pe_proj"], arrived["w_pe_gate"]) = _down_bwd(dy, w_down_f, g, u, [_slabs(g_pep_t), _slabs(g_peg)])
    dh1, dmx, d_pf, d_pm, (arrived["w_down"],) = _ffn_in_bwd(dg, du, w_gu_t, h1, dh2, mixed, sm["ln_pre_ffn"],
                                                            sm["ln_post_mix"], [_slabs(g_down)])
    dattn, stats, dsgu, d_ga, d_gs, g_gu_t = _out_bwd(dmx, w_out_f, attn, lse, sgu, sm["attn_out_norm"],
                                                      sm["sgu_out_norm"], dg, f)
    duz, d_ws, d_bst, d_lg, d_lb, g_out, g_gu_t = _sgu_bwd(proj, dsgu, sm["sgu_ln_g"], sm["sgu_ln_b"],
                                                           sm["w_spatial"], b_st, groups, dmx, du, f, g_gu_t)
    gs = dict(sgu_ln_g=d_lg, sgu_ln_b=d_lb, w_spatial=d_ws, b_spatial=d_bst[:, :N_GROUPS].T, attn_out_norm=d_ga,
              sgu_out_norm=d_gs, ln_post_mix=d_pm, ln_pre_ffn=d_pf, ln_post_ffn=d_pff, b_pe_gate=db_peg)
    small_grads = _pack_small_grads(gs, loss_term)
    dqkv, (arrived["w_gate_up"], arrived["w_out"], *arrived_small) = _attn_bwd(
        proj, dattn, stats, slopes, [_slabs(g_gu_t), _slabs(g_out), *small_grads],
        [True, True] + [False] * len(small_grads))
    send_sems, recv_sems, slabs, landing, token = _scatter_begin(_slabs(_grad_w_in([dqkv, duz], a)), "w_in_grad_send")
    grad_x, d_g1 = _in_bwd([dqkv, duz], w_in_t, xs, dh1, sm["ln_pre_mix"] + token[:1, :1])
    slabs, landing = _scatter_end(send_sems, recv_sems, slabs, landing, d_g1, "w_in_grad_arrive")
    me = 4 * lax.axis_index("x") + 2 * lax.axis_index("y") + lax.axis_index("c")
    own = lax.dynamic_slice_in_dim(slabs, me, 1, axis=0)
    arrived["w_in"] = lax.dynamic_update_slice_in_dim(landing, own, me, axis=0)
    (arrived_late,) = _scatter_call([jnp.pad(d_g1, ((0, SUBLANES - 1), (0, 0)))], [False], "ln_pre_mix_grad_exchange")

    res = {}
    for n in ROW_SHARDED:
        res[n] = _reduce_adamw(arrived[n], w[n][0], m[n][0], v[n][0], n)
    for n in ("w_in", "w_gate_up"):
        res[n] = [t.T for t in _reduce_adamw(arrived[n], w[n][0].T, m[n][0].T, v[n][0].T, n)]
    for n in ("w_pe_proj",):
        grad = _reduce(arrived[n], n).T
        res[n] = (grad, *_adamw(grad, w[n][0], m[n][0], v[n][0], n))
    small, loss_row = _small_adamw(arrived_small, arrived_late, w, m, v)

    out = []
    for k in range(4):
        out += [res[n][k][None] if n in res else small[n][k] for n in WEIGHTS]
    return (loss_row[0, 0], grad_x[None], *out)
```
